```python
import math
import jax
import jax.numpy as jnp
from jax import lax
import numpy as np


D_MODEL = 1024
BATCH = 8
SEQ = 4096
DEPTH = 2

GRID_W = 64
CTX_LEN = 256
EPS = 1e-6
ROPE_BASE = 10000.0

BR_WIDTH = D_MODEL // 2
A_DK = 128
A_DV = 128
A_HEADS = BR_WIDTH // A_DK
A_WIDTH = A_HEADS * A_DV
A_CONV = 5
A_CHUNK = 64
B_HD = 64
B_Q_HEADS = BR_WIDTH // B_HD
B_KV_HEADS = B_Q_HEADS // 4
B_WIDTH = B_Q_HEADS * B_HD
WINDOW = 128
B_BLOCK = 128
C_HD = 128
C_HEADS = BR_WIDTH // C_HD
C_WIDTH = C_HEADS * C_HD
C_CHUNK = 64
N_BRANCH = 3

SPLIT_SIZES = (3 * A_WIDTH, A_WIDTH, 2 * A_HEADS, 2 * A_HEADS,
               B_Q_HEADS * B_HD, 2 * B_KV_HEADS * B_HD, B_WIDTH,
               3 * C_WIDTH, C_WIDTH, N_BRANCH * D_MODEL)
SPLIT_POINTS = tuple(int(v) for v in np.cumsum(SPLIT_SIZES)[:-1])
IN_WIDTH = int(sum(SPLIT_SIZES))

kernel_name = 'hybrid_delta_window_retention_dit'


def rms_norm(x, w):
    xf = x.astype(jnp.float32)
    y = xf * lax.rsqrt(jnp.mean(xf * xf, axis=-1, keepdims=True) + EPS)
    return (y * w.astype(jnp.float32)).astype(x.dtype)


def l2_normalize(x):
    xf = x.astype(jnp.float32)
    return xf * lax.rsqrt(jnp.sum(xf * xf, axis=-1, keepdims=True) + EPS)


def to_heads(x, h):
    b, t, _ = x.shape
    return x.reshape(b, t, h, -1).transpose(0, 2, 1, 3)


def from_heads(x):
    b, h, t, d = x.shape
    return x.transpose(0, 2, 1, 3).reshape(b, t, h * d)


def flip_t(a):
    return jnp.flip(a, axis=2)


def rope_angles(pos, n_freq):
    inv = ROPE_BASE ** (-jnp.arange(n_freq, dtype=jnp.float32) / n_freq)
    return pos[:, None] * inv[None, :]


def apply_rot(x, ang):
    x1, x2 = jnp.split(x, 2, axis=-1)
    cos = jnp.cos(ang).astype(x.dtype)
    sin = jnp.sin(ang).astype(x.dtype)
    return jnp.concatenate([x1 * cos - x2 * sin, x1 * sin + x2 * cos], axis=-1)


def apply_axial(x, ang_r, ang_c):
    half = x.shape[-1] // 2
    return jnp.concatenate([apply_rot(x[..., :half], ang_r), apply_rot(x[..., half:], ang_c)], axis=-1)


def short_conv(x, w):
    k = w.shape[0]
    p = k // 2
    return lax.conv_general_dilated(x, w[:, None, :].astype(x.dtype), window_strides=(1,),
                                    padding=[(p, p)], dimension_numbers=('NWC', 'WIO', 'NWC'),
                                    feature_group_count=x.shape[-1])


def gated_delta_chunk(q, k, v, g, beta, s0):
    b, h, t, dk = q.shape
    dv = v.shape[-1]
    c = A_CHUNK
    n = t // c
    q = q.reshape(b, h, n, c, dk)
    k = k.reshape(b, h, n, c, dk)
    v = v.reshape(b, h, n, c, dv)
    g = jnp.cumsum(g.reshape(b, h, n, c), axis=-1)
    beta = beta.reshape(b, h, n, c)
    incl = jnp.tril(jnp.ones((c, c), dtype=bool))
    strict = jnp.tril(jnp.ones((c, c), dtype=bool), -1)
    decay = jnp.exp(jnp.where(incl, g[..., :, None] - g[..., None, :], -jnp.inf))
    kb = k * beta[..., None]
    lmat = jnp.where(strict, jnp.einsum('bhnid,bhnjd->bhnij', kb, k) * decay, 0.0)
    eye = jnp.eye(c, dtype=jnp.float32)
    tinv = lax.linalg.triangular_solve(lmat + eye, jnp.broadcast_to(eye, lmat.shape),
                                       left_side=True, lower=True)
    u = jnp.einsum('bhnij,bhnjd->bhnid', tinv, v * beta[..., None])
    w = jnp.einsum('bhnij,bhnjd->bhnid', tinv, kb * jnp.exp(g)[..., None])
    qk = jnp.einsum('bhnid,bhnjd->bhnij', q, k) * decay
    q_dec = q * jnp.exp(g)[..., None]
    g_last = g[..., -1]
    k_tail = k * jnp.exp(g_last[..., None] - g)[..., None]

    def step(state, inp):
        u_n, w_n, qk_n, qd_n, kt_n, gl_n = inp
        v_new = u_n - jnp.einsum('bhcd,bhde->bhce', w_n, state)
        o_n = jnp.einsum('bhcd,bhde->bhce', qd_n, state) + jnp.einsum('bhij,bhje->bhie', qk_n, v_new)
        state = state * jnp.exp(gl_n)[..., None, None] + jnp.einsum('bhcd,bhce->bhde', kt_n, v_new)
        return state, o_n

    mv = lambda a: jnp.moveaxis(a, 2, 0)
    s_fin, o = lax.scan(step, s0, (mv(u), mv(w), mv(qk), mv(q_dec), mv(k_tail), mv(g_last)))
    return jnp.moveaxis(o, 0, 2).reshape(b, h, t, dv), s_fin


def delta_prep(qkv, b_raw, a_raw, conv_w, a_log, dt_bias):
    qkv = jax.nn.silu(short_conv(qkv, conv_w))
    q, k, v = jnp.split(qkv, 3, axis=-1)
    q = l2_normalize(to_heads(q, A_HEADS)) * (A_DK ** -0.5)
    k = l2_normalize(to_heads(k, A_HEADS))
    v = to_heads(v, A_HEADS).astype(jnp.float32)
    beta = jax.nn.sigmoid(b_raw.astype(jnp.float32)).transpose(0, 2, 1)
    g = -jnp.exp(a_log.astype(jnp.float32))[None, :, None] * jax.nn.softplus(
        (a_raw.astype(jnp.float32) + dt_bias.astype(jnp.float32)).transpose(0, 2, 1))
    return q, k, v, beta, g


def bidir_delta(q, k, v, beta, g, s0_f, s0_b):
    h = A_HEADS
    o_f, s_f = gated_delta_chunk(q, k, v, g[:, :h], beta[:, :h], s0_f)
    o_b, s_b = gated_delta_chunk(flip_t(q), flip_t(k), flip_t(v), flip_t(g[:, h:]), flip_t(beta[:, h:]), s0_b)
    return o_f + flip_t(o_b), s_f, s_b


def gated_head_rmsnorm(o, z, w):
    o = o * lax.rsqrt(jnp.mean(o * o, axis=-1, keepdims=True) + EPS) * w.astype(jnp.float32)
    return (from_heads(o) * jax.nn.silu(z.astype(jnp.float32))).astype(z.dtype)


def window_attention(q, k, v, k_ctx, v_ctx, sink):
    b, t, hq, dh = q.shape
    hkv = k.shape[2]
    grp = hq // hkv
    blk = B_BLOCK
    nb = t // blk
    l = k_ctx.shape[1]
    scale = dh ** -0.5
    qb = q.reshape(b, nb, blk, hkv, grp, dh)
    pad = ((0, 0), (blk, blk), (0, 0), (0, 0))
    kp = jnp.pad(k, pad).reshape(b, nb + 2, blk, hkv, dh)
    vp = jnp.pad(v, pad).reshape(b, nb + 2, blk, hkv, dh)
    kw = jnp.concatenate([kp[:, :-2], kp[:, 1:-1], kp[:, 2:]], axis=2)
    vw = jnp.concatenate([vp[:, :-2], vp[:, 1:-1], vp[:, 2:]], axis=2)
    s_win = jnp.einsum('bnqhgd,bnkhd->bnhgqk', qb, kw).astype(jnp.float32) * scale
    s_ctx = jnp.einsum('bnqhgd,blhd->bnhgql', qb, k_ctx).astype(jnp.float32) * scale
    rel = jnp.arange(3 * blk)[None, :] - blk - jnp.arange(blk)[:, None]
    kpos = jnp.arange(nb)[:, None] * blk - blk + jnp.arange(3 * blk)[None, :]
    valid = (jnp.abs(rel) <= WINDOW)[None] & ((kpos >= 0) & (kpos < t))[:, None, :]
    s_win = jnp.where(valid[None, :, None, None], s_win, -jnp.inf)
    sink_s = jnp.broadcast_to(sink.astype(jnp.float32).reshape(1, 1, hkv, grp, 1, 1), s_win.shape[:-1] + (1,))
    p = jax.nn.softmax(jnp.concatenate([sink_s, s_ctx, s_win], axis=-1), axis=-1)
    p_ctx = p[..., 1:1 + l].astype(v.dtype)
    p_win = p[..., 1 + l:].astype(v.dtype)
    o = (jnp.einsum('bnhgql,blhd->bnqhgd', p_ctx, v_ctx)
         + jnp.einsum('bnhgqk,bnkhd->bnqhgd', p_win, vw))
    return o.reshape(b, t, hq * dh)


def context_attention(q, k, v, sink):
    b, l, hq, dh = q.shape
    hkv = k.shape[2]
    grp = hq // hkv
    qg = q.reshape(b, l, hkv, grp, dh)
    s = jnp.einsum('bqhgd,bkhd->bhgqk', qg, k).astype(jnp.float32) * (dh ** -0.5)
    sink_s = jnp.broadcast_to(sink.astype(jnp.float32).reshape(1, hkv, grp, 1, 1), s.shape[:-1] + (1,))
    p = jax.nn.softmax(jnp.concatenate([sink_s, s], axis=-1), axis=-1)[..., 1:]
    o = jnp.einsum('bhgqk,bkhd->bqhgd', p.astype(v.dtype), v)
    return o.reshape(b, l, hq * dh)


def retention_chunk(q, k, v, log_gamma, s0):
    b, h, t, dk = q.shape
    dv = v.shape[-1]
    c = C_CHUNK
    n = t // c
    q = q.reshape(b, h, n, c, dk)
    k = k.reshape(b, h, n, c, dk)
    v = v.reshape(b, h, n, c, dv)
    idx = jnp.arange(c, dtype=jnp.float32)
    lg = log_gamma.astype(jnp.float32)[:, None]
    incl = jnp.tril(jnp.ones((c, c), dtype=bool))
    dmask = jnp.exp(jnp.where(incl, (idx[:, None] - idx[None, :]) * lg[:, :, None], -jnp.inf))
    o_inner = jnp.einsum('bhnij,bhnje->bhnie',
                         jnp.einsum('bhnid,bhnjd->bhnij', q, k) * dmask[None, :, None], v)
    q_dec = q * jnp.exp((idx + 1.0) * lg)[None, :, None, :, None]
    k_dec = k * jnp.exp((c - 1.0 - idx) * lg)[None, :, None, :, None]
    chunk_dec = jnp.exp(c * lg)[None, :, :, None]

    def step(state, inp):
        qd_n, kd_n, v_n = inp
        o_n = jnp.einsum('bhcd,bhde->bhce', qd_n, state)
        state = state * chunk_dec + jnp.einsum('bhcd,bhce->bhde', kd_n, v_n)
        return state, o_n

    mv = lambda a: jnp.moveaxis(a, 2, 0)
    s_fin, o_cross = lax.scan(step, s0, (mv(q_dec), mv(k_dec), mv(v)))
    o = o_inner + jnp.moveaxis(o_cross, 0, 2)
    return o.reshape(b, h, t, dv), s_fin


def retention_prep(qkv, ang):
    q, k, v = jnp.split(qkv, 3, axis=-1)
    q = to_heads(q, C_HEADS).astype(jnp.float32)
    k = to_heads(k, C_HEADS).astype(jnp.float32) * (C_HD ** -0.5)
    v = to_heads(v, C_HEADS).astype(jnp.float32)
    if ang is not None:
        q = apply_rot(q, ang)
        k = apply_rot(k, ang)
    return q, k, v


def bidir_retention(q, k, v, log_gamma, s0_f, s0_b):
    h = C_HEADS
    o_f, s_f = retention_chunk(q, k, v, log_gamma[:h], s0_f)
    o_b, s_b = retention_chunk(flip_t(q), flip_t(k), flip_t(v), log_gamma[h:], s0_b)
    return o_f + flip_t(o_b), s_f, s_b


def retention_out(o, z, w):
    mu = jnp.mean(o, axis=-1, keepdims=True)
    var = jnp.mean(jnp.square(o - mu), axis=-1, keepdims=True)
    o = from_heads((o - mu) * lax.rsqrt(var + EPS)) * w.astype(jnp.float32)
    return (o * jax.nn.silu(z.astype(jnp.float32))).astype(z.dtype)


def merge_branches(y_a, y_b, y_c, logits, w_br, w_o):
    g_a, g_b, g_c = jnp.split(jax.nn.sigmoid(logits), 3, axis=-1)
    merged = g_a * (y_a @ w_br[0]) + g_b * (y_b @ w_br[1]) + g_c * (y_c @ w_br[2])
    return merged @ w_o


def _fwd_setup_inputs(seed: int = 0) -> dict:
    key = jax.random.key(seed)
    ks = jax.random.split(key, 18)
    f32 = jnp.float32

    def nrm(k, shape, s):
        return jax.random.normal(k, shape, f32) * s

    x = nrm(ks[0], (BATCH, SEQ, D_MODEL), 1.0)
    c = nrm(ks[1], (BATCH, D_MODEL), 1.0)
    ctx = nrm(ks[2], (BATCH, CTX_LEN, D_MODEL), 1.0)
    c_ctx = nrm(ks[3], (D_MODEL,), 1.0)
    w_ada = nrm(ks[4], (DEPTH, D_MODEL, 3 * D_MODEL), 0.5 * D_MODEL ** -0.5)
    b_ada = nrm(ks[5], (DEPTH, 3 * D_MODEL), 0.02)
    norm_w = 1.0 + nrm(ks[6], (DEPTH, D_MODEL), 0.02)
    w_in = nrm(ks[7], (DEPTH, D_MODEL, IN_WIDTH), D_MODEL ** -0.5)
    a_conv_w = nrm(ks[8], (DEPTH, A_CONV, 3 * A_WIDTH), A_CONV ** -0.5)
    a_log = jnp.log(jax.random.uniform(ks[9], (DEPTH, 2 * A_HEADS), f32, 1.0, 16.0))
    dt = jnp.exp(jax.random.uniform(ks[10], (DEPTH, 2 * A_HEADS), f32, math.log(1e-3), math.log(1e-1)))
    a_dt_bias = dt + jnp.log(-jnp.expm1(-dt))
    a_norm_w = 1.0 + nrm(ks[11], (DEPTH, A_DV), 0.02)
    b_sink = nrm(ks[12], (DEPTH, B_Q_HEADS), 0.5)
    gam = 1.0 - 2.0 ** (-5.0 - np.arange(C_HEADS, dtype=np.float32))
    base = jnp.asarray(np.log(gam) - np.log1p(-gam), f32)
    c_decay = jnp.tile(base, 2)[None, :] + nrm(ks[13], (DEPTH, 2 * C_HEADS), 0.1)
    c_norm_w = 1.0 + nrm(ks[14], (DEPTH, C_WIDTH), 0.02)
    w_branch = nrm(ks[15], (DEPTH, N_BRANCH, BR_WIDTH, D_MODEL), BR_WIDTH ** -0.5)
    w_out = nrm(ks[16], (DEPTH, D_MODEL, D_MODEL), D_MODEL ** -0.5)
    final_norm_w = 1.0 + nrm(ks[17], (D_MODEL,), 0.02)
    return {'x': x, 'c': c, 'ctx': ctx, 'c_ctx': c_ctx, 'w_ada': w_ada, 'b_ada': b_ada,
            'norm_w': norm_w, 'w_in': w_in, 'a_conv_w': a_conv_w, 'a_log': a_log,
            'a_dt_bias': a_dt_bias, 'a_norm_w': a_norm_w, 'b_sink': b_sink, 'c_decay': c_decay,
            'c_norm_w': c_norm_w, 'w_branch': w_branch, 'w_out': w_out, 'final_norm_w': final_norm_w}


def _fwd_reference(x, c, ctx, c_ctx, w_ada, b_ada, norm_w, w_in, a_conv_w, a_log, a_dt_bias, a_norm_w,
              b_sink, c_decay, c_norm_w, w_branch, w_out, final_norm_w):
    b, t, d = x.shape
    l = ctx.shape[1]
    rows_n = t // GRID_W
    rows = jnp.repeat(jnp.arange(rows_n, dtype=jnp.float32), GRID_W)
    cols = jnp.tile(jnp.arange(GRID_W, dtype=jnp.float32), rows_n)
    n_ax = B_HD // 4
    ang_r = rope_angles(rows, n_ax)[:, None, :]
    ang_c = rope_angles(cols, n_ax)[:, None, :]
    ang_ret = rope_angles(jnp.arange(t, dtype=jnp.float32), C_HD // 2)
    zeros_a = jnp.zeros((b, A_HEADS, A_DK, A_DV), jnp.float32)
    zeros_c = jnp.zeros((b, C_HEADS, C_HD, C_HD), jnp.float32)

    for layer in range(DEPTH):
        mod = jax.nn.silu(c) @ w_ada[layer] + b_ada[layer]
        shift, scale, gate = [m[:, None, :] for m in jnp.split(mod, 3, axis=-1)]
        mod_c = jax.nn.silu(c_ctx) @ w_ada[layer] + b_ada[layer]
        shift_c, scale_c, gate_c = jnp.split(mod_c, 3, axis=-1)
        h = rms_norm(x, norm_w[layer]) * (1.0 + scale) + shift
        hc = rms_norm(ctx, norm_w[layer]) * (1.0 + scale_c) + shift_c
        (a_qkv, a_z, a_beta, a_alpha, b_q, b_kv, b_z, c_qkv, c_z, merge) = jnp.split(
            h @ w_in[layer], SPLIT_POINTS, axis=-1)
        (a_qkv_c, a_z_c, a_beta_c, a_alpha_c, b_q_c, b_kv_c, b_z_c, c_qkv_c, c_z_c, merge_c) = jnp.split(
            hc @ w_in[layer], SPLIT_POINTS, axis=-1)

        qa, ka, va, beta_a, g_a = delta_prep(a_qkv, a_beta, a_alpha, a_conv_w[layer], a_log[layer], a_dt_bias[layer])
        qa_c, ka_c, va_c, beta_ac, g_ac = delta_prep(a_qkv_c, a_beta_c, a_alpha_c, a_conv_w[layer],
                                                     a_log[layer], a_dt_bias[layer])
        o_ac, s_af, s_ab = bidir_delta(qa_c, ka_c, va_c, beta_ac, g_ac, zeros_a, zeros_a)
        o_a, _, _ = bidir_delta(qa, ka, va, beta_a, g_a, s_af, s_ab)
        y_a = gated_head_rmsnorm(o_a, a_z, a_norm_w[layer]).astype(x.dtype)

        q_b = apply_axial(b_q.reshape(b, t, B_Q_HEADS, B_HD), ang_r, ang_c)
        k_b, v_b = jnp.split(b_kv.reshape(b, t, 2 * B_KV_HEADS, B_HD), 2, axis=2)
        k_b = apply_axial(k_b, ang_r, ang_c)
        q_bc = b_q_c.reshape(b, l, B_Q_HEADS, B_HD)
        k_bc, v_bc = jnp.split(b_kv_c.reshape(b, l, 2 * B_KV_HEADS, B_HD), 2, axis=2)
        y_b = (window_attention(q_b, k_b, v_b, k_bc, v_bc, b_sink[layer]) * jax.nn.silu(b_z)).astype(x.dtype)

        log_gamma = jax.nn.log_sigmoid(c_decay[layer].astype(jnp.float32))
        q_c, k_c, v_c = retention_prep(c_qkv, ang_ret)
        q_cc, k_cc, v_cc = retention_prep(c_qkv_c, None)
        o_cc, s_cf, s_cb = bidir_retention(q_cc, k_cc, v_cc, log_gamma, zeros_c, zeros_c)
        o_c, _, _ = bidir_retention(q_c, k_c, v_c, log_gamma, s_cf, s_cb)
        y_c = retention_out(o_c, c_z, c_norm_w[layer]).astype(x.dtype)

        out = merge_branches(y_a, y_b, y_c, merge, w_branch[layer], w_out[layer])
        x_new = x + gate * out

        if layer < DEPTH - 1:
            y_ac = gated_head_rmsnorm(o_ac, a_z_c, a_norm_w[layer]).astype(ctx.dtype)
            y_bc = (context_attention(q_bc, k_bc, v_bc, b_sink[layer]) * jax.nn.silu(b_z_c)).astype(ctx.dtype)
            y_cc = retention_out(o_cc, c_z_c, c_norm_w[layer]).astype(ctx.dtype)
            out_c = merge_branches(y_ac, y_bc, y_cc, merge_c, w_branch[layer], w_out[layer])
            ctx = ctx + gate_c * out_c
        x = x_new

    return rms_norm(x, final_norm_w)


import jax as _jax
import jax.numpy as _jnp

TWIN_FORMAT = 'train_step'
FWD_PARAMS = ['x', 'c', 'ctx', 'c_ctx', 'w_ada', 'b_ada', 'norm_w', 'w_in', 'a_conv_w', 'a_log', 'a_dt_bias', 'a_norm_w', 'b_sink', 'c_decay', 'c_norm_w', 'w_branch', 'w_out', 'final_norm_w']
TWIN_WEIGHTS = ['c_ctx', 'w_ada', 'b_ada', 'norm_w', 'w_in', 'a_conv_w', 'a_log', 'a_dt_bias', 'a_norm_w', 'b_sink', 'c_decay', 'c_norm_w', 'w_branch', 'w_out', 'final_norm_w']
TWIN_DIFF_INPUT = 'x'
TWIN_INPUTS = ['x', 'c', 'ctx', 'c_ctx', 'w_ada', 'b_ada', 'norm_w', 'w_in', 'a_conv_w', 'a_log', 'a_dt_bias', 'a_norm_w', 'b_sink', 'c_decay', 'c_norm_w', 'w_branch', 'w_out', 'final_norm_w', 'loss_target', 'm_c_ctx', 'm_w_ada', 'm_b_ada', 'm_norm_w', 'm_w_in', 'm_a_conv_w', 'm_a_log', 'm_a_dt_bias', 'm_a_norm_w', 'm_b_sink', 'm_c_decay', 'm_c_norm_w', 'm_w_branch', 'm_w_out', 'm_final_norm_w', 'v_c_ctx', 'v_w_ada', 'v_b_ada', 'v_norm_w', 'v_w_in', 'v_a_conv_w', 'v_a_log', 'v_a_dt_bias', 'v_a_norm_w', 'v_b_sink', 'v_c_decay', 'v_c_norm_w', 'v_w_branch', 'v_w_out', 'v_final_norm_w']
TWIN_OUTPUTS = ['loss', 'grad_x', 'grad_c_ctx', 'grad_w_ada', 'grad_b_ada', 'grad_norm_w', 'grad_w_in', 'grad_a_conv_w', 'grad_a_log', 'grad_a_dt_bias', 'grad_a_norm_w', 'grad_b_sink', 'grad_c_decay', 'grad_c_norm_w', 'grad_w_branch', 'grad_w_out', 'grad_final_norm_w', 'delta_c_ctx', 'delta_w_ada', 'delta_b_ada', 'delta_norm_w', 'delta_w_in', 'delta_a_conv_w', 'delta_a_log', 'delta_a_dt_bias', 'delta_a_norm_w', 'delta_b_sink', 'delta_c_decay', 'delta_c_norm_w', 'delta_w_branch', 'delta_w_out', 'delta_final_norm_w', 'new_m_c_ctx', 'new_m_w_ada', 'new_m_b_ada', 'new_m_norm_w', 'new_m_w_in', 'new_m_a_conv_w', 'new_m_a_log', 'new_m_a_dt_bias', 'new_m_a_norm_w', 'new_m_b_sink', 'new_m_c_decay', 'new_m_c_norm_w', 'new_m_w_branch', 'new_m_w_out', 'new_m_final_norm_w', 'new_v_c_ctx', 'new_v_w_ada', 'new_v_b_ada', 'new_v_norm_w', 'new_v_w_in', 'new_v_a_conv_w', 'new_v_a_log', 'new_v_a_dt_bias', 'new_v_a_norm_w', 'new_v_b_sink', 'new_v_c_decay', 'new_v_c_norm_w', 'new_v_w_branch', 'new_v_w_out', 'new_v_final_norm_w']
TWIN_LEAF_KINDS = {'loss': 'loss', 'grad_x': 'grad_x', 'grad_c_ctx': 'grad_w', 'grad_w_ada': 'grad_w', 'grad_b_ada': 'grad_w', 'grad_norm_w': 'grad_w', 'grad_w_in': 'grad_w', 'grad_a_conv_w': 'grad_w', 'grad_a_log': 'grad_w', 'grad_a_dt_bias': 'grad_w', 'grad_a_norm_w': 'grad_w', 'grad_b_sink': 'grad_w', 'grad_c_decay': 'grad_w', 'grad_c_norm_w': 'grad_w', 'grad_w_branch': 'grad_w', 'grad_w_out': 'grad_w', 'grad_final_norm_w': 'grad_w', 'delta_c_ctx': 'delta_w', 'delta_w_ada': 'delta_w', 'delta_b_ada': 'delta_w', 'delta_norm_w': 'delta_w', 'delta_w_in': 'delta_w', 'delta_a_conv_w': 'delta_w', 'delta_a_log': 'delta_w', 'delta_a_dt_bias': 'delta_w', 'delta_a_norm_w': 'delta_w', 'delta_b_sink': 'delta_w', 'delta_c_decay': 'delta_w', 'delta_c_norm_w': 'delta_w', 'delta_w_branch': 'delta_w', 'delta_w_out': 'delta_w', 'delta_final_norm_w': 'delta_w', 'new_m_c_ctx': 'new_m', 'new_m_w_ada': 'new_m', 'new_m_b_ada': 'new_m', 'new_m_norm_w': 'new_m', 'new_m_w_in': 'new_m', 'new_m_a_conv_w': 'new_m', 'new_m_a_log': 'new_m', 'new_m_a_dt_bias': 'new_m', 'new_m_a_norm_w': 'new_m', 'new_m_b_sink': 'new_m', 'new_m_c_decay': 'new_m', 'new_m_c_norm_w': 'new_m', 'new_m_w_branch': 'new_m', 'new_m_w_out': 'new_m', 'new_m_final_norm_w': 'new_m', 'new_v_c_ctx': 'new_v', 'new_v_w_ada': 'new_v', 'new_v_b_ada': 'new_v', 'new_v_norm_w': 'new_v', 'new_v_w_in': 'new_v', 'new_v_a_conv_w': 'new_v', 'new_v_a_log': 'new_v', 'new_v_a_dt_bias': 'new_v', 'new_v_a_norm_w': 'new_v', 'new_v_b_sink': 'new_v', 'new_v_c_decay': 'new_v', 'new_v_c_norm_w': 'new_v', 'new_v_w_branch': 'new_v', 'new_v_w_out': 'new_v', 'new_v_final_norm_w': 'new_v'}


def _forward(args):
    return _fwd_reference(*[args[k] for k in FWD_PARAMS])


def _output_shape():
    out = _jax.eval_shape(lambda: _forward(_fwd_setup_inputs(0)))
    return out.shape, out.dtype

N_MICROBATCH = 1
ADAM_LR = 0.001
ADAM_B1 = 0.9
ADAM_B2 = 0.999
ADAM_EPS = 1e-08
ADAM_WD = 0.01
ADAM_STEP = 10
PER_EXAMPLE_BATCH_AXIS = {'x': 0, 'c': 0, 'ctx': 0, 'loss_target': 0}
SHARED_INPUTS = []
_WEIGHT_DTYPES = {'c_ctx': _jnp.float32, 'w_ada': _jnp.float32, 'b_ada': _jnp.float32, 'norm_w': _jnp.float32, 'w_in': _jnp.float32, 'a_conv_w': _jnp.float32, 'a_log': _jnp.float32, 'a_dt_bias': _jnp.float32, 'a_norm_w': _jnp.float32, 'b_sink': _jnp.float32, 'c_decay': _jnp.float32, 'c_norm_w': _jnp.float32, 'w_branch': _jnp.float32, 'w_out': _jnp.float32, 'final_norm_w': _jnp.float32}
MOMENT_SCALE = {'c_ctx': 1.667031e-02, 'w_ada': 4.719308e-02, 'b_ada': 7.519230e-02, 'norm_w': 4.821969e-02, 'w_in': 1.882506e-02, 'a_conv_w': 2.101891e-02, 'a_log': 7.034403e-02, 'a_dt_bias': 6.939816e-02, 'a_norm_w': 6.435917e-02, 'b_sink': 9.147058e-05, 'c_decay': 1.394559e-01, 'c_norm_w': 2.929207e-02, 'w_branch': 1.625288e-02, 'w_out': 2.817718e-02, 'final_norm_w': 3.202102e+01}


def _to_microbatches(a, axis):
    t = _jnp.moveaxis(a, axis, 0)
    t = t.reshape((N_MICROBATCH, t.shape[0] // N_MICROBATCH) + t.shape[1:])
    return _jnp.moveaxis(t, 1, axis + 1)


def setup_inputs(seed: int = 0) -> dict:
    inp = _fwd_setup_inputs(seed)
    key = _jax.random.fold_in(_jax.random.key(seed), 7919)
    shape, _ = _output_shape()
    out = dict(inp)
    out["loss_target"] = _jax.random.normal(_jax.random.fold_in(key, 0), shape, _jnp.float32)
    for i, name in enumerate(TWIN_WEIGHTS):
        w = inp[name].astype(_jnp.float32)
        if MOMENT_SCALE is None:
            s = _jnp.sqrt(_jnp.mean(_jnp.square(w)) + 1e-30)
        else:
            s = MOMENT_SCALE[name]
        km, kv = _jax.random.split(_jax.random.fold_in(key, i + 1))
        out[name] = w
        out["m_" + name] = s * _jax.random.normal(km, w.shape, _jnp.float32)
        out["v_" + name] = (s * s) * _jax.random.uniform(kv, w.shape, _jnp.float32, 0.5, 1.5)
    if N_MICROBATCH > 1:
        for name, axis in PER_EXAMPLE_BATCH_AXIS.items():
            out[name] = _to_microbatches(out[name], axis)
    return {'x': out['x'], 'c': out['c'], 'ctx': out['ctx'], 'c_ctx': out['c_ctx'], 'w_ada': out['w_ada'], 'b_ada': out['b_ada'], 'norm_w': out['norm_w'], 'w_in': out['w_in'], 'a_conv_w': out['a_conv_w'], 'a_log': out['a_log'], 'a_dt_bias': out['a_dt_bias'], 'a_norm_w': out['a_norm_w'], 'b_sink': out['b_sink'], 'c_decay': out['c_decay'], 'c_norm_w': out['c_norm_w'], 'w_branch': out['w_branch'], 'w_out': out['w_out'], 'final_norm_w': out['final_norm_w'], 'loss_target': out['loss_target'], 'm_c_ctx': out['m_c_ctx'], 'm_w_ada': out['m_w_ada'], 'm_b_ada': out['m_b_ada'], 'm_norm_w': out['m_norm_w'], 'm_w_in': out['m_w_in'], 'm_a_conv_w': out['m_a_conv_w'], 'm_a_log': out['m_a_log'], 'm_a_dt_bias': out['m_a_dt_bias'], 'm_a_norm_w': out['m_a_norm_w'], 'm_b_sink': out['m_b_sink'], 'm_c_decay': out['m_c_decay'], 'm_c_norm_w': out['m_c_norm_w'], 'm_w_branch': out['m_w_branch'], 'm_w_out': out['m_w_out'], 'm_final_norm_w': out['m_final_norm_w'], 'v_c_ctx': out['v_c_ctx'], 'v_w_ada': out['v_w_ada'], 'v_b_ada': out['v_b_ada'], 'v_norm_w': out['v_norm_w'], 'v_w_in': out['v_w_in'], 'v_a_conv_w': out['v_a_conv_w'], 'v_a_log': out['v_a_log'], 'v_a_dt_bias': out['v_a_dt_bias'], 'v_a_norm_w': out['v_a_norm_w'], 'v_b_sink': out['v_b_sink'], 'v_c_decay': out['v_c_decay'], 'v_c_norm_w': out['v_c_norm_w'], 'v_w_branch': out['v_w_branch'], 'v_w_out': out['v_w_out'], 'v_final_norm_w': out['v_final_norm_w']}


def _loss(weights, diff, rest, loss_target):
    with _jax.named_scope("forward"):
        args = {**rest, TWIN_DIFF_INPUT: diff, **{k: w.astype(_WEIGHT_DTYPES[k]) for k, w in weights.items()}}
        y = _forward(args)
    with _jax.named_scope("loss_head"):
        err = _jnp.square(y.astype(_jnp.float32) - loss_target)
        return 0.5 * _jnp.sum(_jnp.mean(err, axis=-1)) if err.ndim else 0.5 * err


def _adamw(w, g, m, v):
    m = ADAM_B1 * m + (1.0 - ADAM_B1) * g
    v = ADAM_B2 * v + (1.0 - ADAM_B2) * _jnp.square(g)
    m_hat = m / (1.0 - ADAM_B1 ** ADAM_STEP)
    v_hat = v / (1.0 - ADAM_B2 ** ADAM_STEP)
    delta = -ADAM_LR * (m_hat / (_jnp.sqrt(v_hat) + ADAM_EPS) + ADAM_WD * w)
    return delta, m, v


def reference(x, c, ctx, c_ctx, w_ada, b_ada, norm_w, w_in, a_conv_w, a_log, a_dt_bias, a_norm_w, b_sink, c_decay, c_norm_w, w_branch, w_out, final_norm_w, loss_target, m_c_ctx, m_w_ada, m_b_ada, m_norm_w, m_w_in, m_a_conv_w, m_a_log, m_a_dt_bias, m_a_norm_w, m_b_sink, m_c_decay, m_c_norm_w, m_w_branch, m_w_out, m_final_norm_w, v_c_ctx, v_w_ada, v_b_ada, v_norm_w, v_w_in, v_a_conv_w, v_a_log, v_a_dt_bias, v_a_norm_w, v_b_sink, v_c_decay, v_c_norm_w, v_w_branch, v_w_out, v_final_norm_w):
    given = dict(x=x, c=c, ctx=ctx, c_ctx=c_ctx, w_ada=w_ada, b_ada=b_ada, norm_w=norm_w, w_in=w_in, a_conv_w=a_conv_w, a_log=a_log, a_dt_bias=a_dt_bias, a_norm_w=a_norm_w, b_sink=b_sink, c_decay=c_decay, c_norm_w=c_norm_w, w_branch=w_branch, w_out=w_out, final_norm_w=final_norm_w, loss_target=loss_target, m_c_ctx=m_c_ctx, m_w_ada=m_w_ada, m_b_ada=m_b_ada, m_norm_w=m_norm_w, m_w_in=m_w_in, m_a_conv_w=m_a_conv_w, m_a_log=m_a_log, m_a_dt_bias=m_a_dt_bias, m_a_norm_w=m_a_norm_w, m_b_sink=m_b_sink, m_c_decay=m_c_decay, m_c_norm_w=m_c_norm_w, m_w_branch=m_w_branch, m_w_out=m_w_out, m_final_norm_w=m_final_norm_w, v_c_ctx=v_c_ctx, v_w_ada=v_w_ada, v_b_ada=v_b_ada, v_norm_w=v_norm_w, v_w_in=v_w_in, v_a_conv_w=v_a_conv_w, v_a_log=v_a_log, v_a_dt_bias=v_a_dt_bias, v_a_norm_w=v_a_norm_w, v_b_sink=v_b_sink, v_c_decay=v_c_decay, v_c_norm_w=v_c_norm_w, v_w_branch=v_w_branch, v_w_out=v_w_out, v_final_norm_w=v_final_norm_w)
    weights = {n: given[n] for n in TWIN_WEIGHTS}
    shared = {n: given[n] for n in SHARED_INPUTS}
    per_example = {n: given[n] for n in ['x', 'c', 'ctx']}
    grad_fn = _jax.value_and_grad(_loss, argnums=(0, 1))

    def one_microbatch(ex, loss_target):
        ex = dict(ex)
        diff = ex.pop(TWIN_DIFF_INPUT)
        return grad_fn(weights, diff, {**shared, **ex}, loss_target)

    if N_MICROBATCH == 1:
        loss, (grad_w, grad_x) = one_microbatch(per_example, given["loss_target"])
    else:
        def body(carry, xs):
            loss_sum, grad_sum = carry
            l_k, (gw_k, gx_k) = one_microbatch(xs[0], xs[1])
            with _jax.named_scope("update"):
                return (loss_sum + l_k, _jax.tree.map(_jnp.add, grad_sum, gw_k)), gx_k

        init = (_jnp.zeros((), _jnp.float32), _jax.tree.map(_jnp.zeros_like, weights))
        (loss, grad_w), grad_x = _jax.lax.scan(body, init, (per_example, given["loss_target"]))
    with _jax.named_scope("update"):
        delta_w, new_m, new_v = {}, {}, {}
        for n in TWIN_WEIGHTS:
            delta_w[n], new_m[n], new_v[n] = _adamw(weights[n], grad_w[n], given["m_" + n], given["v_" + n])
    return (loss, grad_x, *[grad_w[n] for n in TWIN_WEIGHTS], *[delta_w[n] for n in TWIN_WEIGHTS],
            *[new_m[n] for n in TWIN_WEIGHTS], *[new_v[n] for n in TWIN_WEIGHTS])
```

```python
import functools
import math

import jax
import jax.numpy as jnp
from jax import lax
from jax.experimental import pallas as pl
from jax.experimental.pallas import tpu as pltpu

F32 = jnp.float32
BF16 = jnp.bfloat16
HIGHEST = lax.Precision.HIGHEST

D_MODEL = 1024
SEQ = 4096
DEPTH = 2
GRID_W = 64
CTX_LEN = 256
EPS = 1e-6
ROPE_BASE = 10000.0
BR_WIDTH = D_MODEL // 2
A_DK = 128
A_HEADS = 4
A_WIDTH = 512
A_CONV = 5
B_HD = 64
B_Q_HEADS = 8
B_KV_HEADS = 2
WINDOW = 128
B_BLOCK = 128
C_HD = 128
C_HEADS = 4
C_WIDTH = 512
CHUNK = 64
ADAM_LR = 0.001
ADAM_B1 = 0.9
ADAM_B2 = 0.999
ADAM_EPS = 1e-08
ADAM_WD = 0.01
ADAM_STEP = 10

N_DEV = 8
ROWS = CTX_LEN + SEQ
N_CHUNK = ROWS // CHUNK
N_CTX_CHUNK = CTX_LEN // CHUNK
IN_WIDTH = 8464
IN_PAD = 8704
NEG = -1e30

VMEM_LIMIT = 48 * 1024 * 1024
MESH = pl.DeviceIdType.MESH

C_AQKV, C_AZ, C_BQ, C_BKV, C_BZ, C_CQKV, C_CZ, C_MERGE, C_AB = 0, 1536, 2048, 2560, 2816, 3328, 4864, 5376, 8448


def _cparams(sem=None):
    if sem is None:
        return pltpu.CompilerParams(vmem_limit_bytes=VMEM_LIMIT)
    return pltpu.CompilerParams(dimension_semantics=sem, vmem_limit_bytes=VMEM_LIMIT)


def _dg(a, b, ca, cb, prec=None):
    return lax.dot_general(a, b, (((ca,), (cb,)), ((), ())), preferred_element_type=F32, precision=prec)


@functools.partial(jax.custom_vjp, nondiff_argnums=(2, 3))
def _bdot(a, b, ca, cb):
    return _dg(a.astype(BF16), b.astype(BF16), ca, cb)


def _bdot_fwd(a, b, ca, cb):
    return _bdot(a, b, ca, cb), (a, b)


def _bdot_bwd(ca, cb, res, ct):
    a, b = res
    da = _bdot(ct, b, 1, 1 - cb) if ca == 1 else _bdot(b, ct, 1 - cb, 1)
    db = _bdot(a, ct, 1 - ca, 0) if cb == 0 else _bdot(ct, a, 0, 1 - ca)
    return da, db


_bdot.defvjp(_bdot_fwd, _bdot_bwd)


def _hdot(a, b):
    return _dg(a, b, 1, 0, HIGHEST)


def _pick(dim, prefs):
    for p in prefs:
        if dim % p == 0:
            return p
    return dim


def _matmul(a, b, name):
    a = a.astype(BF16)
    b = b.astype(BF16)
    m, k = a.shape
    _, n = b.shape
    tm = _pick(m, (1088, 1024, 512, 256, 128))
    tn = _pick(n, (512, 256, 128))
    tk = _pick(k, (2176, 2048, 1024, 512, 256, 128))
    nk = k // tk

    def body(a_ref, b_ref, o_ref, acc_ref):
        part = jnp.dot(a_ref[...], b_ref[...], preferred_element_type=F32)
        if nk == 1:
            o_ref[...] = part
        else:
            kk = pl.program_id(2)

            @pl.when(kk == 0)
            def _():
                acc_ref[...] = part

            @pl.when(kk > 0)
            def _():
                acc_ref[...] += part

            @pl.when(kk == nk - 1)
            def _():
                o_ref[...] = acc_ref[...]

    return pl.pallas_call(
        body,
        grid=(m // tm, n // tn, nk),
        in_specs=[pl.BlockSpec((tm, tk), lambda i, j, kk: (i, kk)),
                  pl.BlockSpec((tk, tn), lambda i, j, kk: (kk, j))],
        out_specs=pl.BlockSpec((tm, tn), lambda i, j, kk: (i, j)),
        out_shape=jax.ShapeDtypeStruct((m, n), F32),
        scratch_shapes=[pltpu.VMEM((tm, tn), F32)],
        compiler_params=_cparams(("parallel", "parallel", "arbitrary")),
        name=name,
    )(a, b)


@functools.partial(jax.custom_vjp, nondiff_argnums=(2,))
def _mm(a, b, name):
    return _matmul(a, b, name)


def _mm_fwd(a, b, name):
    a16 = a.astype(BF16)
    b16 = b.astype(BF16)
    return _matmul(a16, b16, name), (a16, b16)


def _mm_bwd(name, res, ct):
    a16, b16 = res
    ct16 = ct.astype(BF16)
    da = _matmul(ct16, b16.T, name + "_da")
    db = _matmul(a16.T, ct16, name + "_db")
    return da, db


_mm.defvjp(_mm_fwd, _mm_bwd)


def _tri_masks(rev):
    ii = lax.broadcasted_iota(jnp.int32, (CHUNK, CHUNK), 0)
    jj = lax.broadcasted_iota(jnp.int32, (CHUNK, CHUNK), 1)
    diff = (ii - jj) * jnp.where(rev, -1, 1)
    incl = diff >= 0
    strict = diff > 0
    eye = jnp.where(ii == jj, 1.0, 0.0).astype(F32)
    return incl, strict, eye


def _tri_inv(l, eye):
    x = eye - l
    p = _hdot(l, l)
    for i in range(5):
        x = x + _hdot(x, p)
        if i < 4:
            p = _hdot(p, p)
    return x


def _delta_chunk(q, k, v, beta, gcc, gcr, gl, s, incl, strict, eye):
    decay = jnp.exp(jnp.where(incl, gcc - gcr, NEG))
    kb = k * beta
    lmat = jnp.where(strict, _bdot(kb, k, 1, 1) * decay, 0.0)
    tinv = _tri_inv(lmat, eye)
    eg = jnp.exp(gcc)
    u = _bdot(tinv, v * beta, 1, 0)
    w = _bdot(tinv, kb * eg, 1, 0)
    qk = _bdot(q, k, 1, 1) * decay
    v_new = u - _bdot(w, s, 1, 0)
    o = _bdot(q * eg, s, 1, 0) + _bdot(qk, v_new, 1, 0)
    s_new = s * jnp.exp(gl) + _bdot(k * jnp.exp(gl - gcc), v_new, 0, 0)
    return o, s_new


def _scan_chunk_index(d, n):
    return jnp.where(d == 0, n, jnp.where(n < N_CTX_CHUNK, N_CTX_CHUNK - 1 - n, N_CHUNK + N_CTX_CHUNK - 1 - n))


def _scan_specs(step_of):
    ci = lambda d, n: _scan_chunk_index(d, step_of(n))
    tok = pl.BlockSpec((CHUNK, 512), lambda d, n: (ci(d, n), 0))
    tok_d = pl.BlockSpec((None, CHUNK, 512), lambda d, n: (d, ci(d, n), 0))
    col = pl.BlockSpec((4, CHUNK, 1), lambda d, n: (d, ci(d, n), 0))
    row = pl.BlockSpec((4, 1, 1, CHUNK), lambda d, n: (d, ci(d, n), 0, 0))
    one = pl.BlockSpec((4, 1, 1, 1), lambda d, n: (d, ci(d, n), 0, 0))
    state = pl.BlockSpec((None, None, 4, 128, 128), lambda d, n: (d, ci(d, n), 0, 0, 0))
    return tok, tok_d, col, row, one, state


def _delta_fwd_call(q, k, v, beta_c, gc_c, gc_r, gl):
    tok, tok_d, col, row, one, state = _scan_specs(lambda n: n)

    def body(q_ref, k_ref, v_ref, b_ref, gcc_ref, gcr_ref, gl_ref, o_ref, ssave_ref, s_scr):
        @pl.when(pl.program_id(1) == 0)
        def _():
            s_scr[...] = jnp.zeros_like(s_scr)

        incl, strict, eye = _tri_masks(pl.program_id(0) == 1)
        for h in range(4):
            hs = slice(h * 128, (h + 1) * 128)
            s = s_scr[h]
            ssave_ref[h] = s
            o, s_new = _delta_chunk(q_ref[:, hs], k_ref[:, hs], v_ref[:, hs], b_ref[h], gcc_ref[h],
                                    gcr_ref[h, 0], gl_ref[h, 0], s, incl, strict, eye)
            o_ref[:, hs] = o
            s_scr[h] = s_new

    return pl.pallas_call(
        body,
        grid=(2, N_CHUNK),
        in_specs=[tok, tok, tok, col, col, row, one],
        out_specs=[tok_d, state],
        out_shape=[jax.ShapeDtypeStruct((2, ROWS, 512), F32),
                   jax.ShapeDtypeStruct((2, N_CHUNK, 4, 128, 128), F32)],
        scratch_shapes=[pltpu.VMEM((4, 128, 128), F32)],
        compiler_params=_cparams(("arbitrary", "arbitrary")),
        name="delta_fwd",
    )(q, k, v, beta_c, gc_c, gc_r, gl)


def _delta_bwd_call(q, k, v, beta_c, gc_c, gc_r, gl, ssave, do):
    tok, tok_d, col, row, one, state = _scan_specs(lambda n: N_CHUNK - 1 - n)

    def body(q_ref, k_ref, v_ref, b_ref, gcc_ref, gcr_ref, gl_ref, ssave_ref, do_ref,
             dq_ref, dk_ref, dv_ref, db_ref, dgcc_ref, dgcr_ref, dgl_ref, ds_scr):
        @pl.when(pl.program_id(1) == 0)
        def _():
            ds_scr[...] = jnp.zeros_like(ds_scr)

        incl, strict, eye = _tri_masks(pl.program_id(0) == 1)
        for h in range(4):
            hs = slice(h * 128, (h + 1) * 128)
            fn = functools.partial(_delta_chunk, incl=incl, strict=strict, eye=eye)
            _, vjp = jax.vjp(fn, q_ref[:, hs], k_ref[:, hs], v_ref[:, hs], b_ref[h], gcc_ref[h],
                             gcr_ref[h, 0], gl_ref[h, 0], ssave_ref[h])
            dq, dk, dv, db, dgcc, dgcr, dgl, ds = vjp((do_ref[:, hs], ds_scr[h]))
            dq_ref[:, hs] = dq
            dk_ref[:, hs] = dk
            dv_ref[:, hs] = dv
            db_ref[h] = db
            dgcc_ref[h] = dgcc
            dgcr_ref[h, 0] = dgcr
            dgl_ref[h, 0] = dgl
            ds_scr[h] = ds

    return pl.pallas_call(
        body,
        grid=(2, N_CHUNK),
        in_specs=[tok, tok, tok, col, col, row, one, state, tok_d],
        out_specs=[tok_d, tok_d, tok_d, col, col, row, one],
        out_shape=[jax.ShapeDtypeStruct((2, ROWS, 512), F32)] * 3
        + [jax.ShapeDtypeStruct((8, ROWS, 1), F32)] * 2
        + [jax.ShapeDtypeStruct((8, N_CHUNK, 1, CHUNK), F32), jax.ShapeDtypeStruct((8, N_CHUNK, 1, 1), F32)],
        scratch_shapes=[pltpu.VMEM((4, 128, 128), F32)],
        compiler_params=_cparams(("arbitrary", "arbitrary")),
        name="delta_bwd",
    )(q, k, v, beta_c, gc_c, gc_r, gl, ssave, do)


@jax.custom_vjp
def _delta_scan(q, k, v, beta_c, gc_c, gc_r, gl):
    return _delta_fwd_call(q, k, v, beta_c, gc_c, gc_r, gl)[0]


def _delta_scan_fwd(q, k, v, beta_c, gc_c, gc_r, gl):
    o, ssave = _delta_fwd_call(q, k, v, beta_c, gc_c, gc_r, gl)
    return o, (q, k, v, beta_c, gc_c, gc_r, gl, ssave)


def _delta_scan_bwd(res, do):
    dq, dk, dv, db, dgcc, dgcr, dgl = _delta_bwd_call(*res, do)
    return dq[0] + dq[1], dk[0] + dk[1], dv[0] + dv[1], db, dgcc, dgcr, dgl


_delta_scan.defvjp(_delta_scan_fwd, _delta_scan_bwd)


def _ret_chunk(q, k, v, dm, qs, ks, cd, s):
    o = _bdot(_bdot(q, k, 1, 1) * dm, v, 1, 0) + _bdot(q * qs, s, 1, 0)
    s_new = s * cd + _bdot(k * ks, v, 0, 0)
    return o, s_new


def _ret_specs():
    dm = pl.BlockSpec((4, CHUNK, CHUNK), lambda d, n: (d, 0, 0))
    sc = pl.BlockSpec((4, CHUNK, 1), lambda d, n: (d, 0, 0))
    cd = pl.BlockSpec((4, 1, 1), lambda d, n: (d, 0, 0))
    return dm, sc, cd


def _ret_fwd_call(q, k, v, dm, qs, ks, cd):
    tok, tok_d, _, _, _, state = _scan_specs(lambda n: n)
    dm_s, sc_s, cd_s = _ret_specs()

    def body(q_ref, k_ref, v_ref, dm_ref, qs_ref, ks_ref, cd_ref, o_ref, ssave_ref, s_scr):
        @pl.when(pl.program_id(1) == 0)
        def _():
            s_scr[...] = jnp.zeros_like(s_scr)

        for h in range(4):
            hs = slice(h * 128, (h + 1) * 128)
            s = s_scr[h]
            ssave_ref[h] = s
            o, s_new = _ret_chunk(q_ref[:, hs], k_ref[:, hs], v_ref[:, hs], dm_ref[h], qs_ref[h], ks_ref[h],
                                  cd_ref[h], s)
            o_ref[:, hs] = o
            s_scr[h] = s_new

    return pl.pallas_call(
        body,
        grid=(2, N_CHUNK),
        in_specs=[tok, tok, tok, dm_s, sc_s, sc_s, cd_s],
        out_specs=[tok_d, state],
        out_shape=[jax.ShapeDtypeStruct((2, ROWS, 512), F32),
                   jax.ShapeDtypeStruct((2, N_CHUNK, 4, 128, 128), F32)],
        scratch_shapes=[pltpu.VMEM((4, 128, 128), F32)],
        compiler_params=_cparams(("arbitrary", "arbitrary")),
        name="ret_fwd",
    )(q, k, v, dm, qs, ks, cd)


def _ret_bwd_call(q, k, v, dm, qs, ks, cd, ssave, do):
    tok, tok_d, _, _, _, state = _scan_specs(lambda n: N_CHUNK - 1 - n)
    dm_s, sc_s, cd_s = _ret_specs()

    def body(q_ref, k_ref, v_ref, dm_ref, qs_ref, ks_ref, cd_ref, ssave_ref, do_ref,
             dq_ref, dk_ref, dv_ref, ddm_ref, dqs_ref, dks_ref, dcd_ref, ds_scr):
        @pl.when(pl.program_id(1) == 0)
        def _():
            ds_scr[...] = jnp.zeros_like(ds_scr)
            ddm_ref[...] = jnp.zeros_like(ddm_ref)
            dqs_ref[...] = jnp.zeros_like(dqs_ref)
            dks_ref[...] = jnp.zeros_like(dks_ref)
            dcd_ref[...] = jnp.zeros_like(dcd_ref)

        for h in range(4):
            hs = slice(h * 128, (h + 1) * 128)
            _, vjp = jax.vjp(_ret_chunk, q_ref[:, hs], k_ref[:, hs], v_ref[:, hs], dm_ref[h], qs_ref[h],
                             ks_ref[h], cd_ref[h], ssave_ref[h])
            dq, dk, dv, ddm, dqs, dks, dcd, ds = vjp((do_ref[:, hs], ds_scr[h]))
            dq_ref[:, hs] = dq
            dk_ref[:, hs] = dk
            dv_ref[:, hs] = dv
            ddm_ref[h] += ddm
            dqs_ref[h] += dqs
            dks_ref[h] += dks
            dcd_ref[h] += dcd
            ds_scr[h] = ds

    return pl.pallas_call(
        body,
        grid=(2, N_CHUNK),
        in_specs=[tok, tok, tok, dm_s, sc_s, sc_s, cd_s, state, tok_d],
        out_specs=[tok_d, tok_d, tok_d, dm_s, sc_s, sc_s, cd_s],
        out_shape=[jax.ShapeDtypeStruct((2, ROWS, 512), F32)] * 3
        + [jax.ShapeDtypeStruct((8, CHUNK, CHUNK), F32), jax.ShapeDtypeStruct((8, CHUNK, 1), F32),
           jax.ShapeDtypeStruct((8, CHUNK, 1), F32), jax.ShapeDtypeStruct((8, 1, 1), F32)],
        scratch_shapes=[pltpu.VMEM((4, 128, 128), F32)],
        compiler_params=_cparams(("arbitrary", "arbitrary")),
        name="ret_bwd",
    )(q, k, v, dm, qs, ks, cd, ssave, do)


@jax.custom_vjp
def _ret_scan(q, k, v, dm, qs, ks, cd):
    return _ret_fwd_call(q, k, v, dm, qs, ks, cd)[0]


def _ret_scan_fwd(q, k, v, dm, qs, ks, cd):
    o, ssave = _ret_fwd_call(q, k, v, dm, qs, ks, cd)
    return o, (q, k, v, dm, qs, ks, cd, ssave)


def _ret_scan_bwd(res, do):
    dq, dk, dv, ddm, dqs, dks, dcd = _ret_bwd_call(*res, do)
    return dq[0] + dq[1], dk[0] + dk[1], dv[0] + dv[1], ddm, dqs, dks, dcd


_ret_scan.defvjp(_ret_scan_fwd, _ret_scan_bwd)


def _attn_head(q, kc, vc, kw, vw, sink, valid):
    scale = B_HD ** -0.5
    s_c = _bdot(q, kc, 1, 1) * scale
    m = jnp.maximum(jnp.max(s_c, axis=-1, keepdims=True), sink)
    if kw is not None:
        s_w = jnp.where(valid, _bdot(q, kw, 1, 1) * scale, NEG)
        m = jnp.maximum(m, jnp.max(s_w, axis=-1, keepdims=True))
    m = lax.stop_gradient(m)
    e_c = jnp.exp(s_c - m)
    den = jnp.sum(e_c, axis=-1, keepdims=True) + jnp.exp(sink - m)
    num = _bdot(e_c, vc, 1, 0)
    if kw is not None:
        e_w = jnp.exp(s_w - m)
        den = den + jnp.sum(e_w, axis=-1, keepdims=True)
        num = num + _bdot(e_w, vw, 1, 0)
    return num / den


def _window_valid(blk, t):
    qi = lax.broadcasted_iota(jnp.int32, (B_BLOCK, 3 * B_BLOCK), 0)
    kj = lax.broadcasted_iota(jnp.int32, (B_BLOCK, 3 * B_BLOCK), 1)
    rel = kj - B_BLOCK - qi
    kpos = blk * B_BLOCK - B_BLOCK + kj
    return (jnp.abs(rel) <= WINDOW) & (kpos >= 0) & (kpos < t)


def _attn_fwd_call(q, kc, vc, sink, kp, vp, name):
    t = q.shape[0]
    window = kp is not None

    def body(*refs):
        if window:
            q_ref, kc_ref, vc_ref, sink_ref, kp_ref, vp_ref, o_ref = refs
        else:
            q_ref, kc_ref, vc_ref, sink_ref, o_ref = refs
        blk = pl.program_id(0)
        valid = _window_valid(blk, t) if window else None
        start = pl.multiple_of(blk * B_BLOCK, B_BLOCK)
        for hk in range(B_KV_HEADS):
            ks = slice(hk * B_HD, (hk + 1) * B_HD)
            kw = kp_ref[pl.ds(start, 3 * B_BLOCK), ks] if window else None
            vw = vp_ref[pl.ds(start, 3 * B_BLOCK), ks] if window else None
            for g in range(4):
                h = hk * 4 + g
                hs = slice(h * B_HD, (h + 1) * B_HD)
                o_ref[:, hs] = _attn_head(q_ref[:, hs], kc_ref[:, ks], vc_ref[:, ks], kw, vw,
                                          jnp.full((1, 1), sink_ref[h], F32), valid)

    qspec = pl.BlockSpec((B_BLOCK, 512), lambda i: (i, 0))
    cspec = pl.BlockSpec((CTX_LEN, 128), lambda i: (0, 0))
    sspec = pl.BlockSpec(memory_space=pltpu.SMEM)
    pspec = pl.BlockSpec((t + 2 * B_BLOCK, 128), lambda i: (0, 0))
    ins = [q, kc, vc, sink] + ([kp, vp] if window else [])
    return pl.pallas_call(
        body,
        grid=(t // B_BLOCK,),
        in_specs=[qspec, cspec, cspec, sspec] + ([pspec, pspec] if window else []),
        out_specs=qspec,
        out_shape=jax.ShapeDtypeStruct((t, 512), F32),
        compiler_params=_cparams(("arbitrary",)),
        name=name,
    )(*ins)


def _attn_bwd_call(q, kc, vc, sink, kp, vp, do, name):
    t = q.shape[0]
    window = kp is not None

    def body(*refs):
        if window:
            (q_ref, kc_ref, vc_ref, sink_ref, kp_ref, vp_ref, do_ref,
             dq_ref, dkc_ref, dvc_ref, dsink_ref, dkp_ref, dvp_ref) = refs
        else:
            q_ref, kc_ref, vc_ref, sink_ref, do_ref, dq_ref, dkc_ref, dvc_ref, dsink_ref = refs
        blk = pl.program_id(0)

        @pl.when(blk == 0)
        def _():
            dkc_ref[...] = jnp.zeros_like(dkc_ref)
            dvc_ref[...] = jnp.zeros_like(dvc_ref)
            dsink_ref[...] = jnp.zeros_like(dsink_ref)
            if window:
                dkp_ref[...] = jnp.zeros_like(dkp_ref)
                dvp_ref[...] = jnp.zeros_like(dvp_ref)

        valid = _window_valid(blk, t) if window else None
        start = pl.multiple_of(blk * B_BLOCK, B_BLOCK)
        for hk in range(B_KV_HEADS):
            ks = slice(hk * B_HD, (hk + 1) * B_HD)
            kch = kc_ref[:, ks]
            vch = vc_ref[:, ks]
            kw = kp_ref[pl.ds(start, 3 * B_BLOCK), ks] if window else None
            vw = vp_ref[pl.ds(start, 3 * B_BLOCK), ks] if window else None
            dkc = jnp.zeros((CTX_LEN, B_HD), F32)
            dvc = jnp.zeros((CTX_LEN, B_HD), F32)
            dkw = jnp.zeros((3 * B_BLOCK, B_HD), F32)
            dvw = jnp.zeros((3 * B_BLOCK, B_HD), F32)
            for g in range(4):
                h = hk * 4 + g
                hs = slice(h * B_HD, (h + 1) * B_HD)
                sink = jnp.full((1, 1), sink_ref[h], F32)
                if window:
                    fn = functools.partial(_attn_head, valid=valid)
                    _, vjp = jax.vjp(fn, q_ref[:, hs], kch, vch, kw, vw, sink)
                    dq, dkc_h, dvc_h, dkw_h, dvw_h, dsink = vjp(do_ref[:, hs])
                    dkw = dkw + dkw_h
                    dvw = dvw + dvw_h
                else:
                    fn = lambda a, b, c, s: _attn_head(a, b, c, None, None, s, None)
                    _, vjp = jax.vjp(fn, q_ref[:, hs], kch, vch, sink)
                    dq, dkc_h, dvc_h, dsink = vjp(do_ref[:, hs])
                dq_ref[:, hs] = dq
                dkc = dkc + dkc_h
                dvc = dvc + dvc_h
                dsink_ref[h:h + 1, :] += jnp.broadcast_to(dsink, (1, 128))
            dkc_ref[:, ks] += dkc
            dvc_ref[:, ks] += dvc
            if window:
                dkp_ref[pl.ds(start, 3 * B_BLOCK), ks] += dkw
                dvp_ref[pl.ds(start, 3 * B_BLOCK), ks] += dvw

    qspec = pl.BlockSpec((B_BLOCK, 512), lambda i: (i, 0))
    cspec = pl.BlockSpec((CTX_LEN, 128), lambda i: (0, 0))
    sspec = pl.BlockSpec(memory_space=pltpu.SMEM)
    dsspec = pl.BlockSpec((8, 128), lambda i: (0, 0))
    pspec = pl.BlockSpec((t + 2 * B_BLOCK, 128), lambda i: (0, 0))
    ins = [q, kc, vc, sink] + ([kp, vp] if window else []) + [do]
    pshape = jax.ShapeDtypeStruct((t + 2 * B_BLOCK, 128), F32)
    return pl.pallas_call(
        body,
        grid=(t // B_BLOCK,),
        in_specs=[qspec, cspec, cspec, sspec] + ([pspec, pspec] if window else []) + [qspec],
        out_specs=[qspec, cspec, cspec, dsspec] + ([pspec, pspec] if window else []),
        out_shape=[jax.ShapeDtypeStruct((t, 512), F32), jax.ShapeDtypeStruct((CTX_LEN, 128), F32),
                   jax.ShapeDtypeStruct((CTX_LEN, 128), F32), jax.ShapeDtypeStruct((8, 128), F32)]
        + ([pshape, pshape] if window else []),
        compiler_params=_cparams(("arbitrary",)),
        name=name,
    )(*ins)


@jax.custom_vjp
def _win_attn(q, kc, vc, sink, kp, vp):
    return _attn_fwd_call(q, kc, vc, sink, kp, vp, "win_attn_fwd")


def _win_attn_fwd(q, kc, vc, sink, kp, vp):
    return _attn_fwd_call(q, kc, vc, sink, kp, vp, "win_attn_fwd"), (q, kc, vc, sink, kp, vp)


def _win_attn_bwd(res, do):
    dq, dkc, dvc, dsink, dkp, dvp = _attn_bwd_call(*res, do, "win_attn_bwd")
    return dq, dkc, dvc, dsink[:, 0], dkp, dvp


_win_attn.defvjp(_win_attn_fwd, _win_attn_bwd)


@jax.custom_vjp
def _ctx_attn(q, kc, vc, sink):
    return _attn_fwd_call(q, kc, vc, sink, None, None, "ctx_attn_fwd")


def _ctx_attn_fwd(q, kc, vc, sink):
    return _attn_fwd_call(q, kc, vc, sink, None, None, "ctx_attn_fwd"), (q, kc, vc, sink)


def _ctx_attn_bwd(res, do):
    q, kc, vc, sink = res
    dq, dkc, dvc, dsink = _attn_bwd_call(q, kc, vc, sink, None, None, do, "ctx_attn_bwd")
    return dq, dkc, dvc, dsink[:, 0]


_ctx_attn.defvjp(_ctx_attn_fwd, _ctx_attn_bwd)


def _my_id():
    return 4 * lax.axis_index("x") + 2 * lax.axis_index("y") + lax.axis_index("c")


def _peer(k):
    x, y, c = lax.axis_index("x"), lax.axis_index("y"), lax.axis_index("c")
    return (1 - x if k & 4 else x, 1 - y if k & 2 else y, 1 - c if k & 1 else c)


def _exchange(arrays, gather, name):
    n = len(arrays)

    def body(*refs):
        ins, outs = refs[:n], refs[n:2 * n]
        send_sems, recv_sems, local_sems = refs[2 * n:]
        me = _my_id()
        own, sent = [], []
        for a in range(n):
            cp = pltpu.make_async_copy(ins[a] if gather else ins[a].at[me], outs[a].at[me], local_sems.at[a])
            cp.start()
            own.append(cp)
            for k in range(1, N_DEV):
                src = ins[a] if gather else ins[a].at[jnp.bitwise_xor(me, k)]
                cp = pltpu.make_async_remote_copy(src_ref=src, dst_ref=outs[a].at[me],
                                                  send_sem=send_sems.at[a, k - 1], recv_sem=recv_sems.at[a, k - 1],
                                                  device_id=_peer(k), device_id_type=MESH)
                cp.start()
                sent.append(cp)
        for a in range(n):
            for k in range(1, N_DEV):
                src = ins[a] if gather else ins[a].at[jnp.bitwise_xor(me, k)]
                arrive = pltpu.make_async_remote_copy(src_ref=src, dst_ref=outs[a].at[jnp.bitwise_xor(me, k)],
                                                      send_sem=send_sems.at[a, k - 1],
                                                      recv_sem=recv_sems.at[a, k - 1],
                                                      device_id=_peer(k), device_id_type=MESH)
                arrive.wait_recv()
        for cp in sent:
            cp.wait_send()
        for cp in own:
            cp.wait()

    hbm = pl.BlockSpec(memory_space=pltpu.HBM)
    out_shape = [jax.ShapeDtypeStruct((N_DEV,) + (a.shape if gather else a.shape[1:]), a.dtype) for a in arrays]
    return pl.pallas_call(
        body,
        in_specs=[hbm] * n,
        out_specs=[hbm] * n,
        out_shape=out_shape,
        scratch_shapes=[pltpu.SemaphoreType.DMA((n, N_DEV - 1)), pltpu.SemaphoreType.DMA((n, N_DEV - 1)),
                        pltpu.SemaphoreType.DMA((n,))],
        compiler_params=pltpu.CompilerParams(has_side_effects=True),
        name=name,
    )(*arrays)


def _adamw(w, m, v, contrib, name):
    r, c = w.shape
    br = _pick(r, (256, 128, 64, 32, 16, 8))
    bc1 = 1.0 - ADAM_B1 ** ADAM_STEP
    bc2 = 1.0 - ADAM_B2 ** ADAM_STEP

    def body(w_ref, m_ref, v_ref, c_ref, g_ref, d_ref, nm_ref, nv_ref):
        g = c_ref[0]
        for j in range(1, N_DEV):
            g = g + c_ref[j]
        m_new = ADAM_B1 * m_ref[...] + (1.0 - ADAM_B1) * g
        v_new = ADAM_B2 * v_ref[...] + (1.0 - ADAM_B2) * (g * g)
        m_hat = m_new / bc1
        v_hat = v_new / bc2
        g_ref[...] = g
        d_ref[...] = -ADAM_LR * (m_hat / (jnp.sqrt(v_hat) + ADAM_EPS) + ADAM_WD * w_ref[...])
        nm_ref[...] = m_new
        nv_ref[...] = v_new

    spec = pl.BlockSpec((br, c), lambda i: (i, 0))
    cspec = pl.BlockSpec((N_DEV, br, c), lambda i: (0, i, 0))
    return pl.pallas_call(
        body,
        grid=(r // br,),
        in_specs=[spec, spec, spec, cspec],
        out_specs=[spec] * 4,
        out_shape=[jax.ShapeDtypeStruct((r, c), F32)] * 4,
        compiler_params=_cparams(("parallel",)),
        name=name,
    )(w, m, v, contrib)


def _silu(x):
    return x * jax.nn.sigmoid(x)


def _rope_angles(pos, n_freq):
    inv = ROPE_BASE ** (-jnp.arange(n_freq, dtype=F32) / n_freq)
    return pos[:, None] * inv[None, :]


def _apply_rot(x, ang):
    x1, x2 = jnp.split(x, 2, axis=-1)
    cos, sin = jnp.cos(ang), jnp.sin(ang)
    return jnp.concatenate([x1 * cos - x2 * sin, x1 * sin + x2 * cos], axis=-1)


def _apply_axial(x, ang_r, ang_c):
    half = x.shape[-1] // 2
    return jnp.concatenate([_apply_rot(x[..., :half], ang_r), _apply_rot(x[..., half:], ang_c)], axis=-1)


def _conv_silu(xs, w):
    n = xs.shape[0]
    p = A_CONV // 2
    xp = jnp.pad(xs, ((p, p), (0, 0)))
    y = xp[0:n] * w[0]
    for j in range(1, A_CONV):
        y = y + xp[j:j + n] * w[j]
    return _silu(y)


def _l2n(x):
    return x * lax.rsqrt(jnp.sum(x * x, axis=-1, keepdims=True) + EPS)


def _chunk_cumsum(g):
    gch = g.reshape(N_CHUNK, CHUNK, 8)
    fwd = jnp.cumsum(gch[..., :4], axis=1)
    bwd = jnp.flip(jnp.cumsum(jnp.flip(gch[..., 4:], axis=1), axis=1), axis=1)
    gc = jnp.concatenate([fwd, bwd], axis=-1)
    gl = jnp.sum(gch, axis=1)
    return gc, gl


def _mixer_a(a_qkv, a_beta, a_alpha, a_z, conv_w, a_log, dt_bias, norm_w):
    qkv = jnp.concatenate([_conv_silu(a_qkv[:CTX_LEN], conv_w), _conv_silu(a_qkv[CTX_LEN:], conv_w)], axis=0)
    q, k, v = jnp.split(qkv, 3, axis=-1)
    q = (_l2n(q.reshape(ROWS, A_HEADS, A_DK)) * (A_DK ** -0.5)).reshape(ROWS, A_WIDTH)
    k = _l2n(k.reshape(ROWS, A_HEADS, A_DK)).reshape(ROWS, A_WIDTH)
    beta = jax.nn.sigmoid(a_beta)
    g = -jnp.exp(a_log)[None, :] * jax.nn.softplus(a_alpha + dt_bias[None, :])
    gc, gl = _chunk_cumsum(g)
    beta_c = beta.T[:, :, None]
    gc_c = gc.reshape(ROWS, 8).T[:, :, None]
    gc_r = gc.transpose(2, 0, 1)[:, :, None, :]
    gl_b = gl.T[:, :, None, None]
    o = _delta_scan(q, k, v, beta_c, gc_c, gc_r, gl_b)
    o = (o[0] + o[1]).reshape(ROWS, A_HEADS, A_DK)
    o = o * lax.rsqrt(jnp.mean(o * o, axis=-1, keepdims=True) + EPS) * norm_w
    return o.reshape(ROWS, A_WIDTH) * _silu(a_z)


def _mixer_c(c_qkv, c_z, c_decay, norm_w, ang_ret):
    q, k, v = jnp.split(c_qkv, 3, axis=-1)
    q = q.reshape(ROWS, C_HEADS, C_HD)
    k = k.reshape(ROWS, C_HEADS, C_HD) * (C_HD ** -0.5)
    ang = ang_ret[:, None, :]
    q = jnp.concatenate([q[:CTX_LEN], _apply_rot(q[CTX_LEN:], ang)], axis=0).reshape(ROWS, C_WIDTH)
    k = jnp.concatenate([k[:CTX_LEN], _apply_rot(k[CTX_LEN:], ang)], axis=0).reshape(ROWS, C_WIDTH)
    lg = jax.nn.log_sigmoid(c_decay)
    idx = jnp.arange(CHUNK, dtype=F32)
    diff = idx[:, None] - idx[None, :]
    lgf, lgb = lg[:4, None, None], lg[4:, None, None]
    dm_f = jnp.exp(jnp.where(diff >= 0, diff * lgf, -jnp.inf))
    dm_b = jnp.exp(jnp.where(diff <= 0, -diff * lgb, -jnp.inf))
    dm = jnp.concatenate([dm_f, dm_b], axis=0)
    qs = jnp.concatenate([jnp.exp((idx + 1.0)[None, :] * lg[:4, None]),
                          jnp.exp((CHUNK - idx)[None, :] * lg[4:, None])], axis=0)[:, :, None]
    ks = jnp.concatenate([jnp.exp((CHUNK - 1.0 - idx)[None, :] * lg[:4, None]),
                          jnp.exp(idx[None, :] * lg[4:, None])], axis=0)[:, :, None]
    cd = jnp.exp(CHUNK * lg)[:, None, None]
    o = _ret_scan(q, k, v, dm, qs, ks, cd)
    o = (o[0] + o[1]).reshape(ROWS, C_HEADS, C_HD)
    mu = jnp.mean(o, axis=-1, keepdims=True)
    var = jnp.mean(jnp.square(o - mu), axis=-1, keepdims=True)
    o = ((o - mu) * lax.rsqrt(var + EPS)).reshape(ROWS, C_WIDTH) * norm_w
    return o * _silu(c_z)


def _mixer_b(b_q, b_kv, b_z, sink, ang_r, ang_c, with_ctx):
    q_x = _apply_axial(b_q[CTX_LEN:].reshape(SEQ, B_Q_HEADS, B_HD), ang_r, ang_c).reshape(SEQ, 512)
    k_x = _apply_axial(b_kv[CTX_LEN:, :128].reshape(SEQ, B_KV_HEADS, B_HD), ang_r, ang_c).reshape(SEQ, 128)
    v_x = b_kv[CTX_LEN:, 128:]
    kc, vc = b_kv[:CTX_LEN, :128], b_kv[:CTX_LEN, 128:]
    pad = ((B_BLOCK, B_BLOCK), (0, 0))
    o_x = _win_attn(q_x, kc, vc, sink, jnp.pad(k_x, pad), jnp.pad(v_x, pad))
    y_x = o_x * _silu(b_z[CTX_LEN:])
    if not with_ctx:
        return y_x
    o_c = _ctx_attn(b_q[:CTX_LEN], kc, vc, sink)
    return jnp.concatenate([o_c * _silu(b_z[:CTX_LEN]), y_x], axis=0)


def _pad_w_in(w):
    return jnp.concatenate([w[:, :2048], w[:, 2064:], w[:, 2048:2064], jnp.zeros((D_MODEL, IN_PAD - IN_WIDTH), w.dtype)],
                           axis=1)


def _local_loss(wts, x, c, ctx, loss_target):
    rows_n = SEQ // GRID_W
    rows = jnp.repeat(jnp.arange(rows_n, dtype=F32), GRID_W)
    cols = jnp.tile(jnp.arange(GRID_W, dtype=F32), rows_n)
    n_ax = B_HD // 4
    ang_r = _rope_angles(rows, n_ax)[:, None, :]
    ang_c = _rope_angles(cols, n_ax)[:, None, :]
    ang_ret = _rope_angles(jnp.arange(SEQ, dtype=F32), C_HD // 2)

    sc16 = jnp.zeros((16, D_MODEL), F32).at[0].set(_silu(c)).at[1].set(_silu(wts["c_ctx"]))
    xs = jnp.concatenate([ctx, x], axis=0)
    for layer in range(DEPTH):
        last = layer == DEPTH - 1
        mod16 = _mm(sc16, wts["w_ada"][layer], "ada%d" % layer) + wts["b_ada"][layer][None, :]
        shift, scale, gate = jnp.split(mod16[0], 3)
        shift_c, scale_c, gate_c = jnp.split(mod16[1], 3)
        is_x = (jnp.arange(ROWS) >= CTX_LEN)[:, None]
        shift_r = jnp.where(is_x, shift[None, :], shift_c[None, :])
        scale_r = jnp.where(is_x, scale[None, :], scale_c[None, :])
        y = xs * lax.rsqrt(jnp.mean(xs * xs, axis=-1, keepdims=True) + EPS) * wts["norm_w"][layer]
        h = y * (1.0 + scale_r) + shift_r
        proj = _mm(h, _pad_w_in(wts["w_in"][layer]), "w_in%d" % layer)
        a_qkv, a_z = proj[:, C_AQKV:C_AZ], proj[:, C_AZ:C_BQ]
        b_q, b_kv, b_z = proj[:, C_BQ:C_BKV], proj[:, C_BKV:C_BZ], proj[:, C_BZ:C_CQKV]
        c_qkv, c_z = proj[:, C_CQKV:C_CZ], proj[:, C_CZ:C_MERGE]
        merge = proj[:, C_MERGE:C_AB]
        a_beta, a_alpha = proj[:, C_AB:C_AB + 8], proj[:, C_AB + 8:C_AB + 16]

        y_a = _mixer_a(a_qkv, a_beta, a_alpha, a_z, wts["a_conv_w"][layer], wts["a_log"][layer],
                       wts["a_dt_bias"][layer], wts["a_norm_w"][layer])
        y_b = _mixer_b(b_q, b_kv, b_z, wts["b_sink"][layer], ang_r, ang_c, with_ctx=not last)
        y_c = _mixer_c(c_qkv, c_z, wts["c_decay"][layer], wts["c_norm_w"][layer], ang_ret)
        if last:
            y_a, y_c, merge, res = y_a[CTX_LEN:], y_c[CTX_LEN:], merge[CTX_LEN:], xs[CTX_LEN:]
            gate_r = gate[None, :]
        else:
            res = xs
            gate_r = jnp.where(is_x, gate[None, :], gate_c[None, :])
        g_a, g_b, g_c = jnp.split(jax.nn.sigmoid(merge), 3, axis=-1)
        wb = wts["w_branch"][layer]
        merged = (g_a * _mm(y_a, wb[0], "br_a%d" % layer) + g_b * _mm(y_b, wb[1], "br_b%d" % layer)
                  + g_c * _mm(y_c, wb[2], "br_c%d" % layer))
        out = _mm(merged, wts["w_out"][layer], "w_out%d" % layer)
        xs = res + gate_r * out
    y = xs * lax.rsqrt(jnp.mean(xs * xs, axis=-1, keepdims=True) + EPS) * wts["final_norm_w"]
    err = jnp.square(y - loss_target)
    return 0.5 * jnp.sum(jnp.mean(err, axis=-1))


SHARDED = ("w_ada", "w_in", "a_conv_w", "w_branch", "w_out")
SMALL = ("c_ctx", "b_ada", "norm_w", "a_log", "a_dt_bias", "a_norm_w", "b_sink", "c_decay", "c_norm_w",
         "final_norm_w")
WEIGHTS = ("c_ctx", "w_ada", "b_ada", "norm_w", "w_in", "a_conv_w", "a_log", "a_dt_bias", "a_norm_w", "b_sink",
           "c_decay", "c_norm_w", "w_branch", "w_out", "final_norm_w")
SMALL_PACK = 12288


def _unshard(name, g):
    if name == "w_branch":
        return g.transpose(1, 2, 3, 0, 4).reshape(DEPTH, 3, BR_WIDTH, D_MODEL)
    if name == "w_out":
        return g.transpose(1, 0, 2, 3).reshape(DEPTH, D_MODEL, D_MODEL)
    s = g.shape
    return g.transpose(1, 2, 0, 3).reshape(s[1], s[2], N_DEV * s[3])


def _reshard(name, w):
    if name == "w_branch":
        return w.reshape(DEPTH, 3, BR_WIDTH, N_DEV, D_MODEL // N_DEV).transpose(3, 0, 1, 2, 4)
    if name == "w_out":
        return w.reshape(DEPTH, N_DEV, D_MODEL // N_DEV, D_MODEL).transpose(1, 0, 2, 3)
    s = w.shape
    return w.reshape(s[0], s[1], N_DEV, s[2] // N_DEV).transpose(2, 0, 1, 3)


def _pack_small(tree):
    flat = jnp.concatenate([tree[n].reshape(-1) for n in SMALL])
    return jnp.pad(flat, (0, SMALL_PACK - flat.shape[0])).reshape(SMALL_PACK // 128, 128)


def _unpack_small(packed, like):
    flat = packed.reshape(-1)
    out, off = {}, 0
    for n in SMALL:
        size = math.prod(like[n].shape)
        out[n] = flat[off:off + size].reshape(like[n].shape)
        off += size
    return out


def kernel(x, c, ctx, c_ctx, w_ada, b_ada, norm_w, w_in, a_conv_w, a_log, a_dt_bias, a_norm_w, b_sink, c_decay, c_norm_w, w_branch, w_out, final_norm_w, loss_target, m_c_ctx, m_w_ada, m_b_ada, m_norm_w, m_w_in, m_a_conv_w, m_a_log, m_a_dt_bias, m_a_norm_w, m_b_sink, m_c_decay, m_c_norm_w, m_w_branch, m_w_out, m_final_norm_w, v_c_ctx, v_w_ada, v_b_ada, v_norm_w, v_w_in, v_a_conv_w, v_a_log, v_a_dt_bias, v_a_norm_w, v_b_sink, v_c_decay, v_c_norm_w, v_w_branch, v_w_out, v_final_norm_w):
    w = dict(c_ctx=c_ctx, w_ada=w_ada, b_ada=b_ada, norm_w=norm_w, w_in=w_in, a_conv_w=a_conv_w, a_log=a_log,
             a_dt_bias=a_dt_bias, a_norm_w=a_norm_w, b_sink=b_sink, c_decay=c_decay, c_norm_w=c_norm_w,
             w_branch=w_branch, w_out=w_out, final_norm_w=final_norm_w)
    m = dict(c_ctx=m_c_ctx, w_ada=m_w_ada, b_ada=m_b_ada, norm_w=m_norm_w, w_in=m_w_in, a_conv_w=m_a_conv_w,
             a_log=m_a_log, a_dt_bias=m_a_dt_bias, a_norm_w=m_a_norm_w, b_sink=m_b_sink, c_decay=m_c_decay,
             c_norm_w=m_c_norm_w, w_branch=m_w_branch, w_out=m_w_out, final_norm_w=m_final_norm_w)
    v = dict(c_ctx=v_c_ctx, w_ada=v_w_ada, b_ada=v_b_ada, norm_w=v_norm_w, w_in=v_w_in, a_conv_w=v_a_conv_w,
             a_log=v_a_log, a_dt_bias=v_a_dt_bias, a_norm_w=v_a_norm_w, b_sink=v_b_sink, c_decay=v_c_decay,
             c_norm_w=v_c_norm_w, w_branch=v_w_branch, w_out=v_w_out, final_norm_w=v_final_norm_w)

    gathered = _exchange([w[n] for n in SHARDED], True, "gather_weights")
    full = dict(w)
    for n, g in zip(SHARDED, gathered):
        full[n] = _unshard(n, g)

    loss, (gw, gx) = jax.value_and_grad(_local_loss, argnums=(0, 1))(full, x[0], c[0], ctx[0], loss_target[0])
    loss = lax.psum(loss, ("x", "y", "c"))

    blocks = _exchange([_reshard(n, gw[n]) for n in SHARDED], False, "scatter_grads")
    small = _exchange([_pack_small(gw)], True, "gather_small_grads")[0]

    grad, delta, new_m, new_v = {}, {}, {}, {}
    for n, contrib in zip(SHARDED, blocks):
        shp = w[n].shape
        two_d = (math.prod(shp[:-1]), shp[-1])
        outs = _adamw(w[n].reshape(two_d), m[n].reshape(two_d), v[n].reshape(two_d),
                      contrib.reshape((N_DEV,) + two_d), "adamw_" + n)
        grad[n], delta[n], new_m[n], new_v[n] = [o.reshape(shp) for o in outs]
    outs = _adamw(_pack_small(w), _pack_small(m), _pack_small(v), small, "adamw_small")
    for tree, packed in zip((grad, delta, new_m, new_v), outs):
        tree.update(_unpack_small(packed, w))

    return (loss, gx[None], *[grad[n] for n in WEIGHTS], *[delta[n] for n in WEIGHTS],
            *[new_m[n] for n in WEIGHTS], *[new_v[n] for n in WEIGHTS])
```

```python
import functools
import math

import jax
import jax.numpy as jnp
from jax import lax
from jax.experimental import pallas as pl
from jax.experimental.pallas import tpu as pltpu

F32 = jnp.float32
BF16 = jnp.bfloat16
INV_PRECISION = lax.Precision.HIGH

D_MODEL = 1024
SEQ = 4096
DEPTH = 2
GRID_W = 64
CTX_LEN = 256
EPS = 1e-6
ROPE_BASE = 10000.0
BR_WIDTH = D_MODEL // 2
A_DK = 128
A_HEADS = 4
A_WIDTH = 512
A_CONV = 5
B_HD = 64
B_Q_HEADS = 8
B_KV_HEADS = 2
WINDOW = 128
B_BLOCK = 128
C_HD = 128
C_HEADS = 4
C_WIDTH = 512
CHUNK = 64
ADAM_LR = 0.001
ADAM_B1 = 0.9
ADAM_B2 = 0.999
ADAM_EPS = 1e-08
ADAM_WD = 0.01
ADAM_STEP = 10

N_DEV = 8
ROWS = CTX_LEN + SEQ
N_CHUNK = ROWS // CHUNK
N_CTX_CHUNK = CTX_LEN // CHUNK
IN_WIDTH = 8464
IN_PAD = 8704
NEG = -1e30

VMEM_LIMIT = 48 * 1024 * 1024
MESH = pl.DeviceIdType.MESH

C_AQKV, C_AZ, C_BQ, C_BKV, C_BZ, C_CQKV, C_CZ, C_MERGE, C_AB = 0, 1536, 2048, 2560, 2816, 3328, 4864, 5376, 8448


def _cparams(sem=None):
    if sem is None:
        return pltpu.CompilerParams(vmem_limit_bytes=VMEM_LIMIT)
    return pltpu.CompilerParams(dimension_semantics=sem, vmem_limit_bytes=VMEM_LIMIT)


def _dg(a, b, ca, cb, prec=None):
    return lax.dot_general(a, b, (((ca,), (cb,)), ((), ())), preferred_element_type=F32, precision=prec)


@functools.partial(jax.custom_vjp, nondiff_argnums=(2, 3))
def _bdot(a, b, ca, cb):
    return _dg(a.astype(BF16), b.astype(BF16), ca, cb)


def _bdot_fwd(a, b, ca, cb):
    return _bdot(a, b, ca, cb), (a, b)


def _bdot_bwd(ca, cb, res, ct):
    a, b = res
    da = _bdot(ct, b, 1, 1 - cb) if ca == 1 else _bdot(b, ct, 1 - cb, 1)
    db = _bdot(a, ct, 1 - ca, 0) if cb == 0 else _bdot(ct, a, 0, 1 - ca)
    return da, db


_bdot.defvjp(_bdot_fwd, _bdot_bwd)


def _hdot(a, b):
    return _dg(a, b, 1, 0, INV_PRECISION)


def _pick(dim, prefs):
    for p in prefs:
        if dim % p == 0:
            return p
    return dim


def _matmul(a, b, name, mode="nn"):
    ca, cb = {"nn": (1, 0), "nt": (1, 1), "tn": (0, 0)}[mode]
    m, k = a.shape[1 - ca], a.shape[ca]
    n = b.shape[1 - cb]
    tm = _pick(m, (1088, 1024, 512, 256, 128))
    tn = _pick(n, (512, 256, 128))
    tk = _pick(k, (1088, 1024, 512, 256, 128) if mode == "tn" else (2176, 2048, 1024, 512, 256, 128))
    nk = k // tk
    a_spec = (pl.BlockSpec((tm, tk), lambda i, j, kk: (i, kk)) if ca == 1
              else pl.BlockSpec((tk, tm), lambda i, j, kk: (kk, i)))
    b_spec = (pl.BlockSpec((tk, tn), lambda i, j, kk: (kk, j)) if cb == 0
              else pl.BlockSpec((tn, tk), lambda i, j, kk: (j, kk)))

    def body(a_ref, b_ref, o_ref, acc_ref):
        part = _dg(a_ref[...].astype(BF16), b_ref[...].astype(BF16), ca, cb)
        if nk == 1:
            o_ref[...] = part
        else:
            kk = pl.program_id(2)

            @pl.when(kk == 0)
            def _():
                acc_ref[...] = part

            @pl.when(kk > 0)
            def _():
                acc_ref[...] += part

            @pl.when(kk == nk - 1)
            def _():
                o_ref[...] = acc_ref[...]

    return pl.pallas_call(
        body,
        grid=(m // tm, n // tn, nk),
        in_specs=[a_spec, b_spec],
        out_specs=pl.BlockSpec((tm, tn), lambda i, j, kk: (i, j)),
        out_shape=jax.ShapeDtypeStruct((m, n), F32),
        scratch_shapes=[pltpu.VMEM((tm, tn), F32)],
        compiler_params=_cparams(("parallel", "parallel", "arbitrary")),
        name=name,
    )(a, b)


@functools.partial(jax.custom_vjp, nondiff_argnums=(2,))
def _mm(a, b, name):
    return _matmul(a, b, name)


def _mm_fwd(a, b, name):
    b16 = b.astype(BF16)
    return _matmul(a, b16, name), (a, b16)


def _mm_bwd(name, res, ct):
    a, b16 = res
    da = _matmul(ct, b16, name + "_da", "nt")
    db = _matmul(a, ct, name + "_db", "tn")
    return da, db


_mm.defvjp(_mm_fwd, _mm_bwd)


ROW_BLOCK = 256


def _pieces(val, pw):
    return [val[:, j * pw:(j + 1) * pw] for j in range(val.shape[1] // pw)]


def _rowwise(fn, name, row_pw, par_pw, out_wpw):
    def load(row_refs, par_refs, with_ctx):
        is_ctx = jnp.logical_and(with_ctx, pl.program_id(0) == 0)
        rows = [_pieces(r[...].astype(F32), pw) for r, pw in zip(row_refs, row_pw)]
        pars = []
        for p, pw in zip(par_refs, par_pw):
            val = p[...] if p.shape[0] == 1 else jnp.where(is_ctx, p[0:1, :], p[1:2, :])
            pars.append(_pieces(val, pw))
        return rows, pars, is_ctx

    def specs(rows, params):
        return ([pl.BlockSpec((ROW_BLOCK, r.shape[1]), lambda i: (i, 0)) for r in rows]
                + [pl.BlockSpec(p.shape, lambda i: (0, 0)) for p in params])

    def fwd_call(rows, params):
        n_rows = rows[0].shape[0]
        with_ctx = n_rows == ROWS

        def body(*refs):
            row_refs, par_refs = refs[:len(rows)], refs[len(rows):len(rows) + len(params)]
            out_refs = refs[len(rows) + len(params):]
            r, p, _ = load(row_refs, par_refs, with_ctx)
            for o_ref, pieces, (_, pw) in zip(out_refs, fn(r, p), out_wpw):
                for j, piece in enumerate(pieces):
                    o_ref[:, j * pw:(j + 1) * pw] = piece

        return pl.pallas_call(
            body,
            grid=(n_rows // ROW_BLOCK,),
            in_specs=specs(rows, params),
            out_specs=[pl.BlockSpec((ROW_BLOCK, w), lambda i: (i, 0)) for w, _ in out_wpw],
            out_shape=[jax.ShapeDtypeStruct((n_rows, w), F32) for w, _ in out_wpw],
            compiler_params=_cparams(("parallel",)),
            name=name + "_fwd",
        )(*rows, *params)

    def bwd_call(rows, params, douts):
        n_rows = rows[0].shape[0]
        with_ctx = n_rows == ROWS
        n_in = len(rows) + len(params)

        def body(*refs):
            row_refs, par_refs = refs[:len(rows)], refs[len(rows):n_in]
            dout_refs = refs[n_in:n_in + len(douts)]
            drow_refs = refs[n_in + len(douts):n_in + len(douts) + len(rows)]
            dpar_refs = refs[n_in + len(douts) + len(rows):]

            @pl.when(pl.program_id(0) == 0)
            def _():
                for d in dpar_refs:
                    d[...] = jnp.zeros_like(d)

            r, p, is_ctx = load(row_refs, par_refs, with_ctx)
            cts = [_pieces(d[...], pw) for d, (_, pw) in zip(dout_refs, out_wpw)]
            _, vjp = jax.vjp(fn, r, p)
            dr, dp = vjp(cts)
            for d_ref, pieces, pw in zip(drow_refs, dr, row_pw):
                for j, piece in enumerate(pieces):
                    d_ref[:, j * pw:(j + 1) * pw] = piece
            for d_ref, pieces, pw in zip(dpar_refs, dp, par_pw):
                for j, piece in enumerate(pieces):
                    cols = slice(j * pw, (j + 1) * pw)
                    if d_ref.shape[0] == 1:
                        d_ref[:, cols] += piece
                    else:
                        d_ref[0:1, cols] += jnp.where(is_ctx, piece, 0.0)
                        d_ref[1:2, cols] += jnp.where(is_ctx, 0.0, piece)

        return pl.pallas_call(
            body,
            grid=(n_rows // ROW_BLOCK,),
            in_specs=specs(rows, params) + [pl.BlockSpec((ROW_BLOCK, w), lambda i: (i, 0)) for w, _ in out_wpw],
            out_specs=specs(rows, params),
            out_shape=[jax.ShapeDtypeStruct(a.shape, F32) for a in list(rows) + list(params)],
            compiler_params=_cparams(("arbitrary",)),
            name=name + "_bwd",
        )(*rows, *params, *douts)

    @jax.custom_vjp
    def call(rows, params):
        return fwd_call(rows, params)

    def call_fwd(rows, params):
        return fwd_call(rows, params), (rows, params)

    def call_bwd(res, douts):
        rows, params = res
        g = bwd_call(rows, params, douts)
        return list(g[:len(rows)]), list(g[len(rows):])

    call.defvjp(call_fwd, call_bwd)
    return call


def _k_silu(x):
    return x / (1.0 + jnp.exp(-x))


def _k_sigmoid(x):
    return 1.0 / (1.0 + jnp.exp(-x))


def _fn_norm_mod(rows, pars):
    (x,), (nw,), (shift,), (scale,) = rows[0], pars[0], pars[1], pars[2]
    y = x * lax.rsqrt(jnp.mean(x * x, axis=-1, keepdims=True) + EPS) * nw
    return [[y * (1.0 + scale) + shift]]


def _fn_head_rms_gate(rows, pars):
    (w,) = pars[0]
    return [[o * lax.rsqrt(jnp.mean(o * o, axis=-1, keepdims=True) + EPS) * w * _k_silu(z)
             for o, z in zip(rows[0], rows[1])]]


def _fn_group_norm_gate(rows, pars):
    out = []
    for o, z, w in zip(rows[0], rows[1], pars[0]):
        mu = jnp.mean(o, axis=-1, keepdims=True)
        var = jnp.mean(jnp.square(o - mu), axis=-1, keepdims=True)
        out.append((o - mu) * lax.rsqrt(var + EPS) * w * _k_silu(z))
    return [out]


def _fn_gate(rows, pars):
    return [[o * _k_silu(z) for o, z in zip(rows[0], rows[1])]]


def _fn_merge(rows, pars):
    (ma,), (mb,), (mc,), (pa,), (pb,), (pc,) = rows
    return [[_k_sigmoid(ma) * pa + _k_sigmoid(mb) * pb + _k_sigmoid(mc) * pc]]


def _fn_residual(rows, pars):
    (res,), (out,), (gate,) = rows[0], rows[1], pars[0]
    return [[res + gate * out]]


def _fn_loss(rows, pars):
    (x,), (target,), (w,) = rows[0], rows[1], pars[0]
    y = x * lax.rsqrt(jnp.mean(x * x, axis=-1, keepdims=True) + EPS) * w
    per_row = 0.5 * jnp.mean(jnp.square(y - target), axis=-1, keepdims=True)
    return [[jnp.broadcast_to(per_row, (ROW_BLOCK, 128))]]


_norm_mod = _rowwise(_fn_norm_mod, "norm_mod", [D_MODEL], [D_MODEL] * 3, [(D_MODEL, D_MODEL)])
_head_rms_gate = _rowwise(_fn_head_rms_gate, "a_out", [128, 128], [128], [(512, 128)])
_group_norm_gate = _rowwise(_fn_group_norm_gate, "c_out", [128, 128], [128], [(512, 128)])
_gate = _rowwise(_fn_gate, "b_out", [512, 512], [], [(512, 512)])
_merge = _rowwise(_fn_merge, "merge", [D_MODEL] * 6, [], [(D_MODEL, D_MODEL)])
_residual = _rowwise(_fn_residual, "residual", [D_MODEL] * 2, [D_MODEL], [(D_MODEL, D_MODEL)])
_loss_rows = _rowwise(_fn_loss, "loss", [D_MODEL] * 2, [D_MODEL], [(128, 128)])


N_CHAIN = 8


def _rev_chunk(s):
    return jnp.where(s < N_CTX_CHUNK, N_CTX_CHUNK - 1 - s, N_CHUNK + N_CTX_CHUNK - 1 - s)


def _scan_specs(step_of):
    cf = step_of
    cr = lambda n: _rev_chunk(step_of(n))

    def pair(shape, index):
        return (pl.BlockSpec(shape, lambda n: index(cf(n))), pl.BlockSpec(shape, lambda n: index(cr(n))))

    return dict(
        tok=pair((CHUNK, 512), lambda c: (c, 0)),
        col=pair((4, CHUNK, 1), lambda c: (0, c, 0)),
        row=pair((4, 1, 1, CHUNK), lambda c: (0, c, 0, 0)),
        one=pair((4, 1, 1, 1), lambda c: (0, c, 0, 0)),
        state=pair((None, 4, 128, 128), lambda c: (c, 0, 0, 0)),
        tinv=pair((None, 4, CHUNK, CHUNK), lambda c: (c, 0, 0, 0)),
    )


def _both(specs, kinds):
    out = []
    for kind in kinds:
        out += list(specs[kind])
    return out


def _chain_masks():
    ii = lax.broadcasted_iota(jnp.int32, (CHUNK, CHUNK), 0)
    jj = lax.broadcasted_iota(jnp.int32, (CHUNK, CHUNK), 1)
    eye = jnp.where(ii == jj, 1.0, 0.0).astype(F32)
    lower = (ii >= jj, ii > jj)
    upper = (ii <= jj, ii < jj)
    return [lower] * 4 + [upper] * 4, eye


def _tri_inv_all(ls, eye):
    xs = [eye - l for l in ls]
    ps = [_hdot(l, l) for l in ls]
    for i in range(5):
        xs = [x + _hdot(x, p) for x, p in zip(xs, ps)]
        if i < 4:
            ps = [_hdot(p, p) for p in ps]
    return xs


@jax.custom_vjp
def _inv_saved(l, x):
    return x


def _inv_saved_fwd(l, x):
    return x, x


def _inv_saved_bwd(x, dx):
    return -_bdot(x, _bdot(dx, x, 1, 1), 0, 0), jnp.zeros_like(x)


_inv_saved.defvjp(_inv_saved_fwd, _inv_saved_bwd)


def _delta_chains(q, k, v, beta, gcc, gcr, gl, s, masks, eye, tinv_saved):
    n = range(len(q))
    decay = [jnp.exp(jnp.where(masks[i][0], gcc[i] - gcr[i], NEG)) for i in n]
    kb = [k[i] * beta[i] for i in n]
    lmat = [jnp.where(masks[i][1], _bdot(kb[i], k[i], 1, 1) * decay[i], 0.0) for i in n]
    if tinv_saved is None:
        tinv = _tri_inv_all(lmat, eye)
    else:
        tinv = [_inv_saved(lmat[i], tinv_saved[i]) for i in n]
    eg = [jnp.exp(gcc[i]) for i in n]
    u = [_bdot(tinv[i], v[i] * beta[i], 1, 0) for i in n]
    w = [_bdot(tinv[i], kb[i] * eg[i], 1, 0) for i in n]
    qk = [_bdot(q[i], k[i], 1, 1) * decay[i] for i in n]
    v_new = [u[i] - _bdot(w[i], s[i], 1, 0) for i in n]
    o = [_bdot(q[i] * eg[i], s[i], 1, 0) + _bdot(qk[i], v_new[i], 1, 0) for i in n]
    s_new = [s[i] * jnp.exp(gl[i]) + _bdot(k[i] * jnp.exp(gl[i] - gcc[i]), v_new[i], 0, 0) for i in n]
    return (o, s_new), tinv


def _chain_loads(tok_pairs, small_pairs):
    toks = [[pair[i // 4][:, (i % 4) * 128:(i % 4 + 1) * 128] for i in range(N_CHAIN)] for pair in tok_pairs]
    smalls = [[pair[i // 4][i % 4] for i in range(N_CHAIN)] for pair in small_pairs]
    return toks, smalls


def _delta_fwd_call(q, k, v, beta, gcc, gcr, gl):
    sp = _scan_specs(lambda n: n)

    def body(qf, qr, kf, kr, vf, vr, bf, br, gccf, gccr, gcrf, gcrr, glf, glr,
             of, orv, ssf, ssr, tsf, tsr, s_scr):
        @pl.when(pl.program_id(0) == 0)
        def _():
            s_scr[...] = jnp.zeros_like(s_scr)

        masks, eye = _chain_masks()
        (qs, ks, vs), (bs, gccs) = _chain_loads([(qf, qr), (kf, kr), (vf, vr)], [(bf, br), (gccf, gccr)])
        gcrs = [(gcrf, gcrr)[i // 4][i % 4, 0] for i in range(N_CHAIN)]
        gls = [(glf, glr)[i // 4][i % 4, 0] for i in range(N_CHAIN)]
        ss = [s_scr[i] for i in range(N_CHAIN)]
        (o, s_new), tinv = _delta_chains(qs, ks, vs, bs, gccs, gcrs, gls, ss, masks, eye, None)
        for i in range(N_CHAIN):
            d, h = i // 4, i % 4
            (ssf, ssr)[d][h] = ss[i]
            (tsf, tsr)[d][h] = tinv[i]
            (of, orv)[d][:, h * 128:(h + 1) * 128] = o[i]
            s_scr[i] = s_new[i]

    return pl.pallas_call(
        body,
        grid=(N_CHUNK,),
        in_specs=_both(sp, ["tok", "tok", "tok", "col", "col", "row", "one"]),
        out_specs=_both(sp, ["tok", "state", "tinv"]),
        out_shape=[jax.ShapeDtypeStruct((ROWS, 512), F32)] * 2
        + [jax.ShapeDtypeStruct((N_CHUNK, 4, 128, 128), F32)] * 2
        + [jax.ShapeDtypeStruct((N_CHUNK, 4, CHUNK, CHUNK), F32)] * 2,
        scratch_shapes=[pltpu.VMEM((N_CHAIN, 128, 128), F32)],
        compiler_params=_cparams(("arbitrary",)),
        name="delta_fwd",
    )(q, q, k, k, v, v, *beta, *gcc, *gcr, *gl)


def _delta_bwd_call(q, k, v, beta, gcc, gcr, gl, ssave, tsave, do):
    sp = _scan_specs(lambda n: N_CHUNK - 1 - n)

    def body(qf, qr, kf, kr, vf, vr, bf, br, gccf, gccr, gcrf, gcrr, glf, glr, ssf, ssr, tsf, tsr, dof, dor,
             dqf, dqr, dkf, dkr, dvf, dvr, dbf, dbr, dgccf, dgccr, dgcrf, dgcrr, dglf, dglr, ds_scr):
        @pl.when(pl.program_id(0) == 0)
        def _():
            ds_scr[...] = jnp.zeros_like(ds_scr)

        masks, eye = _chain_masks()
        (qs, ks, vs, dos), (bs, gccs, ss, ts) = _chain_loads(
            [(qf, qr), (kf, kr), (vf, vr), (dof, dor)], [(bf, br), (gccf, gccr), (ssf, ssr), (tsf, tsr)])
        gcrs = [(gcrf, gcrr)[i // 4][i % 4, 0] for i in range(N_CHAIN)]
        gls = [(glf, glr)[i // 4][i % 4, 0] for i in range(N_CHAIN)]
        fn = lambda *a: _delta_chains(*a, masks, eye, ts)
        _, vjp, _ = jax.vjp(fn, qs, ks, vs, bs, gccs, gcrs, gls, ss, has_aux=True)
        dq, dk, dv, db, dgcc, dgcr, dgl, ds = vjp((dos, [ds_scr[i] for i in range(N_CHAIN)]))
        for i in range(N_CHAIN):
            d, h = i // 4, i % 4
            hs = slice(h * 128, (h + 1) * 128)
            (dqf, dqr)[d][:, hs] = dq[i]
            (dkf, dkr)[d][:, hs] = dk[i]
            (dvf, dvr)[d][:, hs] = dv[i]
            (dbf, dbr)[d][h] = db[i]
            (dgccf, dgccr)[d][h] = dgcc[i]
            (dgcrf, dgcrr)[d][h, 0] = dgcr[i]
            (dglf, dglr)[d][h, 0] = dgl[i]
            ds_scr[i] = ds[i]

    tok = jax.ShapeDtypeStruct((ROWS, 512), F32)
    return pl.pallas_call(
        body,
        grid=(N_CHUNK,),
        in_specs=_both(sp, ["tok", "tok", "tok", "col", "col", "row", "one", "state", "tinv", "tok"]),
        out_specs=_both(sp, ["tok", "tok", "tok", "col", "col", "row", "one"]),
        out_shape=[tok] * 6 + [jax.ShapeDtypeStruct((4, ROWS, 1), F32)] * 4
        + [jax.ShapeDtypeStruct((4, N_CHUNK, 1, CHUNK), F32)] * 2 + [jax.ShapeDtypeStruct((4, N_CHUNK, 1, 1), F32)] * 2,
        scratch_shapes=[pltpu.VMEM((N_CHAIN, 128, 128), F32)],
        compiler_params=_cparams(("arbitrary",)),
        name="delta_bwd",
    )(q, q, k, k, v, v, *beta, *gcc, *gcr, *gl, *ssave, *tsave, do, do)


@jax.custom_vjp
def _delta_scan(q, k, v, beta, gcc, gcr, gl):
    o = _delta_fwd_call(q, k, v, beta, gcc, gcr, gl)
    return o[0] + o[1]


def _delta_scan_fwd(q, k, v, beta, gcc, gcr, gl):
    of, orv, ssf, ssr, tsf, tsr = _delta_fwd_call(q, k, v, beta, gcc, gcr, gl)
    return of + orv, (q, k, v, beta, gcc, gcr, gl, (ssf, ssr), (tsf, tsr))


def _delta_scan_bwd(res, do):
    g = _delta_bwd_call(*res, do)
    return (g[0] + g[1], g[2] + g[3], g[4] + g[5], (g[6], g[7]), (g[8], g[9]), (g[10], g[11]), (g[12], g[13]))


_delta_scan.defvjp(_delta_scan_fwd, _delta_scan_bwd)


def _ret_chains(q, k, v, dm, qs, ks, cd, s):
    n = range(len(q))
    a = [_bdot(q[i], k[i], 1, 1) * dm[i] for i in n]
    o = [_bdot(a[i], v[i], 1, 0) + _bdot(q[i] * qs[i], s[i], 1, 0) for i in n]
    s_new = [s[i] * cd[i] + _bdot(k[i] * ks[i], v[i], 0, 0) for i in n]
    return o, s_new


def _ret_const_specs():
    return [pl.BlockSpec((N_CHAIN, CHUNK, CHUNK), lambda n: (0, 0, 0)), pl.BlockSpec((N_CHAIN, CHUNK, 1), lambda n: (0, 0, 0)),
            pl.BlockSpec((N_CHAIN, CHUNK, 1), lambda n: (0, 0, 0)), pl.BlockSpec((N_CHAIN, 1, 1), lambda n: (0, 0, 0))]


def _ret_fwd_call(q, k, v, dm, qs, ks, cd):
    sp = _scan_specs(lambda n: n)

    def body(qf, qr, kf, kr, vf, vr, dm_ref, qs_ref, ks_ref, cd_ref, of, orv, ssf, ssr, s_scr):
        @pl.when(pl.program_id(0) == 0)
        def _():
            s_scr[...] = jnp.zeros_like(s_scr)

        (qc, kc, vc), _ = _chain_loads([(qf, qr), (kf, kr), (vf, vr)], [])
        ss = [s_scr[i] for i in range(N_CHAIN)]
        consts = [[r[i] for i in range(N_CHAIN)] for r in (dm_ref, qs_ref, ks_ref, cd_ref)]
        o, s_new = _ret_chains(qc, kc, vc, *consts, ss)
        for i in range(N_CHAIN):
            d, h = i // 4, i % 4
            (ssf, ssr)[d][h] = ss[i]
            (of, orv)[d][:, h * 128:(h + 1) * 128] = o[i]
            s_scr[i] = s_new[i]

    return pl.pallas_call(
        body,
        grid=(N_CHUNK,),
        in_specs=_both(sp, ["tok", "tok", "tok"]) + _ret_const_specs(),
        out_specs=_both(sp, ["tok", "state"]),
        out_shape=[jax.ShapeDtypeStruct((ROWS, 512), F32)] * 2 + [jax.ShapeDtypeStruct((N_CHUNK, 4, 128, 128), F32)] * 2,
        scratch_shapes=[pltpu.VMEM((N_CHAIN, 128, 128), F32)],
        compiler_params=_cparams(("arbitrary",)),
        name="ret_fwd",
    )(q, q, k, k, v, v, dm, qs, ks, cd)


def _ret_bwd_call(q, k, v, dm, qs, ks, cd, ssave, do):
    sp = _scan_specs(lambda n: N_CHUNK - 1 - n)

    def body(qf, qr, kf, kr, vf, vr, dm_ref, qs_ref, ks_ref, cd_ref, ssf, ssr, dof, dor,
             dqf, dqr, dkf, dkr, dvf, dvr, ddm_ref, dqs_ref, dks_ref, dcd_ref, ds_scr):
        @pl.when(pl.program_id(0) == 0)
        def _():
            ds_scr[...] = jnp.zeros_like(ds_scr)
            ddm_ref[...] = jnp.zeros_like(ddm_ref)
            dqs_ref[...] = jnp.zeros_like(dqs_ref)
            dks_ref[...] = jnp.zeros_like(dks_ref)
            dcd_ref[...] = jnp.zeros_like(dcd_ref)

        (qc, kc, vc, dos), (ss,) = _chain_loads([(qf, qr), (kf, kr), (vf, vr), (dof, dor)], [(ssf, ssr)])
        consts = [[r[i] for i in range(N_CHAIN)] for r in (dm_ref, qs_ref, ks_ref, cd_ref)]
        _, vjp = jax.vjp(_ret_chains, qc, kc, vc, *consts, ss)
        dq, dk, dv, ddm, dqs, dks, dcd, ds = vjp((dos, [ds_scr[i] for i in range(N_CHAIN)]))
        for i in range(N_CHAIN):
            d, h = i // 4, i % 4
            hs = slice(h * 128, (h + 1) * 128)
            (dqf, dqr)[d][:, hs] = dq[i]
            (dkf, dkr)[d][:, hs] = dk[i]
            (dvf, dvr)[d][:, hs] = dv[i]
            ddm_ref[i] += ddm[i]
            dqs_ref[i] += dqs[i]
            dks_ref[i] += dks[i]
            dcd_ref[i] += dcd[i]
            ds_scr[i] = ds[i]

    tok = jax.ShapeDtypeStruct((ROWS, 512), F32)
    return pl.pallas_call(
        body,
        grid=(N_CHUNK,),
        in_specs=_both(sp, ["tok", "tok", "tok"]) + _ret_const_specs() + _both(sp, ["state", "tok"]),
        out_specs=_both(sp, ["tok", "tok", "tok"]) + _ret_const_specs(),
        out_shape=[tok] * 6 + [jax.ShapeDtypeStruct((N_CHAIN, CHUNK, CHUNK), F32), jax.ShapeDtypeStruct((N_CHAIN, CHUNK, 1), F32),
                               jax.ShapeDtypeStruct((N_CHAIN, CHUNK, 1), F32), jax.ShapeDtypeStruct((N_CHAIN, 1, 1), F32)],
        scratch_shapes=[pltpu.VMEM((N_CHAIN, 128, 128), F32)],
        compiler_params=_cparams(("arbitrary",)),
        name="ret_bwd",
    )(q, q, k, k, v, v, dm, qs, ks, cd, *ssave, do, do)


@jax.custom_vjp
def _ret_scan(q, k, v, dm, qs, ks, cd):
    o = _ret_fwd_call(q, k, v, dm, qs, ks, cd)
    return o[0] + o[1]


def _ret_scan_fwd(q, k, v, dm, qs, ks, cd):
    of, orv, ssf, ssr = _ret_fwd_call(q, k, v, dm, qs, ks, cd)
    return of + orv, (q, k, v, dm, qs, ks, cd, (ssf, ssr))


def _ret_scan_bwd(res, do):
    g = _ret_bwd_call(*res, do)
    return g[0] + g[1], g[2] + g[3], g[4] + g[5], g[6], g[7], g[8], g[9]


_ret_scan.defvjp(_ret_scan_fwd, _ret_scan_bwd)


def _attn_head(q, kc, vc, kw, vw, sink, valid):
    scale = B_HD ** -0.5
    s_c = _bdot(q, kc, 1, 1) * scale
    m = jnp.maximum(jnp.max(s_c, axis=-1, keepdims=True), sink)
    if kw is not None:
        s_w = jnp.where(valid, _bdot(q, kw, 1, 1) * scale, NEG)
        m = jnp.maximum(m, jnp.max(s_w, axis=-1, keepdims=True))
    m = lax.stop_gradient(m)
    e_c = jnp.exp(s_c - m)
    den = jnp.sum(e_c, axis=-1, keepdims=True) + jnp.exp(sink - m)
    num = _bdot(e_c, vc, 1, 0)
    if kw is not None:
        e_w = jnp.exp(s_w - m)
        den = den + jnp.sum(e_w, axis=-1, keepdims=True)
        num = num + _bdot(e_w, vw, 1, 0)
    return num / den


def _window_valid(blk, t):
    qi = lax.broadcasted_iota(jnp.int32, (B_BLOCK, 3 * B_BLOCK), 0)
    kj = lax.broadcasted_iota(jnp.int32, (B_BLOCK, 3 * B_BLOCK), 1)
    rel = kj - B_BLOCK - qi
    kpos = blk * B_BLOCK - B_BLOCK + kj
    return (jnp.abs(rel) <= WINDOW) & (kpos >= 0) & (kpos < t)


def _attn_fwd_call(q, kc, vc, sink, kp, vp, name):
    t = q.shape[0]
    window = kp is not None

    def body(*refs):
        if window:
            q_ref, kc_ref, vc_ref, sink_ref, kp_ref, vp_ref, o_ref = refs
        else:
            q_ref, kc_ref, vc_ref, sink_ref, o_ref = refs
        blk = pl.program_id(0)
        valid = _window_valid(blk, t) if window else None
        start = pl.multiple_of(blk * B_BLOCK, B_BLOCK)
        for hk in range(B_KV_HEADS):
            ks = slice(hk * B_HD, (hk + 1) * B_HD)
            kw = kp_ref[pl.ds(start, 3 * B_BLOCK), ks] if window else None
            vw = vp_ref[pl.ds(start, 3 * B_BLOCK), ks] if window else None
            for g in range(4):
                h = hk * 4 + g
                hs = slice(h * B_HD, (h + 1) * B_HD)
                o_ref[:, hs] = _attn_head(q_ref[:, hs], kc_ref[:, ks], vc_ref[:, ks], kw, vw,
                                          jnp.full((1, 1), sink_ref[h], F32), valid)

    qspec = pl.BlockSpec((B_BLOCK, 512), lambda i: (i, 0))
    cspec = pl.BlockSpec((CTX_LEN, 128), lambda i: (0, 0))
    sspec = pl.BlockSpec(memory_space=pltpu.SMEM)
    pspec = pl.BlockSpec((t + 2 * B_BLOCK, 128), lambda i: (0, 0))
    ins = [q, kc, vc, sink] + ([kp, vp] if window else [])
    return pl.pallas_call(
        body,
        grid=(t // B_BLOCK,),
        in_specs=[qspec, cspec, cspec, sspec] + ([pspec, pspec] if window else []),
        out_specs=qspec,
        out_shape=jax.ShapeDtypeStruct((t, 512), F32),
        compiler_params=_cparams(("arbitrary",)),
        name=name,
    )(*ins)


def _attn_bwd_call(q, kc, vc, sink, kp, vp, do, name):
    t = q.shape[0]
    window = kp is not None

    def body(*refs):
        if window:
            (q_ref, kc_ref, vc_ref, sink_ref, kp_ref, vp_ref, do_ref,
             dq_ref, dkc_ref, dvc_ref, dsink_ref, dkp_ref, dvp_ref) = refs
        else:
            q_ref, kc_ref, vc_ref, sink_ref, do_ref, dq_ref, dkc_ref, dvc_ref, dsink_ref = refs
        blk = pl.program_id(0)

        @pl.when(blk == 0)
        def _():
            dkc_ref[...] = jnp.zeros_like(dkc_ref)
            dvc_ref[...] = jnp.zeros_like(dvc_ref)
            dsink_ref[...] = jnp.zeros_like(dsink_ref)
            if window:
                dkp_ref[...] = jnp.zeros_like(dkp_ref)
                dvp_ref[...] = jnp.zeros_like(dvp_ref)

        valid = _window_valid(blk, t) if window else None
        start = pl.multiple_of(blk * B_BLOCK, B_BLOCK)
        for hk in range(B_KV_HEADS):
            ks = slice(hk * B_HD, (hk + 1) * B_HD)
            kch = kc_ref[:, ks]
            vch = vc_ref[:, ks]
            kw = kp_ref[pl.ds(start, 3 * B_BLOCK), ks] if window else None
            vw = vp_ref[pl.ds(start, 3 * B_BLOCK), ks] if window else None
            dkc = jnp.zeros((CTX_LEN, B_HD), F32)
            dvc = jnp.zeros((CTX_LEN, B_HD), F32)
            dkw = jnp.zeros((3 * B_BLOCK, B_HD), F32)
            dvw = jnp.zeros((3 * B_BLOCK, B_HD), F32)
            for g in range(4):
                h = hk * 4 + g
                hs = slice(h * B_HD, (h + 1) * B_HD)
                sink = jnp.full((1, 1), sink_ref[h], F32)
                if window:
                    fn = functools.partial(_attn_head, valid=valid)
                    _, vjp = jax.vjp(fn, q_ref[:, hs], kch, vch, kw, vw, sink)
                    dq, dkc_h, dvc_h, dkw_h, dvw_h, dsink = vjp(do_ref[:, hs])
                    dkw = dkw + dkw_h
                    dvw = dvw + dvw_h
                else:
                    fn = lambda a, b, c, s: _attn_head(a, b, c, None, None, s, None)
                    _, vjp = jax.vjp(fn, q_ref[:, hs], kch, vch, sink)
                    dq, dkc_h, dvc_h, dsink = vjp(do_ref[:, hs])
                dq_ref[:, hs] = dq
                dkc = dkc + dkc_h
                dvc = dvc + dvc_h
                dsink_ref[h:h + 1, :] += jnp.broadcast_to(dsink, (1, 128))
            dkc_ref[:, ks] += dkc
            dvc_ref[:, ks] += dvc
            if window:
                dkp_ref[pl.ds(start, 3 * B_BLOCK), ks] += dkw
                dvp_ref[pl.ds(start, 3 * B_BLOCK), ks] += dvw

    qspec = pl.BlockSpec((B_BLOCK, 512), lambda i: (i, 0))
    cspec = pl.BlockSpec((CTX_LEN, 128), lambda i: (0, 0))
    sspec = pl.BlockSpec(memory_space=pltpu.SMEM)
    dsspec = pl.BlockSpec((8, 128), lambda i: (0, 0))
    pspec = pl.BlockSpec((t + 2 * B_BLOCK, 128), lambda i: (0, 0))
    ins = [q, kc, vc, sink] + ([kp, vp] if window else []) + [do]
    pshape = jax.ShapeDtypeStruct((t + 2 * B_BLOCK, 128), F32)
    return pl.pallas_call(
        body,
        grid=(t // B_BLOCK,),
        in_specs=[qspec, cspec, cspec, sspec] + ([pspec, pspec] if window else []) + [qspec],
        out_specs=[qspec, cspec, cspec, dsspec] + ([pspec, pspec] if window else []),
        out_shape=[jax.ShapeDtypeStruct((t, 512), F32), jax.ShapeDtypeStruct((CTX_LEN, 128), F32),
                   jax.ShapeDtypeStruct((CTX_LEN, 128), F32), jax.ShapeDtypeStruct((8, 128), F32)]
        + ([pshape, pshape] if window else []),
        compiler_params=_cparams(("arbitrary",)),
        name=name,
    )(*ins)


@jax.custom_vjp
def _win_attn(q, kc, vc, sink, kp, vp):
    return _attn_fwd_call(q, kc, vc, sink, kp, vp, "win_attn_fwd")


def _win_attn_fwd(q, kc, vc, sink, kp, vp):
    return _attn_fwd_call(q, kc, vc, sink, kp, vp, "win_attn_fwd"), (q, kc, vc, sink, kp, vp)


def _win_attn_bwd(res, do):
    dq, dkc, dvc, dsink, dkp, dvp = _attn_bwd_call(*res, do, "win_attn_bwd")
    return dq, dkc, dvc, dsink[:, 0], dkp, dvp


_win_attn.defvjp(_win_attn_fwd, _win_attn_bwd)


@jax.custom_vjp
def _ctx_attn(q, kc, vc, sink):
    return _attn_fwd_call(q, kc, vc, sink, None, None, "ctx_attn_fwd")


def _ctx_attn_fwd(q, kc, vc, sink):
    return _attn_fwd_call(q, kc, vc, sink, None, None, "ctx_attn_fwd"), (q, kc, vc, sink)


def _ctx_attn_bwd(res, do):
    q, kc, vc, sink = res
    dq, dkc, dvc, dsink = _attn_bwd_call(q, kc, vc, sink, None, None, do, "ctx_attn_bwd")
    return dq, dkc, dvc, dsink[:, 0]


_ctx_attn.defvjp(_ctx_attn_fwd, _ctx_attn_bwd)


def _my_id():
    return 4 * lax.axis_index("x") + 2 * lax.axis_index("y") + lax.axis_index("c")


def _peer(k):
    x, y, c = lax.axis_index("x"), lax.axis_index("y"), lax.axis_index("c")
    return (1 - x if k & 4 else x, 1 - y if k & 2 else y, 1 - c if k & 1 else c)


def _exchange(arrays, gather, name):
    n = len(arrays)

    def body(*refs):
        ins, outs = refs[:n], refs[n:2 * n]
        send_sems, recv_sems, local_sems = refs[2 * n:]
        me = _my_id()
        own, sent = [], []
        for a in range(n):
            cp = pltpu.make_async_copy(ins[a] if gather else ins[a].at[me], outs[a].at[me], local_sems.at[a])
            cp.start()
            own.append(cp)
            for k in range(1, N_DEV):
                src = ins[a] if gather else ins[a].at[jnp.bitwise_xor(me, k)]
                cp = pltpu.make_async_remote_copy(src_ref=src, dst_ref=outs[a].at[me],
                                                  send_sem=send_sems.at[a, k - 1], recv_sem=recv_sems.at[a, k - 1],
                                                  device_id=_peer(k), device_id_type=MESH)
                cp.start()
                sent.append(cp)
        for a in range(n):
            for k in range(1, N_DEV):
                src = ins[a] if gather else ins[a].at[jnp.bitwise_xor(me, k)]
                arrive = pltpu.make_async_remote_copy(src_ref=src, dst_ref=outs[a].at[jnp.bitwise_xor(me, k)],
                                                      send_sem=send_sems.at[a, k - 1],
                                                      recv_sem=recv_sems.at[a, k - 1],
                                                      device_id=_peer(k), device_id_type=MESH)
                arrive.wait_recv()
        for cp in sent:
            cp.wait_send()
        for cp in own:
            cp.wait()

    hbm = pl.BlockSpec(memory_space=pltpu.HBM)
    out_shape = [jax.ShapeDtypeStruct((N_DEV,) + (a.shape if gather else a.shape[1:]), a.dtype) for a in arrays]
    return pl.pallas_call(
        body,
        in_specs=[hbm] * n,
        out_specs=[hbm] * n,
        out_shape=out_shape,
        scratch_shapes=[pltpu.SemaphoreType.DMA((n, N_DEV - 1)), pltpu.SemaphoreType.DMA((n, N_DEV - 1)),
                        pltpu.SemaphoreType.DMA((n,))],
        compiler_params=pltpu.CompilerParams(has_side_effects=True),
        name=name,
    )(*arrays)


def _adamw(w, m, v, contrib, name):
    r, c = w.shape
    br = _pick(r, (256, 128, 64, 32, 16, 8))
    bc1 = 1.0 - ADAM_B1 ** ADAM_STEP
    bc2 = 1.0 - ADAM_B2 ** ADAM_STEP

    def body(w_ref, m_ref, v_ref, c_ref, g_ref, d_ref, nm_ref, nv_ref):
        g = c_ref[0].astype(F32)
        for j in range(1, N_DEV):
            g = g + c_ref[j].astype(F32)
        m_new = ADAM_B1 * m_ref[...] + (1.0 - ADAM_B1) * g
        v_new = ADAM_B2 * v_ref[...] + (1.0 - ADAM_B2) * (g * g)
        m_hat = m_new / bc1
        v_hat = v_new / bc2
        g_ref[...] = g
        d_ref[...] = -ADAM_LR * (m_hat / (jnp.sqrt(v_hat) + ADAM_EPS) + ADAM_WD * w_ref[...])
        nm_ref[...] = m_new
        nv_ref[...] = v_new

    spec = pl.BlockSpec((br, c), lambda i: (i, 0))
    cspec = pl.BlockSpec((N_DEV, br, c), lambda i: (0, i, 0))
    return pl.pallas_call(
        body,
        grid=(r // br,),
        in_specs=[spec, spec, spec, cspec],
        out_specs=[spec] * 4,
        out_shape=[jax.ShapeDtypeStruct((r, c), F32)] * 4,
        compiler_params=_cparams(("parallel",)),
        name=name,
    )(w, m, v, contrib)


def _silu(x):
    return x * jax.nn.sigmoid(x)


def _rope_angles(pos, n_freq):
    inv = ROPE_BASE ** (-jnp.arange(n_freq, dtype=F32) / n_freq)
    return pos[:, None] * inv[None, :]


def _apply_rot(x, ang):
    x1, x2 = jnp.split(x, 2, axis=-1)
    cos, sin = jnp.cos(ang), jnp.sin(ang)
    return jnp.concatenate([x1 * cos - x2 * sin, x1 * sin + x2 * cos], axis=-1)


def _apply_axial(x, ang_r, ang_c):
    half = x.shape[-1] // 2
    return jnp.concatenate([_apply_rot(x[..., :half], ang_r), _apply_rot(x[..., half:], ang_c)], axis=-1)


def _conv_silu(xs, w):
    n = xs.shape[0]
    p = A_CONV // 2
    xp = jnp.pad(xs, ((p, p), (0, 0)))
    y = xp[0:n] * w[0]
    for j in range(1, A_CONV):
        y = y + xp[j:j + n] * w[j]
    return _silu(y)


def _l2n(x):
    return x * lax.rsqrt(jnp.sum(x * x, axis=-1, keepdims=True) + EPS)


def _chunk_cumsum(g):
    gch = g.reshape(N_CHUNK, CHUNK, 8)
    fwd = jnp.cumsum(gch[..., :4], axis=1)
    bwd = jnp.flip(jnp.cumsum(jnp.flip(gch[..., 4:], axis=1), axis=1), axis=1)
    gc = jnp.concatenate([fwd, bwd], axis=-1)
    gl = jnp.sum(gch, axis=1)
    return gc, gl


def _halves(a):
    return a[:4], a[4:]


def _mixer_a(a_qkv, a_beta, a_alpha, a_z, conv_w, a_log, dt_bias, norm_w):
    qkv = jnp.concatenate([_conv_silu(a_qkv[:CTX_LEN], conv_w), _conv_silu(a_qkv[CTX_LEN:], conv_w)], axis=0)
    q, k, v = jnp.split(qkv, 3, axis=-1)
    q = (_l2n(q.reshape(ROWS, A_HEADS, A_DK)) * (A_DK ** -0.5)).reshape(ROWS, A_WIDTH)
    k = _l2n(k.reshape(ROWS, A_HEADS, A_DK)).reshape(ROWS, A_WIDTH)
    beta = jax.nn.sigmoid(a_beta)
    g = -jnp.exp(a_log)[None, :] * jax.nn.softplus(a_alpha + dt_bias[None, :])
    gc, gl = _chunk_cumsum(g)
    beta_c = _halves(beta.T[:, :, None])
    gc_c = _halves(gc.reshape(ROWS, 8).T[:, :, None])
    gc_r = _halves(gc.transpose(2, 0, 1)[:, :, None, :])
    gl_b = _halves(gl.T[:, :, None, None])
    o = _delta_scan(q, k, v, beta_c, gc_c, gc_r, gl_b)
    return _head_rms_gate([o, a_z], [norm_w[None, :]])[0]


def _mixer_c(c_qkv, c_z, c_decay, norm_w, ang_ret):
    q, k, v = jnp.split(c_qkv, 3, axis=-1)
    q = q.reshape(ROWS, C_HEADS, C_HD)
    k = k.reshape(ROWS, C_HEADS, C_HD) * (C_HD ** -0.5)
    ang = ang_ret[:, None, :]
    q = jnp.concatenate([q[:CTX_LEN], _apply_rot(q[CTX_LEN:], ang)], axis=0).reshape(ROWS, C_WIDTH)
    k = jnp.concatenate([k[:CTX_LEN], _apply_rot(k[CTX_LEN:], ang)], axis=0).reshape(ROWS, C_WIDTH)
    lg = jax.nn.log_sigmoid(c_decay)
    idx = jnp.arange(CHUNK, dtype=F32)
    diff = idx[:, None] - idx[None, :]
    lgf, lgb = lg[:4, None, None], lg[4:, None, None]
    dm_f = jnp.exp(jnp.where(diff >= 0, diff * lgf, -jnp.inf))
    dm_b = jnp.exp(jnp.where(diff <= 0, -diff * lgb, -jnp.inf))
    dm = jnp.concatenate([dm_f, dm_b], axis=0)
    qs = jnp.concatenate([jnp.exp((idx + 1.0)[None, :] * lg[:4, None]),
                          jnp.exp((CHUNK - idx)[None, :] * lg[4:, None])], axis=0)[:, :, None]
    ks = jnp.concatenate([jnp.exp((CHUNK - 1.0 - idx)[None, :] * lg[:4, None]),
                          jnp.exp(idx[None, :] * lg[4:, None])], axis=0)[:, :, None]
    cd = jnp.exp(CHUNK * lg)[:, None, None]
    o = _ret_scan(q, k, v, dm, qs, ks, cd)
    return _group_norm_gate([o, c_z], [norm_w[None, :]])[0]


def _mixer_b(b_q, b_kv, b_z, sink, ang_r, ang_c, with_ctx):
    q_x = _apply_axial(b_q[CTX_LEN:].reshape(SEQ, B_Q_HEADS, B_HD), ang_r, ang_c).reshape(SEQ, 512)
    k_x = _apply_axial(b_kv[CTX_LEN:, :128].reshape(SEQ, B_KV_HEADS, B_HD), ang_r, ang_c).reshape(SEQ, 128)
    v_x = b_kv[CTX_LEN:, 128:]
    kc, vc = b_kv[:CTX_LEN, :128], b_kv[:CTX_LEN, 128:]
    pad = ((B_BLOCK, B_BLOCK), (0, 0))
    o_x = _win_attn(q_x, kc, vc, sink, jnp.pad(k_x, pad), jnp.pad(v_x, pad))
    if not with_ctx:
        return _gate([o_x, b_z[CTX_LEN:]], [])[0]
    o_c = _ctx_attn(b_q[:CTX_LEN], kc, vc, sink)
    return _gate([jnp.concatenate([o_c, o_x], axis=0), b_z], [])[0]


def _pad_w_in(w):
    return jnp.concatenate([w[:, :2048], w[:, 2064:], w[:, 2048:2064], jnp.zeros((D_MODEL, IN_PAD - IN_WIDTH), w.dtype)],
                           axis=1)


def _local_loss(wts, x, c, ctx, loss_target):
    rows_n = SEQ // GRID_W
    rows = jnp.repeat(jnp.arange(rows_n, dtype=F32), GRID_W)
    cols = jnp.tile(jnp.arange(GRID_W, dtype=F32), rows_n)
    n_ax = B_HD // 4
    ang_r = _rope_angles(rows, n_ax)[:, None, :]
    ang_c = _rope_angles(cols, n_ax)[:, None, :]
    ang_ret = _rope_angles(jnp.arange(SEQ, dtype=F32), C_HD // 2)

    sc16 = jnp.zeros((16, D_MODEL), F32).at[0].set(_silu(c)).at[1].set(_silu(wts["c_ctx"]))
    xs = jnp.concatenate([ctx, x], axis=0)
    for layer in range(DEPTH):
        last = layer == DEPTH - 1
        mod16 = _mm(sc16, wts["w_ada"][layer], "ada%d" % layer) + wts["b_ada"][layer][None, :]
        mod_cx = jnp.stack([mod16[1], mod16[0]])
        shift, scale, gate = jnp.split(mod_cx, 3, axis=1)
        h = _norm_mod([xs], [wts["norm_w"][layer][None, :], shift, scale])[0]
        proj = _mm(h, _pad_w_in(wts["w_in"][layer]), "w_in%d" % layer)
        a_qkv, a_z = proj[:, C_AQKV:C_AZ], proj[:, C_AZ:C_BQ]
        b_q, b_kv, b_z = proj[:, C_BQ:C_BKV], proj[:, C_BKV:C_BZ], proj[:, C_BZ:C_CQKV]
        c_qkv, c_z = proj[:, C_CQKV:C_CZ], proj[:, C_CZ:C_MERGE]
        merge = proj[:, C_MERGE:C_AB]
        a_beta, a_alpha = proj[:, C_AB:C_AB + 8], proj[:, C_AB + 8:C_AB + 16]

        y_a = _mixer_a(a_qkv, a_beta, a_alpha, a_z, wts["a_conv_w"][layer], wts["a_log"][layer],
                       wts["a_dt_bias"][layer], wts["a_norm_w"][layer])
        y_b = _mixer_b(b_q, b_kv, b_z, wts["b_sink"][layer], ang_r, ang_c, with_ctx=not last)
        y_c = _mixer_c(c_qkv, c_z, wts["c_decay"][layer], wts["c_norm_w"][layer], ang_ret)
        if last:
            y_a, y_c, merge, res = y_a[CTX_LEN:], y_c[CTX_LEN:], merge[CTX_LEN:], xs[CTX_LEN:]
            gate = gate[1:2]
        else:
            res = xs
        wb = wts["w_branch"][layer]
        merged = _merge([merge[:, :D_MODEL], merge[:, D_MODEL:2 * D_MODEL], merge[:, 2 * D_MODEL:],
                         _mm(y_a, wb[0], "br_a%d" % layer), _mm(y_b, wb[1], "br_b%d" % layer),
                         _mm(y_c, wb[2], "br_c%d" % layer)], [])[0]
        out = _mm(merged, wts["w_out"][layer], "w_out%d" % layer)
        xs = _residual([res, out], [gate])[0]
    per_row = _loss_rows([xs, loss_target], [wts["final_norm_w"][None, :]])[0]
    return jnp.sum(per_row[:, 0])


SHARDED = ("w_ada", "w_in", "a_conv_w", "w_branch", "w_out")
MATMUL_ONLY = ("w_ada", "w_in", "w_branch", "w_out")
SMALL = ("c_ctx", "b_ada", "norm_w", "a_log", "a_dt_bias", "a_norm_w", "b_sink", "c_decay", "c_norm_w",
         "final_norm_w")
WEIGHTS = ("c_ctx", "w_ada", "b_ada", "norm_w", "w_in", "a_conv_w", "a_log", "a_dt_bias", "a_norm_w", "b_sink",
           "c_decay", "c_norm_w", "w_branch", "w_out", "final_norm_w")
SMALL_PACK = 12288


def _wire(name, a):
    return a.astype(BF16) if name in MATMUL_ONLY else a


def _unshard(name, g):
    if name == "w_branch":
        return g.transpose(1, 2, 3, 0, 4).reshape(DEPTH, 3, BR_WIDTH, D_MODEL)
    if name == "w_out":
        return g.transpose(1, 0, 2, 3).reshape(DEPTH, D_MODEL, D_MODEL)
    s = g.shape
    return g.transpose(1, 2, 0, 3).reshape(s[1], s[2], N_DEV * s[3])


def _reshard(name, w):
    if name == "w_branch":
        return w.reshape(DEPTH, 3, BR_WIDTH, N_DEV, D_MODEL // N_DEV).transpose(3, 0, 1, 2, 4)
    if name == "w_out":
        return w.reshape(DEPTH, N_DEV, D_MODEL // N_DEV, D_MODEL).transpose(1, 0, 2, 3)
    s = w.shape
    return w.reshape(s[0], s[1], N_DEV, s[2] // N_DEV).transpose(2, 0, 1, 3)


def _pack_small(tree):
    flat = jnp.concatenate([tree[n].reshape(-1) for n in SMALL])
    return jnp.pad(flat, (0, SMALL_PACK - flat.shape[0])).reshape(SMALL_PACK // 128, 128)


def _unpack_small(packed, like):
    flat = packed.reshape(-1)
    out, off = {}, 0
    for n in SMALL:
        size = math.prod(like[n].shape)
        out[n] = flat[off:off + size].reshape(like[n].shape)
        off += size
    return out


def kernel(x, c, ctx, c_ctx, w_ada, b_ada, norm_w, w_in, a_conv_w, a_log, a_dt_bias, a_norm_w, b_sink, c_decay, c_norm_w, w_branch, w_out, final_norm_w, loss_target, m_c_ctx, m_w_ada, m_b_ada, m_norm_w, m_w_in, m_a_conv_w, m_a_log, m_a_dt_bias, m_a_norm_w, m_b_sink, m_c_decay, m_c_norm_w, m_w_branch, m_w_out, m_final_norm_w, v_c_ctx, v_w_ada, v_b_ada, v_norm_w, v_w_in, v_a_conv_w, v_a_log, v_a_dt_bias, v_a_norm_w, v_b_sink, v_c_decay, v_c_norm_w, v_w_branch, v_w_out, v_final_norm_w):
    w = dict(c_ctx=c_ctx, w_ada=w_ada, b_ada=b_ada, norm_w=norm_w, w_in=w_in, a_conv_w=a_conv_w, a_log=a_log,
             a_dt_bias=a_dt_bias, a_norm_w=a_norm_w, b_sink=b_sink, c_decay=c_decay, c_norm_w=c_norm_w,
             w_branch=w_branch, w_out=w_out, final_norm_w=final_norm_w)
    m = dict(c_ctx=m_c_ctx, w_ada=m_w_ada, b_ada=m_b_ada, norm_w=m_norm_w, w_in=m_w_in, a_conv_w=m_a_conv_w,
             a_log=m_a_log, a_dt_bias=m_a_dt_bias, a_norm_w=m_a_norm_w, b_sink=m_b_sink, c_decay=m_c_decay,
             c_norm_w=m_c_norm_w, w_branch=m_w_branch, w_out=m_w_out, final_norm_w=m_final_norm_w)
    v = dict(c_ctx=v_c_ctx, w_ada=v_w_ada, b_ada=v_b_ada, norm_w=v_norm_w, w_in=v_w_in, a_conv_w=v_a_conv_w,
             a_log=v_a_log, a_dt_bias=v_a_dt_bias, a_norm_w=v_a_norm_w, b_sink=v_b_sink, c_decay=v_c_decay,
             c_norm_w=v_c_norm_w, w_branch=v_w_branch, w_out=v_w_out, final_norm_w=v_final_norm_w)

    gathered = _exchange([_wire(n, w[n]) for n in SHARDED], True, "gather_weights")
    full = dict(w)
    for n, g in zip(SHARDED, gathered):
        full[n] = _unshard(n, g).astype(F32)

    loss, (gw, gx) = jax.value_and_grad(_local_loss, argnums=(0, 1))(full, x[0], c[0], ctx[0], loss_target[0])
    loss = lax.psum(loss, ("x", "y", "c"))

    blocks = _exchange([_wire(n, _reshard(n, gw[n])) for n in SHARDED], False, "scatter_grads")
    small = _exchange([_pack_small(gw)], True, "gather_small_grads")[0]

    grad, delta, new_m, new_v = {}, {}, {}, {}
    for n, contrib in zip(SHARDED, blocks):
        shp = w[n].shape
        two_d = (math.prod(shp[:-1]), shp[-1])
        outs = _adamw(w[n].reshape(two_d), m[n].reshape(two_d), v[n].reshape(two_d),
                      contrib.reshape((N_DEV,) + two_d), "adamw_" + n)
        grad[n], delta[n], new_m[n], new_v[n] = [o.reshape(shp) for o in outs]
    outs = _adamw(_pack_small(w), _pack_small(m), _pack_small(v), small, "adamw_small")
    for tree, packed in zip((grad, delta, new_m, new_v), outs):
        tree.update(_unpack_small(packed, w))

    return (loss, gx[None], *[grad[n] for n in WEIGHTS], *[delta[n] for n in WEIGHTS],
            *[new_m[n] for n in WEIGHTS], *[new_v[n] for n in WEIGHTS])
```

```python
import functools
import math

import jax
import jax.numpy as jnp
from jax import lax
from jax.experimental import pallas as pl
from jax.experimental.pallas import tpu as pltpu

F32 = jnp.float32
BF16 = jnp.bfloat16
INV_PRECISION = lax.Precision.HIGH

D_MODEL = 1024
SEQ = 4096
DEPTH = 2
GRID_W = 64
CTX_LEN = 256
EPS = 1e-6
ROPE_BASE = 10000.0
BR_WIDTH = D_MODEL // 2
A_DK = 128
A_HEADS = 4
A_WIDTH = 512
A_CONV = 5
B_HD = 64
B_Q_HEADS = 8
B_KV_HEADS = 2
WINDOW = 128
B_BLOCK = 128
C_HD = 128
C_HEADS = 4
C_WIDTH = 512
CHUNK = 64
ADAM_LR = 0.001
ADAM_B1 = 0.9
ADAM_B2 = 0.999
ADAM_EPS = 1e-08
ADAM_WD = 0.01
ADAM_STEP = 10

N_DEV = 8
ROWS = CTX_LEN + SEQ
N_CHUNK = ROWS // CHUNK
N_CTX_CHUNK = CTX_LEN // CHUNK
IN_WIDTH = 8464
IN_PAD = 8704
NEG = -1e30

VMEM_LIMIT = 48 * 1024 * 1024
MESH = pl.DeviceIdType.MESH

C_AQ, C_AK, C_AV, C_AZ, C_BQ, C_BZ, C_CQ, C_CK, C_CV, C_CZ = (i * 512 for i in range(10))
C_MERGE = 5120
C_BKV = 8192
C_AB = 8448


def _cparams(sem=None):
    if sem is None:
        return pltpu.CompilerParams(vmem_limit_bytes=VMEM_LIMIT)
    return pltpu.CompilerParams(dimension_semantics=sem, vmem_limit_bytes=VMEM_LIMIT)


def _dg(a, b, ca, cb, prec=None):
    return lax.dot_general(a, b, (((ca,), (cb,)), ((), ())), preferred_element_type=F32, precision=prec)


@functools.partial(jax.custom_vjp, nondiff_argnums=(2, 3))
def _bdot(a, b, ca, cb):
    return _dg(a.astype(BF16), b.astype(BF16), ca, cb)


def _bdot_fwd(a, b, ca, cb):
    return _bdot(a, b, ca, cb), (a, b)


def _bdot_bwd(ca, cb, res, ct):
    a, b = res
    da = _bdot(ct, b, 1, 1 - cb) if ca == 1 else _bdot(b, ct, 1 - cb, 1)
    db = _bdot(a, ct, 1 - ca, 0) if cb == 0 else _bdot(ct, a, 0, 1 - ca)
    return da, db


_bdot.defvjp(_bdot_fwd, _bdot_bwd)


def _hdot(a, b):
    return _dg(a, b, 1, 0, INV_PRECISION)


def _k_silu(x):
    return x / (1.0 + jnp.exp(-x))


def _k_sigmoid(x):
    return 1.0 / (1.0 + jnp.exp(-x))


@jax.custom_vjp
def _swap64(x):
    return pltpu.roll(x, 64, 1)


_swap64.defvjp(lambda x: (pltpu.roll(x, 64, 1), None), lambda _, ct: (pltpu.roll(ct, 64, 1),))


def _swap16_impl(x):
    lane = lax.broadcasted_iota(jnp.int32, x.shape, 1)
    return jnp.where((lane & 16) == 0, pltpu.roll(x, 112, 1), pltpu.roll(x, 16, 1))


@jax.custom_vjp
def _swap16(x):
    return _swap16_impl(x)


_swap16.defvjp(lambda x: (_swap16_impl(x), None), lambda _, ct: (_swap16_impl(ct),))


def _pick(dim, prefs):
    for p in prefs:
        if dim % p == 0:
            return p
    return dim


def _matmul(a, b, name, mode="nn"):
    ca, cb = {"nn": (1, 0), "nt": (1, 1), "tn": (0, 0)}[mode]
    m, k = a.shape[1 - ca], a.shape[ca]
    n = b.shape[1 - cb]
    tm = _pick(m, (1088, 1024, 512, 256, 128))
    tn = _pick(n, (512, 256, 128))
    tk = _pick(k, (1088, 1024, 512, 256, 128) if mode == "tn" else (2176, 2048, 1024, 512, 256, 128))
    nk = k // tk
    a_spec = (pl.BlockSpec((tm, tk), lambda i, j, kk: (i, kk)) if ca == 1
              else pl.BlockSpec((tk, tm), lambda i, j, kk: (kk, i)))
    b_spec = (pl.BlockSpec((tk, tn), lambda i, j, kk: (kk, j)) if cb == 0
              else pl.BlockSpec((tn, tk), lambda i, j, kk: (j, kk)))

    def body(a_ref, b_ref, o_ref, acc_ref):
        part = _dg(a_ref[...].astype(BF16), b_ref[...].astype(BF16), ca, cb)
        if nk == 1:
            o_ref[...] = part
        else:
            kk = pl.program_id(2)

            @pl.when(kk == 0)
            def _():
                acc_ref[...] = part

            @pl.when(kk > 0)
            def _():
                acc_ref[...] += part

            @pl.when(kk == nk - 1)
            def _():
                o_ref[...] = acc_ref[...]

    return pl.pallas_call(
        body,
        grid=(m // tm, n // tn, nk),
        in_specs=[a_spec, b_spec],
        out_specs=pl.BlockSpec((tm, tn), lambda i, j, kk: (i, j)),
        out_shape=jax.ShapeDtypeStruct((m, n), F32),
        scratch_shapes=[pltpu.VMEM((tm, tn), F32)],
        compiler_params=_cparams(("parallel", "parallel", "arbitrary")),
        name=name,
    )(a, b)


@functools.partial(jax.custom_vjp, nondiff_argnums=(2,))
def _mm(a, b, name):
    return _matmul(a, b, name)


def _mm_fwd(a, b, name):
    b16 = b.astype(BF16)
    return _matmul(a, b16, name), (a, b16)


def _mm_bwd(name, res, ct):
    a, b16 = res
    da = _matmul(ct, b16, name + "_da", "nt")
    db = _matmul(a, ct, name + "_db", "tn")
    return da, db


_mm.defvjp(_mm_fwd, _mm_bwd)


ROW_BLOCK = 256


def _pieces(val, pw):
    return [val[:, j * pw:(j + 1) * pw] for j in range(val.shape[1] // pw)]


def _flat(groups):
    arrays, sizes = [], []
    for g in groups:
        g = g if isinstance(g, (tuple, list)) else (g,)
        arrays += list(g)
        sizes.append(len(g))
    return arrays, sizes


def _regroup(refs, sizes):
    out, at = [], 0
    for n in sizes:
        val = refs[at][...]
        for r in refs[at + 1:at + n]:
            val = val + r[...]
        out.append(val)
        at += n
    return out


class _Rowwise:
    def __init__(self, fn, name, row_wpw, par_pw, out_wpw, n_diff=None):
        self.fn, self.name, self.row_wpw, self.par_pw, self.out_wpw = fn, name, row_wpw, par_pw, out_wpw
        self.n_diff = len(row_wpw) if n_diff is None else n_diff

        @jax.custom_vjp
        def call(rows, params):
            return self.fwd(rows, params)

        def call_fwd(rows, params):
            return self.fwd(rows, params), (rows, params)

        def call_bwd(res, douts):
            return self.bwd(res[0], res[1], douts)

        call.defvjp(call_fwd, call_bwd)
        self.call = call

    def _load(self, row_vals, par_refs, with_ctx):
        is_ctx = jnp.logical_and(with_ctx, pl.program_id(0) == 0)
        rows = [_pieces(v, pw) for v, (_, pw) in zip(row_vals, self.row_wpw)]
        pars = []
        for p, pw in zip(par_refs, self.par_pw):
            val = p[...] if p.shape[0] == 1 else jnp.where(is_ctx, p[0:1, :], p[1:2, :])
            pars.append(_pieces(val, pw))
        return rows, pars, is_ctx

    def _row_specs(self, sizes, cols):
        out = []
        for (w, _), n, c in zip(self.row_wpw, sizes, cols):
            out += [pl.BlockSpec((ROW_BLOCK, w), lambda i, c=c: (i, c))] * n
        return out

    def fwd(self, rows, params, cols=None):
        arrays, sizes = _flat(rows)
        cols = cols or [0] * len(rows)
        n_rows = arrays[0].shape[0]
        n_in = len(arrays)

        def body(*refs):
            r, p, _ = self._load(_regroup(refs[:n_in], sizes), refs[n_in:n_in + len(params)], n_rows == ROWS)
            for o_ref, pieces, (_, pw) in zip(refs[n_in + len(params):], self.fn(r, p), self.out_wpw):
                for j, piece in enumerate(pieces):
                    o_ref[:, j * pw:(j + 1) * pw] = piece

        return pl.pallas_call(
            body,
            grid=(n_rows // ROW_BLOCK,),
            in_specs=self._row_specs(sizes, cols) + [pl.BlockSpec(p.shape, lambda i: (0, 0)) for p in params],
            out_specs=[pl.BlockSpec((ROW_BLOCK, w), lambda i: (i, 0)) for w, _ in self.out_wpw],
            out_shape=[jax.ShapeDtypeStruct((n_rows, w), F32) for w, _ in self.out_wpw],
            compiler_params=_cparams(("parallel",)),
            name=self.name + "_fwd",
        )(*arrays, *params)

    def bwd(self, rows, params, douts, cols=None):
        arrays, sizes = _flat(rows)
        darrays, dsizes = _flat(douts)
        cols = cols or [0] * len(rows)
        n_rows = arrays[0].shape[0]
        n_in, n_par, n_dout, n_diff = len(arrays), len(params), len(darrays), self.n_diff

        def body(*refs):
            par_refs = refs[n_in:n_in + n_par]
            dout_refs = refs[n_in + n_par:n_in + n_par + n_dout]
            drow_refs = refs[n_in + n_par + n_dout:n_in + n_par + n_dout + n_diff]
            dpar_refs = refs[n_in + n_par + n_dout + n_diff:]

            @pl.when(pl.program_id(0) == 0)
            def _():
                for d in dpar_refs:
                    d[...] = jnp.zeros_like(d)

            r, p, is_ctx = self._load(_regroup(refs[:n_in], sizes), par_refs, n_rows == ROWS)
            cts = [_pieces(d, pw) for d, (_, pw) in zip(_regroup(dout_refs, dsizes), self.out_wpw)]
            fixed = r[n_diff:]
            _, vjp = jax.vjp(lambda rd, pp: self.fn(rd + fixed, pp), r[:n_diff], p)
            dr, dp = vjp(cts)
            for d_ref, pieces, (_, pw) in zip(drow_refs, dr, self.row_wpw):
                for j, piece in enumerate(pieces):
                    d_ref[:, j * pw:(j + 1) * pw] = piece
            for d_ref, pieces, pw in zip(dpar_refs, dp, self.par_pw):
                for j, piece in enumerate(pieces):
                    lanes = slice(j * pw, (j + 1) * pw)
                    if d_ref.shape[0] == 1:
                        d_ref[:, lanes] += piece
                    else:
                        d_ref[0:1, lanes] += jnp.where(is_ctx, piece, 0.0)
                        d_ref[1:2, lanes] += jnp.where(is_ctx, 0.0, piece)

        par_specs = [pl.BlockSpec(p.shape, lambda i: (0, 0)) for p in params]
        dout_specs = []
        for (w, _), n in zip(self.out_wpw, dsizes):
            dout_specs += [pl.BlockSpec((ROW_BLOCK, w), lambda i: (i, 0))] * n
        drow_w = [w for w, _ in self.row_wpw[:n_diff]]
        g = pl.pallas_call(
            body,
            grid=(n_rows // ROW_BLOCK,),
            in_specs=self._row_specs(sizes, cols) + par_specs + dout_specs,
            out_specs=[pl.BlockSpec((ROW_BLOCK, w), lambda i: (i, 0)) for w in drow_w] + par_specs,
            out_shape=[jax.ShapeDtypeStruct((n_rows, w), F32) for w in drow_w]
            + [jax.ShapeDtypeStruct(p.shape, F32) for p in params],
            compiler_params=_cparams(("arbitrary",)),
            name=self.name + "_bwd",
        )(*arrays, *params, *darrays)
        return list(g[:n_diff]), list(g[n_diff:])


def _fn_norm_mod(rows, pars):
    (x,), (nw,), (shift,), (scale,) = rows[0], pars[0], pars[1], pars[2]
    y = x * lax.rsqrt(jnp.mean(x * x, axis=-1, keepdims=True) + EPS) * nw
    return [[y * (1.0 + scale) + shift]]


def _fn_head_rms_gate(rows, pars):
    (w,) = pars[0]
    return [[o * lax.rsqrt(jnp.mean(o * o, axis=-1, keepdims=True) + EPS) * w * _k_silu(z)
             for o, z in zip(rows[0], rows[1])]]


def _fn_group_norm_gate(rows, pars):
    out = []
    for o, z, w in zip(rows[0], rows[1], pars[0]):
        mu = jnp.mean(o, axis=-1, keepdims=True)
        var = jnp.mean(jnp.square(o - mu), axis=-1, keepdims=True)
        out.append((o - mu) * lax.rsqrt(var + EPS) * w * _k_silu(z))
    return [out]


def _fn_gate(rows, pars):
    return [[o * _k_silu(z) for o, z in zip(rows[0], rows[1])]]


def _fn_merge(rows, pars):
    (ma,), (mb,), (mc,), (pa,), (pb,), (pc,) = rows
    return [[_k_sigmoid(ma) * pa + _k_sigmoid(mb) * pb + _k_sigmoid(mc) * pc]]


def _fn_residual(rows, pars):
    (res,), (out,), (gate,) = rows[0], rows[1], pars[0]
    return [[res + gate * out]]


def _fn_loss(rows, pars):
    (x,), (target,), (w,) = rows[0], rows[1], pars[0]
    y = x * lax.rsqrt(jnp.mean(x * x, axis=-1, keepdims=True) + EPS) * w
    per_row = 0.5 * jnp.mean(jnp.square(y - target), axis=-1, keepdims=True)
    return [[jnp.broadcast_to(per_row, (ROW_BLOCK, 128))]]


def _fn_b_rope(rows, pars):
    q, (k, v), (cos,), (sin,) = rows
    rot = lambda x: x * cos + _swap16(x) * sin
    return [[rot(x) for x in q], [rot(k), v]]


def _fn_c_rope(rows, pars):
    q, k, (cos,), (sin,) = rows
    rot = lambda x: x * cos + _swap64(x) * sin
    return [[rot(x) for x in q], [rot(x) * (C_HD ** -0.5) for x in k]]


_norm_mod = _Rowwise(_fn_norm_mod, "norm_mod", [(D_MODEL, D_MODEL)], [D_MODEL] * 3, [(D_MODEL, D_MODEL)])
_residual = _Rowwise(_fn_residual, "residual", [(D_MODEL, D_MODEL)] * 2, [D_MODEL], [(D_MODEL, D_MODEL)])
_loss_rows = _Rowwise(_fn_loss, "loss", [(D_MODEL, D_MODEL)] * 2, [D_MODEL], [(128, 128)])
_a_out = _Rowwise(_fn_head_rms_gate, "a_out", [(512, 128)] * 2, [128], [(512, 128)])
_c_out = _Rowwise(_fn_group_norm_gate, "c_out", [(512, 128)] * 2, [128], [(512, 128)])
_b_out = _Rowwise(_fn_gate, "b_out", [(512, 512)] * 2, [], [(512, 512)])
_merge = _Rowwise(_fn_merge, "merge", [(D_MODEL, D_MODEL)] * 6, [], [(D_MODEL, D_MODEL)])
_b_rope = _Rowwise(_fn_b_rope, "b_rope", [(512, 128), (256, 128), (128, 128), (128, 128)], [],
                   [(512, 128), (256, 128)], n_diff=2)
_c_rope = _Rowwise(_fn_c_rope, "c_rope", [(512, 128), (512, 128), (128, 128), (128, 128)], [],
                   [(512, 128), (512, 128)], n_diff=2)


HALO = 8
EXT = ROW_BLOCK + 2 * HALO


def _halo_specs(col, width=512):
    last = ROWS // HALO - 1
    per = ROW_BLOCK // HALO
    prev = pl.BlockSpec((HALO, width), lambda i: (jnp.maximum(i * per - 1, 0), col))
    cur = pl.BlockSpec((ROW_BLOCK, width), lambda i: (i, col))
    nxt = pl.BlockSpec((HALO, width), lambda i: (jnp.minimum((i + 1) * per, last), col))
    return [prev, cur, nxt]


def _extended(prev_ref, cur_ref, next_ref):
    i = pl.program_id(0)
    prev_ok = i >= 2
    next_ok = jnp.logical_and(i >= 1, i < ROWS // ROW_BLOCK - 1)
    return jnp.concatenate([jnp.where(prev_ok, prev_ref[...], 0.0), cur_ref[...],
                            jnp.where(next_ok, next_ref[...], 0.0)], axis=0)


def _conv_taps(x_ext, w_ref, flip):
    acc = None
    for j in range(A_CONV):
        shift = (j - 2) if flip else (2 - j)
        term = w_ref[j:j + 1, :] * pltpu.roll(x_ext, shift % EXT, 0)
        acc = term if acc is None else acc + term
    return acc


def _conv_post(pre_pieces, normalize, scale):
    out = []
    for p in pre_pieces:
        y = _k_silu(p)
        if normalize:
            y = y * lax.rsqrt(jnp.sum(y * y, axis=-1, keepdims=True) + EPS) * scale
        out.append(y)
    return out


def _a_prep_fwd(proj, conv8, col, normalize, scale, name):
    def body(prev_ref, cur_ref, next_ref, w_ref, o_ref):
        pre = _conv_taps(_extended(prev_ref, cur_ref, next_ref), w_ref, False)[HALO:HALO + ROW_BLOCK]
        for h, y in enumerate(_conv_post(_pieces(pre, 128), normalize, scale)):
            o_ref[:, h * 128:(h + 1) * 128] = y

    return pl.pallas_call(
        body,
        grid=(ROWS // ROW_BLOCK,),
        in_specs=_halo_specs(col) + [pl.BlockSpec((8, 512), lambda i: (0, col))],
        out_specs=pl.BlockSpec((ROW_BLOCK, 512), lambda i: (i, 0)),
        out_shape=jax.ShapeDtypeStruct((ROWS, 512), F32),
        compiler_params=_cparams(("parallel",)),
        name=name + "_fwd",
    )(proj, proj, proj, conv8)


def _a_prep_bwd(proj, conv8, col, normalize, scale, dout_f, dout_r, name):
    def body(xp, xc, xn, w_ref, fp, fc, fn_, rp, rc, rn, dx_ref, dw_ref):
        @pl.when(pl.program_id(0) == 0)
        def _():
            dw_ref[...] = jnp.zeros_like(dw_ref)

        x_ext = _extended(xp, xc, xn)
        dout = _extended(fp, fc, fn_) + _extended(rp, rc, rn)
        pre = _conv_taps(x_ext, w_ref, False)
        _, vjp = jax.vjp(lambda p: _conv_post(p, normalize, scale), _pieces(pre, 128))
        (dpre,) = vjp(_pieces(dout, 128))
        dpre = jnp.concatenate(dpre, axis=1)
        dx_ref[...] = _conv_taps(dpre, w_ref, True)[HALO:HALO + ROW_BLOCK]
        own = dpre[HALO:HALO + ROW_BLOCK]
        for j in range(A_CONV):
            shifted = pltpu.roll(x_ext, (2 - j) % EXT, 0)[HALO:HALO + ROW_BLOCK]
            dw_ref[j:j + 1, :] += jnp.sum(own * shifted, axis=0, keepdims=True)

    return pl.pallas_call(
        body,
        grid=(ROWS // ROW_BLOCK,),
        in_specs=_halo_specs(col) + [pl.BlockSpec((8, 512), lambda i: (0, col))] + _halo_specs(0) + _halo_specs(0),
        out_specs=[pl.BlockSpec((ROW_BLOCK, 512), lambda i: (i, 0)), pl.BlockSpec((8, 512), lambda i: (0, 0))],
        out_shape=[jax.ShapeDtypeStruct((ROWS, 512), F32), jax.ShapeDtypeStruct((8, 512), F32)],
        compiler_params=_cparams(("arbitrary",)),
        name=name + "_bwd",
    )(proj, proj, proj, conv8, dout_f, dout_f, dout_f, dout_r, dout_r, dout_r)


N_CHAIN = 8


def _rev_chunk(s):
    return jnp.where(s < N_CTX_CHUNK, N_CTX_CHUNK - 1 - s, N_CHUNK + N_CTX_CHUNK - 1 - s)


def _scan_specs(step_of, v_col=0):
    cf = step_of
    cr = lambda n: _rev_chunk(step_of(n))

    def pair(shape, index):
        return (pl.BlockSpec(shape, lambda n: index(cf(n))), pl.BlockSpec(shape, lambda n: index(cr(n))))

    return dict(
        tok=pair((CHUNK, 512), lambda c: (c, 0)),
        tokv=pair((CHUNK, 512), lambda c: (c, v_col)),
        col=pair((4, CHUNK, 1), lambda c: (0, c, 0)),
        row=pair((4, 1, 1, CHUNK), lambda c: (0, c, 0, 0)),
        one=pair((4, 1, 1, 1), lambda c: (0, c, 0, 0)),
        state=pair((None, 4, 128, 128), lambda c: (c, 0, 0, 0)),
        tinv=pair((None, 4, CHUNK, CHUNK), lambda c: (c, 0, 0, 0)),
    )


def _both(specs, kinds):
    out = []
    for kind in kinds:
        out += list(specs[kind])
    return out


def _chain_masks():
    ii = lax.broadcasted_iota(jnp.int32, (CHUNK, CHUNK), 0)
    jj = lax.broadcasted_iota(jnp.int32, (CHUNK, CHUNK), 1)
    eye = jnp.where(ii == jj, 1.0, 0.0).astype(F32)
    lower = (ii >= jj, ii > jj)
    upper = (ii <= jj, ii < jj)
    return [lower] * 4 + [upper] * 4, eye


def _tri_inv_all(ls, eye):
    xs = [eye - l for l in ls]
    ps = [_hdot(l, l) for l in ls]
    for i in range(5):
        xs = [x + _hdot(x, p) for x, p in zip(xs, ps)]
        if i < 4:
            ps = [_hdot(p, p) for p in ps]
    return xs


@jax.custom_vjp
def _inv_saved(l, x):
    return x


def _inv_saved_fwd(l, x):
    return x, x


def _inv_saved_bwd(x, dx):
    return -_bdot(x, _bdot(dx, x, 1, 1), 0, 0), jnp.zeros_like(x)


_inv_saved.defvjp(_inv_saved_fwd, _inv_saved_bwd)


def _delta_chains(q, k, v, beta, gcc, gcr, gl, s, masks, eye, tinv_saved):
    n = range(len(q))
    decay = [jnp.exp(jnp.where(masks[i][0], gcc[i] - gcr[i], NEG)) for i in n]
    kb = [k[i] * beta[i] for i in n]
    lmat = [jnp.where(masks[i][1], _bdot(kb[i], k[i], 1, 1) * decay[i], 0.0) for i in n]
    if tinv_saved is None:
        tinv = _tri_inv_all(lmat, eye)
    else:
        tinv = [_inv_saved(lmat[i], tinv_saved[i]) for i in n]
    eg = [jnp.exp(gcc[i]) for i in n]
    u = [_bdot(tinv[i], v[i] * beta[i], 1, 0) for i in n]
    w = [_bdot(tinv[i], kb[i] * eg[i], 1, 0) for i in n]
    qk = [_bdot(q[i], k[i], 1, 1) * decay[i] for i in n]
    v_new = [u[i] - _bdot(w[i], s[i], 1, 0) for i in n]
    o = [_bdot(q[i] * eg[i], s[i], 1, 0) + _bdot(qk[i], v_new[i], 1, 0) for i in n]
    s_new = [s[i] * jnp.exp(gl[i]) + _bdot(k[i] * jnp.exp(gl[i] - gcc[i]), v_new[i], 0, 0) for i in n]
    return (o, s_new), tinv


def _chain_loads(tok_pairs, small_pairs):
    toks = [[pair[i // 4][:, (i % 4) * 128:(i % 4 + 1) * 128] for i in range(N_CHAIN)] for pair in tok_pairs]
    smalls = [[pair[i // 4][i % 4] for i in range(N_CHAIN)] for pair in small_pairs]
    return toks, smalls


def _delta_fwd_call(q, k, v, beta, gcc, gcr, gl):
    sp = _scan_specs(lambda n: n)

    def body(qf, qr, kf, kr, vf, vr, bf, br, gccf, gccr, gcrf, gcrr, glf, glr,
             of, orv, ssf, ssr, tsf, tsr, s_scr):
        @pl.when(pl.program_id(0) == 0)
        def _():
            s_scr[...] = jnp.zeros_like(s_scr)

        masks, eye = _chain_masks()
        (qs, ks, vs), (bs, gccs) = _chain_loads([(qf, qr), (kf, kr), (vf, vr)], [(bf, br), (gccf, gccr)])
        gcrs = [(gcrf, gcrr)[i // 4][i % 4, 0] for i in range(N_CHAIN)]
        gls = [(glf, glr)[i // 4][i % 4, 0] for i in range(N_CHAIN)]
        ss = [s_scr[i] for i in range(N_CHAIN)]
        (o, s_new), tinv = _delta_chains(qs, ks, vs, bs, gccs, gcrs, gls, ss, masks, eye, None)
        for i in range(N_CHAIN):
            d, h = i // 4, i % 4
            (ssf, ssr)[d][h] = ss[i]
            (tsf, tsr)[d][h] = tinv[i]
            (of, orv)[d][:, h * 128:(h + 1) * 128] = o[i]
            s_scr[i] = s_new[i]

    return pl.pallas_call(
        body,
        grid=(N_CHUNK,),
        in_specs=_both(sp, ["tok", "tok", "tok", "col", "col", "row", "one"]),
        out_specs=_both(sp, ["tok", "state", "tinv"]),
        out_shape=[jax.ShapeDtypeStruct((ROWS, 512), F32)] * 2
        + [jax.ShapeDtypeStruct((N_CHUNK, 4, 128, 128), F32)] * 2
        + [jax.ShapeDtypeStruct((N_CHUNK, 4, CHUNK, CHUNK), F32)] * 2,
        scratch_shapes=[pltpu.VMEM((N_CHAIN, 128, 128), F32)],
        compiler_params=_cparams(("arbitrary",)),
        name="delta_fwd",
    )(q, q, k, k, v, v, *beta, *gcc, *gcr, *gl)


def _delta_bwd_call(q, k, v, beta, gcc, gcr, gl, ssave, tsave, do):
    sp = _scan_specs(lambda n: N_CHUNK - 1 - n)

    def body(qf, qr, kf, kr, vf, vr, bf, br, gccf, gccr, gcrf, gcrr, glf, glr, ssf, ssr, tsf, tsr, dof, dor,
             dqf, dqr, dkf, dkr, dvf, dvr, dbf, dbr, dgccf, dgccr, dgcrf, dgcrr, dglf, dglr, ds_scr):
        @pl.when(pl.program_id(0) == 0)
        def _():
            ds_scr[...] = jnp.zeros_like(ds_scr)

        masks, eye = _chain_masks()
        (qs, ks, vs, dos), (bs, gccs, ss, ts) = _chain_loads(
            [(qf, qr), (kf, kr), (vf, vr), (dof, dor)], [(bf, br), (gccf, gccr), (ssf, ssr), (tsf, tsr)])
        gcrs = [(gcrf, gcrr)[i // 4][i % 4, 0] for i in range(N_CHAIN)]
        gls = [(glf, glr)[i // 4][i % 4, 0] for i in range(N_CHAIN)]
        fn = lambda *a: _delta_chains(*a, masks, eye, ts)
        _, vjp, _ = jax.vjp(fn, qs, ks, vs, bs, gccs, gcrs, gls, ss, has_aux=True)
        dq, dk, dv, db, dgcc, dgcr, dgl, ds = vjp((dos, [ds_scr[i] for i in range(N_CHAIN)]))
        for i in range(N_CHAIN):
            d, h = i // 4, i % 4
            hs = slice(h * 128, (h + 1) * 128)
            (dqf, dqr)[d][:, hs] = dq[i]
            (dkf, dkr)[d][:, hs] = dk[i]
            (dvf, dvr)[d][:, hs] = dv[i]
            (dbf, dbr)[d][h] = db[i]
            (dgccf, dgccr)[d][h] = dgcc[i]
            (dgcrf, dgcrr)[d][h, 0] = dgcr[i]
            (dglf, dglr)[d][h, 0] = dgl[i]
            ds_scr[i] = ds[i]

    tok = jax.ShapeDtypeStruct((ROWS, 512), F32)
    return pl.pallas_call(
        body,
        grid=(N_CHUNK,),
        in_specs=_both(sp, ["tok", "tok", "tok", "col", "col", "row", "one", "state", "tinv", "tok"]),
        out_specs=_both(sp, ["tok", "tok", "tok", "col", "col", "row", "one"]),
        out_shape=[tok] * 6 + [jax.ShapeDtypeStruct((4, ROWS, 1), F32)] * 4
        + [jax.ShapeDtypeStruct((4, N_CHUNK, 1, CHUNK), F32)] * 2 + [jax.ShapeDtypeStruct((4, N_CHUNK, 1, 1), F32)] * 2,
        scratch_shapes=[pltpu.VMEM((N_CHAIN, 128, 128), F32)],
        compiler_params=_cparams(("arbitrary",)),
        name="delta_bwd",
    )(q, q, k, k, v, v, *beta, *gcc, *gcr, *gl, *ssave, *tsave, do, do)


def _ret_chains(q, k, v, dm, qs, ks, cd, s):
    n = range(len(q))
    a = [_bdot(q[i], k[i], 1, 1) * dm[i] for i in n]
    o = [_bdot(a[i], v[i], 1, 0) + _bdot(q[i] * qs[i], s[i], 1, 0) for i in n]
    s_new = [s[i] * cd[i] + _bdot(k[i] * ks[i], v[i], 0, 0) for i in n]
    return o, s_new


def _ret_const_specs():
    return [pl.BlockSpec((N_CHAIN, CHUNK, CHUNK), lambda n: (0, 0, 0)), pl.BlockSpec((N_CHAIN, CHUNK, 1), lambda n: (0, 0, 0)),
            pl.BlockSpec((N_CHAIN, CHUNK, 1), lambda n: (0, 0, 0)), pl.BlockSpec((N_CHAIN, 1, 1), lambda n: (0, 0, 0))]


def _ret_fwd_call(q, k, v, v_col, dm, qs, ks, cd):
    sp = _scan_specs(lambda n: n, v_col)

    def body(qf, qr, kf, kr, vf, vr, dm_ref, qs_ref, ks_ref, cd_ref, of, orv, ssf, ssr, s_scr):
        @pl.when(pl.program_id(0) == 0)
        def _():
            s_scr[...] = jnp.zeros_like(s_scr)

        (qc, kc, vc), _ = _chain_loads([(qf, qr), (kf, kr), (vf, vr)], [])
        ss = [s_scr[i] for i in range(N_CHAIN)]
        consts = [[r[i] for i in range(N_CHAIN)] for r in (dm_ref, qs_ref, ks_ref, cd_ref)]
        o, s_new = _ret_chains(qc, kc, vc, *consts, ss)
        for i in range(N_CHAIN):
            d, h = i // 4, i % 4
            (ssf, ssr)[d][h] = ss[i]
            (of, orv)[d][:, h * 128:(h + 1) * 128] = o[i]
            s_scr[i] = s_new[i]

    return pl.pallas_call(
        body,
        grid=(N_CHUNK,),
        in_specs=_both(sp, ["tok", "tok", "tokv"]) + _ret_const_specs(),
        out_specs=_both(sp, ["tok", "state"]),
        out_shape=[jax.ShapeDtypeStruct((ROWS, 512), F32)] * 2 + [jax.ShapeDtypeStruct((N_CHUNK, 4, 128, 128), F32)] * 2,
        scratch_shapes=[pltpu.VMEM((N_CHAIN, 128, 128), F32)],
        compiler_params=_cparams(("arbitrary",)),
        name="ret_fwd",
    )(q, q, k, k, v, v, dm, qs, ks, cd)


def _ret_bwd_call(q, k, v, v_col, dm, qs, ks, cd, ssave, do):
    sp = _scan_specs(lambda n: N_CHUNK - 1 - n, v_col)

    def body(qf, qr, kf, kr, vf, vr, dm_ref, qs_ref, ks_ref, cd_ref, ssf, ssr, dof, dor,
             dqf, dqr, dkf, dkr, dvf, dvr, ddm_ref, dqs_ref, dks_ref, dcd_ref, ds_scr):
        @pl.when(pl.program_id(0) == 0)
        def _():
            ds_scr[...] = jnp.zeros_like(ds_scr)
            ddm_ref[...] = jnp.zeros_like(ddm_ref)
            dqs_ref[...] = jnp.zeros_like(dqs_ref)
            dks_ref[...] = jnp.zeros_like(dks_ref)
            dcd_ref[...] = jnp.zeros_like(dcd_ref)

        (qc, kc, vc, dos), (ss,) = _chain_loads([(qf, qr), (kf, kr), (vf, vr), (dof, dor)], [(ssf, ssr)])
        consts = [[r[i] for i in range(N_CHAIN)] for r in (dm_ref, qs_ref, ks_ref, cd_ref)]
        _, vjp = jax.vjp(_ret_chains, qc, kc, vc, *consts, ss)
        dq, dk, dv, ddm, dqs, dks, dcd, ds = vjp((dos, [ds_scr[i] for i in range(N_CHAIN)]))
        for i in range(N_CHAIN):
            d, h = i // 4, i % 4
            hs = slice(h * 128, (h + 1) * 128)
            (dqf, dqr)[d][:, hs] = dq[i]
            (dkf, dkr)[d][:, hs] = dk[i]
            (dvf, dvr)[d][:, hs] = dv[i]
            ddm_ref[i] += ddm[i]
            dqs_ref[i] += dqs[i]
            dks_ref[i] += dks[i]
            dcd_ref[i] += dcd[i]
            ds_scr[i] = ds[i]

    tok = jax.ShapeDtypeStruct((ROWS, 512), F32)
    return pl.pallas_call(
        body,
        grid=(N_CHUNK,),
        in_specs=_both(sp, ["tok", "tok", "tokv"]) + _ret_const_specs() + _both(sp, ["state", "tok"]),
        out_specs=_both(sp, ["tok", "tok", "tok"]) + _ret_const_specs(),
        out_shape=[tok] * 6 + [jax.ShapeDtypeStruct((N_CHAIN, CHUNK, CHUNK), F32), jax.ShapeDtypeStruct((N_CHAIN, CHUNK, 1), F32),
                               jax.ShapeDtypeStruct((N_CHAIN, CHUNK, 1), F32), jax.ShapeDtypeStruct((N_CHAIN, 1, 1), F32)],
        scratch_shapes=[pltpu.VMEM((N_CHAIN, 128, 128), F32)],
        compiler_params=_cparams(("arbitrary",)),
        name="ret_bwd",
    )(q, q, k, k, v, v, dm, qs, ks, cd, *ssave, do, do)


N_QBLK = ROWS // B_BLOCK
CTX_QBLK = CTX_LEN // B_BLOCK


def _attn_head(q, kc, vc, kw, vw, sink, valid):
    scale = B_HD ** -0.5
    s_c = _bdot(q, kc, 1, 1) * scale
    s_w = jnp.where(valid, _bdot(q, kw, 1, 1) * scale, NEG)
    m = jnp.maximum(jnp.maximum(jnp.max(s_c, axis=-1, keepdims=True), sink), jnp.max(s_w, axis=-1, keepdims=True))
    m = lax.stop_gradient(m)
    e_c = jnp.exp(s_c - m)
    e_w = jnp.exp(s_w - m)
    den = jnp.sum(e_c, axis=-1, keepdims=True) + jnp.sum(e_w, axis=-1, keepdims=True) + jnp.exp(sink - m)
    return (_bdot(e_c, vc, 1, 0) + _bdot(e_w, vw, 1, 0)) / den


def _window(blk):
    xblk = blk - CTX_QBLK
    first = jnp.clip((xblk - 1) * B_BLOCK, 0, SEQ - 3 * B_BLOCK)
    qpos = xblk * B_BLOCK + lax.broadcasted_iota(jnp.int32, (B_BLOCK, 3 * B_BLOCK), 0)
    kpos = first + lax.broadcasted_iota(jnp.int32, (B_BLOCK, 3 * B_BLOCK), 1)
    far = jnp.where(blk >= CTX_QBLK, 0, 2 * SEQ)
    valid = jnp.abs(kpos - qpos) + far <= WINDOW
    return pl.multiple_of(first + CTX_LEN, B_BLOCK), valid


def _attn_specs():
    qspec = pl.BlockSpec((B_BLOCK, 512), lambda i: (i, 0))
    kvspec = pl.BlockSpec((ROWS, 256), lambda i: (0, 0))
    return qspec, kvspec, pl.BlockSpec(memory_space=pltpu.SMEM)


def _attn_fwd_call(q, kv, sink):
    def body(q_ref, kv_ref, sink_ref, o_ref):
        start, valid = _window(pl.program_id(0))
        for hk in range(B_KV_HEADS):
            ks = slice(hk * B_HD, (hk + 1) * B_HD)
            vs = slice(128 + hk * B_HD, 128 + (hk + 1) * B_HD)
            kc, vc = kv_ref[0:CTX_LEN, ks], kv_ref[0:CTX_LEN, vs]
            kw, vw = kv_ref[pl.ds(start, 3 * B_BLOCK), ks], kv_ref[pl.ds(start, 3 * B_BLOCK), vs]
            for g in range(4):
                h = hk * 4 + g
                hs = slice(h * B_HD, (h + 1) * B_HD)
                o_ref[:, hs] = _attn_head(q_ref[:, hs], kc, vc, kw, vw, jnp.full((1, 1), sink_ref[h], F32), valid)

    qspec, kvspec, sspec = _attn_specs()
    return pl.pallas_call(
        body,
        grid=(N_QBLK,),
        in_specs=[qspec, kvspec, sspec],
        out_specs=qspec,
        out_shape=jax.ShapeDtypeStruct((ROWS, 512), F32),
        compiler_params=_cparams(("arbitrary",)),
        name="attn_fwd",
    )(q, kv, sink)


def _attn_bwd_call(q, kv, sink, do):
    def body(q_ref, kv_ref, sink_ref, do_ref, dq_ref, dkv_ref, dsink_ref):
        @pl.when(pl.program_id(0) == 0)
        def _():
            dkv_ref[...] = jnp.zeros_like(dkv_ref)
            dsink_ref[...] = jnp.zeros_like(dsink_ref)

        start, valid = _window(pl.program_id(0))
        for hk in range(B_KV_HEADS):
            ks = slice(hk * B_HD, (hk + 1) * B_HD)
            vs = slice(128 + hk * B_HD, 128 + (hk + 1) * B_HD)
            kc, vc = kv_ref[0:CTX_LEN, ks], kv_ref[0:CTX_LEN, vs]
            kw, vw = kv_ref[pl.ds(start, 3 * B_BLOCK), ks], kv_ref[pl.ds(start, 3 * B_BLOCK), vs]
            dkc = jnp.zeros((CTX_LEN, B_HD), F32)
            dvc = jnp.zeros((CTX_LEN, B_HD), F32)
            dkw = jnp.zeros((3 * B_BLOCK, B_HD), F32)
            dvw = jnp.zeros((3 * B_BLOCK, B_HD), F32)
            for g in range(4):
                h = hk * 4 + g
                hs = slice(h * B_HD, (h + 1) * B_HD)
                fn = functools.partial(_attn_head, valid=valid)
                _, vjp = jax.vjp(fn, q_ref[:, hs], kc, vc, kw, vw, jnp.full((1, 1), sink_ref[h], F32))
                dq, dkc_h, dvc_h, dkw_h, dvw_h, dsink = vjp(do_ref[:, hs])
                dq_ref[:, hs] = dq
                dkc, dvc, dkw, dvw = dkc + dkc_h, dvc + dvc_h, dkw + dkw_h, dvw + dvw_h
                dsink_ref[h:h + 1, :] += jnp.broadcast_to(dsink, (1, 128))
            dkv_ref[0:CTX_LEN, ks] += dkc
            dkv_ref[0:CTX_LEN, vs] += dvc
            dkv_ref[pl.ds(start, 3 * B_BLOCK), ks] += dkw
            dkv_ref[pl.ds(start, 3 * B_BLOCK), vs] += dvw

    qspec, kvspec, sspec = _attn_specs()
    return pl.pallas_call(
        body,
        grid=(N_QBLK,),
        in_specs=[qspec, kvspec, sspec, qspec],
        out_specs=[qspec, kvspec, pl.BlockSpec((8, 128), lambda i: (0, 0))],
        out_shape=[jax.ShapeDtypeStruct((ROWS, 512), F32), jax.ShapeDtypeStruct((ROWS, 256), F32),
                   jax.ShapeDtypeStruct((8, 128), F32)],
        compiler_params=_cparams(("arbitrary",)),
        name="attn_bwd",
    )(q, kv, sink, do)


def _my_id():
    return 4 * lax.axis_index("x") + 2 * lax.axis_index("y") + lax.axis_index("c")


def _peer(k):
    x, y, c = lax.axis_index("x"), lax.axis_index("y"), lax.axis_index("c")
    return (1 - x if k & 4 else x, 1 - y if k & 2 else y, 1 - c if k & 1 else c)


def _exchange(arrays, gather, name):
    n = len(arrays)

    def body(*refs):
        ins, outs = refs[:n], refs[n:2 * n]
        send_sems, recv_sems, local_sems = refs[2 * n:]
        me = _my_id()
        own, sent = [], []
        for a in range(n):
            cp = pltpu.make_async_copy(ins[a] if gather else ins[a].at[me], outs[a].at[me], local_sems.at[a])
            cp.start()
            own.append(cp)
            for k in range(1, N_DEV):
                src = ins[a] if gather else ins[a].at[jnp.bitwise_xor(me, k)]
                cp = pltpu.make_async_remote_copy(src_ref=src, dst_ref=outs[a].at[me],
                                                  send_sem=send_sems.at[a, k - 1], recv_sem=recv_sems.at[a, k - 1],
                                                  device_id=_peer(k), device_id_type=MESH)
                cp.start()
                sent.append(cp)
        for a in range(n):
            for k in range(1, N_DEV):
                src = ins[a] if gather else ins[a].at[jnp.bitwise_xor(me, k)]
                arrive = pltpu.make_async_remote_copy(src_ref=src, dst_ref=outs[a].at[jnp.bitwise_xor(me, k)],
                                                      send_sem=send_sems.at[a, k - 1],
                                                      recv_sem=recv_sems.at[a, k - 1],
                                                      device_id=_peer(k), device_id_type=MESH)
                arrive.wait_recv()
        for cp in sent:
            cp.wait_send()
        for cp in own:
            cp.wait()

    hbm = pl.BlockSpec(memory_space=pltpu.HBM)
    out_shape = [jax.ShapeDtypeStruct((N_DEV,) + (a.shape if gather else a.shape[1:]), a.dtype) for a in arrays]
    return pl.pallas_call(
        body,
        in_specs=[hbm] * n,
        out_specs=[hbm] * n,
        out_shape=out_shape,
        scratch_shapes=[pltpu.SemaphoreType.DMA((n, N_DEV - 1)), pltpu.SemaphoreType.DMA((n, N_DEV - 1)),
                        pltpu.SemaphoreType.DMA((n,))],
        compiler_params=pltpu.CompilerParams(has_side_effects=True),
        name=name,
    )(*arrays)


def _adamw(w, m, v, contrib, name):
    r, c = w.shape
    br = _pick(r, (256, 128, 64, 32, 16, 8))
    bc1 = 1.0 - ADAM_B1 ** ADAM_STEP
    bc2 = 1.0 - ADAM_B2 ** ADAM_STEP

    def body(w_ref, m_ref, v_ref, c_ref, g_ref, d_ref, nm_ref, nv_ref):
        g = c_ref[0].astype(F32)
        for j in range(1, N_DEV):
            g = g + c_ref[j].astype(F32)
        m_new = ADAM_B1 * m_ref[...] + (1.0 - ADAM_B1) * g
        v_new = ADAM_B2 * v_ref[...] + (1.0 - ADAM_B2) * (g * g)
        m_hat = m_new / bc1
        v_hat = v_new / bc2
        g_ref[...] = g
        d_ref[...] = -ADAM_LR * (m_hat / (jnp.sqrt(v_hat) + ADAM_EPS) + ADAM_WD * w_ref[...])
        nm_ref[...] = m_new
        nv_ref[...] = v_new

    spec = pl.BlockSpec((br, c), lambda i: (i, 0))
    cspec = pl.BlockSpec((N_DEV, br, c), lambda i: (0, i, 0))
    return pl.pallas_call(
        body,
        grid=(r // br,),
        in_specs=[spec, spec, spec, cspec],
        out_specs=[spec] * 4,
        out_shape=[jax.ShapeDtypeStruct((r, c), F32)] * 4,
        compiler_params=_cparams(("parallel",)),
        name=name,
    )(w, m, v, contrib)


def _silu(x):
    return x * jax.nn.sigmoid(x)


def _rope_angles(pos, n_freq):
    inv = ROPE_BASE ** (-jnp.arange(n_freq, dtype=F32) / n_freq)
    return pos[:, None] * inv[None, :]


def _with_ctx_rows(cos, sin):
    return (jnp.concatenate([jnp.ones((CTX_LEN, 128), F32), cos], axis=0),
            jnp.concatenate([jnp.zeros((CTX_LEN, 128), F32), sin], axis=0))


def _rope_tables():
    rows_n = SEQ // GRID_W
    rows = jnp.repeat(jnp.arange(rows_n, dtype=F32), GRID_W)
    cols = jnp.tile(jnp.arange(GRID_W, dtype=F32), rows_n)
    ang_r = _rope_angles(rows, B_HD // 4)
    ang_c = _rope_angles(cols, B_HD // 4)
    cos_b = jnp.tile(jnp.concatenate([jnp.cos(ang_r)] * 2 + [jnp.cos(ang_c)] * 2, axis=1), (1, 2))
    sin_b = jnp.tile(jnp.concatenate([-jnp.sin(ang_r), jnp.sin(ang_r), -jnp.sin(ang_c), jnp.sin(ang_c)], axis=1), (1, 2))
    ang = _rope_angles(jnp.arange(SEQ, dtype=F32), C_HD // 2)
    cos_c = jnp.concatenate([jnp.cos(ang)] * 2, axis=1)
    sin_c = jnp.concatenate([-jnp.sin(ang), jnp.sin(ang)], axis=1)
    return _with_ctx_rows(cos_b, sin_b), _with_ctx_rows(cos_c, sin_c)


def _halves(a):
    return a[:4], a[4:]


def _delta_gates(ab, a_log, dt_bias):
    beta = jax.nn.sigmoid(ab[:, :8])
    g = -jnp.exp(a_log)[None, :] * jax.nn.softplus(ab[:, 8:] + dt_bias[None, :])
    gch = g.reshape(N_CHUNK, CHUNK, 8)
    fwd = jnp.cumsum(gch[..., :4], axis=1)
    bwd = jnp.flip(jnp.cumsum(jnp.flip(gch[..., 4:], axis=1), axis=1), axis=1)
    gc = jnp.concatenate([fwd, bwd], axis=-1)
    gl = jnp.sum(gch, axis=1)
    return (_halves(beta.T[:, :, None]), _halves(gc.reshape(ROWS, 8).T[:, :, None]),
            _halves(gc.transpose(2, 0, 1)[:, :, None, :]), _halves(gl.T[:, :, None, None]))


def _ret_consts(c_decay):
    lg = jax.nn.log_sigmoid(c_decay)
    idx = jnp.arange(CHUNK, dtype=F32)
    diff = idx[:, None] - idx[None, :]
    lgf, lgb = lg[:4, None, None], lg[4:, None, None]
    dm = jnp.concatenate([jnp.exp(jnp.where(diff >= 0, diff * lgf, -jnp.inf)),
                          jnp.exp(jnp.where(diff <= 0, -diff * lgb, -jnp.inf))], axis=0)
    qs = jnp.concatenate([jnp.exp((idx + 1.0)[None, :] * lg[:4, None]),
                          jnp.exp((CHUNK - idx)[None, :] * lg[4:, None])], axis=0)[:, :, None]
    ks = jnp.concatenate([jnp.exp((CHUNK - 1.0 - idx)[None, :] * lg[:4, None]),
                          jnp.exp(idx[None, :] * lg[4:, None])], axis=0)[:, :, None]
    return dm, qs, ks, jnp.exp(CHUNK * lg)[:, None, None]


A_PIECES = ((0, True, A_DK ** -0.5, "a_q"), (1, True, 1.0, "a_k"), (2, False, 1.0, "a_v"))
B_ROPE_COLS = [C_BQ // 512, C_BKV // 256, 0, 0]
C_ROPE_COLS = [C_CQ // 512, C_CK // 512, 0, 0]
MERGE_COLS = [C_MERGE // 1024, C_MERGE // 1024 + 1, C_MERGE // 1024 + 2, 0, 0, 0]


def _conv8(conv_w):
    return jnp.pad(conv_w, ((0, 8 - A_CONV), (0, 0)))


def _core_forward(proj, p):
    (cos_b, sin_b), (cos_c, sin_c) = _rope_tables()
    conv8 = _conv8(p["a_conv_w"])
    q, k, v = [_a_prep_fwd(proj, conv8, col, nrm, scl, nm) for col, nrm, scl, nm in A_PIECES]
    gates = _delta_gates(proj[:, C_AB:C_AB + 16], p["a_log"], p["a_dt_bias"])
    of, orv, ssf, ssr, tsf, tsr = _delta_fwd_call(q, k, v, *gates)
    (y_a,) = _a_out.fwd([(of, orv), proj], [p["a_norm_w"][None, :]], [0, C_AZ // 512])

    qb, kvb = _b_rope.fwd([proj, proj, cos_b, sin_b], [], B_ROPE_COLS)
    ob = _attn_fwd_call(qb, kvb, p["b_sink"])
    (y_b,) = _b_out.fwd([ob, proj], [], [0, C_BZ // 512])

    qc, kc = _c_rope.fwd([proj, proj, cos_c, sin_c], [], C_ROPE_COLS)
    cf, cr, csf, csr = _ret_fwd_call(qc, kc, proj, C_CV // 512, *_ret_consts(p["c_decay"]))
    (y_c,) = _c_out.fwd([(cf, cr), proj], [p["c_norm_w"][None, :]], [0, C_CZ // 512])

    wb = p["w_branch"].astype(BF16)
    pa, pb, pc = [_matmul(y, wb[i], "branch_" + "abc"[i]) for i, y in enumerate((y_a, y_b, y_c))]
    (merged,) = _merge.fwd([proj, proj, proj, pa, pb, pc], [], MERGE_COLS)
    saved = dict(q=q, k=k, v=v, of=of, orv=orv, ss=(ssf, ssr), ts=(tsf, tsr), qb=qb, kvb=kvb, ob=ob,
                 qc=qc, kc=kc, cf=cf, cr=cr, cs=(csf, csr), y=(y_a, y_b, y_c), pabc=(pa, pb, pc))
    return merged, saved


@jax.custom_vjp
def _layer_core(proj, p):
    return _core_forward(proj, p)[0]


def _layer_core_fwd(proj, p):
    merged, saved = _core_forward(proj, p)
    return merged, (proj, p, saved)


def _layer_core_bwd(res, dmerged):
    proj, p, s = res
    (cos_b, sin_b), (cos_c, sin_c) = _rope_tables()
    conv8 = _conv8(p["a_conv_w"])
    wb = p["w_branch"].astype(BF16)
    y_a, y_b, y_c = s["y"]

    (dma, dmb, dmc, dpa, dpb, dpc), _ = _merge.bwd([proj, proj, proj, *s["pabc"]], [], [dmerged], MERGE_COLS)
    dy = [_matmul(d, wb[i], "branch_%s_da" % "abc"[i], "nt") for i, d in enumerate((dpa, dpb, dpc))]
    dwb = jnp.stack([_matmul(y, d, "branch_%s_db" % "abc"[i], "tn")
                     for i, (y, d) in enumerate(zip((y_a, y_b, y_c), (dpa, dpb, dpc)))])

    consts, consts_vjp = jax.vjp(_ret_consts, p["c_decay"])
    (do_c, dcz), (dcnw,) = _c_out.bwd([(s["cf"], s["cr"]), proj], [p["c_norm_w"][None, :]], [dy[2]], [0, C_CZ // 512])
    g = _ret_bwd_call(s["qc"], s["kc"], proj, C_CV // 512, *consts, s["cs"], do_c)
    (dcq, dck), _ = _c_rope.bwd([proj, proj, cos_c, sin_c], [], [(g[0], g[1]), (g[2], g[3])], C_ROPE_COLS)
    dcv = g[4] + g[5]
    (dc_decay,) = consts_vjp(tuple(g[6:10]))

    (dob, dbz), _ = _b_out.bwd([s["ob"], proj], [], [dy[1]], [0, C_BZ // 512])
    dqb, dkvb, dsink = _attn_bwd_call(s["qb"], s["kvb"], p["b_sink"], dob)
    (dbq, dbkv), _ = _b_rope.bwd([proj, proj, cos_b, sin_b], [], [dqb, dkvb], B_ROPE_COLS)

    ab = proj[:, C_AB:C_AB + 16]
    gates, gates_vjp = jax.vjp(_delta_gates, ab, p["a_log"], p["a_dt_bias"])
    (do_a, daz), (danw,) = _a_out.bwd([(s["of"], s["orv"]), proj], [p["a_norm_w"][None, :]], [dy[0]], [0, C_AZ // 512])
    g = _delta_bwd_call(s["q"], s["k"], s["v"], *gates, s["ss"], s["ts"], do_a)
    dgates = ((g[6], g[7]), (g[8], g[9]), (g[10], g[11]), (g[12], g[13]))
    dab, da_log, ddt = gates_vjp(dgates)
    dpre, dconv = [], []
    for (col, nrm, scl, nm), df, dr in zip(A_PIECES, (g[0], g[2], g[4]), (g[1], g[3], g[5])):
        dx, dw = _a_prep_bwd(proj, conv8, col, nrm, scl, df, dr, nm)
        dpre.append(dx)
        dconv.append(dw[:A_CONV])

    dproj = jnp.concatenate(dpre + [daz, dbq, dbz, dcq, dck, dcv, dcz, dma, dmb, dmc, dbkv,
                                    jnp.pad(dab, ((0, 0), (0, IN_PAD - C_AB - 16)))], axis=1)
    dp = dict(a_conv_w=jnp.concatenate(dconv, axis=1), a_log=da_log, a_dt_bias=ddt, a_norm_w=danw[0],
              b_sink=dsink[:, 0], c_decay=dc_decay, c_norm_w=dcnw[0], w_branch=dwb)
    return dproj, dp


_layer_core.defvjp(_layer_core_fwd, _layer_core_bwd)
CORE_PARAMS = ("a_conv_w", "a_log", "a_dt_bias", "a_norm_w", "b_sink", "c_decay", "c_norm_w", "w_branch")


def _pad_w_in(w):
    return jnp.concatenate([w[:, 0:2048], w[:, 2064:2576], w[:, 2832:3344], w[:, 3344:8464], w[:, 2576:2832],
                            w[:, 2048:2064], jnp.zeros((D_MODEL, IN_PAD - IN_WIDTH), w.dtype)], axis=1)


def _local_loss(wts, x, c, ctx, loss_target):
    sc16 = jnp.zeros((16, D_MODEL), F32).at[0].set(_silu(c)).at[1].set(_silu(wts["c_ctx"]))
    xs = jnp.concatenate([ctx, x], axis=0)
    for layer in range(DEPTH):
        last = layer == DEPTH - 1
        mod16 = _mm(sc16, wts["w_ada"][layer], "ada%d" % layer) + wts["b_ada"][layer][None, :]
        mod_cx = jnp.stack([mod16[1], mod16[0]])
        shift, scale, gate = jnp.split(mod_cx, 3, axis=1)
        (h,) = _norm_mod.call([xs], [wts["norm_w"][layer][None, :], shift, scale])
        proj = _mm(h, _pad_w_in(wts["w_in"][layer]), "w_in%d" % layer)
        merged = _layer_core(proj, {n: wts[n][layer] for n in CORE_PARAMS})
        if last:
            merged, xs, gate = merged[CTX_LEN:], xs[CTX_LEN:], gate[1:2]
        out = _mm(merged, wts["w_out"][layer], "w_out%d" % layer)
        (xs,) = _residual.call([xs, out], [gate])
    (per_row,) = _loss_rows.call([xs, loss_target], [wts["final_norm_w"][None, :]])
    return jnp.sum(per_row[:, 0])


SHARDED = ("w_ada", "w_in", "a_conv_w", "w_branch", "w_out")
MATMUL_ONLY = ("w_ada", "w_in", "w_branch", "w_out")
SMALL = ("c_ctx", "b_ada", "norm_w", "a_log", "a_dt_bias", "a_norm_w", "b_sink", "c_decay", "c_norm_w",
         "final_norm_w")
WEIGHTS = ("c_ctx", "w_ada", "b_ada", "norm_w", "w_in", "a_conv_w", "a_log", "a_dt_bias", "a_norm_w", "b_sink",
           "c_decay", "c_norm_w", "w_branch", "w_out", "final_norm_w")
SMALL_PACK = 12288


def _wire(name, a):
    return a.astype(BF16) if name in MATMUL_ONLY else a


def _unshard(name, g):
    if name == "w_branch":
        return g.transpose(1, 2, 3, 0, 4).reshape(DEPTH, 3, BR_WIDTH, D_MODEL)
    if name == "w_out":
        return g.transpose(1, 0, 2, 3).reshape(DEPTH, D_MODEL, D_MODEL)
    s = g.shape
    return g.transpose(1, 2, 0, 3).reshape(s[1], s[2], N_DEV * s[3])


def _reshard(name, w):
    if name == "w_branch":
        return w.reshape(DEPTH, 3, BR_WIDTH, N_DEV, D_MODEL // N_DEV).transpose(3, 0, 1, 2, 4)
    if name == "w_out":
        return w.reshape(DEPTH, N_DEV, D_MODEL // N_DEV, D_MODEL).transpose(1, 0, 2, 3)
    s = w.shape
    return w.reshape(s[0], s[1], N_DEV, s[2] // N_DEV).transpose(2, 0, 1, 3)


def _pack_small(tree):
    flat = jnp.concatenate([tree[n].reshape(-1) for n in SMALL])
    return jnp.pad(flat, (0, SMALL_PACK - flat.shape[0])).reshape(SMALL_PACK // 128, 128)


def _unpack_small(packed, like):
    flat = packed.reshape(-1)
    out, off = {}, 0
    for n in SMALL:
        size = math.prod(like[n].shape)
        out[n] = flat[off:off + size].reshape(like[n].shape)
        off += size
    return out


def kernel(x, c, ctx, c_ctx, w_ada, b_ada, norm_w, w_in, a_conv_w, a_log, a_dt_bias, a_norm_w, b_sink, c_decay, c_norm_w, w_branch, w_out, final_norm_w, loss_target, m_c_ctx, m_w_ada, m_b_ada, m_norm_w, m_w_in, m_a_conv_w, m_a_log, m_a_dt_bias, m_a_norm_w, m_b_sink, m_c_decay, m_c_norm_w, m_w_branch, m_w_out, m_final_norm_w, v_c_ctx, v_w_ada, v_b_ada, v_norm_w, v_w_in, v_a_conv_w, v_a_log, v_a_dt_bias, v_a_norm_w, v_b_sink, v_c_decay, v_c_norm_w, v_w_branch, v_w_out, v_final_norm_w):
    w = dict(c_ctx=c_ctx, w_ada=w_ada, b_ada=b_ada, norm_w=norm_w, w_in=w_in, a_conv_w=a_conv_w, a_log=a_log,
             a_dt_bias=a_dt_bias, a_norm_w=a_norm_w, b_sink=b_sink, c_decay=c_decay, c_norm_w=c_norm_w,
             w_branch=w_branch, w_out=w_out, final_norm_w=final_norm_w)
    m = dict(c_ctx=m_c_ctx, w_ada=m_w_ada, b_ada=m_b_ada, norm_w=m_norm_w, w_in=m_w_in, a_conv_w=m_a_conv_w,
             a_log=m_a_log, a_dt_bias=m_a_dt_bias, a_norm_w=m_a_norm_w, b_sink=m_b_sink, c_decay=m_c_decay,
             c_norm_w=m_c_norm_w, w_branch=m_w_branch, w_out=m_w_out, final_norm_w=m_final_norm_w)
    v = dict(c_ctx=v_c_ctx, w_ada=v_w_ada, b_ada=v_b_ada, norm_w=v_norm_w, w_in=v_w_in, a_conv_w=v_a_conv_w,
             a_log=v_a_log, a_dt_bias=v_a_dt_bias, a_norm_w=v_a_norm_w, b_sink=v_b_sink, c_decay=v_c_decay,
             c_norm_w=v_c_norm_w, w_branch=v_w_branch, w_out=v_w_out, final_norm_w=v_final_norm_w)

    gathered = _exchange([_wire(n, w[n]) for n in SHARDED], True, "gather_weights")
    full = dict(w)
    for n, g in zip(SHARDED, gathered):
        full[n] = _unshard(n, g).astype(F32)

    loss, (gw, gx) = jax.value_and_grad(_local_loss, argnums=(0, 1))(full, x[0], c[0], ctx[0], loss_target[0])
    loss = lax.psum(loss, ("x", "y", "c"))

    blocks = _exchange([_wire(n, _reshard(n, gw[n])) for n in SHARDED], False, "scatter_grads")
    small = _exchange([_pack_small(gw)], True, "gather_small_grads")[0]

    grad, delta, new_m, new_v = {}, {}, {}, {}
    for n, contrib in zip(SHARDED, blocks):
        shp = w[n].shape
        two_d = (math.prod(shp[:-1]), shp[-1])
        outs = _adamw(w[n].reshape(two_d), m[n].reshape(two_d), v[n].reshape(two_d),
                      contrib.reshape((N_DEV,) + two_d), "adamw_" + n)
        grad[n], delta[n], new_m[n], new_v[n] = [o.reshape(shp) for o in outs]
    outs = _adamw(_pack_small(w), _pack_small(m), _pack_small(v), small, "adamw_small")
    for tree, packed in zip((grad, delta, new_m, new_v), outs):
        tree.update(_unpack_small(packed, w))

    return (loss, gx[None], *[grad[n] for n in WEIGHTS], *[delta[n] for n in WEIGHTS],
            *[new_m[n] for n in WEIGHTS], *[new_v[n] for n in WEIGHTS])
```

```python
import functools
import math

import jax
import jax.numpy as jnp
from jax import lax
from jax.experimental import pallas as pl
from jax.experimental.pallas import tpu as pltpu

F32 = jnp.float32
BF16 = jnp.bfloat16
INV_PRECISION = lax.Precision.HIGH

D_MODEL = 1024
SEQ = 4096
DEPTH = 2
GRID_W = 64
CTX_LEN = 256
EPS = 1e-6
ROPE_BASE = 10000.0
BR_WIDTH = D_MODEL // 2
A_DK = 128
A_HEADS = 4
A_WIDTH = 512
A_CONV = 5
B_HD = 64
B_Q_HEADS = 8
B_KV_HEADS = 2
WINDOW = 128
B_BLOCK = 128
C_HD = 128
C_HEADS = 4
C_WIDTH = 512
CHUNK = 64
ADAM_LR = 0.001
ADAM_B1 = 0.9
ADAM_B2 = 0.999
ADAM_EPS = 1e-08
ADAM_WD = 0.01
ADAM_STEP = 10

N_DEV = 8
ROWS = CTX_LEN + SEQ
N_CHUNK = ROWS // CHUNK
N_CTX_CHUNK = CTX_LEN // CHUNK
IN_WIDTH = 8464
IN_PAD = 8704
NEG = -1e30

VMEM_LIMIT = 48 * 1024 * 1024
MESH = pl.DeviceIdType.MESH

C_AQ, C_AK, C_AV, C_AZ, C_BQ, C_BZ, C_CQ, C_CK, C_CV, C_CZ = (i * 512 for i in range(10))
C_MERGE = 5120
C_BKV = 8192
C_AB = 8448


def _cparams(sem=None):
    if sem is None:
        return pltpu.CompilerParams(vmem_limit_bytes=VMEM_LIMIT)
    return pltpu.CompilerParams(dimension_semantics=sem, vmem_limit_bytes=VMEM_LIMIT)


def _dg(a, b, ca, cb, prec=None):
    return lax.dot_general(a, b, (((ca,), (cb,)), ((), ())), preferred_element_type=F32, precision=prec)


@functools.partial(jax.custom_vjp, nondiff_argnums=(2, 3))
def _bdot(a, b, ca, cb):
    return _dg(a.astype(BF16), b.astype(BF16), ca, cb)


def _bdot_fwd(a, b, ca, cb):
    return _bdot(a, b, ca, cb), (a, b)


def _bdot_bwd(ca, cb, res, ct):
    a, b = res
    da = _bdot(ct, b, 1, 1 - cb) if ca == 1 else _bdot(b, ct, 1 - cb, 1)
    db = _bdot(a, ct, 1 - ca, 0) if cb == 0 else _bdot(ct, a, 0, 1 - ca)
    return da, db


_bdot.defvjp(_bdot_fwd, _bdot_bwd)


def _hdot(a, b):
    return _dg(a, b, 1, 0, INV_PRECISION)


def _k_silu(x):
    return x / (1.0 + jnp.exp(-x))


def _k_sigmoid(x):
    return 1.0 / (1.0 + jnp.exp(-x))


@jax.custom_vjp
def _swap64(x):
    return pltpu.roll(x, 64, 1)


_swap64.defvjp(lambda x: (pltpu.roll(x, 64, 1), None), lambda _, ct: (pltpu.roll(ct, 64, 1),))


def _swap16_impl(x):
    lane = lax.broadcasted_iota(jnp.int32, x.shape, 1)
    return jnp.where((lane & 16) == 0, pltpu.roll(x, 112, 1), pltpu.roll(x, 16, 1))


@jax.custom_vjp
def _swap16(x):
    return _swap16_impl(x)


_swap16.defvjp(lambda x: (_swap16_impl(x), None), lambda _, ct: (_swap16_impl(ct),))


def _pick(dim, prefs):
    for p in prefs:
        if dim % p == 0:
            return p
    return dim


def _matmul(a, b, name, mode="nn"):
    ca, cb = {"nn": (1, 0), "nt": (1, 1), "tn": (0, 0)}[mode]
    m, k = a.shape[1 - ca], a.shape[ca]
    n = b.shape[1 - cb]
    tm = _pick(m, (1088, 1024, 512, 256, 128))
    tn = _pick(n, (512, 256, 128))
    tk = _pick(k, (1088, 1024, 512, 256, 128) if mode == "tn" else (2176, 2048, 1024, 512, 256, 128))
    nk = k // tk
    a_spec = (pl.BlockSpec((tm, tk), lambda i, j, kk: (i, kk)) if ca == 1
              else pl.BlockSpec((tk, tm), lambda i, j, kk: (kk, i)))
    b_spec = (pl.BlockSpec((tk, tn), lambda i, j, kk: (kk, j)) if cb == 0
              else pl.BlockSpec((tn, tk), lambda i, j, kk: (j, kk)))

    def body(a_ref, b_ref, o_ref, acc_ref):
        part = _dg(a_ref[...].astype(BF16), b_ref[...].astype(BF16), ca, cb)
        if nk == 1:
            o_ref[...] = part
        else:
            kk = pl.program_id(2)

            @pl.when(kk == 0)
            def _():
                acc_ref[...] = part

            @pl.when(kk > 0)
            def _():
                acc_ref[...] += part

            @pl.when(kk == nk - 1)
            def _():
                o_ref[...] = acc_ref[...]

    return pl.pallas_call(
        body,
        grid=(m // tm, n // tn, nk),
        in_specs=[a_spec, b_spec],
        out_specs=pl.BlockSpec((tm, tn), lambda i, j, kk: (i, j)),
        out_shape=jax.ShapeDtypeStruct((m, n), F32),
        scratch_shapes=[pltpu.VMEM((tm, tn), F32)],
        compiler_params=_cparams(("parallel", "parallel", "arbitrary")),
        name=name,
    )(a, b)


@functools.partial(jax.custom_vjp, nondiff_argnums=(2,))
def _mm(a, b, name):
    return _matmul(a, b, name)


def _mm_fwd(a, b, name):
    b16 = b.astype(BF16)
    return _matmul(a, b16, name), (a, b16)


def _mm_bwd(name, res, ct):
    a, b16 = res
    da = _matmul(ct, b16, name + "_da", "nt")
    db = _matmul(a, ct, name + "_db", "tn")
    return da, db


_mm.defvjp(_mm_fwd, _mm_bwd)


ROW_BLOCK = 256
ROW_VMEM_BUDGET = 16 * 1024 * 1024


def _pieces(val, pw):
    return [val[:, j * pw:(j + 1) * pw] for j in range(val.shape[1] // pw)]


def _flat(groups):
    arrays, sizes = [], []
    for g in groups:
        g = g if isinstance(g, (tuple, list)) else (g,)
        arrays += list(g)
        sizes.append(len(g))
    return arrays, sizes


def _regroup(refs, sizes):
    out, at = [], 0
    for n in sizes:
        val = refs[at][...]
        for r in refs[at + 1:at + n]:
            val = val + r[...]
        out.append(val)
        at += n
    return out


class _Rowwise:
    def __init__(self, fn, name, row_wpw, par_pw, out_wpw, n_diff=None):
        self.fn, self.name, self.row_wpw, self.par_pw, self.out_wpw = fn, name, row_wpw, par_pw, out_wpw
        self.n_diff = len(row_wpw) if n_diff is None else n_diff

        @jax.custom_vjp
        def call(rows, params):
            return self.fwd(rows, params)

        def call_fwd(rows, params):
            return self.fwd(rows, params), (rows, params)

        def call_bwd(res, douts):
            return self.bwd(res[0], res[1], douts)

        call.defvjp(call_fwd, call_bwd)
        self.call = call

    def _load(self, row_vals, par_refs, br, with_ctx):
        row = pl.program_id(0) * br + lax.broadcasted_iota(jnp.int32, (br, 1), 0)
        is_ctx = (row < (CTX_LEN if with_ctx else 0)).astype(F32)
        rows = [_pieces(v, pw) for v, (_, pw) in zip(row_vals, self.row_wpw)]
        pars = []
        for p, pw in zip(par_refs, self.par_pw):
            val = p[...] if p.shape[0] == 1 else is_ctx * p[0:1, :] + (1.0 - is_ctx) * p[1:2, :]
            pars.append(_pieces(val, pw))
        return rows, pars, is_ctx

    def _block_rows(self, n_rows, widths):
        for br in (1088, 1024, 544, 512, 272):
            if n_rows % br == 0 and 2 * 4 * br * sum(widths) <= ROW_VMEM_BUDGET:
                return br
        return ROW_BLOCK

    def _row_specs(self, br, sizes, cols):
        out = []
        for (w, _), n, c in zip(self.row_wpw, sizes, cols):
            out += [pl.BlockSpec((br, w), lambda i, c=c: (i, c))] * n
        return out

    def fwd(self, rows, params, cols=None):
        arrays, sizes = _flat(rows)
        cols = cols or [0] * len(rows)
        n_rows = arrays[0].shape[0]
        n_in = len(arrays)
        br = self._block_rows(n_rows, [w for (w, _), n in zip(self.row_wpw, sizes) for _ in range(n)]
                              + [w for w, _ in self.out_wpw])

        def body(*refs):
            r, p, _ = self._load(_regroup(refs[:n_in], sizes), refs[n_in:n_in + len(params)], br, n_rows == ROWS)
            for o_ref, pieces, (_, pw) in zip(refs[n_in + len(params):], self.fn(r, p), self.out_wpw):
                for j, piece in enumerate(pieces):
                    o_ref[:, j * pw:(j + 1) * pw] = piece

        return pl.pallas_call(
            body,
            grid=(n_rows // br,),
            in_specs=self._row_specs(br, sizes, cols) + [pl.BlockSpec(p.shape, lambda i: (0, 0)) for p in params],
            out_specs=[pl.BlockSpec((br, w), lambda i: (i, 0)) for w, _ in self.out_wpw],
            out_shape=[jax.ShapeDtypeStruct((n_rows, w), F32) for w, _ in self.out_wpw],
            compiler_params=_cparams(("parallel",)),
            name=self.name + "_fwd",
        )(*arrays, *params)

    def bwd(self, rows, params, douts, cols=None):
        arrays, sizes = _flat(rows)
        darrays, dsizes = _flat(douts)
        cols = cols or [0] * len(rows)
        n_rows = arrays[0].shape[0]
        n_in, n_par, n_dout, n_diff = len(arrays), len(params), len(darrays), self.n_diff
        br = self._block_rows(n_rows, [w for (w, _), n in zip(self.row_wpw, sizes) for _ in range(n)]
                              + [w for (w, _), n in zip(self.out_wpw, dsizes) for _ in range(n)]
                              + [w for w, _ in self.row_wpw[:n_diff]])

        def body(*refs):
            par_refs = refs[n_in:n_in + n_par]
            dout_refs = refs[n_in + n_par:n_in + n_par + n_dout]
            drow_refs = refs[n_in + n_par + n_dout:n_in + n_par + n_dout + n_diff]
            dpar_refs = refs[n_in + n_par + n_dout + n_diff:]

            @pl.when(pl.program_id(0) == 0)
            def _():
                for d in dpar_refs:
                    d[...] = jnp.zeros_like(d)

            r, p, is_ctx = self._load(_regroup(refs[:n_in], sizes), par_refs, br, n_rows == ROWS)
            cts = [_pieces(d, pw) for d, (_, pw) in zip(_regroup(dout_refs, dsizes), self.out_wpw)]
            fixed = r[n_diff:]
            _, vjp = jax.vjp(lambda rd, pp: self.fn(rd + fixed, pp), r[:n_diff], p)
            dr, dp = vjp(cts)
            for d_ref, pieces, (_, pw) in zip(drow_refs, dr, self.row_wpw):
                for j, piece in enumerate(pieces):
                    d_ref[:, j * pw:(j + 1) * pw] = piece
            for d_ref, pieces, pw in zip(dpar_refs, dp, self.par_pw):
                for j, piece in enumerate(pieces):
                    lanes = slice(j * pw, (j + 1) * pw)
                    if d_ref.shape[0] == 1:
                        d_ref[:, lanes] += piece
                    else:
                        d_ref[0:1, lanes] += jnp.sum(is_ctx * piece, axis=0, keepdims=True)
                        d_ref[1:2, lanes] += jnp.sum((1.0 - is_ctx) * piece, axis=0, keepdims=True)

        par_specs = [pl.BlockSpec(p.shape, lambda i: (0, 0)) for p in params]
        dout_specs = []
        for (w, _), n in zip(self.out_wpw, dsizes):
            dout_specs += [pl.BlockSpec((br, w), lambda i: (i, 0))] * n
        drow_w = [w for w, _ in self.row_wpw[:n_diff]]
        g = pl.pallas_call(
            body,
            grid=(n_rows // br,),
            in_specs=self._row_specs(br, sizes, cols) + par_specs + dout_specs,
            out_specs=[pl.BlockSpec((br, w), lambda i: (i, 0)) for w in drow_w] + par_specs,
            out_shape=[jax.ShapeDtypeStruct((n_rows, w), F32) for w in drow_w]
            + [jax.ShapeDtypeStruct(p.shape, F32) for p in params],
            compiler_params=_cparams(("arbitrary",)),
            name=self.name + "_bwd",
        )(*arrays, *params, *darrays)
        return list(g[:n_diff]), list(g[n_diff:])


def _fn_norm_mod(rows, pars):
    (x,), (nw,), (shift,), (scale,) = rows[0], pars[0], pars[1], pars[2]
    y = x * lax.rsqrt(jnp.mean(x * x, axis=-1, keepdims=True) + EPS) * nw
    return [[y * (1.0 + scale) + shift]]


def _fn_head_rms_gate(rows, pars):
    (w,) = pars[0]
    return [[o * lax.rsqrt(jnp.mean(o * o, axis=-1, keepdims=True) + EPS) * w * _k_silu(z)
             for o, z in zip(rows[0], rows[1])]]


def _fn_group_norm_gate(rows, pars):
    out = []
    for o, z, w in zip(rows[0], rows[1], pars[0]):
        mu = jnp.mean(o, axis=-1, keepdims=True)
        var = jnp.mean(jnp.square(o - mu), axis=-1, keepdims=True)
        out.append((o - mu) * lax.rsqrt(var + EPS) * w * _k_silu(z))
    return [out]


def _fn_gate(rows, pars):
    return [[o * _k_silu(z) for o, z in zip(rows[0], rows[1])]]


def _fn_merge(rows, pars):
    (ma,), (mb,), (mc,), (pa,), (pb,), (pc,) = rows
    return [[_k_sigmoid(ma) * pa + _k_sigmoid(mb) * pb + _k_sigmoid(mc) * pc]]


def _fn_residual(rows, pars):
    (res,), (out,), (gate,) = rows[0], rows[1], pars[0]
    return [[res + gate * out]]


def _fn_loss(rows, pars):
    (x,), (target,), (w,) = rows[0], rows[1], pars[0]
    y = x * lax.rsqrt(jnp.mean(x * x, axis=-1, keepdims=True) + EPS) * w
    per_row = 0.5 * jnp.mean(jnp.square(y - target), axis=-1, keepdims=True)
    return [[jnp.broadcast_to(per_row, (per_row.shape[0], 128))]]


def _fn_b_rope(rows, pars):
    q, (k, v), (cos,), (sin,) = rows
    rot = lambda x: x * cos + _swap16(x) * sin
    return [[rot(x) for x in q], [rot(k), v]]


def _fn_c_rope(rows, pars):
    q, k, (cos,), (sin,) = rows
    rot = lambda x: x * cos + _swap64(x) * sin
    return [[rot(x) for x in q], [rot(x) * (C_HD ** -0.5) for x in k]]


_norm_mod = _Rowwise(_fn_norm_mod, "norm_mod", [(D_MODEL, D_MODEL)], [D_MODEL] * 3, [(D_MODEL, D_MODEL)])
_residual = _Rowwise(_fn_residual, "residual", [(D_MODEL, D_MODEL)] * 2, [D_MODEL], [(D_MODEL, D_MODEL)])
_loss_rows = _Rowwise(_fn_loss, "loss", [(D_MODEL, D_MODEL)] * 2, [D_MODEL], [(128, 128)])
_a_out = _Rowwise(_fn_head_rms_gate, "a_out", [(512, 128)] * 2, [128], [(512, 128)])
_c_out = _Rowwise(_fn_group_norm_gate, "c_out", [(512, 128)] * 2, [128], [(512, 128)])
_b_out = _Rowwise(_fn_gate, "b_out", [(512, 512)] * 2, [], [(512, 512)])
_merge = _Rowwise(_fn_merge, "merge", [(D_MODEL, D_MODEL)] * 6, [], [(D_MODEL, D_MODEL)])
_b_rope = _Rowwise(_fn_b_rope, "b_rope", [(512, 128), (256, 128), (128, 128), (128, 128)], [],
                   [(512, 128), (256, 128)], n_diff=2)
_c_rope = _Rowwise(_fn_c_rope, "c_rope", [(512, 128), (512, 128), (128, 128), (128, 128)], [],
                   [(512, 128), (512, 128)], n_diff=2)


HALO = 8
EXT = ROW_BLOCK + 2 * HALO


def _halo_specs(col, width=512):
    last = ROWS // HALO - 1
    per = ROW_BLOCK // HALO
    prev = pl.BlockSpec((HALO, width), lambda i: (jnp.maximum(i * per - 1, 0), col))
    cur = pl.BlockSpec((ROW_BLOCK, width), lambda i: (i, col))
    nxt = pl.BlockSpec((HALO, width), lambda i: (jnp.minimum((i + 1) * per, last), col))
    return [prev, cur, nxt]


def _extended(prev_ref, cur_ref, next_ref):
    i = pl.program_id(0)
    prev_ok = i >= 2
    next_ok = jnp.logical_and(i >= 1, i < ROWS // ROW_BLOCK - 1)
    return jnp.concatenate([jnp.where(prev_ok, prev_ref[...], 0.0), cur_ref[...],
                            jnp.where(next_ok, next_ref[...], 0.0)], axis=0)


def _conv_taps(x_ext, w_ref, flip):
    acc = None
    for j in range(A_CONV):
        shift = (j - 2) if flip else (2 - j)
        term = w_ref[j:j + 1, :] * pltpu.roll(x_ext, shift % EXT, 0)
        acc = term if acc is None else acc + term
    return acc


def _conv_post(pre_pieces, normalize, scale):
    out = []
    for p in pre_pieces:
        y = _k_silu(p)
        if normalize:
            y = y * lax.rsqrt(jnp.sum(y * y, axis=-1, keepdims=True) + EPS) * scale
        out.append(y)
    return out


def _a_prep_fwd(proj, conv8, col, normalize, scale, name):
    def body(prev_ref, cur_ref, next_ref, w_ref, o_ref):
        pre = _conv_taps(_extended(prev_ref, cur_ref, next_ref), w_ref, False)[HALO:HALO + ROW_BLOCK]
        for h, y in enumerate(_conv_post(_pieces(pre, 128), normalize, scale)):
            o_ref[:, h * 128:(h + 1) * 128] = y

    return pl.pallas_call(
        body,
        grid=(ROWS // ROW_BLOCK,),
        in_specs=_halo_specs(col) + [pl.BlockSpec((8, 512), lambda i: (0, col))],
        out_specs=pl.BlockSpec((ROW_BLOCK, 512), lambda i: (i, 0)),
        out_shape=jax.ShapeDtypeStruct((ROWS, 512), F32),
        compiler_params=_cparams(("parallel",)),
        name=name + "_fwd",
    )(proj, proj, proj, conv8)


def _a_prep_bwd(proj, conv8, col, normalize, scale, dout_f, dout_r, name):
    def body(xp, xc, xn, w_ref, fp, fc, fn_, rp, rc, rn, dx_ref, dw_ref):
        @pl.when(pl.program_id(0) == 0)
        def _():
            dw_ref[...] = jnp.zeros_like(dw_ref)

        x_ext = _extended(xp, xc, xn)
        dout = _extended(fp, fc, fn_) + _extended(rp, rc, rn)
        pre = _conv_taps(x_ext, w_ref, False)
        _, vjp = jax.vjp(lambda p: _conv_post(p, normalize, scale), _pieces(pre, 128))
        (dpre,) = vjp(_pieces(dout, 128))
        dpre = jnp.concatenate(dpre, axis=1)
        dx_ref[...] = _conv_taps(dpre, w_ref, True)[HALO:HALO + ROW_BLOCK]
        own = dpre[HALO:HALO + ROW_BLOCK]
        for j in range(A_CONV):
            shifted = pltpu.roll(x_ext, (2 - j) % EXT, 0)[HALO:HALO + ROW_BLOCK]
            dw_ref[j:j + 1, :] += jnp.sum(own * shifted, axis=0, keepdims=True)

    return pl.pallas_call(
        body,
        grid=(ROWS // ROW_BLOCK,),
        in_specs=_halo_specs(col) + [pl.BlockSpec((8, 512), lambda i: (0, col))] + _halo_specs(0) + _halo_specs(0),
        out_specs=[pl.BlockSpec((ROW_BLOCK, 512), lambda i: (i, 0)), pl.BlockSpec((8, 512), lambda i: (0, 0))],
        out_shape=[jax.ShapeDtypeStruct((ROWS, 512), F32), jax.ShapeDtypeStruct((8, 512), F32)],
        compiler_params=_cparams(("arbitrary",)),
        name=name + "_bwd",
    )(proj, proj, proj, conv8, dout_f, dout_f, dout_f, dout_r, dout_r, dout_r)


N_CHAIN = 8


def _rev_chunk(s):
    return jnp.where(s < N_CTX_CHUNK, N_CTX_CHUNK - 1 - s, N_CHUNK + N_CTX_CHUNK - 1 - s)


def _scan_specs(step_of, v_col=0):
    cf = step_of
    cr = lambda n: _rev_chunk(step_of(n))

    def pair(shape, index):
        return (pl.BlockSpec(shape, lambda n: index(cf(n))), pl.BlockSpec(shape, lambda n: index(cr(n))))

    return dict(
        tok=pair((CHUNK, 512), lambda c: (c, 0)),
        tokv=pair((CHUNK, 512), lambda c: (c, v_col)),
        col=pair((4, CHUNK, 1), lambda c: (0, c, 0)),
        row=pair((4, 1, 1, CHUNK), lambda c: (0, c, 0, 0)),
        one=pair((4, 1, 1, 1), lambda c: (0, c, 0, 0)),
        state=pair((None, 4, 128, 128), lambda c: (c, 0, 0, 0)),
        tinv=pair((None, 4, CHUNK, CHUNK), lambda c: (c, 0, 0, 0)),
    )


def _both(specs, kinds):
    out = []
    for kind in kinds:
        out += list(specs[kind])
    return out


def _chain_masks():
    ii = lax.broadcasted_iota(jnp.int32, (CHUNK, CHUNK), 0)
    jj = lax.broadcasted_iota(jnp.int32, (CHUNK, CHUNK), 1)
    eye = jnp.where(ii == jj, 1.0, 0.0).astype(F32)
    lower = (ii >= jj, ii > jj)
    upper = (ii <= jj, ii < jj)
    return [lower] * 4 + [upper] * 4, eye


def _tri_inv_all(ls, eye):
    xs = [eye - l for l in ls]
    ps = [_hdot(l, l) for l in ls]
    for i in range(5):
        xs = [x + _hdot(x, p) for x, p in zip(xs, ps)]
        if i < 4:
            ps = [_hdot(p, p) for p in ps]
    return xs


@jax.custom_vjp
def _inv_saved(l, x):
    return x


def _inv_saved_fwd(l, x):
    return x, x


def _inv_saved_bwd(x, dx):
    return -_bdot(x, _bdot(dx, x, 1, 1), 0, 0), jnp.zeros_like(x)


_inv_saved.defvjp(_inv_saved_fwd, _inv_saved_bwd)


def _delta_chains(q, k, v, beta_r, gcr, gl, s, masks, eye, tinv_saved):
    n = range(len(q))
    beta = [jnp.sum(eye * beta_r[i], axis=1, keepdims=True) for i in n]
    gcc = [jnp.sum(eye * gcr[i], axis=1, keepdims=True) for i in n]
    decay = [jnp.exp(jnp.where(masks[i][0], gcc[i] - gcr[i], NEG)) for i in n]
    kb = [k[i] * beta[i] for i in n]
    lmat = [jnp.where(masks[i][1], _bdot(kb[i], k[i], 1, 1) * decay[i], 0.0) for i in n]
    if tinv_saved is None:
        tinv = _tri_inv_all(lmat, eye)
    else:
        tinv = [_inv_saved(lmat[i], tinv_saved[i]) for i in n]
    eg = [jnp.exp(gcc[i]) for i in n]
    u = [_bdot(tinv[i], v[i] * beta[i], 1, 0) for i in n]
    w = [_bdot(tinv[i], kb[i] * eg[i], 1, 0) for i in n]
    qk = [_bdot(q[i], k[i], 1, 1) * decay[i] for i in n]
    v_new = [u[i] - _bdot(w[i], s[i], 1, 0) for i in n]
    o = [_bdot(q[i] * eg[i], s[i], 1, 0) + _bdot(qk[i], v_new[i], 1, 0) for i in n]
    s_new = [s[i] * jnp.exp(gl[i]) + _bdot(k[i] * jnp.exp(gl[i] - gcc[i]), v_new[i], 0, 0) for i in n]
    return (o, s_new), tinv


def _chain_loads(tok_pairs, small_pairs):
    toks = [[pair[i // 4][:, (i % 4) * 128:(i % 4 + 1) * 128] for i in range(N_CHAIN)] for pair in tok_pairs]
    smalls = [[pair[i // 4][i % 4] for i in range(N_CHAIN)] for pair in small_pairs]
    return toks, smalls


def _delta_fwd_call(q, k, v, beta, gc, gl):
    sp = _scan_specs(lambda n: n)

    def body(qf, qr, kf, kr, vf, vr, bf, br, gcrf, gcrr, glf, glr, of, orv, ssf, ssr, tsf, tsr, s_scr):
        @pl.when(pl.program_id(0) == 0)
        def _():
            s_scr[...] = jnp.zeros_like(s_scr)

        masks, eye = _chain_masks()
        (qs, ks, vs), _ = _chain_loads([(qf, qr), (kf, kr), (vf, vr)], [])
        bs = [(bf, br)[i // 4][i % 4, 0] for i in range(N_CHAIN)]
        gcrs = [(gcrf, gcrr)[i // 4][i % 4, 0] for i in range(N_CHAIN)]
        gls = [(glf, glr)[i // 4][i % 4, 0] for i in range(N_CHAIN)]
        ss = [s_scr[i] for i in range(N_CHAIN)]
        (o, s_new), tinv = _delta_chains(qs, ks, vs, bs, gcrs, gls, ss, masks, eye, None)
        for i in range(N_CHAIN):
            d, h = i // 4, i % 4
            (ssf, ssr)[d][h] = ss[i]
            (tsf, tsr)[d][h] = tinv[i]
            (of, orv)[d][:, h * 128:(h + 1) * 128] = o[i]
            s_scr[i] = s_new[i]

    return pl.pallas_call(
        body,
        grid=(N_CHUNK,),
        in_specs=_both(sp, ["tok", "tok", "tok", "row", "row", "one"]),
        out_specs=_both(sp, ["tok", "state", "tinv"]),
        out_shape=[jax.ShapeDtypeStruct((ROWS, 512), F32)] * 2
        + [jax.ShapeDtypeStruct((N_CHUNK, 4, 128, 128), F32)] * 2
        + [jax.ShapeDtypeStruct((N_CHUNK, 4, CHUNK, CHUNK), F32)] * 2,
        scratch_shapes=[pltpu.VMEM((N_CHAIN, 128, 128), F32)],
        compiler_params=_cparams(("arbitrary",)),
        name="delta_fwd",
    )(q, q, k, k, v, v, *beta, *gc, *gl)


def _delta_bwd_call(q, k, v, beta, gc, gl, ssave, tsave, do):
    sp = _scan_specs(lambda n: N_CHUNK - 1 - n)

    def body(qf, qr, kf, kr, vf, vr, bf, br, gcrf, gcrr, glf, glr, ssf, ssr, tsf, tsr, dof, dor,
             dqf, dqr, dkf, dkr, dvf, dvr, dbf, dbr, dgcrf, dgcrr, dglf, dglr, ds_scr):
        @pl.when(pl.program_id(0) == 0)
        def _():
            ds_scr[...] = jnp.zeros_like(ds_scr)

        masks, eye = _chain_masks()
        (qs, ks, vs, dos), (ss, ts) = _chain_loads(
            [(qf, qr), (kf, kr), (vf, vr), (dof, dor)], [(ssf, ssr), (tsf, tsr)])
        bs = [(bf, br)[i // 4][i % 4, 0] for i in range(N_CHAIN)]
        gcrs = [(gcrf, gcrr)[i // 4][i % 4, 0] for i in range(N_CHAIN)]
        gls = [(glf, glr)[i // 4][i % 4, 0] for i in range(N_CHAIN)]
        fn = lambda *a: _delta_chains(*a, masks, eye, ts)
        _, vjp, _ = jax.vjp(fn, qs, ks, vs, bs, gcrs, gls, ss, has_aux=True)
        dq, dk, dv, db, dgcr, dgl, ds = vjp((dos, [ds_scr[i] for i in range(N_CHAIN)]))
        for i in range(N_CHAIN):
            d, h = i // 4, i % 4
            hs = slice(h * 128, (h + 1) * 128)
            (dqf, dqr)[d][:, hs] = dq[i]
            (dkf, dkr)[d][:, hs] = dk[i]
            (dvf, dvr)[d][:, hs] = dv[i]
            (dbf, dbr)[d][h, 0] = db[i]
            (dgcrf, dgcrr)[d][h, 0] = dgcr[i]
            (dglf, dglr)[d][h, 0] = dgl[i]
            ds_scr[i] = ds[i]

    tok = jax.ShapeDtypeStruct((ROWS, 512), F32)
    return pl.pallas_call(
        body,
        grid=(N_CHUNK,),
        in_specs=_both(sp, ["tok", "tok", "tok", "row", "row", "one", "state", "tinv", "tok"]),
        out_specs=_both(sp, ["tok", "tok", "tok", "row", "row", "one"]),
        out_shape=[tok] * 6 + [jax.ShapeDtypeStruct((4, N_CHUNK, 1, CHUNK), F32)] * 4
        + [jax.ShapeDtypeStruct((4, N_CHUNK, 1, 1), F32)] * 2,
        scratch_shapes=[pltpu.VMEM((N_CHAIN, 128, 128), F32)],
        compiler_params=_cparams(("arbitrary",)),
        name="delta_bwd",
    )(q, q, k, k, v, v, *beta, *gc, *gl, *ssave, *tsave, do, do)


def _ret_chains(q, k, v, dm, qs, ks, cd, s):
    n = range(len(q))
    a = [_bdot(q[i], k[i], 1, 1) * dm[i] for i in n]
    o = [_bdot(a[i], v[i], 1, 0) + _bdot(q[i] * qs[i], s[i], 1, 0) for i in n]
    s_new = [s[i] * cd[i] + _bdot(k[i] * ks[i], v[i], 0, 0) for i in n]
    return o, s_new


def _ret_const_specs():
    return [pl.BlockSpec((N_CHAIN, CHUNK, CHUNK), lambda n: (0, 0, 0)), pl.BlockSpec((N_CHAIN, CHUNK, 1), lambda n: (0, 0, 0)),
            pl.BlockSpec((N_CHAIN, CHUNK, 1), lambda n: (0, 0, 0)), pl.BlockSpec((N_CHAIN, 1, 1), lambda n: (0, 0, 0))]


def _ret_fwd_call(q, k, v, v_col, dm, qs, ks, cd):
    sp = _scan_specs(lambda n: n, v_col)

    def body(qf, qr, kf, kr, vf, vr, dm_ref, qs_ref, ks_ref, cd_ref, of, orv, ssf, ssr, s_scr):
        @pl.when(pl.program_id(0) == 0)
        def _():
            s_scr[...] = jnp.zeros_like(s_scr)

        (qc, kc, vc), _ = _chain_loads([(qf, qr), (kf, kr), (vf, vr)], [])
        ss = [s_scr[i] for i in range(N_CHAIN)]
        consts = [[r[i] for i in range(N_CHAIN)] for r in (dm_ref, qs_ref, ks_ref, cd_ref)]
        o, s_new = _ret_chains(qc, kc, vc, *consts, ss)
        for i in range(N_CHAIN):
            d, h = i // 4, i % 4
            (ssf, ssr)[d][h] = ss[i]
            (of, orv)[d][:, h * 128:(h + 1) * 128] = o[i]
            s_scr[i] = s_new[i]

    return pl.pallas_call(
        body,
        grid=(N_CHUNK,),
        in_specs=_both(sp, ["tok", "tok", "tokv"]) + _ret_const_specs(),
        out_specs=_both(sp, ["tok", "state"]),
        out_shape=[jax.ShapeDtypeStruct((ROWS, 512), F32)] * 2 + [jax.ShapeDtypeStruct((N_CHUNK, 4, 128, 128), F32)] * 2,
        scratch_shapes=[pltpu.VMEM((N_CHAIN, 128, 128), F32)],
        compiler_params=_cparams(("arbitrary",)),
        name="ret_fwd",
    )(q, q, k, k, v, v, dm, qs, ks, cd)


def _ret_bwd_call(q, k, v, v_col, dm, qs, ks, cd, ssave, do):
    sp = _scan_specs(lambda n: N_CHUNK - 1 - n, v_col)

    def body(qf, qr, kf, kr, vf, vr, dm_ref, qs_ref, ks_ref, cd_ref, ssf, ssr, dof, dor,
             dqf, dqr, dkf, dkr, dvf, dvr, ddm_ref, dqs_ref, dks_ref, dcd_ref, ds_scr):
        @pl.when(pl.program_id(0) == 0)
        def _():
            ds_scr[...] = jnp.zeros_like(ds_scr)
            ddm_ref[...] = jnp.zeros_like(ddm_ref)
            dqs_ref[...] = jnp.zeros_like(dqs_ref)
            dks_ref[...] = jnp.zeros_like(dks_ref)
            dcd_ref[...] = jnp.zeros_like(dcd_ref)

        (qc, kc, vc, dos), (ss,) = _chain_loads([(qf, qr), (kf, kr), (vf, vr), (dof, dor)], [(ssf, ssr)])
        consts = [[r[i] for i in range(N_CHAIN)] for r in (dm_ref, qs_ref, ks_ref, cd_ref)]
        _, vjp = jax.vjp(_ret_chains, qc, kc, vc, *consts, ss)
        dq, dk, dv, ddm, dqs, dks, dcd, ds = vjp((dos, [ds_scr[i] for i in range(N_CHAIN)]))
        for i in range(N_CHAIN):
            d, h = i // 4, i % 4
            hs = slice(h * 128, (h + 1) * 128)
            (dqf, dqr)[d][:, hs] = dq[i]
            (dkf, dkr)[d][:, hs] = dk[i]
            (dvf, dvr)[d][:, hs] = dv[i]
            ddm_ref[i] += ddm[i]
            dqs_ref[i] += dqs[i]
            dks_ref[i] += dks[i]
            dcd_ref[i] += dcd[i]
            ds_scr[i] = ds[i]

    tok = jax.ShapeDtypeStruct((ROWS, 512), F32)
    return pl.pallas_call(
        body,
        grid=(N_CHUNK,),
        in_specs=_both(sp, ["tok", "tok", "tokv"]) + _ret_const_specs() + _both(sp, ["state", "tok"]),
        out_specs=_both(sp, ["tok", "tok", "tok"]) + _ret_const_specs(),
        out_shape=[tok] * 6 + [jax.ShapeDtypeStruct((N_CHAIN, CHUNK, CHUNK), F32), jax.ShapeDtypeStruct((N_CHAIN, CHUNK, 1), F32),
                               jax.ShapeDtypeStruct((N_CHAIN, CHUNK, 1), F32), jax.ShapeDtypeStruct((N_CHAIN, 1, 1), F32)],
        scratch_shapes=[pltpu.VMEM((N_CHAIN, 128, 128), F32)],
        compiler_params=_cparams(("arbitrary",)),
        name="ret_bwd",
    )(q, q, k, k, v, v, dm, qs, ks, cd, *ssave, do, do)


N_QBLK = ROWS // B_BLOCK
CTX_QBLK = CTX_LEN // B_BLOCK


def _attn_heads(q, kc, vc, kw, vw, sink, valid):
    n = range(len(q))
    scale = B_HD ** -0.5
    s_c = [_bdot(q[i], kc[i], 1, 1) * scale for i in n]
    s_w = [jnp.where(valid, _bdot(q[i], kw[i], 1, 1) * scale, NEG) for i in n]
    m = [lax.stop_gradient(jnp.maximum(jnp.maximum(jnp.max(s_c[i], axis=-1, keepdims=True), sink[i]),
                                       jnp.max(s_w[i], axis=-1, keepdims=True))) for i in n]
    e_c = [jnp.exp(s_c[i] - m[i]) for i in n]
    e_w = [jnp.exp(s_w[i] - m[i]) for i in n]
    den = [jnp.sum(e_c[i], axis=-1, keepdims=True) + jnp.sum(e_w[i], axis=-1, keepdims=True)
           + jnp.exp(sink[i] - m[i]) for i in n]
    return [(_bdot(e_c[i], vc[i], 1, 0) + _bdot(e_w[i], vw[i], 1, 0)) / den[i] for i in n]


def _attn_loads(q_ref, kv_ref, sink_ref, start):
    q, kc, vc, kw, vw, sink = [], [], [], [], [], []
    for hk in range(B_KV_HEADS):
        ks = slice(hk * B_HD, (hk + 1) * B_HD)
        vs = slice(128 + hk * B_HD, 128 + (hk + 1) * B_HD)
        grp = (kv_ref[0:CTX_LEN, ks], kv_ref[0:CTX_LEN, vs],
               kv_ref[pl.ds(start, 3 * B_BLOCK), ks], kv_ref[pl.ds(start, 3 * B_BLOCK), vs])
        for g in range(4):
            h = hk * 4 + g
            q.append(q_ref[:, h * B_HD:(h + 1) * B_HD])
            for lst, val in zip((kc, vc, kw, vw), grp):
                lst.append(val)
            sink.append(jnp.full((1, 1), sink_ref[h], F32))
    return q, kc, vc, kw, vw, sink


def _window(blk):
    xblk = blk - CTX_QBLK
    first = jnp.clip((xblk - 1) * B_BLOCK, 0, SEQ - 3 * B_BLOCK)
    qpos = xblk * B_BLOCK + lax.broadcasted_iota(jnp.int32, (B_BLOCK, 3 * B_BLOCK), 0)
    kpos = first + lax.broadcasted_iota(jnp.int32, (B_BLOCK, 3 * B_BLOCK), 1)
    far = jnp.where(blk >= CTX_QBLK, 0, 2 * SEQ)
    valid = jnp.abs(kpos - qpos) + far <= WINDOW
    return pl.multiple_of(first + CTX_LEN, B_BLOCK), valid


def _attn_specs():
    qspec = pl.BlockSpec((B_BLOCK, 512), lambda i: (i, 0))
    kvspec = pl.BlockSpec((ROWS, 256), lambda i: (0, 0))
    return qspec, kvspec, pl.BlockSpec(memory_space=pltpu.SMEM)


def _attn_fwd_call(q, kv, sink):
    def body(q_ref, kv_ref, sink_ref, o_ref):
        start, valid = _window(pl.program_id(0))
        out = _attn_heads(*_attn_loads(q_ref, kv_ref, sink_ref, start), valid)
        for h in range(B_Q_HEADS):
            o_ref[:, h * B_HD:(h + 1) * B_HD] = out[h]

    qspec, kvspec, sspec = _attn_specs()
    return pl.pallas_call(
        body,
        grid=(N_QBLK,),
        in_specs=[qspec, kvspec, sspec],
        out_specs=qspec,
        out_shape=jax.ShapeDtypeStruct((ROWS, 512), F32),
        compiler_params=_cparams(("arbitrary",)),
        name="attn_fwd",
    )(q, kv, sink)


def _attn_bwd_call(q, kv, sink, do):
    def body(q_ref, kv_ref, sink_ref, do_ref, dq_ref, dkv_ref, dsink_ref):
        @pl.when(pl.program_id(0) == 0)
        def _():
            dkv_ref[...] = jnp.zeros_like(dkv_ref)
            dsink_ref[...] = jnp.zeros_like(dsink_ref)

        start, valid = _window(pl.program_id(0))
        _, vjp = jax.vjp(functools.partial(_attn_heads, valid=valid), *_attn_loads(q_ref, kv_ref, sink_ref, start))
        dq, dkc, dvc, dkw, dvw, dsink = vjp([do_ref[:, h * B_HD:(h + 1) * B_HD] for h in range(B_Q_HEADS)])
        for h in range(B_Q_HEADS):
            dq_ref[:, h * B_HD:(h + 1) * B_HD] = dq[h]
            dsink_ref[h:h + 1, :] += jnp.broadcast_to(dsink[h], (1, 128))
        for hk in range(B_KV_HEADS):
            ks = slice(hk * B_HD, (hk + 1) * B_HD)
            vs = slice(128 + hk * B_HD, 128 + (hk + 1) * B_HD)
            grp = lambda parts: parts[hk * 4] + parts[hk * 4 + 1] + parts[hk * 4 + 2] + parts[hk * 4 + 3]
            dkv_ref[0:CTX_LEN, ks] += grp(dkc)
            dkv_ref[0:CTX_LEN, vs] += grp(dvc)
            dkv_ref[pl.ds(start, 3 * B_BLOCK), ks] += grp(dkw)
            dkv_ref[pl.ds(start, 3 * B_BLOCK), vs] += grp(dvw)

    qspec, kvspec, sspec = _attn_specs()
    return pl.pallas_call(
        body,
        grid=(N_QBLK,),
        in_specs=[qspec, kvspec, sspec, qspec],
        out_specs=[qspec, kvspec, pl.BlockSpec((8, 128), lambda i: (0, 0))],
        out_shape=[jax.ShapeDtypeStruct((ROWS, 512), F32), jax.ShapeDtypeStruct((ROWS, 256), F32),
                   jax.ShapeDtypeStruct((8, 128), F32)],
        compiler_params=_cparams(("arbitrary",)),
        name="attn_bwd",
    )(q, kv, sink, do)


def _my_id():
    return 4 * lax.axis_index("x") + 2 * lax.axis_index("y") + lax.axis_index("c")


def _peer(k):
    x, y, c = lax.axis_index("x"), lax.axis_index("y"), lax.axis_index("c")
    return (1 - x if k & 4 else x, 1 - y if k & 2 else y, 1 - c if k & 1 else c)


def _exchange(arrays, gather, name):
    n = len(arrays)

    def body(*refs):
        ins, outs = refs[:n], refs[n:2 * n]
        send_sems, recv_sems, local_sems = refs[2 * n:]
        me = _my_id()
        own, sent = [], []
        for a in range(n):
            cp = pltpu.make_async_copy(ins[a] if gather else ins[a].at[me], outs[a].at[me], local_sems.at[a])
            cp.start()
            own.append(cp)
            for k in range(1, N_DEV):
                src = ins[a] if gather else ins[a].at[jnp.bitwise_xor(me, k)]
                cp = pltpu.make_async_remote_copy(src_ref=src, dst_ref=outs[a].at[me],
                                                  send_sem=send_sems.at[a, k - 1], recv_sem=recv_sems.at[a, k - 1],
                                                  device_id=_peer(k), device_id_type=MESH)
                cp.start()
                sent.append(cp)
        for a in range(n):
            for k in range(1, N_DEV):
                src = ins[a] if gather else ins[a].at[jnp.bitwise_xor(me, k)]
                arrive = pltpu.make_async_remote_copy(src_ref=src, dst_ref=outs[a].at[jnp.bitwise_xor(me, k)],
                                                      send_sem=send_sems.at[a, k - 1],
                                                      recv_sem=recv_sems.at[a, k - 1],
                                                      device_id=_peer(k), device_id_type=MESH)
                arrive.wait_recv()
        for cp in sent:
            cp.wait_send()
        for cp in own:
            cp.wait()

    hbm = pl.BlockSpec(memory_space=pltpu.HBM)
    out_shape = [jax.ShapeDtypeStruct((N_DEV,) + (a.shape if gather else a.shape[1:]), a.dtype) for a in arrays]
    return pl.pallas_call(
        body,
        in_specs=[hbm] * n,
        out_specs=[hbm] * n,
        out_shape=out_shape,
        scratch_shapes=[pltpu.SemaphoreType.DMA((n, N_DEV - 1)), pltpu.SemaphoreType.DMA((n, N_DEV - 1)),
                        pltpu.SemaphoreType.DMA((n,))],
        compiler_params=pltpu.CompilerParams(has_side_effects=True),
        name=name,
    )(*arrays)


def _adamw(w, m, v, contrib, name):
    r, c = w.shape
    br = _pick(r, (256, 128, 64, 32, 16, 8))
    bc1 = 1.0 - ADAM_B1 ** ADAM_STEP
    bc2 = 1.0 - ADAM_B2 ** ADAM_STEP

    def body(w_ref, m_ref, v_ref, c_ref, g_ref, d_ref, nm_ref, nv_ref):
        g = c_ref[0].astype(F32)
        for j in range(1, N_DEV):
            g = g + c_ref[j].astype(F32)
        m_new = ADAM_B1 * m_ref[...] + (1.0 - ADAM_B1) * g
        v_new = ADAM_B2 * v_ref[...] + (1.0 - ADAM_B2) * (g * g)
        m_hat = m_new / bc1
        v_hat = v_new / bc2
        g_ref[...] = g
        d_ref[...] = -ADAM_LR * (m_hat / (jnp.sqrt(v_hat) + ADAM_EPS) + ADAM_WD * w_ref[...])
        nm_ref[...] = m_new
        nv_ref[...] = v_new

    spec = pl.BlockSpec((br, c), lambda i: (i, 0))
    cspec = pl.BlockSpec((N_DEV, br, c), lambda i: (0, i, 0))
    return pl.pallas_call(
        body,
        grid=(r // br,),
        in_specs=[spec, spec, spec, cspec],
        out_specs=[spec] * 4,
        out_shape=[jax.ShapeDtypeStruct((r, c), F32)] * 4,
        compiler_params=_cparams(("parallel",)),
        name=name,
    )(w, m, v, contrib)


def _silu(x):
    return x * jax.nn.sigmoid(x)


def _rope_angles(pos, n_freq):
    inv = ROPE_BASE ** (-jnp.arange(n_freq, dtype=F32) / n_freq)
    return pos[:, None] * inv[None, :]


def _with_ctx_rows(cos, sin):
    return (jnp.concatenate([jnp.ones((CTX_LEN, 128), F32), cos], axis=0),
            jnp.concatenate([jnp.zeros((CTX_LEN, 128), F32), sin], axis=0))


def _rope_tables():
    rows_n = SEQ // GRID_W
    rows = jnp.repeat(jnp.arange(rows_n, dtype=F32), GRID_W)
    cols = jnp.tile(jnp.arange(GRID_W, dtype=F32), rows_n)
    ang_r = _rope_angles(rows, B_HD // 4)
    ang_c = _rope_angles(cols, B_HD // 4)
    cos_b = jnp.tile(jnp.concatenate([jnp.cos(ang_r)] * 2 + [jnp.cos(ang_c)] * 2, axis=1), (1, 2))
    sin_b = jnp.tile(jnp.concatenate([-jnp.sin(ang_r), jnp.sin(ang_r), -jnp.sin(ang_c), jnp.sin(ang_c)], axis=1), (1, 2))
    ang = _rope_angles(jnp.arange(SEQ, dtype=F32), C_HD // 2)
    cos_c = jnp.concatenate([jnp.cos(ang)] * 2, axis=1)
    sin_c = jnp.concatenate([-jnp.sin(ang), jnp.sin(ang)], axis=1)
    return _with_ctx_rows(cos_b, sin_b), _with_ctx_rows(cos_c, sin_c)


def _halves(a):
    return a[:4], a[4:]


def _delta_gates(ab, a_log, dt_bias):
    beta = jax.nn.sigmoid(ab[:, :8])
    g = -jnp.exp(a_log)[None, :] * jax.nn.softplus(ab[:, 8:] + dt_bias[None, :])
    gch = g.reshape(N_CHUNK, CHUNK, 8)
    fwd = jnp.cumsum(gch[..., :4], axis=1)
    bwd = jnp.flip(jnp.cumsum(jnp.flip(gch[..., 4:], axis=1), axis=1), axis=1)
    gc = jnp.concatenate([fwd, bwd], axis=-1)
    gl = jnp.sum(gch, axis=1)
    rows = lambda a: _halves(a.transpose(2, 0, 1)[:, :, None, :])
    return rows(beta.reshape(N_CHUNK, CHUNK, 8)), rows(gc), _halves(gl.T[:, :, None, None])


def _ret_consts(c_decay):
    lg = jax.nn.log_sigmoid(c_decay)
    idx = jnp.arange(CHUNK, dtype=F32)
    diff = idx[:, None] - idx[None, :]
    lgf, lgb = lg[:4, None, None], lg[4:, None, None]
    dm = jnp.concatenate([jnp.exp(jnp.where(diff >= 0, diff * lgf, -jnp.inf)),
                          jnp.exp(jnp.where(diff <= 0, -diff * lgb, -jnp.inf))], axis=0)
    qs = jnp.concatenate([jnp.exp((idx + 1.0)[None, :] * lg[:4, None]),
                          jnp.exp((CHUNK - idx)[None, :] * lg[4:, None])], axis=0)[:, :, None]
    ks = jnp.concatenate([jnp.exp((CHUNK - 1.0 - idx)[None, :] * lg[:4, None]),
                          jnp.exp(idx[None, :] * lg[4:, None])], axis=0)[:, :, None]
    return dm, qs, ks, jnp.exp(CHUNK * lg)[:, None, None]


A_PIECES = ((0, True, A_DK ** -0.5, "a_q"), (1, True, 1.0, "a_k"), (2, False, 1.0, "a_v"))
B_ROPE_COLS = [C_BQ // 512, C_BKV // 256, 0, 0]
C_ROPE_COLS = [C_CQ // 512, C_CK // 512, 0, 0]
MERGE_COLS = [C_MERGE // 1024, C_MERGE // 1024 + 1, C_MERGE // 1024 + 2, 0, 0, 0]


def _conv8(conv_w):
    return jnp.pad(conv_w, ((0, 8 - A_CONV), (0, 0)))


def _core_forward(proj, p):
    (cos_b, sin_b), (cos_c, sin_c) = _rope_tables()
    conv8 = _conv8(p["a_conv_w"])
    q, k, v = [_a_prep_fwd(proj, conv8, col, nrm, scl, nm) for col, nrm, scl, nm in A_PIECES]
    gates = _delta_gates(proj[:, C_AB:C_AB + 16], p["a_log"], p["a_dt_bias"])
    of, orv, ssf, ssr, tsf, tsr = _delta_fwd_call(q, k, v, *gates)
    (y_a,) = _a_out.fwd([(of, orv), proj], [p["a_norm_w"][None, :]], [0, C_AZ // 512])

    qb, kvb = _b_rope.fwd([proj, proj, cos_b, sin_b], [], B_ROPE_COLS)
    ob = _attn_fwd_call(qb, kvb, p["b_sink"])
    (y_b,) = _b_out.fwd([ob, proj], [], [0, C_BZ // 512])

    qc, kc = _c_rope.fwd([proj, proj, cos_c, sin_c], [], C_ROPE_COLS)
    cf, cr, csf, csr = _ret_fwd_call(qc, kc, proj, C_CV // 512, *_ret_consts(p["c_decay"]))
    (y_c,) = _c_out.fwd([(cf, cr), proj], [p["c_norm_w"][None, :]], [0, C_CZ // 512])

    wb = p["w_branch"].astype(BF16)
    pa, pb, pc = [_matmul(y, wb[i], "branch_" + "abc"[i]) for i, y in enumerate((y_a, y_b, y_c))]
    (merged,) = _merge.fwd([proj, proj, proj, pa, pb, pc], [], MERGE_COLS)
    saved = dict(q=q, k=k, v=v, of=of, orv=orv, ss=(ssf, ssr), ts=(tsf, tsr), qb=qb, kvb=kvb, ob=ob,
                 qc=qc, kc=kc, cf=cf, cr=cr, cs=(csf, csr), y=(y_a, y_b, y_c), pabc=(pa, pb, pc))
    return merged, saved


@jax.custom_vjp
def _layer_core(proj, p):
    return _core_forward(proj, p)[0]


def _layer_core_fwd(proj, p):
    merged, saved = _core_forward(proj, p)
    return merged, (proj, p, saved)


def _layer_core_bwd(res, dmerged):
    proj, p, s = res
    (cos_b, sin_b), (cos_c, sin_c) = _rope_tables()
    conv8 = _conv8(p["a_conv_w"])
    wb = p["w_branch"].astype(BF16)
    y_a, y_b, y_c = s["y"]

    (dma, dmb, dmc, dpa, dpb, dpc), _ = _merge.bwd([proj, proj, proj, *s["pabc"]], [], [dmerged], MERGE_COLS)
    dy = [_matmul(d, wb[i], "branch_%s_da" % "abc"[i], "nt") for i, d in enumerate((dpa, dpb, dpc))]
    dwb = jnp.stack([_matmul(y, d, "branch_%s_db" % "abc"[i], "tn")
                     for i, (y, d) in enumerate(zip((y_a, y_b, y_c), (dpa, dpb, dpc)))])

    consts, consts_vjp = jax.vjp(_ret_consts, p["c_decay"])
    (do_c, dcz), (dcnw,) = _c_out.bwd([(s["cf"], s["cr"]), proj], [p["c_norm_w"][None, :]], [dy[2]], [0, C_CZ // 512])
    g = _ret_bwd_call(s["qc"], s["kc"], proj, C_CV // 512, *consts, s["cs"], do_c)
    (dcq, dck), _ = _c_rope.bwd([proj, proj, cos_c, sin_c], [], [(g[0], g[1]), (g[2], g[3])], C_ROPE_COLS)
    dcv = g[4] + g[5]
    (dc_decay,) = consts_vjp(tuple(g[6:10]))

    (dob, dbz), _ = _b_out.bwd([s["ob"], proj], [], [dy[1]], [0, C_BZ // 512])
    dqb, dkvb, dsink = _attn_bwd_call(s["qb"], s["kvb"], p["b_sink"], dob)
    (dbq, dbkv), _ = _b_rope.bwd([proj, proj, cos_b, sin_b], [], [dqb, dkvb], B_ROPE_COLS)

    ab = proj[:, C_AB:C_AB + 16]
    gates, gates_vjp = jax.vjp(_delta_gates, ab, p["a_log"], p["a_dt_bias"])
    (do_a, daz), (danw,) = _a_out.bwd([(s["of"], s["orv"]), proj], [p["a_norm_w"][None, :]], [dy[0]], [0, C_AZ // 512])
    g = _delta_bwd_call(s["q"], s["k"], s["v"], *gates, s["ss"], s["ts"], do_a)
    dgates = ((g[6], g[7]), (g[8], g[9]), (g[10], g[11]))
    dab, da_log, ddt = gates_vjp(dgates)
    dpre, dconv = [], []
    for (col, nrm, scl, nm), df, dr in zip(A_PIECES, (g[0], g[2], g[4]), (g[1], g[3], g[5])):
        dx, dw = _a_prep_bwd(proj, conv8, col, nrm, scl, df, dr, nm)
        dpre.append(dx)
        dconv.append(dw[:A_CONV])

    dproj = jnp.concatenate(dpre + [daz, dbq, dbz, dcq, dck, dcv, dcz, dma, dmb, dmc, dbkv,
                                    jnp.pad(dab, ((0, 0), (0, IN_PAD - C_AB - 16)))], axis=1)
    dp = dict(a_conv_w=jnp.concatenate(dconv, axis=1), a_log=da_log, a_dt_bias=ddt, a_norm_w=danw[0],
              b_sink=dsink[:, 0], c_decay=dc_decay, c_norm_w=dcnw[0], w_branch=dwb)
    return dproj, dp


_layer_core.defvjp(_layer_core_fwd, _layer_core_bwd)
CORE_PARAMS = ("a_conv_w", "a_log", "a_dt_bias", "a_norm_w", "b_sink", "c_decay", "c_norm_w", "w_branch")


def _pad_w_in(w):
    return jnp.concatenate([w[:, 0:2048], w[:, 2064:2576], w[:, 2832:3344], w[:, 3344:8464], w[:, 2576:2832],
                            w[:, 2048:2064], jnp.zeros((D_MODEL, IN_PAD - IN_WIDTH), w.dtype)], axis=1)


def _local_loss(wts, x, c, ctx, loss_target):
    sc16 = jnp.zeros((16, D_MODEL), F32).at[0].set(_silu(c)).at[1].set(_silu(wts["c_ctx"]))
    xs = jnp.concatenate([ctx, x], axis=0)
    for layer in range(DEPTH):
        last = layer == DEPTH - 1
        mod16 = _mm(sc16, wts["w_ada"][layer], "ada%d" % layer) + wts["b_ada"][layer][None, :]
        mod_cx = jnp.stack([mod16[1], mod16[0]])
        shift, scale, gate = jnp.split(mod_cx, 3, axis=1)
        (h,) = _norm_mod.call([xs], [wts["norm_w"][layer][None, :], shift, scale])
        proj = _mm(h, _pad_w_in(wts["w_in"][layer]), "w_in%d" % layer)
        merged = _layer_core(proj, {n: wts[n][layer] for n in CORE_PARAMS})
        if last:
            merged, xs, gate = merged[CTX_LEN:], xs[CTX_LEN:], gate[1:2]
        out = _mm(merged, wts["w_out"][layer], "w_out%d" % layer)
        (xs,) = _residual.call([xs, out], [gate])
    (per_row,) = _loss_rows.call([xs, loss_target], [wts["final_norm_w"][None, :]])
    return jnp.sum(per_row[:, 0])


SHARDED = ("w_ada", "w_in", "a_conv_w", "w_branch", "w_out")
MATMUL_ONLY = ("w_ada", "w_in", "w_branch", "w_out")
SMALL = ("c_ctx", "b_ada", "norm_w", "a_log", "a_dt_bias", "a_norm_w", "b_sink", "c_decay", "c_norm_w",
         "final_norm_w")
WEIGHTS = ("c_ctx", "w_ada", "b_ada", "norm_w", "w_in", "a_conv_w", "a_log", "a_dt_bias", "a_norm_w", "b_sink",
           "c_decay", "c_norm_w", "w_branch", "w_out", "final_norm_w")
SMALL_PACK = 12288


def _wire(name, a):
    return a.astype(BF16) if name in MATMUL_ONLY else a


def _unshard(name, g):
    if name == "w_branch":
        return g.transpose(1, 2, 3, 0, 4).reshape(DEPTH, 3, BR_WIDTH, D_MODEL)
    if name == "w_out":
        return g.transpose(1, 0, 2, 3).reshape(DEPTH, D_MODEL, D_MODEL)
    s = g.shape
    return g.transpose(1, 2, 0, 3).reshape(s[1], s[2], N_DEV * s[3])


def _reshard(name, w):
    if name == "w_branch":
        return w.reshape(DEPTH, 3, BR_WIDTH, N_DEV, D_MODEL // N_DEV).transpose(3, 0, 1, 2, 4)
    if name == "w_out":
        return w.reshape(DEPTH, N_DEV, D_MODEL // N_DEV, D_MODEL).transpose(1, 0, 2, 3)
    s = w.shape
    return w.reshape(s[0], s[1], N_DEV, s[2] // N_DEV).transpose(2, 0, 1, 3)


def _pack_small(tree):
    flat = jnp.concatenate([tree[n].reshape(-1) for n in SMALL])
    return jnp.pad(flat, (0, SMALL_PACK - flat.shape[0])).reshape(SMALL_PACK // 128, 128)


def _unpack_small(packed, like):
    flat = packed.reshape(-1)
    out, off = {}, 0
    for n in SMALL:
        size = math.prod(like[n].shape)
        out[n] = flat[off:off + size].reshape(like[n].shape)
        off += size
    return out


def kernel(x, c, ctx, c_ctx, w_ada, b_ada, norm_w, w_in, a_conv_w, a_log, a_dt_bias, a_norm_w, b_sink, c_decay, c_norm_w, w_branch, w_out, final_norm_w, loss_target, m_c_ctx, m_w_ada, m_b_ada, m_norm_w, m_w_in, m_a_conv_w, m_a_log, m_a_dt_bias, m_a_norm_w, m_b_sink, m_c_decay, m_c_norm_w, m_w_branch, m_w_out, m_final_norm_w, v_c_ctx, v_w_ada, v_b_ada, v_norm_w, v_w_in, v_a_conv_w, v_a_log, v_a_dt_bias, v_a_norm_w, v_b_sink, v_c_decay, v_c_norm_w, v_w_branch, v_w_out, v_final_norm_w):
    w = dict(c_ctx=c_ctx, w_ada=w_ada, b_ada=b_ada, norm_w=norm_w, w_in=w_in, a_conv_w=a_conv_w, a_log=a_log,
             a_dt_bias=a_dt_bias, a_norm_w=a_norm_w, b_sink=b_sink, c_decay=c_decay, c_norm_w=c_norm_w,
             w_branch=w_branch, w_out=w_out, final_norm_w=final_norm_w)
    m = dict(c_ctx=m_c_ctx, w_ada=m_w_ada, b_ada=m_b_ada, norm_w=m_norm_w, w_in=m_w_in, a_conv_w=m_a_conv_w,
             a_log=m_a_log, a_dt_bias=m_a_dt_bias, a_norm_w=m_a_norm_w, b_sink=m_b_sink, c_decay=m_c_decay,
             c_norm_w=m_c_norm_w, w_branch=m_w_branch, w_out=m_w_out, final_norm_w=m_final_norm_w)
    v = dict(c_ctx=v_c_ctx, w_ada=v_w_ada, b_ada=v_b_ada, norm_w=v_norm_w, w_in=v_w_in, a_conv_w=v_a_conv_w,
             a_log=v_a_log, a_dt_bias=v_a_dt_bias, a_norm_w=v_a_norm_w, b_sink=v_b_sink, c_decay=v_c_decay,
             c_norm_w=v_c_norm_w, w_branch=v_w_branch, w_out=v_w_out, final_norm_w=v_final_norm_w)

    gathered = _exchange([_wire(n, w[n]) for n in SHARDED], True, "gather_weights")
    full = dict(w)
    for n, g in zip(SHARDED, gathered):
        full[n] = _unshard(n, g).astype(F32)

    loss, (gw, gx) = jax.value_and_grad(_local_loss, argnums=(0, 1))(full, x[0], c[0], ctx[0], loss_target[0])
    loss = lax.psum(loss, ("x", "y", "c"))

    blocks = _exchange([_wire(n, _reshard(n, gw[n])) for n in SHARDED], False, "scatter_grads")
    small = _exchange([_pack_small(gw)], True, "gather_small_grads")[0]

    grad, delta, new_m, new_v = {}, {}, {}, {}
    for n, contrib in zip(SHARDED, blocks):
        shp = w[n].shape
        two_d = (math.prod(shp[:-1]), shp[-1])
        outs = _adamw(w[n].reshape(two_d), m[n].reshape(two_d), v[n].reshape(two_d),
                      contrib.reshape((N_DEV,) + two_d), "adamw_" + n)
        grad[n], delta[n], new_m[n], new_v[n] = [o.reshape(shp) for o in outs]
    outs = _adamw(_pack_small(w), _pack_small(m), _pack_small(v), small, "adamw_small")
    for tree, packed in zip((grad, delta, new_m, new_v), outs):
        tree.update(_unpack_small(packed, w))

    return (loss, gx[None], *[grad[n] for n in WEIGHTS], *[delta[n] for n in WEIGHTS],
            *[new_m[n] for n in WEIGHTS], *[new_v[n] for n in WEIGHTS])
```

```python
import functools
import math

import jax
import jax.numpy as jnp
from jax import lax
from jax.experimental import pallas as pl
from jax.experimental.pallas import tpu as pltpu

F32 = jnp.float32
BF16 = jnp.bfloat16
INV_PRECISION = lax.Precision.HIGH

D_MODEL = 1024
SEQ = 4096
DEPTH = 2
GRID_W = 64
CTX_LEN = 256
EPS = 1e-6
ROPE_BASE = 10000.0
BR_WIDTH = D_MODEL // 2
A_DK = 128
A_HEADS = 4
A_WIDTH = 512
A_CONV = 5
B_HD = 64
B_Q_HEADS = 8
B_KV_HEADS = 2
WINDOW = 128
B_BLOCK = 128
C_HD = 128
C_HEADS = 4
C_WIDTH = 512
CHUNK = 64
ADAM_LR = 0.001
ADAM_B1 = 0.9
ADAM_B2 = 0.999
ADAM_EPS = 1e-08
ADAM_WD = 0.01
ADAM_STEP = 10

N_DEV = 8
ROWS = CTX_LEN + SEQ
N_CHUNK = ROWS // CHUNK
N_CTX_CHUNK = CTX_LEN // CHUNK
IN_WIDTH = 8464
IN_PAD = 8704
NEG = -1e30

VMEM_LIMIT = 48 * 1024 * 1024
MESH = pl.DeviceIdType.MESH

C_AQ, C_AK, C_AV, C_AZ, C_BQ, C_BZ, C_CQ, C_CK, C_CV, C_CZ = (i * 512 for i in range(10))
C_MERGE = 5120
C_BKV = 8192
C_AB = 8448


def _cparams(sem=None):
    if sem is None:
        return pltpu.CompilerParams(vmem_limit_bytes=VMEM_LIMIT)
    return pltpu.CompilerParams(dimension_semantics=sem, vmem_limit_bytes=VMEM_LIMIT)


def _dg(a, b, ca, cb, prec=None):
    return lax.dot_general(a, b, (((ca,), (cb,)), ((), ())), preferred_element_type=F32, precision=prec)


@functools.partial(jax.custom_vjp, nondiff_argnums=(2, 3))
def _bdot(a, b, ca, cb):
    return _dg(a.astype(BF16), b.astype(BF16), ca, cb)


def _bdot_fwd(a, b, ca, cb):
    return _bdot(a, b, ca, cb), (a, b)


def _bdot_bwd(ca, cb, res, ct):
    a, b = res
    da = _bdot(ct, b, 1, 1 - cb) if ca == 1 else _bdot(b, ct, 1 - cb, 1)
    db = _bdot(a, ct, 1 - ca, 0) if cb == 0 else _bdot(ct, a, 0, 1 - ca)
    return da, db


_bdot.defvjp(_bdot_fwd, _bdot_bwd)


def _hdot(a, b):
    return _dg(a, b, 1, 0, INV_PRECISION)


def _k_silu(x):
    return x / (1.0 + jnp.exp(-x))


def _k_sigmoid(x):
    return 1.0 / (1.0 + jnp.exp(-x))


@jax.custom_vjp
def _swap64(x):
    return pltpu.roll(x, 64, 1)


_swap64.defvjp(lambda x: (pltpu.roll(x, 64, 1), None), lambda _, ct: (pltpu.roll(ct, 64, 1),))


def _swap16_impl(x):
    lane = lax.broadcasted_iota(jnp.int32, x.shape, 1)
    return jnp.where((lane & 16) == 0, pltpu.roll(x, 112, 1), pltpu.roll(x, 16, 1))


@jax.custom_vjp
def _swap16(x):
    return _swap16_impl(x)


_swap16.defvjp(lambda x: (_swap16_impl(x), None), lambda _, ct: (_swap16_impl(ct),))


def _pick(dim, prefs):
    for p in prefs:
        if dim % p == 0:
            return p
    return dim


def _matmul(a, b, name, mode="nn", tiles=None):
    ca, cb = {"nn": (1, 0), "nt": (1, 1), "tn": (0, 0)}[mode]
    m, k = a.shape[1 - ca], a.shape[ca]
    n = b.shape[1 - cb]
    if tiles is None:
        tiles = (_pick(m, (1088, 1024, 512, 256, 128)), _pick(n, (512, 256, 128)),
                 _pick(k, (1088, 1024, 512, 256, 128) if mode == "tn" else (2176, 2048, 1024, 512, 256, 128)))
    tm, tn, tk = tiles
    nk = k // tk
    a_spec = (pl.BlockSpec((tm, tk), lambda i, j, kk: (i, kk)) if ca == 1
              else pl.BlockSpec((tk, tm), lambda i, j, kk: (kk, i)))
    b_spec = (pl.BlockSpec((tk, tn), lambda i, j, kk: (kk, j)) if cb == 0
              else pl.BlockSpec((tn, tk), lambda i, j, kk: (j, kk)))

    def body(a_ref, b_ref, o_ref):
        part = _dg(a_ref[...].astype(BF16), b_ref[...].astype(BF16), ca, cb)
        if nk == 1:
            o_ref[...] = part
        else:
            kk = pl.program_id(2)

            @pl.when(kk == 0)
            def _():
                o_ref[...] = part

            @pl.when(kk > 0)
            def _():
                o_ref[...] += part

    return pl.pallas_call(
        body,
        grid=(m // tm, n // tn, nk),
        in_specs=[a_spec, b_spec],
        out_specs=pl.BlockSpec((tm, tn), lambda i, j, kk: (i, j)),
        out_shape=jax.ShapeDtypeStruct((m, n), F32),
        compiler_params=_cparams(("parallel", "parallel", "arbitrary")),
        name=name,
    )(a, b)


@functools.partial(jax.custom_vjp, nondiff_argnums=(2,))
def _mm(a, b, name):
    return _matmul(a, b, name)


def _mm_fwd(a, b, name):
    b16 = b.astype(BF16)
    return _matmul(a, b16, name), (a, b16)


def _mm_bwd(name, res, ct):
    a, b16 = res
    da = _matmul(ct, b16, name + "_da", "nt")
    db = _matmul(a, ct, name + "_db", "tn")
    return da, db


_mm.defvjp(_mm_fwd, _mm_bwd)


ROW_BLOCK = 256
ROW_VMEM_BUDGET = 16 * 1024 * 1024


def _pieces(val, pw):
    return [val[:, j * pw:(j + 1) * pw] for j in range(val.shape[1] // pw)]


def _flat(groups):
    arrays, sizes = [], []
    for g in groups:
        g = g if isinstance(g, (tuple, list)) else (g,)
        arrays += list(g)
        sizes.append(len(g))
    return arrays, sizes


def _regroup(refs, sizes):
    out, at = [], 0
    for n in sizes:
        val = refs[at][...]
        for r in refs[at + 1:at + n]:
            val = val + r[...]
        out.append(val)
        at += n
    return out


class _Rowwise:
    def __init__(self, fn, name, row_wpw, par_pw, out_wpw, n_diff=None):
        self.fn, self.name, self.row_wpw, self.par_pw, self.out_wpw = fn, name, row_wpw, par_pw, out_wpw
        self.n_diff = len(row_wpw) if n_diff is None else n_diff

        @jax.custom_vjp
        def call(rows, params):
            return self.fwd(rows, params)

        def call_fwd(rows, params):
            return self.fwd(rows, params), (rows, params)

        def call_bwd(res, douts):
            return self.bwd(res[0], res[1], douts)

        call.defvjp(call_fwd, call_bwd)
        self.call = call

    def _load(self, row_vals, par_refs, br, with_ctx):
        row = pl.program_id(0) * br + lax.broadcasted_iota(jnp.int32, (br, 1), 0)
        is_ctx = (row < (CTX_LEN if with_ctx else 0)).astype(F32)
        rows = [_pieces(v, pw) for v, (_, pw) in zip(row_vals, self.row_wpw)]
        pars = []
        for p, pw in zip(par_refs, self.par_pw):
            val = p[...] if p.shape[0] == 1 else is_ctx * p[0:1, :] + (1.0 - is_ctx) * p[1:2, :]
            pars.append(_pieces(val, pw))
        return rows, pars, is_ctx

    def _block_rows(self, n_rows, widths):
        for br in (1088, 1024, 544, 512, 272):
            if n_rows % br == 0 and 2 * 4 * br * sum(widths) <= ROW_VMEM_BUDGET:
                return br
        return ROW_BLOCK

    def _row_specs(self, br, sizes, cols):
        out = []
        for (w, _), n, c in zip(self.row_wpw, sizes, cols):
            out += [pl.BlockSpec((br, w), lambda i, c=c: (i, c))] * n
        return out

    def fwd(self, rows, params, cols=None):
        arrays, sizes = _flat(rows)
        cols = cols or [0] * len(rows)
        n_rows = arrays[0].shape[0]
        n_in = len(arrays)
        br = self._block_rows(n_rows, [w for (w, _), n in zip(self.row_wpw, sizes) for _ in range(n)]
                              + [w for w, _ in self.out_wpw])

        def body(*refs):
            r, p, _ = self._load(_regroup(refs[:n_in], sizes), refs[n_in:n_in + len(params)], br, n_rows == ROWS)
            for o_ref, pieces, (_, pw) in zip(refs[n_in + len(params):], self.fn(r, p), self.out_wpw):
                for j, piece in enumerate(pieces):
                    o_ref[:, j * pw:(j + 1) * pw] = piece

        return pl.pallas_call(
            body,
            grid=(n_rows // br,),
            in_specs=self._row_specs(br, sizes, cols) + [pl.BlockSpec(p.shape, lambda i: (0, 0)) for p in params],
            out_specs=[pl.BlockSpec((br, w), lambda i: (i, 0)) for w, _ in self.out_wpw],
            out_shape=[jax.ShapeDtypeStruct((n_rows, w), F32) for w, _ in self.out_wpw],
            compiler_params=_cparams(("parallel",)),
            name=self.name + "_fwd",
        )(*arrays, *params)

    def bwd(self, rows, params, douts, cols=None):
        arrays, sizes = _flat(rows)
        darrays, dsizes = _flat(douts)
        cols = cols or [0] * len(rows)
        n_rows = arrays[0].shape[0]
        n_in, n_par, n_dout, n_diff = len(arrays), len(params), len(darrays), self.n_diff
        br = self._block_rows(n_rows, [w for (w, _), n in zip(self.row_wpw, sizes) for _ in range(n)]
                              + [w for (w, _), n in zip(self.out_wpw, dsizes) for _ in range(n)]
                              + [w for w, _ in self.row_wpw[:n_diff]])

        def body(*refs):
            par_refs = refs[n_in:n_in + n_par]
            dout_refs = refs[n_in + n_par:n_in + n_par + n_dout]
            drow_refs = refs[n_in + n_par + n_dout:n_in + n_par + n_dout + n_diff]
            dpar_refs = refs[n_in + n_par + n_dout + n_diff:]

            @pl.when(pl.program_id(0) == 0)
            def _():
                for d in dpar_refs:
                    d[...] = jnp.zeros_like(d)

            r, p, is_ctx = self._load(_regroup(refs[:n_in], sizes), par_refs, br, n_rows == ROWS)
            cts = [_pieces(d, pw) for d, (_, pw) in zip(_regroup(dout_refs, dsizes), self.out_wpw)]
            fixed = r[n_diff:]
            _, vjp = jax.vjp(lambda rd, pp: self.fn(rd + fixed, pp), r[:n_diff], p)
            dr, dp = vjp(cts)
            for d_ref, pieces, (_, pw) in zip(drow_refs, dr, self.row_wpw):
                for j, piece in enumerate(pieces):
                    d_ref[:, j * pw:(j + 1) * pw] = piece
            for d_ref, pieces, pw in zip(dpar_refs, dp, self.par_pw):
                for j, piece in enumerate(pieces):
                    lanes = slice(j * pw, (j + 1) * pw)
                    if d_ref.shape[0] == 1:
                        d_ref[:, lanes] += piece
                    else:
                        d_ref[0:1, lanes] += jnp.sum(is_ctx * piece, axis=0, keepdims=True)
                        d_ref[1:2, lanes] += jnp.sum((1.0 - is_ctx) * piece, axis=0, keepdims=True)

        par_specs = [pl.BlockSpec(p.shape, lambda i: (0, 0)) for p in params]
        dout_specs = []
        for (w, _), n in zip(self.out_wpw, dsizes):
            dout_specs += [pl.BlockSpec((br, w), lambda i: (i, 0))] * n
        drow_w = [w for w, _ in self.row_wpw[:n_diff]]
        g = pl.pallas_call(
            body,
            grid=(n_rows // br,),
            in_specs=self._row_specs(br, sizes, cols) + par_specs + dout_specs,
            out_specs=[pl.BlockSpec((br, w), lambda i: (i, 0)) for w in drow_w] + par_specs,
            out_shape=[jax.ShapeDtypeStruct((n_rows, w), F32) for w in drow_w]
            + [jax.ShapeDtypeStruct(p.shape, F32) for p in params],
            compiler_params=_cparams(("arbitrary",)),
            name=self.name + "_bwd",
        )(*arrays, *params, *darrays)
        return list(g[:n_diff]), list(g[n_diff:])


def _fn_norm_mod(rows, pars):
    (x,), (nw,), (shift,), (scale,) = rows[0], pars[0], pars[1], pars[2]
    y = x * lax.rsqrt(jnp.mean(x * x, axis=-1, keepdims=True) + EPS) * nw
    return [[y * (1.0 + scale) + shift]]


def _fn_head_rms_gate(rows, pars):
    (w,) = pars[0]
    return [[o * lax.rsqrt(jnp.mean(o * o, axis=-1, keepdims=True) + EPS) * w * _k_silu(z)
             for o, z in zip(rows[0], rows[1])]]


def _fn_group_norm_gate(rows, pars):
    out = []
    for o, z, w in zip(rows[0], rows[1], pars[0]):
        mu = jnp.mean(o, axis=-1, keepdims=True)
        var = jnp.mean(jnp.square(o - mu), axis=-1, keepdims=True)
        out.append((o - mu) * lax.rsqrt(var + EPS) * w * _k_silu(z))
    return [out]


def _fn_gate(rows, pars):
    return [[o * _k_silu(z) for o, z in zip(rows[0], rows[1])]]


def _fn_merge(rows, pars):
    (ma,), (mb,), (mc,), (pa,), (pb,), (pc,) = rows
    return [[_k_sigmoid(ma) * pa + _k_sigmoid(mb) * pb + _k_sigmoid(mc) * pc]]


def _fn_residual(rows, pars):
    (res,), (out,), (gate,) = rows[0], rows[1], pars[0]
    return [[res + gate * out]]


def _fn_loss(rows, pars):
    (x,), (target,), (w,) = rows[0], rows[1], pars[0]
    y = x * lax.rsqrt(jnp.mean(x * x, axis=-1, keepdims=True) + EPS) * w
    per_row = 0.5 * jnp.mean(jnp.square(y - target), axis=-1, keepdims=True)
    return [[jnp.broadcast_to(per_row, (per_row.shape[0], 128))]]


def _fn_b_rope(rows, pars):
    q, (k, v), (cos,), (sin,) = rows
    rot = lambda x: x * cos + _swap16(x) * sin
    return [[rot(x) for x in q], [rot(k), v]]


def _fn_c_rope(rows, pars):
    q, k, (cos,), (sin,) = rows
    rot = lambda x: x * cos + _swap64(x) * sin
    return [[rot(x) for x in q], [rot(x) * (C_HD ** -0.5) for x in k]]


_norm_mod = _Rowwise(_fn_norm_mod, "norm_mod", [(D_MODEL, D_MODEL)], [D_MODEL] * 3, [(D_MODEL, D_MODEL)])
_residual = _Rowwise(_fn_residual, "residual", [(D_MODEL, D_MODEL)] * 2, [D_MODEL], [(D_MODEL, D_MODEL)])
_loss_rows = _Rowwise(_fn_loss, "loss", [(D_MODEL, D_MODEL)] * 2, [D_MODEL], [(128, 128)])
_a_out = _Rowwise(_fn_head_rms_gate, "a_out", [(512, 128)] * 2, [128], [(512, 128)])
_c_out = _Rowwise(_fn_group_norm_gate, "c_out", [(512, 128)] * 2, [128], [(512, 128)])
_b_out = _Rowwise(_fn_gate, "b_out", [(512, 512)] * 2, [], [(512, 512)])
_merge = _Rowwise(_fn_merge, "merge", [(D_MODEL, D_MODEL)] * 6, [], [(D_MODEL, D_MODEL)])
_b_rope = _Rowwise(_fn_b_rope, "b_rope", [(512, 128), (256, 128), (128, 128), (128, 128)], [],
                   [(512, 128), (256, 128)], n_diff=2)
_c_rope = _Rowwise(_fn_c_rope, "c_rope", [(512, 128), (512, 128), (128, 128), (128, 128)], [],
                   [(512, 128), (512, 128)], n_diff=2)


HALO = 8
EXT = ROW_BLOCK + 2 * HALO


def _halo_specs(col, width=512):
    last = ROWS // HALO - 1
    per = ROW_BLOCK // HALO
    prev = pl.BlockSpec((HALO, width), lambda i: (jnp.maximum(i * per - 1, 0), col))
    cur = pl.BlockSpec((ROW_BLOCK, width), lambda i: (i, col))
    nxt = pl.BlockSpec((HALO, width), lambda i: (jnp.minimum((i + 1) * per, last), col))
    return [prev, cur, nxt]


def _extended(prev_ref, cur_ref, next_ref):
    i = pl.program_id(0)
    prev_ok = i >= 2
    next_ok = jnp.logical_and(i >= 1, i < ROWS // ROW_BLOCK - 1)
    return jnp.concatenate([jnp.where(prev_ok, prev_ref[...], 0.0), cur_ref[...],
                            jnp.where(next_ok, next_ref[...], 0.0)], axis=0)


def _conv_taps(x_ext, w_ref, flip):
    acc = None
    for j in range(A_CONV):
        shift = (j - 2) if flip else (2 - j)
        term = w_ref[j:j + 1, :] * pltpu.roll(x_ext, shift % EXT, 0)
        acc = term if acc is None else acc + term
    return acc


def _conv_post(pre_pieces, normalize, scale):
    out = []
    for p in pre_pieces:
        y = _k_silu(p)
        if normalize:
            y = y * lax.rsqrt(jnp.sum(y * y, axis=-1, keepdims=True) + EPS) * scale
        out.append(y)
    return out


def _a_prep_fwd(proj, conv8, col, normalize, scale, name):
    def body(prev_ref, cur_ref, next_ref, w_ref, o_ref):
        pre = _conv_taps(_extended(prev_ref, cur_ref, next_ref), w_ref, False)[HALO:HALO + ROW_BLOCK]
        for h, y in enumerate(_conv_post(_pieces(pre, 128), normalize, scale)):
            o_ref[:, h * 128:(h + 1) * 128] = y

    return pl.pallas_call(
        body,
        grid=(ROWS // ROW_BLOCK,),
        in_specs=_halo_specs(col) + [pl.BlockSpec((8, 512), lambda i: (0, col))],
        out_specs=pl.BlockSpec((ROW_BLOCK, 512), lambda i: (i, 0)),
        out_shape=jax.ShapeDtypeStruct((ROWS, 512), F32),
        compiler_params=_cparams(("parallel",)),
        name=name + "_fwd",
    )(proj, proj, proj, conv8)


def _a_prep_bwd(proj, conv8, col, normalize, scale, dout_f, dout_r, name):
    def body(xp, xc, xn, w_ref, fp, fc, fn_, rp, rc, rn, dx_ref, dw_ref):
        @pl.when(pl.program_id(0) == 0)
        def _():
            dw_ref[...] = jnp.zeros_like(dw_ref)

        x_ext = _extended(xp, xc, xn)
        dout = _extended(fp, fc, fn_) + _extended(rp, rc, rn)
        pre = _conv_taps(x_ext, w_ref, False)
        _, vjp = jax.vjp(lambda p: _conv_post(p, normalize, scale), _pieces(pre, 128))
        (dpre,) = vjp(_pieces(dout, 128))
        dpre = jnp.concatenate(dpre, axis=1)
        dx_ref[...] = _conv_taps(dpre, w_ref, True)[HALO:HALO + ROW_BLOCK]
        own = dpre[HALO:HALO + ROW_BLOCK]
        for j in range(A_CONV):
            shifted = pltpu.roll(x_ext, (2 - j) % EXT, 0)[HALO:HALO + ROW_BLOCK]
            dw_ref[j:j + 1, :] += jnp.sum(own * shifted, axis=0, keepdims=True)

    return pl.pallas_call(
        body,
        grid=(ROWS // ROW_BLOCK,),
        in_specs=_halo_specs(col) + [pl.BlockSpec((8, 512), lambda i: (0, col))] + _halo_specs(0) + _halo_specs(0),
        out_specs=[pl.BlockSpec((ROW_BLOCK, 512), lambda i: (i, 0)), pl.BlockSpec((8, 512), lambda i: (0, 0))],
        out_shape=[jax.ShapeDtypeStruct((ROWS, 512), F32), jax.ShapeDtypeStruct((8, 512), F32)],
        compiler_params=_cparams(("arbitrary",)),
        name=name + "_bwd",
    )(proj, proj, proj, conv8, dout_f, dout_f, dout_f, dout_r, dout_r, dout_r)


N_CHAIN = 8


def _rev_chunk(s):
    return jnp.where(s < N_CTX_CHUNK, N_CTX_CHUNK - 1 - s, N_CHUNK + N_CTX_CHUNK - 1 - s)


def _scan_specs(step_of, v_col=0):
    cf = step_of
    cr = lambda n: _rev_chunk(step_of(n))

    def pair(shape, index):
        return (pl.BlockSpec(shape, lambda n: index(cf(n))), pl.BlockSpec(shape, lambda n: index(cr(n))))

    return dict(
        tok=pair((CHUNK, 512), lambda c: (c, 0)),
        tokv=pair((CHUNK, 512), lambda c: (c, v_col)),
        col=pair((4, CHUNK, 1), lambda c: (0, c, 0)),
        row=pair((4, 1, 1, CHUNK), lambda c: (0, c, 0, 0)),
        one=pair((4, 1, 1, 1), lambda c: (0, c, 0, 0)),
        state=pair((None, 4, 128, 128), lambda c: (c, 0, 0, 0)),
        tinv=pair((None, 4, CHUNK, CHUNK), lambda c: (c, 0, 0, 0)),
    )


def _both(specs, kinds):
    out = []
    for kind in kinds:
        out += list(specs[kind])
    return out


def _chain_masks():
    ii = lax.broadcasted_iota(jnp.int32, (CHUNK, CHUNK), 0)
    jj = lax.broadcasted_iota(jnp.int32, (CHUNK, CHUNK), 1)
    eye = jnp.where(ii == jj, 1.0, 0.0).astype(F32)
    lower = (ii >= jj, ii > jj)
    upper = (ii <= jj, ii < jj)
    return [lower] * 4 + [upper] * 4, eye


def _tri_inv_all(ls, eye):
    xs = [eye - l for l in ls]
    ps = [_hdot(l, l) for l in ls]
    for i in range(5):
        xs = [x + _hdot(x, p) for x, p in zip(xs, ps)]
        if i < 4:
            ps = [_hdot(p, p) for p in ps]
    return xs


@jax.custom_vjp
def _inv_saved(l, x):
    return x


def _inv_saved_fwd(l, x):
    return x, x


def _inv_saved_bwd(x, dx):
    return -_bdot(x, _bdot(dx, x, 1, 1), 0, 0), jnp.zeros_like(x)


_inv_saved.defvjp(_inv_saved_fwd, _inv_saved_bwd)


def _delta_chains(q, k, v, beta_r, gcr, gl, s, masks, eye, tinv_saved):
    n = range(len(q))
    beta = [jnp.sum(eye * beta_r[i], axis=1, keepdims=True) for i in n]
    gcc = [jnp.sum(eye * gcr[i], axis=1, keepdims=True) for i in n]
    decay = [jnp.exp(jnp.where(masks[i][0], gcc[i] - gcr[i], NEG)) for i in n]
    kb = [k[i] * beta[i] for i in n]
    lmat = [jnp.where(masks[i][1], _bdot(kb[i], k[i], 1, 1) * decay[i], 0.0) for i in n]
    if tinv_saved is None:
        tinv = _tri_inv_all(lmat, eye)
    else:
        tinv = [_inv_saved(lmat[i], tinv_saved[i]) for i in n]
    eg = [jnp.exp(gcc[i]) for i in n]
    u = [_bdot(tinv[i], v[i] * beta[i], 1, 0) for i in n]
    w = [_bdot(tinv[i], kb[i] * eg[i], 1, 0) for i in n]
    qk = [_bdot(q[i], k[i], 1, 1) * decay[i] for i in n]
    v_new = [u[i] - _bdot(w[i], s[i], 1, 0) for i in n]
    o = [_bdot(q[i] * eg[i], s[i], 1, 0) + _bdot(qk[i], v_new[i], 1, 0) for i in n]
    s_new = [s[i] * jnp.exp(gl[i]) + _bdot(k[i] * jnp.exp(gl[i] - gcc[i]), v_new[i], 0, 0) for i in n]
    return (o, s_new), tinv


def _chain_loads(tok_pairs, small_pairs):
    toks = [[pair[i // 4][:, (i % 4) * 128:(i % 4 + 1) * 128] for i in range(N_CHAIN)] for pair in tok_pairs]
    smalls = [[pair[i // 4][i % 4] for i in range(N_CHAIN)] for pair in small_pairs]
    return toks, smalls


def _delta_fwd_call(q, k, v, beta, gc, gl):
    sp = _scan_specs(lambda n: n)

    def body(qf, qr, kf, kr, vf, vr, bf, br, gcrf, gcrr, glf, glr, of, orv, ssf, ssr, tsf, tsr, s_scr):
        @pl.when(pl.program_id(0) == 0)
        def _():
            s_scr[...] = jnp.zeros_like(s_scr)

        masks, eye = _chain_masks()
        (qs, ks, vs), _ = _chain_loads([(qf, qr), (kf, kr), (vf, vr)], [])
        bs = [(bf, br)[i // 4][i % 4, 0] for i in range(N_CHAIN)]
        gcrs = [(gcrf, gcrr)[i // 4][i % 4, 0] for i in range(N_CHAIN)]
        gls = [(glf, glr)[i // 4][i % 4, 0] for i in range(N_CHAIN)]
        ss = [s_scr[i] for i in range(N_CHAIN)]
        (o, s_new), tinv = _delta_chains(qs, ks, vs, bs, gcrs, gls, ss, masks, eye, None)
        for i in range(N_CHAIN):
            d, h = i // 4, i % 4
            (ssf, ssr)[d][h] = ss[i]
            (tsf, tsr)[d][h] = tinv[i]
            (of, orv)[d][:, h * 128:(h + 1) * 128] = o[i]
            s_scr[i] = s_new[i]

    return pl.pallas_call(
        body,
        grid=(N_CHUNK,),
        in_specs=_both(sp, ["tok", "tok", "tok", "row", "row", "one"]),
        out_specs=_both(sp, ["tok", "state", "tinv"]),
        out_shape=[jax.ShapeDtypeStruct((ROWS, 512), F32)] * 2
        + [jax.ShapeDtypeStruct((N_CHUNK, 4, 128, 128), F32)] * 2
        + [jax.ShapeDtypeStruct((N_CHUNK, 4, CHUNK, CHUNK), F32)] * 2,
        scratch_shapes=[pltpu.VMEM((N_CHAIN, 128, 128), F32)],
        compiler_params=_cparams(("arbitrary",)),
        name="delta_fwd",
    )(q, q, k, k, v, v, *beta, *gc, *gl)


def _delta_bwd_call(q, k, v, beta, gc, gl, ssave, tsave, do):
    sp = _scan_specs(lambda n: N_CHUNK - 1 - n)

    def body(qf, qr, kf, kr, vf, vr, bf, br, gcrf, gcrr, glf, glr, ssf, ssr, tsf, tsr, dof, dor,
             dqf, dqr, dkf, dkr, dvf, dvr, dbf, dbr, dgcrf, dgcrr, dglf, dglr, ds_scr):
        @pl.when(pl.program_id(0) == 0)
        def _():
            ds_scr[...] = jnp.zeros_like(ds_scr)

        masks, eye = _chain_masks()
        (qs, ks, vs, dos), (ss, ts) = _chain_loads(
            [(qf, qr), (kf, kr), (vf, vr), (dof, dor)], [(ssf, ssr), (tsf, tsr)])
        bs = [(bf, br)[i // 4][i % 4, 0] for i in range(N_CHAIN)]
        gcrs = [(gcrf, gcrr)[i // 4][i % 4, 0] for i in range(N_CHAIN)]
        gls = [(glf, glr)[i // 4][i % 4, 0] for i in range(N_CHAIN)]
        fn = lambda *a: _delta_chains(*a, masks, eye, ts)
        _, vjp, _ = jax.vjp(fn, qs, ks, vs, bs, gcrs, gls, ss, has_aux=True)
        dq, dk, dv, db, dgcr, dgl, ds = vjp((dos, [ds_scr[i] for i in range(N_CHAIN)]))
        for i in range(N_CHAIN):
            d, h = i // 4, i % 4
            hs = slice(h * 128, (h + 1) * 128)
            (dqf, dqr)[d][:, hs] = dq[i]
            (dkf, dkr)[d][:, hs] = dk[i]
            (dvf, dvr)[d][:, hs] = dv[i]
            (dbf, dbr)[d][h, 0] = db[i]
            (dgcrf, dgcrr)[d][h, 0] = dgcr[i]
            (dglf, dglr)[d][h, 0] = dgl[i]
            ds_scr[i] = ds[i]

    tok = jax.ShapeDtypeStruct((ROWS, 512), F32)
    return pl.pallas_call(
        body,
        grid=(N_CHUNK,),
        in_specs=_both(sp, ["tok", "tok", "tok", "row", "row", "one", "state", "tinv", "tok"]),
        out_specs=_both(sp, ["tok", "tok", "tok", "row", "row", "one"]),
        out_shape=[tok] * 6 + [jax.ShapeDtypeStruct((4, N_CHUNK, 1, CHUNK), F32)] * 4
        + [jax.ShapeDtypeStruct((4, N_CHUNK, 1, 1), F32)] * 2,
        scratch_shapes=[pltpu.VMEM((N_CHAIN, 128, 128), F32)],
        compiler_params=_cparams(("arbitrary",)),
        name="delta_bwd",
    )(q, q, k, k, v, v, *beta, *gc, *gl, *ssave, *tsave, do, do)


def _ret_chains(q, k, v, dm, qs, ks, cd, s):
    n = range(len(q))
    a = [_bdot(q[i], k[i], 1, 1) * dm[i] for i in n]
    o = [_bdot(a[i], v[i], 1, 0) + _bdot(q[i] * qs[i], s[i], 1, 0) for i in n]
    s_new = [s[i] * cd[i] + _bdot(k[i] * ks[i], v[i], 0, 0) for i in n]
    return o, s_new


def _ret_const_specs():
    return [pl.BlockSpec((N_CHAIN, CHUNK, CHUNK), lambda n: (0, 0, 0)), pl.BlockSpec((N_CHAIN, CHUNK, 1), lambda n: (0, 0, 0)),
            pl.BlockSpec((N_CHAIN, CHUNK, 1), lambda n: (0, 0, 0)), pl.BlockSpec((N_CHAIN, 1, 1), lambda n: (0, 0, 0))]


def _ret_fwd_call(q, k, v, v_col, dm, qs, ks, cd):
    sp = _scan_specs(lambda n: n, v_col)

    def body(qf, qr, kf, kr, vf, vr, dm_ref, qs_ref, ks_ref, cd_ref, of, orv, ssf, ssr, s_scr):
        @pl.when(pl.program_id(0) == 0)
        def _():
            s_scr[...] = jnp.zeros_like(s_scr)

        (qc, kc, vc), _ = _chain_loads([(qf, qr), (kf, kr), (vf, vr)], [])
        ss = [s_scr[i] for i in range(N_CHAIN)]
        consts = [[r[i] for i in range(N_CHAIN)] for r in (dm_ref, qs_ref, ks_ref, cd_ref)]
        o, s_new = _ret_chains(qc, kc, vc, *consts, ss)
        for i in range(N_CHAIN):
            d, h = i // 4, i % 4
            (ssf, ssr)[d][h] = ss[i]
            (of, orv)[d][:, h * 128:(h + 1) * 128] = o[i]
            s_scr[i] = s_new[i]

    return pl.pallas_call(
        body,
        grid=(N_CHUNK,),
        in_specs=_both(sp, ["tok", "tok", "tokv"]) + _ret_const_specs(),
        out_specs=_both(sp, ["tok", "state"]),
        out_shape=[jax.ShapeDtypeStruct((ROWS, 512), F32)] * 2 + [jax.ShapeDtypeStruct((N_CHUNK, 4, 128, 128), F32)] * 2,
        scratch_shapes=[pltpu.VMEM((N_CHAIN, 128, 128), F32)],
        compiler_params=_cparams(("arbitrary",)),
        name="ret_fwd",
    )(q, q, k, k, v, v, dm, qs, ks, cd)


def _ret_bwd_call(q, k, v, v_col, dm, qs, ks, cd, ssave, do):
    sp = _scan_specs(lambda n: N_CHUNK - 1 - n, v_col)

    def body(qf, qr, kf, kr, vf, vr, dm_ref, qs_ref, ks_ref, cd_ref, ssf, ssr, dof, dor,
             dqf, dqr, dkf, dkr, dvf, dvr, ddm_ref, dqs_ref, dks_ref, dcd_ref, ds_scr):
        @pl.when(pl.program_id(0) == 0)
        def _():
            ds_scr[...] = jnp.zeros_like(ds_scr)
            ddm_ref[...] = jnp.zeros_like(ddm_ref)
            dqs_ref[...] = jnp.zeros_like(dqs_ref)
            dks_ref[...] = jnp.zeros_like(dks_ref)
            dcd_ref[...] = jnp.zeros_like(dcd_ref)

        (qc, kc, vc, dos), (ss,) = _chain_loads([(qf, qr), (kf, kr), (vf, vr), (dof, dor)], [(ssf, ssr)])
        consts = [[r[i] for i in range(N_CHAIN)] for r in (dm_ref, qs_ref, ks_ref, cd_ref)]
        _, vjp = jax.vjp(_ret_chains, qc, kc, vc, *consts, ss)
        dq, dk, dv, ddm, dqs, dks, dcd, ds = vjp((dos, [ds_scr[i] for i in range(N_CHAIN)]))
        for i in range(N_CHAIN):
            d, h = i // 4, i % 4
            hs = slice(h * 128, (h + 1) * 128)
            (dqf, dqr)[d][:, hs] = dq[i]
            (dkf, dkr)[d][:, hs] = dk[i]
            (dvf, dvr)[d][:, hs] = dv[i]
            ddm_ref[i] += ddm[i]
            dqs_ref[i] += dqs[i]
            dks_ref[i] += dks[i]
            dcd_ref[i] += dcd[i]
            ds_scr[i] = ds[i]

    tok = jax.ShapeDtypeStruct((ROWS, 512), F32)
    return pl.pallas_call(
        body,
        grid=(N_CHUNK,),
        in_specs=_both(sp, ["tok", "tok", "tokv"]) + _ret_const_specs() + _both(sp, ["state", "tok"]),
        out_specs=_both(sp, ["tok", "tok", "tok"]) + _ret_const_specs(),
        out_shape=[tok] * 6 + [jax.ShapeDtypeStruct((N_CHAIN, CHUNK, CHUNK), F32), jax.ShapeDtypeStruct((N_CHAIN, CHUNK, 1), F32),
                               jax.ShapeDtypeStruct((N_CHAIN, CHUNK, 1), F32), jax.ShapeDtypeStruct((N_CHAIN, 1, 1), F32)],
        scratch_shapes=[pltpu.VMEM((N_CHAIN, 128, 128), F32)],
        compiler_params=_cparams(("arbitrary",)),
        name="ret_bwd",
    )(q, q, k, k, v, v, dm, qs, ks, cd, *ssave, do, do)


N_QBLK = ROWS // B_BLOCK
CTX_QBLK = CTX_LEN // B_BLOCK


def _attn_heads(q, kc, vc, kw, vw, sink, valid):
    n = range(len(q))
    scale = B_HD ** -0.5
    s_c = [_bdot(q[i], kc[i], 1, 1) * scale for i in n]
    s_w = [jnp.where(valid, _bdot(q[i], kw[i], 1, 1) * scale, NEG) for i in n]
    m = [lax.stop_gradient(jnp.maximum(jnp.maximum(jnp.max(s_c[i], axis=-1, keepdims=True), sink[i]),
                                       jnp.max(s_w[i], axis=-1, keepdims=True))) for i in n]
    e_c = [jnp.exp(s_c[i] - m[i]) for i in n]
    e_w = [jnp.exp(s_w[i] - m[i]) for i in n]
    den = [jnp.sum(e_c[i], axis=-1, keepdims=True) + jnp.sum(e_w[i], axis=-1, keepdims=True)
           + jnp.exp(sink[i] - m[i]) for i in n]
    return [(_bdot(e_c[i], vc[i], 1, 0) + _bdot(e_w[i], vw[i], 1, 0)) / den[i] for i in n]


def _attn_loads(q_ref, kv_ref, sink_ref, start):
    q, kc, vc, kw, vw, sink = [], [], [], [], [], []
    for hk in range(B_KV_HEADS):
        ks = slice(hk * B_HD, (hk + 1) * B_HD)
        vs = slice(128 + hk * B_HD, 128 + (hk + 1) * B_HD)
        grp = (kv_ref[0:CTX_LEN, ks], kv_ref[0:CTX_LEN, vs],
               kv_ref[pl.ds(start, 3 * B_BLOCK), ks], kv_ref[pl.ds(start, 3 * B_BLOCK), vs])
        for g in range(4):
            h = hk * 4 + g
            q.append(q_ref[:, h * B_HD:(h + 1) * B_HD])
            for lst, val in zip((kc, vc, kw, vw), grp):
                lst.append(val)
            sink.append(jnp.full((1, 1), sink_ref[h], F32))
    return q, kc, vc, kw, vw, sink


def _window(blk):
    xblk = blk - CTX_QBLK
    first = jnp.clip((xblk - 1) * B_BLOCK, 0, SEQ - 3 * B_BLOCK)
    qpos = xblk * B_BLOCK + lax.broadcasted_iota(jnp.int32, (B_BLOCK, 3 * B_BLOCK), 0)
    kpos = first + lax.broadcasted_iota(jnp.int32, (B_BLOCK, 3 * B_BLOCK), 1)
    far = jnp.where(blk >= CTX_QBLK, 0, 2 * SEQ)
    valid = jnp.abs(kpos - qpos) + far <= WINDOW
    return pl.multiple_of(first + CTX_LEN, B_BLOCK), valid


def _attn_specs():
    qspec = pl.BlockSpec((B_BLOCK, 512), lambda i: (i, 0))
    kvspec = pl.BlockSpec((ROWS, 256), lambda i: (0, 0))
    return qspec, kvspec, pl.BlockSpec(memory_space=pltpu.SMEM)


def _attn_fwd_call(q, kv, sink):
    def body(q_ref, kv_ref, sink_ref, o_ref):
        start, valid = _window(pl.program_id(0))
        out = _attn_heads(*_attn_loads(q_ref, kv_ref, sink_ref, start), valid)
        for h in range(B_Q_HEADS):
            o_ref[:, h * B_HD:(h + 1) * B_HD] = out[h]

    qspec, kvspec, sspec = _attn_specs()
    return pl.pallas_call(
        body,
        grid=(N_QBLK,),
        in_specs=[qspec, kvspec, sspec],
        out_specs=qspec,
        out_shape=jax.ShapeDtypeStruct((ROWS, 512), F32),
        compiler_params=_cparams(("arbitrary",)),
        name="attn_fwd",
    )(q, kv, sink)


def _attn_bwd_call(q, kv, sink, do):
    def body(q_ref, kv_ref, sink_ref, do_ref, dq_ref, dkv_ref, dsink_ref):
        @pl.when(pl.program_id(0) == 0)
        def _():
            dkv_ref[...] = jnp.zeros_like(dkv_ref)
            dsink_ref[...] = jnp.zeros_like(dsink_ref)

        start, valid = _window(pl.program_id(0))
        _, vjp = jax.vjp(functools.partial(_attn_heads, valid=valid), *_attn_loads(q_ref, kv_ref, sink_ref, start))
        dq, dkc, dvc, dkw, dvw, dsink = vjp([do_ref[:, h * B_HD:(h + 1) * B_HD] for h in range(B_Q_HEADS)])
        for h in range(B_Q_HEADS):
            dq_ref[:, h * B_HD:(h + 1) * B_HD] = dq[h]
            dsink_ref[h:h + 1, :] += jnp.broadcast_to(dsink[h], (1, 128))
        for hk in range(B_KV_HEADS):
            ks = slice(hk * B_HD, (hk + 1) * B_HD)
            vs = slice(128 + hk * B_HD, 128 + (hk + 1) * B_HD)
            grp = lambda parts: parts[hk * 4] + parts[hk * 4 + 1] + parts[hk * 4 + 2] + parts[hk * 4 + 3]
            dkv_ref[0:CTX_LEN, ks] += grp(dkc)
            dkv_ref[0:CTX_LEN, vs] += grp(dvc)
            dkv_ref[pl.ds(start, 3 * B_BLOCK), ks] += grp(dkw)
            dkv_ref[pl.ds(start, 3 * B_BLOCK), vs] += grp(dvw)

    qspec, kvspec, sspec = _attn_specs()
    return pl.pallas_call(
        body,
        grid=(N_QBLK,),
        in_specs=[qspec, kvspec, sspec, qspec],
        out_specs=[qspec, kvspec, pl.BlockSpec((8, 128), lambda i: (0, 0))],
        out_shape=[jax.ShapeDtypeStruct((ROWS, 512), F32), jax.ShapeDtypeStruct((ROWS, 256), F32),
                   jax.ShapeDtypeStruct((8, 128), F32)],
        compiler_params=_cparams(("arbitrary",)),
        name="attn_bwd",
    )(q, kv, sink, do)


def _my_id():
    return 4 * lax.axis_index("x") + 2 * lax.axis_index("y") + lax.axis_index("c")


def _peer(k):
    x, y, c = lax.axis_index("x"), lax.axis_index("y"), lax.axis_index("c")
    return (1 - x if k & 4 else x, 1 - y if k & 2 else y, 1 - c if k & 1 else c)


def _exchange(arrays, gather, name):
    n = len(arrays)

    def body(*refs):
        ins, outs = refs[:n], refs[n:2 * n]
        send_sems, recv_sems, local_sems = refs[2 * n:]
        me = _my_id()
        own, sent = [], []
        for a in range(n):
            cp = pltpu.make_async_copy(ins[a] if gather else ins[a].at[me], outs[a].at[me], local_sems.at[a])
            cp.start()
            own.append(cp)
            for k in range(1, N_DEV):
                src = ins[a] if gather else ins[a].at[jnp.bitwise_xor(me, k)]
                cp = pltpu.make_async_remote_copy(src_ref=src, dst_ref=outs[a].at[me],
                                                  send_sem=send_sems.at[a, k - 1], recv_sem=recv_sems.at[a, k - 1],
                                                  device_id=_peer(k), device_id_type=MESH)
                cp.start()
                sent.append(cp)
        for a in range(n):
            for k in range(1, N_DEV):
                src = ins[a] if gather else ins[a].at[jnp.bitwise_xor(me, k)]
                arrive = pltpu.make_async_remote_copy(src_ref=src, dst_ref=outs[a].at[jnp.bitwise_xor(me, k)],
                                                      send_sem=send_sems.at[a, k - 1],
                                                      recv_sem=recv_sems.at[a, k - 1],
                                                      device_id=_peer(k), device_id_type=MESH)
                arrive.wait_recv()
        for cp in sent:
            cp.wait_send()
        for cp in own:
            cp.wait()

    hbm = pl.BlockSpec(memory_space=pltpu.HBM)
    out_shape = [jax.ShapeDtypeStruct((N_DEV,) + (a.shape if gather else a.shape[1:]), a.dtype) for a in arrays]
    return pl.pallas_call(
        body,
        in_specs=[hbm] * n,
        out_specs=[hbm] * n,
        out_shape=out_shape,
        scratch_shapes=[pltpu.SemaphoreType.DMA((n, N_DEV - 1)), pltpu.SemaphoreType.DMA((n, N_DEV - 1)),
                        pltpu.SemaphoreType.DMA((n,))],
        compiler_params=pltpu.CompilerParams(has_side_effects=True),
        name=name,
    )(*arrays)


def _adamw(w, m, v, contrib, name):
    r, c = w.shape
    br = _pick(r, (256, 128, 64, 32, 16, 8))
    bc1 = 1.0 - ADAM_B1 ** ADAM_STEP
    bc2 = 1.0 - ADAM_B2 ** ADAM_STEP

    def body(w_ref, m_ref, v_ref, c_ref, g_ref, d_ref, nm_ref, nv_ref):
        g = c_ref[0].astype(F32)
        for j in range(1, N_DEV):
            g = g + c_ref[j].astype(F32)
        m_new = ADAM_B1 * m_ref[...] + (1.0 - ADAM_B1) * g
        v_new = ADAM_B2 * v_ref[...] + (1.0 - ADAM_B2) * (g * g)
        m_hat = m_new / bc1
        v_hat = v_new / bc2
        g_ref[...] = g
        d_ref[...] = -ADAM_LR * (m_hat / (jnp.sqrt(v_hat) + ADAM_EPS) + ADAM_WD * w_ref[...])
        nm_ref[...] = m_new
        nv_ref[...] = v_new

    spec = pl.BlockSpec((br, c), lambda i: (i, 0))
    cspec = pl.BlockSpec((N_DEV, br, c), lambda i: (0, i, 0))
    return pl.pallas_call(
        body,
        grid=(r // br,),
        in_specs=[spec, spec, spec, cspec],
        out_specs=[spec] * 4,
        out_shape=[jax.ShapeDtypeStruct((r, c), F32)] * 4,
        compiler_params=_cparams(("parallel",)),
        name=name,
    )(w, m, v, contrib)


def _silu(x):
    return x * jax.nn.sigmoid(x)


def _rope_angles(pos, n_freq):
    inv = ROPE_BASE ** (-jnp.arange(n_freq, dtype=F32) / n_freq)
    return pos[:, None] * inv[None, :]


def _with_ctx_rows(cos, sin):
    return (jnp.concatenate([jnp.ones((CTX_LEN, 128), F32), cos], axis=0),
            jnp.concatenate([jnp.zeros((CTX_LEN, 128), F32), sin], axis=0))


def _rope_tables():
    rows_n = SEQ // GRID_W
    rows = jnp.repeat(jnp.arange(rows_n, dtype=F32), GRID_W)
    cols = jnp.tile(jnp.arange(GRID_W, dtype=F32), rows_n)
    ang_r = _rope_angles(rows, B_HD // 4)
    ang_c = _rope_angles(cols, B_HD // 4)
    cos_b = jnp.tile(jnp.concatenate([jnp.cos(ang_r)] * 2 + [jnp.cos(ang_c)] * 2, axis=1), (1, 2))
    sin_b = jnp.tile(jnp.concatenate([-jnp.sin(ang_r), jnp.sin(ang_r), -jnp.sin(ang_c), jnp.sin(ang_c)], axis=1), (1, 2))
    ang = _rope_angles(jnp.arange(SEQ, dtype=F32), C_HD // 2)
    cos_c = jnp.concatenate([jnp.cos(ang)] * 2, axis=1)
    sin_c = jnp.concatenate([-jnp.sin(ang), jnp.sin(ang)], axis=1)
    return _with_ctx_rows(cos_b, sin_b), _with_ctx_rows(cos_c, sin_c)


def _halves(a):
    return a[:4], a[4:]


def _delta_gates(ab, a_log, dt_bias):
    beta = jax.nn.sigmoid(ab[:, :8])
    g = -jnp.exp(a_log)[None, :] * jax.nn.softplus(ab[:, 8:] + dt_bias[None, :])
    gch = g.reshape(N_CHUNK, CHUNK, 8)
    fwd = jnp.cumsum(gch[..., :4], axis=1)
    bwd = jnp.flip(jnp.cumsum(jnp.flip(gch[..., 4:], axis=1), axis=1), axis=1)
    gc = jnp.concatenate([fwd, bwd], axis=-1)
    gl = jnp.sum(gch, axis=1)
    rows = lambda a: _halves(a.transpose(2, 0, 1)[:, :, None, :])
    return rows(beta.reshape(N_CHUNK, CHUNK, 8)), rows(gc), _halves(gl.T[:, :, None, None])


def _ret_consts(c_decay):
    lg = jax.nn.log_sigmoid(c_decay)
    idx = jnp.arange(CHUNK, dtype=F32)
    diff = idx[:, None] - idx[None, :]
    lgf, lgb = lg[:4, None, None], lg[4:, None, None]
    dm = jnp.concatenate([jnp.exp(jnp.where(diff >= 0, diff * lgf, -jnp.inf)),
                          jnp.exp(jnp.where(diff <= 0, -diff * lgb, -jnp.inf))], axis=0)
    qs = jnp.concatenate([jnp.exp((idx + 1.0)[None, :] * lg[:4, None]),
                          jnp.exp((CHUNK - idx)[None, :] * lg[4:, None])], axis=0)[:, :, None]
    ks = jnp.concatenate([jnp.exp((CHUNK - 1.0 - idx)[None, :] * lg[:4, None]),
                          jnp.exp(idx[None, :] * lg[4:, None])], axis=0)[:, :, None]
    return dm, qs, ks, jnp.exp(CHUNK * lg)[:, None, None]


A_PIECES = ((0, True, A_DK ** -0.5, "a_q"), (1, True, 1.0, "a_k"), (2, False, 1.0, "a_v"))
B_ROPE_COLS = [C_BQ // 512, C_BKV // 256, 0, 0]
C_ROPE_COLS = [C_CQ // 512, C_CK // 512, 0, 0]
MERGE_COLS = [C_MERGE // 1024, C_MERGE // 1024 + 1, C_MERGE // 1024 + 2, 0, 0, 0]


def _conv8(conv_w):
    return jnp.pad(conv_w, ((0, 8 - A_CONV), (0, 0)))


W_IN_TILES = {"nn": (2176, 512, 1024), "nt": (1088, 1024, 2176), "tn": (1024, 2176, 544)}


def _core_forward(h, w16, p):
    proj = _matmul(h, w16, "w_in", "nn", W_IN_TILES["nn"])
    (cos_b, sin_b), (cos_c, sin_c) = _rope_tables()
    conv8 = _conv8(p["a_conv_w"])
    q, k, v = [_a_prep_fwd(proj, conv8, col, nrm, scl, nm) for col, nrm, scl, nm in A_PIECES]
    gates = _delta_gates(proj[:, C_AB:C_AB + 16], p["a_log"], p["a_dt_bias"])
    of, orv, ssf, ssr, tsf, tsr = _delta_fwd_call(q, k, v, *gates)
    (y_a,) = _a_out.fwd([(of, orv), proj], [p["a_norm_w"][None, :]], [0, C_AZ // 512])

    qb, kvb = _b_rope.fwd([proj, proj, cos_b, sin_b], [], B_ROPE_COLS)
    ob = _attn_fwd_call(qb, kvb, p["b_sink"])
    (y_b,) = _b_out.fwd([ob, proj], [], [0, C_BZ // 512])

    qc, kc = _c_rope.fwd([proj, proj, cos_c, sin_c], [], C_ROPE_COLS)
    cf, cr, csf, csr = _ret_fwd_call(qc, kc, proj, C_CV // 512, *_ret_consts(p["c_decay"]))
    (y_c,) = _c_out.fwd([(cf, cr), proj], [p["c_norm_w"][None, :]], [0, C_CZ // 512])

    wb = p["w_branch"].astype(BF16)
    pa, pb, pc = [_matmul(y, wb[i], "branch_" + "abc"[i]) for i, y in enumerate((y_a, y_b, y_c))]
    (merged,) = _merge.fwd([proj, proj, proj, pa, pb, pc], [], MERGE_COLS)
    saved = dict(proj=proj, q=q, k=k, v=v, of=of, orv=orv, ss=(ssf, ssr), ts=(tsf, tsr), qb=qb, kvb=kvb, ob=ob,
                 qc=qc, kc=kc, cf=cf, cr=cr, cs=(csf, csr), y=(y_a, y_b, y_c), pabc=(pa, pb, pc))
    return merged, saved


@jax.custom_vjp
def _layer_core(h, w16, w_grad_carrier, p):
    return _core_forward(h, w16, p)[0]


def _layer_core_fwd(h, w16, w_grad_carrier, p):
    merged, saved = _core_forward(h, w16, p)
    return merged, (h, w16, p, saved)


def _layer_core_bwd(res, dmerged):
    h, w16, p, s = res
    proj = s["proj"]
    (cos_b, sin_b), (cos_c, sin_c) = _rope_tables()
    conv8 = _conv8(p["a_conv_w"])
    wb = p["w_branch"].astype(BF16)
    y_a, y_b, y_c = s["y"]

    (dma, dmb, dmc, dpa, dpb, dpc), _ = _merge.bwd([proj, proj, proj, *s["pabc"]], [], [dmerged], MERGE_COLS)
    dy = [_matmul(d, wb[i], "branch_%s_da" % "abc"[i], "nt") for i, d in enumerate((dpa, dpb, dpc))]
    dwb = jnp.stack([_matmul(y, d, "branch_%s_db" % "abc"[i], "tn")
                     for i, (y, d) in enumerate(zip((y_a, y_b, y_c), (dpa, dpb, dpc)))])

    consts, consts_vjp = jax.vjp(_ret_consts, p["c_decay"])
    (do_c, dcz), (dcnw,) = _c_out.bwd([(s["cf"], s["cr"]), proj], [p["c_norm_w"][None, :]], [dy[2]], [0, C_CZ // 512])
    g = _ret_bwd_call(s["qc"], s["kc"], proj, C_CV // 512, *consts, s["cs"], do_c)
    (dcq, dck), _ = _c_rope.bwd([proj, proj, cos_c, sin_c], [], [(g[0], g[1]), (g[2], g[3])], C_ROPE_COLS)
    dcv = g[4] + g[5]
    (dc_decay,) = consts_vjp(tuple(g[6:10]))

    (dob, dbz), _ = _b_out.bwd([s["ob"], proj], [], [dy[1]], [0, C_BZ // 512])
    dqb, dkvb, dsink = _attn_bwd_call(s["qb"], s["kvb"], p["b_sink"], dob)
    (dbq, dbkv), _ = _b_rope.bwd([proj, proj, cos_b, sin_b], [], [dqb, dkvb], B_ROPE_COLS)

    ab = proj[:, C_AB:C_AB + 16]
    gates, gates_vjp = jax.vjp(_delta_gates, ab, p["a_log"], p["a_dt_bias"])
    (do_a, daz), (danw,) = _a_out.bwd([(s["of"], s["orv"]), proj], [p["a_norm_w"][None, :]], [dy[0]], [0, C_AZ // 512])
    g = _delta_bwd_call(s["q"], s["k"], s["v"], *gates, s["ss"], s["ts"], do_a)
    dgates = ((g[6], g[7]), (g[8], g[9]), (g[10], g[11]))
    dab, da_log, ddt = gates_vjp(dgates)
    dpre, dconv = [], []
    for (col, nrm, scl, nm), df, dr in zip(A_PIECES, (g[0], g[2], g[4]), (g[1], g[3], g[5])):
        dx, dw = _a_prep_bwd(proj, conv8, col, nrm, scl, df, dr, nm)
        dpre.append(dx)
        dconv.append(dw[:A_CONV])

    dproj = jnp.concatenate(dpre + [daz, dbq, dbz, dcq, dck, dcv, dcz, dma, dmb, dmc, dbkv,
                                    jnp.pad(dab, ((0, 0), (0, IN_PAD - C_AB - 16)))], axis=1).astype(BF16)
    dh = _matmul(dproj, w16, "w_in_da", "nt", W_IN_TILES["nt"])
    dw = _matmul(h, dproj, "w_in_db", "tn", W_IN_TILES["tn"])
    dp = dict(a_conv_w=jnp.concatenate(dconv, axis=1), a_log=da_log, a_dt_bias=ddt, a_norm_w=danw[0],
              b_sink=dsink[:, 0], c_decay=dc_decay, c_norm_w=dcnw[0], w_branch=dwb)
    return dh, jnp.zeros_like(w16), dw, dp


_layer_core.defvjp(_layer_core_fwd, _layer_core_bwd)
CORE_PARAMS = ("a_conv_w", "a_log", "a_dt_bias", "a_norm_w", "b_sink", "c_decay", "c_norm_w", "w_branch")


def _pad_w_in(w):
    return jnp.concatenate([w[..., 0:2048], w[..., 2064:2576], w[..., 2832:3344], w[..., 3344:8464],
                            w[..., 2576:2832], w[..., 2048:2064],
                            jnp.zeros(w.shape[:-1] + (IN_PAD - IN_WIDTH,), w.dtype)], axis=-1)


def _unpad_w_in(g):
    return jnp.concatenate([g[..., 0:2048], g[..., C_AB:C_AB + 16], g[..., C_BQ:C_BQ + 512],
                            g[..., C_BKV:C_BKV + 256], g[..., C_BZ:C_BZ + 512], g[..., C_CQ:C_BKV]], axis=-1)


def _local_loss(wts, w_in_carrier, x, c, ctx, loss_target, w_in16):
    sc16 = jnp.zeros((16, D_MODEL), F32).at[0].set(_silu(c)).at[1].set(_silu(wts["c_ctx"]))
    xs = jnp.concatenate([ctx, x], axis=0)
    for layer in range(DEPTH):
        last = layer == DEPTH - 1
        mod16 = _mm(sc16, wts["w_ada"][layer], "ada%d" % layer) + wts["b_ada"][layer][None, :]
        mod_cx = jnp.stack([mod16[1], mod16[0]])
        shift, scale, gate = jnp.split(mod_cx, 3, axis=1)
        (h,) = _norm_mod.call([xs], [wts["norm_w"][layer][None, :], shift, scale])
        merged = _layer_core(h, w_in16[layer], w_in_carrier[layer], {n: wts[n][layer] for n in CORE_PARAMS})
        if last:
            merged, xs, gate = merged[CTX_LEN:], xs[CTX_LEN:], gate[1:2]
        out = _mm(merged, wts["w_out"][layer], "w_out%d" % layer)
        (xs,) = _residual.call([xs, out], [gate])
    (per_row,) = _loss_rows.call([xs, loss_target], [wts["final_norm_w"][None, :]])
    return jnp.sum(per_row[:, 0])


SHARDED = ("w_ada", "w_in", "a_conv_w", "w_branch", "w_out")
MATMUL_ONLY = ("w_ada", "w_in", "w_branch", "w_out")
SMALL = ("c_ctx", "b_ada", "norm_w", "a_log", "a_dt_bias", "a_norm_w", "b_sink", "c_decay", "c_norm_w",
         "final_norm_w")
WEIGHTS = ("c_ctx", "w_ada", "b_ada", "norm_w", "w_in", "a_conv_w", "a_log", "a_dt_bias", "a_norm_w", "b_sink",
           "c_decay", "c_norm_w", "w_branch", "w_out", "final_norm_w")
SMALL_PACK = 12288


def _wire(name, a):
    return a.astype(BF16) if name in MATMUL_ONLY else a


def _unshard(name, g):
    if name == "w_branch":
        return g.transpose(1, 2, 3, 0, 4).reshape(DEPTH, 3, BR_WIDTH, D_MODEL)
    if name == "w_out":
        return g.transpose(1, 0, 2, 3).reshape(DEPTH, D_MODEL, D_MODEL)
    s = g.shape
    return g.transpose(1, 2, 0, 3).reshape(s[1], s[2], N_DEV * s[3])


def _reshard(name, w):
    if name == "w_branch":
        return w.reshape(DEPTH, 3, BR_WIDTH, N_DEV, D_MODEL // N_DEV).transpose(3, 0, 1, 2, 4)
    if name == "w_out":
        return w.reshape(DEPTH, N_DEV, D_MODEL // N_DEV, D_MODEL).transpose(1, 0, 2, 3)
    s = w.shape
    return w.reshape(s[0], s[1], N_DEV, s[2] // N_DEV).transpose(2, 0, 1, 3)


def _pack_small(tree):
    flat = jnp.concatenate([tree[n].reshape(-1) for n in SMALL])
    return jnp.pad(flat, (0, SMALL_PACK - flat.shape[0])).reshape(SMALL_PACK // 128, 128)


def _unpack_small(packed, like):
    flat = packed.reshape(-1)
    out, off = {}, 0
    for n in SMALL:
        size = math.prod(like[n].shape)
        out[n] = flat[off:off + size].reshape(like[n].shape)
        off += size
    return out


def kernel(x, c, ctx, c_ctx, w_ada, b_ada, norm_w, w_in, a_conv_w, a_log, a_dt_bias, a_norm_w, b_sink, c_decay, c_norm_w, w_branch, w_out, final_norm_w, loss_target, m_c_ctx, m_w_ada, m_b_ada, m_norm_w, m_w_in, m_a_conv_w, m_a_log, m_a_dt_bias, m_a_norm_w, m_b_sink, m_c_decay, m_c_norm_w, m_w_branch, m_w_out, m_final_norm_w, v_c_ctx, v_w_ada, v_b_ada, v_norm_w, v_w_in, v_a_conv_w, v_a_log, v_a_dt_bias, v_a_norm_w, v_b_sink, v_c_decay, v_c_norm_w, v_w_branch, v_w_out, v_final_norm_w):
    w = dict(c_ctx=c_ctx, w_ada=w_ada, b_ada=b_ada, norm_w=norm_w, w_in=w_in, a_conv_w=a_conv_w, a_log=a_log,
             a_dt_bias=a_dt_bias, a_norm_w=a_norm_w, b_sink=b_sink, c_decay=c_decay, c_norm_w=c_norm_w,
             w_branch=w_branch, w_out=w_out, final_norm_w=final_norm_w)
    m = dict(c_ctx=m_c_ctx, w_ada=m_w_ada, b_ada=m_b_ada, norm_w=m_norm_w, w_in=m_w_in, a_conv_w=m_a_conv_w,
             a_log=m_a_log, a_dt_bias=m_a_dt_bias, a_norm_w=m_a_norm_w, b_sink=m_b_sink, c_decay=m_c_decay,
             c_norm_w=m_c_norm_w, w_branch=m_w_branch, w_out=m_w_out, final_norm_w=m_final_norm_w)
    v = dict(c_ctx=v_c_ctx, w_ada=v_w_ada, b_ada=v_b_ada, norm_w=v_norm_w, w_in=v_w_in, a_conv_w=v_a_conv_w,
             a_log=v_a_log, a_dt_bias=v_a_dt_bias, a_norm_w=v_a_norm_w, b_sink=v_b_sink, c_decay=v_c_decay,
             c_norm_w=v_c_norm_w, w_branch=v_w_branch, w_out=v_w_out, final_norm_w=v_final_norm_w)

    gathered = _exchange([_wire(n, w[n]) for n in SHARDED], True, "gather_weights")
    full = {n: w[n] for n in SMALL}
    for n, g in zip(SHARDED, gathered):
        if n != "w_in":
            full[n] = _unshard(n, g).astype(F32)
    w_in16 = _pad_w_in(_unshard("w_in", gathered[SHARDED.index("w_in")]))
    carrier = jnp.zeros((DEPTH, D_MODEL, IN_PAD), F32)
    loss, (gw, g_in, gx) = jax.value_and_grad(_local_loss, argnums=(0, 1, 2))(
        full, carrier, x[0], c[0], ctx[0], loss_target[0], w_in16)
    gw["w_in"] = _unpad_w_in(g_in)
    loss = lax.psum(loss, ("x", "y", "c"))

    blocks = _exchange([_wire(n, _reshard(n, gw[n])) for n in SHARDED], False, "scatter_grads")
    small = _exchange([_pack_small(gw)], True, "gather_small_grads")[0]

    grad, delta, new_m, new_v = {}, {}, {}, {}
    for n, contrib in zip(SHARDED, blocks):
        shp = w[n].shape
        two_d = (math.prod(shp[:-1]), shp[-1])
        outs = _adamw(w[n].reshape(two_d), m[n].reshape(two_d), v[n].reshape(two_d),
                      contrib.reshape((N_DEV,) + two_d), "adamw_" + n)
        grad[n], delta[n], new_m[n], new_v[n] = [o.reshape(shp) for o in outs]
    outs = _adamw(_pack_small(w), _pack_small(m), _pack_small(v), small, "adamw_small")
    for tree, packed in zip((grad, delta, new_m, new_v), outs):
        tree.update(_unpack_small(packed, w))

    return (loss, gx[None], *[grad[n] for n in WEIGHTS], *[delta[n] for n in WEIGHTS],
            *[new_m[n] for n in WEIGHTS], *[new_v[n] for n in WEIGHTS])
```

```python
import functools
import math

import jax
import jax.numpy as jnp
from jax import lax
from jax.experimental import pallas as pl
from jax.experimental.pallas import tpu as pltpu

F32 = jnp.float32
BF16 = jnp.bfloat16
INV_PRECISION = lax.Precision.HIGH

D_MODEL = 1024
SEQ = 4096
DEPTH = 2
GRID_W = 64
CTX_LEN = 256
EPS = 1e-6
ROPE_BASE = 10000.0
BR_WIDTH = D_MODEL // 2
A_DK = 128
A_HEADS = 4
A_WIDTH = 512
A_CONV = 5
B_HD = 64
B_Q_HEADS = 8
B_KV_HEADS = 2
WINDOW = 128
B_BLOCK = 128
C_HD = 128
C_HEADS = 4
C_WIDTH = 512
CHUNK = 64
ADAM_LR = 0.001
ADAM_B1 = 0.9
ADAM_B2 = 0.999
ADAM_EPS = 1e-08
ADAM_WD = 0.01
ADAM_STEP = 10

N_DEV = 8
ROWS = CTX_LEN + SEQ
N_CHUNK = ROWS // CHUNK
N_CTX_CHUNK = CTX_LEN // CHUNK
IN_WIDTH = 8464
IN_PAD = 8704
NEG = -1e30

VMEM_LIMIT = 48 * 1024 * 1024
MESH = pl.DeviceIdType.MESH

C_AQ, C_AK, C_AV, C_AZ, C_BQ, C_BZ, C_CQ, C_CK, C_CV, C_CZ = (i * 512 for i in range(10))
C_MERGE = 5120
C_BKV = 8192
C_AB = 8448


def _cparams(sem=None):
    if sem is None:
        return pltpu.CompilerParams(vmem_limit_bytes=VMEM_LIMIT)
    return pltpu.CompilerParams(dimension_semantics=sem, vmem_limit_bytes=VMEM_LIMIT)


def _dg(a, b, ca, cb, prec=None):
    return lax.dot_general(a, b, (((ca,), (cb,)), ((), ())), preferred_element_type=F32, precision=prec)


@functools.partial(jax.custom_vjp, nondiff_argnums=(2, 3))
def _bdot(a, b, ca, cb):
    return _dg(a.astype(BF16), b.astype(BF16), ca, cb)


def _bdot_fwd(a, b, ca, cb):
    return _bdot(a, b, ca, cb), (a, b)


def _bdot_bwd(ca, cb, res, ct):
    a, b = res
    da = _bdot(ct, b, 1, 1 - cb) if ca == 1 else _bdot(b, ct, 1 - cb, 1)
    db = _bdot(a, ct, 1 - ca, 0) if cb == 0 else _bdot(ct, a, 0, 1 - ca)
    return da, db


_bdot.defvjp(_bdot_fwd, _bdot_bwd)


def _hdot(a, b):
    return _dg(a, b, 1, 0, INV_PRECISION)


def _k_silu(x):
    return x / (1.0 + jnp.exp(-x))


def _k_sigmoid(x):
    return 1.0 / (1.0 + jnp.exp(-x))


@jax.custom_vjp
def _swap64(x):
    return pltpu.roll(x, 64, 1)


_swap64.defvjp(lambda x: (pltpu.roll(x, 64, 1), None), lambda _, ct: (pltpu.roll(ct, 64, 1),))


def _swap16_impl(x):
    lane = lax.broadcasted_iota(jnp.int32, x.shape, 1)
    return jnp.where((lane & 16) == 0, pltpu.roll(x, 112, 1), pltpu.roll(x, 16, 1))


@jax.custom_vjp
def _swap16(x):
    return _swap16_impl(x)


_swap16.defvjp(lambda x: (_swap16_impl(x), None), lambda _, ct: (_swap16_impl(ct),))


def _pick(dim, prefs):
    for p in prefs:
        if dim % p == 0:
            return p
    return dim


def _matmul(a, b, name, mode="nn", tiles=None):
    ca, cb = {"nn": (1, 0), "nt": (1, 1), "tn": (0, 0)}[mode]
    m, k = a.shape[1 - ca], a.shape[ca]
    n = b.shape[1 - cb]
    if tiles is None:
        tiles = (_pick(m, (1088, 1024, 512, 256, 128)), _pick(n, (512, 256, 128)),
                 _pick(k, (1088, 1024, 512, 256, 128) if mode == "tn" else (2176, 2048, 1024, 512, 256, 128)))
    tm, tn, tk = tiles
    nk = k // tk
    a_spec = (pl.BlockSpec((tm, tk), lambda i, j, kk: (i, kk)) if ca == 1
              else pl.BlockSpec((tk, tm), lambda i, j, kk: (kk, i)))
    b_spec = (pl.BlockSpec((tk, tn), lambda i, j, kk: (kk, j)) if cb == 0
              else pl.BlockSpec((tn, tk), lambda i, j, kk: (j, kk)))

    def body(a_ref, b_ref, o_ref):
        part = _dg(a_ref[...].astype(BF16), b_ref[...].astype(BF16), ca, cb)
        if nk == 1:
            o_ref[...] = part
        else:
            kk = pl.program_id(2)

            @pl.when(kk == 0)
            def _():
                o_ref[...] = part

            @pl.when(kk > 0)
            def _():
                o_ref[...] += part

    return pl.pallas_call(
        body,
        grid=(m // tm, n // tn, nk),
        in_specs=[a_spec, b_spec],
        out_specs=pl.BlockSpec((tm, tn), lambda i, j, kk: (i, j)),
        out_shape=jax.ShapeDtypeStruct((m, n), F32),
        compiler_params=_cparams(("parallel", "parallel", "arbitrary")),
        name=name,
    )(a, b)


ROW_BLOCK = 256
ROW_VMEM_BUDGET = 16 * 1024 * 1024


def _pieces(val, pw):
    return [val[:, j * pw:(j + 1) * pw] for j in range(val.shape[1] // pw)]


def _flat(groups):
    arrays, sizes = [], []
    for g in groups:
        g = g if isinstance(g, (tuple, list)) else (g,)
        arrays += list(g)
        sizes.append(len(g))
    return arrays, sizes


def _regroup(refs, sizes):
    out, at = [], 0
    for n in sizes:
        val = refs[at][...]
        for r in refs[at + 1:at + n]:
            val = val + r[...]
        out.append(val)
        at += n
    return out


class _Rowwise:
    def __init__(self, fn, name, row_wpw, par_pw, out_wpw, n_diff=None):
        self.fn, self.name, self.row_wpw, self.par_pw, self.out_wpw = fn, name, row_wpw, par_pw, out_wpw
        self.n_diff = len(row_wpw) if n_diff is None else n_diff

        @jax.custom_vjp
        def call(rows, params):
            return self.fwd(rows, params)

        def call_fwd(rows, params):
            return self.fwd(rows, params), (rows, params)

        def call_bwd(res, douts):
            return self.bwd(res[0], res[1], douts)

        call.defvjp(call_fwd, call_bwd)
        self.call = call

    def _load(self, row_vals, par_refs, br, with_ctx):
        row = pl.program_id(0) * br + lax.broadcasted_iota(jnp.int32, (br, 1), 0)
        is_ctx = (row < (CTX_LEN if with_ctx else 0)).astype(F32)
        rows = [_pieces(v, pw) for v, (_, pw) in zip(row_vals, self.row_wpw)]
        pars = []
        for p, pw in zip(par_refs, self.par_pw):
            val = p[...] if p.shape[0] == 1 else is_ctx * p[0:1, :] + (1.0 - is_ctx) * p[1:2, :]
            pars.append(_pieces(val, pw))
        return rows, pars, is_ctx

    def _block_rows(self, n_rows, widths):
        for br in (1088, 1024, 544, 512, 272):
            if n_rows % br == 0 and 2 * 4 * br * sum(widths) <= ROW_VMEM_BUDGET:
                return br
        return ROW_BLOCK

    def _row_specs(self, br, sizes, cols):
        out = []
        for (w, _), n, c in zip(self.row_wpw, sizes, cols):
            out += [pl.BlockSpec((br, w), lambda i, c=c: (i, c))] * n
        return out

    def fwd(self, rows, params, cols=None):
        arrays, sizes = _flat(rows)
        cols = cols or [0] * len(rows)
        n_rows = arrays[0].shape[0]
        n_in = len(arrays)
        br = self._block_rows(n_rows, [w for (w, _), n in zip(self.row_wpw, sizes) for _ in range(n)]
                              + [w for w, _ in self.out_wpw])

        def body(*refs):
            r, p, _ = self._load(_regroup(refs[:n_in], sizes), refs[n_in:n_in + len(params)], br, n_rows == ROWS)
            for o_ref, pieces, (_, pw) in zip(refs[n_in + len(params):], self.fn(r, p), self.out_wpw):
                for j, piece in enumerate(pieces):
                    o_ref[:, j * pw:(j + 1) * pw] = piece

        return pl.pallas_call(
            body,
            grid=(n_rows // br,),
            in_specs=self._row_specs(br, sizes, cols) + [pl.BlockSpec(p.shape, lambda i: (0, 0)) for p in params],
            out_specs=[pl.BlockSpec((br, w), lambda i: (i, 0)) for w, _ in self.out_wpw],
            out_shape=[jax.ShapeDtypeStruct((n_rows, w), F32) for w, _ in self.out_wpw],
            compiler_params=_cparams(("parallel",)),
            name=self.name + "_fwd",
        )(*arrays, *params)

    def bwd(self, rows, params, douts, cols=None):
        arrays, sizes = _flat(rows)
        darrays, dsizes = _flat(douts)
        cols = cols or [0] * len(rows)
        n_rows = arrays[0].shape[0]
        n_in, n_par, n_dout, n_diff = len(arrays), len(params), len(darrays), self.n_diff
        br = self._block_rows(n_rows, [w for (w, _), n in zip(self.row_wpw, sizes) for _ in range(n)]
                              + [w for (w, _), n in zip(self.out_wpw, dsizes) for _ in range(n)]
                              + [w for w, _ in self.row_wpw[:n_diff]])

        def body(*refs):
            par_refs = refs[n_in:n_in + n_par]
            dout_refs = refs[n_in + n_par:n_in + n_par + n_dout]
            drow_refs = refs[n_in + n_par + n_dout:n_in + n_par + n_dout + n_diff]
            dpar_refs = refs[n_in + n_par + n_dout + n_diff:]

            @pl.when(pl.program_id(0) == 0)
            def _():
                for d in dpar_refs:
                    d[...] = jnp.zeros_like(d)

            r, p, is_ctx = self._load(_regroup(refs[:n_in], sizes), par_refs, br, n_rows == ROWS)
            cts = [_pieces(d, pw) for d, (_, pw) in zip(_regroup(dout_refs, dsizes), self.out_wpw)]
            fixed = r[n_diff:]
            _, vjp = jax.vjp(lambda rd, pp: self.fn(rd + fixed, pp), r[:n_diff], p)
            dr, dp = vjp(cts)
            for d_ref, pieces, (_, pw) in zip(drow_refs, dr, self.row_wpw):
                for j, piece in enumerate(pieces):
                    d_ref[:, j * pw:(j + 1) * pw] = piece
            for d_ref, pieces, pw in zip(dpar_refs, dp, self.par_pw):
                for j, piece in enumerate(pieces):
                    lanes = slice(j * pw, (j + 1) * pw)
                    if d_ref.shape[0] == 1:
                        d_ref[:, lanes] += piece
                    else:
                        d_ref[0:1, lanes] += jnp.sum(is_ctx * piece, axis=0, keepdims=True)
                        d_ref[1:2, lanes] += jnp.sum((1.0 - is_ctx) * piece, axis=0, keepdims=True)

        par_specs = [pl.BlockSpec(p.shape, lambda i: (0, 0)) for p in params]
        dout_specs = []
        for (w, _), n in zip(self.out_wpw, dsizes):
            dout_specs += [pl.BlockSpec((br, w), lambda i: (i, 0))] * n
        drow_w = [w for w, _ in self.row_wpw[:n_diff]]
        g = pl.pallas_call(
            body,
            grid=(n_rows // br,),
            in_specs=self._row_specs(br, sizes, cols) + par_specs + dout_specs,
            out_specs=[pl.BlockSpec((br, w), lambda i: (i, 0)) for w in drow_w] + par_specs,
            out_shape=[jax.ShapeDtypeStruct((n_rows, w), F32) for w in drow_w]
            + [jax.ShapeDtypeStruct(p.shape, F32) for p in params],
            compiler_params=_cparams(("arbitrary",)),
            name=self.name + "_bwd",
        )(*arrays, *params, *darrays)
        return list(g[:n_diff]), list(g[n_diff:])


def _fn_norm_mod(rows, pars):
    (x,), (nw,), (shift,), (scale,) = rows[0], pars[0], pars[1], pars[2]
    y = x * lax.rsqrt(jnp.mean(x * x, axis=-1, keepdims=True) + EPS) * nw
    return [[y * (1.0 + scale) + shift]]


def _fn_head_rms_gate(rows, pars):
    (w,) = pars[0]
    return [[o * lax.rsqrt(jnp.mean(o * o, axis=-1, keepdims=True) + EPS) * w * _k_silu(z)
             for o, z in zip(rows[0], rows[1])]]


def _fn_group_norm_gate(rows, pars):
    out = []
    for o, z, w in zip(rows[0], rows[1], pars[0]):
        mu = jnp.mean(o, axis=-1, keepdims=True)
        var = jnp.mean(jnp.square(o - mu), axis=-1, keepdims=True)
        out.append((o - mu) * lax.rsqrt(var + EPS) * w * _k_silu(z))
    return [out]


def _fn_gate(rows, pars):
    return [[o * _k_silu(z) for o, z in zip(rows[0], rows[1])]]


def _fn_merge(rows, pars):
    (ma,), (mb,), (mc,), (pa,), (pb,), (pc,) = rows
    return [[_k_sigmoid(ma) * pa + _k_sigmoid(mb) * pb + _k_sigmoid(mc) * pc]]


def _fn_residual(rows, pars):
    (res,), (out,), (gate,) = rows[0], rows[1], pars[0]
    return [[res + gate * out]]


def _fn_loss(rows, pars):
    (x,), (target,), (w,) = rows[0], rows[1], pars[0]
    y = x * lax.rsqrt(jnp.mean(x * x, axis=-1, keepdims=True) + EPS) * w
    per_row = 0.5 * jnp.mean(jnp.square(y - target), axis=-1, keepdims=True)
    return [[jnp.broadcast_to(per_row, (per_row.shape[0], 128))]]


def _fn_b_rope(rows, pars):
    q, (k, v), (cos,), (sin,) = rows
    rot = lambda x: x * cos + _swap16(x) * sin
    return [[rot(x) for x in q], [rot(k), v]]


def _fn_c_rope(rows, pars):
    q, k, (cos,), (sin,) = rows
    rot = lambda x: x * cos + _swap64(x) * sin
    return [[rot(x) for x in q], [rot(x) * (C_HD ** -0.5) for x in k]]


_norm_mod = _Rowwise(_fn_norm_mod, "norm_mod", [(D_MODEL, D_MODEL)], [D_MODEL] * 3, [(D_MODEL, D_MODEL)])
_residual = _Rowwise(_fn_residual, "residual", [(D_MODEL, D_MODEL)] * 2, [D_MODEL], [(D_MODEL, D_MODEL)])
_loss_rows = _Rowwise(_fn_loss, "loss", [(D_MODEL, D_MODEL)] * 2, [D_MODEL], [(128, 128)], n_diff=1)
_a_out = _Rowwise(_fn_head_rms_gate, "a_out", [(512, 128)] * 2, [128], [(512, 128)])
_c_out = _Rowwise(_fn_group_norm_gate, "c_out", [(512, 128)] * 2, [128], [(512, 128)])
_b_out = _Rowwise(_fn_gate, "b_out", [(512, 512)] * 2, [], [(512, 512)])
_merge = _Rowwise(_fn_merge, "merge", [(D_MODEL, D_MODEL)] * 6, [], [(D_MODEL, D_MODEL)])
_b_rope = _Rowwise(_fn_b_rope, "b_rope", [(512, 128), (256, 128), (128, 128), (128, 128)], [],
                   [(512, 128), (256, 128)], n_diff=2)
_c_rope = _Rowwise(_fn_c_rope, "c_rope", [(512, 128), (512, 128), (128, 128), (128, 128)], [],
                   [(512, 128), (512, 128)], n_diff=2)


HALO = 8
EXT = ROW_BLOCK + 2 * HALO


def _halo_specs(col, width=512):
    last = ROWS // HALO - 1
    per = ROW_BLOCK // HALO
    prev = pl.BlockSpec((HALO, width), lambda i: (jnp.maximum(i * per - 1, 0), col))
    cur = pl.BlockSpec((ROW_BLOCK, width), lambda i: (i, col))
    nxt = pl.BlockSpec((HALO, width), lambda i: (jnp.minimum((i + 1) * per, last), col))
    return [prev, cur, nxt]


def _extended(prev_ref, cur_ref, next_ref):
    i = pl.program_id(0)
    prev_ok = i >= 2
    next_ok = jnp.logical_and(i >= 1, i < ROWS // ROW_BLOCK - 1)
    return jnp.concatenate([jnp.where(prev_ok, prev_ref[...], 0.0), cur_ref[...],
                            jnp.where(next_ok, next_ref[...], 0.0)], axis=0)


def _conv_taps(x_ext, w_ref, flip):
    acc = None
    for j in range(A_CONV):
        shift = (j - 2) if flip else (2 - j)
        term = w_ref[j:j + 1, :] * pltpu.roll(x_ext, shift % EXT, 0)
        acc = term if acc is None else acc + term
    return acc


def _conv_post(pre_pieces, normalize, scale):
    out = []
    for p in pre_pieces:
        y = _k_silu(p)
        if normalize:
            y = y * lax.rsqrt(jnp.sum(y * y, axis=-1, keepdims=True) + EPS) * scale
        out.append(y)
    return out


def _a_prep_fwd(proj, conv8, col, normalize, scale, name):
    def body(prev_ref, cur_ref, next_ref, w_ref, o_ref):
        pre = _conv_taps(_extended(prev_ref, cur_ref, next_ref), w_ref, False)[HALO:HALO + ROW_BLOCK]
        for h, y in enumerate(_conv_post(_pieces(pre, 128), normalize, scale)):
            o_ref[:, h * 128:(h + 1) * 128] = y

    return pl.pallas_call(
        body,
        grid=(ROWS // ROW_BLOCK,),
        in_specs=_halo_specs(col) + [pl.BlockSpec((8, 512), lambda i: (0, col))],
        out_specs=pl.BlockSpec((ROW_BLOCK, 512), lambda i: (i, 0)),
        out_shape=jax.ShapeDtypeStruct((ROWS, 512), F32),
        compiler_params=_cparams(("parallel",)),
        name=name + "_fwd",
    )(proj, proj, proj, conv8)


def _a_prep_bwd(proj, conv8, col, normalize, scale, dout_f, dout_r, name):
    def body(xp, xc, xn, w_ref, fp, fc, fn_, rp, rc, rn, dx_ref, dw_ref):
        @pl.when(pl.program_id(0) == 0)
        def _():
            dw_ref[...] = jnp.zeros_like(dw_ref)

        x_ext = _extended(xp, xc, xn)
        dout = _extended(fp, fc, fn_) + _extended(rp, rc, rn)
        pre = _conv_taps(x_ext, w_ref, False)
        _, vjp = jax.vjp(lambda p: _conv_post(p, normalize, scale), _pieces(pre, 128))
        (dpre,) = vjp(_pieces(dout, 128))
        dpre = jnp.concatenate(dpre, axis=1)
        dx_ref[...] = _conv_taps(dpre, w_ref, True)[HALO:HALO + ROW_BLOCK]
        own = dpre[HALO:HALO + ROW_BLOCK]
        for j in range(A_CONV):
            shifted = pltpu.roll(x_ext, (2 - j) % EXT, 0)[HALO:HALO + ROW_BLOCK]
            dw_ref[j:j + 1, :] += jnp.sum(own * shifted, axis=0, keepdims=True)

    return pl.pallas_call(
        body,
        grid=(ROWS // ROW_BLOCK,),
        in_specs=_halo_specs(col) + [pl.BlockSpec((8, 512), lambda i: (0, col))] + _halo_specs(0) + _halo_specs(0),
        out_specs=[pl.BlockSpec((ROW_BLOCK, 512), lambda i: (i, 0)), pl.BlockSpec((8, 512), lambda i: (0, 0))],
        out_shape=[jax.ShapeDtypeStruct((ROWS, 512), F32), jax.ShapeDtypeStruct((8, 512), F32)],
        compiler_params=_cparams(("arbitrary",)),
        name=name + "_bwd",
    )(proj, proj, proj, conv8, dout_f, dout_f, dout_f, dout_r, dout_r, dout_r)


N_CHAIN = 8


def _rev_chunk(s):
    return jnp.where(s < N_CTX_CHUNK, N_CTX_CHUNK - 1 - s, N_CHUNK + N_CTX_CHUNK - 1 - s)


def _scan_specs(step_of, v_col=0):
    cf = step_of
    cr = lambda n: _rev_chunk(step_of(n))

    def pair(shape, index):
        return (pl.BlockSpec(shape, lambda n: index(cf(n))), pl.BlockSpec(shape, lambda n: index(cr(n))))

    return dict(
        tok=pair((CHUNK, 512), lambda c: (c, 0)),
        tokv=pair((CHUNK, 512), lambda c: (c, v_col)),
        col=pair((4, CHUNK, 1), lambda c: (0, c, 0)),
        row=pair((4, 1, 1, CHUNK), lambda c: (0, c, 0, 0)),
        one=pair((4, 1, 1, 1), lambda c: (0, c, 0, 0)),
        state=pair((None, 4, 128, 128), lambda c: (c, 0, 0, 0)),
        tinv=pair((None, 4, CHUNK, CHUNK), lambda c: (c, 0, 0, 0)),
    )


def _both(specs, kinds):
    out = []
    for kind in kinds:
        out += list(specs[kind])
    return out


def _chain_masks():
    ii = lax.broadcasted_iota(jnp.int32, (CHUNK, CHUNK), 0)
    jj = lax.broadcasted_iota(jnp.int32, (CHUNK, CHUNK), 1)
    eye = jnp.where(ii == jj, 1.0, 0.0).astype(F32)
    lower = (ii >= jj, ii > jj)
    upper = (ii <= jj, ii < jj)
    return [lower] * 4 + [upper] * 4, eye


def _tri_inv_all(ls, eye):
    xs = [eye - l for l in ls]
    ps = [_hdot(l, l) for l in ls]
    for i in range(5):
        xs = [x + _hdot(x, p) for x, p in zip(xs, ps)]
        if i < 4:
            ps = [_hdot(p, p) for p in ps]
    return xs


@jax.custom_vjp
def _inv_saved(l, x):
    return x


def _inv_saved_fwd(l, x):
    return x, x


def _inv_saved_bwd(x, dx):
    return -_bdot(x, _bdot(dx, x, 1, 1), 0, 0), jnp.zeros_like(x)


_inv_saved.defvjp(_inv_saved_fwd, _inv_saved_bwd)


def _delta_chains(q, k, v, beta_r, gcr, gl, s, masks, eye, tinv_saved):
    n = range(len(q))
    beta = [jnp.sum(eye * beta_r[i], axis=1, keepdims=True) for i in n]
    gcc = [jnp.sum(eye * gcr[i], axis=1, keepdims=True) for i in n]
    decay = [jnp.exp(jnp.where(masks[i][0], gcc[i] - gcr[i], NEG)) for i in n]
    kb = [k[i] * beta[i] for i in n]
    lmat = [jnp.where(masks[i][1], _bdot(kb[i], k[i], 1, 1) * decay[i], 0.0) for i in n]
    if tinv_saved is None:
        tinv = _tri_inv_all(lmat, eye)
    else:
        tinv = [_inv_saved(lmat[i], tinv_saved[i]) for i in n]
    eg = [jnp.exp(gcc[i]) for i in n]
    u = [_bdot(tinv[i], v[i] * beta[i], 1, 0) for i in n]
    w = [_bdot(tinv[i], kb[i] * eg[i], 1, 0) for i in n]
    qk = [_bdot(q[i], k[i], 1, 1) * decay[i] for i in n]
    v_new = [u[i] - _bdot(w[i], s[i], 1, 0) for i in n]
    o = [_bdot(q[i] * eg[i], s[i], 1, 0) + _bdot(qk[i], v_new[i], 1, 0) for i in n]
    s_new = [s[i] * jnp.exp(gl[i]) + _bdot(k[i] * jnp.exp(gl[i] - gcc[i]), v_new[i], 0, 0) for i in n]
    return (o, s_new), tinv


def _chain_loads(tok_pairs, small_pairs):
    toks = [[pair[i // 4][:, (i % 4) * 128:(i % 4 + 1) * 128] for i in range(N_CHAIN)] for pair in tok_pairs]
    smalls = [[pair[i // 4][i % 4] for i in range(N_CHAIN)] for pair in small_pairs]
    return toks, smalls


def _delta_fwd_call(q, k, v, beta, gc, gl):
    sp = _scan_specs(lambda n: n)

    def body(qf, qr, kf, kr, vf, vr, bf, br, gcrf, gcrr, glf, glr, of, orv, ssf, ssr, tsf, tsr, s_scr):
        @pl.when(pl.program_id(0) == 0)
        def _():
            s_scr[...] = jnp.zeros_like(s_scr)

        masks, eye = _chain_masks()
        (qs, ks, vs), _ = _chain_loads([(qf, qr), (kf, kr), (vf, vr)], [])
        bs = [(bf, br)[i // 4][i % 4, 0] for i in range(N_CHAIN)]
        gcrs = [(gcrf, gcrr)[i // 4][i % 4, 0] for i in range(N_CHAIN)]
        gls = [(glf, glr)[i // 4][i % 4, 0] for i in range(N_CHAIN)]
        ss = [s_scr[i] for i in range(N_CHAIN)]
        (o, s_new), tinv = _delta_chains(qs, ks, vs, bs, gcrs, gls, ss, masks, eye, None)
        for i in range(N_CHAIN):
            d, h = i // 4, i % 4
            (ssf, ssr)[d][h] = ss[i]
            (tsf, tsr)[d][h] = tinv[i]
            (of, orv)[d][:, h * 128:(h + 1) * 128] = o[i]
            s_scr[i] = s_new[i]

    return pl.pallas_call(
        body,
        grid=(N_CHUNK,),
        in_specs=_both(sp, ["tok", "tok", "tok", "row", "row", "one"]),
        out_specs=_both(sp, ["tok", "state", "tinv"]),
        out_shape=[jax.ShapeDtypeStruct((ROWS, 512), F32)] * 2
        + [jax.ShapeDtypeStruct((N_CHUNK, 4, 128, 128), F32)] * 2
        + [jax.ShapeDtypeStruct((N_CHUNK, 4, CHUNK, CHUNK), F32)] * 2,
        scratch_shapes=[pltpu.VMEM((N_CHAIN, 128, 128), F32)],
        compiler_params=_cparams(("arbitrary",)),
        name="delta_fwd",
    )(q, q, k, k, v, v, *beta, *gc, *gl)


def _delta_bwd_call(q, k, v, beta, gc, gl, ssave, tsave, do):
    sp = _scan_specs(lambda n: N_CHUNK - 1 - n)

    def body(qf, qr, kf, kr, vf, vr, bf, br, gcrf, gcrr, glf, glr, ssf, ssr, tsf, tsr, dof, dor,
             dqf, dqr, dkf, dkr, dvf, dvr, dbf, dbr, dgcrf, dgcrr, dglf, dglr, ds_scr):
        @pl.when(pl.program_id(0) == 0)
        def _():
            ds_scr[...] = jnp.zeros_like(ds_scr)

        masks, eye = _chain_masks()
        (qs, ks, vs, dos), (ss, ts) = _chain_loads(
            [(qf, qr), (kf, kr), (vf, vr), (dof, dor)], [(ssf, ssr), (tsf, tsr)])
        bs = [(bf, br)[i // 4][i % 4, 0] for i in range(N_CHAIN)]
        gcrs = [(gcrf, gcrr)[i // 4][i % 4, 0] for i in range(N_CHAIN)]
        gls = [(glf, glr)[i // 4][i % 4, 0] for i in range(N_CHAIN)]
        fn = lambda *a: _delta_chains(*a, masks, eye, ts)
        _, vjp, _ = jax.vjp(fn, qs, ks, vs, bs, gcrs, gls, ss, has_aux=True)
        dq, dk, dv, db, dgcr, dgl, ds = vjp((dos, [ds_scr[i] for i in range(N_CHAIN)]))
        for i in range(N_CHAIN):
            d, h = i // 4, i % 4
            hs = slice(h * 128, (h + 1) * 128)
            (dqf, dqr)[d][:, hs] = dq[i]
            (dkf, dkr)[d][:, hs] = dk[i]
            (dvf, dvr)[d][:, hs] = dv[i]
            (dbf, dbr)[d][h, 0] = db[i]
            (dgcrf, dgcrr)[d][h, 0] = dgcr[i]
            (dglf, dglr)[d][h, 0] = dgl[i]
            ds_scr[i] = ds[i]

    tok = jax.ShapeDtypeStruct((ROWS, 512), F32)
    return pl.pallas_call(
        body,
        grid=(N_CHUNK,),
        in_specs=_both(sp, ["tok", "tok", "tok", "row", "row", "one", "state", "tinv", "tok"]),
        out_specs=_both(sp, ["tok", "tok", "tok", "row", "row", "one"]),
        out_shape=[tok] * 6 + [jax.ShapeDtypeStruct((4, N_CHUNK, 1, CHUNK), F32)] * 4
        + [jax.ShapeDtypeStruct((4, N_CHUNK, 1, 1), F32)] * 2,
        scratch_shapes=[pltpu.VMEM((N_CHAIN, 128, 128), F32)],
        compiler_params=_cparams(("arbitrary",)),
        name="delta_bwd",
    )(q, q, k, k, v, v, *beta, *gc, *gl, *ssave, *tsave, do, do)


def _ret_chains(q, k, v, dm, qs, ks, cd, s):
    n = range(len(q))
    a = [_bdot(q[i], k[i], 1, 1) * dm[i] for i in n]
    o = [_bdot(a[i], v[i], 1, 0) + _bdot(q[i] * qs[i], s[i], 1, 0) for i in n]
    s_new = [s[i] * cd[i] + _bdot(k[i] * ks[i], v[i], 0, 0) for i in n]
    return o, s_new


def _ret_const_specs():
    return [pl.BlockSpec((N_CHAIN, CHUNK, CHUNK), lambda n: (0, 0, 0)), pl.BlockSpec((N_CHAIN, CHUNK, 1), lambda n: (0, 0, 0)),
            pl.BlockSpec((N_CHAIN, CHUNK, 1), lambda n: (0, 0, 0)), pl.BlockSpec((N_CHAIN, 1, 1), lambda n: (0, 0, 0))]


def _ret_fwd_call(q, k, v, v_col, dm, qs, ks, cd):
    sp = _scan_specs(lambda n: n, v_col)

    def body(qf, qr, kf, kr, vf, vr, dm_ref, qs_ref, ks_ref, cd_ref, of, orv, ssf, ssr, s_scr):
        @pl.when(pl.program_id(0) == 0)
        def _():
            s_scr[...] = jnp.zeros_like(s_scr)

        (qc, kc, vc), _ = _chain_loads([(qf, qr), (kf, kr), (vf, vr)], [])
        ss = [s_scr[i] for i in range(N_CHAIN)]
        consts = [[r[i] for i in range(N_CHAIN)] for r in (dm_ref, qs_ref, ks_ref, cd_ref)]
        o, s_new = _ret_chains(qc, kc, vc, *consts, ss)
        for i in range(N_CHAIN):
            d, h = i // 4, i % 4
            (ssf, ssr)[d][h] = ss[i]
            (of, orv)[d][:, h * 128:(h + 1) * 128] = o[i]
            s_scr[i] = s_new[i]

    return pl.pallas_call(
        body,
        grid=(N_CHUNK,),
        in_specs=_both(sp, ["tok", "tok", "tokv"]) + _ret_const_specs(),
        out_specs=_both(sp, ["tok", "state"]),
        out_shape=[jax.ShapeDtypeStruct((ROWS, 512), F32)] * 2 + [jax.ShapeDtypeStruct((N_CHUNK, 4, 128, 128), F32)] * 2,
        scratch_shapes=[pltpu.VMEM((N_CHAIN, 128, 128), F32)],
        compiler_params=_cparams(("arbitrary",)),
        name="ret_fwd",
    )(q, q, k, k, v, v, dm, qs, ks, cd)


def _ret_bwd_call(q, k, v, v_col, dm, qs, ks, cd, ssave, do):
    sp = _scan_specs(lambda n: N_CHUNK - 1 - n, v_col)

    def body(qf, qr, kf, kr, vf, vr, dm_ref, qs_ref, ks_ref, cd_ref, ssf, ssr, dof, dor,
             dqf, dqr, dkf, dkr, dvf, dvr, ddm_ref, dqs_ref, dks_ref, dcd_ref, ds_scr):
        @pl.when(pl.program_id(0) == 0)
        def _():
            ds_scr[...] = jnp.zeros_like(ds_scr)
            ddm_ref[...] = jnp.zeros_like(ddm_ref)
            dqs_ref[...] = jnp.zeros_like(dqs_ref)
            dks_ref[...] = jnp.zeros_like(dks_ref)
            dcd_ref[...] = jnp.zeros_like(dcd_ref)

        (qc, kc, vc, dos), (ss,) = _chain_loads([(qf, qr), (kf, kr), (vf, vr), (dof, dor)], [(ssf, ssr)])
        consts = [[r[i] for i in range(N_CHAIN)] for r in (dm_ref, qs_ref, ks_ref, cd_ref)]
        _, vjp = jax.vjp(_ret_chains, qc, kc, vc, *consts, ss)
        dq, dk, dv, ddm, dqs, dks, dcd, ds = vjp((dos, [ds_scr[i] for i in range(N_CHAIN)]))
        for i in range(N_CHAIN):
            d, h = i // 4, i % 4
            hs = slice(h * 128, (h + 1) * 128)
            (dqf, dqr)[d][:, hs] = dq[i]
            (dkf, dkr)[d][:, hs] = dk[i]
            (dvf, dvr)[d][:, hs] = dv[i]
            ddm_ref[i] += ddm[i]
            dqs_ref[i] += dqs[i]
            dks_ref[i] += dks[i]
            dcd_ref[i] += dcd[i]
            ds_scr[i] = ds[i]

    tok = jax.ShapeDtypeStruct((ROWS, 512), F32)
    return pl.pallas_call(
        body,
        grid=(N_CHUNK,),
        in_specs=_both(sp, ["tok", "tok", "tokv"]) + _ret_const_specs() + _both(sp, ["state", "tok"]),
        out_specs=_both(sp, ["tok", "tok", "tok"]) + _ret_const_specs(),
        out_shape=[tok] * 6 + [jax.ShapeDtypeStruct((N_CHAIN, CHUNK, CHUNK), F32), jax.ShapeDtypeStruct((N_CHAIN, CHUNK, 1), F32),
                               jax.ShapeDtypeStruct((N_CHAIN, CHUNK, 1), F32), jax.ShapeDtypeStruct((N_CHAIN, 1, 1), F32)],
        scratch_shapes=[pltpu.VMEM((N_CHAIN, 128, 128), F32)],
        compiler_params=_cparams(("arbitrary",)),
        name="ret_bwd",
    )(q, q, k, k, v, v, dm, qs, ks, cd, *ssave, do, do)


N_QBLK = ROWS // B_BLOCK
CTX_QBLK = CTX_LEN // B_BLOCK


def _attn_heads(q, kc, vc, kw, vw, sink, valid):
    n = range(len(q))
    scale = B_HD ** -0.5
    s_c = [_bdot(q[i], kc[i], 1, 1) * scale for i in n]
    s_w = [jnp.where(valid, _bdot(q[i], kw[i], 1, 1) * scale, NEG) for i in n]
    m = [lax.stop_gradient(jnp.maximum(jnp.maximum(jnp.max(s_c[i], axis=-1, keepdims=True), sink[i]),
                                       jnp.max(s_w[i], axis=-1, keepdims=True))) for i in n]
    e_c = [jnp.exp(s_c[i] - m[i]) for i in n]
    e_w = [jnp.exp(s_w[i] - m[i]) for i in n]
    den = [jnp.sum(e_c[i], axis=-1, keepdims=True) + jnp.sum(e_w[i], axis=-1, keepdims=True)
           + jnp.exp(sink[i] - m[i]) for i in n]
    return [(_bdot(e_c[i], vc[i], 1, 0) + _bdot(e_w[i], vw[i], 1, 0)) / den[i] for i in n]


def _attn_loads(q_ref, kv_ref, sink_ref, start):
    q, kc, vc, kw, vw, sink = [], [], [], [], [], []
    for hk in range(B_KV_HEADS):
        ks = slice(hk * B_HD, (hk + 1) * B_HD)
        vs = slice(128 + hk * B_HD, 128 + (hk + 1) * B_HD)
        grp = (kv_ref[0:CTX_LEN, ks], kv_ref[0:CTX_LEN, vs],
               kv_ref[pl.ds(start, 3 * B_BLOCK), ks], kv_ref[pl.ds(start, 3 * B_BLOCK), vs])
        for g in range(4):
            h = hk * 4 + g
            q.append(q_ref[:, h * B_HD:(h + 1) * B_HD])
            for lst, val in zip((kc, vc, kw, vw), grp):
                lst.append(val)
            sink.append(jnp.full((1, 1), sink_ref[h], F32))
    return q, kc, vc, kw, vw, sink


def _window(blk):
    xblk = blk - CTX_QBLK
    first = jnp.clip((xblk - 1) * B_BLOCK, 0, SEQ - 3 * B_BLOCK)
    qpos = xblk * B_BLOCK + lax.broadcasted_iota(jnp.int32, (B_BLOCK, 3 * B_BLOCK), 0)
    kpos = first + lax.broadcasted_iota(jnp.int32, (B_BLOCK, 3 * B_BLOCK), 1)
    far = jnp.where(blk >= CTX_QBLK, 0, 2 * SEQ)
    valid = jnp.abs(kpos - qpos) + far <= WINDOW
    return pl.multiple_of(first + CTX_LEN, B_BLOCK), valid


def _attn_specs():
    qspec = pl.BlockSpec((B_BLOCK, 512), lambda i: (i, 0))
    kvspec = pl.BlockSpec((ROWS, 256), lambda i: (0, 0))
    return qspec, kvspec, pl.BlockSpec(memory_space=pltpu.SMEM)


def _attn_fwd_call(q, kv, sink):
    def body(q_ref, kv_ref, sink_ref, o_ref):
        start, valid = _window(pl.program_id(0))
        out = _attn_heads(*_attn_loads(q_ref, kv_ref, sink_ref, start), valid)
        for h in range(B_Q_HEADS):
            o_ref[:, h * B_HD:(h + 1) * B_HD] = out[h]

    qspec, kvspec, sspec = _attn_specs()
    return pl.pallas_call(
        body,
        grid=(N_QBLK,),
        in_specs=[qspec, kvspec, sspec],
        out_specs=qspec,
        out_shape=jax.ShapeDtypeStruct((ROWS, 512), F32),
        compiler_params=_cparams(("arbitrary",)),
        name="attn_fwd",
    )(q, kv, sink)


def _attn_bwd_call(q, kv, sink, do):
    def body(q_ref, kv_ref, sink_ref, do_ref, dq_ref, dkv_ref, dsink_ref):
        @pl.when(pl.program_id(0) == 0)
        def _():
            dkv_ref[...] = jnp.zeros_like(dkv_ref)
            dsink_ref[...] = jnp.zeros_like(dsink_ref)

        start, valid = _window(pl.program_id(0))
        _, vjp = jax.vjp(functools.partial(_attn_heads, valid=valid), *_attn_loads(q_ref, kv_ref, sink_ref, start))
        dq, dkc, dvc, dkw, dvw, dsink = vjp([do_ref[:, h * B_HD:(h + 1) * B_HD] for h in range(B_Q_HEADS)])
        for h in range(B_Q_HEADS):
            dq_ref[:, h * B_HD:(h + 1) * B_HD] = dq[h]
            dsink_ref[h:h + 1, :] += jnp.broadcast_to(dsink[h], (1, 128))
        for hk in range(B_KV_HEADS):
            ks = slice(hk * B_HD, (hk + 1) * B_HD)
            vs = slice(128 + hk * B_HD, 128 + (hk + 1) * B_HD)
            grp = lambda parts: parts[hk * 4] + parts[hk * 4 + 1] + parts[hk * 4 + 2] + parts[hk * 4 + 3]
            dkv_ref[0:CTX_LEN, ks] += grp(dkc)
            dkv_ref[0:CTX_LEN, vs] += grp(dvc)
            dkv_ref[pl.ds(start, 3 * B_BLOCK), ks] += grp(dkw)
            dkv_ref[pl.ds(start, 3 * B_BLOCK), vs] += grp(dvw)

    qspec, kvspec, sspec = _attn_specs()
    return pl.pallas_call(
        body,
        grid=(N_QBLK,),
        in_specs=[qspec, kvspec, sspec, qspec],
        out_specs=[qspec, kvspec, pl.BlockSpec((8, 128), lambda i: (0, 0))],
        out_shape=[jax.ShapeDtypeStruct((ROWS, 512), F32), jax.ShapeDtypeStruct((ROWS, 256), F32),
                   jax.ShapeDtypeStruct((8, 128), F32)],
        compiler_params=_cparams(("arbitrary",)),
        name="attn_bwd",
    )(q, kv, sink, do)


def _my_id():
    return 4 * lax.axis_index("x") + 2 * lax.axis_index("y") + lax.axis_index("c")


def _peer(k):
    x, y, c = lax.axis_index("x"), lax.axis_index("y"), lax.axis_index("c")
    return (1 - x if k & 4 else x, 1 - y if k & 2 else y, 1 - c if k & 1 else c)


def _exchange(arrays, gather, name):
    n = len(arrays)

    def body(*refs):
        ins, outs = refs[:n], refs[n:2 * n]
        send_sems, recv_sems, local_sems = refs[2 * n:]
        me = _my_id()
        own, sent = [], []
        for a in range(n):
            cp = pltpu.make_async_copy(ins[a] if gather else ins[a].at[me], outs[a].at[me], local_sems.at[a])
            cp.start()
            own.append(cp)
            for k in range(1, N_DEV):
                src = ins[a] if gather else ins[a].at[jnp.bitwise_xor(me, k)]
                cp = pltpu.make_async_remote_copy(src_ref=src, dst_ref=outs[a].at[me],
                                                  send_sem=send_sems.at[a, k - 1], recv_sem=recv_sems.at[a, k - 1],
                                                  device_id=_peer(k), device_id_type=MESH)
                cp.start()
                sent.append(cp)
        for a in range(n):
            for k in range(1, N_DEV):
                src = ins[a] if gather else ins[a].at[jnp.bitwise_xor(me, k)]
                arrive = pltpu.make_async_remote_copy(src_ref=src, dst_ref=outs[a].at[jnp.bitwise_xor(me, k)],
                                                      send_sem=send_sems.at[a, k - 1],
                                                      recv_sem=recv_sems.at[a, k - 1],
                                                      device_id=_peer(k), device_id_type=MESH)
                arrive.wait_recv()
        for cp in sent:
            cp.wait_send()
        for cp in own:
            cp.wait()

    hbm = pl.BlockSpec(memory_space=pltpu.HBM)
    out_shape = [jax.ShapeDtypeStruct((N_DEV,) + (a.shape if gather else a.shape[1:]), a.dtype) for a in arrays]
    return pl.pallas_call(
        body,
        in_specs=[hbm] * n,
        out_specs=[hbm] * n,
        out_shape=out_shape,
        scratch_shapes=[pltpu.SemaphoreType.DMA((n, N_DEV - 1)), pltpu.SemaphoreType.DMA((n, N_DEV - 1)),
                        pltpu.SemaphoreType.DMA((n,))],
        compiler_params=pltpu.CompilerParams(has_side_effects=True),
        name=name,
    )(*arrays)


def _adamw(w, m, v, contrib, name):
    r, c = w.shape
    br = _pick(r, (256, 128, 64, 32, 16, 8))
    bc1 = 1.0 - ADAM_B1 ** ADAM_STEP
    bc2 = 1.0 - ADAM_B2 ** ADAM_STEP

    def body(w_ref, m_ref, v_ref, c_ref, g_ref, d_ref, nm_ref, nv_ref):
        g = c_ref[0].astype(F32)
        for j in range(1, N_DEV):
            g = g + c_ref[j].astype(F32)
        m_new = ADAM_B1 * m_ref[...] + (1.0 - ADAM_B1) * g
        v_new = ADAM_B2 * v_ref[...] + (1.0 - ADAM_B2) * (g * g)
        m_hat = m_new / bc1
        v_hat = v_new / bc2
        g_ref[...] = g
        d_ref[...] = -ADAM_LR * (m_hat / (jnp.sqrt(v_hat) + ADAM_EPS) + ADAM_WD * w_ref[...])
        nm_ref[...] = m_new
        nv_ref[...] = v_new

    spec = pl.BlockSpec((br, c), lambda i: (i, 0))
    cspec = pl.BlockSpec((N_DEV, br, c), lambda i: (0, i, 0))
    return pl.pallas_call(
        body,
        grid=(r // br,),
        in_specs=[spec, spec, spec, cspec],
        out_specs=[spec] * 4,
        out_shape=[jax.ShapeDtypeStruct((r, c), F32)] * 4,
        compiler_params=_cparams(("parallel",)),
        name=name,
    )(w, m, v, contrib)


def _silu(x):
    return x * jax.nn.sigmoid(x)


def _rope_angles(pos, n_freq):
    inv = ROPE_BASE ** (-jnp.arange(n_freq, dtype=F32) / n_freq)
    return pos[:, None] * inv[None, :]


def _with_ctx_rows(cos, sin):
    return (jnp.concatenate([jnp.ones((CTX_LEN, 128), F32), cos], axis=0),
            jnp.concatenate([jnp.zeros((CTX_LEN, 128), F32), sin], axis=0))


def _rope_tables():
    rows_n = SEQ // GRID_W
    rows = jnp.repeat(jnp.arange(rows_n, dtype=F32), GRID_W)
    cols = jnp.tile(jnp.arange(GRID_W, dtype=F32), rows_n)
    ang_r = _rope_angles(rows, B_HD // 4)
    ang_c = _rope_angles(cols, B_HD // 4)
    cos_b = jnp.tile(jnp.concatenate([jnp.cos(ang_r)] * 2 + [jnp.cos(ang_c)] * 2, axis=1), (1, 2))
    sin_b = jnp.tile(jnp.concatenate([-jnp.sin(ang_r), jnp.sin(ang_r), -jnp.sin(ang_c), jnp.sin(ang_c)], axis=1), (1, 2))
    ang = _rope_angles(jnp.arange(SEQ, dtype=F32), C_HD // 2)
    cos_c = jnp.concatenate([jnp.cos(ang)] * 2, axis=1)
    sin_c = jnp.concatenate([-jnp.sin(ang), jnp.sin(ang)], axis=1)
    return _with_ctx_rows(cos_b, sin_b), _with_ctx_rows(cos_c, sin_c)


def _halves(a):
    return a[:4], a[4:]


def _delta_gates(ab, a_log, dt_bias):
    beta = jax.nn.sigmoid(ab[:, :8])
    g = -jnp.exp(a_log)[None, :] * jax.nn.softplus(ab[:, 8:] + dt_bias[None, :])
    gch = g.reshape(N_CHUNK, CHUNK, 8)
    fwd = jnp.cumsum(gch[..., :4], axis=1)
    bwd = jnp.flip(jnp.cumsum(jnp.flip(gch[..., 4:], axis=1), axis=1), axis=1)
    gc = jnp.concatenate([fwd, bwd], axis=-1)
    gl = jnp.sum(gch, axis=1)
    rows = lambda a: _halves(a.transpose(2, 0, 1)[:, :, None, :])
    return rows(beta.reshape(N_CHUNK, CHUNK, 8)), rows(gc), _halves(gl.T[:, :, None, None])


def _ret_consts(c_decay):
    lg = jax.nn.log_sigmoid(c_decay)
    idx = jnp.arange(CHUNK, dtype=F32)
    diff = idx[:, None] - idx[None, :]
    lgf, lgb = lg[:4, None, None], lg[4:, None, None]
    dm = jnp.concatenate([jnp.exp(jnp.where(diff >= 0, diff * lgf, -jnp.inf)),
                          jnp.exp(jnp.where(diff <= 0, -diff * lgb, -jnp.inf))], axis=0)
    qs = jnp.concatenate([jnp.exp((idx + 1.0)[None, :] * lg[:4, None]),
                          jnp.exp((CHUNK - idx)[None, :] * lg[4:, None])], axis=0)[:, :, None]
    ks = jnp.concatenate([jnp.exp((CHUNK - 1.0 - idx)[None, :] * lg[:4, None]),
                          jnp.exp(idx[None, :] * lg[4:, None])], axis=0)[:, :, None]
    return dm, qs, ks, jnp.exp(CHUNK * lg)[:, None, None]


A_PIECES = ((0, True, A_DK ** -0.5, "a_q"), (1, True, 1.0, "a_k"), (2, False, 1.0, "a_v"))
B_ROPE_COLS = [C_BQ // 512, C_BKV // 256, 0, 0]
C_ROPE_COLS = [C_CQ // 512, C_CK // 512, 0, 0]
MERGE_COLS = [C_MERGE // 1024, C_MERGE // 1024 + 1, C_MERGE // 1024 + 2, 0, 0, 0]


def _conv8(conv_w):
    return jnp.pad(conv_w, ((0, 8 - A_CONV), (0, 0)))


W_IN_TILES = {"nn": (2176, 512, 1024), "nt": (1088, 1024, 2176), "db": (1024, 2176, 256)}


def _core_forward(h, w16, p):
    proj = _matmul(h, w16, "w_in", "nn", W_IN_TILES["nn"])
    (cos_b, sin_b), (cos_c, sin_c) = _rope_tables()
    conv8 = _conv8(p["a_conv_w"])
    q, k, v = [_a_prep_fwd(proj, conv8, col, nrm, scl, nm) for col, nrm, scl, nm in A_PIECES]
    gates = _delta_gates(proj[:, C_AB:C_AB + 16], p["a_log"], p["a_dt_bias"])
    of, orv, ssf, ssr, tsf, tsr = _delta_fwd_call(q, k, v, *gates)
    (y_a,) = _a_out.fwd([(of, orv), proj], [p["a_norm_w"][None, :]], [0, C_AZ // 512])

    qb, kvb = _b_rope.fwd([proj, proj, cos_b, sin_b], [], B_ROPE_COLS)
    ob = _attn_fwd_call(qb, kvb, p["b_sink"])
    (y_b,) = _b_out.fwd([ob, proj], [], [0, C_BZ // 512])

    qc, kc = _c_rope.fwd([proj, proj, cos_c, sin_c], [], C_ROPE_COLS)
    cf, cr, csf, csr = _ret_fwd_call(qc, kc, proj, C_CV // 512, *_ret_consts(p["c_decay"]))
    (y_c,) = _c_out.fwd([(cf, cr), proj], [p["c_norm_w"][None, :]], [0, C_CZ // 512])

    wb = p["w_branch"].astype(BF16)
    pa, pb, pc = [_matmul(y, wb[i], "branch_" + "abc"[i]) for i, y in enumerate((y_a, y_b, y_c))]
    (merged,) = _merge.fwd([proj, proj, proj, pa, pb, pc], [], MERGE_COLS)
    saved = dict(proj=proj, q=q, k=k, v=v, of=of, orv=orv, ss=(ssf, ssr), ts=(tsf, tsr), qb=qb, kvb=kvb, ob=ob,
                 qc=qc, kc=kc, cf=cf, cr=cr, cs=(csf, csr), y=(y_a, y_b, y_c), pabc=(pa, pb, pc))
    return merged, saved


def _core_backward(h, w16, p, s, dmerged):
    proj = s["proj"]
    (cos_b, sin_b), (cos_c, sin_c) = _rope_tables()
    conv8 = _conv8(p["a_conv_w"])
    wb = p["w_branch"].astype(BF16)
    y_a, y_b, y_c = s["y"]

    (dma, dmb, dmc, dpa, dpb, dpc), _ = _merge.bwd([proj, proj, proj, *s["pabc"]], [], [dmerged], MERGE_COLS)
    dy = [_matmul(d, wb[i], "branch_%s_da" % "abc"[i], "nt") for i, d in enumerate((dpa, dpb, dpc))]
    dwb = jnp.stack([_matmul(y, d, "branch_%s_db" % "abc"[i], "tn")
                     for i, (y, d) in enumerate(zip((y_a, y_b, y_c), (dpa, dpb, dpc)))])

    consts, consts_vjp = jax.vjp(_ret_consts, p["c_decay"])
    (do_c, dcz), (dcnw,) = _c_out.bwd([(s["cf"], s["cr"]), proj], [p["c_norm_w"][None, :]], [dy[2]], [0, C_CZ // 512])
    g = _ret_bwd_call(s["qc"], s["kc"], proj, C_CV // 512, *consts, s["cs"], do_c)
    (dcq, dck), _ = _c_rope.bwd([proj, proj, cos_c, sin_c], [], [(g[0], g[1]), (g[2], g[3])], C_ROPE_COLS)
    dcv = g[4] + g[5]
    (dc_decay,) = consts_vjp(tuple(g[6:10]))

    (dob, dbz), _ = _b_out.bwd([s["ob"], proj], [], [dy[1]], [0, C_BZ // 512])
    dqb, dkvb, dsink = _attn_bwd_call(s["qb"], s["kvb"], p["b_sink"], dob)
    (dbq, dbkv), _ = _b_rope.bwd([proj, proj, cos_b, sin_b], [], [dqb, dkvb], B_ROPE_COLS)

    ab = proj[:, C_AB:C_AB + 16]
    gates, gates_vjp = jax.vjp(_delta_gates, ab, p["a_log"], p["a_dt_bias"])
    (do_a, daz), (danw,) = _a_out.bwd([(s["of"], s["orv"]), proj], [p["a_norm_w"][None, :]], [dy[0]], [0, C_AZ // 512])
    g = _delta_bwd_call(s["q"], s["k"], s["v"], *gates, s["ss"], s["ts"], do_a)
    dgates = ((g[6], g[7]), (g[8], g[9]), (g[10], g[11]))
    dab, da_log, ddt = gates_vjp(dgates)
    dpre, dconv = [], []
    for (col, nrm, scl, nm), df, dr in zip(A_PIECES, (g[0], g[2], g[4]), (g[1], g[3], g[5])):
        dx, dw = _a_prep_bwd(proj, conv8, col, nrm, scl, df, dr, nm)
        dpre.append(dx)
        dconv.append(dw[:A_CONV])

    dproj = jnp.concatenate(dpre + [daz, dbq, dbz, dcq, dck, dcv, dcz, dma, dmb, dmc, dbkv,
                                    jnp.pad(dab, ((0, 0), (0, IN_PAD - C_AB - 16)))], axis=1).astype(BF16)
    dh = _matmul(dproj, w16, "w_in_da", "nt", W_IN_TILES["nt"])
    dw = _matmul(h.T.astype(BF16), dproj, "w_in_db", "nn", W_IN_TILES["db"])
    dp = dict(a_conv_w=jnp.concatenate(dconv, axis=1), a_log=da_log, a_dt_bias=ddt, a_norm_w=danw[0],
              b_sink=dsink[:, 0], c_decay=dc_decay, c_norm_w=dcnw[0], w_branch=dwb)
    return dh, dw, dp


CORE_PARAMS = ("a_conv_w", "a_log", "a_dt_bias", "a_norm_w", "b_sink", "c_decay", "c_norm_w", "w_branch")


def _pad_w_in(w):
    return jnp.concatenate([w[..., 0:2048], w[..., 2064:2576], w[..., 2832:3344], w[..., 3344:8464],
                            w[..., 2576:2832], w[..., 2048:2064],
                            jnp.zeros(w.shape[:-1] + (IN_PAD - IN_WIDTH,), w.dtype)], axis=-1)


def _unpad_w_in(g):
    return jnp.concatenate([g[..., 0:2048], g[..., C_AB:C_AB + 16], g[..., C_BQ:C_BQ + 512],
                            g[..., C_BKV:C_BKV + 256], g[..., C_BZ:C_BZ + 512], g[..., C_CQ:C_BKV]], axis=-1)


def _forward_backward(wts, w_in16, x, c, ctx, loss_target):
    c_ctx = wts["c_ctx"]
    sc16 = jnp.zeros((16, D_MODEL), F32).at[0].set(_silu(c)).at[1].set(_silu(c_ctx))
    xs = jnp.concatenate([ctx, x], axis=0)
    layers = []
    for l in range(DEPTH):
        last = l == DEPTH - 1
        mod16 = _matmul(sc16, wts["w_ada"][l], "ada") + wts["b_ada"][l][None, :]
        mod_cx = jnp.stack([mod16[1], mod16[0]])
        shift, scale, gate = jnp.split(mod_cx, 3, axis=1)
        nw = wts["norm_w"][l][None, :]
        (h,) = _norm_mod.fwd([xs], [nw, shift, scale])
        p = {n: wts[n][l] for n in CORE_PARAMS}
        merged, saved = _core_forward(h, w_in16[l], p)
        xs_in = xs
        if last:
            merged, xs_in, gate = merged[CTX_LEN:], xs[CTX_LEN:], gate[1:2]
        out = _matmul(merged, wts["w_out"][l], "w_out")
        (xs_next,) = _residual.fwd([xs_in, out], [gate])
        layers.append(dict(xs=xs, h=h, p=p, saved=saved, merged=merged, xs_in=xs_in, gate=gate, out=out, nw=nw,
                           shift=shift, scale=scale))
        xs = xs_next
    fw = wts["final_norm_w"][None, :]
    (per_row,) = _loss_rows.fwd([xs, loss_target], [fw])
    loss = jnp.sum(per_row[:, 0])

    d_per_row = jnp.zeros((SEQ, 128), F32).at[:, 0].set(1.0)
    (dxs,), (dfw,) = _loss_rows.bwd([xs, loss_target], [fw], [d_per_row])
    per_layer = {n: [None] * DEPTH for n in CORE_PARAMS + ("w_in", "w_out", "w_ada", "b_ada", "norm_w")}
    dsc16 = jnp.zeros((16, D_MODEL), F32)
    for l in reversed(range(DEPTH)):
        s = layers[l]
        (dres, dout), (dgate,) = _residual.bwd([s["xs_in"], s["out"]], [s["gate"]], [dxs])
        dmerged = _matmul(dout, wts["w_out"][l], "w_out_da", "nt")
        per_layer["w_out"][l] = _matmul(s["merged"], dout, "w_out_db", "tn")
        if l == DEPTH - 1:
            ctx_rows = ((CTX_LEN, 0), (0, 0))
            dmerged, dres = jnp.pad(dmerged, ctx_rows), jnp.pad(dres, ctx_rows)
            dgate = jnp.pad(dgate, ((1, 0), (0, 0)))
        dh, per_layer["w_in"][l], dp = _core_backward(s["h"], w_in16[l], s["p"], s["saved"], dmerged)
        for n in CORE_PARAMS:
            per_layer[n][l] = dp[n]
        (dxn,), (dnw, dshift, dscale) = _norm_mod.bwd([s["xs"]], [s["nw"], s["shift"], s["scale"]], [dh])
        dxs = dres + dxn
        dmod_cx = jnp.concatenate([dshift, dscale, dgate], axis=1)
        dmod16 = jnp.zeros((16, 3 * D_MODEL), F32).at[0].set(dmod_cx[1]).at[1].set(dmod_cx[0])
        per_layer["norm_w"][l] = dnw[0]
        per_layer["b_ada"][l] = dmod_cx[0] + dmod_cx[1]
        per_layer["w_ada"][l] = _matmul(sc16, dmod16, "ada_db", "tn")
        dsc16 = dsc16 + _matmul(dmod16, wts["w_ada"][l], "ada_da", "nt")
    grads = {n: jnp.stack(v) for n, v in per_layer.items()}
    grads["w_in"] = _unpad_w_in(grads["w_in"])
    grads["final_norm_w"] = dfw[0]
    sig = jax.nn.sigmoid(c_ctx)
    grads["c_ctx"] = dsc16[1] * sig * (1.0 + c_ctx * (1.0 - sig))
    return loss, dxs[CTX_LEN:], grads


SHARDED = ("w_ada", "w_in", "a_conv_w", "w_branch", "w_out")
MATMUL_ONLY = ("w_ada", "w_in", "w_branch", "w_out")
SMALL = ("c_ctx", "b_ada", "norm_w", "a_log", "a_dt_bias", "a_norm_w", "b_sink", "c_decay", "c_norm_w",
         "final_norm_w")
WEIGHTS = ("c_ctx", "w_ada", "b_ada", "norm_w", "w_in", "a_conv_w", "a_log", "a_dt_bias", "a_norm_w", "b_sink",
           "c_decay", "c_norm_w", "w_branch", "w_out", "final_norm_w")
SMALL_PACK = 12288


def _wire(name, a):
    return a.astype(BF16) if name in MATMUL_ONLY else a


def _unshard(name, g):
    if name == "w_branch":
        return g.transpose(1, 2, 3, 0, 4).reshape(DEPTH, 3, BR_WIDTH, D_MODEL)
    if name == "w_out":
        return g.transpose(1, 0, 2, 3).reshape(DEPTH, D_MODEL, D_MODEL)
    s = g.shape
    return g.transpose(1, 2, 0, 3).reshape(s[1], s[2], N_DEV * s[3])


def _reshard(name, w):
    if name == "w_branch":
        return w.reshape(DEPTH, 3, BR_WIDTH, N_DEV, D_MODEL // N_DEV).transpose(3, 0, 1, 2, 4)
    if name == "w_out":
        return w.reshape(DEPTH, N_DEV, D_MODEL // N_DEV, D_MODEL).transpose(1, 0, 2, 3)
    s = w.shape
    return w.reshape(s[0], s[1], N_DEV, s[2] // N_DEV).transpose(2, 0, 1, 3)


def _pack_small(tree):
    flat = jnp.concatenate([tree[n].reshape(-1) for n in SMALL])
    return jnp.pad(flat, (0, SMALL_PACK - flat.shape[0])).reshape(SMALL_PACK // 128, 128)


def _unpack_small(packed, like):
    flat = packed.reshape(-1)
    out, off = {}, 0
    for n in SMALL:
        size = math.prod(like[n].shape)
        out[n] = flat[off:off + size].reshape(like[n].shape)
        off += size
    return out


def kernel(x, c, ctx, c_ctx, w_ada, b_ada, norm_w, w_in, a_conv_w, a_log, a_dt_bias, a_norm_w, b_sink, c_decay, c_norm_w, w_branch, w_out, final_norm_w, loss_target, m_c_ctx, m_w_ada, m_b_ada, m_norm_w, m_w_in, m_a_conv_w, m_a_log, m_a_dt_bias, m_a_norm_w, m_b_sink, m_c_decay, m_c_norm_w, m_w_branch, m_w_out, m_final_norm_w, v_c_ctx, v_w_ada, v_b_ada, v_norm_w, v_w_in, v_a_conv_w, v_a_log, v_a_dt_bias, v_a_norm_w, v_b_sink, v_c_decay, v_c_norm_w, v_w_branch, v_w_out, v_final_norm_w):
    w = dict(c_ctx=c_ctx, w_ada=w_ada, b_ada=b_ada, norm_w=norm_w, w_in=w_in, a_conv_w=a_conv_w, a_log=a_log,
             a_dt_bias=a_dt_bias, a_norm_w=a_norm_w, b_sink=b_sink, c_decay=c_decay, c_norm_w=c_norm_w,
             w_branch=w_branch, w_out=w_out, final_norm_w=final_norm_w)
    m = dict(c_ctx=m_c_ctx, w_ada=m_w_ada, b_ada=m_b_ada, norm_w=m_norm_w, w_in=m_w_in, a_conv_w=m_a_conv_w,
             a_log=m_a_log, a_dt_bias=m_a_dt_bias, a_norm_w=m_a_norm_w, b_sink=m_b_sink, c_decay=m_c_decay,
             c_norm_w=m_c_norm_w, w_branch=m_w_branch, w_out=m_w_out, final_norm_w=m_final_norm_w)
    v = dict(c_ctx=v_c_ctx, w_ada=v_w_ada, b_ada=v_b_ada, norm_w=v_norm_w, w_in=v_w_in, a_conv_w=v_a_conv_w,
             a_log=v_a_log, a_dt_bias=v_a_dt_bias, a_norm_w=v_a_norm_w, b_sink=v_b_sink, c_decay=v_c_decay,
             c_norm_w=v_c_norm_w, w_branch=v_w_branch, w_out=v_w_out, final_norm_w=v_final_norm_w)

    gathered = _exchange([_wire(n, w[n]) for n in SHARDED], True, "gather_weights")
    full = {n: w[n] for n in SMALL}
    for n, g in zip(SHARDED, gathered):
        if n != "w_in":
            full[n] = _unshard(n, g)
    w_in16 = _pad_w_in(_unshard("w_in", gathered[SHARDED.index("w_in")]))
    loss, gx, gw = _forward_backward(full, w_in16, x[0], c[0], ctx[0], loss_target[0])
    loss = lax.psum(loss, ("x", "y", "c"))

    blocks = _exchange([_wire(n, _reshard(n, gw[n])) for n in SHARDED], False, "scatter_grads")
    small = _exchange([_pack_small(gw)], True, "gather_small_grads")[0]

    grad, delta, new_m, new_v = {}, {}, {}, {}
    for n, contrib in zip(SHARDED, blocks):
        shp = w[n].shape
        two_d = (math.prod(shp[:-1]), shp[-1])
        outs = _adamw(w[n].reshape(two_d), m[n].reshape(two_d), v[n].reshape(two_d),
                      contrib.reshape((N_DEV,) + two_d), "adamw_" + n)
        grad[n], delta[n], new_m[n], new_v[n] = [o.reshape(shp) for o in outs]
    outs = _adamw(_pack_small(w), _pack_small(m), _pack_small(v), small, "adamw_small")
    for tree, packed in zip((grad, delta, new_m, new_v), outs):
        tree.update(_unpack_small(packed, w))

    return (loss, gx[None], *[grad[n] for n in WEIGHTS], *[delta[n] for n in WEIGHTS],
            *[new_m[n] for n in WEIGHTS], *[new_v[n] for n in WEIGHTS])
```

```python
import functools
import math

import jax
import jax.numpy as jnp
from jax import lax
from jax.experimental import pallas as pl
from jax.experimental.pallas import tpu as pltpu

F32 = jnp.float32
BF16 = jnp.bfloat16
INV_PRECISION = lax.Precision.HIGH

D_MODEL = 1024
SEQ = 4096
DEPTH = 2
GRID_W = 64
CTX_LEN = 256
EPS = 1e-6
ROPE_BASE = 10000.0
BR_WIDTH = D_MODEL // 2
A_DK = 128
A_HEADS = 4
A_WIDTH = 512
A_CONV = 5
B_HD = 64
B_Q_HEADS = 8
B_KV_HEADS = 2
WINDOW = 128
B_BLOCK = 128
C_HD = 128
C_HEADS = 4
C_WIDTH = 512
CHUNK = 64
ADAM_LR = 0.001
ADAM_B1 = 0.9
ADAM_B2 = 0.999
ADAM_EPS = 1e-08
ADAM_WD = 0.01
ADAM_STEP = 10

N_DEV = 8
ROWS = CTX_LEN + SEQ
N_CHUNK = ROWS // CHUNK
N_CTX_CHUNK = CTX_LEN // CHUNK
IN_WIDTH = 8464
IN_PAD = 8704
NEG = -1e30

VMEM_LIMIT = 48 * 1024 * 1024
MESH = pl.DeviceIdType.MESH

C_AQ, C_AK, C_AV, C_AZ, C_BQ, C_BZ, C_CQ, C_CK, C_CV, C_CZ = (i * 512 for i in range(10))
C_MERGE = 5120
C_BKV = 8192
C_AB = 8448


def _cparams(sem=None):
    if sem is None:
        return pltpu.CompilerParams(vmem_limit_bytes=VMEM_LIMIT)
    return pltpu.CompilerParams(dimension_semantics=sem, vmem_limit_bytes=VMEM_LIMIT)


def _dg(a, b, ca, cb, prec=None):
    return lax.dot_general(a, b, (((ca,), (cb,)), ((), ())), preferred_element_type=F32, precision=prec)


@functools.partial(jax.custom_vjp, nondiff_argnums=(2, 3))
def _bdot(a, b, ca, cb):
    return _dg(a.astype(BF16), b.astype(BF16), ca, cb)


def _bdot_fwd(a, b, ca, cb):
    return _bdot(a, b, ca, cb), (a, b)


def _bdot_bwd(ca, cb, res, ct):
    a, b = res
    da = _bdot(ct, b, 1, 1 - cb) if ca == 1 else _bdot(b, ct, 1 - cb, 1)
    db = _bdot(a, ct, 1 - ca, 0) if cb == 0 else _bdot(ct, a, 0, 1 - ca)
    return da, db


_bdot.defvjp(_bdot_fwd, _bdot_bwd)


def _hdot(a, b):
    return _dg(a, b, 1, 0, INV_PRECISION)


def _k_silu(x):
    return x / (1.0 + jnp.exp(-x))


def _k_sigmoid(x):
    return 1.0 / (1.0 + jnp.exp(-x))


@jax.custom_vjp
def _swap64(x):
    return pltpu.roll(x, 64, 1)


_swap64.defvjp(lambda x: (pltpu.roll(x, 64, 1), None), lambda _, ct: (pltpu.roll(ct, 64, 1),))


def _swap16_impl(x):
    lane = lax.broadcasted_iota(jnp.int32, x.shape, 1)
    return jnp.where((lane & 16) == 0, pltpu.roll(x, 112, 1), pltpu.roll(x, 16, 1))


@jax.custom_vjp
def _swap16(x):
    return _swap16_impl(x)


_swap16.defvjp(lambda x: (_swap16_impl(x), None), lambda _, ct: (_swap16_impl(ct),))


def _pick(dim, prefs):
    for p in prefs:
        if dim % p == 0:
            return p
    return dim


def _matmul(a, b, name, mode="nn", tiles=None):
    ca, cb = {"nn": (1, 0), "nt": (1, 1), "tn": (0, 0)}[mode]
    m, k = a.shape[1 - ca], a.shape[ca]
    n = b.shape[1 - cb]
    if tiles is None:
        tiles = (_pick(m, (1088, 1024, 512, 256, 128)), _pick(n, (512, 256, 128)),
                 _pick(k, (1088, 1024, 512, 256, 128) if mode == "tn" else (2176, 2048, 1024, 512, 256, 128)))
    tm, tn, tk = tiles
    nk = k // tk
    a_spec = (pl.BlockSpec((tm, tk), lambda i, j, kk: (i, kk)) if ca == 1
              else pl.BlockSpec((tk, tm), lambda i, j, kk: (kk, i)))
    b_spec = (pl.BlockSpec((tk, tn), lambda i, j, kk: (kk, j)) if cb == 0
              else pl.BlockSpec((tn, tk), lambda i, j, kk: (j, kk)))

    def body(a_ref, b_ref, o_ref):
        part = _dg(a_ref[...].astype(BF16), b_ref[...].astype(BF16), ca, cb)
        if nk == 1:
            o_ref[...] = part
        else:
            kk = pl.program_id(2)

            @pl.when(kk == 0)
            def _():
                o_ref[...] = part

            @pl.when(kk > 0)
            def _():
                o_ref[...] += part

    return pl.pallas_call(
        body,
        grid=(m // tm, n // tn, nk),
        in_specs=[a_spec, b_spec],
        out_specs=pl.BlockSpec((tm, tn), lambda i, j, kk: (i, j)),
        out_shape=jax.ShapeDtypeStruct((m, n), F32),
        compiler_params=_cparams(("parallel", "parallel", "arbitrary")),
        name=name,
    )(a, b)


ROW_BLOCK = 256
ROW_VMEM_BUDGET = 16 * 1024 * 1024


def _pieces(val, pw):
    return [val[:, j * pw:(j + 1) * pw] for j in range(val.shape[1] // pw)]


def _flat(groups):
    arrays, sizes = [], []
    for g in groups:
        g = g if isinstance(g, (tuple, list)) else (g,)
        arrays += list(g)
        sizes.append(len(g))
    return arrays, sizes


def _regroup(refs, sizes):
    out, at = [], 0
    for n in sizes:
        val = refs[at][...]
        for r in refs[at + 1:at + n]:
            val = val + r[...]
        out.append(val)
        at += n
    return out


class _Rowwise:
    def __init__(self, fn, name, row_wpw, par_pw, out_wpw, n_diff=None):
        self.fn, self.name, self.row_wpw, self.par_pw, self.out_wpw = fn, name, row_wpw, par_pw, out_wpw
        self.n_diff = len(row_wpw) if n_diff is None else n_diff

        @jax.custom_vjp
        def call(rows, params):
            return self.fwd(rows, params)

        def call_fwd(rows, params):
            return self.fwd(rows, params), (rows, params)

        def call_bwd(res, douts):
            return self.bwd(res[0], res[1], douts)

        call.defvjp(call_fwd, call_bwd)
        self.call = call

    def _load(self, row_vals, par_refs, br, with_ctx):
        row = pl.program_id(0) * br + lax.broadcasted_iota(jnp.int32, (br, 1), 0)
        is_ctx = (row < (CTX_LEN if with_ctx else 0)).astype(F32)
        rows = [_pieces(v, pw) for v, (_, pw) in zip(row_vals, self.row_wpw)]
        pars = []
        for p, pw in zip(par_refs, self.par_pw):
            val = p[...] if p.shape[0] == 1 else is_ctx * p[0:1, :] + (1.0 - is_ctx) * p[1:2, :]
            pars.append(_pieces(val, pw))
        return rows, pars, is_ctx

    def _block_rows(self, n_rows, widths):
        for br in (1088, 1024, 544, 512, 272):
            if n_rows % br == 0 and 2 * 4 * br * sum(widths) <= ROW_VMEM_BUDGET:
                return br
        return ROW_BLOCK

    def _row_specs(self, br, sizes, cols):
        out = []
        for (w, _), n, c in zip(self.row_wpw, sizes, cols):
            out += [pl.BlockSpec((br, w), lambda i, c=c: (i, c))] * n
        return out

    def fwd(self, rows, params, cols=None):
        arrays, sizes = _flat(rows)
        cols = cols or [0] * len(rows)
        n_rows = arrays[0].shape[0]
        n_in = len(arrays)
        br = self._block_rows(n_rows, [w for (w, _), n in zip(self.row_wpw, sizes) for _ in range(n)]
                              + [w for w, _ in self.out_wpw])

        def body(*refs):
            r, p, _ = self._load(_regroup(refs[:n_in], sizes), refs[n_in:n_in + len(params)], br, n_rows == ROWS)
            for o_ref, pieces, (_, pw) in zip(refs[n_in + len(params):], self.fn(r, p), self.out_wpw):
                for j, piece in enumerate(pieces):
                    o_ref[:, j * pw:(j + 1) * pw] = piece

        return pl.pallas_call(
            body,
            grid=(n_rows // br,),
            in_specs=self._row_specs(br, sizes, cols) + [pl.BlockSpec(p.shape, lambda i: (0, 0)) for p in params],
            out_specs=[pl.BlockSpec((br, w), lambda i: (i, 0)) for w, _ in self.out_wpw],
            out_shape=[jax.ShapeDtypeStruct((n_rows, w), F32) for w, _ in self.out_wpw],
            compiler_params=_cparams(("parallel",)),
            name=self.name + "_fwd",
        )(*arrays, *params)

    def bwd(self, rows, params, douts, cols=None):
        arrays, sizes = _flat(rows)
        darrays, dsizes = _flat(douts)
        cols = cols or [0] * len(rows)
        n_rows = arrays[0].shape[0]
        n_in, n_par, n_dout, n_diff = len(arrays), len(params), len(darrays), self.n_diff
        br = self._block_rows(n_rows, [w for (w, _), n in zip(self.row_wpw, sizes) for _ in range(n)]
                              + [w for (w, _), n in zip(self.out_wpw, dsizes) for _ in range(n)]
                              + [w for w, _ in self.row_wpw[:n_diff]])

        def body(*refs):
            par_refs = refs[n_in:n_in + n_par]
            dout_refs = refs[n_in + n_par:n_in + n_par + n_dout]
            drow_refs = refs[n_in + n_par + n_dout:n_in + n_par + n_dout + n_diff]
            dpar_refs = refs[n_in + n_par + n_dout + n_diff:]

            @pl.when(pl.program_id(0) == 0)
            def _():
                for d in dpar_refs:
                    d[...] = jnp.zeros_like(d)

            r, p, is_ctx = self._load(_regroup(refs[:n_in], sizes), par_refs, br, n_rows == ROWS)
            cts = [_pieces(d, pw) for d, (_, pw) in zip(_regroup(dout_refs, dsizes), self.out_wpw)]
            fixed = r[n_diff:]
            _, vjp = jax.vjp(lambda rd, pp: self.fn(rd + fixed, pp), r[:n_diff], p)
            dr, dp = vjp(cts)
            for d_ref, pieces, (_, pw) in zip(drow_refs, dr, self.row_wpw):
                for j, piece in enumerate(pieces):
                    d_ref[:, j * pw:(j + 1) * pw] = piece
            for d_ref, pieces, pw in zip(dpar_refs, dp, self.par_pw):
                for j, piece in enumerate(pieces):
                    lanes = slice(j * pw, (j + 1) * pw)
                    if d_ref.shape[0] == 1:
                        d_ref[:, lanes] += piece
                    else:
                        d_ref[0:1, lanes] += jnp.sum(is_ctx * piece, axis=0, keepdims=True)
                        d_ref[1:2, lanes] += jnp.sum((1.0 - is_ctx) * piece, axis=0, keepdims=True)

        par_specs = [pl.BlockSpec(p.shape, lambda i: (0, 0)) for p in params]
        dout_specs = []
        for (w, _), n in zip(self.out_wpw, dsizes):
            dout_specs += [pl.BlockSpec((br, w), lambda i: (i, 0))] * n
        drow_w = [w for w, _ in self.row_wpw[:n_diff]]
        g = pl.pallas_call(
            body,
            grid=(n_rows // br,),
            in_specs=self._row_specs(br, sizes, cols) + par_specs + dout_specs,
            out_specs=[pl.BlockSpec((br, w), lambda i: (i, 0)) for w in drow_w] + par_specs,
            out_shape=[jax.ShapeDtypeStruct((n_rows, w), F32) for w in drow_w]
            + [jax.ShapeDtypeStruct(p.shape, F32) for p in params],
            compiler_params=_cparams(("arbitrary",)),
            name=self.name + "_bwd",
        )(*arrays, *params, *darrays)
        return list(g[:n_diff]), list(g[n_diff:])


def _fn_norm_mod(rows, pars):
    (x,), (nw,), (shift,), (scale,) = rows[0], pars[0], pars[1], pars[2]
    y = x * lax.rsqrt(jnp.mean(x * x, axis=-1, keepdims=True) + EPS) * nw
    return [[y * (1.0 + scale) + shift]]


def _fn_head_rms_gate(rows, pars):
    (w,) = pars[0]
    return [[o * lax.rsqrt(jnp.mean(o * o, axis=-1, keepdims=True) + EPS) * w * _k_silu(z)
             for o, z in zip(rows[0], rows[1])]]


def _fn_group_norm_gate(rows, pars):
    out = []
    for o, z, w in zip(rows[0], rows[1], pars[0]):
        mu = jnp.mean(o, axis=-1, keepdims=True)
        var = jnp.mean(jnp.square(o - mu), axis=-1, keepdims=True)
        out.append((o - mu) * lax.rsqrt(var + EPS) * w * _k_silu(z))
    return [out]


def _fn_gate(rows, pars):
    return [[o * _k_silu(z) for o, z in zip(rows[0], rows[1])]]


def _fn_merge(rows, pars):
    (ma,), (mb,), (mc,), (pa,), (pb,), (pc,) = rows
    return [[_k_sigmoid(ma) * pa + _k_sigmoid(mb) * pb + _k_sigmoid(mc) * pc]]


def _fn_residual(rows, pars):
    (res,), (out,), (gate,) = rows[0], rows[1], pars[0]
    return [[res + gate * out]]


def _fn_loss(rows, pars):
    (x,), (target,), (w,) = rows[0], rows[1], pars[0]
    y = x * lax.rsqrt(jnp.mean(x * x, axis=-1, keepdims=True) + EPS) * w
    per_row = 0.5 * jnp.mean(jnp.square(y - target), axis=-1, keepdims=True)
    return [[jnp.broadcast_to(per_row, (per_row.shape[0], 128))]]


def _fn_b_rope(rows, pars):
    q, (k, v), (cos,), (sin,) = rows
    rot = lambda x: x * cos + _swap16(x) * sin
    return [[rot(x) for x in q], [rot(k), v]]


def _fn_c_rope(rows, pars):
    q, k, (cos,), (sin,) = rows
    rot = lambda x: x * cos + _swap64(x) * sin
    return [[rot(x) for x in q], [rot(x) * (C_HD ** -0.5) for x in k]]


_norm_mod = _Rowwise(_fn_norm_mod, "norm_mod", [(D_MODEL, D_MODEL)], [D_MODEL] * 3, [(D_MODEL, D_MODEL)])
_residual = _Rowwise(_fn_residual, "residual", [(D_MODEL, D_MODEL)] * 2, [D_MODEL], [(D_MODEL, D_MODEL)])
_loss_rows = _Rowwise(_fn_loss, "loss", [(D_MODEL, D_MODEL)] * 2, [D_MODEL], [(128, 128)], n_diff=1)
_a_out = _Rowwise(_fn_head_rms_gate, "a_out", [(512, 128)] * 2, [128], [(512, 128)])
_c_out = _Rowwise(_fn_group_norm_gate, "c_out", [(512, 128)] * 2, [128], [(512, 128)])
_b_out = _Rowwise(_fn_gate, "b_out", [(512, 512)] * 2, [], [(512, 512)])
_merge = _Rowwise(_fn_merge, "merge", [(D_MODEL, D_MODEL)] * 6, [], [(D_MODEL, D_MODEL)])
_b_rope = _Rowwise(_fn_b_rope, "b_rope", [(512, 128), (256, 128), (128, 128), (128, 128)], [],
                   [(512, 128), (256, 128)], n_diff=2)
_c_rope = _Rowwise(_fn_c_rope, "c_rope", [(512, 128), (512, 128), (128, 128), (128, 128)], [],
                   [(512, 128), (512, 128)], n_diff=2)


HALO = 8
EXT = ROW_BLOCK + 2 * HALO


def _halo_specs(col, width=512):
    last = ROWS // HALO - 1
    per = ROW_BLOCK // HALO
    prev = pl.BlockSpec((HALO, width), lambda i: (jnp.maximum(i * per - 1, 0), col))
    cur = pl.BlockSpec((ROW_BLOCK, width), lambda i: (i, col))
    nxt = pl.BlockSpec((HALO, width), lambda i: (jnp.minimum((i + 1) * per, last), col))
    return [prev, cur, nxt]


def _extended(prev_ref, cur_ref, next_ref):
    i = pl.program_id(0)
    prev_ok = i >= 2
    next_ok = jnp.logical_and(i >= 1, i < ROWS // ROW_BLOCK - 1)
    return jnp.concatenate([jnp.where(prev_ok, prev_ref[...], 0.0), cur_ref[...],
                            jnp.where(next_ok, next_ref[...], 0.0)], axis=0)


def _conv_taps(x_ext, w_ref, flip):
    acc = None
    for j in range(A_CONV):
        shift = (j - 2) if flip else (2 - j)
        term = w_ref[j:j + 1, :] * pltpu.roll(x_ext, shift % EXT, 0)
        acc = term if acc is None else acc + term
    return acc


def _conv_post(pre_pieces, normalize, scale):
    out = []
    for p in pre_pieces:
        y = _k_silu(p)
        if normalize:
            y = y * lax.rsqrt(jnp.sum(y * y, axis=-1, keepdims=True) + EPS) * scale
        out.append(y)
    return out


def _a_prep_fwd(proj, conv8, col, normalize, scale, name):
    def body(prev_ref, cur_ref, next_ref, w_ref, o_ref):
        pre = _conv_taps(_extended(prev_ref, cur_ref, next_ref), w_ref, False)[HALO:HALO + ROW_BLOCK]
        for h, y in enumerate(_conv_post(_pieces(pre, 128), normalize, scale)):
            o_ref[:, h * 128:(h + 1) * 128] = y

    return pl.pallas_call(
        body,
        grid=(ROWS // ROW_BLOCK,),
        in_specs=_halo_specs(col) + [pl.BlockSpec((8, 512), lambda i: (0, col))],
        out_specs=pl.BlockSpec((ROW_BLOCK, 512), lambda i: (i, 0)),
        out_shape=jax.ShapeDtypeStruct((ROWS, 512), F32),
        compiler_params=_cparams(("parallel",)),
        name=name + "_fwd",
    )(proj, proj, proj, conv8)


def _a_prep_bwd(proj, conv8, col, normalize, scale, dout_f, dout_r, name):
    def body(xp, xc, xn, w_ref, fp, fc, fn_, rp, rc, rn, dx_ref, dw_ref):
        @pl.when(pl.program_id(0) == 0)
        def _():
            dw_ref[...] = jnp.zeros_like(dw_ref)

        x_ext = _extended(xp, xc, xn)
        dout = _extended(fp, fc, fn_) + _extended(rp, rc, rn)
        pre = _conv_taps(x_ext, w_ref, False)
        _, vjp = jax.vjp(lambda p: _conv_post(p, normalize, scale), _pieces(pre, 128))
        (dpre,) = vjp(_pieces(dout, 128))
        dpre = jnp.concatenate(dpre, axis=1)
        dx_ref[...] = _conv_taps(dpre, w_ref, True)[HALO:HALO + ROW_BLOCK]
        own = dpre[HALO:HALO + ROW_BLOCK]
        for j in range(A_CONV):
            shifted = pltpu.roll(x_ext, (2 - j) % EXT, 0)[HALO:HALO + ROW_BLOCK]
            dw_ref[j:j + 1, :] += jnp.sum(own * shifted, axis=0, keepdims=True)

    return pl.pallas_call(
        body,
        grid=(ROWS // ROW_BLOCK,),
        in_specs=_halo_specs(col) + [pl.BlockSpec((8, 512), lambda i: (0, col))] + _halo_specs(0) + _halo_specs(0),
        out_specs=[pl.BlockSpec((ROW_BLOCK, 512), lambda i: (i, 0)), pl.BlockSpec((8, 512), lambda i: (0, 0))],
        out_shape=[jax.ShapeDtypeStruct((ROWS, 512), F32), jax.ShapeDtypeStruct((8, 512), F32)],
        compiler_params=_cparams(("arbitrary",)),
        name=name + "_bwd",
    )(proj, proj, proj, conv8, dout_f, dout_f, dout_f, dout_r, dout_r, dout_r)


N_CHAIN = 8


def _rev_chunk(s):
    return jnp.where(s < N_CTX_CHUNK, N_CTX_CHUNK - 1 - s, N_CHUNK + N_CTX_CHUNK - 1 - s)


def _scan_specs(step_of, v_col=0):
    cf = step_of
    cr = lambda n: _rev_chunk(step_of(n))

    def pair(shape, index):
        return (pl.BlockSpec(shape, lambda n: index(cf(n))), pl.BlockSpec(shape, lambda n: index(cr(n))))

    return dict(
        tok=pair((CHUNK, 512), lambda c: (c, 0)),
        tokv=pair((CHUNK, 512), lambda c: (c, v_col)),
        col=pair((4, CHUNK, 1), lambda c: (0, c, 0)),
        row=pair((4, 1, 1, CHUNK), lambda c: (0, c, 0, 0)),
        one=pair((4, 1, 1, 1), lambda c: (0, c, 0, 0)),
        state=pair((None, 4, 128, 128), lambda c: (c, 0, 0, 0)),
        tinv=pair((None, 4, CHUNK, CHUNK), lambda c: (c, 0, 0, 0)),
    )


def _both(specs, kinds):
    out = []
    for kind in kinds:
        out += list(specs[kind])
    return out


def _scan_call(body, name, in_specs, out_specs, out_shape, operands, ride):
    body, r_in, r_out, r_shape, r_scratch = _riding(body, len(in_specs), len(out_specs), 1, ride, N_CHUNK)
    return pl.pallas_call(
        body,
        grid=(N_CHUNK,),
        in_specs=in_specs + r_in,
        out_specs=out_specs + r_out,
        out_shape=out_shape + r_shape,
        scratch_shapes=[pltpu.VMEM((N_CHAIN, 128, 128), F32)] + r_scratch,
        compiler_params=_cparams(("arbitrary",)),
        name=name,
    )(*operands, *(ride[0] if ride else []))


def _chain_masks():
    ii = lax.broadcasted_iota(jnp.int32, (CHUNK, CHUNK), 0)
    jj = lax.broadcasted_iota(jnp.int32, (CHUNK, CHUNK), 1)
    eye = jnp.where(ii == jj, 1.0, 0.0).astype(F32)
    lower = (ii >= jj, ii > jj)
    upper = (ii <= jj, ii < jj)
    return [lower] * 4 + [upper] * 4, eye


def _tri_inv_all(ls, eye):
    xs = [eye - l for l in ls]
    ps = [_hdot(l, l) for l in ls]
    for i in range(5):
        xs = [x + _hdot(x, p) for x, p in zip(xs, ps)]
        if i < 4:
            ps = [_hdot(p, p) for p in ps]
    return xs


@jax.custom_vjp
def _inv_saved(l, x):
    return x


def _inv_saved_fwd(l, x):
    return x, x


def _inv_saved_bwd(x, dx):
    return -_bdot(x, _bdot(dx, x, 1, 1), 0, 0), jnp.zeros_like(x)


_inv_saved.defvjp(_inv_saved_fwd, _inv_saved_bwd)


def _delta_chains(q, k, v, beta_r, gcr, gl, s, masks, eye, tinv_saved):
    n = range(len(q))
    beta = [jnp.sum(eye * beta_r[i], axis=1, keepdims=True) for i in n]
    gcc = [jnp.sum(eye * gcr[i], axis=1, keepdims=True) for i in n]
    decay = [jnp.exp(jnp.where(masks[i][0], gcc[i] - gcr[i], NEG)) for i in n]
    kb = [k[i] * beta[i] for i in n]
    lmat = [jnp.where(masks[i][1], _bdot(kb[i], k[i], 1, 1) * decay[i], 0.0) for i in n]
    if tinv_saved is None:
        tinv = _tri_inv_all(lmat, eye)
    else:
        tinv = [_inv_saved(lmat[i], tinv_saved[i]) for i in n]
    eg = [jnp.exp(gcc[i]) for i in n]
    u = [_bdot(tinv[i], v[i] * beta[i], 1, 0) for i in n]
    w = [_bdot(tinv[i], kb[i] * eg[i], 1, 0) for i in n]
    qk = [_bdot(q[i], k[i], 1, 1) * decay[i] for i in n]
    v_new = [u[i] - _bdot(w[i], s[i], 1, 0) for i in n]
    o = [_bdot(q[i] * eg[i], s[i], 1, 0) + _bdot(qk[i], v_new[i], 1, 0) for i in n]
    s_new = [s[i] * jnp.exp(gl[i]) + _bdot(k[i] * jnp.exp(gl[i] - gcc[i]), v_new[i], 0, 0) for i in n]
    return (o, s_new), tinv


def _chain_loads(tok_pairs, small_pairs):
    toks = [[pair[i // 4][:, (i % 4) * 128:(i % 4 + 1) * 128] for i in range(N_CHAIN)] for pair in tok_pairs]
    smalls = [[pair[i // 4][i % 4] for i in range(N_CHAIN)] for pair in small_pairs]
    return toks, smalls


def _delta_fwd_call(q, k, v, beta, gc, gl, ride=None):
    sp = _scan_specs(lambda n: n)

    def body(qf, qr, kf, kr, vf, vr, bf, br, gcrf, gcrr, glf, glr, of, orv, ssf, ssr, tsf, tsr, s_scr):
        @pl.when(pl.program_id(0) == 0)
        def _():
            s_scr[...] = jnp.zeros_like(s_scr)

        masks, eye = _chain_masks()
        (qs, ks, vs), _ = _chain_loads([(qf, qr), (kf, kr), (vf, vr)], [])
        bs = [(bf, br)[i // 4][i % 4, 0] for i in range(N_CHAIN)]
        gcrs = [(gcrf, gcrr)[i // 4][i % 4, 0] for i in range(N_CHAIN)]
        gls = [(glf, glr)[i // 4][i % 4, 0] for i in range(N_CHAIN)]
        ss = [s_scr[i] for i in range(N_CHAIN)]
        (o, s_new), tinv = _delta_chains(qs, ks, vs, bs, gcrs, gls, ss, masks, eye, None)
        for i in range(N_CHAIN):
            d, h = i // 4, i % 4
            (ssf, ssr)[d][h] = ss[i]
            (tsf, tsr)[d][h] = tinv[i]
            (of, orv)[d][:, h * 128:(h + 1) * 128] = o[i]
            s_scr[i] = s_new[i]

    return _scan_call(
        body, "delta_fwd",
        _both(sp, ["tok", "tok", "tok", "row", "row", "one"]),
        _both(sp, ["tok", "state", "tinv"]),
        [jax.ShapeDtypeStruct((ROWS, 512), F32)] * 2 + [jax.ShapeDtypeStruct((N_CHUNK, 4, 128, 128), F32)] * 2
        + [jax.ShapeDtypeStruct((N_CHUNK, 4, CHUNK, CHUNK), F32)] * 2,
        [q, q, k, k, v, v, *beta, *gc, *gl], ride)


def _delta_bwd_call(q, k, v, beta, gc, gl, ssave, tsave, do, ride=None):
    sp = _scan_specs(lambda n: N_CHUNK - 1 - n)

    def body(qf, qr, kf, kr, vf, vr, bf, br, gcrf, gcrr, glf, glr, ssf, ssr, tsf, tsr, dof, dor,
             dqf, dqr, dkf, dkr, dvf, dvr, dbf, dbr, dgcrf, dgcrr, dglf, dglr, ds_scr):
        @pl.when(pl.program_id(0) == 0)
        def _():
            ds_scr[...] = jnp.zeros_like(ds_scr)

        masks, eye = _chain_masks()
        (qs, ks, vs, dos), (ss, ts) = _chain_loads(
            [(qf, qr), (kf, kr), (vf, vr), (dof, dor)], [(ssf, ssr), (tsf, tsr)])
        bs = [(bf, br)[i // 4][i % 4, 0] for i in range(N_CHAIN)]
        gcrs = [(gcrf, gcrr)[i // 4][i % 4, 0] for i in range(N_CHAIN)]
        gls = [(glf, glr)[i // 4][i % 4, 0] for i in range(N_CHAIN)]
        fn = lambda *a: _delta_chains(*a, masks, eye, ts)
        _, vjp, _ = jax.vjp(fn, qs, ks, vs, bs, gcrs, gls, ss, has_aux=True)
        dq, dk, dv, db, dgcr, dgl, ds = vjp((dos, [ds_scr[i] for i in range(N_CHAIN)]))
        for i in range(N_CHAIN):
            d, h = i // 4, i % 4
            hs = slice(h * 128, (h + 1) * 128)
            (dqf, dqr)[d][:, hs] = dq[i]
            (dkf, dkr)[d][:, hs] = dk[i]
            (dvf, dvr)[d][:, hs] = dv[i]
            (dbf, dbr)[d][h, 0] = db[i]
            (dgcrf, dgcrr)[d][h, 0] = dgcr[i]
            (dglf, dglr)[d][h, 0] = dgl[i]
            ds_scr[i] = ds[i]

    tok = jax.ShapeDtypeStruct((ROWS, 512), F32)
    return _scan_call(
        body, "delta_bwd",
        _both(sp, ["tok", "tok", "tok", "row", "row", "one", "state", "tinv", "tok"]),
        _both(sp, ["tok", "tok", "tok", "row", "row", "one"]),
        [tok] * 6 + [jax.ShapeDtypeStruct((4, N_CHUNK, 1, CHUNK), F32)] * 4
        + [jax.ShapeDtypeStruct((4, N_CHUNK, 1, 1), F32)] * 2,
        [q, q, k, k, v, v, *beta, *gc, *gl, *ssave, *tsave, do, do], ride)


def _ret_chains(q, k, v, dm, qs, ks, cd, s):
    n = range(len(q))
    a = [_bdot(q[i], k[i], 1, 1) * dm[i] for i in n]
    o = [_bdot(a[i], v[i], 1, 0) + _bdot(q[i] * qs[i], s[i], 1, 0) for i in n]
    s_new = [s[i] * cd[i] + _bdot(k[i] * ks[i], v[i], 0, 0) for i in n]
    return o, s_new


def _ret_const_specs():
    return [pl.BlockSpec((N_CHAIN, CHUNK, CHUNK), lambda n: (0, 0, 0)), pl.BlockSpec((N_CHAIN, CHUNK, 1), lambda n: (0, 0, 0)),
            pl.BlockSpec((N_CHAIN, CHUNK, 1), lambda n: (0, 0, 0)), pl.BlockSpec((N_CHAIN, 1, 1), lambda n: (0, 0, 0))]


def _ret_fwd_call(q, k, v, v_col, dm, qs, ks, cd, ride=None):
    sp = _scan_specs(lambda n: n, v_col)

    def body(qf, qr, kf, kr, vf, vr, dm_ref, qs_ref, ks_ref, cd_ref, of, orv, ssf, ssr, s_scr):
        @pl.when(pl.program_id(0) == 0)
        def _():
            s_scr[...] = jnp.zeros_like(s_scr)

        (qc, kc, vc), _ = _chain_loads([(qf, qr), (kf, kr), (vf, vr)], [])
        ss = [s_scr[i] for i in range(N_CHAIN)]
        consts = [[r[i] for i in range(N_CHAIN)] for r in (dm_ref, qs_ref, ks_ref, cd_ref)]
        o, s_new = _ret_chains(qc, kc, vc, *consts, ss)
        for i in range(N_CHAIN):
            d, h = i // 4, i % 4
            (ssf, ssr)[d][h] = ss[i]
            (of, orv)[d][:, h * 128:(h + 1) * 128] = o[i]
            s_scr[i] = s_new[i]

    return _scan_call(
        body, "ret_fwd",
        _both(sp, ["tok", "tok", "tokv"]) + _ret_const_specs(),
        _both(sp, ["tok", "state"]),
        [jax.ShapeDtypeStruct((ROWS, 512), F32)] * 2 + [jax.ShapeDtypeStruct((N_CHUNK, 4, 128, 128), F32)] * 2,
        [q, q, k, k, v, v, dm, qs, ks, cd], ride)


def _ret_bwd_call(q, k, v, v_col, dm, qs, ks, cd, ssave, do, ride=None):
    sp = _scan_specs(lambda n: N_CHUNK - 1 - n, v_col)

    def body(qf, qr, kf, kr, vf, vr, dm_ref, qs_ref, ks_ref, cd_ref, ssf, ssr, dof, dor,
             dqf, dqr, dkf, dkr, dvf, dvr, ddm_ref, dqs_ref, dks_ref, dcd_ref, ds_scr):
        @pl.when(pl.program_id(0) == 0)
        def _():
            ds_scr[...] = jnp.zeros_like(ds_scr)
            ddm_ref[...] = jnp.zeros_like(ddm_ref)
            dqs_ref[...] = jnp.zeros_like(dqs_ref)
            dks_ref[...] = jnp.zeros_like(dks_ref)
            dcd_ref[...] = jnp.zeros_like(dcd_ref)

        (qc, kc, vc, dos), (ss,) = _chain_loads([(qf, qr), (kf, kr), (vf, vr), (dof, dor)], [(ssf, ssr)])
        consts = [[r[i] for i in range(N_CHAIN)] for r in (dm_ref, qs_ref, ks_ref, cd_ref)]
        _, vjp = jax.vjp(_ret_chains, qc, kc, vc, *consts, ss)
        dq, dk, dv, ddm, dqs, dks, dcd, ds = vjp((dos, [ds_scr[i] for i in range(N_CHAIN)]))
        for i in range(N_CHAIN):
            d, h = i // 4, i % 4
            hs = slice(h * 128, (h + 1) * 128)
            (dqf, dqr)[d][:, hs] = dq[i]
            (dkf, dkr)[d][:, hs] = dk[i]
            (dvf, dvr)[d][:, hs] = dv[i]
            ddm_ref[i] += ddm[i]
            dqs_ref[i] += dqs[i]
            dks_ref[i] += dks[i]
            dcd_ref[i] += dcd[i]
            ds_scr[i] = ds[i]

    tok = jax.ShapeDtypeStruct((ROWS, 512), F32)
    return _scan_call(
        body, "ret_bwd",
        _both(sp, ["tok", "tok", "tokv"]) + _ret_const_specs() + _both(sp, ["state", "tok"]),
        _both(sp, ["tok", "tok", "tok"]) + _ret_const_specs(),
        [tok] * 6 + [jax.ShapeDtypeStruct((N_CHAIN, CHUNK, CHUNK), F32), jax.ShapeDtypeStruct((N_CHAIN, CHUNK, 1), F32),
                     jax.ShapeDtypeStruct((N_CHAIN, CHUNK, 1), F32), jax.ShapeDtypeStruct((N_CHAIN, 1, 1), F32)],
        [q, q, k, k, v, v, dm, qs, ks, cd, *ssave, do, do], ride)


N_QBLK = ROWS // B_BLOCK
CTX_QBLK = CTX_LEN // B_BLOCK


def _attn_heads(q, kc, vc, kw, vw, sink, valid):
    n = range(len(q))
    scale = B_HD ** -0.5
    s_c = [_bdot(q[i], kc[i], 1, 1) * scale for i in n]
    s_w = [jnp.where(valid, _bdot(q[i], kw[i], 1, 1) * scale, NEG) for i in n]
    m = [lax.stop_gradient(jnp.maximum(jnp.maximum(jnp.max(s_c[i], axis=-1, keepdims=True), sink[i]),
                                       jnp.max(s_w[i], axis=-1, keepdims=True))) for i in n]
    e_c = [jnp.exp(s_c[i] - m[i]) for i in n]
    e_w = [jnp.exp(s_w[i] - m[i]) for i in n]
    den = [jnp.sum(e_c[i], axis=-1, keepdims=True) + jnp.sum(e_w[i], axis=-1, keepdims=True)
           + jnp.exp(sink[i] - m[i]) for i in n]
    return [(_bdot(e_c[i], vc[i], 1, 0) + _bdot(e_w[i], vw[i], 1, 0)) / den[i] for i in n]


def _attn_loads(q_ref, kv_ref, sink_ref, start):
    q, kc, vc, kw, vw, sink = [], [], [], [], [], []
    for hk in range(B_KV_HEADS):
        ks = slice(hk * B_HD, (hk + 1) * B_HD)
        vs = slice(128 + hk * B_HD, 128 + (hk + 1) * B_HD)
        grp = (kv_ref[0:CTX_LEN, ks], kv_ref[0:CTX_LEN, vs],
               kv_ref[pl.ds(start, 3 * B_BLOCK), ks], kv_ref[pl.ds(start, 3 * B_BLOCK), vs])
        for g in range(4):
            h = hk * 4 + g
            q.append(q_ref[:, h * B_HD:(h + 1) * B_HD])
            for lst, val in zip((kc, vc, kw, vw), grp):
                lst.append(val)
            sink.append(jnp.full((1, 1), sink_ref[h], F32))
    return q, kc, vc, kw, vw, sink


def _window(blk):
    xblk = blk - CTX_QBLK
    first = jnp.clip((xblk - 1) * B_BLOCK, 0, SEQ - 3 * B_BLOCK)
    qpos = xblk * B_BLOCK + lax.broadcasted_iota(jnp.int32, (B_BLOCK, 3 * B_BLOCK), 0)
    kpos = first + lax.broadcasted_iota(jnp.int32, (B_BLOCK, 3 * B_BLOCK), 1)
    far = jnp.where(blk >= CTX_QBLK, 0, 2 * SEQ)
    valid = jnp.abs(kpos - qpos) + far <= WINDOW
    return pl.multiple_of(first + CTX_LEN, B_BLOCK), valid


def _attn_specs():
    qspec = pl.BlockSpec((B_BLOCK, 512), lambda i: (i, 0))
    kvspec = pl.BlockSpec((ROWS, 256), lambda i: (0, 0))
    return qspec, kvspec, pl.BlockSpec(memory_space=pltpu.SMEM)


def _attn_fwd_call(q, kv, sink):
    def body(q_ref, kv_ref, sink_ref, o_ref):
        start, valid = _window(pl.program_id(0))
        out = _attn_heads(*_attn_loads(q_ref, kv_ref, sink_ref, start), valid)
        for h in range(B_Q_HEADS):
            o_ref[:, h * B_HD:(h + 1) * B_HD] = out[h]

    qspec, kvspec, sspec = _attn_specs()
    return pl.pallas_call(
        body,
        grid=(N_QBLK,),
        in_specs=[qspec, kvspec, sspec],
        out_specs=qspec,
        out_shape=jax.ShapeDtypeStruct((ROWS, 512), F32),
        compiler_params=_cparams(("arbitrary",)),
        name="attn_fwd",
    )(q, kv, sink)


def _attn_bwd_call(q, kv, sink, do):
    def body(q_ref, kv_ref, sink_ref, do_ref, dq_ref, dkv_ref, dsink_ref):
        @pl.when(pl.program_id(0) == 0)
        def _():
            dkv_ref[...] = jnp.zeros_like(dkv_ref)
            dsink_ref[...] = jnp.zeros_like(dsink_ref)

        start, valid = _window(pl.program_id(0))
        _, vjp = jax.vjp(functools.partial(_attn_heads, valid=valid), *_attn_loads(q_ref, kv_ref, sink_ref, start))
        dq, dkc, dvc, dkw, dvw, dsink = vjp([do_ref[:, h * B_HD:(h + 1) * B_HD] for h in range(B_Q_HEADS)])
        for h in range(B_Q_HEADS):
            dq_ref[:, h * B_HD:(h + 1) * B_HD] = dq[h]
            dsink_ref[h:h + 1, :] += jnp.broadcast_to(dsink[h], (1, 128))
        for hk in range(B_KV_HEADS):
            ks = slice(hk * B_HD, (hk + 1) * B_HD)
            vs = slice(128 + hk * B_HD, 128 + (hk + 1) * B_HD)
            grp = lambda parts: parts[hk * 4] + parts[hk * 4 + 1] + parts[hk * 4 + 2] + parts[hk * 4 + 3]
            dkv_ref[0:CTX_LEN, ks] += grp(dkc)
            dkv_ref[0:CTX_LEN, vs] += grp(dvc)
            dkv_ref[pl.ds(start, 3 * B_BLOCK), ks] += grp(dkw)
            dkv_ref[pl.ds(start, 3 * B_BLOCK), vs] += grp(dvw)

    qspec, kvspec, sspec = _attn_specs()
    return pl.pallas_call(
        body,
        grid=(N_QBLK,),
        in_specs=[qspec, kvspec, sspec, qspec],
        out_specs=[qspec, kvspec, pl.BlockSpec((8, 128), lambda i: (0, 0))],
        out_shape=[jax.ShapeDtypeStruct((ROWS, 512), F32), jax.ShapeDtypeStruct((ROWS, 256), F32),
                   jax.ShapeDtypeStruct((8, 128), F32)],
        compiler_params=_cparams(("arbitrary",)),
        name="attn_bwd",
    )(q, kv, sink, do)


def _my_id():
    return 4 * lax.axis_index("x") + 2 * lax.axis_index("y") + lax.axis_index("c")


def _peer(k):
    x, y, c = lax.axis_index("x"), lax.axis_index("y"), lax.axis_index("c")
    return (1 - x if k & 4 else x, 1 - y if k & 2 else y, 1 - c if k & 1 else c)


def _exchange_copies(ins, outs, sems, gather):
    send_sems, recv_sems, local_sems = sems
    me = _my_id()
    own, remote = [], []
    for a in range(len(ins)):
        own.append(pltpu.make_async_copy(ins[a] if gather else ins[a].at[me], outs[a].at[me], local_sems.at[a]))
        for k in range(1, N_DEV):
            peer_slot = jnp.bitwise_xor(me, k)
            src = ins[a] if gather else ins[a].at[peer_slot]
            common = dict(src_ref=src, send_sem=send_sems.at[a, k - 1], recv_sem=recv_sems.at[a, k - 1],
                          device_id=_peer(k), device_id_type=MESH)
            remote.append((pltpu.make_async_remote_copy(dst_ref=outs[a].at[me], **common),
                           pltpu.make_async_remote_copy(dst_ref=outs[a].at[peer_slot], **common)))
    return own, remote


def _exchange_start(ins, outs, sems, gather):
    own, remote = _exchange_copies(ins, outs, sems, gather)
    for cp in own:
        cp.start()
    for send, _ in remote:
        send.start()


def _exchange_wait(ins, outs, sems, gather):
    own, remote = _exchange_copies(ins, outs, sems, gather)
    for _, arrival in remote:
        arrival.wait_recv()
    for send, _ in remote:
        send.wait_send()
    for cp in own:
        cp.wait()


def _exchange_plumbing(arrays, gather):
    n = len(arrays)
    hbm = [pl.BlockSpec(memory_space=pltpu.HBM)] * n
    out_shape = [jax.ShapeDtypeStruct((N_DEV,) + (a.shape if gather else a.shape[1:]), a.dtype) for a in arrays]
    sems = [pltpu.SemaphoreType.DMA((n, N_DEV - 1)), pltpu.SemaphoreType.DMA((n, N_DEV - 1)),
            pltpu.SemaphoreType.DMA((n,))]
    return hbm, out_shape, sems


def _exchange(arrays, gather, name):
    n = len(arrays)

    def body(*refs):
        ins, outs, sems = refs[:n], refs[n:2 * n], refs[2 * n:]
        _exchange_start(ins, outs, sems, gather)
        _exchange_wait(ins, outs, sems, gather)

    hbm, out_shape, sems = _exchange_plumbing(arrays, gather)
    return pl.pallas_call(
        body,
        in_specs=hbm,
        out_specs=hbm,
        out_shape=out_shape,
        scratch_shapes=sems,
        compiler_params=pltpu.CompilerParams(has_side_effects=True),
        name=name,
    )(*arrays)


def _riding(body, n_in, n_out, n_scratch, ride, n_steps):
    if ride is None:
        return body, [], [], [], []
    arrays, gather = ride
    n = len(arrays)

    def wrapped(*refs):
        ins, rin = refs[:n_in], refs[n_in:n_in + n]
        outs = refs[n_in + n:n_in + n + n_out]
        rout = refs[n_in + n + n_out:n_in + 2 * n + n_out]
        scratch = refs[n_in + 2 * n + n_out:n_in + 2 * n + n_out + n_scratch]
        sems = refs[n_in + 2 * n + n_out + n_scratch:]

        @pl.when(pl.program_id(0) == 0)
        def _():
            _exchange_start(rin, rout, sems, gather)

        body(*ins, *outs, *scratch)

        @pl.when(pl.program_id(0) == n_steps - 1)
        def _():
            _exchange_wait(rin, rout, sems, gather)

    hbm, out_shape, sems = _exchange_plumbing(arrays, gather)
    return wrapped, hbm, hbm, out_shape, sems


def _adamw(w, m, v, contrib, name):
    r, c = w.shape
    br = _pick(r, (256, 128, 64, 32, 16, 8))
    bc1 = 1.0 - ADAM_B1 ** ADAM_STEP
    bc2 = 1.0 - ADAM_B2 ** ADAM_STEP

    def body(w_ref, m_ref, v_ref, c_ref, g_ref, d_ref, nm_ref, nv_ref):
        g = c_ref[0].astype(F32)
        for j in range(1, N_DEV):
            g = g + c_ref[j].astype(F32)
        m_new = ADAM_B1 * m_ref[...] + (1.0 - ADAM_B1) * g
        v_new = ADAM_B2 * v_ref[...] + (1.0 - ADAM_B2) * (g * g)
        m_hat = m_new / bc1
        v_hat = v_new / bc2
        g_ref[...] = g
        d_ref[...] = -ADAM_LR * (m_hat / (jnp.sqrt(v_hat) + ADAM_EPS) + ADAM_WD * w_ref[...])
        nm_ref[...] = m_new
        nv_ref[...] = v_new

    spec = pl.BlockSpec((br, c), lambda i: (i, 0))
    cspec = pl.BlockSpec((N_DEV, br, c), lambda i: (0, i, 0))
    return pl.pallas_call(
        body,
        grid=(r // br,),
        in_specs=[spec, spec, spec, cspec],
        out_specs=[spec] * 4,
        out_shape=[jax.ShapeDtypeStruct((r, c), F32)] * 4,
        compiler_params=_cparams(("parallel",)),
        name=name,
    )(w, m, v, contrib)


def _silu(x):
    return x * jax.nn.sigmoid(x)


def _rope_angles(pos, n_freq):
    inv = ROPE_BASE ** (-jnp.arange(n_freq, dtype=F32) / n_freq)
    return pos[:, None] * inv[None, :]


def _with_ctx_rows(cos, sin):
    return (jnp.concatenate([jnp.ones((CTX_LEN, 128), F32), cos], axis=0),
            jnp.concatenate([jnp.zeros((CTX_LEN, 128), F32), sin], axis=0))


def _rope_tables():
    rows_n = SEQ // GRID_W
    rows = jnp.repeat(jnp.arange(rows_n, dtype=F32), GRID_W)
    cols = jnp.tile(jnp.arange(GRID_W, dtype=F32), rows_n)
    ang_r = _rope_angles(rows, B_HD // 4)
    ang_c = _rope_angles(cols, B_HD // 4)
    cos_b = jnp.tile(jnp.concatenate([jnp.cos(ang_r)] * 2 + [jnp.cos(ang_c)] * 2, axis=1), (1, 2))
    sin_b = jnp.tile(jnp.concatenate([-jnp.sin(ang_r), jnp.sin(ang_r), -jnp.sin(ang_c), jnp.sin(ang_c)], axis=1), (1, 2))
    ang = _rope_angles(jnp.arange(SEQ, dtype=F32), C_HD // 2)
    cos_c = jnp.concatenate([jnp.cos(ang)] * 2, axis=1)
    sin_c = jnp.concatenate([-jnp.sin(ang), jnp.sin(ang)], axis=1)
    return _with_ctx_rows(cos_b, sin_b), _with_ctx_rows(cos_c, sin_c)


def _halves(a):
    return a[:4], a[4:]


def _delta_gates(ab, a_log, dt_bias):
    beta = jax.nn.sigmoid(ab[:, :8])
    g = -jnp.exp(a_log)[None, :] * jax.nn.softplus(ab[:, 8:] + dt_bias[None, :])
    gch = g.reshape(N_CHUNK, CHUNK, 8)
    fwd = jnp.cumsum(gch[..., :4], axis=1)
    bwd = jnp.flip(jnp.cumsum(jnp.flip(gch[..., 4:], axis=1), axis=1), axis=1)
    gc = jnp.concatenate([fwd, bwd], axis=-1)
    gl = jnp.sum(gch, axis=1)
    rows = lambda a: _halves(a.transpose(2, 0, 1)[:, :, None, :])
    return rows(beta.reshape(N_CHUNK, CHUNK, 8)), rows(gc), _halves(gl.T[:, :, None, None])


def _ret_consts(c_decay):
    lg = jax.nn.log_sigmoid(c_decay)
    idx = jnp.arange(CHUNK, dtype=F32)
    diff = idx[:, None] - idx[None, :]
    lgf, lgb = lg[:4, None, None], lg[4:, None, None]
    dm = jnp.concatenate([jnp.exp(jnp.where(diff >= 0, diff * lgf, -jnp.inf)),
                          jnp.exp(jnp.where(diff <= 0, -diff * lgb, -jnp.inf))], axis=0)
    qs = jnp.concatenate([jnp.exp((idx + 1.0)[None, :] * lg[:4, None]),
                          jnp.exp((CHUNK - idx)[None, :] * lg[4:, None])], axis=0)[:, :, None]
    ks = jnp.concatenate([jnp.exp((CHUNK - 1.0 - idx)[None, :] * lg[:4, None]),
                          jnp.exp(idx[None, :] * lg[4:, None])], axis=0)[:, :, None]
    return dm, qs, ks, jnp.exp(CHUNK * lg)[:, None, None]


A_PIECES = ((0, True, A_DK ** -0.5, "a_q"), (1, True, 1.0, "a_k"), (2, False, 1.0, "a_v"))
B_ROPE_COLS = [C_BQ // 512, C_BKV // 256, 0, 0]
C_ROPE_COLS = [C_CQ // 512, C_CK // 512, 0, 0]
MERGE_COLS = [C_MERGE // 1024, C_MERGE // 1024 + 1, C_MERGE // 1024 + 2, 0, 0, 0]


def _conv8(conv_w):
    return jnp.pad(conv_w, ((0, 8 - A_CONV), (0, 0)))


W_IN_TILES = {"nn": (2176, 512, 1024), "nt": (1088, 1024, 2176), "db": (1024, 512, ROWS)}


def _core_forward(h, w16, p, rides=None):
    proj = _matmul(h, w16, "w_in", "nn", W_IN_TILES["nn"])
    (cos_b, sin_b), (cos_c, sin_c) = _rope_tables()
    conv8 = _conv8(p["a_conv_w"])
    q, k, v = [_a_prep_fwd(proj, conv8, col, nrm, scl, nm) for col, nrm, scl, nm in A_PIECES]
    gates = _delta_gates(proj[:, C_AB:C_AB + 16], p["a_log"], p["a_dt_bias"])
    res = _delta_fwd_call(q, k, v, *gates, ride=rides and rides["delta"])
    (of, orv, ssf, ssr, tsf, tsr), ride_delta = res[:6], res[6:]
    (y_a,) = _a_out.fwd([(of, orv), proj], [p["a_norm_w"][None, :]], [0, C_AZ // 512])

    qb, kvb = _b_rope.fwd([proj, proj, cos_b, sin_b], [], B_ROPE_COLS)
    ob = _attn_fwd_call(qb, kvb, p["b_sink"])
    (y_b,) = _b_out.fwd([ob, proj], [], [0, C_BZ // 512])

    qc, kc = _c_rope.fwd([proj, proj, cos_c, sin_c], [], C_ROPE_COLS)
    res = _ret_fwd_call(qc, kc, proj, C_CV // 512, *_ret_consts(p["c_decay"]), ride=rides and rides["ret"])
    (cf, cr, csf, csr), ride_ret = res[:4], res[4:]
    (y_c,) = _c_out.fwd([(cf, cr), proj], [p["c_norm_w"][None, :]], [0, C_CZ // 512])

    wb = p["w_branch"].astype(BF16)
    pa, pb, pc = [_matmul(y, wb[i], "branch_" + "abc"[i]) for i, y in enumerate((y_a, y_b, y_c))]
    (merged,) = _merge.fwd([proj, proj, proj, pa, pb, pc], [], MERGE_COLS)
    saved = dict(proj=proj, q=q, k=k, v=v, of=of, orv=orv, ss=(ssf, ssr), ts=(tsf, tsr), qb=qb, kvb=kvb, ob=ob,
                 qc=qc, kc=kc, cf=cf, cr=cr, cs=(csf, csr), y=(y_a, y_b, y_c), pabc=(pa, pb, pc))
    return merged, saved, (ride_delta, ride_ret)


def _core_backward(h, w16, p, s, dmerged, rides=None):
    proj = s["proj"]
    (cos_b, sin_b), (cos_c, sin_c) = _rope_tables()
    conv8 = _conv8(p["a_conv_w"])
    wb = p["w_branch"].astype(BF16)
    y_a, y_b, y_c = s["y"]

    (dma, dmb, dmc, dpa, dpb, dpc), _ = _merge.bwd([proj, proj, proj, *s["pabc"]], [], [dmerged], MERGE_COLS)
    dy = [_matmul(d, wb[i], "branch_%s_da" % "abc"[i], "nt") for i, d in enumerate((dpa, dpb, dpc))]
    dwb = jnp.stack([_matmul(y, d, "branch_%s_db" % "abc"[i], "tn")
                     for i, (y, d) in enumerate(zip((y_a, y_b, y_c), (dpa, dpb, dpc)))])

    consts, consts_vjp = jax.vjp(_ret_consts, p["c_decay"])
    (do_c, dcz), (dcnw,) = _c_out.bwd([(s["cf"], s["cr"]), proj], [p["c_norm_w"][None, :]], [dy[2]], [0, C_CZ // 512])
    g = _ret_bwd_call(s["qc"], s["kc"], proj, C_CV // 512, *consts, s["cs"], do_c, ride=rides and rides["ret"])
    ride_ret = g[10:]
    (dcq, dck), _ = _c_rope.bwd([proj, proj, cos_c, sin_c], [], [(g[0], g[1]), (g[2], g[3])], C_ROPE_COLS)
    dcv = g[4] + g[5]
    (dc_decay,) = consts_vjp(tuple(g[6:10]))

    (dob, dbz), _ = _b_out.bwd([s["ob"], proj], [], [dy[1]], [0, C_BZ // 512])
    dqb, dkvb, dsink = _attn_bwd_call(s["qb"], s["kvb"], p["b_sink"], dob)
    (dbq, dbkv), _ = _b_rope.bwd([proj, proj, cos_b, sin_b], [], [dqb, dkvb], B_ROPE_COLS)

    ab = proj[:, C_AB:C_AB + 16]
    gates, gates_vjp = jax.vjp(_delta_gates, ab, p["a_log"], p["a_dt_bias"])
    (do_a, daz), (danw,) = _a_out.bwd([(s["of"], s["orv"]), proj], [p["a_norm_w"][None, :]], [dy[0]], [0, C_AZ // 512])
    g = _delta_bwd_call(s["q"], s["k"], s["v"], *gates, s["ss"], s["ts"], do_a, ride=rides and rides["delta"])
    ride_delta = g[12:]
    dgates = ((g[6], g[7]), (g[8], g[9]), (g[10], g[11]))
    dab, da_log, ddt = gates_vjp(dgates)
    dpre, dconv = [], []
    for (col, nrm, scl, nm), df, dr in zip(A_PIECES, (g[0], g[2], g[4]), (g[1], g[3], g[5])):
        dx, dw = _a_prep_bwd(proj, conv8, col, nrm, scl, df, dr, nm)
        dpre.append(dx)
        dconv.append(dw[:A_CONV])

    dproj = jnp.concatenate(dpre + [daz, dbq, dbz, dcq, dck, dcv, dcz, dma, dmb, dmc, dbkv,
                                    jnp.pad(dab, ((0, 0), (0, IN_PAD - C_AB - 16)))], axis=1).astype(BF16)
    dh = _matmul(dproj, w16, "w_in_da", "nt", W_IN_TILES["nt"])
    dw = _matmul(h.T.astype(BF16), dproj, "w_in_db", "nn", W_IN_TILES["db"])
    dp = dict(a_conv_w=jnp.concatenate(dconv, axis=1), a_log=da_log, a_dt_bias=ddt, a_norm_w=danw[0],
              b_sink=dsink[:, 0], c_decay=dc_decay, c_norm_w=dcnw[0], w_branch=dwb)
    return dh, dw, dp, (ride_delta, ride_ret)


CORE_PARAMS = ("a_conv_w", "a_log", "a_dt_bias", "a_norm_w", "b_sink", "c_decay", "c_norm_w", "w_branch")


def _pad_w_in(w):
    return jnp.concatenate([w[..., 0:2048], w[..., 2064:2576], w[..., 2832:3344], w[..., 3344:8464],
                            w[..., 2576:2832], w[..., 2048:2064],
                            jnp.zeros(w.shape[:-1] + (IN_PAD - IN_WIDTH,), w.dtype)], axis=-1)


def _unpad_w_in(g):
    return jnp.concatenate([g[..., 0:2048], g[..., C_AB:C_AB + 16], g[..., C_BQ:C_BQ + 512],
                            g[..., C_BKV:C_BKV + 256], g[..., C_BZ:C_BZ + 512], g[..., C_CQ:C_BKV]], axis=-1)


LAYER_SHARDED = ("w_ada", "w_in", "w_branch", "w_out")
RIDE_DELTA, RIDE_RET = ("w_in",), ("w_ada", "w_branch", "w_out")


def _unshard_layer(name, g):
    if name == "w_branch":
        return g.transpose(1, 2, 0, 3).reshape(3, BR_WIDTH, D_MODEL)
    if name == "w_out":
        return g.reshape(D_MODEL, D_MODEL)
    return g.transpose(1, 0, 2).reshape(D_MODEL, -1)


def _reshard_layer(name, w):
    if name == "w_branch":
        return w.reshape(3, BR_WIDTH, N_DEV, D_MODEL // N_DEV).transpose(2, 0, 1, 3)
    if name == "w_out":
        return w.reshape(N_DEV, D_MODEL // N_DEV, D_MODEL)
    return w.reshape(D_MODEL, N_DEV, -1).transpose(1, 0, 2)


def _layer_weights(gathered):
    out = {n: _unshard_layer(n, g) for n, g in gathered.items()}
    out["w_in16"] = _pad_w_in(out.pop("w_in"))
    return out


def _forward_backward(small, layer0, shards1, x, c, ctx, loss_target):
    c_ctx = small["c_ctx"]
    sc16 = jnp.zeros((16, D_MODEL), F32).at[0].set(_silu(c)).at[1].set(_silu(c_ctx))
    xs = jnp.concatenate([ctx, x], axis=0)
    weights = [layer0, None]
    layers = []
    for l in range(DEPTH):
        last = l == DEPTH - 1
        wl = weights[l]
        mod16 = _matmul(sc16, wl["w_ada"], "ada") + small["b_ada"][l][None, :]
        mod_cx = jnp.stack([mod16[1], mod16[0]])
        shift, scale, gate = jnp.split(mod_cx, 3, axis=1)
        nw = small["norm_w"][l][None, :]
        (h,) = _norm_mod.fwd([xs], [nw, shift, scale])
        p = {n: small[n][l] for n in CORE_PARAMS if n != "w_branch"}
        p["w_branch"] = wl["w_branch"]
        rides = None
        if l == 0:
            rides = dict(delta=([shards1[n] for n in RIDE_DELTA], True), ret=([shards1[n] for n in RIDE_RET], True))
        merged, saved, (ride_delta, ride_ret) = _core_forward(h, wl["w_in16"], p, rides)
        if l == 0:
            weights[1] = _layer_weights(dict(zip(RIDE_DELTA + RIDE_RET, list(ride_delta) + list(ride_ret))))
        xs_in = xs
        if last:
            merged, xs_in, gate = merged[CTX_LEN:], xs[CTX_LEN:], gate[1:2]
        out = _matmul(merged, wl["w_out"], "w_out")
        (xs_next,) = _residual.fwd([xs_in, out], [gate])
        layers.append(dict(xs=xs, h=h, p=p, saved=saved, merged=merged, xs_in=xs_in, gate=gate, out=out, nw=nw,
                           shift=shift, scale=scale))
        xs = xs_next
    fw = small["final_norm_w"][None, :]
    (per_row,) = _loss_rows.fwd([xs, loss_target], [fw])
    loss = jnp.sum(per_row[:, 0])

    d_per_row = jnp.zeros((SEQ, 128), F32).at[:, 0].set(1.0)
    (dxs,), (dfw,) = _loss_rows.bwd([xs, loss_target], [fw], [d_per_row])
    small_names = tuple(n for n in CORE_PARAMS if n != "w_branch") + ("b_ada", "norm_w")
    dsmall = {n: [None] * DEPTH for n in small_names}
    dlayer = [None] * DEPTH
    contrib1 = None
    dsc16 = jnp.zeros((16, D_MODEL), F32)
    for l in reversed(range(DEPTH)):
        s, wl = layers[l], weights[l]
        (dres, dout), (dgate,) = _residual.bwd([s["xs_in"], s["out"]], [s["gate"]], [dxs])
        dmerged = _matmul(dout, wl["w_out"], "w_out_da", "nt")
        dw_out = _matmul(s["merged"], dout, "w_out_db", "tn")
        if l == DEPTH - 1:
            ctx_rows = ((CTX_LEN, 0), (0, 0))
            dmerged, dres = jnp.pad(dmerged, ctx_rows), jnp.pad(dres, ctx_rows)
            dgate = jnp.pad(dgate, ((1, 0), (0, 0)))
        rides = None
        if l == 0:
            blocks1 = {n: _reshard_layer(n, g).astype(BF16) for n, g in dlayer[1].items()}
            rides = dict(delta=([blocks1[n] for n in RIDE_DELTA], False), ret=([blocks1[n] for n in RIDE_RET], False))
        dh, dw_in, dp, (ride_delta, ride_ret) = _core_backward(s["h"], wl["w_in16"], s["p"], s["saved"], dmerged, rides)
        if l == 0:
            contrib1 = dict(zip(RIDE_DELTA + RIDE_RET, list(ride_delta) + list(ride_ret)))
        (dxn,), (dnw, dshift, dscale) = _norm_mod.bwd([s["xs"]], [s["nw"], s["shift"], s["scale"]], [dh])
        dxs = dres + dxn
        dmod_cx = jnp.concatenate([dshift, dscale, dgate], axis=1)
        dmod16 = jnp.zeros((16, 3 * D_MODEL), F32).at[0].set(dmod_cx[1]).at[1].set(dmod_cx[0])
        dsc16 = dsc16 + _matmul(dmod16, wl["w_ada"], "ada_da", "nt")
        dlayer[l] = dict(w_ada=_matmul(sc16, dmod16, "ada_db", "tn"), w_in=_unpad_w_in(dw_in),
                         w_branch=dp["w_branch"], w_out=dw_out)
        for n in small_names:
            if n in dp:
                dsmall[n][l] = dp[n]
        dsmall["norm_w"][l] = dnw[0]
        dsmall["b_ada"][l] = dmod_cx[0] + dmod_cx[1]
    gsmall = {n: jnp.stack(v) for n, v in dsmall.items()}
    gsmall["final_norm_w"] = dfw[0]
    sig = jax.nn.sigmoid(c_ctx)
    gsmall["c_ctx"] = dsc16[1] * sig * (1.0 + c_ctx * (1.0 - sig))
    return loss, dxs[CTX_LEN:], gsmall, dlayer[0], contrib1


SHARDED = ("w_ada", "w_in", "a_conv_w", "w_branch", "w_out")
SMALL = ("c_ctx", "b_ada", "norm_w", "a_log", "a_dt_bias", "a_norm_w", "b_sink", "c_decay", "c_norm_w",
         "final_norm_w")
WEIGHTS = ("c_ctx", "w_ada", "b_ada", "norm_w", "w_in", "a_conv_w", "a_log", "a_dt_bias", "a_norm_w", "b_sink",
           "c_decay", "c_norm_w", "w_branch", "w_out", "final_norm_w")
SMALL_PACK = 12288


def _unshard_conv(g):
    return g.transpose(1, 2, 0, 3).reshape(DEPTH, A_CONV, 3 * A_WIDTH)


def _reshard_conv(w):
    return w.reshape(DEPTH, A_CONV, N_DEV, 3 * A_WIDTH // N_DEV).transpose(2, 0, 1, 3)


def _pack_small(tree):
    flat = jnp.concatenate([tree[n].reshape(-1) for n in SMALL])
    return jnp.pad(flat, (0, SMALL_PACK - flat.shape[0])).reshape(SMALL_PACK // 128, 128)


def _unpack_small(packed, like):
    flat = packed.reshape(-1)
    out, off = {}, 0
    for n in SMALL:
        size = math.prod(like[n].shape)
        out[n] = flat[off:off + size].reshape(like[n].shape)
        off += size
    return out


def kernel(x, c, ctx, c_ctx, w_ada, b_ada, norm_w, w_in, a_conv_w, a_log, a_dt_bias, a_norm_w, b_sink, c_decay, c_norm_w, w_branch, w_out, final_norm_w, loss_target, m_c_ctx, m_w_ada, m_b_ada, m_norm_w, m_w_in, m_a_conv_w, m_a_log, m_a_dt_bias, m_a_norm_w, m_b_sink, m_c_decay, m_c_norm_w, m_w_branch, m_w_out, m_final_norm_w, v_c_ctx, v_w_ada, v_b_ada, v_norm_w, v_w_in, v_a_conv_w, v_a_log, v_a_dt_bias, v_a_norm_w, v_b_sink, v_c_decay, v_c_norm_w, v_w_branch, v_w_out, v_final_norm_w):
    w = dict(c_ctx=c_ctx, w_ada=w_ada, b_ada=b_ada, norm_w=norm_w, w_in=w_in, a_conv_w=a_conv_w, a_log=a_log,
             a_dt_bias=a_dt_bias, a_norm_w=a_norm_w, b_sink=b_sink, c_decay=c_decay, c_norm_w=c_norm_w,
             w_branch=w_branch, w_out=w_out, final_norm_w=final_norm_w)
    m = dict(c_ctx=m_c_ctx, w_ada=m_w_ada, b_ada=m_b_ada, norm_w=m_norm_w, w_in=m_w_in, a_conv_w=m_a_conv_w,
             a_log=m_a_log, a_dt_bias=m_a_dt_bias, a_norm_w=m_a_norm_w, b_sink=m_b_sink, c_decay=m_c_decay,
             c_norm_w=m_c_norm_w, w_branch=m_w_branch, w_out=m_w_out, final_norm_w=m_final_norm_w)
    v = dict(c_ctx=v_c_ctx, w_ada=v_w_ada, b_ada=v_b_ada, norm_w=v_norm_w, w_in=v_w_in, a_conv_w=v_a_conv_w,
             a_log=v_a_log, a_dt_bias=v_a_dt_bias, a_norm_w=v_a_norm_w, b_sink=v_b_sink, c_decay=v_c_decay,
             c_norm_w=v_c_norm_w, w_branch=v_w_branch, w_out=v_w_out, final_norm_w=v_final_norm_w)

    shards = {n: w[n].astype(BF16) for n in LAYER_SHARDED}
    first = _exchange([shards[n][0] for n in LAYER_SHARDED] + [w["a_conv_w"]], True, "gather_layer0")
    layer0 = _layer_weights(dict(zip(LAYER_SHARDED, first)))
    small_w = {n: w[n] for n in SMALL}
    small_w["a_conv_w"] = _unshard_conv(first[len(LAYER_SHARDED)])
    loss, gx, gw, glayer0, contrib1 = _forward_backward(small_w, layer0, {n: shards[n][1] for n in LAYER_SHARDED},
                                                        x[0], c[0], ctx[0], loss_target[0])
    loss = lax.psum(loss, ("x", "y", "c"))

    last = _exchange([_reshard_layer(n, glayer0[n]).astype(BF16) for n in LAYER_SHARDED]
                     + [_reshard_conv(gw["a_conv_w"])], False, "scatter_layer0")
    blocks = {n: jnp.stack([c0, contrib1[n]], axis=1) for n, c0 in zip(LAYER_SHARDED, last)}
    blocks["a_conv_w"] = last[len(LAYER_SHARDED)]
    small = _exchange([_pack_small(gw)], True, "gather_small_grads")[0]

    grad, delta, new_m, new_v = {}, {}, {}, {}
    for n in SHARDED:
        contrib = blocks[n]
        shp = w[n].shape
        two_d = (math.prod(shp[:-1]), shp[-1])
        outs = _adamw(w[n].reshape(two_d), m[n].reshape(two_d), v[n].reshape(two_d),
                      contrib.reshape((N_DEV,) + two_d), "adamw_" + n)
        grad[n], delta[n], new_m[n], new_v[n] = [o.reshape(shp) for o in outs]
    outs = _adamw(_pack_small(w), _pack_small(m), _pack_small(v), small, "adamw_small")
    for tree, packed in zip((grad, delta, new_m, new_v), outs):
        tree.update(_unpack_small(packed, w))

    return (loss, gx[None], *[grad[n] for n in WEIGHTS], *[delta[n] for n in WEIGHTS],
            *[new_m[n] for n in WEIGHTS], *[new_v[n] for n in WEIGHTS])
```

```python
import functools
import math

import jax
import jax.numpy as jnp
from jax import lax
from jax.experimental import pallas as pl
from jax.experimental.pallas import tpu as pltpu

F32 = jnp.float32
BF16 = jnp.bfloat16
INV_PRECISION = lax.Precision.HIGH

D_MODEL = 1024
SEQ = 4096
DEPTH = 2
GRID_W = 64
CTX_LEN = 256
EPS = 1e-6
ROPE_BASE = 10000.0
BR_WIDTH = D_MODEL // 2
A_DK = 128
A_HEADS = 4
A_WIDTH = 512
A_CONV = 5
B_HD = 64
B_Q_HEADS = 8
B_KV_HEADS = 2
WINDOW = 128
B_BLOCK = 128
C_HD = 128
C_HEADS = 4
C_WIDTH = 512
CHUNK = 64
ADAM_LR = 0.001
ADAM_B1 = 0.9
ADAM_B2 = 0.999
ADAM_EPS = 1e-08
ADAM_WD = 0.01
ADAM_STEP = 10

N_DEV = 8
ROWS = CTX_LEN + SEQ
N_CHUNK = ROWS // CHUNK
N_CTX_CHUNK = CTX_LEN // CHUNK
IN_WIDTH = 8464
IN_PAD = 8704
NEG = -1e30

VMEM_LIMIT = 48 * 1024 * 1024
MESH = pl.DeviceIdType.MESH

C_AQ, C_AK, C_AV, C_AZ, C_BQ, C_BZ, C_CQ, C_CK, C_CV, C_CZ = (i * 512 for i in range(10))
C_MERGE = 5120
C_BKV = 8192
C_AB = 8448


def _cparams(sem=None):
    if sem is None:
        return pltpu.CompilerParams(vmem_limit_bytes=VMEM_LIMIT)
    return pltpu.CompilerParams(dimension_semantics=sem, vmem_limit_bytes=VMEM_LIMIT)


def _dg(a, b, ca, cb, prec=None):
    return lax.dot_general(a, b, (((ca,), (cb,)), ((), ())), preferred_element_type=F32, precision=prec)


@functools.partial(jax.custom_vjp, nondiff_argnums=(2, 3))
def _bdot(a, b, ca, cb):
    return _dg(a.astype(BF16), b.astype(BF16), ca, cb)


def _bdot_fwd(a, b, ca, cb):
    return _bdot(a, b, ca, cb), (a, b)


def _bdot_bwd(ca, cb, res, ct):
    a, b = res
    da = _bdot(ct, b, 1, 1 - cb) if ca == 1 else _bdot(b, ct, 1 - cb, 1)
    db = _bdot(a, ct, 1 - ca, 0) if cb == 0 else _bdot(ct, a, 0, 1 - ca)
    return da, db


_bdot.defvjp(_bdot_fwd, _bdot_bwd)


def _hdot(a, b):
    return _dg(a, b, 1, 0, INV_PRECISION)


def _k_silu(x):
    return x / (1.0 + jnp.exp(-x))


def _k_sigmoid(x):
    return 1.0 / (1.0 + jnp.exp(-x))


@jax.custom_vjp
def _swap64(x):
    return pltpu.roll(x, 64, 1)


_swap64.defvjp(lambda x: (pltpu.roll(x, 64, 1), None), lambda _, ct: (pltpu.roll(ct, 64, 1),))


def _swap16_impl(x):
    lane = lax.broadcasted_iota(jnp.int32, x.shape, 1)
    return jnp.where((lane & 16) == 0, pltpu.roll(x, 112, 1), pltpu.roll(x, 16, 1))


@jax.custom_vjp
def _swap16(x):
    return _swap16_impl(x)


_swap16.defvjp(lambda x: (_swap16_impl(x), None), lambda _, ct: (_swap16_impl(ct),))


def _pick(dim, prefs):
    for p in prefs:
        if dim % p == 0:
            return p
    return dim


def _matmul(a, b, name, mode="nn", tiles=None):
    ca, cb = {"nn": (1, 0), "nt": (1, 1), "tn": (0, 0)}[mode]
    m, k = a.shape[1 - ca], a.shape[ca]
    n = b.shape[1 - cb]
    if tiles is None:
        tiles = (_pick(m, (1088, 1024, 512, 256, 128)), _pick(n, (512, 256, 128)),
                 _pick(k, (1088, 1024, 512, 256, 128) if mode == "tn" else (2176, 2048, 1024, 512, 256, 128)))
    tm, tn, tk = tiles
    nk = k // tk
    a_spec = (pl.BlockSpec((tm, tk), lambda i, j, kk: (i, kk)) if ca == 1
              else pl.BlockSpec((tk, tm), lambda i, j, kk: (kk, i)))
    b_spec = (pl.BlockSpec((tk, tn), lambda i, j, kk: (kk, j)) if cb == 0
              else pl.BlockSpec((tn, tk), lambda i, j, kk: (j, kk)))

    def body(a_ref, b_ref, o_ref):
        part = _dg(a_ref[...].astype(BF16), b_ref[...].astype(BF16), ca, cb)
        if nk == 1:
            o_ref[...] = part
        else:
            kk = pl.program_id(2)

            @pl.when(kk == 0)
            def _():
                o_ref[...] = part

            @pl.when(kk > 0)
            def _():
                o_ref[...] += part

    return pl.pallas_call(
        body,
        grid=(m // tm, n // tn, nk),
        in_specs=[a_spec, b_spec],
        out_specs=pl.BlockSpec((tm, tn), lambda i, j, kk: (i, j)),
        out_shape=jax.ShapeDtypeStruct((m, n), F32),
        compiler_params=_cparams(("parallel", "parallel", "arbitrary")),
        name=name,
    )(a, b)


ROW_BLOCK = 256
ROW_VMEM_BUDGET = 16 * 1024 * 1024


def _pieces(val, pw):
    return [val[:, j * pw:(j + 1) * pw] for j in range(val.shape[1] // pw)]


def _flat(groups):
    arrays, sizes = [], []
    for g in groups:
        g = g if isinstance(g, (tuple, list)) else (g,)
        arrays += list(g)
        sizes.append(len(g))
    return arrays, sizes


def _regroup(refs, sizes):
    out, at = [], 0
    for n in sizes:
        val = refs[at][...]
        for r in refs[at + 1:at + n]:
            val = val + r[...]
        out.append(val)
        at += n
    return out


class _Rowwise:
    def __init__(self, fn, name, row_wpw, par_pw, out_wpw, n_diff=None):
        self.fn, self.name, self.row_wpw, self.par_pw, self.out_wpw = fn, name, row_wpw, par_pw, out_wpw
        self.n_diff = len(row_wpw) if n_diff is None else n_diff

        @jax.custom_vjp
        def call(rows, params):
            return self.fwd(rows, params)

        def call_fwd(rows, params):
            return self.fwd(rows, params), (rows, params)

        def call_bwd(res, douts):
            return self.bwd(res[0], res[1], douts)

        call.defvjp(call_fwd, call_bwd)
        self.call = call

    def _load(self, row_vals, par_refs, br, with_ctx):
        row = pl.program_id(0) * br + lax.broadcasted_iota(jnp.int32, (br, 1), 0)
        is_ctx = (row < (CTX_LEN if with_ctx else 0)).astype(F32)
        rows = [_pieces(v, pw) for v, (_, pw) in zip(row_vals, self.row_wpw)]
        pars = []
        for p, pw in zip(par_refs, self.par_pw):
            val = p[...].astype(F32)
            if p.shape[0] == 2:
                val = is_ctx * val[0:1, :] + (1.0 - is_ctx) * val[1:2, :]
            pars.append(_pieces(val, pw))
        return rows, pars, is_ctx

    def _block_rows(self, n_rows, widths):
        for br in (1088, 1024, 544, 512, 272):
            if n_rows % br == 0 and 2 * 4 * br * sum(widths) <= ROW_VMEM_BUDGET:
                return br
        return ROW_BLOCK

    def _row_specs(self, br, sizes, cols):
        out = []
        for (w, _), n, c in zip(self.row_wpw, sizes, cols):
            out += [pl.BlockSpec((br, w), lambda i, c=c: (i, c))] * n
        return out

    def fwd(self, rows, params, cols=None):
        arrays, sizes = _flat(rows)
        cols = cols or [0] * len(rows)
        n_rows = arrays[0].shape[0]
        n_in = len(arrays)
        br = self._block_rows(n_rows, [w for (w, _), n in zip(self.row_wpw, sizes) for _ in range(n)]
                              + [w for w, _ in self.out_wpw])

        def body(*refs):
            r, p, _ = self._load(_regroup(refs[:n_in], sizes), refs[n_in:n_in + len(params)], br, n_rows == ROWS)
            for o_ref, pieces, (_, pw) in zip(refs[n_in + len(params):], self.fn(r, p), self.out_wpw):
                for j, piece in enumerate(pieces):
                    o_ref[:, j * pw:(j + 1) * pw] = piece

        return pl.pallas_call(
            body,
            grid=(n_rows // br,),
            in_specs=self._row_specs(br, sizes, cols) + [pl.BlockSpec(p.shape, lambda i: (0, 0)) for p in params],
            out_specs=[pl.BlockSpec((br, w), lambda i: (i, 0)) for w, _ in self.out_wpw],
            out_shape=[jax.ShapeDtypeStruct((n_rows, w), F32) for w, _ in self.out_wpw],
            compiler_params=_cparams(("parallel",)),
            name=self.name + "_fwd",
        )(*arrays, *params)

    def bwd(self, rows, params, douts, cols=None):
        arrays, sizes = _flat(rows)
        darrays, dsizes = _flat(douts)
        cols = cols or [0] * len(rows)
        n_rows = arrays[0].shape[0]
        n_in, n_par, n_dout, n_diff = len(arrays), len(params), len(darrays), self.n_diff
        br = self._block_rows(n_rows, [w for (w, _), n in zip(self.row_wpw, sizes) for _ in range(n)]
                              + [w for (w, _), n in zip(self.out_wpw, dsizes) for _ in range(n)]
                              + [w for w, _ in self.row_wpw[:n_diff]])

        def body(*refs):
            par_refs = refs[n_in:n_in + n_par]
            dout_refs = refs[n_in + n_par:n_in + n_par + n_dout]
            drow_refs = refs[n_in + n_par + n_dout:n_in + n_par + n_dout + n_diff]
            dpar_refs = refs[n_in + n_par + n_dout + n_diff:]

            @pl.when(pl.program_id(0) == 0)
            def _():
                for d in dpar_refs:
                    d[...] = jnp.zeros_like(d)

            r, p, is_ctx = self._load(_regroup(refs[:n_in], sizes), par_refs, br, n_rows == ROWS)
            cts = [_pieces(d, pw) for d, (_, pw) in zip(_regroup(dout_refs, dsizes), self.out_wpw)]
            fixed = r[n_diff:]
            _, vjp = jax.vjp(lambda rd, pp: self.fn(rd + fixed, pp), r[:n_diff], p)
            dr, dp = vjp(cts)
            for d_ref, pieces, (_, pw) in zip(drow_refs, dr, self.row_wpw):
                for j, piece in enumerate(pieces):
                    d_ref[:, j * pw:(j + 1) * pw] = piece
            for d_ref, pieces, pw in zip(dpar_refs, dp, self.par_pw):
                for j, piece in enumerate(pieces):
                    lanes = slice(j * pw, (j + 1) * pw)
                    if d_ref.shape[0] != 2:
                        d_ref[:, lanes] += piece
                    else:
                        d_ref[0:1, lanes] += jnp.sum(is_ctx * piece, axis=0, keepdims=True)
                        d_ref[1:2, lanes] += jnp.sum((1.0 - is_ctx) * piece, axis=0, keepdims=True)

        par_specs = [pl.BlockSpec(p.shape, lambda i: (0, 0)) for p in params]
        dout_specs = []
        for (w, _), n in zip(self.out_wpw, dsizes):
            dout_specs += [pl.BlockSpec((br, w), lambda i: (i, 0))] * n
        drow_w = [w for w, _ in self.row_wpw[:n_diff]]
        g = pl.pallas_call(
            body,
            grid=(n_rows // br,),
            in_specs=self._row_specs(br, sizes, cols) + par_specs + dout_specs,
            out_specs=[pl.BlockSpec((br, w), lambda i: (i, 0)) for w in drow_w] + par_specs,
            out_shape=[jax.ShapeDtypeStruct((n_rows, w), F32) for w in drow_w]
            + [jax.ShapeDtypeStruct(p.shape, F32) for p in params],
            compiler_params=_cparams(("arbitrary",)),
            name=self.name + "_bwd",
        )(*arrays, *params, *darrays)
        return list(g[:n_diff]), list(g[n_diff:])


def _fn_norm_mod(rows, pars):
    (x,), (nw,), (shift,), (scale,) = rows[0], pars[0], pars[1], pars[2]
    y = x * lax.rsqrt(jnp.mean(x * x, axis=-1, keepdims=True) + EPS) * nw
    return [[y * (1.0 + scale) + shift]]


def _fn_head_rms_gate(rows, pars):
    (w,) = pars[0]
    return [[o * lax.rsqrt(jnp.mean(o * o, axis=-1, keepdims=True) + EPS) * w * _k_silu(z)
             for o, z in zip(rows[0], rows[1])]]


def _fn_group_norm_gate(rows, pars):
    out = []
    for o, z, w in zip(rows[0], rows[1], pars[0]):
        mu = jnp.mean(o, axis=-1, keepdims=True)
        var = jnp.mean(jnp.square(o - mu), axis=-1, keepdims=True)
        out.append((o - mu) * lax.rsqrt(var + EPS) * w * _k_silu(z))
    return [out]


def _fn_gate(rows, pars):
    return [[o * _k_silu(z) for o, z in zip(rows[0], rows[1])]]


def _fn_branch_merge(rows, pars):
    (ya,), (yb,), (yc,), (ma,), (mb,), (mc,) = rows
    (wa,), (wb,), (wc,) = pars
    return [[_k_sigmoid(ma) * _bdot(ya, wa, 1, 0) + _k_sigmoid(mb) * _bdot(yb, wb, 1, 0)
             + _k_sigmoid(mc) * _bdot(yc, wc, 1, 0)]]


def _fn_out_residual(rows, pars):
    (res,), (merged,), (w,), (gate,) = rows[0], rows[1], pars[0], pars[1]
    return [[res + gate * _bdot(merged, w, 1, 0)]]


def _fn_loss(rows, pars):
    (x,), (target,), (w,) = rows[0], rows[1], pars[0]
    y = x * lax.rsqrt(jnp.mean(x * x, axis=-1, keepdims=True) + EPS) * w
    per_row = 0.5 * jnp.mean(jnp.square(y - target), axis=-1, keepdims=True)
    return [[jnp.broadcast_to(per_row, (per_row.shape[0], 128))]]


def _fn_b_rope(rows, pars):
    q, (k, v), (cos,), (sin,) = rows
    rot = lambda x: x * cos + _swap16(x) * sin
    return [[rot(x) for x in q], [rot(k), v]]


def _fn_c_rope(rows, pars):
    q, k, (cos,), (sin,) = rows
    rot = lambda x: x * cos + _swap64(x) * sin
    return [[rot(x) for x in q], [rot(x) * (C_HD ** -0.5) for x in k]]


_norm_mod = _Rowwise(_fn_norm_mod, "norm_mod", [(D_MODEL, D_MODEL)], [D_MODEL] * 3, [(D_MODEL, D_MODEL)])
_out_residual = _Rowwise(_fn_out_residual, "out_residual", [(D_MODEL, D_MODEL)] * 2, [D_MODEL] * 2,
                         [(D_MODEL, D_MODEL)])
_loss_rows = _Rowwise(_fn_loss, "loss", [(D_MODEL, D_MODEL)] * 2, [D_MODEL], [(128, 128)], n_diff=1)
_a_out = _Rowwise(_fn_head_rms_gate, "a_out", [(512, 128)] * 2, [128], [(512, 128)])
_c_out = _Rowwise(_fn_group_norm_gate, "c_out", [(512, 128)] * 2, [128], [(512, 128)])
_b_out = _Rowwise(_fn_gate, "b_out", [(512, 512)] * 2, [], [(512, 512)])
_branch_merge = _Rowwise(_fn_branch_merge, "branch_merge", [(512, 512)] * 3 + [(D_MODEL, D_MODEL)] * 3,
                         [D_MODEL] * 3, [(D_MODEL, D_MODEL)])
_b_rope = _Rowwise(_fn_b_rope, "b_rope", [(512, 128), (256, 128), (128, 128), (128, 128)], [],
                   [(512, 128), (256, 128)], n_diff=2)
_c_rope = _Rowwise(_fn_c_rope, "c_rope", [(512, 128), (512, 128), (128, 128), (128, 128)], [],
                   [(512, 128), (512, 128)], n_diff=2)


HALO = 8
EXT = ROW_BLOCK + 2 * HALO


def _halo_specs(col, width=512):
    last = ROWS // HALO - 1
    per = ROW_BLOCK // HALO
    prev = pl.BlockSpec((HALO, width), lambda i: (jnp.maximum(i * per - 1, 0), col))
    cur = pl.BlockSpec((ROW_BLOCK, width), lambda i: (i, col))
    nxt = pl.BlockSpec((HALO, width), lambda i: (jnp.minimum((i + 1) * per, last), col))
    return [prev, cur, nxt]


def _extended(prev_ref, cur_ref, next_ref):
    i = pl.program_id(0)
    prev_ok = i >= 2
    next_ok = jnp.logical_and(i >= 1, i < ROWS // ROW_BLOCK - 1)
    return jnp.concatenate([jnp.where(prev_ok, prev_ref[...], 0.0), cur_ref[...],
                            jnp.where(next_ok, next_ref[...], 0.0)], axis=0)


def _conv_taps(x_ext, w_ref, flip):
    acc = None
    for j in range(A_CONV):
        shift = (j - 2) if flip else (2 - j)
        term = w_ref[j:j + 1, :] * pltpu.roll(x_ext, shift % EXT, 0)
        acc = term if acc is None else acc + term
    return acc


def _conv_post(pre_pieces, normalize, scale):
    out = []
    for p in pre_pieces:
        y = _k_silu(p)
        if normalize:
            y = y * lax.rsqrt(jnp.sum(y * y, axis=-1, keepdims=True) + EPS) * scale
        out.append(y)
    return out


def _a_prep_fwd(proj, conv8, col, normalize, scale, name):
    def body(prev_ref, cur_ref, next_ref, w_ref, o_ref):
        pre = _conv_taps(_extended(prev_ref, cur_ref, next_ref), w_ref, False)[HALO:HALO + ROW_BLOCK]
        for h, y in enumerate(_conv_post(_pieces(pre, 128), normalize, scale)):
            o_ref[:, h * 128:(h + 1) * 128] = y

    return pl.pallas_call(
        body,
        grid=(ROWS // ROW_BLOCK,),
        in_specs=_halo_specs(col) + [pl.BlockSpec((8, 512), lambda i: (0, col))],
        out_specs=pl.BlockSpec((ROW_BLOCK, 512), lambda i: (i, 0)),
        out_shape=jax.ShapeDtypeStruct((ROWS, 512), F32),
        compiler_params=_cparams(("parallel",)),
        name=name + "_fwd",
    )(proj, proj, proj, conv8)


def _a_prep_bwd(proj, conv8, col, normalize, scale, dout_f, dout_r, name):
    def body(xp, xc, xn, w_ref, fp, fc, fn_, rp, rc, rn, dx_ref, dw_ref):
        @pl.when(pl.program_id(0) == 0)
        def _():
            dw_ref[...] = jnp.zeros_like(dw_ref)

        x_ext = _extended(xp, xc, xn)
        dout = _extended(fp, fc, fn_) + _extended(rp, rc, rn)
        pre = _conv_taps(x_ext, w_ref, False)
        _, vjp = jax.vjp(lambda p: _conv_post(p, normalize, scale), _pieces(pre, 128))
        (dpre,) = vjp(_pieces(dout, 128))
        dpre = jnp.concatenate(dpre, axis=1)
        dx_ref[...] = _conv_taps(dpre, w_ref, True)[HALO:HALO + ROW_BLOCK]
        own = dpre[HALO:HALO + ROW_BLOCK]
        for j in range(A_CONV):
            shifted = pltpu.roll(x_ext, (2 - j) % EXT, 0)[HALO:HALO + ROW_BLOCK]
            dw_ref[j:j + 1, :] += jnp.sum(own * shifted, axis=0, keepdims=True)

    return pl.pallas_call(
        body,
        grid=(ROWS // ROW_BLOCK,),
        in_specs=_halo_specs(col) + [pl.BlockSpec((8, 512), lambda i: (0, col))] + _halo_specs(0) + _halo_specs(0),
        out_specs=[pl.BlockSpec((ROW_BLOCK, 512), lambda i: (i, 0)), pl.BlockSpec((8, 512), lambda i: (0, 0))],
        out_shape=[jax.ShapeDtypeStruct((ROWS, 512), F32), jax.ShapeDtypeStruct((8, 512), F32)],
        compiler_params=_cparams(("arbitrary",)),
        name=name + "_bwd",
    )(proj, proj, proj, conv8, dout_f, dout_f, dout_f, dout_r, dout_r, dout_r)


N_CHAIN = 8


def _rev_chunk(s):
    return jnp.where(s < N_CTX_CHUNK, N_CTX_CHUNK - 1 - s, N_CHUNK + N_CTX_CHUNK - 1 - s)


def _scan_specs(step_of, v_col=0):
    cf = step_of
    cr = lambda n: _rev_chunk(step_of(n))

    def pair(shape, index):
        return (pl.BlockSpec(shape, lambda n: index(cf(n))), pl.BlockSpec(shape, lambda n: index(cr(n))))

    return dict(
        tok=pair((CHUNK, 512), lambda c: (c, 0)),
        tokv=pair((CHUNK, 512), lambda c: (c, v_col)),
        col=pair((4, CHUNK, 1), lambda c: (0, c, 0)),
        row=pair((4, 1, 1, CHUNK), lambda c: (0, c, 0, 0)),
        one=pair((4, 1, 1, 1), lambda c: (0, c, 0, 0)),
        state=pair((None, 4, 128, 128), lambda c: (c, 0, 0, 0)),
        tinv=pair((None, 4, CHUNK, CHUNK), lambda c: (c, 0, 0, 0)),
    )


def _both(specs, kinds):
    out = []
    for kind in kinds:
        out += list(specs[kind])
    return out


def _scan_call(body, name, in_specs, out_specs, out_shape, operands, ride):
    body, r_in, r_out, r_shape, r_scratch = _riding(body, len(in_specs), len(out_specs), 1, ride, N_CHUNK)
    return pl.pallas_call(
        body,
        grid=(N_CHUNK,),
        in_specs=in_specs + r_in,
        out_specs=out_specs + r_out,
        out_shape=out_shape + r_shape,
        scratch_shapes=[pltpu.VMEM((N_CHAIN, 128, 128), F32)] + r_scratch,
        compiler_params=_cparams(("arbitrary",)),
        name=name,
    )(*operands, *(ride[0] if ride else []))


def _chain_masks():
    ii = lax.broadcasted_iota(jnp.int32, (CHUNK, CHUNK), 0)
    jj = lax.broadcasted_iota(jnp.int32, (CHUNK, CHUNK), 1)
    eye = jnp.where(ii == jj, 1.0, 0.0).astype(F32)
    lower = (ii >= jj, ii > jj)
    upper = (ii <= jj, ii < jj)
    return [lower] * 4 + [upper] * 4, eye


def _tri_inv_all(ls, eye):
    xs = [eye - l for l in ls]
    ps = [_hdot(l, l) for l in ls]
    for i in range(5):
        xs = [x + _hdot(x, p) for x, p in zip(xs, ps)]
        if i < 4:
            ps = [_hdot(p, p) for p in ps]
    return xs


@jax.custom_vjp
def _inv_saved(l, x):
    return x


def _inv_saved_fwd(l, x):
    return x, x


def _inv_saved_bwd(x, dx):
    return -_bdot(x, _bdot(dx, x, 1, 1), 0, 0), jnp.zeros_like(x)


_inv_saved.defvjp(_inv_saved_fwd, _inv_saved_bwd)


def _delta_chains(q, k, v, beta_r, gcr, gl, s, masks, eye, tinv_saved):
    n = range(len(q))
    beta = [jnp.sum(eye * beta_r[i], axis=1, keepdims=True) for i in n]
    gcc = [jnp.sum(eye * gcr[i], axis=1, keepdims=True) for i in n]
    decay = [jnp.exp(jnp.where(masks[i][0], gcc[i] - gcr[i], NEG)) for i in n]
    kb = [k[i] * beta[i] for i in n]
    lmat = [jnp.where(masks[i][1], _bdot(kb[i], k[i], 1, 1) * decay[i], 0.0) for i in n]
    if tinv_saved is None:
        tinv = _tri_inv_all(lmat, eye)
    else:
        tinv = [_inv_saved(lmat[i], tinv_saved[i]) for i in n]
    eg = [jnp.exp(gcc[i]) for i in n]
    u = [_bdot(tinv[i], v[i] * beta[i], 1, 0) for i in n]
    w = [_bdot(tinv[i], kb[i] * eg[i], 1, 0) for i in n]
    qk = [_bdot(q[i], k[i], 1, 1) * decay[i] for i in n]
    v_new = [u[i] - _bdot(w[i], s[i], 1, 0) for i in n]
    o = [_bdot(q[i] * eg[i], s[i], 1, 0) + _bdot(qk[i], v_new[i], 1, 0) for i in n]
    s_new = [s[i] * jnp.exp(gl[i]) + _bdot(k[i] * jnp.exp(gl[i] - gcc[i]), v_new[i], 0, 0) for i in n]
    return (o, s_new), tinv


def _chain_loads(tok_pairs, small_pairs):
    toks = [[pair[i // 4][:, (i % 4) * 128:(i % 4 + 1) * 128] for i in range(N_CHAIN)] for pair in tok_pairs]
    smalls = [[pair[i // 4][i % 4] for i in range(N_CHAIN)] for pair in small_pairs]
    return toks, smalls


def _delta_fwd_call(q, k, v, beta, gc, gl, ride=None):
    sp = _scan_specs(lambda n: n)

    def body(qf, qr, kf, kr, vf, vr, bf, br, gcrf, gcrr, glf, glr, of, orv, ssf, ssr, tsf, tsr, s_scr):
        @pl.when(pl.program_id(0) == 0)
        def _():
            s_scr[...] = jnp.zeros_like(s_scr)

        masks, eye = _chain_masks()
        (qs, ks, vs), _ = _chain_loads([(qf, qr), (kf, kr), (vf, vr)], [])
        bs = [(bf, br)[i // 4][i % 4, 0] for i in range(N_CHAIN)]
        gcrs = [(gcrf, gcrr)[i // 4][i % 4, 0] for i in range(N_CHAIN)]
        gls = [(glf, glr)[i // 4][i % 4, 0] for i in range(N_CHAIN)]
        ss = [s_scr[i] for i in range(N_CHAIN)]
        (o, s_new), tinv = _delta_chains(qs, ks, vs, bs, gcrs, gls, ss, masks, eye, None)
        for i in range(N_CHAIN):
            d, h = i // 4, i % 4
            (ssf, ssr)[d][h] = ss[i]
            (tsf, tsr)[d][h] = tinv[i]
            (of, orv)[d][:, h * 128:(h + 1) * 128] = o[i]
            s_scr[i] = s_new[i]

    return _scan_call(
        body, "delta_fwd",
        _both(sp, ["tok", "tok", "tok", "row", "row", "one"]),
        _both(sp, ["tok", "state", "tinv"]),
        [jax.ShapeDtypeStruct((ROWS, 512), F32)] * 2 + [jax.ShapeDtypeStruct((N_CHUNK, 4, 128, 128), F32)] * 2
        + [jax.ShapeDtypeStruct((N_CHUNK, 4, CHUNK, CHUNK), F32)] * 2,
        [q, q, k, k, v, v, *beta, *gc, *gl], ride)


def _delta_bwd_call(q, k, v, beta, gc, gl, ssave, tsave, do, ride=None):
    sp = _scan_specs(lambda n: N_CHUNK - 1 - n)

    def body(qf, qr, kf, kr, vf, vr, bf, br, gcrf, gcrr, glf, glr, ssf, ssr, tsf, tsr, dof, dor,
             dqf, dqr, dkf, dkr, dvf, dvr, dbf, dbr, dgcrf, dgcrr, dglf, dglr, ds_scr):
        @pl.when(pl.program_id(0) == 0)
        def _():
            ds_scr[...] = jnp.zeros_like(ds_scr)

        masks, eye = _chain_masks()
        (qs, ks, vs, dos), (ss, ts) = _chain_loads(
            [(qf, qr), (kf, kr), (vf, vr), (dof, dor)], [(ssf, ssr), (tsf, tsr)])
        bs = [(bf, br)[i // 4][i % 4, 0] for i in range(N_CHAIN)]
        gcrs = [(gcrf, gcrr)[i // 4][i % 4, 0] for i in range(N_CHAIN)]
        gls = [(glf, glr)[i // 4][i % 4, 0] for i in range(N_CHAIN)]
        fn = lambda *a: _delta_chains(*a, masks, eye, ts)
        _, vjp, _ = jax.vjp(fn, qs, ks, vs, bs, gcrs, gls, ss, has_aux=True)
        dq, dk, dv, db, dgcr, dgl, ds = vjp((dos, [ds_scr[i] for i in range(N_CHAIN)]))
        for i in range(N_CHAIN):
            d, h = i // 4, i % 4
            hs = slice(h * 128, (h + 1) * 128)
            (dqf, dqr)[d][:, hs] = dq[i]
            (dkf, dkr)[d][:, hs] = dk[i]
            (dvf, dvr)[d][:, hs] = dv[i]
            (dbf, dbr)[d][h, 0] = db[i]
            (dgcrf, dgcrr)[d][h, 0] = dgcr[i]
            (dglf, dglr)[d][h, 0] = dgl[i]
            ds_scr[i] = ds[i]

    tok = jax.ShapeDtypeStruct((ROWS, 512), F32)
    return _scan_call(
        body, "delta_bwd",
        _both(sp, ["tok", "tok", "tok", "row", "row", "one", "state", "tinv", "tok"]),
        _both(sp, ["tok", "tok", "tok", "row", "row", "one"]),
        [tok] * 6 + [jax.ShapeDtypeStruct((4, N_CHUNK, 1, CHUNK), F32)] * 4
        + [jax.ShapeDtypeStruct((4, N_CHUNK, 1, 1), F32)] * 2,
        [q, q, k, k, v, v, *beta, *gc, *gl, *ssave, *tsave, do, do], ride)


def _ret_chains(q, k, v, dm, qs, ks, cd, s):
    n = range(len(q))
    a = [_bdot(q[i], k[i], 1, 1) * dm[i] for i in n]
    o = [_bdot(a[i], v[i], 1, 0) + _bdot(q[i] * qs[i], s[i], 1, 0) for i in n]
    s_new = [s[i] * cd[i] + _bdot(k[i] * ks[i], v[i], 0, 0) for i in n]
    return o, s_new


def _ret_const_specs():
    return [pl.BlockSpec((N_CHAIN, CHUNK, CHUNK), lambda n: (0, 0, 0)), pl.BlockSpec((N_CHAIN, CHUNK, 1), lambda n: (0, 0, 0)),
            pl.BlockSpec((N_CHAIN, CHUNK, 1), lambda n: (0, 0, 0)), pl.BlockSpec((N_CHAIN, 1, 1), lambda n: (0, 0, 0))]


def _ret_fwd_call(q, k, v, v_col, dm, qs, ks, cd, ride=None):
    sp = _scan_specs(lambda n: n, v_col)

    def body(qf, qr, kf, kr, vf, vr, dm_ref, qs_ref, ks_ref, cd_ref, of, orv, ssf, ssr, s_scr):
        @pl.when(pl.program_id(0) == 0)
        def _():
            s_scr[...] = jnp.zeros_like(s_scr)

        (qc, kc, vc), _ = _chain_loads([(qf, qr), (kf, kr), (vf, vr)], [])
        ss = [s_scr[i] for i in range(N_CHAIN)]
        consts = [[r[i] for i in range(N_CHAIN)] for r in (dm_ref, qs_ref, ks_ref, cd_ref)]
        o, s_new = _ret_chains(qc, kc, vc, *consts, ss)
        for i in range(N_CHAIN):
            d, h = i // 4, i % 4
            (ssf, ssr)[d][h] = ss[i]
            (of, orv)[d][:, h * 128:(h + 1) * 128] = o[i]
            s_scr[i] = s_new[i]

    return _scan_call(
        body, "ret_fwd",
        _both(sp, ["tok", "tok", "tokv"]) + _ret_const_specs(),
        _both(sp, ["tok", "state"]),
        [jax.ShapeDtypeStruct((ROWS, 512), F32)] * 2 + [jax.ShapeDtypeStruct((N_CHUNK, 4, 128, 128), F32)] * 2,
        [q, q, k, k, v, v, dm, qs, ks, cd], ride)


def _ret_bwd_call(q, k, v, v_col, dm, qs, ks, cd, ssave, do, ride=None):
    sp = _scan_specs(lambda n: N_CHUNK - 1 - n, v_col)

    def body(qf, qr, kf, kr, vf, vr, dm_ref, qs_ref, ks_ref, cd_ref, ssf, ssr, dof, dor,
             dqf, dqr, dkf, dkr, dvf, dvr, ddm_ref, dqs_ref, dks_ref, dcd_ref, ds_scr):
        @pl.when(pl.program_id(0) == 0)
        def _():
            ds_scr[...] = jnp.zeros_like(ds_scr)
            ddm_ref[...] = jnp.zeros_like(ddm_ref)
            dqs_ref[...] = jnp.zeros_like(dqs_ref)
            dks_ref[...] = jnp.zeros_like(dks_ref)
            dcd_ref[...] = jnp.zeros_like(dcd_ref)

        (qc, kc, vc, dos), (ss,) = _chain_loads([(qf, qr), (kf, kr), (vf, vr), (dof, dor)], [(ssf, ssr)])
        consts = [[r[i] for i in range(N_CHAIN)] for r in (dm_ref, qs_ref, ks_ref, cd_ref)]
        _, vjp = jax.vjp(_ret_chains, qc, kc, vc, *consts, ss)
        dq, dk, dv, ddm, dqs, dks, dcd, ds = vjp((dos, [ds_scr[i] for i in range(N_CHAIN)]))
        for i in range(N_CHAIN):
            d, h = i // 4, i % 4
            hs = slice(h * 128, (h + 1) * 128)
            (dqf, dqr)[d][:, hs] = dq[i]
            (dkf, dkr)[d][:, hs] = dk[i]
            (dvf, dvr)[d][:, hs] = dv[i]
            ddm_ref[i] += ddm[i]
            dqs_ref[i] += dqs[i]
            dks_ref[i] += dks[i]
            dcd_ref[i] += dcd[i]
            ds_scr[i] = ds[i]

    tok = jax.ShapeDtypeStruct((ROWS, 512), F32)
    return _scan_call(
        body, "ret_bwd",
        _both(sp, ["tok", "tok", "tokv"]) + _ret_const_specs() + _both(sp, ["state", "tok"]),
        _both(sp, ["tok", "tok", "tok"]) + _ret_const_specs(),
        [tok] * 6 + [jax.ShapeDtypeStruct((N_CHAIN, CHUNK, CHUNK), F32), jax.ShapeDtypeStruct((N_CHAIN, CHUNK, 1), F32),
                     jax.ShapeDtypeStruct((N_CHAIN, CHUNK, 1), F32), jax.ShapeDtypeStruct((N_CHAIN, 1, 1), F32)],
        [q, q, k, k, v, v, dm, qs, ks, cd, *ssave, do, do], ride)


N_QBLK = ROWS // B_BLOCK
CTX_QBLK = CTX_LEN // B_BLOCK


def _attn_heads(q, kc, vc, kw, vw, sink, valid):
    n = range(len(q))
    scale = B_HD ** -0.5
    s_c = [_bdot(q[i], kc[i], 1, 1) * scale for i in n]
    s_w = [jnp.where(valid, _bdot(q[i], kw[i], 1, 1) * scale, NEG) for i in n]
    m = [lax.stop_gradient(jnp.maximum(jnp.maximum(jnp.max(s_c[i], axis=-1, keepdims=True), sink[i]),
                                       jnp.max(s_w[i], axis=-1, keepdims=True))) for i in n]
    e_c = [jnp.exp(s_c[i] - m[i]) for i in n]
    e_w = [jnp.exp(s_w[i] - m[i]) for i in n]
    den = [jnp.sum(e_c[i], axis=-1, keepdims=True) + jnp.sum(e_w[i], axis=-1, keepdims=True)
           + jnp.exp(sink[i] - m[i]) for i in n]
    return [(_bdot(e_c[i], vc[i], 1, 0) + _bdot(e_w[i], vw[i], 1, 0)) / den[i] for i in n]


def _attn_loads(q_ref, kv_ref, sink_ref, start):
    q, kc, vc, kw, vw, sink = [], [], [], [], [], []
    for hk in range(B_KV_HEADS):
        ks = slice(hk * B_HD, (hk + 1) * B_HD)
        vs = slice(128 + hk * B_HD, 128 + (hk + 1) * B_HD)
        grp = (kv_ref[0:CTX_LEN, ks], kv_ref[0:CTX_LEN, vs],
               kv_ref[pl.ds(start, 3 * B_BLOCK), ks], kv_ref[pl.ds(start, 3 * B_BLOCK), vs])
        for g in range(4):
            h = hk * 4 + g
            q.append(q_ref[:, h * B_HD:(h + 1) * B_HD])
            for lst, val in zip((kc, vc, kw, vw), grp):
                lst.append(val)
            sink.append(jnp.full((1, 1), sink_ref[h], F32))
    return q, kc, vc, kw, vw, sink


def _window(blk):
    xblk = blk - CTX_QBLK
    first = jnp.clip((xblk - 1) * B_BLOCK, 0, SEQ - 3 * B_BLOCK)
    qpos = xblk * B_BLOCK + lax.broadcasted_iota(jnp.int32, (B_BLOCK, 3 * B_BLOCK), 0)
    kpos = first + lax.broadcasted_iota(jnp.int32, (B_BLOCK, 3 * B_BLOCK), 1)
    far = jnp.where(blk >= CTX_QBLK, 0, 2 * SEQ)
    valid = jnp.abs(kpos - qpos) + far <= WINDOW
    return pl.multiple_of(first + CTX_LEN, B_BLOCK), valid


def _attn_specs():
    qspec = pl.BlockSpec((B_BLOCK, 512), lambda i: (i, 0))
    kvspec = pl.BlockSpec((ROWS, 256), lambda i: (0, 0))
    return qspec, kvspec, pl.BlockSpec(memory_space=pltpu.SMEM)


def _attn_fwd_call(q, kv, sink):
    def body(q_ref, kv_ref, sink_ref, o_ref):
        start, valid = _window(pl.program_id(0))
        out = _attn_heads(*_attn_loads(q_ref, kv_ref, sink_ref, start), valid)
        for h in range(B_Q_HEADS):
            o_ref[:, h * B_HD:(h + 1) * B_HD] = out[h]

    qspec, kvspec, sspec = _attn_specs()
    return pl.pallas_call(
        body,
        grid=(N_QBLK,),
        in_specs=[qspec, kvspec, sspec],
        out_specs=qspec,
        out_shape=jax.ShapeDtypeStruct((ROWS, 512), F32),
        compiler_params=_cparams(("arbitrary",)),
        name="attn_fwd",
    )(q, kv, sink)


def _attn_bwd_call(q, kv, sink, do):
    def body(q_ref, kv_ref, sink_ref, do_ref, dq_ref, dkv_ref, dsink_ref):
        @pl.when(pl.program_id(0) == 0)
        def _():
            dkv_ref[...] = jnp.zeros_like(dkv_ref)
            dsink_ref[...] = jnp.zeros_like(dsink_ref)

        start, valid = _window(pl.program_id(0))
        _, vjp = jax.vjp(functools.partial(_attn_heads, valid=valid), *_attn_loads(q_ref, kv_ref, sink_ref, start))
        dq, dkc, dvc, dkw, dvw, dsink = vjp([do_ref[:, h * B_HD:(h + 1) * B_HD] for h in range(B_Q_HEADS)])
        for h in range(B_Q_HEADS):
            dq_ref[:, h * B_HD:(h + 1) * B_HD] = dq[h]
            dsink_ref[h:h + 1, :] += jnp.broadcast_to(dsink[h], (1, 128))
        for hk in range(B_KV_HEADS):
            ks = slice(hk * B_HD, (hk + 1) * B_HD)
            vs = slice(128 + hk * B_HD, 128 + (hk + 1) * B_HD)
            grp = lambda parts: parts[hk * 4] + parts[hk * 4 + 1] + parts[hk * 4 + 2] + parts[hk * 4 + 3]
            dkv_ref[0:CTX_LEN, ks] += grp(dkc)
            dkv_ref[0:CTX_LEN, vs] += grp(dvc)
            dkv_ref[pl.ds(start, 3 * B_BLOCK), ks] += grp(dkw)
            dkv_ref[pl.ds(start, 3 * B_BLOCK), vs] += grp(dvw)

    qspec, kvspec, sspec = _attn_specs()
    return pl.pallas_call(
        body,
        grid=(N_QBLK,),
        in_specs=[qspec, kvspec, sspec, qspec],
        out_specs=[qspec, kvspec, pl.BlockSpec((8, 128), lambda i: (0, 0))],
        out_shape=[jax.ShapeDtypeStruct((ROWS, 512), F32), jax.ShapeDtypeStruct((ROWS, 256), F32),
                   jax.ShapeDtypeStruct((8, 128), F32)],
        compiler_params=_cparams(("arbitrary",)),
        name="attn_bwd",
    )(q, kv, sink, do)


def _my_id():
    return 4 * lax.axis_index("x") + 2 * lax.axis_index("y") + lax.axis_index("c")


def _peer(k):
    x, y, c = lax.axis_index("x"), lax.axis_index("y"), lax.axis_index("c")
    return (1 - x if k & 4 else x, 1 - y if k & 2 else y, 1 - c if k & 1 else c)


def _exchange_copies(ins, outs, sems, gather):
    send_sems, recv_sems, local_sems = sems
    me = _my_id()
    own, remote = [], []
    for a in range(len(ins)):
        own.append(pltpu.make_async_copy(ins[a] if gather else ins[a].at[me], outs[a].at[me], local_sems.at[a]))
        for k in range(1, N_DEV):
            peer_slot = jnp.bitwise_xor(me, k)
            src = ins[a] if gather else ins[a].at[peer_slot]
            common = dict(src_ref=src, send_sem=send_sems.at[a, k - 1], recv_sem=recv_sems.at[a, k - 1],
                          device_id=_peer(k), device_id_type=MESH)
            remote.append((pltpu.make_async_remote_copy(dst_ref=outs[a].at[me], **common),
                           pltpu.make_async_remote_copy(dst_ref=outs[a].at[peer_slot], **common)))
    return own, remote


def _exchange_start(ins, outs, sems, gather):
    own, remote = _exchange_copies(ins, outs, sems, gather)
    for cp in own:
        cp.start()
    for send, _ in remote:
        send.start()


def _exchange_wait(ins, outs, sems, gather):
    own, remote = _exchange_copies(ins, outs, sems, gather)
    for _, arrival in remote:
        arrival.wait_recv()
    for send, _ in remote:
        send.wait_send()
    for cp in own:
        cp.wait()


def _exchange_plumbing(arrays, gather):
    n = len(arrays)
    hbm = [pl.BlockSpec(memory_space=pltpu.HBM)] * n
    out_shape = [jax.ShapeDtypeStruct((N_DEV,) + (a.shape if gather else a.shape[1:]), a.dtype) for a in arrays]
    sems = [pltpu.SemaphoreType.DMA((n, N_DEV - 1)), pltpu.SemaphoreType.DMA((n, N_DEV - 1)),
            pltpu.SemaphoreType.DMA((n,))]
    return hbm, out_shape, sems


def _exchange(arrays, gather, name):
    n = len(arrays)

    def body(*refs):
        ins, outs, sems = refs[:n], refs[n:2 * n], refs[2 * n:]
        _exchange_start(ins, outs, sems, gather)
        _exchange_wait(ins, outs, sems, gather)

    hbm, out_shape, sems = _exchange_plumbing(arrays, gather)
    return pl.pallas_call(
        body,
        in_specs=hbm,
        out_specs=hbm,
        out_shape=out_shape,
        scratch_shapes=sems,
        compiler_params=pltpu.CompilerParams(has_side_effects=True),
        name=name,
    )(*arrays)


def _riding(body, n_in, n_out, n_scratch, ride, n_steps):
    if ride is None:
        return body, [], [], [], []
    arrays, gather = ride
    n = len(arrays)

    def wrapped(*refs):
        ins, rin = refs[:n_in], refs[n_in:n_in + n]
        outs = refs[n_in + n:n_in + n + n_out]
        rout = refs[n_in + n + n_out:n_in + 2 * n + n_out]
        scratch = refs[n_in + 2 * n + n_out:n_in + 2 * n + n_out + n_scratch]
        sems = refs[n_in + 2 * n + n_out + n_scratch:]

        @pl.when(pl.program_id(0) == 0)
        def _():
            _exchange_start(rin, rout, sems, gather)

        body(*ins, *outs, *scratch)

        @pl.when(pl.program_id(0) == n_steps - 1)
        def _():
            _exchange_wait(rin, rout, sems, gather)

    hbm, out_shape, sems = _exchange_plumbing(arrays, gather)
    return wrapped, hbm, hbm, out_shape, sems


def _sum_contributions(c_ref):
    g = c_ref[0].astype(F32)
    for j in range(1, N_DEV):
        g = g + c_ref[j].astype(F32)
    return g


def _adamw_update(g, w_ref, m_ref, v_ref, g_ref, d_ref, nm_ref, nv_ref):
    m_new = ADAM_B1 * m_ref[...] + (1.0 - ADAM_B1) * g
    v_new = ADAM_B2 * v_ref[...] + (1.0 - ADAM_B2) * (g * g)
    m_hat = m_new / (1.0 - ADAM_B1 ** ADAM_STEP)
    v_hat = v_new / (1.0 - ADAM_B2 ** ADAM_STEP)
    g_ref[...] = g
    d_ref[...] = -ADAM_LR * (m_hat / (jnp.sqrt(v_hat) + ADAM_EPS) + ADAM_WD * w_ref[...])
    nm_ref[...] = m_new
    nv_ref[...] = v_new


def _adamw_layers(w, m, v, contrib0, contrib1, name):
    _, r, c = w.shape
    br = _pick(r, (256, 128, 64, 32, 16, 8))
    nb = r // br

    def body(w_ref, m_ref, v_ref, c0_ref, c1_ref, g_ref, d_ref, nm_ref, nv_ref):
        g = jnp.where(pl.program_id(0) == 0, _sum_contributions(c0_ref), _sum_contributions(c1_ref))
        _adamw_update(g, w_ref, m_ref, v_ref, g_ref, d_ref, nm_ref, nv_ref)

    spec = pl.BlockSpec((None, br, c), lambda l, i: (l, i, 0))
    return pl.pallas_call(
        body,
        grid=(DEPTH, nb),
        in_specs=[spec, spec, spec,
                  pl.BlockSpec((N_DEV, br, c), lambda l, i: (0, jnp.where(l == 0, i, nb - 1), 0)),
                  pl.BlockSpec((N_DEV, br, c), lambda l, i: (0, jnp.where(l == 1, i, 0), 0))],
        out_specs=[spec] * 4,
        out_shape=[jax.ShapeDtypeStruct(w.shape, F32)] * 4,
        compiler_params=_cparams(("arbitrary", "arbitrary")),
        name=name,
    )(w, m, v, contrib0, contrib1)


def _adamw(w, m, v, contrib, name):
    r, c = w.shape
    br = _pick(r, (256, 128, 64, 32, 16, 8))

    def body(w_ref, m_ref, v_ref, c_ref, g_ref, d_ref, nm_ref, nv_ref):
        _adamw_update(_sum_contributions(c_ref), w_ref, m_ref, v_ref, g_ref, d_ref, nm_ref, nv_ref)

    spec = pl.BlockSpec((br, c), lambda i: (i, 0))
    cspec = pl.BlockSpec((N_DEV, br, c), lambda i: (0, i, 0))
    return pl.pallas_call(
        body,
        grid=(r // br,),
        in_specs=[spec, spec, spec, cspec],
        out_specs=[spec] * 4,
        out_shape=[jax.ShapeDtypeStruct((r, c), F32)] * 4,
        compiler_params=_cparams(("parallel",)),
        name=name,
    )(w, m, v, contrib)


def _silu(x):
    return x * jax.nn.sigmoid(x)


def _rope_angles(pos, n_freq):
    inv = ROPE_BASE ** (-jnp.arange(n_freq, dtype=F32) / n_freq)
    return pos[:, None] * inv[None, :]


def _with_ctx_rows(cos, sin):
    return (jnp.concatenate([jnp.ones((CTX_LEN, 128), F32), cos], axis=0),
            jnp.concatenate([jnp.zeros((CTX_LEN, 128), F32), sin], axis=0))


def _rope_tables():
    rows_n = SEQ // GRID_W
    rows = jnp.repeat(jnp.arange(rows_n, dtype=F32), GRID_W)
    cols = jnp.tile(jnp.arange(GRID_W, dtype=F32), rows_n)
    ang_r = _rope_angles(rows, B_HD // 4)
    ang_c = _rope_angles(cols, B_HD // 4)
    cos_b = jnp.tile(jnp.concatenate([jnp.cos(ang_r)] * 2 + [jnp.cos(ang_c)] * 2, axis=1), (1, 2))
    sin_b = jnp.tile(jnp.concatenate([-jnp.sin(ang_r), jnp.sin(ang_r), -jnp.sin(ang_c), jnp.sin(ang_c)], axis=1), (1, 2))
    ang = _rope_angles(jnp.arange(SEQ, dtype=F32), C_HD // 2)
    cos_c = jnp.concatenate([jnp.cos(ang)] * 2, axis=1)
    sin_c = jnp.concatenate([-jnp.sin(ang), jnp.sin(ang)], axis=1)
    return _with_ctx_rows(cos_b, sin_b), _with_ctx_rows(cos_c, sin_c)


def _halves(a):
    return a[:4], a[4:]


def _delta_gates(ab, a_log, dt_bias):
    beta = jax.nn.sigmoid(ab[:, :8])
    g = -jnp.exp(a_log)[None, :] * jax.nn.softplus(ab[:, 8:] + dt_bias[None, :])
    gch = g.reshape(N_CHUNK, CHUNK, 8)
    fwd = jnp.cumsum(gch[..., :4], axis=1)
    bwd = jnp.flip(jnp.cumsum(jnp.flip(gch[..., 4:], axis=1), axis=1), axis=1)
    gc = jnp.concatenate([fwd, bwd], axis=-1)
    gl = jnp.sum(gch, axis=1)
    rows = lambda a: _halves(a.transpose(2, 0, 1)[:, :, None, :])
    return rows(beta.reshape(N_CHUNK, CHUNK, 8)), rows(gc), _halves(gl.T[:, :, None, None])


def _ret_consts(c_decay):
    lg = jax.nn.log_sigmoid(c_decay)
    idx = jnp.arange(CHUNK, dtype=F32)
    diff = idx[:, None] - idx[None, :]
    lgf, lgb = lg[:4, None, None], lg[4:, None, None]
    dm = jnp.concatenate([jnp.exp(jnp.where(diff >= 0, diff * lgf, -jnp.inf)),
                          jnp.exp(jnp.where(diff <= 0, -diff * lgb, -jnp.inf))], axis=0)
    qs = jnp.concatenate([jnp.exp((idx + 1.0)[None, :] * lg[:4, None]),
                          jnp.exp((CHUNK - idx)[None, :] * lg[4:, None])], axis=0)[:, :, None]
    ks = jnp.concatenate([jnp.exp((CHUNK - 1.0 - idx)[None, :] * lg[:4, None]),
                          jnp.exp(idx[None, :] * lg[4:, None])], axis=0)[:, :, None]
    return dm, qs, ks, jnp.exp(CHUNK * lg)[:, None, None]


A_PIECES = ((0, True, A_DK ** -0.5, "a_q"), (1, True, 1.0, "a_k"), (2, False, 1.0, "a_v"))
B_ROPE_COLS = [C_BQ // 512, C_BKV // 256, 0, 0]
C_ROPE_COLS = [C_CQ // 512, C_CK // 512, 0, 0]
MERGE_COLS = [0, 0, 0, C_MERGE // 1024, C_MERGE // 1024 + 1, C_MERGE // 1024 + 2]


def _conv8(conv_w):
    return jnp.pad(conv_w, ((0, 8 - A_CONV), (0, 0)))


W_IN_TILES = {"nn": (2176, 512, 1024), "nt": (1088, 1024, 2176), "db": (1024, 512, ROWS)}


def _core_forward(h, w16, p, rides=None):
    proj = _matmul(h, w16, "w_in", "nn", W_IN_TILES["nn"])
    (cos_b, sin_b), (cos_c, sin_c) = _rope_tables()
    conv8 = _conv8(p["a_conv_w"])
    q, k, v = [_a_prep_fwd(proj, conv8, col, nrm, scl, nm) for col, nrm, scl, nm in A_PIECES]
    gates = _delta_gates(proj[:, C_AB:C_AB + 16], p["a_log"], p["a_dt_bias"])
    res = _delta_fwd_call(q, k, v, *gates, ride=rides and rides["delta"])
    (of, orv, ssf, ssr, tsf, tsr), ride_delta = res[:6], res[6:]
    (y_a,) = _a_out.fwd([(of, orv), proj], [p["a_norm_w"][None, :]], [0, C_AZ // 512])

    qb, kvb = _b_rope.fwd([proj, proj, cos_b, sin_b], [], B_ROPE_COLS)
    ob = _attn_fwd_call(qb, kvb, p["b_sink"])
    (y_b,) = _b_out.fwd([ob, proj], [], [0, C_BZ // 512])

    qc, kc = _c_rope.fwd([proj, proj, cos_c, sin_c], [], C_ROPE_COLS)
    res = _ret_fwd_call(qc, kc, proj, C_CV // 512, *_ret_consts(p["c_decay"]), ride=rides and rides["ret"])
    (cf, cr, csf, csr), ride_ret = res[:4], res[4:]
    (y_c,) = _c_out.fwd([(cf, cr), proj], [p["c_norm_w"][None, :]], [0, C_CZ // 512])

    wb = p["w_branch"].astype(BF16)
    (merged,) = _branch_merge.fwd([y_a, y_b, y_c, proj, proj, proj], [wb[0], wb[1], wb[2]], MERGE_COLS)
    saved = dict(proj=proj, q=q, k=k, v=v, of=of, orv=orv, ss=(ssf, ssr), ts=(tsf, tsr), qb=qb, kvb=kvb, ob=ob,
                 qc=qc, kc=kc, cf=cf, cr=cr, cs=(csf, csr), y=(y_a, y_b, y_c))
    return merged, saved, (ride_delta, ride_ret)


def _core_backward(h, w16, p, s, dmerged, rides=None):
    proj = s["proj"]
    (cos_b, sin_b), (cos_c, sin_c) = _rope_tables()
    conv8 = _conv8(p["a_conv_w"])
    wb = p["w_branch"].astype(BF16)
    y_a, y_b, y_c = s["y"]

    (*dy, dma, dmb, dmc), dwb = _branch_merge.bwd([y_a, y_b, y_c, proj, proj, proj], [wb[0], wb[1], wb[2]],
                                                   [dmerged], MERGE_COLS)
    dwb = jnp.stack(dwb)

    consts, consts_vjp = jax.vjp(_ret_consts, p["c_decay"])
    (do_c, dcz), (dcnw,) = _c_out.bwd([(s["cf"], s["cr"]), proj], [p["c_norm_w"][None, :]], [dy[2]], [0, C_CZ // 512])
    g = _ret_bwd_call(s["qc"], s["kc"], proj, C_CV // 512, *consts, s["cs"], do_c, ride=rides and rides["ret"])
    ride_ret = g[10:]
    (dcq, dck), _ = _c_rope.bwd([proj, proj, cos_c, sin_c], [], [(g[0], g[1]), (g[2], g[3])], C_ROPE_COLS)
    dcv = g[4] + g[5]
    (dc_decay,) = consts_vjp(tuple(g[6:10]))

    (dob, dbz), _ = _b_out.bwd([s["ob"], proj], [], [dy[1]], [0, C_BZ // 512])
    dqb, dkvb, dsink = _attn_bwd_call(s["qb"], s["kvb"], p["b_sink"], dob)
    (dbq, dbkv), _ = _b_rope.bwd([proj, proj, cos_b, sin_b], [], [dqb, dkvb], B_ROPE_COLS)

    ab = proj[:, C_AB:C_AB + 16]
    gates, gates_vjp = jax.vjp(_delta_gates, ab, p["a_log"], p["a_dt_bias"])
    (do_a, daz), (danw,) = _a_out.bwd([(s["of"], s["orv"]), proj], [p["a_norm_w"][None, :]], [dy[0]], [0, C_AZ // 512])
    g = _delta_bwd_call(s["q"], s["k"], s["v"], *gates, s["ss"], s["ts"], do_a, ride=rides and rides["delta"])
    ride_delta = g[12:]
    dgates = ((g[6], g[7]), (g[8], g[9]), (g[10], g[11]))
    dab, da_log, ddt = gates_vjp(dgates)
    dpre, dconv = [], []
    for (col, nrm, scl, nm), df, dr in zip(A_PIECES, (g[0], g[2], g[4]), (g[1], g[3], g[5])):
        dx, dw = _a_prep_bwd(proj, conv8, col, nrm, scl, df, dr, nm)
        dpre.append(dx)
        dconv.append(dw[:A_CONV])

    dproj = jnp.concatenate(dpre + [daz, dbq, dbz, dcq, dck, dcv, dcz, dma, dmb, dmc, dbkv,
                                    jnp.pad(dab, ((0, 0), (0, IN_PAD - C_AB - 16)))], axis=1).astype(BF16)
    dh = _matmul(dproj, w16, "w_in_da", "nt", W_IN_TILES["nt"])
    dw = _matmul(h.T.astype(BF16), dproj, "w_in_db", "nn", W_IN_TILES["db"])
    dp = dict(a_conv_w=jnp.concatenate(dconv, axis=1), a_log=da_log, a_dt_bias=ddt, a_norm_w=danw[0],
              b_sink=dsink[:, 0], c_decay=dc_decay, c_norm_w=dcnw[0], w_branch=dwb)
    return dh, dw, dp, (ride_delta, ride_ret)


CORE_PARAMS = ("a_conv_w", "a_log", "a_dt_bias", "a_norm_w", "b_sink", "c_decay", "c_norm_w", "w_branch")


def _pad_w_in(w):
    return jnp.concatenate([w[..., 0:2048], w[..., 2064:2576], w[..., 2832:3344], w[..., 3344:8464],
                            w[..., 2576:2832], w[..., 2048:2064],
                            jnp.zeros(w.shape[:-1] + (IN_PAD - IN_WIDTH,), w.dtype)], axis=-1)


def _unpad_w_in(g):
    return jnp.concatenate([g[..., 0:2048], g[..., C_AB:C_AB + 16], g[..., C_BQ:C_BQ + 512],
                            g[..., C_BKV:C_BKV + 256], g[..., C_BZ:C_BZ + 512], g[..., C_CQ:C_BKV]], axis=-1)


LAYER_SHARDED = ("w_ada", "w_in", "w_branch", "w_out")
RIDE_DELTA, RIDE_RET = ("w_in",), ("w_ada", "w_branch", "w_out")


def _unshard_layer(name, g):
    if name == "w_branch":
        return g.transpose(1, 2, 0, 3).reshape(3, BR_WIDTH, D_MODEL)
    if name == "w_out":
        return g.reshape(D_MODEL, D_MODEL)
    return g.transpose(1, 0, 2).reshape(D_MODEL, -1)


def _reshard_layer(name, w):
    if name == "w_branch":
        return w.reshape(3, BR_WIDTH, N_DEV, D_MODEL // N_DEV).transpose(2, 0, 1, 3)
    if name == "w_out":
        return w.reshape(N_DEV, D_MODEL // N_DEV, D_MODEL)
    return w.reshape(D_MODEL, N_DEV, -1).transpose(1, 0, 2)


def _layer_weights(gathered):
    out = {n: _unshard_layer(n, g) for n, g in gathered.items()}
    out["w_in16"] = _pad_w_in(out.pop("w_in"))
    return out


def _forward_backward(small, layer0, shards1, x, c, ctx, loss_target):
    c_ctx = small["c_ctx"]
    sc16 = jnp.zeros((16, D_MODEL), F32).at[0].set(_silu(c)).at[1].set(_silu(c_ctx))
    xs = jnp.concatenate([ctx, x], axis=0)
    weights = [layer0, None]
    layers = []
    for l in range(DEPTH):
        wl = weights[l]
        mod16 = _matmul(sc16, wl["w_ada"], "ada") + small["b_ada"][l][None, :]
        mod_cx = jnp.stack([mod16[1], mod16[0]])
        shift, scale, gate = jnp.split(mod_cx, 3, axis=1)
        nw = small["norm_w"][l][None, :]
        (h,) = _norm_mod.fwd([xs], [nw, shift, scale])
        p = {n: small[n][l] for n in CORE_PARAMS if n != "w_branch"}
        p["w_branch"] = wl["w_branch"]
        rides = None
        if l == 0:
            rides = dict(delta=([shards1[n] for n in RIDE_DELTA], True), ret=([shards1[n] for n in RIDE_RET], True))
        merged, saved, (ride_delta, ride_ret) = _core_forward(h, wl["w_in16"], p, rides)
        if l == 0:
            weights[1] = _layer_weights(dict(zip(RIDE_DELTA + RIDE_RET, list(ride_delta) + list(ride_ret))))
        (xs_next,) = _out_residual.fwd([xs, merged], [wl["w_out"], gate])
        layers.append(dict(xs=xs, h=h, p=p, saved=saved, merged=merged, gate=gate, nw=nw, shift=shift, scale=scale))
        xs = xs_next
    fw = small["final_norm_w"][None, :]
    xs = xs[CTX_LEN:]
    (per_row,) = _loss_rows.fwd([xs, loss_target], [fw])
    loss = jnp.sum(per_row[:, 0])

    d_per_row = jnp.zeros((SEQ, 128), F32).at[:, 0].set(1.0)
    (dxs,), (dfw,) = _loss_rows.bwd([xs, loss_target], [fw], [d_per_row])
    dxs = jnp.pad(dxs, ((CTX_LEN, 0), (0, 0)))
    small_names = tuple(n for n in CORE_PARAMS if n != "w_branch") + ("b_ada", "norm_w")
    dsmall = {n: [None] * DEPTH for n in small_names}
    dlayer = [None] * DEPTH
    contrib1 = None
    dsc16 = jnp.zeros((16, D_MODEL), F32)
    for l in reversed(range(DEPTH)):
        s, wl = layers[l], weights[l]
        (dres, dmerged), (dw_out, dgate) = _out_residual.bwd([s["xs"], s["merged"]], [wl["w_out"], s["gate"]], [dxs])
        rides = None
        if l == 0:
            blocks1 = {n: _reshard_layer(n, g).astype(BF16) for n, g in dlayer[1].items()}
            rides = dict(delta=([blocks1[n] for n in RIDE_DELTA], False), ret=([blocks1[n] for n in RIDE_RET], False))
        dh, dw_in, dp, (ride_delta, ride_ret) = _core_backward(s["h"], wl["w_in16"], s["p"], s["saved"], dmerged, rides)
        if l == 0:
            contrib1 = dict(zip(RIDE_DELTA + RIDE_RET, list(ride_delta) + list(ride_ret)))
        (dxn,), (dnw, dshift, dscale) = _norm_mod.bwd([s["xs"]], [s["nw"], s["shift"], s["scale"]], [dh])
        dxs = dres + dxn
        dmod_cx = jnp.concatenate([dshift, dscale, dgate], axis=1)
        dmod16 = jnp.zeros((16, 3 * D_MODEL), F32).at[0].set(dmod_cx[1]).at[1].set(dmod_cx[0])
        dsc16 = dsc16 + _matmul(dmod16, wl["w_ada"], "ada_da", "nt")
        dlayer[l] = dict(w_ada=_matmul(sc16, dmod16, "ada_db", "tn"), w_in=_unpad_w_in(dw_in),
                         w_branch=dp["w_branch"], w_out=dw_out)
        for n in small_names:
            if n in dp:
                dsmall[n][l] = dp[n]
        dsmall["norm_w"][l] = dnw[0]
        dsmall["b_ada"][l] = dmod_cx[0] + dmod_cx[1]
    gsmall = {n: jnp.stack(v) for n, v in dsmall.items()}
    gsmall["final_norm_w"] = dfw[0]
    sig = jax.nn.sigmoid(c_ctx)
    gsmall["c_ctx"] = dsc16[1] * sig * (1.0 + c_ctx * (1.0 - sig))
    return loss, dxs[CTX_LEN:], gsmall, dlayer[0], contrib1


SMALL = ("c_ctx", "b_ada", "norm_w", "a_log", "a_dt_bias", "a_norm_w", "b_sink", "c_decay", "c_norm_w",
         "final_norm_w")
WEIGHTS = ("c_ctx", "w_ada", "b_ada", "norm_w", "w_in", "a_conv_w", "a_log", "a_dt_bias", "a_norm_w", "b_sink",
           "c_decay", "c_norm_w", "w_branch", "w_out", "final_norm_w")
SMALL_PACK = 12288


def _unshard_conv(g):
    return g.transpose(1, 2, 0, 3).reshape(DEPTH, A_CONV, 3 * A_WIDTH)


def _reshard_conv(w):
    return w.reshape(DEPTH, A_CONV, N_DEV, 3 * A_WIDTH // N_DEV).transpose(2, 0, 1, 3)


def _pack_small(tree):
    flat = jnp.concatenate([tree[n].reshape(-1) for n in SMALL])
    return jnp.pad(flat, (0, SMALL_PACK - flat.shape[0])).reshape(SMALL_PACK // 128, 128)


def _unpack_small(packed, like):
    flat = packed.reshape(-1)
    out, off = {}, 0
    for n in SMALL:
        size = math.prod(like[n].shape)
        out[n] = flat[off:off + size].reshape(like[n].shape)
        off += size
    return out


def kernel(x, c, ctx, c_ctx, w_ada, b_ada, norm_w, w_in, a_conv_w, a_log, a_dt_bias, a_norm_w, b_sink, c_decay, c_norm_w, w_branch, w_out, final_norm_w, loss_target, m_c_ctx, m_w_ada, m_b_ada, m_norm_w, m_w_in, m_a_conv_w, m_a_log, m_a_dt_bias, m_a_norm_w, m_b_sink, m_c_decay, m_c_norm_w, m_w_branch, m_w_out, m_final_norm_w, v_c_ctx, v_w_ada, v_b_ada, v_norm_w, v_w_in, v_a_conv_w, v_a_log, v_a_dt_bias, v_a_norm_w, v_b_sink, v_c_decay, v_c_norm_w, v_w_branch, v_w_out, v_final_norm_w):
    w = dict(c_ctx=c_ctx, w_ada=w_ada, b_ada=b_ada, norm_w=norm_w, w_in=w_in, a_conv_w=a_conv_w, a_log=a_log,
             a_dt_bias=a_dt_bias, a_norm_w=a_norm_w, b_sink=b_sink, c_decay=c_decay, c_norm_w=c_norm_w,
             w_branch=w_branch, w_out=w_out, final_norm_w=final_norm_w)
    m = dict(c_ctx=m_c_ctx, w_ada=m_w_ada, b_ada=m_b_ada, norm_w=m_norm_w, w_in=m_w_in, a_conv_w=m_a_conv_w,
             a_log=m_a_log, a_dt_bias=m_a_dt_bias, a_norm_w=m_a_norm_w, b_sink=m_b_sink, c_decay=m_c_decay,
             c_norm_w=m_c_norm_w, w_branch=m_w_branch, w_out=m_w_out, final_norm_w=m_final_norm_w)
    v = dict(c_ctx=v_c_ctx, w_ada=v_w_ada, b_ada=v_b_ada, norm_w=v_norm_w, w_in=v_w_in, a_conv_w=v_a_conv_w,
             a_log=v_a_log, a_dt_bias=v_a_dt_bias, a_norm_w=v_a_norm_w, b_sink=v_b_sink, c_decay=v_c_decay,
             c_norm_w=v_c_norm_w, w_branch=v_w_branch, w_out=v_w_out, final_norm_w=v_final_norm_w)

    shards = {n: w[n].astype(BF16) for n in LAYER_SHARDED}
    first = _exchange([shards[n][0] for n in LAYER_SHARDED] + [w["a_conv_w"]], True, "gather_layer0")
    layer0 = _layer_weights(dict(zip(LAYER_SHARDED, first)))
    small_w = {n: w[n] for n in SMALL}
    small_w["a_conv_w"] = _unshard_conv(first[len(LAYER_SHARDED)])
    loss, gx, gw, glayer0, contrib1 = _forward_backward(small_w, layer0, {n: shards[n][1] for n in LAYER_SHARDED},
                                                        x[0], c[0], ctx[0], loss_target[0])
    loss = lax.psum(loss, ("x", "y", "c"))

    last = _exchange([_reshard_layer(n, glayer0[n]).astype(BF16) for n in LAYER_SHARDED]
                     + [_reshard_conv(gw["a_conv_w"])], False, "scatter_layer0")
    small = _exchange([_pack_small(gw)], True, "gather_small_grads")[0]

    grad, delta, new_m, new_v = {}, {}, {}, {}
    for n, contrib0 in zip(LAYER_SHARDED, last):
        shp = w[n].shape
        per_layer = (math.prod(shp[1:-1]), shp[-1])
        outs = _adamw_layers(*[a.reshape((DEPTH,) + per_layer) for a in (w[n], m[n], v[n])],
                             *[cb.reshape((N_DEV,) + per_layer) for cb in (contrib0, contrib1[n])], "adamw_" + n)
        grad[n], delta[n], new_m[n], new_v[n] = [o.reshape(shp) for o in outs]
    shp = a_conv_w.shape
    two_d = (math.prod(shp[:-1]), shp[-1])
    outs = _adamw(*[a.reshape(two_d) for a in (a_conv_w, m_a_conv_w, v_a_conv_w)],
                  last[len(LAYER_SHARDED)].reshape((N_DEV,) + two_d), "adamw_a_conv_w")
    grad["a_conv_w"], delta["a_conv_w"], new_m["a_conv_w"], new_v["a_conv_w"] = [o.reshape(shp) for o in outs]
    outs = _adamw(_pack_small(w), _pack_small(m), _pack_small(v), small, "adamw_small")
    for tree, packed in zip((grad, delta, new_m, new_v), outs):
        tree.update(_unpack_small(packed, w))

    return (loss, gx[None], *[grad[n] for n in WEIGHTS], *[delta[n] for n in WEIGHTS],
            *[new_m[n] for n in WEIGHTS], *[new_v[n] for n in WEIGHTS])
```

```python
import functools
import math

import jax
import jax.numpy as jnp
from jax import lax
from jax.experimental import pallas as pl
from jax.experimental.pallas import tpu as pltpu

F32 = jnp.float32
BF16 = jnp.bfloat16
HIGHEST = lax.Precision.HIGHEST

D_MODEL = 1024
SEQ = 4096
DEPTH = 2
GRID_W = 64
CTX_LEN = 256
EPS = 1e-6
ROPE_BASE = 10000.0
BR_WIDTH = D_MODEL // 2
A_DK = 128
A_HEADS = 4
A_WIDTH = 512
A_CONV = 5
B_HD = 64
B_Q_HEADS = 8
B_KV_HEADS = 2
WINDOW = 128
B_BLOCK = 128
C_HD = 128
C_HEADS = 4
C_WIDTH = 512
CHUNK = 64
ADAM_LR = 0.001
ADAM_B1 = 0.9
ADAM_B2 = 0.999
ADAM_EPS = 1e-08
ADAM_WD = 0.01
ADAM_STEP = 10

N_DEV = 8
ROWS = CTX_LEN + SEQ
N_CHUNK = ROWS // CHUNK
N_CTX_CHUNK = CTX_LEN // CHUNK
IN_WIDTH = 8464
IN_PAD = 8704
NEG = -1e30

VMEM_LIMIT = 48 * 1024 * 1024
MESH = pl.DeviceIdType.MESH

C_AQ, C_AK, C_AV, C_AZ, C_BQ, C_BZ, C_CQ, C_CK, C_CV, C_CZ = (i * 512 for i in range(10))
C_MERGE = 5120
C_BKV = 8192
C_AB = 8448


def _cparams(sem=None):
    if sem is None:
        return pltpu.CompilerParams(vmem_limit_bytes=VMEM_LIMIT)
    return pltpu.CompilerParams(dimension_semantics=sem, vmem_limit_bytes=VMEM_LIMIT)


def _dg(a, b, ca, cb, prec=None):
    return lax.dot_general(a, b, (((ca,), (cb,)), ((), ())), preferred_element_type=F32, precision=prec)


@functools.partial(jax.custom_vjp, nondiff_argnums=(2, 3))
def _bdot(a, b, ca, cb):
    return _dg(a.astype(BF16), b.astype(BF16), ca, cb)


def _bdot_fwd(a, b, ca, cb):
    return _bdot(a, b, ca, cb), (a, b)


def _bdot_bwd(ca, cb, res, ct):
    a, b = res
    da = _bdot(ct, b, 1, 1 - cb) if ca == 1 else _bdot(b, ct, 1 - cb, 1)
    db = _bdot(a, ct, 1 - ca, 0) if cb == 0 else _bdot(ct, a, 0, 1 - ca)
    return da, db


_bdot.defvjp(_bdot_fwd, _bdot_bwd)


def _hdot(a, b):
    return _dg(a, b, 1, 0, lax.Precision.HIGH)


def _k_silu(x):
    return x / (1.0 + jnp.exp(-x))


def _k_sigmoid(x):
    return 1.0 / (1.0 + jnp.exp(-x))


@jax.custom_vjp
def _swap64(x):
    return pltpu.roll(x, 64, 1)


_swap64.defvjp(lambda x: (pltpu.roll(x, 64, 1), None), lambda _, ct: (pltpu.roll(ct, 64, 1),))


def _swap16_impl(x):
    lane = lax.broadcasted_iota(jnp.int32, x.shape, 1)
    return jnp.where((lane & 16) == 0, pltpu.roll(x, 112, 1), pltpu.roll(x, 16, 1))


@jax.custom_vjp
def _swap16(x):
    return _swap16_impl(x)


_swap16.defvjp(lambda x: (_swap16_impl(x), None), lambda _, ct: (_swap16_impl(ct),))


def _pick(dim, prefs):
    for p in prefs:
        if dim % p == 0:
            return p
    return dim


def _matmul(a, b, name, mode="nn", tiles=None, ride=None):
    ca, cb = {"nn": (1, 0), "nt": (1, 1), "tn": (0, 0)}[mode]
    m, k = a.shape[1 - ca], a.shape[ca]
    n = b.shape[1 - cb]
    if tiles is None:
        tiles = (_pick(m, (1088, 1024, 512, 256, 128)), _pick(n, (512, 256, 128)),
                 _pick(k, (1088, 1024, 512, 256, 128) if mode == "tn" else (2176, 2048, 1024, 512, 256, 128)))
    tm, tn, tk = tiles
    nk = k // tk
    a_spec = (pl.BlockSpec((tm, tk), lambda i, j, kk: (i, kk)) if ca == 1
              else pl.BlockSpec((tk, tm), lambda i, j, kk: (kk, i)))
    b_spec = (pl.BlockSpec((tk, tn), lambda i, j, kk: (kk, j)) if cb == 0
              else pl.BlockSpec((tn, tk), lambda i, j, kk: (j, kk)))

    def body(a_ref, b_ref, o_ref):
        part = _dg(a_ref[...].astype(BF16), b_ref[...].astype(BF16), ca, cb)
        if nk == 1:
            o_ref[...] = part
        else:
            kk = pl.program_id(2)

            @pl.when(kk == 0)
            def _():
                o_ref[...] = part

            @pl.when(kk > 0)
            def _():
                o_ref[...] += part

    grid = (m // tm, n // tn, nk)
    if ride is None:
        return pl.pallas_call(
            body,
            grid=grid,
            in_specs=[a_spec, b_spec],
            out_specs=pl.BlockSpec((tm, tn), lambda i, j, kk: (i, j)),
            out_shape=jax.ShapeDtypeStruct((m, n), F32),
            compiler_params=_cparams(("parallel", "parallel", "arbitrary")),
            name=name,
        )(a, b)
    body, r_in, r_out, r_shape, r_scratch = _riding(body, 2, 1, 0, ride, grid)
    return pl.pallas_call(
        body,
        grid=grid,
        in_specs=[a_spec, b_spec] + r_in,
        out_specs=[pl.BlockSpec((tm, tn), lambda i, j, kk: (i, j))] + r_out,
        out_shape=[jax.ShapeDtypeStruct((m, n), F32)] + r_shape,
        scratch_shapes=r_scratch,
        compiler_params=_cparams(("arbitrary", "arbitrary", "arbitrary")),
        name=name,
    )(a, b, *ride[0])


ROW_BLOCK = 256
ROW_VMEM_BUDGET = 16 * 1024 * 1024


def _pieces(val, pw):
    return [val[:, j * pw:(j + 1) * pw] for j in range(val.shape[1] // pw)]


def _flat(groups):
    arrays, sizes = [], []
    for g in groups:
        g = g if isinstance(g, (tuple, list)) else (g,)
        arrays += list(g)
        sizes.append(len(g))
    return arrays, sizes


def _regroup(refs, sizes):
    out, at = [], 0
    for n in sizes:
        val = refs[at][...]
        for r in refs[at + 1:at + n]:
            val = val + r[...]
        out.append(val)
        at += n
    return out


class _Rowwise:
    def __init__(self, fn, name, row_wpw, par_pw, out_wpw, n_diff=None):
        self.fn, self.name, self.row_wpw, self.par_pw, self.out_wpw = fn, name, row_wpw, par_pw, out_wpw
        self.n_diff = len(row_wpw) if n_diff is None else n_diff

        @jax.custom_vjp
        def call(rows, params):
            return self.fwd(rows, params)

        def call_fwd(rows, params):
            return self.fwd(rows, params), (rows, params)

        def call_bwd(res, douts):
            return self.bwd(res[0], res[1], douts)

        call.defvjp(call_fwd, call_bwd)
        self.call = call

    def _load(self, row_vals, par_refs, br, with_ctx):
        row = pl.program_id(0) * br + lax.broadcasted_iota(jnp.int32, (br, 1), 0)
        is_ctx = (row < (CTX_LEN if with_ctx else 0)).astype(F32)
        rows = [_pieces(v, pw) for v, (_, pw) in zip(row_vals, self.row_wpw)]
        pars = []
        for p, pw in zip(par_refs, self.par_pw):
            val = p[...].astype(F32)
            if p.shape[0] == 2:
                val = is_ctx * val[0:1, :] + (1.0 - is_ctx) * val[1:2, :]
            pars.append(_pieces(val, pw))
        return rows, pars, is_ctx

    def _block_rows(self, n_rows, widths):
        for br in (1088, 1024, 544, 512, 272):
            if n_rows % br == 0 and 2 * 4 * br * sum(widths) <= ROW_VMEM_BUDGET:
                return br
        return ROW_BLOCK

    def _row_specs(self, br, sizes, cols):
        out = []
        for (w, _), n, c in zip(self.row_wpw, sizes, cols):
            out += [pl.BlockSpec((br, w), lambda i, c=c: (i, c))] * n
        return out

    def fwd(self, rows, params, cols=None):
        arrays, sizes = _flat(rows)
        cols = cols or [0] * len(rows)
        n_rows = arrays[0].shape[0]
        n_in = len(arrays)
        br = self._block_rows(n_rows, [w for (w, _), n in zip(self.row_wpw, sizes) for _ in range(n)]
                              + [w for w, _ in self.out_wpw])

        def body(*refs):
            r, p, _ = self._load(_regroup(refs[:n_in], sizes), refs[n_in:n_in + len(params)], br, n_rows == ROWS)
            for o_ref, pieces, (_, pw) in zip(refs[n_in + len(params):], self.fn(r, p), self.out_wpw):
                for j, piece in enumerate(pieces):
                    o_ref[:, j * pw:(j + 1) * pw] = piece

        return pl.pallas_call(
            body,
            grid=(n_rows // br,),
            in_specs=self._row_specs(br, sizes, cols) + [pl.BlockSpec(p.shape, lambda i: (0, 0)) for p in params],
            out_specs=[pl.BlockSpec((br, w), lambda i: (i, 0)) for w, _ in self.out_wpw],
            out_shape=[jax.ShapeDtypeStruct((n_rows, w), F32) for w, _ in self.out_wpw],
            compiler_params=_cparams(("parallel",)),
            name=self.name + "_fwd",
        )(*arrays, *params)

    def bwd(self, rows, params, douts, cols=None, bf16_rows=()):
        arrays, sizes = _flat(rows)
        darrays, dsizes = _flat(douts)
        cols = cols or [0] * len(rows)
        n_rows = arrays[0].shape[0]
        n_in, n_par, n_dout, n_diff = len(arrays), len(params), len(darrays), self.n_diff
        br = self._block_rows(n_rows, [w for (w, _), n in zip(self.row_wpw, sizes) for _ in range(n)]
                              + [w for (w, _), n in zip(self.out_wpw, dsizes) for _ in range(n)]
                              + [w for w, _ in self.row_wpw[:n_diff]])

        def body(*refs):
            par_refs = refs[n_in:n_in + n_par]
            dout_refs = refs[n_in + n_par:n_in + n_par + n_dout]
            drow_refs = refs[n_in + n_par + n_dout:n_in + n_par + n_dout + n_diff]
            dpar_refs = refs[n_in + n_par + n_dout + n_diff:]

            @pl.when(pl.program_id(0) == 0)
            def _():
                for d in dpar_refs:
                    d[...] = jnp.zeros_like(d)

            r, p, is_ctx = self._load(_regroup(refs[:n_in], sizes), par_refs, br, n_rows == ROWS)
            cts = [_pieces(d, pw) for d, (_, pw) in zip(_regroup(dout_refs, dsizes), self.out_wpw)]
            fixed = r[n_diff:]
            _, vjp = jax.vjp(lambda rd, pp: self.fn(rd + fixed, pp), r[:n_diff], p)
            dr, dp = vjp(cts)
            for d_ref, pieces, (_, pw) in zip(drow_refs, dr, self.row_wpw):
                for j, piece in enumerate(pieces):
                    d_ref[:, j * pw:(j + 1) * pw] = piece.astype(d_ref.dtype)
            for d_ref, pieces, pw in zip(dpar_refs, dp, self.par_pw):
                for j, piece in enumerate(pieces):
                    lanes = slice(j * pw, (j + 1) * pw)
                    if d_ref.shape[0] != 2:
                        d_ref[:, lanes] += piece
                    else:
                        d_ref[0:1, lanes] += jnp.sum(is_ctx * piece, axis=0, keepdims=True)
                        d_ref[1:2, lanes] += jnp.sum((1.0 - is_ctx) * piece, axis=0, keepdims=True)

        par_specs = [pl.BlockSpec(p.shape, lambda i: (0, 0)) for p in params]
        dout_specs = []
        for (w, _), n in zip(self.out_wpw, dsizes):
            dout_specs += [pl.BlockSpec((br, w), lambda i: (i, 0))] * n
        drow_w = [w for w, _ in self.row_wpw[:n_diff]]
        g = pl.pallas_call(
            body,
            grid=(n_rows // br,),
            in_specs=self._row_specs(br, sizes, cols) + par_specs + dout_specs,
            out_specs=[pl.BlockSpec((br, w), lambda i: (i, 0)) for w in drow_w] + par_specs,
            out_shape=[jax.ShapeDtypeStruct((n_rows, w), BF16 if a in bf16_rows else F32) for a, w in enumerate(drow_w)]
            + [jax.ShapeDtypeStruct(p.shape, F32) for p in params],
            compiler_params=_cparams(("arbitrary",)),
            name=self.name + "_bwd",
        )(*arrays, *params, *darrays)
        return list(g[:n_diff]), list(g[n_diff:])


def _fn_norm_mod(rows, pars):
    (x,), (nw,), (shift,), (scale,) = rows[0], pars[0], pars[1], pars[2]
    y = x * lax.rsqrt(jnp.mean(x * x, axis=-1, keepdims=True) + EPS) * nw
    return [[y * (1.0 + scale) + shift]]


def _fn_head_rms_gate(rows, pars):
    (w,) = pars[0]
    return [[o * lax.rsqrt(jnp.mean(o * o, axis=-1, keepdims=True) + EPS) * w * _k_silu(z)
             for o, z in zip(rows[0], rows[1])]]


def _fn_group_norm_gate(rows, pars):
    out = []
    for o, z, w in zip(rows[0], rows[1], pars[0]):
        mu = jnp.mean(o, axis=-1, keepdims=True)
        var = jnp.mean(jnp.square(o - mu), axis=-1, keepdims=True)
        out.append((o - mu) * lax.rsqrt(var + EPS) * w * _k_silu(z))
    return [out]


def _fn_gate(rows, pars):
    return [[o * _k_silu(z) for o, z in zip(rows[0], rows[1])]]


def _fn_branch_merge(rows, pars):
    (ya,), (yb,), (yc,), (ma,), (mb,), (mc,) = rows
    (wa,), (wb,), (wc,) = pars
    return [[_k_sigmoid(ma) * _bdot(ya, wa, 1, 0) + _k_sigmoid(mb) * _bdot(yb, wb, 1, 0)
             + _k_sigmoid(mc) * _bdot(yc, wc, 1, 0)]]


def _fn_out_residual(rows, pars):
    (res,), (merged,), (w,), (gate,) = rows[0], rows[1], pars[0], pars[1]
    return [[res + gate * _bdot(merged, w, 1, 0)]]


def _fn_loss(rows, pars):
    (x,), (target,), (w,) = rows[0], rows[1], pars[0]
    y = x * lax.rsqrt(jnp.mean(x * x, axis=-1, keepdims=True) + EPS) * w
    per_row = 0.5 * jnp.mean(jnp.square(y - target), axis=-1, keepdims=True)
    return [[jnp.broadcast_to(per_row, (per_row.shape[0], 128))]]


def _fn_b_rope(rows, pars):
    q, (k, v), (cos,), (sin,) = rows
    rot = lambda x: x * cos + _swap16(x) * sin
    return [[rot(x) for x in q], [rot(k), v]]


def _fn_c_rope(rows, pars):
    q, k, (cos,), (sin,) = rows
    rot = lambda x: x * cos + _swap64(x) * sin
    return [[rot(x) for x in q], [rot(x) * (C_HD ** -0.5) for x in k]]


_norm_mod = _Rowwise(_fn_norm_mod, "norm_mod", [(D_MODEL, D_MODEL)], [D_MODEL] * 3, [(D_MODEL, D_MODEL)])
_out_residual = _Rowwise(_fn_out_residual, "out_residual", [(D_MODEL, D_MODEL)] * 2, [D_MODEL] * 2,
                         [(D_MODEL, D_MODEL)])
_loss_rows = _Rowwise(_fn_loss, "loss", [(D_MODEL, D_MODEL)] * 2, [D_MODEL], [(128, 128)], n_diff=1)
_a_out = _Rowwise(_fn_head_rms_gate, "a_out", [(512, 128)] * 2, [128], [(512, 128)])
_c_out = _Rowwise(_fn_group_norm_gate, "c_out", [(512, 128)] * 2, [128], [(512, 128)])
_b_out = _Rowwise(_fn_gate, "b_out", [(512, 512)] * 2, [], [(512, 512)])
_branch_merge = _Rowwise(_fn_branch_merge, "branch_merge", [(512, 512)] * 3 + [(D_MODEL, D_MODEL)] * 3,
                         [D_MODEL] * 3, [(D_MODEL, D_MODEL)])
_b_rope = _Rowwise(_fn_b_rope, "b_rope", [(512, 128), (256, 128), (128, 128), (128, 128)], [],
                   [(512, 128), (256, 128)], n_diff=2)
_c_rope = _Rowwise(_fn_c_rope, "c_rope", [(512, 128), (512, 128), (128, 128), (128, 128)], [],
                   [(512, 128), (512, 128)], n_diff=2)


HALO = 8
EXT = ROW_BLOCK + 2 * HALO


def _halo_specs(col, width=512):
    last = ROWS // HALO - 1
    per = ROW_BLOCK // HALO
    prev = pl.BlockSpec((HALO, width), lambda i: (jnp.maximum(i * per - 1, 0), col))
    cur = pl.BlockSpec((ROW_BLOCK, width), lambda i: (i, col))
    nxt = pl.BlockSpec((HALO, width), lambda i: (jnp.minimum((i + 1) * per, last), col))
    return [prev, cur, nxt]


def _extended(prev_ref, cur_ref, next_ref):
    i = pl.program_id(0)
    prev_ok = i >= 2
    next_ok = jnp.logical_and(i >= 1, i < ROWS // ROW_BLOCK - 1)
    return jnp.concatenate([jnp.where(prev_ok, prev_ref[...], 0.0), cur_ref[...],
                            jnp.where(next_ok, next_ref[...], 0.0)], axis=0)


def _conv_taps(x_ext, w_ref, flip):
    acc = None
    for j in range(A_CONV):
        shift = (j - 2) if flip else (2 - j)
        term = w_ref[j:j + 1, :] * pltpu.roll(x_ext, shift % EXT, 0)
        acc = term if acc is None else acc + term
    return acc


def _conv_post(pre_pieces, normalize, scale):
    out = []
    for p in pre_pieces:
        y = _k_silu(p)
        if normalize:
            y = y * lax.rsqrt(jnp.sum(y * y, axis=-1, keepdims=True) + EPS) * scale
        out.append(y)
    return out


def _a_prep_fwd(proj, conv8, col, normalize, scale, name):
    def body(prev_ref, cur_ref, next_ref, w_ref, o_ref):
        pre = _conv_taps(_extended(prev_ref, cur_ref, next_ref), w_ref, False)[HALO:HALO + ROW_BLOCK]
        for h, y in enumerate(_conv_post(_pieces(pre, 128), normalize, scale)):
            o_ref[:, h * 128:(h + 1) * 128] = y

    return pl.pallas_call(
        body,
        grid=(ROWS // ROW_BLOCK,),
        in_specs=_halo_specs(col) + [pl.BlockSpec((8, 512), lambda i: (0, col))],
        out_specs=pl.BlockSpec((ROW_BLOCK, 512), lambda i: (i, 0)),
        out_shape=jax.ShapeDtypeStruct((ROWS, 512), F32),
        compiler_params=_cparams(("parallel",)),
        name=name + "_fwd",
    )(proj, proj, proj, conv8)


def _a_prep_bwd(proj, conv8, col, normalize, scale, dout_f, dout_r, name):
    def body(xp, xc, xn, w_ref, fp, fc, fn_, rp, rc, rn, dx_ref, dw_ref):
        @pl.when(pl.program_id(0) == 0)
        def _():
            dw_ref[...] = jnp.zeros_like(dw_ref)

        x_ext = _extended(xp, xc, xn)
        dout = _extended(fp, fc, fn_) + _extended(rp, rc, rn)
        pre = _conv_taps(x_ext, w_ref, False)
        _, vjp = jax.vjp(lambda p: _conv_post(p, normalize, scale), _pieces(pre, 128))
        (dpre,) = vjp(_pieces(dout, 128))
        dpre = jnp.concatenate(dpre, axis=1)
        dx_ref[...] = _conv_taps(dpre, w_ref, True)[HALO:HALO + ROW_BLOCK].astype(BF16)
        own = dpre[HALO:HALO + ROW_BLOCK]
        for j in range(A_CONV):
            shifted = pltpu.roll(x_ext, (2 - j) % EXT, 0)[HALO:HALO + ROW_BLOCK]
            dw_ref[j:j + 1, :] += jnp.sum(own * shifted, axis=0, keepdims=True)

    return pl.pallas_call(
        body,
        grid=(ROWS // ROW_BLOCK,),
        in_specs=_halo_specs(col) + [pl.BlockSpec((8, 512), lambda i: (0, col))] + _halo_specs(0) + _halo_specs(0),
        out_specs=[pl.BlockSpec((ROW_BLOCK, 512), lambda i: (i, 0)), pl.BlockSpec((8, 512), lambda i: (0, 0))],
        out_shape=[jax.ShapeDtypeStruct((ROWS, 512), BF16), jax.ShapeDtypeStruct((8, 512), F32)],
        compiler_params=_cparams(("arbitrary",)),
        name=name + "_bwd",
    )(proj, proj, proj, conv8, dout_f, dout_f, dout_f, dout_r, dout_r, dout_r)


N_CHAIN = 8


def _rev_chunk(s):
    return jnp.where(s < N_CTX_CHUNK, N_CTX_CHUNK - 1 - s, N_CHUNK + N_CTX_CHUNK - 1 - s)


def _scan_specs(step_of, v_col=0):
    cf = step_of
    cr = lambda n: _rev_chunk(step_of(n))

    def pair(shape, index):
        return (pl.BlockSpec(shape, lambda n: index(cf(n))), pl.BlockSpec(shape, lambda n: index(cr(n))))

    return dict(
        tok=pair((CHUNK, 512), lambda c: (c, 0)),
        tokv=pair((CHUNK, 512), lambda c: (c, v_col)),
        col=pair((4, CHUNK, 1), lambda c: (0, c, 0)),
        row=pair((4, 1, 1, CHUNK), lambda c: (0, c, 0, 0)),
        one=pair((4, 1, 1, 1), lambda c: (0, c, 0, 0)),
        state=pair((None, 4, 128, 128), lambda c: (c, 0, 0, 0)),
        tinv=pair((None, 4, CHUNK, CHUNK), lambda c: (c, 0, 0, 0)),
    )


def _both(specs, kinds):
    out = []
    for kind in kinds:
        out += list(specs[kind])
    return out


def _scan_call(body, name, in_specs, out_specs, out_shape, operands, ride):
    body, r_in, r_out, r_shape, r_scratch = _riding(body, len(in_specs), len(out_specs), 1, ride, (N_CHUNK,))
    return pl.pallas_call(
        body,
        grid=(N_CHUNK,),
        in_specs=in_specs + r_in,
        out_specs=out_specs + r_out,
        out_shape=out_shape + r_shape,
        scratch_shapes=[pltpu.VMEM((N_CHAIN, 128, 128), F32)] + r_scratch,
        compiler_params=_cparams(("arbitrary",)),
        name=name,
    )(*operands, *(ride[0] if ride else []))


def _chain_masks():
    ii = lax.broadcasted_iota(jnp.int32, (CHUNK, CHUNK), 0)
    jj = lax.broadcasted_iota(jnp.int32, (CHUNK, CHUNK), 1)
    eye = jnp.where(ii == jj, 1.0, 0.0).astype(F32)
    lower = (ii >= jj, ii > jj)
    upper = (ii <= jj, ii < jj)
    return [lower] * 4 + [upper] * 4, eye


def _tri_inv_all(ls, eye):
    xs = [eye - l for l in ls]
    ps = [_hdot(l, l) for l in ls]
    for i in range(5):
        xs = [x + _hdot(x, p) for x, p in zip(xs, ps)]
        if i < 4:
            ps = [_hdot(p, p) for p in ps]
    return xs


@jax.custom_vjp
def _inv_saved(l, x):
    return x


def _inv_saved_fwd(l, x):
    return x, x


def _inv_saved_bwd(x, dx):
    return -_bdot(x, _bdot(dx, x, 1, 1), 0, 0), jnp.zeros_like(x)


_inv_saved.defvjp(_inv_saved_fwd, _inv_saved_bwd)


def _delta_chains(q, k, v, beta_r, gcr, gl, s, masks, eye, tinv_saved):
    n = range(len(q))
    beta = [jnp.sum(eye * beta_r[i], axis=1, keepdims=True) for i in n]
    gcc = [jnp.sum(eye * gcr[i], axis=1, keepdims=True) for i in n]
    decay = [jnp.exp(jnp.where(masks[i][0], gcc[i] - gcr[i], NEG)) for i in n]
    kb = [k[i] * beta[i] for i in n]
    lmat = [jnp.where(masks[i][1], _bdot(kb[i], k[i], 1, 1) * decay[i], 0.0) for i in n]
    if tinv_saved is None:
        tinv = _tri_inv_all(lmat, eye)
    else:
        tinv = [_inv_saved(lmat[i], tinv_saved[i]) for i in n]
    eg = [jnp.exp(gcc[i]) for i in n]
    u = [_bdot(tinv[i], v[i] * beta[i], 1, 0) for i in n]
    w = [_bdot(tinv[i], kb[i] * eg[i], 1, 0) for i in n]
    qk = [_bdot(q[i], k[i], 1, 1) * decay[i] for i in n]
    v_new = [u[i] - _bdot(w[i], s[i], 1, 0) for i in n]
    o = [_bdot(q[i] * eg[i], s[i], 1, 0) + _bdot(qk[i], v_new[i], 1, 0) for i in n]
    s_new = [s[i] * jnp.exp(gl[i]) + _bdot(k[i] * jnp.exp(gl[i] - gcc[i]), v_new[i], 0, 0) for i in n]
    return (o, s_new), tinv


def _chain_loads(tok_pairs, small_pairs):
    toks = [[pair[i // 4][:, (i % 4) * 128:(i % 4 + 1) * 128] for i in range(N_CHAIN)] for pair in tok_pairs]
    smalls = [[pair[i // 4][i % 4] for i in range(N_CHAIN)] for pair in small_pairs]
    return toks, smalls


def _delta_fwd_call(q, k, v, beta, gc, gl, ride=None):
    sp = _scan_specs(lambda n: n)

    def body(qf, qr, kf, kr, vf, vr, bf, br, gcrf, gcrr, glf, glr, of, orv, ssf, ssr, tsf, tsr, s_scr):
        @pl.when(pl.program_id(0) == 0)
        def _():
            s_scr[...] = jnp.zeros_like(s_scr)

        masks, eye = _chain_masks()
        (qs, ks, vs), _ = _chain_loads([(qf, qr), (kf, kr), (vf, vr)], [])
        bs = [(bf, br)[i // 4][i % 4, 0] for i in range(N_CHAIN)]
        gcrs = [(gcrf, gcrr)[i // 4][i % 4, 0] for i in range(N_CHAIN)]
        gls = [(glf, glr)[i // 4][i % 4, 0] for i in range(N_CHAIN)]
        ss = [s_scr[i] for i in range(N_CHAIN)]
        (o, s_new), tinv = _delta_chains(qs, ks, vs, bs, gcrs, gls, ss, masks, eye, None)
        for i in range(N_CHAIN):
            d, h = i // 4, i % 4
            (ssf, ssr)[d][h] = ss[i]
            (tsf, tsr)[d][h] = tinv[i]
            (of, orv)[d][:, h * 128:(h + 1) * 128] = o[i]
            s_scr[i] = s_new[i]

    return _scan_call(
        body, "delta_fwd",
        _both(sp, ["tok", "tok", "tok", "row", "row", "one"]),
        _both(sp, ["tok", "state", "tinv"]),
        [jax.ShapeDtypeStruct((ROWS, 512), F32)] * 2 + [jax.ShapeDtypeStruct((N_CHUNK, 4, 128, 128), F32)] * 2
        + [jax.ShapeDtypeStruct((N_CHUNK, 4, CHUNK, CHUNK), F32)] * 2,
        [q, q, k, k, v, v, *beta, *gc, *gl], ride)


def _delta_bwd_call(q, k, v, beta, gc, gl, ssave, tsave, do, ride=None):
    sp = _scan_specs(lambda n: N_CHUNK - 1 - n)

    def body(qf, qr, kf, kr, vf, vr, bf, br, gcrf, gcrr, glf, glr, ssf, ssr, tsf, tsr, dof, dor,
             dqf, dqr, dkf, dkr, dvf, dvr, dbf, dbr, dgcrf, dgcrr, dglf, dglr, ds_scr):
        @pl.when(pl.program_id(0) == 0)
        def _():
            ds_scr[...] = jnp.zeros_like(ds_scr)

        masks, eye = _chain_masks()
        (qs, ks, vs, dos), (ss, ts) = _chain_loads(
            [(qf, qr), (kf, kr), (vf, vr), (dof, dor)], [(ssf, ssr), (tsf, tsr)])
        bs = [(bf, br)[i // 4][i % 4, 0] for i in range(N_CHAIN)]
        gcrs = [(gcrf, gcrr)[i // 4][i % 4, 0] for i in range(N_CHAIN)]
        gls = [(glf, glr)[i // 4][i % 4, 0] for i in range(N_CHAIN)]
        fn = lambda *a: _delta_chains(*a, masks, eye, ts)
        _, vjp, _ = jax.vjp(fn, qs, ks, vs, bs, gcrs, gls, ss, has_aux=True)
        dq, dk, dv, db, dgcr, dgl, ds = vjp((dos, [ds_scr[i] for i in range(N_CHAIN)]))
        for i in range(N_CHAIN):
            d, h = i // 4, i % 4
            hs = slice(h * 128, (h + 1) * 128)
            (dqf, dqr)[d][:, hs] = dq[i]
            (dkf, dkr)[d][:, hs] = dk[i]
            (dvf, dvr)[d][:, hs] = dv[i]
            (dbf, dbr)[d][h, 0] = db[i]
            (dgcrf, dgcrr)[d][h, 0] = dgcr[i]
            (dglf, dglr)[d][h, 0] = dgl[i]
            ds_scr[i] = ds[i]

    tok = jax.ShapeDtypeStruct((ROWS, 512), F32)
    return _scan_call(
        body, "delta_bwd",
        _both(sp, ["tok", "tok", "tok", "row", "row", "one", "state", "tinv", "tok"]),
        _both(sp, ["tok", "tok", "tok", "row", "row", "one"]),
        [tok] * 6 + [jax.ShapeDtypeStruct((4, N_CHUNK, 1, CHUNK), F32)] * 4
        + [jax.ShapeDtypeStruct((4, N_CHUNK, 1, 1), F32)] * 2,
        [q, q, k, k, v, v, *beta, *gc, *gl, *ssave, *tsave, do, do], ride)


def _ret_chains(q, k, v, dm, qs, ks, cd, s):
    n = range(len(q))
    a = [_bdot(q[i], k[i], 1, 1) * dm[i] for i in n]
    o = [_bdot(a[i], v[i], 1, 0) + _bdot(q[i] * qs[i], s[i], 1, 0) for i in n]
    s_new = [s[i] * cd[i] + _bdot(k[i] * ks[i], v[i], 0, 0) for i in n]
    return o, s_new


def _ret_const_specs():
    return [pl.BlockSpec((N_CHAIN, CHUNK, CHUNK), lambda n: (0, 0, 0)), pl.BlockSpec((N_CHAIN, CHUNK, 1), lambda n: (0, 0, 0)),
            pl.BlockSpec((N_CHAIN, CHUNK, 1), lambda n: (0, 0, 0)), pl.BlockSpec((N_CHAIN, 1, 1), lambda n: (0, 0, 0))]


def _ret_fwd_call(q, k, v, v_col, dm, qs, ks, cd, ride=None):
    sp = _scan_specs(lambda n: n, v_col)

    def body(qf, qr, kf, kr, vf, vr, dm_ref, qs_ref, ks_ref, cd_ref, of, orv, ssf, ssr, s_scr):
        @pl.when(pl.program_id(0) == 0)
        def _():
            s_scr[...] = jnp.zeros_like(s_scr)

        (qc, kc, vc), _ = _chain_loads([(qf, qr), (kf, kr), (vf, vr)], [])
        ss = [s_scr[i] for i in range(N_CHAIN)]
        consts = [[r[i] for i in range(N_CHAIN)] for r in (dm_ref, qs_ref, ks_ref, cd_ref)]
        o, s_new = _ret_chains(qc, kc, vc, *consts, ss)
        for i in range(N_CHAIN):
            d, h = i // 4, i % 4
            (ssf, ssr)[d][h] = ss[i]
            (of, orv)[d][:, h * 128:(h + 1) * 128] = o[i]
            s_scr[i] = s_new[i]

    return _scan_call(
        body, "ret_fwd",
        _both(sp, ["tok", "tok", "tokv"]) + _ret_const_specs(),
        _both(sp, ["tok", "state"]),
        [jax.ShapeDtypeStruct((ROWS, 512), F32)] * 2 + [jax.ShapeDtypeStruct((N_CHUNK, 4, 128, 128), F32)] * 2,
        [q, q, k, k, v, v, dm, qs, ks, cd], ride)


def _ret_bwd_call(q, k, v, v_col, dm, qs, ks, cd, ssave, do, ride=None):
    sp = _scan_specs(lambda n: N_CHUNK - 1 - n, v_col)

    def body(qf, qr, kf, kr, vf, vr, dm_ref, qs_ref, ks_ref, cd_ref, ssf, ssr, dof, dor,
             dqf, dqr, dkf, dkr, dvf, dvr, ddm_ref, dqs_ref, dks_ref, dcd_ref, ds_scr):
        @pl.when(pl.program_id(0) == 0)
        def _():
            ds_scr[...] = jnp.zeros_like(ds_scr)
            ddm_ref[...] = jnp.zeros_like(ddm_ref)
            dqs_ref[...] = jnp.zeros_like(dqs_ref)
            dks_ref[...] = jnp.zeros_like(dks_ref)
            dcd_ref[...] = jnp.zeros_like(dcd_ref)

        (qc, kc, vc, dos), (ss,) = _chain_loads([(qf, qr), (kf, kr), (vf, vr), (dof, dor)], [(ssf, ssr)])
        consts = [[r[i] for i in range(N_CHAIN)] for r in (dm_ref, qs_ref, ks_ref, cd_ref)]
        _, vjp = jax.vjp(_ret_chains, qc, kc, vc, *consts, ss)
        dq, dk, dv, ddm, dqs, dks, dcd, ds = vjp((dos, [ds_scr[i] for i in range(N_CHAIN)]))
        for i in range(N_CHAIN):
            d, h = i // 4, i % 4
            hs = slice(h * 128, (h + 1) * 128)
            (dqf, dqr)[d][:, hs] = dq[i]
            (dkf, dkr)[d][:, hs] = dk[i]
            (dvf, dvr)[d][:, hs] = dv[i]
            ddm_ref[i] += ddm[i]
            dqs_ref[i] += dqs[i]
            dks_ref[i] += dks[i]
            dcd_ref[i] += dcd[i]
            ds_scr[i] = ds[i]

    tok = jax.ShapeDtypeStruct((ROWS, 512), F32)
    return _scan_call(
        body, "ret_bwd",
        _both(sp, ["tok", "tok", "tokv"]) + _ret_const_specs() + _both(sp, ["state", "tok"]),
        _both(sp, ["tok", "tok", "tok"]) + _ret_const_specs(),
        [tok] * 6 + [jax.ShapeDtypeStruct((N_CHAIN, CHUNK, CHUNK), F32), jax.ShapeDtypeStruct((N_CHAIN, CHUNK, 1), F32),
                     jax.ShapeDtypeStruct((N_CHAIN, CHUNK, 1), F32), jax.ShapeDtypeStruct((N_CHAIN, 1, 1), F32)],
        [q, q, k, k, v, v, dm, qs, ks, cd, *ssave, do, do], ride)


N_QBLK = ROWS // B_BLOCK
CTX_QBLK = CTX_LEN // B_BLOCK


def _attn_heads(q, kc, vc, kw, vw, sink, valid):
    n = range(len(q))
    scale = B_HD ** -0.5
    s_c = [_bdot(q[i], kc[i], 1, 1) * scale for i in n]
    s_w = [jnp.where(valid, _bdot(q[i], kw[i], 1, 1) * scale, NEG) for i in n]
    m = [lax.stop_gradient(jnp.maximum(jnp.maximum(jnp.max(s_c[i], axis=-1, keepdims=True), sink[i]),
                                       jnp.max(s_w[i], axis=-1, keepdims=True))) for i in n]
    e_c = [jnp.exp(s_c[i] - m[i]) for i in n]
    e_w = [jnp.exp(s_w[i] - m[i]) for i in n]
    den = [jnp.sum(e_c[i], axis=-1, keepdims=True) + jnp.sum(e_w[i], axis=-1, keepdims=True)
           + jnp.exp(sink[i] - m[i]) for i in n]
    return [(_bdot(e_c[i], vc[i], 1, 0) + _bdot(e_w[i], vw[i], 1, 0)) / den[i] for i in n]


def _attn_loads(q_ref, kv_ref, sink_ref, start):
    q, kc, vc, kw, vw, sink = [], [], [], [], [], []
    for hk in range(B_KV_HEADS):
        ks = slice(hk * B_HD, (hk + 1) * B_HD)
        vs = slice(128 + hk * B_HD, 128 + (hk + 1) * B_HD)
        grp = (kv_ref[0:CTX_LEN, ks], kv_ref[0:CTX_LEN, vs],
               kv_ref[pl.ds(start, 3 * B_BLOCK), ks], kv_ref[pl.ds(start, 3 * B_BLOCK), vs])
        for g in range(4):
            h = hk * 4 + g
            q.append(q_ref[:, h * B_HD:(h + 1) * B_HD])
            for lst, val in zip((kc, vc, kw, vw), grp):
                lst.append(val)
            sink.append(jnp.full((1, 1), sink_ref[h], F32))
    return q, kc, vc, kw, vw, sink


def _window(blk):
    xblk = blk - CTX_QBLK
    first = jnp.clip((xblk - 1) * B_BLOCK, 0, SEQ - 3 * B_BLOCK)
    qpos = xblk * B_BLOCK + lax.broadcasted_iota(jnp.int32, (B_BLOCK, 3 * B_BLOCK), 0)
    kpos = first + lax.broadcasted_iota(jnp.int32, (B_BLOCK, 3 * B_BLOCK), 1)
    far = jnp.where(blk >= CTX_QBLK, 0, 2 * SEQ)
    valid = jnp.abs(kpos - qpos) + far <= WINDOW
    return pl.multiple_of(first + CTX_LEN, B_BLOCK), valid


def _attn_specs():
    qspec = pl.BlockSpec((B_BLOCK, 512), lambda i: (i, 0))
    kvspec = pl.BlockSpec((ROWS, 256), lambda i: (0, 0))
    return qspec, kvspec, pl.BlockSpec(memory_space=pltpu.SMEM)


def _attn_fwd_call(q, kv, sink, ride=None):
    def body(q_ref, kv_ref, sink_ref, o_ref):
        start, valid = _window(pl.program_id(0))
        out = _attn_heads(*_attn_loads(q_ref, kv_ref, sink_ref, start), valid)
        for h in range(B_Q_HEADS):
            o_ref[:, h * B_HD:(h + 1) * B_HD] = out[h]

    qspec, kvspec, sspec = _attn_specs()
    body, r_in, r_out, r_shape, r_scratch = _riding(body, 3, 1, 0, ride, (N_QBLK,))
    return pl.pallas_call(
        body,
        grid=(N_QBLK,),
        in_specs=[qspec, kvspec, sspec] + r_in,
        out_specs=[qspec] + r_out,
        out_shape=[jax.ShapeDtypeStruct((ROWS, 512), F32)] + r_shape,
        scratch_shapes=r_scratch,
        compiler_params=_cparams(("arbitrary",)),
        name="attn_fwd",
    )(q, kv, sink, *(ride[0] if ride else []))


def _attn_bwd_call(q, kv, sink, do, ride=None):
    def body(q_ref, kv_ref, sink_ref, do_ref, dq_ref, dkv_ref, dsink_ref):
        @pl.when(pl.program_id(0) == 0)
        def _():
            dkv_ref[...] = jnp.zeros_like(dkv_ref)
            dsink_ref[...] = jnp.zeros_like(dsink_ref)

        start, valid = _window(pl.program_id(0))
        _, vjp = jax.vjp(functools.partial(_attn_heads, valid=valid), *_attn_loads(q_ref, kv_ref, sink_ref, start))
        dq, dkc, dvc, dkw, dvw, dsink = vjp([do_ref[:, h * B_HD:(h + 1) * B_HD] for h in range(B_Q_HEADS)])
        for h in range(B_Q_HEADS):
            dq_ref[:, h * B_HD:(h + 1) * B_HD] = dq[h]
            dsink_ref[h:h + 1, :] += jnp.broadcast_to(dsink[h], (1, 128))
        for hk in range(B_KV_HEADS):
            ks = slice(hk * B_HD, (hk + 1) * B_HD)
            vs = slice(128 + hk * B_HD, 128 + (hk + 1) * B_HD)
            grp = lambda parts: parts[hk * 4] + parts[hk * 4 + 1] + parts[hk * 4 + 2] + parts[hk * 4 + 3]
            dkv_ref[0:CTX_LEN, ks] += grp(dkc)
            dkv_ref[0:CTX_LEN, vs] += grp(dvc)
            dkv_ref[pl.ds(start, 3 * B_BLOCK), ks] += grp(dkw)
            dkv_ref[pl.ds(start, 3 * B_BLOCK), vs] += grp(dvw)

    qspec, kvspec, sspec = _attn_specs()
    body, r_in, r_out, r_shape, r_scratch = _riding(body, 4, 3, 0, ride, (N_QBLK,))
    return pl.pallas_call(
        body,
        grid=(N_QBLK,),
        in_specs=[qspec, kvspec, sspec, qspec] + r_in,
        out_specs=[qspec, kvspec, pl.BlockSpec((8, 128), lambda i: (0, 0))] + r_out,
        out_shape=[jax.ShapeDtypeStruct((ROWS, 512), F32), jax.ShapeDtypeStruct((ROWS, 256), F32),
                   jax.ShapeDtypeStruct((8, 128), F32)] + r_shape,
        scratch_shapes=r_scratch,
        compiler_params=_cparams(("arbitrary",)),
        name="attn_bwd",
    )(q, kv, sink, do, *(ride[0] if ride else []))


def _my_id():
    return 4 * lax.axis_index("x") + 2 * lax.axis_index("y") + lax.axis_index("c")


def _peer(k):
    x, y, c = lax.axis_index("x"), lax.axis_index("y"), lax.axis_index("c")
    return (1 - x if k & 4 else x, 1 - y if k & 2 else y, 1 - c if k & 1 else c)


def _exchange_copies(ins, outs, sems, gather):
    send_sems, recv_sems, local_sems = sems
    me = _my_id()
    own, remote = [], []
    for a in range(len(ins)):
        own.append(pltpu.make_async_copy(ins[a] if gather else ins[a].at[me], outs[a].at[me], local_sems.at[a]))
        for k in range(1, N_DEV):
            peer_slot = jnp.bitwise_xor(me, k)
            src = ins[a] if gather else ins[a].at[peer_slot]
            common = dict(src_ref=src, send_sem=send_sems.at[a, k - 1], recv_sem=recv_sems.at[a, k - 1],
                          device_id=_peer(k), device_id_type=MESH)
            remote.append((pltpu.make_async_remote_copy(dst_ref=outs[a].at[me], **common),
                           pltpu.make_async_remote_copy(dst_ref=outs[a].at[peer_slot], **common)))
    return own, remote


def _exchange_start(ins, outs, sems, gather):
    own, remote = _exchange_copies(ins, outs, sems, gather)
    for cp in own:
        cp.start()
    for send, _ in remote:
        send.start()


def _exchange_wait(ins, outs, sems, gather):
    own, remote = _exchange_copies(ins, outs, sems, gather)
    for _, arrival in remote:
        arrival.wait_recv()
    for send, _ in remote:
        send.wait_send()
    for cp in own:
        cp.wait()


def _exchange_plumbing(arrays, gather):
    n = len(arrays)
    hbm = [pl.BlockSpec(memory_space=pltpu.HBM)] * n
    out_shape = [jax.ShapeDtypeStruct((N_DEV,) + (a.shape if gather else a.shape[1:]), a.dtype) for a in arrays]
    sems = [pltpu.SemaphoreType.DMA((n, N_DEV - 1)), pltpu.SemaphoreType.DMA((n, N_DEV - 1)),
            pltpu.SemaphoreType.DMA((n,))]
    return hbm, out_shape, sems


def _exchange(arrays, gather, name):
    n = len(arrays)

    def body(*refs):
        ins, outs, sems = refs[:n], refs[n:2 * n], refs[2 * n:]
        _exchange_start(ins, outs, sems, gather)
        _exchange_wait(ins, outs, sems, gather)

    hbm, out_shape, sems = _exchange_plumbing(arrays, gather)
    return pl.pallas_call(
        body,
        in_specs=hbm,
        out_specs=hbm,
        out_shape=out_shape,
        scratch_shapes=sems,
        compiler_params=pltpu.CompilerParams(has_side_effects=True),
        name=name,
    )(*arrays)


def _riding(body, n_in, n_out, n_scratch, ride, grid):
    if ride is None:
        return body, [], [], [], []
    arrays, gather = ride
    n = len(arrays)

    def at(step_of):
        hit = pl.program_id(0) == step_of(grid[0])
        for d in range(1, len(grid)):
            hit = jnp.logical_and(hit, pl.program_id(d) == step_of(grid[d]))
        return hit

    def wrapped(*refs):
        ins, rin = refs[:n_in], refs[n_in:n_in + n]
        outs = refs[n_in + n:n_in + n + n_out]
        rout = refs[n_in + n + n_out:n_in + 2 * n + n_out]
        scratch = refs[n_in + 2 * n + n_out:n_in + 2 * n + n_out + n_scratch]
        sems = refs[n_in + 2 * n + n_out + n_scratch:]

        @pl.when(at(lambda size: 0))
        def _():
            _exchange_start(rin, rout, sems, gather)

        body(*ins, *outs, *scratch)

        @pl.when(at(lambda size: size - 1))
        def _():
            _exchange_wait(rin, rout, sems, gather)

    hbm, out_shape, sems = _exchange_plumbing(arrays, gather)
    return wrapped, hbm, hbm, out_shape, sems


def _sum_contributions(c_ref):
    g = c_ref[0].astype(F32)
    for j in range(1, N_DEV):
        g = g + c_ref[j].astype(F32)
    return g


def _adamw_update(g, w_ref, m_ref, v_ref, g_ref, d_ref, nm_ref, nv_ref):
    m_new = ADAM_B1 * m_ref[...] + (1.0 - ADAM_B1) * g
    v_new = ADAM_B2 * v_ref[...] + (1.0 - ADAM_B2) * (g * g)
    m_hat = m_new / (1.0 - ADAM_B1 ** ADAM_STEP)
    v_hat = v_new / (1.0 - ADAM_B2 ** ADAM_STEP)
    g_ref[...] = g
    d_ref[...] = -ADAM_LR * (m_hat / (jnp.sqrt(v_hat) + ADAM_EPS) + ADAM_WD * w_ref[...])
    nm_ref[...] = m_new
    nv_ref[...] = v_new


def _adamw_layers(w, m, v, contrib0, contrib1, name):
    _, r, c = w.shape
    br = _pick(r, (256, 128, 64, 32, 16, 8))
    nb = r // br

    def body(w_ref, m_ref, v_ref, c0_ref, c1_ref, g_ref, d_ref, nm_ref, nv_ref):
        g = jnp.where(pl.program_id(0) == 0, _sum_contributions(c0_ref), _sum_contributions(c1_ref))
        _adamw_update(g, w_ref, m_ref, v_ref, g_ref, d_ref, nm_ref, nv_ref)

    spec = pl.BlockSpec((None, br, c), lambda l, i: (l, i, 0))
    return pl.pallas_call(
        body,
        grid=(DEPTH, nb),
        in_specs=[spec, spec, spec,
                  pl.BlockSpec((N_DEV, br, c), lambda l, i: (0, jnp.where(l == 0, i, nb - 1), 0)),
                  pl.BlockSpec((N_DEV, br, c), lambda l, i: (0, jnp.where(l == 1, i, 0), 0))],
        out_specs=[spec] * 4,
        out_shape=[jax.ShapeDtypeStruct(w.shape, F32)] * 4,
        compiler_params=_cparams(("arbitrary", "arbitrary")),
        name=name,
    )(w, m, v, contrib0, contrib1)


def _adamw(w, m, v, contrib, name):
    r, c = w.shape
    br = _pick(r, (256, 128, 64, 32, 16, 8))

    def body(w_ref, m_ref, v_ref, c_ref, g_ref, d_ref, nm_ref, nv_ref):
        _adamw_update(_sum_contributions(c_ref), w_ref, m_ref, v_ref, g_ref, d_ref, nm_ref, nv_ref)

    spec = pl.BlockSpec((br, c), lambda i: (i, 0))
    cspec = pl.BlockSpec((N_DEV, br, c), lambda i: (0, i, 0))
    return pl.pallas_call(
        body,
        grid=(r // br,),
        in_specs=[spec, spec, spec, cspec],
        out_specs=[spec] * 4,
        out_shape=[jax.ShapeDtypeStruct((r, c), F32)] * 4,
        compiler_params=_cparams(("parallel",)),
        name=name,
    )(w, m, v, contrib)


def _silu(x):
    return x * jax.nn.sigmoid(x)


def _rope_angles(pos, n_freq):
    inv = ROPE_BASE ** (-jnp.arange(n_freq, dtype=F32) / n_freq)
    return pos[:, None] * inv[None, :]


def _with_ctx_rows(cos, sin):
    return (jnp.concatenate([jnp.ones((CTX_LEN, 128), F32), cos], axis=0),
            jnp.concatenate([jnp.zeros((CTX_LEN, 128), F32), sin], axis=0))


def _rope_tables():
    rows_n = SEQ // GRID_W
    rows = jnp.repeat(jnp.arange(rows_n, dtype=F32), GRID_W)
    cols = jnp.tile(jnp.arange(GRID_W, dtype=F32), rows_n)
    ang_r = _rope_angles(rows, B_HD // 4)
    ang_c = _rope_angles(cols, B_HD // 4)
    cos_b = jnp.tile(jnp.concatenate([jnp.cos(ang_r)] * 2 + [jnp.cos(ang_c)] * 2, axis=1), (1, 2))
    sin_b = jnp.tile(jnp.concatenate([-jnp.sin(ang_r), jnp.sin(ang_r), -jnp.sin(ang_c), jnp.sin(ang_c)], axis=1), (1, 2))
    ang = _rope_angles(jnp.arange(SEQ, dtype=F32), C_HD // 2)
    cos_c = jnp.concatenate([jnp.cos(ang)] * 2, axis=1)
    sin_c = jnp.concatenate([-jnp.sin(ang), jnp.sin(ang)], axis=1)
    return _with_ctx_rows(cos_b, sin_b), _with_ctx_rows(cos_c, sin_c)


def _halves(a):
    return a[:4], a[4:]


def _delta_gates(ab, a_log, dt_bias):
    beta = jax.nn.sigmoid(ab[:, :8])
    g = -jnp.exp(a_log)[None, :] * jax.nn.softplus(ab[:, 8:] + dt_bias[None, :])
    gch = g.reshape(N_CHUNK, CHUNK, 8)
    tri = jnp.tril(jnp.ones((CHUNK, CHUNK), F32))
    fwd = jnp.einsum("ij,cjh->cih", tri, gch[..., :4], precision=HIGHEST)
    bwd = jnp.einsum("ji,cjh->cih", tri, gch[..., 4:], precision=HIGHEST)
    gc = jnp.concatenate([fwd, bwd], axis=-1)
    gl = jnp.sum(gch, axis=1)
    rows = lambda a: _halves(a.transpose(2, 0, 1)[:, :, None, :])
    return rows(beta.reshape(N_CHUNK, CHUNK, 8)), rows(gc), _halves(gl.T[:, :, None, None])


def _ret_consts(c_decay):
    lg = jax.nn.log_sigmoid(c_decay)
    idx = jnp.arange(CHUNK, dtype=F32)
    diff = idx[:, None] - idx[None, :]
    lgf, lgb = lg[:4, None, None], lg[4:, None, None]
    dm = jnp.concatenate([jnp.exp(jnp.where(diff >= 0, diff * lgf, -jnp.inf)),
                          jnp.exp(jnp.where(diff <= 0, -diff * lgb, -jnp.inf))], axis=0)
    qs = jnp.concatenate([jnp.exp((idx + 1.0)[None, :] * lg[:4, None]),
                          jnp.exp((CHUNK - idx)[None, :] * lg[4:, None])], axis=0)[:, :, None]
    ks = jnp.concatenate([jnp.exp((CHUNK - 1.0 - idx)[None, :] * lg[:4, None]),
                          jnp.exp(idx[None, :] * lg[4:, None])], axis=0)[:, :, None]
    return dm, qs, ks, jnp.exp(CHUNK * lg)[:, None, None]


A_PIECES = ((0, True, A_DK ** -0.5, "a_q"), (1, True, 1.0, "a_k"), (2, False, 1.0, "a_v"))
B_ROPE_COLS = [C_BQ // 512, C_BKV // 256, 0, 0]
C_ROPE_COLS = [C_CQ // 512, C_CK // 512, 0, 0]
MERGE_COLS = [0, 0, 0, C_MERGE // 1024, C_MERGE // 1024 + 1, C_MERGE // 1024 + 2]


def _conv8(conv_w):
    return jnp.pad(conv_w, ((0, 8 - A_CONV), (0, 0)))


W_IN_TILES = {"nn": (2176, 512, 1024), "nt": (1088, 1024, 2176), "db": (1024, 512, ROWS)}


def _core_forward(h, w16, p, rides):
    res = _matmul(h, w16, "w_in", "nn", W_IN_TILES["nn"], ride=rides.get("w_in"))
    proj, rode = (res[0], {"w_in": res[1:]}) if "w_in" in rides else (res, {})
    wb = p["w_branch"] if "w_in" not in rides else _unshard_layer("w_branch", rode["w_in"][0])
    (cos_b, sin_b), (cos_c, sin_c) = _rope_tables()
    conv8 = _conv8(p["a_conv_w"])
    q, k, v = [_a_prep_fwd(proj, conv8, col, nrm, scl, nm) for col, nrm, scl, nm in A_PIECES]
    gates = _delta_gates(proj[:, C_AB:C_AB + 16], p["a_log"], p["a_dt_bias"])
    res = _delta_fwd_call(q, k, v, *gates, ride=rides.get("delta"))
    (of, orv, ssf, ssr, tsf, tsr), rode["delta"] = res[:6], res[6:]
    (y_a,) = _a_out.fwd([(of, orv), proj], [p["a_norm_w"][None, :]], [0, C_AZ // 512])

    qb, kvb = _b_rope.fwd([proj, proj, cos_b, sin_b], [], B_ROPE_COLS)
    res = _attn_fwd_call(qb, kvb, p["b_sink"], ride=rides.get("attn"))
    ob, rode["attn"] = res[0], res[1:]
    (y_b,) = _b_out.fwd([ob, proj], [], [0, C_BZ // 512])

    qc, kc = _c_rope.fwd([proj, proj, cos_c, sin_c], [], C_ROPE_COLS)
    res = _ret_fwd_call(qc, kc, proj, C_CV // 512, *_ret_consts(p["c_decay"]), ride=rides.get("ret"))
    (cf, cr, csf, csr), rode["ret"] = res[:4], res[4:]
    (y_c,) = _c_out.fwd([(cf, cr), proj], [p["c_norm_w"][None, :]], [0, C_CZ // 512])

    (merged,) = _branch_merge.fwd([y_a, y_b, y_c, proj, proj, proj], [wb[0], wb[1], wb[2]], MERGE_COLS)
    saved = dict(proj=proj, q=q, k=k, v=v, of=of, orv=orv, ss=(ssf, ssr), ts=(tsf, tsr), qb=qb, kvb=kvb, ob=ob,
                 qc=qc, kc=kc, cf=cf, cr=cr, cs=(csf, csr), y=(y_a, y_b, y_c), wb=wb)
    return merged, saved, rode


def _core_backward(h, w16, p, s, dmerged, rides, branch_rides_in_attn=False):
    proj, wb = s["proj"], s["wb"]
    (cos_b, sin_b), (cos_c, sin_c) = _rope_tables()
    conv8 = _conv8(p["a_conv_w"])
    y_a, y_b, y_c = s["y"]
    rode = {}

    (*dy, dma, dmb, dmc), dwb = _branch_merge.bwd([y_a, y_b, y_c, proj, proj, proj], [wb[0], wb[1], wb[2]],
                                                   [dmerged], MERGE_COLS, bf16_rows=(3, 4, 5))
    dwb = jnp.stack(dwb)

    consts, consts_vjp = jax.vjp(_ret_consts, p["c_decay"])
    (do_c, dcz), (dcnw,) = _c_out.bwd([(s["cf"], s["cr"]), proj], [p["c_norm_w"][None, :]], [dy[2]],
                                      [0, C_CZ // 512], bf16_rows=(1,))
    g = _ret_bwd_call(s["qc"], s["kc"], proj, C_CV // 512, *consts, s["cs"], do_c, ride=rides.get("ret"))
    rode["ret"] = g[10:]
    (dcq, dck), _ = _c_rope.bwd([proj, proj, cos_c, sin_c], [], [(g[0], g[1]), (g[2], g[3])], C_ROPE_COLS,
                                bf16_rows=(0, 1))
    dcv = (g[4] + g[5]).astype(BF16)
    (dc_decay,) = consts_vjp(tuple(g[6:10]))

    (dob, dbz), _ = _b_out.bwd([s["ob"], proj], [], [dy[1]], [0, C_BZ // 512], bf16_rows=(1,))
    attn_ride = rides.get("attn")
    if branch_rides_in_attn:
        attn_ride = (list(attn_ride[0]) + [_reshard_layer("w_branch", dwb).astype(BF16)], attn_ride[1])
    res = _attn_bwd_call(s["qb"], s["kvb"], p["b_sink"], dob, ride=attn_ride)
    (dqb, dkvb, dsink), rode["attn"] = res[:3], res[3:]
    (dbq, dbkv), _ = _b_rope.bwd([proj, proj, cos_b, sin_b], [], [dqb, dkvb], B_ROPE_COLS, bf16_rows=(0, 1))

    ab = proj[:, C_AB:C_AB + 16]
    gates, gates_vjp = jax.vjp(_delta_gates, ab, p["a_log"], p["a_dt_bias"])
    (do_a, daz), (danw,) = _a_out.bwd([(s["of"], s["orv"]), proj], [p["a_norm_w"][None, :]], [dy[0]],
                                      [0, C_AZ // 512], bf16_rows=(1,))
    g = _delta_bwd_call(s["q"], s["k"], s["v"], *gates, s["ss"], s["ts"], do_a, ride=rides.get("delta"))
    rode["delta"] = g[12:]
    dgates = ((g[6], g[7]), (g[8], g[9]), (g[10], g[11]))
    dab, da_log, ddt = gates_vjp(dgates)
    dpre, dconv = [], []
    for (col, nrm, scl, nm), df, dr in zip(A_PIECES, (g[0], g[2], g[4]), (g[1], g[3], g[5])):
        dx, dw = _a_prep_bwd(proj, conv8, col, nrm, scl, df, dr, nm)
        dpre.append(dx)
        dconv.append(dw[:A_CONV])

    dproj = jnp.concatenate(dpre + [daz, dbq, dbz, dcq, dck, dcv, dcz, dma, dmb, dmc, dbkv,
                                    jnp.pad(dab, ((0, 0), (0, IN_PAD - C_AB - 16))).astype(BF16)], axis=1)
    dh = _matmul(dproj, w16, "w_in_da", "nt", W_IN_TILES["nt"])
    dw = _matmul(h.T.astype(BF16), dproj, "w_in_db", "nn", W_IN_TILES["db"])
    dp = dict(a_conv_w=jnp.concatenate(dconv, axis=1), a_log=da_log, a_dt_bias=ddt, a_norm_w=danw[0],
              b_sink=dsink[:, 0], c_decay=dc_decay, c_norm_w=dcnw[0], w_branch=dwb)
    return dh, dw, dp, rode


CORE_PARAMS = ("a_conv_w", "a_log", "a_dt_bias", "a_norm_w", "b_sink", "c_decay", "c_norm_w", "w_branch")


def _pad_w_in(w):
    return jnp.concatenate([w[..., 0:2048], w[..., 2064:2576], w[..., 2832:3344], w[..., 3344:8464],
                            w[..., 2576:2832], w[..., 2048:2064],
                            jnp.zeros(w.shape[:-1] + (IN_PAD - IN_WIDTH,), w.dtype)], axis=-1)


def _unpad_w_in(g):
    return jnp.concatenate([g[..., 0:2048], g[..., C_AB:C_AB + 16], g[..., C_BQ:C_BQ + 512],
                            g[..., C_BKV:C_BKV + 256], g[..., C_BZ:C_BZ + 512], g[..., C_CQ:C_BKV]], axis=-1)


LAYER_SHARDED = ("w_ada", "w_in", "w_branch", "w_out")


def _unshard_layer(name, g):
    if name == "w_branch":
        return g.transpose(1, 2, 0, 3).reshape(3, BR_WIDTH, D_MODEL)
    if name == "w_out":
        return g.reshape(D_MODEL, D_MODEL)
    return g.transpose(1, 0, 2).reshape(D_MODEL, -1)


def _reshard_layer(name, w):
    if name == "w_branch":
        return w.reshape(3, BR_WIDTH, N_DEV, D_MODEL // N_DEV).transpose(2, 0, 1, 3)
    if name == "w_out":
        return w.reshape(N_DEV, D_MODEL // N_DEV, D_MODEL)
    return w.reshape(D_MODEL, N_DEV, -1).transpose(1, 0, 2)


def _layer_weights(gathered):
    out = {n: _unshard_layer(n, g) for n, g in gathered.items()}
    out["w_in16"] = _pad_w_in(out.pop("w_in"))
    return out


def _forward_backward(small, layer0, shards0, shards1, x, c, ctx, loss_target):
    c_ctx = small["c_ctx"]
    sc16 = jnp.zeros((16, D_MODEL), F32).at[0].set(_silu(c)).at[1].set(_silu(c_ctx))
    xs = jnp.concatenate([ctx, x], axis=0)
    weights = [dict(layer0), None]
    layers = []
    for l in range(DEPTH):
        wl = weights[l]
        mod16 = _matmul(sc16, wl["w_ada"], "ada") + small["b_ada"][l][None, :]
        mod_cx = jnp.stack([mod16[1], mod16[0]])
        shift, scale, gate = jnp.split(mod_cx, 3, axis=1)
        nw = small["norm_w"][l][None, :]
        (h,) = _norm_mod.fwd([xs], [nw, shift, scale])
        p = {n: small[n][l] for n in CORE_PARAMS if n != "w_branch"}
        p["w_branch"] = wl.get("w_branch")
        rides = {}
        if l == 0:
            rides = {"w_in": ([shards0["w_branch"], shards0["w_out"]], True), "delta": ([shards1["w_in"]], True),
                     "attn": ([shards1["w_ada"]], True), "ret": ([shards1["w_branch"], shards1["w_out"]], True)}
        merged, saved, rode = _core_forward(h, wl["w_in16"], p, rides)
        if l == 0:
            wl["w_out"] = _unshard_layer("w_out", rode["w_in"][1])
            weights[1] = _layer_weights(dict(w_in=rode["delta"][0], w_ada=rode["attn"][0],
                                             w_branch=rode["ret"][0], w_out=rode["ret"][1]))
        (xs_next,) = _out_residual.fwd([xs, merged], [wl["w_out"], gate])
        layers.append(dict(xs=xs, h=h, p=p, saved=saved, merged=merged, gate=gate, nw=nw, shift=shift, scale=scale))
        xs = xs_next
    fw = small["final_norm_w"][None, :]
    xs = xs[CTX_LEN:]
    (per_row,) = _loss_rows.fwd([xs, loss_target], [fw])
    loss = jnp.sum(per_row[:, 0])

    d_per_row = jnp.zeros((SEQ, 128), F32).at[:, 0].set(1.0)
    (dxs,), (dfw,) = _loss_rows.bwd([xs, loss_target], [fw], [d_per_row])
    dxs = jnp.pad(dxs, ((CTX_LEN, 0), (0, 0)))
    small_names = tuple(n for n in CORE_PARAMS if n != "w_branch") + ("b_ada", "norm_w")
    dsmall = {n: [None] * DEPTH for n in small_names}
    dlayer = [None] * DEPTH
    contrib0 = contrib1 = None
    dsc16 = jnp.zeros((16, D_MODEL), F32)
    for l in reversed(range(DEPTH)):
        s, wl = layers[l], weights[l]
        (dres, dmerged), (dw_out, dgate) = _out_residual.bwd([s["xs"], s["merged"]], [wl["w_out"], s["gate"]], [dxs])
        rides = {}
        if l == 0:
            blocks1 = {n: _reshard_layer(n, g).astype(BF16) for n, g in dlayer[1].items()}
            rides = {"ret": ([blocks1["w_branch"], blocks1["w_out"]], False),
                     "attn": ([_reshard_layer("w_out", dw_out).astype(BF16), blocks1["w_ada"]], False),
                     "delta": ([blocks1["w_in"]], False)}
        dh, dw_in, dp, rode = _core_backward(s["h"], wl["w_in16"], s["p"], s["saved"], dmerged, rides,
                                             branch_rides_in_attn=(l == 0))
        if l == 0:
            contrib1 = dict(w_in=rode["delta"][0], w_ada=rode["attn"][1], w_branch=rode["ret"][0],
                            w_out=rode["ret"][1])
            contrib0 = dict(w_out=rode["attn"][0], w_branch=rode["attn"][2])
        (dxn,), (dnw, dshift, dscale) = _norm_mod.bwd([s["xs"]], [s["nw"], s["shift"], s["scale"]], [dh])
        dxs = dres + dxn
        dmod_cx = jnp.concatenate([dshift, dscale, dgate], axis=1)
        dmod16 = jnp.zeros((16, 3 * D_MODEL), F32).at[0].set(dmod_cx[1]).at[1].set(dmod_cx[0])
        dsc16 = dsc16 + _matmul(dmod16, wl["w_ada"], "ada_da", "nt")
        dlayer[l] = dict(w_ada=_matmul(sc16, dmod16, "ada_db", "tn"), w_in=_unpad_w_in(dw_in),
                         w_branch=dp["w_branch"], w_out=dw_out)
        for n in small_names:
            if n in dp:
                dsmall[n][l] = dp[n]
        dsmall["norm_w"][l] = dnw[0]
        dsmall["b_ada"][l] = dmod_cx[0] + dmod_cx[1]
    gsmall = {n: jnp.stack(v) for n, v in dsmall.items()}
    gsmall["final_norm_w"] = dfw[0]
    sig = jax.nn.sigmoid(c_ctx)
    gsmall["c_ctx"] = dsc16[1] * sig * (1.0 + c_ctx * (1.0 - sig))
    return loss, dxs[CTX_LEN:], gsmall, {n: dlayer[0][n] for n in ("w_ada", "w_in")}, contrib0, contrib1


SMALL = ("c_ctx", "b_ada", "norm_w", "a_log", "a_dt_bias", "a_norm_w", "b_sink", "c_decay", "c_norm_w",
         "final_norm_w")
WEIGHTS = ("c_ctx", "w_ada", "b_ada", "norm_w", "w_in", "a_conv_w", "a_log", "a_dt_bias", "a_norm_w", "b_sink",
           "c_decay", "c_norm_w", "w_branch", "w_out", "final_norm_w")
SMALL_PACK = 12288


def _unshard_conv(g):
    return g.transpose(1, 2, 0, 3).reshape(DEPTH, A_CONV, 3 * A_WIDTH)


def _reshard_conv(w):
    return w.reshape(DEPTH, A_CONV, N_DEV, 3 * A_WIDTH // N_DEV).transpose(2, 0, 1, 3)


def _pack_small(tree):
    flat = jnp.concatenate([tree[n].reshape(-1) for n in SMALL])
    return jnp.pad(flat, (0, SMALL_PACK - flat.shape[0])).reshape(SMALL_PACK // 128, 128)


def _unpack_small(packed, like):
    flat = packed.reshape(-1)
    out, off = {}, 0
    for n in SMALL:
        size = math.prod(like[n].shape)
        out[n] = flat[off:off + size].reshape(like[n].shape)
        off += size
    return out


def kernel(x, c, ctx, c_ctx, w_ada, b_ada, norm_w, w_in, a_conv_w, a_log, a_dt_bias, a_norm_w, b_sink, c_decay, c_norm_w, w_branch, w_out, final_norm_w, loss_target, m_c_ctx, m_w_ada, m_b_ada, m_norm_w, m_w_in, m_a_conv_w, m_a_log, m_a_dt_bias, m_a_norm_w, m_b_sink, m_c_decay, m_c_norm_w, m_w_branch, m_w_out, m_final_norm_w, v_c_ctx, v_w_ada, v_b_ada, v_norm_w, v_w_in, v_a_conv_w, v_a_log, v_a_dt_bias, v_a_norm_w, v_b_sink, v_c_decay, v_c_norm_w, v_w_branch, v_w_out, v_final_norm_w):
    w = dict(c_ctx=c_ctx, w_ada=w_ada, b_ada=b_ada, norm_w=norm_w, w_in=w_in, a_conv_w=a_conv_w, a_log=a_log,
             a_dt_bias=a_dt_bias, a_norm_w=a_norm_w, b_sink=b_sink, c_decay=c_decay, c_norm_w=c_norm_w,
             w_branch=w_branch, w_out=w_out, final_norm_w=final_norm_w)
    m = dict(c_ctx=m_c_ctx, w_ada=m_w_ada, b_ada=m_b_ada, norm_w=m_norm_w, w_in=m_w_in, a_conv_w=m_a_conv_w,
             a_log=m_a_log, a_dt_bias=m_a_dt_bias, a_norm_w=m_a_norm_w, b_sink=m_b_sink, c_decay=m_c_decay,
             c_norm_w=m_c_norm_w, w_branch=m_w_branch, w_out=m_w_out, final_norm_w=m_final_norm_w)
    v = dict(c_ctx=v_c_ctx, w_ada=v_w_ada, b_ada=v_b_ada, norm_w=v_norm_w, w_in=v_w_in, a_conv_w=v_a_conv_w,
             a_log=v_a_log, a_dt_bias=v_a_dt_bias, a_norm_w=v_a_norm_w, b_sink=v_b_sink, c_decay=v_c_decay,
             c_norm_w=v_c_norm_w, w_branch=v_w_branch, w_out=v_w_out, final_norm_w=v_final_norm_w)

    shards = {n: w[n].astype(BF16) for n in LAYER_SHARDED}
    first = _exchange([shards["w_ada"][0], shards["w_in"][0], w["a_conv_w"]], True, "gather_layer0")
    layer0 = _layer_weights(dict(w_ada=first[0], w_in=first[1]))
    small_w = {n: w[n] for n in SMALL}
    small_w["a_conv_w"] = _unshard_conv(first[2])
    loss, gx, gw, glayer0, contrib0, contrib1 = _forward_backward(
        small_w, layer0, {n: shards[n][0] for n in ("w_branch", "w_out")}, {n: shards[n][1] for n in LAYER_SHARDED},
        x[0], c[0], ctx[0], loss_target[0])
    loss = lax.psum(loss, ("x", "y", "c"))

    last = _exchange([_reshard_layer("w_ada", glayer0["w_ada"]).astype(BF16),
                      _reshard_layer("w_in", glayer0["w_in"]).astype(BF16), _reshard_conv(gw["a_conv_w"])],
                     False, "scatter_layer0")
    contrib0["w_ada"], contrib0["w_in"] = last[0], last[1]
    small = _exchange([_pack_small(gw)], True, "gather_small_grads")[0]

    grad, delta, new_m, new_v = {}, {}, {}, {}
    for n in LAYER_SHARDED:
        shp = w[n].shape
        per_layer = (math.prod(shp[1:-1]), shp[-1])
        outs = _adamw_layers(*[a.reshape((DEPTH,) + per_layer) for a in (w[n], m[n], v[n])],
                             *[cb.reshape((N_DEV,) + per_layer) for cb in (contrib0[n], contrib1[n])], "adamw_" + n)
        grad[n], delta[n], new_m[n], new_v[n] = [o.reshape(shp) for o in outs]
    shp = a_conv_w.shape
    two_d = (math.prod(shp[:-1]), shp[-1])
    outs = _adamw(*[a.reshape(two_d) for a in (a_conv_w, m_a_conv_w, v_a_conv_w)],
                  last[2].reshape((N_DEV,) + two_d), "adamw_a_conv_w")
    grad["a_conv_w"], delta["a_conv_w"], new_m["a_conv_w"], new_v["a_conv_w"] = [o.reshape(shp) for o in outs]
    outs = _adamw(_pack_small(w), _pack_small(m), _pack_small(v), small, "adamw_small")
    for tree, packed in zip((grad, delta, new_m, new_v), outs):
        tree.update(_unpack_small(packed, w))

    return (loss, gx[None], *[grad[n] for n in WEIGHTS], *[delta[n] for n in WEIGHTS],
            *[new_m[n] for n in WEIGHTS], *[new_v[n] for n in WEIGHTS])
```

```python
import functools
import math

import jax
import jax.numpy as jnp
from jax import lax
from jax.experimental import pallas as pl
from jax.experimental.pallas import tpu as pltpu

F32 = jnp.float32
BF16 = jnp.bfloat16
HIGHEST = lax.Precision.HIGHEST

D_MODEL = 1024
SEQ = 4096
DEPTH = 2
GRID_W = 64
CTX_LEN = 256
EPS = 1e-6
ROPE_BASE = 10000.0
BR_WIDTH = D_MODEL // 2
A_DK = 128
A_HEADS = 4
A_WIDTH = 512
A_CONV = 5
B_HD = 64
B_Q_HEADS = 8
B_KV_HEADS = 2
WINDOW = 128
B_BLOCK = 128
C_HD = 128
C_HEADS = 4
C_WIDTH = 512
CHUNK = 64
ADAM_LR = 0.001
ADAM_B1 = 0.9
ADAM_B2 = 0.999
ADAM_EPS = 1e-08
ADAM_WD = 0.01
ADAM_STEP = 10

N_DEV = 8
ROWS = CTX_LEN + SEQ
N_CHUNK = ROWS // CHUNK
N_CTX_CHUNK = CTX_LEN // CHUNK
IN_WIDTH = 8464
IN_PAD = 8704
NEG = -1e30

VMEM_LIMIT = 48 * 1024 * 1024
MESH = pl.DeviceIdType.MESH

C_AQ, C_AK, C_AV, C_AZ, C_BQ, C_BZ, C_CQ, C_CK, C_CV, C_CZ = (i * 512 for i in range(10))
C_MERGE = 5120
C_BKV = 8192
C_AB = 8448


def _cparams(sem=None):
    if sem is None:
        return pltpu.CompilerParams(vmem_limit_bytes=VMEM_LIMIT)
    return pltpu.CompilerParams(dimension_semantics=sem, vmem_limit_bytes=VMEM_LIMIT)


def _dg(a, b, ca, cb, prec=None):
    return lax.dot_general(a, b, (((ca,), (cb,)), ((), ())), preferred_element_type=F32, precision=prec)


@functools.partial(jax.custom_vjp, nondiff_argnums=(2, 3))
def _bdot(a, b, ca, cb):
    return _dg(a.astype(BF16), b.astype(BF16), ca, cb)


def _bdot_fwd(a, b, ca, cb):
    return _bdot(a, b, ca, cb), (a, b)


def _bdot_bwd(ca, cb, res, ct):
    a, b = res
    da = _bdot(ct, b, 1, 1 - cb) if ca == 1 else _bdot(b, ct, 1 - cb, 1)
    db = _bdot(a, ct, 1 - ca, 0) if cb == 0 else _bdot(ct, a, 0, 1 - ca)
    return da, db


_bdot.defvjp(_bdot_fwd, _bdot_bwd)


def _hdot(a, b):
    return _dg(a, b, 1, 0, lax.Precision.HIGH)


def _k_silu(x):
    return x / (1.0 + jnp.exp(-x))


def _k_sigmoid(x):
    return 1.0 / (1.0 + jnp.exp(-x))


@jax.custom_vjp
def _swap64(x):
    return pltpu.roll(x, 64, 1)


_swap64.defvjp(lambda x: (pltpu.roll(x, 64, 1), None), lambda _, ct: (pltpu.roll(ct, 64, 1),))


def _swap16_impl(x):
    lane = lax.broadcasted_iota(jnp.int32, x.shape, 1)
    return jnp.where((lane & 16) == 0, pltpu.roll(x, 112, 1), pltpu.roll(x, 16, 1))


@jax.custom_vjp
def _swap16(x):
    return _swap16_impl(x)


_swap16.defvjp(lambda x: (_swap16_impl(x), None), lambda _, ct: (_swap16_impl(ct),))


def _pick(dim, prefs):
    for p in prefs:
        if dim % p == 0:
            return p
    return dim


def _matmul(a, b, name, mode="nn", tiles=None, ride=None):
    ca, cb = {"nn": (1, 0), "nt": (1, 1), "tn": (0, 0)}[mode]
    m, k = a.shape[1 - ca], a.shape[ca]
    n = b.shape[1 - cb]
    if tiles is None:
        tiles = (_pick(m, (1088, 1024, 512, 256, 128)), _pick(n, (512, 256, 128)),
                 _pick(k, (1088, 1024, 512, 256, 128) if mode == "tn" else (2176, 2048, 1024, 512, 256, 128)))
    tm, tn, tk = tiles
    nk = k // tk
    a_spec = (pl.BlockSpec((tm, tk), lambda i, j, kk: (i, kk)) if ca == 1
              else pl.BlockSpec((tk, tm), lambda i, j, kk: (kk, i)))
    b_spec = (pl.BlockSpec((tk, tn), lambda i, j, kk: (kk, j)) if cb == 0
              else pl.BlockSpec((tn, tk), lambda i, j, kk: (j, kk)))

    def body(a_ref, b_ref, o_ref):
        part = _dg(a_ref[...].astype(BF16), b_ref[...].astype(BF16), ca, cb)
        if nk == 1:
            o_ref[...] = part
        else:
            kk = pl.program_id(2)

            @pl.when(kk == 0)
            def _():
                o_ref[...] = part

            @pl.when(kk > 0)
            def _():
                o_ref[...] += part

    grid = (m // tm, n // tn, nk)
    if ride is None:
        return pl.pallas_call(
            body,
            grid=grid,
            in_specs=[a_spec, b_spec],
            out_specs=pl.BlockSpec((tm, tn), lambda i, j, kk: (i, j)),
            out_shape=jax.ShapeDtypeStruct((m, n), F32),
            compiler_params=_cparams(("parallel", "parallel", "arbitrary")),
            name=name,
        )(a, b)
    body, r_in, r_out, r_shape, r_scratch = _riding(body, 2, 1, 0, ride, grid)
    return pl.pallas_call(
        body,
        grid=grid,
        in_specs=[a_spec, b_spec] + r_in,
        out_specs=[pl.BlockSpec((tm, tn), lambda i, j, kk: (i, j))] + r_out,
        out_shape=[jax.ShapeDtypeStruct((m, n), F32)] + r_shape,
        scratch_shapes=r_scratch,
        compiler_params=_cparams(("arbitrary", "arbitrary", "arbitrary")),
        name=name,
    )(a, b, *ride[0])


ROW_BLOCK = 256
ROW_VMEM_BUDGET = 16 * 1024 * 1024


def _pieces(val, pw):
    return [val[:, j * pw:(j + 1) * pw] for j in range(val.shape[1] // pw)]


def _flat(groups):
    arrays, sizes = [], []
    for g in groups:
        g = g if isinstance(g, (tuple, list)) else (g,)
        arrays += list(g)
        sizes.append(len(g))
    return arrays, sizes


def _regroup(refs, sizes):
    out, at = [], 0
    for n in sizes:
        val = refs[at][...]
        for r in refs[at + 1:at + n]:
            val = val + r[...]
        out.append(val)
        at += n
    return out


class _Rowwise:
    def __init__(self, fn, name, row_wpw, par_pw, out_wpw, n_diff=None):
        self.fn, self.name, self.row_wpw, self.par_pw, self.out_wpw = fn, name, row_wpw, par_pw, out_wpw
        self.n_diff = len(row_wpw) if n_diff is None else n_diff

        @jax.custom_vjp
        def call(rows, params):
            return self.fwd(rows, params)

        def call_fwd(rows, params):
            return self.fwd(rows, params), (rows, params)

        def call_bwd(res, douts):
            return self.bwd(res[0], res[1], douts)

        call.defvjp(call_fwd, call_bwd)
        self.call = call

    def _load(self, row_vals, par_refs, br, with_ctx):
        row = pl.program_id(0) * br + lax.broadcasted_iota(jnp.int32, (br, 1), 0)
        is_ctx = (row < (CTX_LEN if with_ctx else 0)).astype(F32)
        rows = [_pieces(v, pw) for v, (_, pw) in zip(row_vals, self.row_wpw)]
        pars = []
        for p, pw in zip(par_refs, self.par_pw):
            val = p[...].astype(F32)
            if p.shape[0] == 2:
                val = is_ctx * val[0:1, :] + (1.0 - is_ctx) * val[1:2, :]
            pars.append(_pieces(val, pw))
        return rows, pars, is_ctx

    def _block_rows(self, n_rows, widths):
        for br in (1088, 1024, 544, 512, 272):
            if n_rows % br == 0 and 2 * 4 * br * sum(widths) <= ROW_VMEM_BUDGET:
                return br
        return ROW_BLOCK

    def _row_specs(self, br, sizes, cols):
        out = []
        for (w, _), n, c in zip(self.row_wpw, sizes, cols):
            out += [pl.BlockSpec((br, w), lambda i, c=c: (i, c))] * n
        return out

    def fwd(self, rows, params, cols=None):
        arrays, sizes = _flat(rows)
        cols = cols or [0] * len(rows)
        n_rows = arrays[0].shape[0]
        n_in = len(arrays)
        br = self._block_rows(n_rows, [w for (w, _), n in zip(self.row_wpw, sizes) for _ in range(n)]
                              + [w for w, _ in self.out_wpw])

        def body(*refs):
            r, p, _ = self._load(_regroup(refs[:n_in], sizes), refs[n_in:n_in + len(params)], br, n_rows == ROWS)
            for o_ref, pieces, (_, pw) in zip(refs[n_in + len(params):], self.fn(r, p), self.out_wpw):
                for j, piece in enumerate(pieces):
                    o_ref[:, j * pw:(j + 1) * pw] = piece

        return pl.pallas_call(
            body,
            grid=(n_rows // br,),
            in_specs=self._row_specs(br, sizes, cols) + [pl.BlockSpec(p.shape, lambda i: (0, 0)) for p in params],
            out_specs=[pl.BlockSpec((br, w), lambda i: (i, 0)) for w, _ in self.out_wpw],
            out_shape=[jax.ShapeDtypeStruct((n_rows, w), F32) for w, _ in self.out_wpw],
            compiler_params=_cparams(("parallel",)),
            name=self.name + "_fwd",
        )(*arrays, *params)

    def bwd(self, rows, params, douts, cols=None, bf16_rows=()):
        arrays, sizes = _flat(rows)
        darrays, dsizes = _flat(douts)
        cols = cols or [0] * len(rows)
        n_rows = arrays[0].shape[0]
        n_in, n_par, n_dout, n_diff = len(arrays), len(params), len(darrays), self.n_diff
        br = self._block_rows(n_rows, [w for (w, _), n in zip(self.row_wpw, sizes) for _ in range(n)]
                              + [w for (w, _), n in zip(self.out_wpw, dsizes) for _ in range(n)]
                              + [w for w, _ in self.row_wpw[:n_diff]])

        def body(*refs):
            par_refs = refs[n_in:n_in + n_par]
            dout_refs = refs[n_in + n_par:n_in + n_par + n_dout]
            drow_refs = refs[n_in + n_par + n_dout:n_in + n_par + n_dout + n_diff]
            dpar_refs = refs[n_in + n_par + n_dout + n_diff:]

            @pl.when(pl.program_id(0) == 0)
            def _():
                for d in dpar_refs:
                    d[...] = jnp.zeros_like(d)

            r, p, is_ctx = self._load(_regroup(refs[:n_in], sizes), par_refs, br, n_rows == ROWS)
            cts = [_pieces(d, pw) for d, (_, pw) in zip(_regroup(dout_refs, dsizes), self.out_wpw)]
            fixed = r[n_diff:]
            _, vjp = jax.vjp(lambda rd, pp: self.fn(rd + fixed, pp), r[:n_diff], p)
            dr, dp = vjp(cts)
            for d_ref, pieces, (_, pw) in zip(drow_refs, dr, self.row_wpw):
                for j, piece in enumerate(pieces):
                    d_ref[:, j * pw:(j + 1) * pw] = piece.astype(d_ref.dtype)
            for d_ref, pieces, pw in zip(dpar_refs, dp, self.par_pw):
                for j, piece in enumerate(pieces):
                    lanes = slice(j * pw, (j + 1) * pw)
                    if d_ref.shape[0] != 2:
                        d_ref[:, lanes] += piece
                    else:
                        d_ref[0:1, lanes] += jnp.sum(is_ctx * piece, axis=0, keepdims=True)
                        d_ref[1:2, lanes] += jnp.sum((1.0 - is_ctx) * piece, axis=0, keepdims=True)

        par_specs = [pl.BlockSpec(p.shape, lambda i: (0, 0)) for p in params]
        dout_specs = []
        for (w, _), n in zip(self.out_wpw, dsizes):
            dout_specs += [pl.BlockSpec((br, w), lambda i: (i, 0))] * n
        drow_w = [w for w, _ in self.row_wpw[:n_diff]]
        g = pl.pallas_call(
            body,
            grid=(n_rows // br,),
            in_specs=self._row_specs(br, sizes, cols) + par_specs + dout_specs,
            out_specs=[pl.BlockSpec((br, w), lambda i: (i, 0)) for w in drow_w] + par_specs,
            out_shape=[jax.ShapeDtypeStruct((n_rows, w), BF16 if a in bf16_rows else F32) for a, w in enumerate(drow_w)]
            + [jax.ShapeDtypeStruct(p.shape, F32) for p in params],
            compiler_params=_cparams(("arbitrary",)),
            name=self.name + "_bwd",
        )(*arrays, *params, *darrays)
        return list(g[:n_diff]), list(g[n_diff:])


def _fn_norm_mod(rows, pars):
    (x,), (nw,), (shift,), (scale,) = rows[0], pars[0], pars[1], pars[2]
    y = x * lax.rsqrt(jnp.mean(x * x, axis=-1, keepdims=True) + EPS) * nw
    return [[y * (1.0 + scale) + shift]]


def _fn_head_rms_gate(rows, pars):
    (w,) = pars[0]
    return [[o * lax.rsqrt(jnp.mean(o * o, axis=-1, keepdims=True) + EPS) * w * _k_silu(z)
             for o, z in zip(rows[0], rows[1])]]


def _fn_group_norm_gate(rows, pars):
    out = []
    for o, z, w in zip(rows[0], rows[1], pars[0]):
        mu = jnp.mean(o, axis=-1, keepdims=True)
        var = jnp.mean(jnp.square(o - mu), axis=-1, keepdims=True)
        out.append((o - mu) * lax.rsqrt(var + EPS) * w * _k_silu(z))
    return [out]


def _fn_gate(rows, pars):
    return [[o * _k_silu(z) for o, z in zip(rows[0], rows[1])]]


def _fn_branch_merge(rows, pars):
    (ya,), (yb,), (yc,), (ma,), (mb,), (mc,) = rows
    (wa,), (wb,), (wc,) = pars
    return [[_k_sigmoid(ma) * _bdot(ya, wa, 1, 0) + _k_sigmoid(mb) * _bdot(yb, wb, 1, 0)
             + _k_sigmoid(mc) * _bdot(yc, wc, 1, 0)]]


def _fn_out_residual(rows, pars):
    (res,), (merged,), (w,), (gate,) = rows[0], rows[1], pars[0], pars[1]
    return [[res + gate * _bdot(merged, w, 1, 0)]]


def _fn_loss(rows, pars):
    (x,), (target,), (w,) = rows[0], rows[1], pars[0]
    y = x * lax.rsqrt(jnp.mean(x * x, axis=-1, keepdims=True) + EPS) * w
    per_row = 0.5 * jnp.mean(jnp.square(y - target), axis=-1, keepdims=True)
    return [[jnp.broadcast_to(per_row, (per_row.shape[0], 128))]]


def _fn_b_rope(rows, pars):
    q, (k, v), (cos,), (sin,) = rows
    rot = lambda x: x * cos + _swap16(x) * sin
    return [[rot(x) for x in q], [rot(k), v]]


def _fn_c_rope(rows, pars):
    q, k, (cos,), (sin,) = rows
    rot = lambda x: x * cos + _swap64(x) * sin
    return [[rot(x) for x in q], [rot(x) * (C_HD ** -0.5) for x in k]]


_norm_mod = _Rowwise(_fn_norm_mod, "norm_mod", [(D_MODEL, D_MODEL)], [D_MODEL] * 3, [(D_MODEL, D_MODEL)])
_out_residual = _Rowwise(_fn_out_residual, "out_residual", [(D_MODEL, D_MODEL)] * 2, [D_MODEL] * 2,
                         [(D_MODEL, D_MODEL)])
_loss_rows = _Rowwise(_fn_loss, "loss", [(D_MODEL, D_MODEL)] * 2, [D_MODEL], [(128, 128)], n_diff=1)
_a_out = _Rowwise(_fn_head_rms_gate, "a_out", [(512, 128)] * 2, [128], [(512, 128)])
_c_out = _Rowwise(_fn_group_norm_gate, "c_out", [(512, 128)] * 2, [128], [(512, 128)])
_b_out = _Rowwise(_fn_gate, "b_out", [(512, 512)] * 2, [], [(512, 512)])
_branch_merge = _Rowwise(_fn_branch_merge, "branch_merge", [(512, 512)] * 3 + [(D_MODEL, D_MODEL)] * 3,
                         [D_MODEL] * 3, [(D_MODEL, D_MODEL)])
_b_rope = _Rowwise(_fn_b_rope, "b_rope", [(512, 128), (256, 128), (128, 128), (128, 128)], [],
                   [(512, 128), (256, 128)], n_diff=2)
_c_rope = _Rowwise(_fn_c_rope, "c_rope", [(512, 128), (512, 128), (128, 128), (128, 128)], [],
                   [(512, 128), (512, 128)], n_diff=2)


HALO = 8
EXT = ROW_BLOCK + 2 * HALO


def _halo_specs(col, width=512):
    last = ROWS // HALO - 1
    per = ROW_BLOCK // HALO
    prev = pl.BlockSpec((HALO, width), lambda i: (jnp.maximum(i * per - 1, 0), col))
    cur = pl.BlockSpec((ROW_BLOCK, width), lambda i: (i, col))
    nxt = pl.BlockSpec((HALO, width), lambda i: (jnp.minimum((i + 1) * per, last), col))
    return [prev, cur, nxt]


def _extended(prev_ref, cur_ref, next_ref):
    i = pl.program_id(0)
    prev_ok = i >= 2
    next_ok = jnp.logical_and(i >= 1, i < ROWS // ROW_BLOCK - 1)
    return jnp.concatenate([jnp.where(prev_ok, prev_ref[...], 0.0), cur_ref[...],
                            jnp.where(next_ok, next_ref[...], 0.0)], axis=0)


def _conv_taps(x_ext, w_ref, flip):
    acc = None
    for j in range(A_CONV):
        shift = (j - 2) if flip else (2 - j)
        term = w_ref[j:j + 1, :] * pltpu.roll(x_ext, shift % EXT, 0)
        acc = term if acc is None else acc + term
    return acc


def _conv_post(pre_pieces, normalize, scale):
    out = []
    for p in pre_pieces:
        y = _k_silu(p)
        if normalize:
            y = y * lax.rsqrt(jnp.sum(y * y, axis=-1, keepdims=True) + EPS) * scale
        out.append(y)
    return out


def _a_prep_fwd(proj, conv8, col, normalize, scale, name):
    def body(prev_ref, cur_ref, next_ref, w_ref, o_ref):
        pre = _conv_taps(_extended(prev_ref, cur_ref, next_ref), w_ref, False)[HALO:HALO + ROW_BLOCK]
        for h, y in enumerate(_conv_post(_pieces(pre, 128), normalize, scale)):
            o_ref[:, h * 128:(h + 1) * 128] = y

    return pl.pallas_call(
        body,
        grid=(ROWS // ROW_BLOCK,),
        in_specs=_halo_specs(col) + [pl.BlockSpec((8, 512), lambda i: (0, col))],
        out_specs=pl.BlockSpec((ROW_BLOCK, 512), lambda i: (i, 0)),
        out_shape=jax.ShapeDtypeStruct((ROWS, 512), F32),
        compiler_params=_cparams(("parallel",)),
        name=name + "_fwd",
    )(proj, proj, proj, conv8)


def _a_prep_bwd(proj, conv8, col, normalize, scale, dout_f, dout_r, name):
    def body(xp, xc, xn, w_ref, fp, fc, fn_, rp, rc, rn, dx_ref, dw_ref):
        @pl.when(pl.program_id(0) == 0)
        def _():
            dw_ref[...] = jnp.zeros_like(dw_ref)

        x_ext = _extended(xp, xc, xn)
        dout = _extended(fp, fc, fn_) + _extended(rp, rc, rn)
        pre = _conv_taps(x_ext, w_ref, False)
        _, vjp = jax.vjp(lambda p: _conv_post(p, normalize, scale), _pieces(pre, 128))
        (dpre,) = vjp(_pieces(dout, 128))
        dpre = jnp.concatenate(dpre, axis=1)
        dx_ref[...] = _conv_taps(dpre, w_ref, True)[HALO:HALO + ROW_BLOCK].astype(BF16)
        own = dpre[HALO:HALO + ROW_BLOCK]
        for j in range(A_CONV):
            shifted = pltpu.roll(x_ext, (2 - j) % EXT, 0)[HALO:HALO + ROW_BLOCK]
            dw_ref[j:j + 1, :] += jnp.sum(own * shifted, axis=0, keepdims=True)

    return pl.pallas_call(
        body,
        grid=(ROWS // ROW_BLOCK,),
        in_specs=_halo_specs(col) + [pl.BlockSpec((8, 512), lambda i: (0, col))] + _halo_specs(0) + _halo_specs(0),
        out_specs=[pl.BlockSpec((ROW_BLOCK, 512), lambda i: (i, 0)), pl.BlockSpec((8, 512), lambda i: (0, 0))],
        out_shape=[jax.ShapeDtypeStruct((ROWS, 512), BF16), jax.ShapeDtypeStruct((8, 512), F32)],
        compiler_params=_cparams(("arbitrary",)),
        name=name + "_bwd",
    )(proj, proj, proj, conv8, dout_f, dout_f, dout_f, dout_r, dout_r, dout_r)


N_CHAIN = 8


def _rev_chunk(s):
    return jnp.where(s < N_CTX_CHUNK, N_CTX_CHUNK - 1 - s, N_CHUNK + N_CTX_CHUNK - 1 - s)


def _scan_specs(step_of, v_col=0):
    cf = step_of
    cr = lambda n: _rev_chunk(step_of(n))

    def pair(shape, index):
        return (pl.BlockSpec(shape, lambda n: index(cf(n))), pl.BlockSpec(shape, lambda n: index(cr(n))))

    return dict(
        tok=pair((CHUNK, 512), lambda c: (c, 0)),
        tokv=pair((CHUNK, 512), lambda c: (c, v_col)),
        col=pair((4, CHUNK, 1), lambda c: (0, c, 0)),
        row=pair((4, 1, 1, CHUNK), lambda c: (0, c, 0, 0)),
        one=pair((4, 1, 1, 1), lambda c: (0, c, 0, 0)),
        state=pair((None, 4, 128, 128), lambda c: (c, 0, 0, 0)),
        tinv=pair((None, 4, CHUNK, CHUNK), lambda c: (c, 0, 0, 0)),
    )


def _both(specs, kinds):
    out = []
    for kind in kinds:
        out += list(specs[kind])
    return out


def _scan_call(body, name, in_specs, out_specs, out_shape, operands, ride):
    body, r_in, r_out, r_shape, r_scratch = _riding(body, len(in_specs), len(out_specs), 1, ride, (N_CHUNK,))
    return pl.pallas_call(
        body,
        grid=(N_CHUNK,),
        in_specs=in_specs + r_in,
        out_specs=out_specs + r_out,
        out_shape=out_shape + r_shape,
        scratch_shapes=[pltpu.VMEM((N_CHAIN, 128, 128), F32)] + r_scratch,
        compiler_params=_cparams(("arbitrary",)),
        name=name,
    )(*operands, *(ride[0] if ride else []))


def _chain_masks():
    ii = lax.broadcasted_iota(jnp.int32, (CHUNK, CHUNK), 0)
    jj = lax.broadcasted_iota(jnp.int32, (CHUNK, CHUNK), 1)
    eye = jnp.where(ii == jj, 1.0, 0.0).astype(F32)
    lower = (ii >= jj, ii > jj)
    upper = (ii <= jj, ii < jj)
    return [lower] * 4 + [upper] * 4, eye


def _tri_inv_all(ls, eye):
    xs = [eye - l for l in ls]
    ps = [_hdot(l, l) for l in ls]
    for i in range(5):
        xs = [x + _hdot(x, p) for x, p in zip(xs, ps)]
        if i < 4:
            ps = [_hdot(p, p) for p in ps]
    return xs


@jax.custom_vjp
def _inv_saved(l, x):
    return x


def _inv_saved_fwd(l, x):
    return x, x


def _inv_saved_bwd(x, dx):
    return -_bdot(x, _bdot(dx, x, 1, 1), 0, 0), jnp.zeros_like(x)


_inv_saved.defvjp(_inv_saved_fwd, _inv_saved_bwd)


def _delta_chains(q, k, v, beta_r, gcr, gl, s, masks, eye, tinv_saved):
    n = range(len(q))
    beta = [jnp.sum(eye * beta_r[i], axis=1, keepdims=True) for i in n]
    gcc = [jnp.sum(eye * gcr[i], axis=1, keepdims=True) for i in n]
    decay = [jnp.exp(jnp.where(masks[i][0], gcc[i] - gcr[i], NEG)) for i in n]
    kb = [k[i] * beta[i] for i in n]
    lmat = [jnp.where(masks[i][1], _bdot(kb[i], k[i], 1, 1) * decay[i], 0.0) for i in n]
    if tinv_saved is None:
        tinv = _tri_inv_all(lmat, eye)
    else:
        tinv = [_inv_saved(lmat[i], tinv_saved[i]) for i in n]
    eg = [jnp.exp(gcc[i]) for i in n]
    u = [_bdot(tinv[i], v[i] * beta[i], 1, 0) for i in n]
    w = [_bdot(tinv[i], kb[i] * eg[i], 1, 0) for i in n]
    qk = [_bdot(q[i], k[i], 1, 1) * decay[i] for i in n]
    v_new = [u[i] - _bdot(w[i], s[i], 1, 0) for i in n]
    o = [_bdot(q[i] * eg[i], s[i], 1, 0) + _bdot(qk[i], v_new[i], 1, 0) for i in n]
    s_new = [s[i] * jnp.exp(gl[i]) + _bdot(k[i] * jnp.exp(gl[i] - gcc[i]), v_new[i], 0, 0) for i in n]
    return (o, s_new), tinv


def _chain_loads(tok_pairs, small_pairs):
    toks = [[pair[i // 4][:, (i % 4) * 128:(i % 4 + 1) * 128] for i in range(N_CHAIN)] for pair in tok_pairs]
    smalls = [[pair[i // 4][i % 4] for i in range(N_CHAIN)] for pair in small_pairs]
    return toks, smalls


def _delta_fwd_call(q, k, v, beta, gc, gl, ride=None):
    sp = _scan_specs(lambda n: n)

    def body(qf, qr, kf, kr, vf, vr, bf, br, gcrf, gcrr, glf, glr, of, orv, ssf, ssr, tsf, tsr, s_scr):
        @pl.when(pl.program_id(0) == 0)
        def _():
            s_scr[...] = jnp.zeros_like(s_scr)

        masks, eye = _chain_masks()
        (qs, ks, vs), _ = _chain_loads([(qf, qr), (kf, kr), (vf, vr)], [])
        bs = [(bf, br)[i // 4][i % 4, 0] for i in range(N_CHAIN)]
        gcrs = [(gcrf, gcrr)[i // 4][i % 4, 0] for i in range(N_CHAIN)]
        gls = [(glf, glr)[i // 4][i % 4, 0] for i in range(N_CHAIN)]
        ss = [s_scr[i] for i in range(N_CHAIN)]
        (o, s_new), tinv = _delta_chains(qs, ks, vs, bs, gcrs, gls, ss, masks, eye, None)
        for i in range(N_CHAIN):
            d, h = i // 4, i % 4
            (ssf, ssr)[d][h] = ss[i]
            (tsf, tsr)[d][h] = tinv[i]
            (of, orv)[d][:, h * 128:(h + 1) * 128] = o[i]
            s_scr[i] = s_new[i]

    return _scan_call(
        body, "delta_fwd",
        _both(sp, ["tok", "tok", "tok", "row", "row", "one"]),
        _both(sp, ["tok", "state", "tinv"]),
        [jax.ShapeDtypeStruct((ROWS, 512), F32)] * 2 + [jax.ShapeDtypeStruct((N_CHUNK, 4, 128, 128), F32)] * 2
        + [jax.ShapeDtypeStruct((N_CHUNK, 4, CHUNK, CHUNK), F32)] * 2,
        [q, q, k, k, v, v, *beta, *gc, *gl], ride)


def _delta_bwd_call(q, k, v, beta, gc, gl, ssave, tsave, do, ride=None):
    sp = _scan_specs(lambda n: N_CHUNK - 1 - n)

    def body(qf, qr, kf, kr, vf, vr, bf, br, gcrf, gcrr, glf, glr, ssf, ssr, tsf, tsr, dof, dor,
             dqf, dqr, dkf, dkr, dvf, dvr, dbf, dbr, dgcrf, dgcrr, dglf, dglr, ds_scr):
        @pl.when(pl.program_id(0) == 0)
        def _():
            ds_scr[...] = jnp.zeros_like(ds_scr)

        masks, eye = _chain_masks()
        (qs, ks, vs, dos), (ss, ts) = _chain_loads(
            [(qf, qr), (kf, kr), (vf, vr), (dof, dor)], [(ssf, ssr), (tsf, tsr)])
        bs = [(bf, br)[i // 4][i % 4, 0] for i in range(N_CHAIN)]
        gcrs = [(gcrf, gcrr)[i // 4][i % 4, 0] for i in range(N_CHAIN)]
        gls = [(glf, glr)[i // 4][i % 4, 0] for i in range(N_CHAIN)]
        fn = lambda *a: _delta_chains(*a, masks, eye, ts)
        _, vjp, _ = jax.vjp(fn, qs, ks, vs, bs, gcrs, gls, ss, has_aux=True)
        dq, dk, dv, db, dgcr, dgl, ds = vjp((dos, [ds_scr[i] for i in range(N_CHAIN)]))
        for i in range(N_CHAIN):
            d, h = i // 4, i % 4
            hs = slice(h * 128, (h + 1) * 128)
            (dqf, dqr)[d][:, hs] = dq[i]
            (dkf, dkr)[d][:, hs] = dk[i]
            (dvf, dvr)[d][:, hs] = dv[i]
            (dbf, dbr)[d][h, 0] = db[i]
            (dgcrf, dgcrr)[d][h, 0] = dgcr[i]
            (dglf, dglr)[d][h, 0] = dgl[i]
            ds_scr[i] = ds[i]

    tok = jax.ShapeDtypeStruct((ROWS, 512), F32)
    return _scan_call(
        body, "delta_bwd",
        _both(sp, ["tok", "tok", "tok", "row", "row", "one", "state", "tinv", "tok"]),
        _both(sp, ["tok", "tok", "tok", "row", "row", "one"]),
        [tok] * 6 + [jax.ShapeDtypeStruct((4, N_CHUNK, 1, CHUNK), F32)] * 4
        + [jax.ShapeDtypeStruct((4, N_CHUNK, 1, 1), F32)] * 2,
        [q, q, k, k, v, v, *beta, *gc, *gl, *ssave, *tsave, do, do], ride)


def _ret_chains(q, k, v, dm, qs, ks, cd, s):
    n = range(len(q))
    a = [_bdot(q[i], k[i], 1, 1) * dm[i] for i in n]
    o = [_bdot(a[i], v[i], 1, 0) + _bdot(q[i] * qs[i], s[i], 1, 0) for i in n]
    s_new = [s[i] * cd[i] + _bdot(k[i] * ks[i], v[i], 0, 0) for i in n]
    return o, s_new


def _ret_const_specs():
    return [pl.BlockSpec((N_CHAIN, CHUNK, CHUNK), lambda n: (0, 0, 0)), pl.BlockSpec((N_CHAIN, CHUNK, 1), lambda n: (0, 0, 0)),
            pl.BlockSpec((N_CHAIN, CHUNK, 1), lambda n: (0, 0, 0)), pl.BlockSpec((N_CHAIN, 1, 1), lambda n: (0, 0, 0))]


def _ret_fwd_call(q, k, v, v_col, dm, qs, ks, cd, ride=None):
    sp = _scan_specs(lambda n: n, v_col)

    def body(qf, qr, kf, kr, vf, vr, dm_ref, qs_ref, ks_ref, cd_ref, of, orv, ssf, ssr, s_scr):
        @pl.when(pl.program_id(0) == 0)
        def _():
            s_scr[...] = jnp.zeros_like(s_scr)

        (qc, kc, vc), _ = _chain_loads([(qf, qr), (kf, kr), (vf, vr)], [])
        ss = [s_scr[i] for i in range(N_CHAIN)]
        consts = [[r[i] for i in range(N_CHAIN)] for r in (dm_ref, qs_ref, ks_ref, cd_ref)]
        o, s_new = _ret_chains(qc, kc, vc, *consts, ss)
        for i in range(N_CHAIN):
            d, h = i // 4, i % 4
            (ssf, ssr)[d][h] = ss[i]
            (of, orv)[d][:, h * 128:(h + 1) * 128] = o[i]
            s_scr[i] = s_new[i]

    return _scan_call(
        body, "ret_fwd",
        _both(sp, ["tok", "tok", "tokv"]) + _ret_const_specs(),
        _both(sp, ["tok", "state"]),
        [jax.ShapeDtypeStruct((ROWS, 512), F32)] * 2 + [jax.ShapeDtypeStruct((N_CHUNK, 4, 128, 128), F32)] * 2,
        [q, q, k, k, v, v, dm, qs, ks, cd], ride)


def _ret_bwd_call(q, k, v, v_col, dm, qs, ks, cd, ssave, do, ride=None):
    sp = _scan_specs(lambda n: N_CHUNK - 1 - n, v_col)

    def body(qf, qr, kf, kr, vf, vr, dm_ref, qs_ref, ks_ref, cd_ref, ssf, ssr, dof, dor,
             dqf, dqr, dkf, dkr, dvf, dvr, ddm_ref, dqs_ref, dks_ref, dcd_ref, ds_scr):
        @pl.when(pl.program_id(0) == 0)
        def _():
            ds_scr[...] = jnp.zeros_like(ds_scr)
            ddm_ref[...] = jnp.zeros_like(ddm_ref)
            dqs_ref[...] = jnp.zeros_like(dqs_ref)
            dks_ref[...] = jnp.zeros_like(dks_ref)
            dcd_ref[...] = jnp.zeros_like(dcd_ref)

        (qc, kc, vc, dos), (ss,) = _chain_loads([(qf, qr), (kf, kr), (vf, vr), (dof, dor)], [(ssf, ssr)])
        consts = [[r[i] for i in range(N_CHAIN)] for r in (dm_ref, qs_ref, ks_ref, cd_ref)]
        _, vjp = jax.vjp(_ret_chains, qc, kc, vc, *consts, ss)
        dq, dk, dv, ddm, dqs, dks, dcd, ds = vjp((dos, [ds_scr[i] for i in range(N_CHAIN)]))
        for i in range(N_CHAIN):
            d, h = i // 4, i % 4
            hs = slice(h * 128, (h + 1) * 128)
            (dqf, dqr)[d][:, hs] = dq[i]
            (dkf, dkr)[d][:, hs] = dk[i]
            (dvf, dvr)[d][:, hs] = dv[i]
            ddm_ref[i] += ddm[i]
            dqs_ref[i] += dqs[i]
            dks_ref[i] += dks[i]
            dcd_ref[i] += dcd[i]
            ds_scr[i] = ds[i]

    tok = jax.ShapeDtypeStruct((ROWS, 512), F32)
    return _scan_call(
        body, "ret_bwd",
        _both(sp, ["tok", "tok", "tokv"]) + _ret_const_specs() + _both(sp, ["state", "tok"]),
        _both(sp, ["tok", "tok", "tok"]) + _ret_const_specs(),
        [tok] * 6 + [jax.ShapeDtypeStruct((N_CHAIN, CHUNK, CHUNK), F32), jax.ShapeDtypeStruct((N_CHAIN, CHUNK, 1), F32),
                     jax.ShapeDtypeStruct((N_CHAIN, CHUNK, 1), F32), jax.ShapeDtypeStruct((N_CHAIN, 1, 1), F32)],
        [q, q, k, k, v, v, dm, qs, ks, cd, *ssave, do, do], ride)


N_QBLK = ROWS // B_BLOCK
CTX_QBLK = CTX_LEN // B_BLOCK


def _attn_heads(q, kc, vc, kw, vw, sink, valid):
    n = range(len(q))
    scale = B_HD ** -0.5
    s_c = [_bdot(q[i], kc[i], 1, 1) * scale for i in n]
    s_w = [jnp.where(valid, _bdot(q[i], kw[i], 1, 1) * scale, NEG) for i in n]
    m = [lax.stop_gradient(jnp.maximum(jnp.maximum(jnp.max(s_c[i], axis=-1, keepdims=True), sink[i]),
                                       jnp.max(s_w[i], axis=-1, keepdims=True))) for i in n]
    e_c = [jnp.exp(s_c[i] - m[i]) for i in n]
    e_w = [jnp.exp(s_w[i] - m[i]) for i in n]
    den = [jnp.sum(e_c[i], axis=-1, keepdims=True) + jnp.sum(e_w[i], axis=-1, keepdims=True)
           + jnp.exp(sink[i] - m[i]) for i in n]
    return [(_bdot(e_c[i], vc[i], 1, 0) + _bdot(e_w[i], vw[i], 1, 0)) / den[i] for i in n]


def _attn_loads(q_ref, kv_ref, sink_ref, start):
    q, kc, vc, kw, vw, sink = [], [], [], [], [], []
    for hk in range(B_KV_HEADS):
        ks = slice(hk * B_HD, (hk + 1) * B_HD)
        vs = slice(128 + hk * B_HD, 128 + (hk + 1) * B_HD)
        grp = (kv_ref[0:CTX_LEN, ks], kv_ref[0:CTX_LEN, vs],
               kv_ref[pl.ds(start, 3 * B_BLOCK), ks], kv_ref[pl.ds(start, 3 * B_BLOCK), vs])
        for g in range(4):
            h = hk * 4 + g
            q.append(q_ref[:, h * B_HD:(h + 1) * B_HD])
            for lst, val in zip((kc, vc, kw, vw), grp):
                lst.append(val)
            sink.append(jnp.full((1, 1), sink_ref[h], F32))
    return q, kc, vc, kw, vw, sink


def _window(blk):
    xblk = blk - CTX_QBLK
    first = jnp.clip((xblk - 1) * B_BLOCK, 0, SEQ - 3 * B_BLOCK)
    qpos = xblk * B_BLOCK + lax.broadcasted_iota(jnp.int32, (B_BLOCK, 3 * B_BLOCK), 0)
    kpos = first + lax.broadcasted_iota(jnp.int32, (B_BLOCK, 3 * B_BLOCK), 1)
    far = jnp.where(blk >= CTX_QBLK, 0, 2 * SEQ)
    valid = jnp.abs(kpos - qpos) + far <= WINDOW
    return pl.multiple_of(first + CTX_LEN, B_BLOCK), valid


def _attn_specs():
    qspec = pl.BlockSpec((B_BLOCK, 512), lambda i: (i, 0))
    kvspec = pl.BlockSpec((ROWS, 256), lambda i: (0, 0))
    return qspec, kvspec, pl.BlockSpec(memory_space=pltpu.SMEM)


def _attn_fwd_call(q, kv, sink, ride=None):
    def body(q_ref, kv_ref, sink_ref, o_ref):
        start, valid = _window(pl.program_id(0))
        out = _attn_heads(*_attn_loads(q_ref, kv_ref, sink_ref, start), valid)
        for h in range(B_Q_HEADS):
            o_ref[:, h * B_HD:(h + 1) * B_HD] = out[h]

    qspec, kvspec, sspec = _attn_specs()
    body, r_in, r_out, r_shape, r_scratch = _riding(body, 3, 1, 0, ride, (N_QBLK,))
    return pl.pallas_call(
        body,
        grid=(N_QBLK,),
        in_specs=[qspec, kvspec, sspec] + r_in,
        out_specs=[qspec] + r_out,
        out_shape=[jax.ShapeDtypeStruct((ROWS, 512), F32)] + r_shape,
        scratch_shapes=r_scratch,
        compiler_params=_cparams(("arbitrary",)),
        name="attn_fwd",
    )(q, kv, sink, *(ride[0] if ride else []))


def _attn_bwd_call(q, kv, sink, do, ride=None):
    def body(q_ref, kv_ref, sink_ref, do_ref, dq_ref, dkv_ref, dsink_ref):
        @pl.when(pl.program_id(0) == 0)
        def _():
            dkv_ref[...] = jnp.zeros_like(dkv_ref)
            dsink_ref[...] = jnp.zeros_like(dsink_ref)

        start, valid = _window(pl.program_id(0))
        _, vjp = jax.vjp(functools.partial(_attn_heads, valid=valid), *_attn_loads(q_ref, kv_ref, sink_ref, start))
        dq, dkc, dvc, dkw, dvw, dsink = vjp([do_ref[:, h * B_HD:(h + 1) * B_HD] for h in range(B_Q_HEADS)])
        for h in range(B_Q_HEADS):
            dq_ref[:, h * B_HD:(h + 1) * B_HD] = dq[h]
            dsink_ref[h:h + 1, :] += jnp.broadcast_to(dsink[h], (1, 128))
        for hk in range(B_KV_HEADS):
            ks = slice(hk * B_HD, (hk + 1) * B_HD)
            vs = slice(128 + hk * B_HD, 128 + (hk + 1) * B_HD)
            grp = lambda parts: parts[hk * 4] + parts[hk * 4 + 1] + parts[hk * 4 + 2] + parts[hk * 4 + 3]
            dkv_ref[0:CTX_LEN, ks] += grp(dkc)
            dkv_ref[0:CTX_LEN, vs] += grp(dvc)
            dkv_ref[pl.ds(start, 3 * B_BLOCK), ks] += grp(dkw)
            dkv_ref[pl.ds(start, 3 * B_BLOCK), vs] += grp(dvw)

    qspec, kvspec, sspec = _attn_specs()
    body, r_in, r_out, r_shape, r_scratch = _riding(body, 4, 3, 0, ride, (N_QBLK,))
    return pl.pallas_call(
        body,
        grid=(N_QBLK,),
        in_specs=[qspec, kvspec, sspec, qspec] + r_in,
        out_specs=[qspec, kvspec, pl.BlockSpec((8, 128), lambda i: (0, 0))] + r_out,
        out_shape=[jax.ShapeDtypeStruct((ROWS, 512), F32), jax.ShapeDtypeStruct((ROWS, 256), F32),
                   jax.ShapeDtypeStruct((8, 128), F32)] + r_shape,
        scratch_shapes=r_scratch,
        compiler_params=_cparams(("arbitrary",)),
        name="attn_bwd",
    )(q, kv, sink, do, *(ride[0] if ride else []))


def _my_id():
    return 4 * lax.axis_index("x") + 2 * lax.axis_index("y") + lax.axis_index("c")


def _peer(k):
    x, y, c = lax.axis_index("x"), lax.axis_index("y"), lax.axis_index("c")
    return (1 - x if k & 4 else x, 1 - y if k & 2 else y, 1 - c if k & 1 else c)


def _exchange_copies(ins, outs, sems, gather):
    send_sems, recv_sems, local_sems = sems
    me = _my_id()
    own, remote = [], []
    for a in range(len(ins)):
        own.append(pltpu.make_async_copy(ins[a] if gather else ins[a].at[me], outs[a].at[me], local_sems.at[a]))
        for k in range(1, N_DEV):
            peer_slot = jnp.bitwise_xor(me, k)
            src = ins[a] if gather else ins[a].at[peer_slot]
            common = dict(src_ref=src, send_sem=send_sems.at[a, k - 1], recv_sem=recv_sems.at[a, k - 1],
                          device_id=_peer(k), device_id_type=MESH)
            remote.append((pltpu.make_async_remote_copy(dst_ref=outs[a].at[me], **common),
                           pltpu.make_async_remote_copy(dst_ref=outs[a].at[peer_slot], **common)))
    return own, remote


def _exchange_start(ins, outs, sems, gather):
    own, remote = _exchange_copies(ins, outs, sems, gather)
    for cp in own:
        cp.start()
    for send, _ in remote:
        send.start()


def _exchange_wait(ins, outs, sems, gather):
    own, remote = _exchange_copies(ins, outs, sems, gather)
    for _, arrival in remote:
        arrival.wait_recv()
    for send, _ in remote:
        send.wait_send()
    for cp in own:
        cp.wait()


def _exchange_plumbing(arrays, gather):
    n = len(arrays)
    hbm = [pl.BlockSpec(memory_space=pltpu.HBM)] * n
    out_shape = [jax.ShapeDtypeStruct((N_DEV,) + (a.shape if gather else a.shape[1:]), a.dtype) for a in arrays]
    sems = [pltpu.SemaphoreType.DMA((n, N_DEV - 1)), pltpu.SemaphoreType.DMA((n, N_DEV - 1)),
            pltpu.SemaphoreType.DMA((n,))]
    return hbm, out_shape, sems


def _exchange(arrays, gather, name):
    n = len(arrays)

    def body(*refs):
        ins, outs, sems = refs[:n], refs[n:2 * n], refs[2 * n:]
        _exchange_start(ins, outs, sems, gather)
        _exchange_wait(ins, outs, sems, gather)

    hbm, out_shape, sems = _exchange_plumbing(arrays, gather)
    return pl.pallas_call(
        body,
        in_specs=hbm,
        out_specs=hbm,
        out_shape=out_shape,
        scratch_shapes=sems,
        compiler_params=pltpu.CompilerParams(has_side_effects=True),
        name=name,
    )(*arrays)


def _riding(body, n_in, n_out, n_scratch, ride, grid):
    if ride is None:
        return body, [], [], [], []
    arrays, gather = ride
    n = len(arrays)

    def at(step_of):
        hit = pl.program_id(0) == step_of(grid[0])
        for d in range(1, len(grid)):
            hit = jnp.logical_and(hit, pl.program_id(d) == step_of(grid[d]))
        return hit

    def wrapped(*refs):
        ins, rin = refs[:n_in], refs[n_in:n_in + n]
        outs = refs[n_in + n:n_in + n + n_out]
        rout = refs[n_in + n + n_out:n_in + 2 * n + n_out]
        scratch = refs[n_in + 2 * n + n_out:n_in + 2 * n + n_out + n_scratch]
        sems = refs[n_in + 2 * n + n_out + n_scratch:]

        @pl.when(at(lambda size: 0))
        def _():
            _exchange_start(rin, rout, sems, gather)

        body(*ins, *outs, *scratch)

        @pl.when(at(lambda size: size - 1))
        def _():
            _exchange_wait(rin, rout, sems, gather)

    hbm, out_shape, sems = _exchange_plumbing(arrays, gather)
    return wrapped, hbm, hbm, out_shape, sems


def _sum_contributions(c_ref):
    g = c_ref[0].astype(F32)
    for j in range(1, N_DEV):
        g = g + c_ref[j].astype(F32)
    return g


def _adamw_update(g, w_ref, m_ref, v_ref, g_ref, d_ref, nm_ref, nv_ref):
    m_new = ADAM_B1 * m_ref[...] + (1.0 - ADAM_B1) * g
    v_new = ADAM_B2 * v_ref[...] + (1.0 - ADAM_B2) * (g * g)
    m_hat = m_new / (1.0 - ADAM_B1 ** ADAM_STEP)
    v_hat = v_new / (1.0 - ADAM_B2 ** ADAM_STEP)
    g_ref[...] = g
    d_ref[...] = -ADAM_LR * (m_hat / (jnp.sqrt(v_hat) + ADAM_EPS) + ADAM_WD * w_ref[...])
    nm_ref[...] = m_new
    nv_ref[...] = v_new


def _adamw_layers(w, m, v, contrib0, contrib1, name):
    _, r, c = w.shape
    br = _pick(r, (256, 128, 64, 32, 16, 8))
    nb = r // br

    def body(w_ref, m_ref, v_ref, c0_ref, c1_ref, g_ref, d_ref, nm_ref, nv_ref):
        g = jnp.where(pl.program_id(0) == 0, _sum_contributions(c0_ref), _sum_contributions(c1_ref))
        _adamw_update(g, w_ref, m_ref, v_ref, g_ref, d_ref, nm_ref, nv_ref)

    spec = pl.BlockSpec((None, br, c), lambda l, i: (l, i, 0))
    return pl.pallas_call(
        body,
        grid=(DEPTH, nb),
        in_specs=[spec, spec, spec,
                  pl.BlockSpec((N_DEV, br, c), lambda l, i: (0, jnp.where(l == 0, i, nb - 1), 0)),
                  pl.BlockSpec((N_DEV, br, c), lambda l, i: (0, jnp.where(l == 1, i, 0), 0))],
        out_specs=[spec] * 4,
        out_shape=[jax.ShapeDtypeStruct(w.shape, F32)] * 4,
        compiler_params=_cparams(("arbitrary", "arbitrary")),
        name=name,
    )(w, m, v, contrib0, contrib1)


def _adamw(w, m, v, contrib, name):
    r, c = w.shape
    br = _pick(r, (256, 128, 64, 32, 16, 8))

    def body(w_ref, m_ref, v_ref, c_ref, g_ref, d_ref, nm_ref, nv_ref):
        _adamw_update(_sum_contributions(c_ref), w_ref, m_ref, v_ref, g_ref, d_ref, nm_ref, nv_ref)

    spec = pl.BlockSpec((br, c), lambda i: (i, 0))
    cspec = pl.BlockSpec((N_DEV, br, c), lambda i: (0, i, 0))
    return pl.pallas_call(
        body,
        grid=(r // br,),
        in_specs=[spec, spec, spec, cspec],
        out_specs=[spec] * 4,
        out_shape=[jax.ShapeDtypeStruct((r, c), F32)] * 4,
        compiler_params=_cparams(("parallel",)),
        name=name,
    )(w, m, v, contrib)


def _silu(x):
    return x * jax.nn.sigmoid(x)


def _rope_angles(pos, n_freq):
    inv = ROPE_BASE ** (-jnp.arange(n_freq, dtype=F32) / n_freq)
    return pos[:, None] * inv[None, :]


def _with_ctx_rows(cos, sin):
    return (jnp.concatenate([jnp.ones((CTX_LEN, 128), F32), cos], axis=0),
            jnp.concatenate([jnp.zeros((CTX_LEN, 128), F32), sin], axis=0))


def _rope_tables():
    rows_n = SEQ // GRID_W
    rows = jnp.repeat(jnp.arange(rows_n, dtype=F32), GRID_W)
    cols = jnp.tile(jnp.arange(GRID_W, dtype=F32), rows_n)
    ang_r = _rope_angles(rows, B_HD // 4)
    ang_c = _rope_angles(cols, B_HD // 4)
    cos_b = jnp.tile(jnp.concatenate([jnp.cos(ang_r)] * 2 + [jnp.cos(ang_c)] * 2, axis=1), (1, 2))
    sin_b = jnp.tile(jnp.concatenate([-jnp.sin(ang_r), jnp.sin(ang_r), -jnp.sin(ang_c), jnp.sin(ang_c)], axis=1), (1, 2))
    ang = _rope_angles(jnp.arange(SEQ, dtype=F32), C_HD // 2)
    cos_c = jnp.concatenate([jnp.cos(ang)] * 2, axis=1)
    sin_c = jnp.concatenate([-jnp.sin(ang), jnp.sin(ang)], axis=1)
    return _with_ctx_rows(cos_b, sin_b), _with_ctx_rows(cos_c, sin_c)


def _halves(a):
    return a[:4], a[4:]


def _delta_gates(ab, a_log, dt_bias):
    beta = jax.nn.sigmoid(ab[:, :8])
    g = -jnp.exp(a_log)[None, :] * jax.nn.softplus(ab[:, 8:] + dt_bias[None, :])
    gch = g.reshape(N_CHUNK, CHUNK, 8)
    tri = jnp.tril(jnp.ones((CHUNK, CHUNK), F32))
    fwd = jnp.einsum("ij,cjh->cih", tri, gch[..., :4], precision=HIGHEST)
    bwd = jnp.einsum("ji,cjh->cih", tri, gch[..., 4:], precision=HIGHEST)
    gc = jnp.concatenate([fwd, bwd], axis=-1)
    gl = jnp.sum(gch, axis=1)
    rows = lambda a: _halves(a.transpose(2, 0, 1)[:, :, None, :])
    return rows(beta.reshape(N_CHUNK, CHUNK, 8)), rows(gc), _halves(gl.T[:, :, None, None])


def _ret_consts(c_decay):
    lg = jax.nn.log_sigmoid(c_decay)
    idx = jnp.arange(CHUNK, dtype=F32)
    diff = idx[:, None] - idx[None, :]
    lgf, lgb = lg[:4, None, None], lg[4:, None, None]
    dm = jnp.concatenate([jnp.exp(jnp.where(diff >= 0, diff * lgf, -jnp.inf)),
                          jnp.exp(jnp.where(diff <= 0, -diff * lgb, -jnp.inf))], axis=0)
    qs = jnp.concatenate([jnp.exp((idx + 1.0)[None, :] * lg[:4, None]),
                          jnp.exp((CHUNK - idx)[None, :] * lg[4:, None])], axis=0)[:, :, None]
    ks = jnp.concatenate([jnp.exp((CHUNK - 1.0 - idx)[None, :] * lg[:4, None]),
                          jnp.exp(idx[None, :] * lg[4:, None])], axis=0)[:, :, None]
    return dm, qs, ks, jnp.exp(CHUNK * lg)[:, None, None]


A_PIECES = ((0, True, A_DK ** -0.5, "a_q"), (1, True, 1.0, "a_k"), (2, False, 1.0, "a_v"))
B_ROPE_COLS = [C_BQ // 512, C_BKV // 256, 0, 0]
C_ROPE_COLS = [C_CQ // 512, C_CK // 512, 0, 0]
MERGE_COLS = [0, 0, 0, C_MERGE // 1024, C_MERGE // 1024 + 1, C_MERGE // 1024 + 2]


def _conv8(conv_w):
    return jnp.pad(conv_w, ((0, 8 - A_CONV), (0, 0)))


W_IN_TILES = {"nn": (2176, 512, 1024), "nt": (1088, 1024, 2176), "db": (1024, 512, ROWS)}


def _core_forward(h, w16, p, rides):
    res = _matmul(h, w16, "w_in", "nn", W_IN_TILES["nn"], ride=rides.get("w_in"))
    proj, rode = (res[0], {"w_in": res[1:]}) if "w_in" in rides else (res, {})
    wb = p["w_branch"] if "w_in" not in rides else _unshard_layer("w_branch", rode["w_in"][0])
    (cos_b, sin_b), (cos_c, sin_c) = _rope_tables()
    conv8 = _conv8(p["a_conv_w"])
    q, k, v = [_a_prep_fwd(proj, conv8, col, nrm, scl, nm) for col, nrm, scl, nm in A_PIECES]
    gates = _delta_gates(proj[:, C_AB:C_AB + 16], p["a_log"], p["a_dt_bias"])
    res = _delta_fwd_call(q, k, v, *gates, ride=rides.get("delta"))
    (of, orv, ssf, ssr, tsf, tsr), rode["delta"] = res[:6], res[6:]
    (y_a,) = _a_out.fwd([(of, orv), proj], [p["a_norm_w"][None, :]], [0, C_AZ // 512])

    qb, kvb = _b_rope.fwd([proj, proj, cos_b, sin_b], [], B_ROPE_COLS)
    res = _attn_fwd_call(qb, kvb, p["b_sink"], ride=rides.get("attn"))
    ob, rode["attn"] = res[0], res[1:]
    (y_b,) = _b_out.fwd([ob, proj], [], [0, C_BZ // 512])

    qc, kc = _c_rope.fwd([proj, proj, cos_c, sin_c], [], C_ROPE_COLS)
    res = _ret_fwd_call(qc, kc, proj, C_CV // 512, *_ret_consts(p["c_decay"]), ride=rides.get("ret"))
    (cf, cr, csf, csr), rode["ret"] = res[:4], res[4:]
    (y_c,) = _c_out.fwd([(cf, cr), proj], [p["c_norm_w"][None, :]], [0, C_CZ // 512])

    (merged,) = _branch_merge.fwd([y_a, y_b, y_c, proj, proj, proj], [wb[0], wb[1], wb[2]], MERGE_COLS)
    saved = dict(proj=proj, q=q, k=k, v=v, of=of, orv=orv, ss=(ssf, ssr), ts=(tsf, tsr), qb=qb, kvb=kvb, ob=ob,
                 qc=qc, kc=kc, cf=cf, cr=cr, cs=(csf, csr), y=(y_a, y_b, y_c), wb=wb)
    return merged, saved, rode


def _core_backward(h, w16, p, s, dmerged, rides, branch_rides_in_attn=False):
    proj, wb = s["proj"], s["wb"]
    (cos_b, sin_b), (cos_c, sin_c) = _rope_tables()
    conv8 = _conv8(p["a_conv_w"])
    y_a, y_b, y_c = s["y"]
    rode = {}

    (*dy, dma, dmb, dmc), dwb = _branch_merge.bwd([y_a, y_b, y_c, proj, proj, proj], [wb[0], wb[1], wb[2]],
                                                   [dmerged], MERGE_COLS, bf16_rows=(3, 4, 5))
    dwb = jnp.stack(dwb)

    consts, consts_vjp = jax.vjp(_ret_consts, p["c_decay"])
    (do_c, dcz), (dcnw,) = _c_out.bwd([(s["cf"], s["cr"]), proj], [p["c_norm_w"][None, :]], [dy[2]],
                                      [0, C_CZ // 512], bf16_rows=(1,))
    g = _ret_bwd_call(s["qc"], s["kc"], proj, C_CV // 512, *consts, s["cs"], do_c, ride=rides.get("ret"))
    rode["ret"] = g[10:]
    (dcq, dck), _ = _c_rope.bwd([proj, proj, cos_c, sin_c], [], [(g[0], g[1]), (g[2], g[3])], C_ROPE_COLS,
                                bf16_rows=(0, 1))
    dcv = (g[4] + g[5]).astype(BF16)
    (dc_decay,) = consts_vjp(tuple(g[6:10]))

    (dob, dbz), _ = _b_out.bwd([s["ob"], proj], [], [dy[1]], [0, C_BZ // 512], bf16_rows=(1,))
    attn_ride = rides.get("attn")
    if branch_rides_in_attn:
        attn_ride = (list(attn_ride[0]) + [_reshard_layer("w_branch", dwb).astype(BF16)], attn_ride[1])
    res = _attn_bwd_call(s["qb"], s["kvb"], p["b_sink"], dob, ride=attn_ride)
    (dqb, dkvb, dsink), rode["attn"] = res[:3], res[3:]
    (dbq, dbkv), _ = _b_rope.bwd([proj, proj, cos_b, sin_b], [], [dqb, dkvb], B_ROPE_COLS, bf16_rows=(0, 1))

    ab = proj[:, C_AB:C_AB + 16]
    gates, gates_vjp = jax.vjp(_delta_gates, ab, p["a_log"], p["a_dt_bias"])
    (do_a, daz), (danw,) = _a_out.bwd([(s["of"], s["orv"]), proj], [p["a_norm_w"][None, :]], [dy[0]],
                                      [0, C_AZ // 512], bf16_rows=(1,))
    g = _delta_bwd_call(s["q"], s["k"], s["v"], *gates, s["ss"], s["ts"], do_a, ride=rides.get("delta"))
    rode["delta"] = g[12:]
    dgates = ((g[6], g[7]), (g[8], g[9]), (g[10], g[11]))
    dab, da_log, ddt = gates_vjp(dgates)
    dpre, dconv = [], []
    for (col, nrm, scl, nm), df, dr in zip(A_PIECES, (g[0], g[2], g[4]), (g[1], g[3], g[5])):
        dx, dw = _a_prep_bwd(proj, conv8, col, nrm, scl, df, dr, nm)
        dpre.append(dx)
        dconv.append(dw[:A_CONV])

    dproj = jnp.concatenate(dpre + [daz, dbq, dbz, dcq, dck, dcv, dcz, dma, dmb, dmc, dbkv,
                                    jnp.pad(dab, ((0, 0), (0, IN_PAD - C_AB - 16))).astype(BF16)], axis=1)
    dh = _matmul(dproj, w16, "w_in_da", "nt", W_IN_TILES["nt"])
    dw = _matmul(h.T.astype(BF16), dproj, "w_in_db", "nn", W_IN_TILES["db"])
    dp = dict(a_conv_w=jnp.concatenate(dconv, axis=1), a_log=da_log, a_dt_bias=ddt, a_norm_w=danw[0],
              b_sink=dsink[:, 0], c_decay=dc_decay, c_norm_w=dcnw[0], w_branch=dwb)
    return dh, dw, dp, rode


CORE_PARAMS = ("a_conv_w", "a_log", "a_dt_bias", "a_norm_w", "b_sink", "c_decay", "c_norm_w", "w_branch")


W_IN_SHARD = IN_WIDTH // N_DEV
W_IN_RUNS = ((0, 2048, 0), (2064, 512, C_BQ), (2832, 512, C_BZ), (3344, 5120, C_CQ), (2576, 256, C_BKV),
             (2048, 16, C_AB))


def _shard_overlap(start, width, j):
    lo, hi = max(start, j * W_IN_SHARD), min(start + width, (j + 1) * W_IN_SHARD)
    return (lo, hi) if lo < hi else None


def _w_in_from_shards(g):
    parts = []
    for start, width, _ in W_IN_RUNS:
        for j in range(N_DEV):
            span = _shard_overlap(start, width, j)
            if span:
                parts.append(g[j, :, span[0] - j * W_IN_SHARD:span[1] - j * W_IN_SHARD])
    parts.append(jnp.zeros((D_MODEL, IN_PAD - IN_WIDTH), g.dtype))
    return jnp.concatenate(parts, axis=1)


def _w_in_blocks(dw):
    blocks = []
    for j in range(N_DEV):
        parts = []
        for start, width, pad in sorted(W_IN_RUNS):
            span = _shard_overlap(start, width, j)
            if span:
                parts.append(dw[:, pad + span[0] - start:pad + span[1] - start])
        blocks.append(jnp.concatenate(parts, axis=1))
    return jnp.stack(blocks)


LAYER_SHARDED = ("w_ada", "w_in", "w_branch", "w_out")


def _unshard_layer(name, g):
    if name == "w_branch":
        return g.transpose(1, 2, 0, 3).reshape(3, BR_WIDTH, D_MODEL)
    if name == "w_out":
        return g.reshape(D_MODEL, D_MODEL)
    return g.transpose(1, 0, 2).reshape(D_MODEL, -1)


def _reshard_layer(name, w):
    if name == "w_branch":
        return w.reshape(3, BR_WIDTH, N_DEV, D_MODEL // N_DEV).transpose(2, 0, 1, 3)
    if name == "w_out":
        return w.reshape(N_DEV, D_MODEL // N_DEV, D_MODEL)
    return w.reshape(D_MODEL, N_DEV, -1).transpose(1, 0, 2)


def _layer_weights(gathered):
    out = {n: _unshard_layer(n, g) for n, g in gathered.items() if n != "w_in"}
    out["w_in16"] = _w_in_from_shards(gathered["w_in"])
    return out


def _grad_blocks(name, g):
    return (_w_in_blocks(g) if name == "w_in" else _reshard_layer(name, g)).astype(BF16)


def _forward_backward(small, layer0, shards0, shards1, x, c, ctx, loss_target):
    c_ctx = small["c_ctx"]
    sc16 = jnp.zeros((16, D_MODEL), F32).at[0].set(_silu(c)).at[1].set(_silu(c_ctx))
    xs = jnp.concatenate([ctx, x], axis=0)
    weights = [dict(layer0), None]
    layers = []
    for l in range(DEPTH):
        wl = weights[l]
        mod16 = _matmul(sc16, wl["w_ada"], "ada") + small["b_ada"][l][None, :]
        mod_cx = jnp.stack([mod16[1], mod16[0]])
        shift, scale, gate = jnp.split(mod_cx, 3, axis=1)
        nw = small["norm_w"][l][None, :]
        (h,) = _norm_mod.fwd([xs], [nw, shift, scale])
        p = {n: small[n][l] for n in CORE_PARAMS if n != "w_branch"}
        p["w_branch"] = wl.get("w_branch")
        rides = {}
        if l == 0:
            rides = {"w_in": ([shards0["w_branch"], shards0["w_out"]], True), "delta": ([shards1["w_in"]], True),
                     "attn": ([shards1["w_ada"]], True), "ret": ([shards1["w_branch"], shards1["w_out"]], True)}
        merged, saved, rode = _core_forward(h, wl["w_in16"], p, rides)
        if l == 0:
            wl["w_out"] = _unshard_layer("w_out", rode["w_in"][1])
            weights[1] = _layer_weights(dict(w_in=rode["delta"][0], w_ada=rode["attn"][0],
                                             w_branch=rode["ret"][0], w_out=rode["ret"][1]))
        (xs_next,) = _out_residual.fwd([xs, merged], [wl["w_out"], gate])
        layers.append(dict(xs=xs, h=h, p=p, saved=saved, merged=merged, gate=gate, nw=nw, shift=shift, scale=scale))
        xs = xs_next
    fw = small["final_norm_w"][None, :]
    xs = xs[CTX_LEN:]
    (per_row,) = _loss_rows.fwd([xs, loss_target], [fw])
    loss = jnp.sum(per_row[:, 0])

    d_per_row = jnp.zeros((SEQ, 128), F32).at[:, 0].set(1.0)
    (dxs,), (dfw,) = _loss_rows.bwd([xs, loss_target], [fw], [d_per_row])
    dxs = jnp.pad(dxs, ((CTX_LEN, 0), (0, 0)))
    small_names = tuple(n for n in CORE_PARAMS if n != "w_branch") + ("b_ada", "norm_w")
    dsmall = {n: [None] * DEPTH for n in small_names}
    dlayer = [None] * DEPTH
    contrib0 = contrib1 = None
    dsc16 = jnp.zeros((16, D_MODEL), F32)
    for l in reversed(range(DEPTH)):
        s, wl = layers[l], weights[l]
        (dres, dmerged), (dw_out, dgate) = _out_residual.bwd([s["xs"], s["merged"]], [wl["w_out"], s["gate"]], [dxs])
        rides = {}
        if l == 0:
            blocks1 = {n: _grad_blocks(n, g) for n, g in dlayer[1].items()}
            rides = {"ret": ([blocks1["w_branch"], blocks1["w_out"]], False),
                     "attn": ([_reshard_layer("w_out", dw_out).astype(BF16), blocks1["w_ada"]], False),
                     "delta": ([blocks1["w_in"]], False)}
        dh, dw_in, dp, rode = _core_backward(s["h"], wl["w_in16"], s["p"], s["saved"], dmerged, rides,
                                             branch_rides_in_attn=(l == 0))
        if l == 0:
            contrib1 = dict(w_in=rode["delta"][0], w_ada=rode["attn"][1], w_branch=rode["ret"][0],
                            w_out=rode["ret"][1])
            contrib0 = dict(w_out=rode["attn"][0], w_branch=rode["attn"][2])
        (dxn,), (dnw, dshift, dscale) = _norm_mod.bwd([s["xs"]], [s["nw"], s["shift"], s["scale"]], [dh])
        dxs = dres + dxn
        dmod_cx = jnp.concatenate([dshift, dscale, dgate], axis=1)
        dmod16 = jnp.zeros((16, 3 * D_MODEL), F32).at[0].set(dmod_cx[1]).at[1].set(dmod_cx[0])
        dsc16 = dsc16 + _matmul(dmod16, wl["w_ada"], "ada_da", "nt")
        dlayer[l] = dict(w_ada=_matmul(sc16, dmod16, "ada_db", "tn"), w_in=dw_in,
                         w_branch=dp["w_branch"], w_out=dw_out)
        for n in small_names:
            if n in dp:
                dsmall[n][l] = dp[n]
        dsmall["norm_w"][l] = dnw[0]
        dsmall["b_ada"][l] = dmod_cx[0] + dmod_cx[1]
    gsmall = {n: jnp.stack(v) for n, v in dsmall.items()}
    gsmall["final_norm_w"] = dfw[0]
    sig = jax.nn.sigmoid(c_ctx)
    gsmall["c_ctx"] = dsc16[1] * sig * (1.0 + c_ctx * (1.0 - sig))
    return loss, dxs[CTX_LEN:], gsmall, {n: dlayer[0][n] for n in ("w_ada", "w_in")}, contrib0, contrib1


SMALL = ("c_ctx", "b_ada", "norm_w", "a_log", "a_dt_bias", "a_norm_w", "b_sink", "c_decay", "c_norm_w",
         "final_norm_w")
WEIGHTS = ("c_ctx", "w_ada", "b_ada", "norm_w", "w_in", "a_conv_w", "a_log", "a_dt_bias", "a_norm_w", "b_sink",
           "c_decay", "c_norm_w", "w_branch", "w_out", "final_norm_w")
SMALL_PACK = 12288


def _unshard_conv(g):
    return g.transpose(1, 2, 0, 3).reshape(DEPTH, A_CONV, 3 * A_WIDTH)


def _reshard_conv(w):
    return w.reshape(DEPTH, A_CONV, N_DEV, 3 * A_WIDTH // N_DEV).transpose(2, 0, 1, 3)


def _pack_small(tree):
    flat = jnp.concatenate([tree[n].reshape(-1) for n in SMALL])
    return jnp.pad(flat, (0, SMALL_PACK - flat.shape[0])).reshape(SMALL_PACK // 128, 128)


def _unpack_small(packed, like):
    flat = packed.reshape(-1)
    out, off = {}, 0
    for n in SMALL:
        size = math.prod(like[n].shape)
        out[n] = flat[off:off + size].reshape(like[n].shape)
        off += size
    return out


def kernel(x, c, ctx, c_ctx, w_ada, b_ada, norm_w, w_in, a_conv_w, a_log, a_dt_bias, a_norm_w, b_sink, c_decay, c_norm_w, w_branch, w_out, final_norm_w, loss_target, m_c_ctx, m_w_ada, m_b_ada, m_norm_w, m_w_in, m_a_conv_w, m_a_log, m_a_dt_bias, m_a_norm_w, m_b_sink, m_c_decay, m_c_norm_w, m_w_branch, m_w_out, m_final_norm_w, v_c_ctx, v_w_ada, v_b_ada, v_norm_w, v_w_in, v_a_conv_w, v_a_log, v_a_dt_bias, v_a_norm_w, v_b_sink, v_c_decay, v_c_norm_w, v_w_branch, v_w_out, v_final_norm_w):
    w = dict(c_ctx=c_ctx, w_ada=w_ada, b_ada=b_ada, norm_w=norm_w, w_in=w_in, a_conv_w=a_conv_w, a_log=a_log,
             a_dt_bias=a_dt_bias, a_norm_w=a_norm_w, b_sink=b_sink, c_decay=c_decay, c_norm_w=c_norm_w,
             w_branch=w_branch, w_out=w_out, final_norm_w=final_norm_w)
    m = dict(c_ctx=m_c_ctx, w_ada=m_w_ada, b_ada=m_b_ada, norm_w=m_norm_w, w_in=m_w_in, a_conv_w=m_a_conv_w,
             a_log=m_a_log, a_dt_bias=m_a_dt_bias, a_norm_w=m_a_norm_w, b_sink=m_b_sink, c_decay=m_c_decay,
             c_norm_w=m_c_norm_w, w_branch=m_w_branch, w_out=m_w_out, final_norm_w=m_final_norm_w)
    v = dict(c_ctx=v_c_ctx, w_ada=v_w_ada, b_ada=v_b_ada, norm_w=v_norm_w, w_in=v_w_in, a_conv_w=v_a_conv_w,
             a_log=v_a_log, a_dt_bias=v_a_dt_bias, a_norm_w=v_a_norm_w, b_sink=v_b_sink, c_decay=v_c_decay,
             c_norm_w=v_c_norm_w, w_branch=v_w_branch, w_out=v_w_out, final_norm_w=v_final_norm_w)

    shards = {n: w[n].astype(BF16) for n in LAYER_SHARDED}
    first = _exchange([shards["w_ada"][0], shards["w_in"][0], w["a_conv_w"]], True, "gather_layer0")
    layer0 = _layer_weights(dict(w_ada=first[0], w_in=first[1]))
    small_w = {n: w[n] for n in SMALL}
    small_w["a_conv_w"] = _unshard_conv(first[2])
    loss, gx, gw, glayer0, contrib0, contrib1 = _forward_backward(
        small_w, layer0, {n: shards[n][0] for n in ("w_branch", "w_out")}, {n: shards[n][1] for n in LAYER_SHARDED},
        x[0], c[0], ctx[0], loss_target[0])
    loss = lax.psum(loss, ("x", "y", "c"))

    last = _exchange([_reshard_layer("w_ada", glayer0["w_ada"]).astype(BF16),
                      _grad_blocks("w_in", glayer0["w_in"]), _reshard_conv(gw["a_conv_w"])],
                     False, "scatter_layer0")
    contrib0["w_ada"], contrib0["w_in"] = last[0], last[1]
    small = _exchange([_pack_small(gw)], True, "gather_small_grads")[0]

    grad, delta, new_m, new_v = {}, {}, {}, {}
    for n in LAYER_SHARDED:
        shp = w[n].shape
        per_layer = (math.prod(shp[1:-1]), shp[-1])
        outs = _adamw_layers(*[a.reshape((DEPTH,) + per_layer) for a in (w[n], m[n], v[n])],
                             *[cb.reshape((N_DEV,) + per_layer) for cb in (contrib0[n], contrib1[n])], "adamw_" + n)
        grad[n], delta[n], new_m[n], new_v[n] = [o.reshape(shp) for o in outs]
    shp = a_conv_w.shape
    two_d = (math.prod(shp[:-1]), shp[-1])
    outs = _adamw(*[a.reshape(two_d) for a in (a_conv_w, m_a_conv_w, v_a_conv_w)],
                  last[2].reshape((N_DEV,) + two_d), "adamw_a_conv_w")
    grad["a_conv_w"], delta["a_conv_w"], new_m["a_conv_w"], new_v["a_conv_w"] = [o.reshape(shp) for o in outs]
    outs = _adamw(_pack_small(w), _pack_small(m), _pack_small(v), small, "adamw_small")
    for tree, packed in zip((grad, delta, new_m, new_v), outs):
        tree.update(_unpack_small(packed, w))

    return (loss, gx[None], *[grad[n] for n in WEIGHTS], *[delta[n] for n in WEIGHTS],
            *[new_m[n] for n in WEIGHTS], *[new_v[n] for n in WEIGHTS])
```

```python
import functools
import math

import jax
import jax.numpy as jnp
from jax import lax
from jax.experimental import pallas as pl
from jax.experimental.pallas import tpu as pltpu

F32 = jnp.float32
BF16 = jnp.bfloat16
HIGHEST = lax.Precision.HIGHEST

D_MODEL = 1024
SEQ = 4096
DEPTH = 2
GRID_W = 64
CTX_LEN = 256
EPS = 1e-6
ROPE_BASE = 10000.0
BR_WIDTH = D_MODEL // 2
A_DK = 128
A_HEADS = 4
A_WIDTH = 512
A_CONV = 5
B_HD = 64
B_Q_HEADS = 8
B_KV_HEADS = 2
WINDOW = 128
B_BLOCK = 128
C_HD = 128
C_HEADS = 4
C_WIDTH = 512
CHUNK = 64
ADAM_LR = 0.001
ADAM_B1 = 0.9
ADAM_B2 = 0.999
ADAM_EPS = 1e-08
ADAM_WD = 0.01
ADAM_STEP = 10

N_DEV = 8
ROWS = CTX_LEN + SEQ
N_CHUNK = ROWS // CHUNK
N_CTX_CHUNK = CTX_LEN // CHUNK
IN_WIDTH = 8464
IN_PAD = 8704
NEG = -1e30

VMEM_LIMIT = 48 * 1024 * 1024
MESH = pl.DeviceIdType.MESH

C_AQ, C_AK, C_AV, C_AZ, C_BQ, C_BZ, C_CQ, C_CK, C_CV, C_CZ = (i * 512 for i in range(10))
C_MERGE = 5120
C_BKV = 8192
C_AB = 8448


def _cparams(sem=None):
    if sem is None:
        return pltpu.CompilerParams(vmem_limit_bytes=VMEM_LIMIT)
    return pltpu.CompilerParams(dimension_semantics=sem, vmem_limit_bytes=VMEM_LIMIT)


def _dg(a, b, ca, cb, prec=None):
    return lax.dot_general(a, b, (((ca,), (cb,)), ((), ())), preferred_element_type=F32, precision=prec)


@functools.partial(jax.custom_vjp, nondiff_argnums=(2, 3))
def _bdot(a, b, ca, cb):
    return _dg(a.astype(BF16), b.astype(BF16), ca, cb)


def _bdot_fwd(a, b, ca, cb):
    return _bdot(a, b, ca, cb), (a, b)


def _bdot_bwd(ca, cb, res, ct):
    a, b = res
    da = _bdot(ct, b, 1, 1 - cb) if ca == 1 else _bdot(b, ct, 1 - cb, 1)
    db = _bdot(a, ct, 1 - ca, 0) if cb == 0 else _bdot(ct, a, 0, 1 - ca)
    return da, db


_bdot.defvjp(_bdot_fwd, _bdot_bwd)


def _hdot(a, b):
    return _dg(a, b, 1, 0, lax.Precision.HIGH)


def _k_silu(x):
    return x / (1.0 + jnp.exp(-x))


def _k_sigmoid(x):
    return 1.0 / (1.0 + jnp.exp(-x))


@jax.custom_vjp
def _swap64(x):
    return pltpu.roll(x, 64, 1)


_swap64.defvjp(lambda x: (pltpu.roll(x, 64, 1), None), lambda _, ct: (pltpu.roll(ct, 64, 1),))


def _swap16_impl(x):
    lane = lax.broadcasted_iota(jnp.int32, x.shape, 1)
    return jnp.where((lane & 16) == 0, pltpu.roll(x, 112, 1), pltpu.roll(x, 16, 1))


@jax.custom_vjp
def _swap16(x):
    return _swap16_impl(x)


_swap16.defvjp(lambda x: (_swap16_impl(x), None), lambda _, ct: (_swap16_impl(ct),))


def _pick(dim, prefs):
    for p in prefs:
        if dim % p == 0:
            return p
    return dim


def _matmul(a, b, name, mode="nn", tiles=None, ride=None):
    ca, cb = {"nn": (1, 0), "nt": (1, 1), "tn": (0, 0)}[mode]
    m, k = a.shape[1 - ca], a.shape[ca]
    n = b.shape[1 - cb]
    if tiles is None:
        tiles = (_pick(m, (1088, 1024, 512, 256, 128)), _pick(n, (512, 256, 128)),
                 _pick(k, (1088, 1024, 512, 256, 128) if mode == "tn" else (2176, 2048, 1024, 512, 256, 128)))
    tm, tn, tk = tiles
    nk = k // tk
    a_spec = (pl.BlockSpec((tm, tk), lambda i, j, kk: (i, kk)) if ca == 1
              else pl.BlockSpec((tk, tm), lambda i, j, kk: (kk, i)))
    b_spec = (pl.BlockSpec((tk, tn), lambda i, j, kk: (kk, j)) if cb == 0
              else pl.BlockSpec((tn, tk), lambda i, j, kk: (j, kk)))

    def body(a_ref, b_ref, o_ref):
        part = _dg(a_ref[...].astype(BF16), b_ref[...].astype(BF16), ca, cb)
        if nk == 1:
            o_ref[...] = part
        else:
            kk = pl.program_id(2)

            @pl.when(kk == 0)
            def _():
                o_ref[...] = part

            @pl.when(kk > 0)
            def _():
                o_ref[...] += part

    grid = (m // tm, n // tn, nk)
    if ride is None:
        return pl.pallas_call(
            body,
            grid=grid,
            in_specs=[a_spec, b_spec],
            out_specs=pl.BlockSpec((tm, tn), lambda i, j, kk: (i, j)),
            out_shape=jax.ShapeDtypeStruct((m, n), F32),
            compiler_params=_cparams(("parallel", "parallel", "arbitrary")),
            name=name,
        )(a, b)
    body, r_in, r_out, r_shape, r_scratch = _riding(body, 2, 1, 0, ride, grid)
    return pl.pallas_call(
        body,
        grid=grid,
        in_specs=[a_spec, b_spec] + r_in,
        out_specs=[pl.BlockSpec((tm, tn), lambda i, j, kk: (i, j))] + r_out,
        out_shape=[jax.ShapeDtypeStruct((m, n), F32)] + r_shape,
        scratch_shapes=r_scratch,
        compiler_params=_cparams(("arbitrary", "arbitrary", "arbitrary")),
        name=name,
    )(a, b, *ride[0])


ROW_BLOCK = 256
ROW_VMEM_BUDGET = 16 * 1024 * 1024


def _pieces(val, pw):
    return [val[:, j * pw:(j + 1) * pw] for j in range(val.shape[1] // pw)]


def _flat(groups):
    arrays, sizes = [], []
    for g in groups:
        g = g if isinstance(g, (tuple, list)) else (g,)
        arrays += list(g)
        sizes.append(len(g))
    return arrays, sizes


def _regroup(refs, sizes):
    out, at = [], 0
    for n in sizes:
        val = refs[at][...]
        for r in refs[at + 1:at + n]:
            val = val + r[...]
        out.append(val)
        at += n
    return out


class _Rowwise:
    def __init__(self, fn, name, row_wpw, par_pw, out_wpw, n_diff=None):
        self.fn, self.name, self.row_wpw, self.par_pw, self.out_wpw = fn, name, row_wpw, par_pw, out_wpw
        self.n_diff = len(row_wpw) if n_diff is None else n_diff

        @jax.custom_vjp
        def call(rows, params):
            return self.fwd(rows, params)

        def call_fwd(rows, params):
            return self.fwd(rows, params), (rows, params)

        def call_bwd(res, douts):
            return self.bwd(res[0], res[1], douts)

        call.defvjp(call_fwd, call_bwd)
        self.call = call

    def _load(self, row_vals, par_refs, br, with_ctx):
        row = pl.program_id(0) * br + lax.broadcasted_iota(jnp.int32, (br, 1), 0)
        is_ctx = (row < (CTX_LEN if with_ctx else 0)).astype(F32)
        rows = [_pieces(v, pw) for v, (_, pw) in zip(row_vals, self.row_wpw)]
        pars = []
        for p, pw in zip(par_refs, self.par_pw):
            val = p[...].astype(F32)
            if p.shape[0] == 2:
                val = is_ctx * val[0:1, :] + (1.0 - is_ctx) * val[1:2, :]
            pars.append(_pieces(val, pw))
        return rows, pars, is_ctx

    def _block_rows(self, n_rows, widths):
        for br in (1088, 1024, 544, 512, 272):
            if n_rows % br == 0 and 2 * 4 * br * sum(widths) <= ROW_VMEM_BUDGET:
                return br
        return ROW_BLOCK

    def _row_specs(self, br, sizes, cols):
        out = []
        for (w, _), n, c in zip(self.row_wpw, sizes, cols):
            out += [pl.BlockSpec((br, w), lambda i, c=c: (i, c))] * n
        return out

    def fwd(self, rows, params, cols=None):
        arrays, sizes = _flat(rows)
        cols = cols or [0] * len(rows)
        n_rows = arrays[0].shape[0]
        n_in = len(arrays)
        br = self._block_rows(n_rows, [w for (w, _), n in zip(self.row_wpw, sizes) for _ in range(n)]
                              + [w for w, _ in self.out_wpw])

        def body(*refs):
            r, p, _ = self._load(_regroup(refs[:n_in], sizes), refs[n_in:n_in + len(params)], br, n_rows == ROWS)
            for o_ref, pieces, (_, pw) in zip(refs[n_in + len(params):], self.fn(r, p), self.out_wpw):
                for j, piece in enumerate(pieces):
                    o_ref[:, j * pw:(j + 1) * pw] = piece

        return pl.pallas_call(
            body,
            grid=(n_rows // br,),
            in_specs=self._row_specs(br, sizes, cols) + [pl.BlockSpec(p.shape, lambda i: (0, 0)) for p in params],
            out_specs=[pl.BlockSpec((br, w), lambda i: (i, 0)) for w, _ in self.out_wpw],
            out_shape=[jax.ShapeDtypeStruct((n_rows, w), F32) for w, _ in self.out_wpw],
            compiler_params=_cparams(("parallel",)),
            name=self.name + "_fwd",
        )(*arrays, *params)

    def bwd(self, rows, params, douts, cols=None, bf16_rows=()):
        arrays, sizes = _flat(rows)
        darrays, dsizes = _flat(douts)
        cols = cols or [0] * len(rows)
        n_rows = arrays[0].shape[0]
        n_in, n_par, n_dout, n_diff = len(arrays), len(params), len(darrays), self.n_diff
        br = self._block_rows(n_rows, [w for (w, _), n in zip(self.row_wpw, sizes) for _ in range(n)]
                              + [w for (w, _), n in zip(self.out_wpw, dsizes) for _ in range(n)]
                              + [w for w, _ in self.row_wpw[:n_diff]])

        def body(*refs):
            par_refs = refs[n_in:n_in + n_par]
            dout_refs = refs[n_in + n_par:n_in + n_par + n_dout]
            drow_refs = refs[n_in + n_par + n_dout:n_in + n_par + n_dout + n_diff]
            dpar_refs = refs[n_in + n_par + n_dout + n_diff:]

            @pl.when(pl.program_id(0) == 0)
            def _():
                for d in dpar_refs:
                    d[...] = jnp.zeros_like(d)

            r, p, is_ctx = self._load(_regroup(refs[:n_in], sizes), par_refs, br, n_rows == ROWS)
            cts = [_pieces(d, pw) for d, (_, pw) in zip(_regroup(dout_refs, dsizes), self.out_wpw)]
            fixed = r[n_diff:]
            _, vjp = jax.vjp(lambda rd, pp: self.fn(rd + fixed, pp), r[:n_diff], p)
            dr, dp = vjp(cts)
            for d_ref, pieces, (_, pw) in zip(drow_refs, dr, self.row_wpw):
                for j, piece in enumerate(pieces):
                    d_ref[:, j * pw:(j + 1) * pw] = piece.astype(d_ref.dtype)
            for d_ref, pieces, pw in zip(dpar_refs, dp, self.par_pw):
                for j, piece in enumerate(pieces):
                    lanes = slice(j * pw, (j + 1) * pw)
                    if d_ref.shape[0] != 2:
                        d_ref[:, lanes] += piece
                    else:
                        d_ref[0:1, lanes] += jnp.sum(is_ctx * piece, axis=0, keepdims=True)
                        d_ref[1:2, lanes] += jnp.sum((1.0 - is_ctx) * piece, axis=0, keepdims=True)

        par_specs = [pl.BlockSpec(p.shape, lambda i: (0, 0)) for p in params]
        dout_specs = []
        for (w, _), n in zip(self.out_wpw, dsizes):
            dout_specs += [pl.BlockSpec((br, w), lambda i: (i, 0))] * n
        drow_w = [w for w, _ in self.row_wpw[:n_diff]]
        g = pl.pallas_call(
            body,
            grid=(n_rows // br,),
            in_specs=self._row_specs(br, sizes, cols) + par_specs + dout_specs,
            out_specs=[pl.BlockSpec((br, w), lambda i: (i, 0)) for w in drow_w] + par_specs,
            out_shape=[jax.ShapeDtypeStruct((n_rows, w), BF16 if a in bf16_rows else F32) for a, w in enumerate(drow_w)]
            + [jax.ShapeDtypeStruct(p.shape, F32) for p in params],
            compiler_params=_cparams(("arbitrary",)),
            name=self.name + "_bwd",
        )(*arrays, *params, *darrays)
        return list(g[:n_diff]), list(g[n_diff:])


def _fn_norm_mod(rows, pars):
    (x,), (nw,), (shift,), (scale,) = rows[0], pars[0], pars[1], pars[2]
    y = x * lax.rsqrt(jnp.mean(x * x, axis=-1, keepdims=True) + EPS) * nw
    return [[y * (1.0 + scale) + shift]]


def _fn_head_rms_gate(rows, pars):
    (w,) = pars[0]
    return [[o * lax.rsqrt(jnp.mean(o * o, axis=-1, keepdims=True) + EPS) * w * _k_silu(z)
             for o, z in zip(rows[0], rows[1])]]


def _fn_group_norm_gate(rows, pars):
    out = []
    for o, z, w in zip(rows[0], rows[1], pars[0]):
        mu = jnp.mean(o, axis=-1, keepdims=True)
        var = jnp.mean(jnp.square(o - mu), axis=-1, keepdims=True)
        out.append((o - mu) * lax.rsqrt(var + EPS) * w * _k_silu(z))
    return [out]


def _fn_gate(rows, pars):
    return [[o * _k_silu(z) for o, z in zip(rows[0], rows[1])]]


def _fn_branch_merge(rows, pars):
    (ya,), (yb,), (yc,), (ma,), (mb,), (mc,) = rows
    (wa,), (wb,), (wc,) = pars
    return [[_k_sigmoid(ma) * _bdot(ya, wa, 1, 0) + _k_sigmoid(mb) * _bdot(yb, wb, 1, 0)
             + _k_sigmoid(mc) * _bdot(yc, wc, 1, 0)]]


def _fn_out_residual(rows, pars):
    (res,), (merged,), (w,), (gate,) = rows[0], rows[1], pars[0], pars[1]
    return [[res + gate * _bdot(merged, w, 1, 0)]]


def _fn_loss(rows, pars):
    (x,), (target,), (w,) = rows[0], rows[1], pars[0]
    y = x * lax.rsqrt(jnp.mean(x * x, axis=-1, keepdims=True) + EPS) * w
    per_row = 0.5 * jnp.mean(jnp.square(y - target), axis=-1, keepdims=True)
    return [[jnp.broadcast_to(per_row, (per_row.shape[0], 128))]]


def _fn_b_rope(rows, pars):
    q, (k, v), (cos,), (sin,) = rows
    rot = lambda x: x * cos + _swap16(x) * sin
    return [[rot(x) for x in q], [rot(k), v]]


def _fn_c_rope(rows, pars):
    q, k, (cos,), (sin,) = rows
    rot = lambda x: x * cos + _swap64(x) * sin
    return [[rot(x) for x in q], [rot(x) * (C_HD ** -0.5) for x in k]]


_norm_mod = _Rowwise(_fn_norm_mod, "norm_mod", [(D_MODEL, D_MODEL)], [D_MODEL] * 3, [(D_MODEL, D_MODEL)])
_out_residual = _Rowwise(_fn_out_residual, "out_residual", [(D_MODEL, D_MODEL)] * 2, [D_MODEL] * 2,
                         [(D_MODEL, D_MODEL)])
_loss_rows = _Rowwise(_fn_loss, "loss", [(D_MODEL, D_MODEL)] * 2, [D_MODEL], [(128, 128)], n_diff=1)
_a_out = _Rowwise(_fn_head_rms_gate, "a_out", [(512, 128)] * 2, [128], [(512, 128)])
_c_out = _Rowwise(_fn_group_norm_gate, "c_out", [(512, 128)] * 2, [128], [(512, 128)])
_b_out = _Rowwise(_fn_gate, "b_out", [(512, 512)] * 2, [], [(512, 512)])
_branch_merge = _Rowwise(_fn_branch_merge, "branch_merge", [(512, 512)] * 3 + [(D_MODEL, D_MODEL)] * 3,
                         [D_MODEL] * 3, [(D_MODEL, D_MODEL)])
_b_rope = _Rowwise(_fn_b_rope, "b_rope", [(512, 128), (256, 128), (128, 128), (128, 128)], [],
                   [(512, 128), (256, 128)], n_diff=2)
_c_rope = _Rowwise(_fn_c_rope, "c_rope", [(512, 128), (512, 128), (128, 128), (128, 128)], [],
                   [(512, 128), (512, 128)], n_diff=2)


HALO = 8
EXT = ROW_BLOCK + 2 * HALO


def _halo_specs(col, width=512):
    last = ROWS // HALO - 1
    per = ROW_BLOCK // HALO
    prev = pl.BlockSpec((HALO, width), lambda i: (jnp.maximum(i * per - 1, 0), col))
    cur = pl.BlockSpec((ROW_BLOCK, width), lambda i: (i, col))
    nxt = pl.BlockSpec((HALO, width), lambda i: (jnp.minimum((i + 1) * per, last), col))
    return [prev, cur, nxt]


def _extended(prev_ref, cur_ref, next_ref):
    i = pl.program_id(0)
    prev_ok = i >= 2
    next_ok = jnp.logical_and(i >= 1, i < ROWS // ROW_BLOCK - 1)
    return jnp.concatenate([jnp.where(prev_ok, prev_ref[...], 0.0), cur_ref[...],
                            jnp.where(next_ok, next_ref[...], 0.0)], axis=0)


def _conv_taps(x_ext, w_ref, flip):
    acc = None
    for j in range(A_CONV):
        shift = (j - 2) if flip else (2 - j)
        term = w_ref[j:j + 1, :] * pltpu.roll(x_ext, shift % EXT, 0)
        acc = term if acc is None else acc + term
    return acc


def _conv_post(pre_pieces, normalize, scale):
    out = []
    for p in pre_pieces:
        y = _k_silu(p)
        if normalize:
            y = y * lax.rsqrt(jnp.sum(y * y, axis=-1, keepdims=True) + EPS) * scale
        out.append(y)
    return out


def _a_prep_fwd(proj, conv8, col, normalize, scale, name):
    def body(prev_ref, cur_ref, next_ref, w_ref, o_ref):
        pre = _conv_taps(_extended(prev_ref, cur_ref, next_ref), w_ref, False)[HALO:HALO + ROW_BLOCK]
        for h, y in enumerate(_conv_post(_pieces(pre, 128), normalize, scale)):
            o_ref[:, h * 128:(h + 1) * 128] = y

    return pl.pallas_call(
        body,
        grid=(ROWS // ROW_BLOCK,),
        in_specs=_halo_specs(col) + [pl.BlockSpec((8, 512), lambda i: (0, col))],
        out_specs=pl.BlockSpec((ROW_BLOCK, 512), lambda i: (i, 0)),
        out_shape=jax.ShapeDtypeStruct((ROWS, 512), F32),
        compiler_params=_cparams(("parallel",)),
        name=name + "_fwd",
    )(proj, proj, proj, conv8)


def _a_prep_bwd(proj, conv8, col, normalize, scale, dout_f, dout_r, name):
    def body(xp, xc, xn, w_ref, fp, fc, fn_, rp, rc, rn, dx_ref, dw_ref):
        @pl.when(pl.program_id(0) == 0)
        def _():
            dw_ref[...] = jnp.zeros_like(dw_ref)

        x_ext = _extended(xp, xc, xn)
        dout = _extended(fp, fc, fn_) + _extended(rp, rc, rn)
        pre = _conv_taps(x_ext, w_ref, False)
        _, vjp = jax.vjp(lambda p: _conv_post(p, normalize, scale), _pieces(pre, 128))
        (dpre,) = vjp(_pieces(dout, 128))
        dpre = jnp.concatenate(dpre, axis=1)
        dx_ref[...] = _conv_taps(dpre, w_ref, True)[HALO:HALO + ROW_BLOCK].astype(BF16)
        own = dpre[HALO:HALO + ROW_BLOCK]
        for j in range(A_CONV):
            shifted = pltpu.roll(x_ext, (2 - j) % EXT, 0)[HALO:HALO + ROW_BLOCK]
            dw_ref[j:j + 1, :] += jnp.sum(own * shifted, axis=0, keepdims=True)

    return pl.pallas_call(
        body,
        grid=(ROWS // ROW_BLOCK,),
        in_specs=_halo_specs(col) + [pl.BlockSpec((8, 512), lambda i: (0, col))] + _halo_specs(0) + _halo_specs(0),
        out_specs=[pl.BlockSpec((ROW_BLOCK, 512), lambda i: (i, 0)), pl.BlockSpec((8, 512), lambda i: (0, 0))],
        out_shape=[jax.ShapeDtypeStruct((ROWS, 512), BF16), jax.ShapeDtypeStruct((8, 512), F32)],
        compiler_params=_cparams(("arbitrary",)),
        name=name + "_bwd",
    )(proj, proj, proj, conv8, dout_f, dout_f, dout_f, dout_r, dout_r, dout_r)


N_CHAIN = 8


def _rev_chunk(s):
    return jnp.where(s < N_CTX_CHUNK, N_CTX_CHUNK - 1 - s, N_CHUNK + N_CTX_CHUNK - 1 - s)


def _scan_specs(step_of, v_col=0):
    cf = step_of
    cr = lambda n: _rev_chunk(step_of(n))

    def pair(shape, index):
        return (pl.BlockSpec(shape, lambda n: index(cf(n))), pl.BlockSpec(shape, lambda n: index(cr(n))))

    return dict(
        tok=pair((CHUNK, 512), lambda c: (c, 0)),
        tokv=pair((CHUNK, 512), lambda c: (c, v_col)),
        col=pair((4, CHUNK, 1), lambda c: (0, c, 0)),
        row=pair((4, 1, 1, CHUNK), lambda c: (0, c, 0, 0)),
        one=pair((4, 1, 1, 1), lambda c: (0, c, 0, 0)),
        state=pair((None, 4, 128, 128), lambda c: (c, 0, 0, 0)),
        tinv=pair((None, 4, CHUNK, CHUNK), lambda c: (c, 0, 0, 0)),
    )


def _both(specs, kinds):
    out = []
    for kind in kinds:
        out += list(specs[kind])
    return out


def _scan_call(body, name, in_specs, out_specs, out_shape, operands, ride):
    body, r_in, r_out, r_shape, r_scratch = _riding(body, len(in_specs), len(out_specs), 1, ride, (N_CHUNK,))
    return pl.pallas_call(
        body,
        grid=(N_CHUNK,),
        in_specs=in_specs + r_in,
        out_specs=out_specs + r_out,
        out_shape=out_shape + r_shape,
        scratch_shapes=[pltpu.VMEM((N_CHAIN, 128, 128), F32)] + r_scratch,
        compiler_params=_cparams(("arbitrary",)),
        name=name,
    )(*operands, *(ride[0] if ride else []))


def _chain_masks():
    ii = lax.broadcasted_iota(jnp.int32, (CHUNK, CHUNK), 0)
    jj = lax.broadcasted_iota(jnp.int32, (CHUNK, CHUNK), 1)
    eye = jnp.where(ii == jj, 1.0, 0.0).astype(F32)
    lower = (ii >= jj, ii > jj)
    upper = (ii <= jj, ii < jj)
    return [lower] * 4 + [upper] * 4, eye


def _tri_inv_all(ls, eye):
    xs = [eye - l for l in ls]
    ps = [_hdot(l, l) for l in ls]
    for i in range(5):
        xs = [x + _hdot(x, p) for x, p in zip(xs, ps)]
        if i < 4:
            ps = [_hdot(p, p) for p in ps]
    return xs


@jax.custom_vjp
def _inv_saved(l, x):
    return x


def _inv_saved_fwd(l, x):
    return x, x


def _inv_saved_bwd(x, dx):
    return -_bdot(x, _bdot(dx, x, 1, 1), 0, 0), jnp.zeros_like(x)


_inv_saved.defvjp(_inv_saved_fwd, _inv_saved_bwd)


def _delta_chains(q, k, v, beta_r, gcr, gl, s, masks, eye, tinv_saved):
    n = range(len(q))
    beta = [jnp.sum(eye * beta_r[i], axis=1, keepdims=True) for i in n]
    gcc = [jnp.sum(eye * gcr[i], axis=1, keepdims=True) for i in n]
    decay = [jnp.exp(jnp.where(masks[i][0], gcc[i] - gcr[i], NEG)) for i in n]
    kb = [k[i] * beta[i] for i in n]
    lmat = [jnp.where(masks[i][1], _bdot(kb[i], k[i], 1, 1) * decay[i], 0.0) for i in n]
    if tinv_saved is None:
        tinv = _tri_inv_all(lmat, eye)
    else:
        tinv = [_inv_saved(lmat[i], tinv_saved[i]) for i in n]
    eg = [jnp.exp(gcc[i]) for i in n]
    u = [_bdot(tinv[i], v[i] * beta[i], 1, 0) for i in n]
    w = [_bdot(tinv[i], kb[i] * eg[i], 1, 0) for i in n]
    qk = [_bdot(q[i], k[i], 1, 1) * decay[i] for i in n]
    v_new = [u[i] - _bdot(w[i], s[i], 1, 0) for i in n]
    o = [_bdot(q[i] * eg[i], s[i], 1, 0) + _bdot(qk[i], v_new[i], 1, 0) for i in n]
    s_new = [s[i] * jnp.exp(gl[i]) + _bdot(k[i] * jnp.exp(gl[i] - gcc[i]), v_new[i], 0, 0) for i in n]
    return (o, s_new), tinv


def _chain_loads(tok_pairs, small_pairs):
    toks = [[pair[i // 4][:, (i % 4) * 128:(i % 4 + 1) * 128] for i in range(N_CHAIN)] for pair in tok_pairs]
    smalls = [[pair[i // 4][i % 4] for i in range(N_CHAIN)] for pair in small_pairs]
    return toks, smalls


def _delta_fwd_call(q, k, v, beta, gc, gl, ride=None):
    sp = _scan_specs(lambda n: n)

    def body(qf, qr, kf, kr, vf, vr, bf, br, gcrf, gcrr, glf, glr, of, orv, ssf, ssr, tsf, tsr, s_scr):
        @pl.when(pl.program_id(0) == 0)
        def _():
            s_scr[...] = jnp.zeros_like(s_scr)

        masks, eye = _chain_masks()
        (qs, ks, vs), _ = _chain_loads([(qf, qr), (kf, kr), (vf, vr)], [])
        bs = [(bf, br)[i // 4][i % 4, 0] for i in range(N_CHAIN)]
        gcrs = [(gcrf, gcrr)[i // 4][i % 4, 0] for i in range(N_CHAIN)]
        gls = [(glf, glr)[i // 4][i % 4, 0] for i in range(N_CHAIN)]
        ss = [s_scr[i] for i in range(N_CHAIN)]
        (o, s_new), tinv = _delta_chains(qs, ks, vs, bs, gcrs, gls, ss, masks, eye, None)
        for i in range(N_CHAIN):
            d, h = i // 4, i % 4
            (ssf, ssr)[d][h] = ss[i]
            (tsf, tsr)[d][h] = tinv[i]
            (of, orv)[d][:, h * 128:(h + 1) * 128] = o[i]
            s_scr[i] = s_new[i]

    return _scan_call(
        body, "delta_fwd",
        _both(sp, ["tok", "tok", "tok", "row", "row", "one"]),
        _both(sp, ["tok", "state", "tinv"]),
        [jax.ShapeDtypeStruct((ROWS, 512), F32)] * 2 + [jax.ShapeDtypeStruct((N_CHUNK, 4, 128, 128), F32)] * 2
        + [jax.ShapeDtypeStruct((N_CHUNK, 4, CHUNK, CHUNK), F32)] * 2,
        [q, q, k, k, v, v, *beta, *gc, *gl], ride)


def _delta_bwd_call(q, k, v, beta, gc, gl, ssave, tsave, do, ride=None):
    sp = _scan_specs(lambda n: N_CHUNK - 1 - n)

    def body(qf, qr, kf, kr, vf, vr, bf, br, gcrf, gcrr, glf, glr, ssf, ssr, tsf, tsr, dof, dor,
             dqf, dqr, dkf, dkr, dvf, dvr, dbf, dbr, dgcrf, dgcrr, dglf, dglr, ds_scr):
        @pl.when(pl.program_id(0) == 0)
        def _():
            ds_scr[...] = jnp.zeros_like(ds_scr)

        masks, eye = _chain_masks()
        (qs, ks, vs, dos), (ss, ts) = _chain_loads(
            [(qf, qr), (kf, kr), (vf, vr), (dof, dor)], [(ssf, ssr), (tsf, tsr)])
        bs = [(bf, br)[i // 4][i % 4, 0] for i in range(N_CHAIN)]
        gcrs = [(gcrf, gcrr)[i // 4][i % 4, 0] for i in range(N_CHAIN)]
        gls = [(glf, glr)[i // 4][i % 4, 0] for i in range(N_CHAIN)]
        fn = lambda *a: _delta_chains(*a, masks, eye, ts)
        _, vjp, _ = jax.vjp(fn, qs, ks, vs, bs, gcrs, gls, ss, has_aux=True)
        dq, dk, dv, db, dgcr, dgl, ds = vjp((dos, [ds_scr[i] for i in range(N_CHAIN)]))
        for i in range(N_CHAIN):
            d, h = i // 4, i % 4
            hs = slice(h * 128, (h + 1) * 128)
            (dqf, dqr)[d][:, hs] = dq[i]
            (dkf, dkr)[d][:, hs] = dk[i]
            (dvf, dvr)[d][:, hs] = dv[i]
            (dbf, dbr)[d][h, 0] = db[i]
            (dgcrf, dgcrr)[d][h, 0] = dgcr[i]
            (dglf, dglr)[d][h, 0] = dgl[i]
            ds_scr[i] = ds[i]

    tok = jax.ShapeDtypeStruct((ROWS, 512), F32)
    return _scan_call(
        body, "delta_bwd",
        _both(sp, ["tok", "tok", "tok", "row", "row", "one", "state", "tinv", "tok"]),
        _both(sp, ["tok", "tok", "tok", "row", "row", "one"]),
        [tok] * 6 + [jax.ShapeDtypeStruct((4, N_CHUNK, 1, CHUNK), F32)] * 4
        + [jax.ShapeDtypeStruct((4, N_CHUNK, 1, 1), F32)] * 2,
        [q, q, k, k, v, v, *beta, *gc, *gl, *ssave, *tsave, do, do], ride)


def _ret_chains(q, k, v, dm, qs, ks, cd, s):
    n = range(len(q))
    a = [_bdot(q[i], k[i], 1, 1) * dm[i] for i in n]
    o = [_bdot(a[i], v[i], 1, 0) + _bdot(q[i] * qs[i], s[i], 1, 0) for i in n]
    s_new = [s[i] * cd[i] + _bdot(k[i] * ks[i], v[i], 0, 0) for i in n]
    return o, s_new


def _ret_const_specs():
    return [pl.BlockSpec((N_CHAIN, CHUNK, CHUNK), lambda n: (0, 0, 0)), pl.BlockSpec((N_CHAIN, CHUNK, 1), lambda n: (0, 0, 0)),
            pl.BlockSpec((N_CHAIN, CHUNK, 1), lambda n: (0, 0, 0)), pl.BlockSpec((N_CHAIN, 1, 1), lambda n: (0, 0, 0))]


def _ret_fwd_call(q, k, v, v_col, dm, qs, ks, cd, ride=None):
    sp = _scan_specs(lambda n: n, v_col)

    def body(qf, qr, kf, kr, vf, vr, dm_ref, qs_ref, ks_ref, cd_ref, of, orv, ssf, ssr, s_scr):
        @pl.when(pl.program_id(0) == 0)
        def _():
            s_scr[...] = jnp.zeros_like(s_scr)

        (qc, kc, vc), _ = _chain_loads([(qf, qr), (kf, kr), (vf, vr)], [])
        ss = [s_scr[i] for i in range(N_CHAIN)]
        consts = [[r[i] for i in range(N_CHAIN)] for r in (dm_ref, qs_ref, ks_ref, cd_ref)]
        o, s_new = _ret_chains(qc, kc, vc, *consts, ss)
        for i in range(N_CHAIN):
            d, h = i // 4, i % 4
            (ssf, ssr)[d][h] = ss[i]
            (of, orv)[d][:, h * 128:(h + 1) * 128] = o[i]
            s_scr[i] = s_new[i]

    return _scan_call(
        body, "ret_fwd",
        _both(sp, ["tok", "tok", "tokv"]) + _ret_const_specs(),
        _both(sp, ["tok", "state"]),
        [jax.ShapeDtypeStruct((ROWS, 512), F32)] * 2 + [jax.ShapeDtypeStruct((N_CHUNK, 4, 128, 128), F32)] * 2,
        [q, q, k, k, v, v, dm, qs, ks, cd], ride)


def _ret_bwd_call(q, k, v, v_col, dm, qs, ks, cd, ssave, do, ride=None):
    sp = _scan_specs(lambda n: N_CHUNK - 1 - n, v_col)

    def body(qf, qr, kf, kr, vf, vr, dm_ref, qs_ref, ks_ref, cd_ref, ssf, ssr, dof, dor,
             dqf, dqr, dkf, dkr, dvf, dvr, ddm_ref, dqs_ref, dks_ref, dcd_ref, ds_scr):
        @pl.when(pl.program_id(0) == 0)
        def _():
            ds_scr[...] = jnp.zeros_like(ds_scr)
            ddm_ref[...] = jnp.zeros_like(ddm_ref)
            dqs_ref[...] = jnp.zeros_like(dqs_ref)
            dks_ref[...] = jnp.zeros_like(dks_ref)
            dcd_ref[...] = jnp.zeros_like(dcd_ref)

        (qc, kc, vc, dos), (ss,) = _chain_loads([(qf, qr), (kf, kr), (vf, vr), (dof, dor)], [(ssf, ssr)])
        consts = [[r[i] for i in range(N_CHAIN)] for r in (dm_ref, qs_ref, ks_ref, cd_ref)]
        _, vjp = jax.vjp(_ret_chains, qc, kc, vc, *consts, ss)
        dq, dk, dv, ddm, dqs, dks, dcd, ds = vjp((dos, [ds_scr[i] for i in range(N_CHAIN)]))
        for i in range(N_CHAIN):
            d, h = i // 4, i % 4
            hs = slice(h * 128, (h + 1) * 128)
            (dqf, dqr)[d][:, hs] = dq[i]
            (dkf, dkr)[d][:, hs] = dk[i]
            (dvf, dvr)[d][:, hs] = dv[i]
            ddm_ref[i] += ddm[i]
            dqs_ref[i] += dqs[i]
            dks_ref[i] += dks[i]
            dcd_ref[i] += dcd[i]
            ds_scr[i] = ds[i]

    tok = jax.ShapeDtypeStruct((ROWS, 512), F32)
    return _scan_call(
        body, "ret_bwd",
        _both(sp, ["tok", "tok", "tokv"]) + _ret_const_specs() + _both(sp, ["state", "tok"]),
        _both(sp, ["tok", "tok", "tok"]) + _ret_const_specs(),
        [tok] * 6 + [jax.ShapeDtypeStruct((N_CHAIN, CHUNK, CHUNK), F32), jax.ShapeDtypeStruct((N_CHAIN, CHUNK, 1), F32),
                     jax.ShapeDtypeStruct((N_CHAIN, CHUNK, 1), F32), jax.ShapeDtypeStruct((N_CHAIN, 1, 1), F32)],
        [q, q, k, k, v, v, dm, qs, ks, cd, *ssave, do, do], ride)


N_QBLK = ROWS // B_BLOCK
CTX_QBLK = CTX_LEN // B_BLOCK


def _attn_heads(q, kc, vc, kw, vw, sink, valid):
    n = range(len(q))
    scale = B_HD ** -0.5
    s_c = [_bdot(q[i], kc[i], 1, 1) * scale for i in n]
    s_w = [jnp.where(valid, _bdot(q[i], kw[i], 1, 1) * scale, NEG) for i in n]
    m = [lax.stop_gradient(jnp.maximum(jnp.maximum(jnp.max(s_c[i], axis=-1, keepdims=True), sink[i]),
                                       jnp.max(s_w[i], axis=-1, keepdims=True))) for i in n]
    e_c = [jnp.exp(s_c[i] - m[i]) for i in n]
    e_w = [jnp.exp(s_w[i] - m[i]) for i in n]
    den = [jnp.sum(e_c[i], axis=-1, keepdims=True) + jnp.sum(e_w[i], axis=-1, keepdims=True)
           + jnp.exp(sink[i] - m[i]) for i in n]
    return [(_bdot(e_c[i], vc[i], 1, 0) + _bdot(e_w[i], vw[i], 1, 0)) / den[i] for i in n]


def _attn_loads(q_ref, kv_ref, sink_ref, start):
    q, kc, vc, kw, vw, sink = [], [], [], [], [], []
    for hk in range(B_KV_HEADS):
        ks = slice(hk * B_HD, (hk + 1) * B_HD)
        vs = slice(128 + hk * B_HD, 128 + (hk + 1) * B_HD)
        grp = (kv_ref[0:CTX_LEN, ks], kv_ref[0:CTX_LEN, vs],
               kv_ref[pl.ds(start, 3 * B_BLOCK), ks], kv_ref[pl.ds(start, 3 * B_BLOCK), vs])
        for g in range(4):
            h = hk * 4 + g
            q.append(q_ref[:, h * B_HD:(h + 1) * B_HD])
            for lst, val in zip((kc, vc, kw, vw), grp):
                lst.append(val)
            sink.append(jnp.full((1, 1), sink_ref[h], F32))
    return q, kc, vc, kw, vw, sink


def _window(blk):
    xblk = blk - CTX_QBLK
    first = jnp.clip((xblk - 1) * B_BLOCK, 0, SEQ - 3 * B_BLOCK)
    qpos = xblk * B_BLOCK + lax.broadcasted_iota(jnp.int32, (B_BLOCK, 3 * B_BLOCK), 0)
    kpos = first + lax.broadcasted_iota(jnp.int32, (B_BLOCK, 3 * B_BLOCK), 1)
    far = jnp.where(blk >= CTX_QBLK, 0, 2 * SEQ)
    valid = jnp.abs(kpos - qpos) + far <= WINDOW
    return pl.multiple_of(first + CTX_LEN, B_BLOCK), valid


def _attn_specs():
    qspec = pl.BlockSpec((B_BLOCK, 512), lambda i: (i, 0))
    kvspec = pl.BlockSpec((ROWS, 256), lambda i: (0, 0))
    return qspec, kvspec, pl.BlockSpec(memory_space=pltpu.SMEM)


def _attn_fwd_call(q, kv, sink, ride=None):
    def body(q_ref, kv_ref, sink_ref, o_ref):
        start, valid = _window(pl.program_id(0))
        out = _attn_heads(*_attn_loads(q_ref, kv_ref, sink_ref, start), valid)
        for h in range(B_Q_HEADS):
            o_ref[:, h * B_HD:(h + 1) * B_HD] = out[h]

    qspec, kvspec, sspec = _attn_specs()
    body, r_in, r_out, r_shape, r_scratch = _riding(body, 3, 1, 0, ride, (N_QBLK,))
    return pl.pallas_call(
        body,
        grid=(N_QBLK,),
        in_specs=[qspec, kvspec, sspec] + r_in,
        out_specs=[qspec] + r_out,
        out_shape=[jax.ShapeDtypeStruct((ROWS, 512), F32)] + r_shape,
        scratch_shapes=r_scratch,
        compiler_params=_cparams(("arbitrary",)),
        name="attn_fwd",
    )(q, kv, sink, *(ride[0] if ride else []))


def _attn_bwd_call(q, kv, sink, do, ride=None):
    def body(q_ref, kv_ref, sink_ref, do_ref, dq_ref, dkv_ref, dsink_ref):
        @pl.when(pl.program_id(0) == 0)
        def _():
            dkv_ref[...] = jnp.zeros_like(dkv_ref)
            dsink_ref[...] = jnp.zeros_like(dsink_ref)

        start, valid = _window(pl.program_id(0))
        _, vjp = jax.vjp(functools.partial(_attn_heads, valid=valid), *_attn_loads(q_ref, kv_ref, sink_ref, start))
        dq, dkc, dvc, dkw, dvw, dsink = vjp([do_ref[:, h * B_HD:(h + 1) * B_HD] for h in range(B_Q_HEADS)])
        for h in range(B_Q_HEADS):
            dq_ref[:, h * B_HD:(h + 1) * B_HD] = dq[h]
            dsink_ref[h:h + 1, :] += jnp.broadcast_to(dsink[h], (1, 128))
        for hk in range(B_KV_HEADS):
            ks = slice(hk * B_HD, (hk + 1) * B_HD)
            vs = slice(128 + hk * B_HD, 128 + (hk + 1) * B_HD)
            grp = lambda parts: parts[hk * 4] + parts[hk * 4 + 1] + parts[hk * 4 + 2] + parts[hk * 4 + 3]
            dkv_ref[0:CTX_LEN, ks] += grp(dkc)
            dkv_ref[0:CTX_LEN, vs] += grp(dvc)
            dkv_ref[pl.ds(start, 3 * B_BLOCK), ks] += grp(dkw)
            dkv_ref[pl.ds(start, 3 * B_BLOCK), vs] += grp(dvw)

    qspec, kvspec, sspec = _attn_specs()
    body, r_in, r_out, r_shape, r_scratch = _riding(body, 4, 3, 0, ride, (N_QBLK,))
    return pl.pallas_call(
        body,
        grid=(N_QBLK,),
        in_specs=[qspec, kvspec, sspec, qspec] + r_in,
        out_specs=[qspec, kvspec, pl.BlockSpec((8, 128), lambda i: (0, 0))] + r_out,
        out_shape=[jax.ShapeDtypeStruct((ROWS, 512), F32), jax.ShapeDtypeStruct((ROWS, 256), F32),
                   jax.ShapeDtypeStruct((8, 128), F32)] + r_shape,
        scratch_shapes=r_scratch,
        compiler_params=_cparams(("arbitrary",)),
        name="attn_bwd",
    )(q, kv, sink, do, *(ride[0] if ride else []))


def _my_id():
    return 4 * lax.axis_index("x") + 2 * lax.axis_index("y") + lax.axis_index("c")


def _peer(k):
    x, y, c = lax.axis_index("x"), lax.axis_index("y"), lax.axis_index("c")
    return (1 - x if k & 4 else x, 1 - y if k & 2 else y, 1 - c if k & 1 else c)


SAME_CORE_PEERS = (2, 4, 6)


def _scatter_copies(ins, outs, sems):
    send_sems, recv_sems, local_sems = sems
    me = _my_id()
    own, remote = [], []
    for a in range(len(ins)):
        own.append(pltpu.make_async_copy(ins[a].at[me], outs[a].at[me], local_sems.at[a]))
        for k in range(1, N_DEV):
            peer_slot = jnp.bitwise_xor(me, k)
            common = dict(src_ref=ins[a].at[peer_slot], send_sem=send_sems.at[a, k - 1],
                          recv_sem=recv_sems.at[a, k - 1], device_id=_peer(k), device_id_type=MESH)
            remote.append((pltpu.make_async_remote_copy(dst_ref=outs[a].at[me], **common),
                           pltpu.make_async_remote_copy(dst_ref=outs[a].at[peer_slot], **common)))
    return own, remote


def _gather_copy(outs, sems, a, k, src, slot, to):
    return pltpu.make_async_remote_copy(src_ref=src, dst_ref=outs[a].at[slot], send_sem=sems[0].at[a, k - 1],
                                        recv_sem=sems[1].at[a, k - 1], device_id=_peer(to), device_id_type=MESH)


def _gather_first_copies(ins, outs, sems):
    me = _my_id()
    own = [pltpu.make_async_copy(ins[a], outs[a].at[me], sems[2].at[a]) for a in range(len(ins))]
    direct = [_gather_copy(outs, sems, a, k, ins[a], me, k) for a in range(len(ins)) for k in (1,) + SAME_CORE_PEERS]
    return own, direct


def _exchange_start(ins, outs, sems, gather):
    own, remote = _gather_first_copies(ins, outs, sems) if gather else _scatter_copies(ins, outs, sems)
    for cp in own:
        cp.start()
    for cp in remote:
        (cp if gather else cp[0]).start()


def _exchange_wait(ins, outs, sems, gather):
    if not gather:
        own, remote = _scatter_copies(ins, outs, sems)
        for _, arrival in remote:
            arrival.wait_recv()
        for send, _ in remote:
            send.wait_send()
        for cp in own:
            cp.wait()
        return
    me = _my_id()
    own, direct = _gather_first_copies(ins, outs, sems)
    passed = []
    for a in range(len(ins)):
        for k in SAME_CORE_PEERS:
            origin = jnp.bitwise_xor(me, k)
            _gather_copy(outs, sems, a, k, ins[a], origin, k).wait_recv()
            onward = _gather_copy(outs, sems, a, k + 1, outs[a].at[origin], origin, 1)
            onward.start()
            passed.append(onward)
    for a in range(len(ins)):
        for k in (1, 3, 5, 7):
            _gather_copy(outs, sems, a, k, ins[a], jnp.bitwise_xor(me, k), 1).wait_recv()
    for cp in direct + passed:
        cp.wait_send()
    for cp in own:
        cp.wait()


def _exchange_plumbing(arrays, gather):
    n = len(arrays)
    hbm = [pl.BlockSpec(memory_space=pltpu.HBM)] * n
    out_shape = [jax.ShapeDtypeStruct((N_DEV,) + (a.shape if gather else a.shape[1:]), a.dtype) for a in arrays]
    sems = [pltpu.SemaphoreType.DMA((n, N_DEV - 1)), pltpu.SemaphoreType.DMA((n, N_DEV - 1)),
            pltpu.SemaphoreType.DMA((n,))]
    return hbm, out_shape, sems


def _exchange(arrays, gather, name):
    n = len(arrays)

    def body(*refs):
        ins, outs, sems = refs[:n], refs[n:2 * n], refs[2 * n:]
        _exchange_start(ins, outs, sems, gather)
        _exchange_wait(ins, outs, sems, gather)

    hbm, out_shape, sems = _exchange_plumbing(arrays, gather)
    return pl.pallas_call(
        body,
        in_specs=hbm,
        out_specs=hbm,
        out_shape=out_shape,
        scratch_shapes=sems,
        compiler_params=pltpu.CompilerParams(has_side_effects=True),
        name=name,
    )(*arrays)


def _riding(body, n_in, n_out, n_scratch, ride, grid):
    if ride is None:
        return body, [], [], [], []
    arrays, gather = ride
    n = len(arrays)

    def at(step_of):
        hit = pl.program_id(0) == step_of(grid[0])
        for d in range(1, len(grid)):
            hit = jnp.logical_and(hit, pl.program_id(d) == step_of(grid[d]))
        return hit

    def wrapped(*refs):
        ins, rin = refs[:n_in], refs[n_in:n_in + n]
        outs = refs[n_in + n:n_in + n + n_out]
        rout = refs[n_in + n + n_out:n_in + 2 * n + n_out]
        scratch = refs[n_in + 2 * n + n_out:n_in + 2 * n + n_out + n_scratch]
        sems = refs[n_in + 2 * n + n_out + n_scratch:]

        @pl.when(at(lambda size: 0))
        def _():
            _exchange_start(rin, rout, sems, gather)

        body(*ins, *outs, *scratch)

        @pl.when(at(lambda size: size - 1))
        def _():
            _exchange_wait(rin, rout, sems, gather)

    hbm, out_shape, sems = _exchange_plumbing(arrays, gather)
    return wrapped, hbm, hbm, out_shape, sems


def _sum_contributions(c_ref):
    g = c_ref[0].astype(F32)
    for j in range(1, N_DEV):
        g = g + c_ref[j].astype(F32)
    return g


def _adamw_update(g, w_ref, m_ref, v_ref, g_ref, d_ref, nm_ref, nv_ref):
    m_new = ADAM_B1 * m_ref[...] + (1.0 - ADAM_B1) * g
    v_new = ADAM_B2 * v_ref[...] + (1.0 - ADAM_B2) * (g * g)
    m_hat = m_new / (1.0 - ADAM_B1 ** ADAM_STEP)
    v_hat = v_new / (1.0 - ADAM_B2 ** ADAM_STEP)
    g_ref[...] = g
    d_ref[...] = -ADAM_LR * (m_hat / (jnp.sqrt(v_hat) + ADAM_EPS) + ADAM_WD * w_ref[...])
    nm_ref[...] = m_new
    nv_ref[...] = v_new


def _adamw_layers(w, m, v, contrib0, contrib1, name):
    _, r, c = w.shape
    br = _pick(r, (256, 128, 64, 32, 16, 8))
    nb = r // br

    def body(w_ref, m_ref, v_ref, c0_ref, c1_ref, g_ref, d_ref, nm_ref, nv_ref):
        g = jnp.where(pl.program_id(0) == 0, _sum_contributions(c0_ref), _sum_contributions(c1_ref))
        _adamw_update(g, w_ref, m_ref, v_ref, g_ref, d_ref, nm_ref, nv_ref)

    spec = pl.BlockSpec((None, br, c), lambda l, i: (l, i, 0))
    return pl.pallas_call(
        body,
        grid=(DEPTH, nb),
        in_specs=[spec, spec, spec,
                  pl.BlockSpec((N_DEV, br, c), lambda l, i: (0, jnp.where(l == 0, i, nb - 1), 0)),
                  pl.BlockSpec((N_DEV, br, c), lambda l, i: (0, jnp.where(l == 1, i, 0), 0))],
        out_specs=[spec] * 4,
        out_shape=[jax.ShapeDtypeStruct(w.shape, F32)] * 4,
        compiler_params=_cparams(("arbitrary", "arbitrary")),
        name=name,
    )(w, m, v, contrib0, contrib1)


def _adamw(w, m, v, contrib, name):
    r, c = w.shape
    br = _pick(r, (256, 128, 64, 32, 16, 8))

    def body(w_ref, m_ref, v_ref, c_ref, g_ref, d_ref, nm_ref, nv_ref):
        _adamw_update(_sum_contributions(c_ref), w_ref, m_ref, v_ref, g_ref, d_ref, nm_ref, nv_ref)

    spec = pl.BlockSpec((br, c), lambda i: (i, 0))
    cspec = pl.BlockSpec((N_DEV, br, c), lambda i: (0, i, 0))
    return pl.pallas_call(
        body,
        grid=(r // br,),
        in_specs=[spec, spec, spec, cspec],
        out_specs=[spec] * 4,
        out_shape=[jax.ShapeDtypeStruct((r, c), F32)] * 4,
        compiler_params=_cparams(("parallel",)),
        name=name,
    )(w, m, v, contrib)


def _silu(x):
    return x * jax.nn.sigmoid(x)


def _rope_angles(pos, n_freq):
    inv = ROPE_BASE ** (-jnp.arange(n_freq, dtype=F32) / n_freq)
    return pos[:, None] * inv[None, :]


def _with_ctx_rows(cos, sin):
    return (jnp.concatenate([jnp.ones((CTX_LEN, 128), F32), cos], axis=0),
            jnp.concatenate([jnp.zeros((CTX_LEN, 128), F32), sin], axis=0))


def _rope_tables():
    rows_n = SEQ // GRID_W
    rows = jnp.repeat(jnp.arange(rows_n, dtype=F32), GRID_W)
    cols = jnp.tile(jnp.arange(GRID_W, dtype=F32), rows_n)
    ang_r = _rope_angles(rows, B_HD // 4)
    ang_c = _rope_angles(cols, B_HD // 4)
    cos_b = jnp.tile(jnp.concatenate([jnp.cos(ang_r)] * 2 + [jnp.cos(ang_c)] * 2, axis=1), (1, 2))
    sin_b = jnp.tile(jnp.concatenate([-jnp.sin(ang_r), jnp.sin(ang_r), -jnp.sin(ang_c), jnp.sin(ang_c)], axis=1), (1, 2))
    ang = _rope_angles(jnp.arange(SEQ, dtype=F32), C_HD // 2)
    cos_c = jnp.concatenate([jnp.cos(ang)] * 2, axis=1)
    sin_c = jnp.concatenate([-jnp.sin(ang), jnp.sin(ang)], axis=1)
    return _with_ctx_rows(cos_b, sin_b), _with_ctx_rows(cos_c, sin_c)


def _halves(a):
    return a[:4], a[4:]


def _delta_gates(ab, a_log, dt_bias):
    beta = jax.nn.sigmoid(ab[:, :8])
    g = -jnp.exp(a_log)[None, :] * jax.nn.softplus(ab[:, 8:] + dt_bias[None, :])
    gch = g.reshape(N_CHUNK, CHUNK, 8)
    tri = jnp.tril(jnp.ones((CHUNK, CHUNK), F32))
    fwd = jnp.einsum("ij,cjh->cih", tri, gch[..., :4], precision=HIGHEST)
    bwd = jnp.einsum("ji,cjh->cih", tri, gch[..., 4:], precision=HIGHEST)
    gc = jnp.concatenate([fwd, bwd], axis=-1)
    gl = jnp.sum(gch, axis=1)
    rows = lambda a: _halves(a.transpose(2, 0, 1)[:, :, None, :])
    return rows(beta.reshape(N_CHUNK, CHUNK, 8)), rows(gc), _halves(gl.T[:, :, None, None])


def _ret_consts(c_decay):
    lg = jax.nn.log_sigmoid(c_decay)
    idx = jnp.arange(CHUNK, dtype=F32)
    diff = idx[:, None] - idx[None, :]
    lgf, lgb = lg[:4, None, None], lg[4:, None, None]
    dm = jnp.concatenate([jnp.exp(jnp.where(diff >= 0, diff * lgf, -jnp.inf)),
                          jnp.exp(jnp.where(diff <= 0, -diff * lgb, -jnp.inf))], axis=0)
    qs = jnp.concatenate([jnp.exp((idx + 1.0)[None, :] * lg[:4, None]),
                          jnp.exp((CHUNK - idx)[None, :] * lg[4:, None])], axis=0)[:, :, None]
    ks = jnp.concatenate([jnp.exp((CHUNK - 1.0 - idx)[None, :] * lg[:4, None]),
                          jnp.exp(idx[None, :] * lg[4:, None])], axis=0)[:, :, None]
    return dm, qs, ks, jnp.exp(CHUNK * lg)[:, None, None]


A_PIECES = ((0, True, A_DK ** -0.5, "a_q"), (1, True, 1.0, "a_k"), (2, False, 1.0, "a_v"))
B_ROPE_COLS = [C_BQ // 512, C_BKV // 256, 0, 0]
C_ROPE_COLS = [C_CQ // 512, C_CK // 512, 0, 0]
MERGE_COLS = [0, 0, 0, C_MERGE // 1024, C_MERGE // 1024 + 1, C_MERGE // 1024 + 2]


def _conv8(conv_w):
    return jnp.pad(conv_w, ((0, 8 - A_CONV), (0, 0)))


W_IN_TILES = {"nn": (2176, 512, 1024), "nt": (1088, 1024, 2176), "db": (1024, 512, ROWS)}


def _core_forward(h, w16, p, rides):
    res = _matmul(h, w16, "w_in", "nn", W_IN_TILES["nn"], ride=rides.get("w_in"))
    proj, rode = (res[0], {"w_in": res[1:]}) if "w_in" in rides else (res, {})
    wb = p["w_branch"] if "w_in" not in rides else _unshard_layer("w_branch", rode["w_in"][0])
    (cos_b, sin_b), (cos_c, sin_c) = _rope_tables()
    conv8 = _conv8(p["a_conv_w"])
    q, k, v = [_a_prep_fwd(proj, conv8, col, nrm, scl, nm) for col, nrm, scl, nm in A_PIECES]
    gates = _delta_gates(proj[:, C_AB:C_AB + 16], p["a_log"], p["a_dt_bias"])
    res = _delta_fwd_call(q, k, v, *gates, ride=rides.get("delta"))
    (of, orv, ssf, ssr, tsf, tsr), rode["delta"] = res[:6], res[6:]
    (y_a,) = _a_out.fwd([(of, orv), proj], [p["a_norm_w"][None, :]], [0, C_AZ // 512])

    qb, kvb = _b_rope.fwd([proj, proj, cos_b, sin_b], [], B_ROPE_COLS)
    res = _attn_fwd_call(qb, kvb, p["b_sink"], ride=rides.get("attn"))
    ob, rode["attn"] = res[0], res[1:]
    (y_b,) = _b_out.fwd([ob, proj], [], [0, C_BZ // 512])

    qc, kc = _c_rope.fwd([proj, proj, cos_c, sin_c], [], C_ROPE_COLS)
    res = _ret_fwd_call(qc, kc, proj, C_CV // 512, *_ret_consts(p["c_decay"]), ride=rides.get("ret"))
    (cf, cr, csf, csr), rode["ret"] = res[:4], res[4:]
    (y_c,) = _c_out.fwd([(cf, cr), proj], [p["c_norm_w"][None, :]], [0, C_CZ // 512])

    (merged,) = _branch_merge.fwd([y_a, y_b, y_c, proj, proj, proj], [wb[0], wb[1], wb[2]], MERGE_COLS)
    saved = dict(proj=proj, q=q, k=k, v=v, of=of, orv=orv, ss=(ssf, ssr), ts=(tsf, tsr), qb=qb, kvb=kvb, ob=ob,
                 qc=qc, kc=kc, cf=cf, cr=cr, cs=(csf, csr), y=(y_a, y_b, y_c), wb=wb)
    return merged, saved, rode


def _core_backward(h, w16, p, s, dmerged, rides, branch_rides_in_attn=False):
    proj, wb = s["proj"], s["wb"]
    (cos_b, sin_b), (cos_c, sin_c) = _rope_tables()
    conv8 = _conv8(p["a_conv_w"])
    y_a, y_b, y_c = s["y"]
    rode = {}

    (*dy, dma, dmb, dmc), dwb = _branch_merge.bwd([y_a, y_b, y_c, proj, proj, proj], [wb[0], wb[1], wb[2]],
                                                   [dmerged], MERGE_COLS, bf16_rows=(3, 4, 5))
    dwb = jnp.stack(dwb)

    consts, consts_vjp = jax.vjp(_ret_consts, p["c_decay"])
    (do_c, dcz), (dcnw,) = _c_out.bwd([(s["cf"], s["cr"]), proj], [p["c_norm_w"][None, :]], [dy[2]],
                                      [0, C_CZ // 512], bf16_rows=(1,))
    g = _ret_bwd_call(s["qc"], s["kc"], proj, C_CV // 512, *consts, s["cs"], do_c, ride=rides.get("ret"))
    rode["ret"] = g[10:]
    (dcq, dck), _ = _c_rope.bwd([proj, proj, cos_c, sin_c], [], [(g[0], g[1]), (g[2], g[3])], C_ROPE_COLS,
                                bf16_rows=(0, 1))
    dcv = (g[4] + g[5]).astype(BF16)
    (dc_decay,) = consts_vjp(tuple(g[6:10]))

    (dob, dbz), _ = _b_out.bwd([s["ob"], proj], [], [dy[1]], [0, C_BZ // 512], bf16_rows=(1,))
    attn_ride = rides.get("attn")
    if branch_rides_in_attn:
        attn_ride = (list(attn_ride[0]) + [_reshard_layer("w_branch", dwb).astype(BF16)], attn_ride[1])
    res = _attn_bwd_call(s["qb"], s["kvb"], p["b_sink"], dob, ride=attn_ride)
    (dqb, dkvb, dsink), rode["attn"] = res[:3], res[3:]
    (dbq, dbkv), _ = _b_rope.bwd([proj, proj, cos_b, sin_b], [], [dqb, dkvb], B_ROPE_COLS, bf16_rows=(0, 1))

    ab = proj[:, C_AB:C_AB + 16]
    gates, gates_vjp = jax.vjp(_delta_gates, ab, p["a_log"], p["a_dt_bias"])
    (do_a, daz), (danw,) = _a_out.bwd([(s["of"], s["orv"]), proj], [p["a_norm_w"][None, :]], [dy[0]],
                                      [0, C_AZ // 512], bf16_rows=(1,))
    g = _delta_bwd_call(s["q"], s["k"], s["v"], *gates, s["ss"], s["ts"], do_a, ride=rides.get("delta"))
    rode["delta"] = g[12:]
    dgates = ((g[6], g[7]), (g[8], g[9]), (g[10], g[11]))
    dab, da_log, ddt = gates_vjp(dgates)
    dpre, dconv = [], []
    for (col, nrm, scl, nm), df, dr in zip(A_PIECES, (g[0], g[2], g[4]), (g[1], g[3], g[5])):
        dx, dw = _a_prep_bwd(proj, conv8, col, nrm, scl, df, dr, nm)
        dpre.append(dx)
        dconv.append(dw[:A_CONV])

    dproj = jnp.concatenate(dpre + [daz, dbq, dbz, dcq, dck, dcv, dcz, dma, dmb, dmc, dbkv,
                                    jnp.pad(dab, ((0, 0), (0, IN_PAD - C_AB - 16))).astype(BF16)], axis=1)
    dh = _matmul(dproj, w16, "w_in_da", "nt", W_IN_TILES["nt"])
    dw = _matmul(h.T.astype(BF16), dproj, "w_in_db", "nn", W_IN_TILES["db"])
    dp = dict(a_conv_w=jnp.concatenate(dconv, axis=1), a_log=da_log, a_dt_bias=ddt, a_norm_w=danw[0],
              b_sink=dsink[:, 0], c_decay=dc_decay, c_norm_w=dcnw[0], w_branch=dwb)
    return dh, dw, dp, rode


CORE_PARAMS = ("a_conv_w", "a_log", "a_dt_bias", "a_norm_w", "b_sink", "c_decay", "c_norm_w", "w_branch")


W_IN_SHARD = IN_WIDTH // N_DEV
W_IN_RUNS = ((0, 2048, 0), (2064, 512, C_BQ), (2832, 512, C_BZ), (3344, 5120, C_CQ), (2576, 256, C_BKV),
             (2048, 16, C_AB))


def _shard_overlap(start, width, j):
    lo, hi = max(start, j * W_IN_SHARD), min(start + width, (j + 1) * W_IN_SHARD)
    return (lo, hi) if lo < hi else None


def _w_in_from_shards(g):
    parts = []
    for start, width, _ in W_IN_RUNS:
        for j in range(N_DEV):
            span = _shard_overlap(start, width, j)
            if span:
                parts.append(g[j, :, span[0] - j * W_IN_SHARD:span[1] - j * W_IN_SHARD])
    parts.append(jnp.zeros((D_MODEL, IN_PAD - IN_WIDTH), g.dtype))
    return jnp.concatenate(parts, axis=1)


def _w_in_blocks(dw):
    blocks = []
    for j in range(N_DEV):
        parts = []
        for start, width, pad in sorted(W_IN_RUNS):
            span = _shard_overlap(start, width, j)
            if span:
                parts.append(dw[:, pad + span[0] - start:pad + span[1] - start])
        blocks.append(jnp.concatenate(parts, axis=1))
    return jnp.stack(blocks)


LAYER_SHARDED = ("w_ada", "w_in", "w_branch", "w_out")


def _unshard_layer(name, g):
    if name == "w_branch":
        return g.transpose(1, 2, 0, 3).reshape(3, BR_WIDTH, D_MODEL)
    if name == "w_out":
        return g.reshape(D_MODEL, D_MODEL)
    return g.transpose(1, 0, 2).reshape(D_MODEL, -1)


def _reshard_layer(name, w):
    if name == "w_branch":
        return w.reshape(3, BR_WIDTH, N_DEV, D_MODEL // N_DEV).transpose(2, 0, 1, 3)
    if name == "w_out":
        return w.reshape(N_DEV, D_MODEL // N_DEV, D_MODEL)
    return w.reshape(D_MODEL, N_DEV, -1).transpose(1, 0, 2)


def _layer_weights(gathered):
    out = {n: _unshard_layer(n, g) for n, g in gathered.items() if n != "w_in"}
    out["w_in16"] = _w_in_from_shards(gathered["w_in"])
    return out


def _grad_blocks(name, g):
    return (_w_in_blocks(g) if name == "w_in" else _reshard_layer(name, g)).astype(BF16)


def _forward_backward(small, layer0, shards0, shards1, x, c, ctx, loss_target):
    c_ctx = small["c_ctx"]
    sc16 = jnp.zeros((16, D_MODEL), F32).at[0].set(_silu(c)).at[1].set(_silu(c_ctx))
    xs = jnp.concatenate([ctx, x], axis=0)
    weights = [dict(layer0), None]
    layers = []
    for l in range(DEPTH):
        wl = weights[l]
        mod16 = _matmul(sc16, wl["w_ada"], "ada") + small["b_ada"][l][None, :]
        mod_cx = jnp.stack([mod16[1], mod16[0]])
        shift, scale, gate = jnp.split(mod_cx, 3, axis=1)
        nw = small["norm_w"][l][None, :]
        (h,) = _norm_mod.fwd([xs], [nw, shift, scale])
        p = {n: small[n][l] for n in CORE_PARAMS if n != "w_branch"}
        p["w_branch"] = wl.get("w_branch")
        rides = {}
        if l == 0:
            rides = {"w_in": ([shards0["w_branch"], shards0["w_out"]], True), "delta": ([shards1["w_in"]], True),
                     "attn": ([shards1["w_ada"]], True), "ret": ([shards1["w_branch"], shards1["w_out"]], True)}
        merged, saved, rode = _core_forward(h, wl["w_in16"], p, rides)
        if l == 0:
            wl["w_out"] = _unshard_layer("w_out", rode["w_in"][1])
            weights[1] = _layer_weights(dict(w_in=rode["delta"][0], w_ada=rode["attn"][0],
                                             w_branch=rode["ret"][0], w_out=rode["ret"][1]))
        (xs_next,) = _out_residual.fwd([xs, merged], [wl["w_out"], gate])
        layers.append(dict(xs=xs, h=h, p=p, saved=saved, merged=merged, gate=gate, nw=nw, shift=shift, scale=scale))
        xs = xs_next
    fw = small["final_norm_w"][None, :]
    xs = xs[CTX_LEN:]
    (per_row,) = _loss_rows.fwd([xs, loss_target], [fw])
    loss = jnp.sum(per_row[:, 0])

    d_per_row = jnp.zeros((SEQ, 128), F32).at[:, 0].set(1.0)
    (dxs,), (dfw,) = _loss_rows.bwd([xs, loss_target], [fw], [d_per_row])
    dxs = jnp.pad(dxs, ((CTX_LEN, 0), (0, 0)))
    small_names = tuple(n for n in CORE_PARAMS if n != "w_branch") + ("b_ada", "norm_w")
    dsmall = {n: [None] * DEPTH for n in small_names}
    dlayer = [None] * DEPTH
    contrib0 = contrib1 = None
    dsc16 = jnp.zeros((16, D_MODEL), F32)
    for l in reversed(range(DEPTH)):
        s, wl = layers[l], weights[l]
        (dres, dmerged), (dw_out, dgate) = _out_residual.bwd([s["xs"], s["merged"]], [wl["w_out"], s["gate"]], [dxs])
        rides = {}
        if l == 0:
            blocks1 = {n: _grad_blocks(n, g) for n, g in dlayer[1].items()}
            rides = {"ret": ([blocks1["w_branch"], blocks1["w_out"]], False),
                     "attn": ([_reshard_layer("w_out", dw_out).astype(BF16), blocks1["w_ada"]], False),
                     "delta": ([blocks1["w_in"]], False)}
        dh, dw_in, dp, rode = _core_backward(s["h"], wl["w_in16"], s["p"], s["saved"], dmerged, rides,
                                             branch_rides_in_attn=(l == 0))
        if l == 0:
            contrib1 = dict(w_in=rode["delta"][0], w_ada=rode["attn"][1], w_branch=rode["ret"][0],
                            w_out=rode["ret"][1])
            contrib0 = dict(w_out=rode["attn"][0], w_branch=rode["attn"][2])
        (dxn,), (dnw, dshift, dscale) = _norm_mod.bwd([s["xs"]], [s["nw"], s["shift"], s["scale"]], [dh])
        dxs = dres + dxn
        dmod_cx = jnp.concatenate([dshift, dscale, dgate], axis=1)
        dmod16 = jnp.zeros((16, 3 * D_MODEL), F32).at[0].set(dmod_cx[1]).at[1].set(dmod_cx[0])
        dsc16 = dsc16 + _matmul(dmod16, wl["w_ada"], "ada_da", "nt")
        dlayer[l] = dict(w_ada=_matmul(sc16, dmod16, "ada_db", "tn"), w_in=dw_in,
                         w_branch=dp["w_branch"], w_out=dw_out)
        for n in small_names:
            if n in dp:
                dsmall[n][l] = dp[n]
        dsmall["norm_w"][l] = dnw[0]
        dsmall["b_ada"][l] = dmod_cx[0] + dmod_cx[1]
    gsmall = {n: jnp.stack(v) for n, v in dsmall.items()}
    gsmall["final_norm_w"] = dfw[0]
    sig = jax.nn.sigmoid(c_ctx)
    gsmall["c_ctx"] = dsc16[1] * sig * (1.0 + c_ctx * (1.0 - sig))
    return loss, dxs[CTX_LEN:], gsmall, {n: dlayer[0][n] for n in ("w_ada", "w_in")}, contrib0, contrib1


SMALL = ("c_ctx", "b_ada", "norm_w", "a_log", "a_dt_bias", "a_norm_w", "b_sink", "c_decay", "c_norm_w",
         "final_norm_w")
WEIGHTS = ("c_ctx", "w_ada", "b_ada", "norm_w", "w_in", "a_conv_w", "a_log", "a_dt_bias", "a_norm_w", "b_sink",
           "c_decay", "c_norm_w", "w_branch", "w_out", "final_norm_w")
SMALL_PACK = 12288


def _unshard_conv(g):
    return g.transpose(1, 2, 0, 3).reshape(DEPTH, A_CONV, 3 * A_WIDTH)


def _reshard_conv(w):
    return w.reshape(DEPTH, A_CONV, N_DEV, 3 * A_WIDTH // N_DEV).transpose(2, 0, 1, 3)


def _pack_small(tree):
    flat = jnp.concatenate([tree[n].reshape(-1) for n in SMALL])
    return jnp.pad(flat, (0, SMALL_PACK - flat.shape[0])).reshape(SMALL_PACK // 128, 128)


def _unpack_small(packed, like):
    flat = packed.reshape(-1)
    out, off = {}, 0
    for n in SMALL:
        size = math.prod(like[n].shape)
        out[n] = flat[off:off + size].reshape(like[n].shape)
        off += size
    return out


def kernel(x, c, ctx, c_ctx, w_ada, b_ada, norm_w, w_in, a_conv_w, a_log, a_dt_bias, a_norm_w, b_sink, c_decay, c_norm_w, w_branch, w_out, final_norm_w, loss_target, m_c_ctx, m_w_ada, m_b_ada, m_norm_w, m_w_in, m_a_conv_w, m_a_log, m_a_dt_bias, m_a_norm_w, m_b_sink, m_c_decay, m_c_norm_w, m_w_branch, m_w_out, m_final_norm_w, v_c_ctx, v_w_ada, v_b_ada, v_norm_w, v_w_in, v_a_conv_w, v_a_log, v_a_dt_bias, v_a_norm_w, v_b_sink, v_c_decay, v_c_norm_w, v_w_branch, v_w_out, v_final_norm_w):
    w = dict(c_ctx=c_ctx, w_ada=w_ada, b_ada=b_ada, norm_w=norm_w, w_in=w_in, a_conv_w=a_conv_w, a_log=a_log,
             a_dt_bias=a_dt_bias, a_norm_w=a_norm_w, b_sink=b_sink, c_decay=c_decay, c_norm_w=c_norm_w,
             w_branch=w_branch, w_out=w_out, final_norm_w=final_norm_w)
    m = dict(c_ctx=m_c_ctx, w_ada=m_w_ada, b_ada=m_b_ada, norm_w=m_norm_w, w_in=m_w_in, a_conv_w=m_a_conv_w,
             a_log=m_a_log, a_dt_bias=m_a_dt_bias, a_norm_w=m_a_norm_w, b_sink=m_b_sink, c_decay=m_c_decay,
             c_norm_w=m_c_norm_w, w_branch=m_w_branch, w_out=m_w_out, final_norm_w=m_final_norm_w)
    v = dict(c_ctx=v_c_ctx, w_ada=v_w_ada, b_ada=v_b_ada, norm_w=v_norm_w, w_in=v_w_in, a_conv_w=v_a_conv_w,
             a_log=v_a_log, a_dt_bias=v_a_dt_bias, a_norm_w=v_a_norm_w, b_sink=v_b_sink, c_decay=v_c_decay,
             c_norm_w=v_c_norm_w, w_branch=v_w_branch, w_out=v_w_out, final_norm_w=v_final_norm_w)

    shards = {n: w[n].astype(BF16) for n in LAYER_SHARDED}
    first = _exchange([shards["w_ada"][0], shards["w_in"][0], w["a_conv_w"]], True, "gather_layer0")
    layer0 = _layer_weights(dict(w_ada=first[0], w_in=first[1]))
    small_w = {n: w[n] for n in SMALL}
    small_w["a_conv_w"] = _unshard_conv(first[2])
    loss, gx, gw, glayer0, contrib0, contrib1 = _forward_backward(
        small_w, layer0, {n: shards[n][0] for n in ("w_branch", "w_out")}, {n: shards[n][1] for n in LAYER_SHARDED},
        x[0], c[0], ctx[0], loss_target[0])
    loss = lax.psum(loss, ("x", "y", "c"))

    last = _exchange([_reshard_layer("w_ada", glayer0["w_ada"]).astype(BF16),
                      _grad_blocks("w_in", glayer0["w_in"]), _reshard_conv(gw["a_conv_w"])],
                     False, "scatter_layer0")
    contrib0["w_ada"], contrib0["w_in"] = last[0], last[1]
    small = _exchange([_pack_small(gw)], True, "gather_small_grads")[0]

    grad, delta, new_m, new_v = {}, {}, {}, {}
    for n in LAYER_SHARDED:
        shp = w[n].shape
        per_layer = (math.prod(shp[1:-1]), shp[-1])
        outs = _adamw_layers(*[a.reshape((DEPTH,) + per_layer) for a in (w[n], m[n], v[n])],
                             *[cb.reshape((N_DEV,) + per_layer) for cb in (contrib0[n], contrib1[n])], "adamw_" + n)
        grad[n], delta[n], new_m[n], new_v[n] = [o.reshape(shp) for o in outs]
    shp = a_conv_w.shape
    two_d = (math.prod(shp[:-1]), shp[-1])
    outs = _adamw(*[a.reshape(two_d) for a in (a_conv_w, m_a_conv_w, v_a_conv_w)],
                  last[2].reshape((N_DEV,) + two_d), "adamw_a_conv_w")
    grad["a_conv_w"], delta["a_conv_w"], new_m["a_conv_w"], new_v["a_conv_w"] = [o.reshape(shp) for o in outs]
    outs = _adamw(_pack_small(w), _pack_small(m), _pack_small(v), small, "adamw_small")
    for tree, packed in zip((grad, delta, new_m, new_v), outs):
        tree.update(_unpack_small(packed, w))

    return (loss, gx[None], *[grad[n] for n in WEIGHTS], *[delta[n] for n in WEIGHTS],
            *[new_m[n] for n in WEIGHTS], *[new_v[n] for n in WEIGHTS])
```

```python
import functools
import math

import jax
import jax.numpy as jnp
from jax import lax
from jax.experimental import pallas as pl
from jax.experimental.pallas import tpu as pltpu

F32 = jnp.float32
BF16 = jnp.bfloat16
HIGHEST = lax.Precision.HIGHEST

D_MODEL = 1024
SEQ = 4096
DEPTH = 2
GRID_W = 64
CTX_LEN = 256
EPS = 1e-6
ROPE_BASE = 10000.0
BR_WIDTH = D_MODEL // 2
A_DK = 128
A_HEADS = 4
A_WIDTH = 512
A_CONV = 5
B_HD = 64
B_Q_HEADS = 8
B_KV_HEADS = 2
WINDOW = 128
B_BLOCK = 128
C_HD = 128
C_HEADS = 4
C_WIDTH = 512
CHUNK = 64
ADAM_LR = 0.001
ADAM_B1 = 0.9
ADAM_B2 = 0.999
ADAM_EPS = 1e-08
ADAM_WD = 0.01
ADAM_STEP = 10

N_DEV = 8
ROWS = CTX_LEN + SEQ
N_CHUNK = ROWS // CHUNK
N_CTX_CHUNK = CTX_LEN // CHUNK
IN_WIDTH = 8464
IN_PAD = 8704
NEG = -1e30

VMEM_LIMIT = 48 * 1024 * 1024
MESH = pl.DeviceIdType.MESH

C_AQ, C_AK, C_AV, C_AZ, C_BQ, C_BZ, C_CQ, C_CK, C_CV, C_CZ = (i * 512 for i in range(10))
C_MERGE = 5120
C_BKV = 8192
C_AB = 8448


def _cparams(sem=None):
    if sem is None:
        return pltpu.CompilerParams(vmem_limit_bytes=VMEM_LIMIT)
    return pltpu.CompilerParams(dimension_semantics=sem, vmem_limit_bytes=VMEM_LIMIT)


def _dg(a, b, ca, cb, prec=None):
    return lax.dot_general(a, b, (((ca,), (cb,)), ((), ())), preferred_element_type=F32, precision=prec)


@functools.partial(jax.custom_vjp, nondiff_argnums=(2, 3))
def _bdot(a, b, ca, cb):
    return _dg(a.astype(BF16), b.astype(BF16), ca, cb)


def _bdot_fwd(a, b, ca, cb):
    return _bdot(a, b, ca, cb), (a, b)


def _bdot_bwd(ca, cb, res, ct):
    a, b = res
    da = _bdot(ct, b, 1, 1 - cb) if ca == 1 else _bdot(b, ct, 1 - cb, 1)
    db = _bdot(a, ct, 1 - ca, 0) if cb == 0 else _bdot(ct, a, 0, 1 - ca)
    return da, db


_bdot.defvjp(_bdot_fwd, _bdot_bwd)


def _hdot(a, b):
    return _dg(a, b, 1, 0, lax.Precision.HIGH)


def _k_silu(x):
    return x / (1.0 + jnp.exp(-x))


def _k_sigmoid(x):
    return 1.0 / (1.0 + jnp.exp(-x))


@jax.custom_vjp
def _swap64(x):
    return pltpu.roll(x, 64, 1)


_swap64.defvjp(lambda x: (pltpu.roll(x, 64, 1), None), lambda _, ct: (pltpu.roll(ct, 64, 1),))


def _swap16_impl(x):
    lane = lax.broadcasted_iota(jnp.int32, x.shape, 1)
    return jnp.where((lane & 16) == 0, pltpu.roll(x, 112, 1), pltpu.roll(x, 16, 1))


@jax.custom_vjp
def _swap16(x):
    return _swap16_impl(x)


_swap16.defvjp(lambda x: (_swap16_impl(x), None), lambda _, ct: (_swap16_impl(ct),))


def _pick(dim, prefs):
    for p in prefs:
        if dim % p == 0:
            return p
    return dim


def _matmul(a, b, name, mode="nn", tiles=None, ride=None):
    ca, cb = {"nn": (1, 0), "nt": (1, 1), "tn": (0, 0)}[mode]
    m, k = a.shape[1 - ca], a.shape[ca]
    n = b.shape[1 - cb]
    if tiles is None:
        tiles = (_pick(m, (1088, 1024, 512, 256, 128)), _pick(n, (512, 256, 128)),
                 _pick(k, (1088, 1024, 512, 256, 128) if mode == "tn" else (2176, 2048, 1024, 512, 256, 128)))
    tm, tn, tk = tiles
    nk = k // tk
    a_spec = (pl.BlockSpec((tm, tk), lambda i, j, kk: (i, kk)) if ca == 1
              else pl.BlockSpec((tk, tm), lambda i, j, kk: (kk, i)))
    b_spec = (pl.BlockSpec((tk, tn), lambda i, j, kk: (kk, j)) if cb == 0
              else pl.BlockSpec((tn, tk), lambda i, j, kk: (j, kk)))

    def body(a_ref, b_ref, o_ref):
        part = _dg(a_ref[...].astype(BF16), b_ref[...].astype(BF16), ca, cb)
        if nk == 1:
            o_ref[...] = part
        else:
            kk = pl.program_id(2)

            @pl.when(kk == 0)
            def _():
                o_ref[...] = part

            @pl.when(kk > 0)
            def _():
                o_ref[...] += part

    grid = (m // tm, n // tn, nk)
    if ride is None:
        return pl.pallas_call(
            body,
            grid=grid,
            in_specs=[a_spec, b_spec],
            out_specs=pl.BlockSpec((tm, tn), lambda i, j, kk: (i, j)),
            out_shape=jax.ShapeDtypeStruct((m, n), F32),
            compiler_params=_cparams(("parallel", "parallel", "arbitrary")),
            name=name,
        )(a, b)
    body, r_in, r_out, r_shape, r_scratch = _riding(body, 2, 1, 0, ride, grid)
    return pl.pallas_call(
        body,
        grid=grid,
        in_specs=[a_spec, b_spec] + r_in,
        out_specs=[pl.BlockSpec((tm, tn), lambda i, j, kk: (i, j))] + r_out,
        out_shape=[jax.ShapeDtypeStruct((m, n), F32)] + r_shape,
        scratch_shapes=r_scratch,
        compiler_params=_cparams(("arbitrary", "arbitrary", "arbitrary")),
        name=name,
    )(a, b, *ride[0])


ROW_BLOCK = 256
ROW_VMEM_BUDGET = 16 * 1024 * 1024


def _pieces(val, pw):
    return [val[:, j * pw:(j + 1) * pw] for j in range(val.shape[1] // pw)]


def _flat(groups):
    arrays, sizes = [], []
    for g in groups:
        g = g if isinstance(g, (tuple, list)) else (g,)
        arrays += list(g)
        sizes.append(len(g))
    return arrays, sizes


def _regroup(refs, sizes):
    out, at = [], 0
    for n in sizes:
        val = refs[at][...]
        for r in refs[at + 1:at + n]:
            val = val + r[...]
        out.append(val)
        at += n
    return out


class _Rowwise:
    def __init__(self, fn, name, row_wpw, par_pw, out_wpw, n_diff=None):
        self.fn, self.name, self.row_wpw, self.par_pw, self.out_wpw = fn, name, row_wpw, par_pw, out_wpw
        self.n_diff = len(row_wpw) if n_diff is None else n_diff

        @jax.custom_vjp
        def call(rows, params):
            return self.fwd(rows, params)

        def call_fwd(rows, params):
            return self.fwd(rows, params), (rows, params)

        def call_bwd(res, douts):
            return self.bwd(res[0], res[1], douts)

        call.defvjp(call_fwd, call_bwd)
        self.call = call

    def _load(self, row_vals, par_refs, br, with_ctx):
        row = pl.program_id(0) * br + lax.broadcasted_iota(jnp.int32, (br, 1), 0)
        is_ctx = (row < (CTX_LEN if with_ctx else 0)).astype(F32)
        rows = [_pieces(v, pw) for v, (_, pw) in zip(row_vals, self.row_wpw)]
        pars = []
        for p, pw in zip(par_refs, self.par_pw):
            val = p[...].astype(F32)
            if p.shape[0] == 2:
                val = is_ctx * val[0:1, :] + (1.0 - is_ctx) * val[1:2, :]
            pars.append(_pieces(val, pw))
        return rows, pars, is_ctx

    def _block_rows(self, n_rows, widths):
        for br in (1088, 1024, 544, 512, 272):
            if n_rows % br == 0 and 2 * 4 * br * sum(widths) <= ROW_VMEM_BUDGET:
                return br
        return ROW_BLOCK

    def _row_specs(self, br, sizes, cols):
        out = []
        for (w, _), n, c in zip(self.row_wpw, sizes, cols):
            out += [pl.BlockSpec((br, w), lambda i, c=c: (i, c))] * n
        return out

    def fwd(self, rows, params, cols=None):
        arrays, sizes = _flat(rows)
        cols = cols or [0] * len(rows)
        n_rows = arrays[0].shape[0]
        n_in = len(arrays)
        br = self._block_rows(n_rows, [w for (w, _), n in zip(self.row_wpw, sizes) for _ in range(n)]
                              + [w for w, _ in self.out_wpw])

        def body(*refs):
            r, p, _ = self._load(_regroup(refs[:n_in], sizes), refs[n_in:n_in + len(params)], br, n_rows == ROWS)
            for o_ref, pieces, (_, pw) in zip(refs[n_in + len(params):], self.fn(r, p), self.out_wpw):
                for j, piece in enumerate(pieces):
                    o_ref[:, j * pw:(j + 1) * pw] = piece

        return pl.pallas_call(
            body,
            grid=(n_rows // br,),
            in_specs=self._row_specs(br, sizes, cols) + [pl.BlockSpec(p.shape, lambda i: (0, 0)) for p in params],
            out_specs=[pl.BlockSpec((br, w), lambda i: (i, 0)) for w, _ in self.out_wpw],
            out_shape=[jax.ShapeDtypeStruct((n_rows, w), F32) for w, _ in self.out_wpw],
            compiler_params=_cparams(("parallel",)),
            name=self.name + "_fwd",
        )(*arrays, *params)

    def bwd(self, rows, params, douts, cols=None, bf16_rows=()):
        arrays, sizes = _flat(rows)
        darrays, dsizes = _flat(douts)
        cols = cols or [0] * len(rows)
        n_rows = arrays[0].shape[0]
        n_in, n_par, n_dout, n_diff = len(arrays), len(params), len(darrays), self.n_diff
        br = self._block_rows(n_rows, [w for (w, _), n in zip(self.row_wpw, sizes) for _ in range(n)]
                              + [w for (w, _), n in zip(self.out_wpw, dsizes) for _ in range(n)]
                              + [w for w, _ in self.row_wpw[:n_diff]])

        def body(*refs):
            par_refs = refs[n_in:n_in + n_par]
            dout_refs = refs[n_in + n_par:n_in + n_par + n_dout]
            drow_refs = refs[n_in + n_par + n_dout:n_in + n_par + n_dout + n_diff]
            dpar_refs = refs[n_in + n_par + n_dout + n_diff:]

            @pl.when(pl.program_id(0) == 0)
            def _():
                for d in dpar_refs:
                    d[...] = jnp.zeros_like(d)

            r, p, is_ctx = self._load(_regroup(refs[:n_in], sizes), par_refs, br, n_rows == ROWS)
            cts = [_pieces(d, pw) for d, (_, pw) in zip(_regroup(dout_refs, dsizes), self.out_wpw)]
            fixed = r[n_diff:]
            _, vjp = jax.vjp(lambda rd, pp: self.fn(rd + fixed, pp), r[:n_diff], p)
            dr, dp = vjp(cts)
            for d_ref, pieces, (_, pw) in zip(drow_refs, dr, self.row_wpw):
                for j, piece in enumerate(pieces):
                    d_ref[:, j * pw:(j + 1) * pw] = piece.astype(d_ref.dtype)
            for d_ref, pieces, pw in zip(dpar_refs, dp, self.par_pw):
                for j, piece in enumerate(pieces):
                    lanes = slice(j * pw, (j + 1) * pw)
                    if d_ref.shape[0] != 2:
                        d_ref[:, lanes] += piece
                    else:
                        d_ref[0:1, lanes] += jnp.sum(is_ctx * piece, axis=0, keepdims=True)
                        d_ref[1:2, lanes] += jnp.sum((1.0 - is_ctx) * piece, axis=0, keepdims=True)

        par_specs = [pl.BlockSpec(p.shape, lambda i: (0, 0)) for p in params]
        dout_specs = []
        for (w, _), n in zip(self.out_wpw, dsizes):
            dout_specs += [pl.BlockSpec((br, w), lambda i: (i, 0))] * n
        drow_w = [w for w, _ in self.row_wpw[:n_diff]]
        g = pl.pallas_call(
            body,
            grid=(n_rows // br,),
            in_specs=self._row_specs(br, sizes, cols) + par_specs + dout_specs,
            out_specs=[pl.BlockSpec((br, w), lambda i: (i, 0)) for w in drow_w] + par_specs,
            out_shape=[jax.ShapeDtypeStruct((n_rows, w), BF16 if a in bf16_rows else F32) for a, w in enumerate(drow_w)]
            + [jax.ShapeDtypeStruct(p.shape, F32) for p in params],
            compiler_params=_cparams(("arbitrary",)),
            name=self.name + "_bwd",
        )(*arrays, *params, *darrays)
        return list(g[:n_diff]), list(g[n_diff:])


def _fn_norm_mod(rows, pars):
    (x,), (nw,), (shift,), (scale,) = rows[0], pars[0], pars[1], pars[2]
    y = x * lax.rsqrt(jnp.mean(x * x, axis=-1, keepdims=True) + EPS) * nw
    return [[y * (1.0 + scale) + shift]]


def _fn_head_rms_gate(rows, pars):
    (w,) = pars[0]
    return [[o * lax.rsqrt(jnp.mean(o * o, axis=-1, keepdims=True) + EPS) * w * _k_silu(z)
             for o, z in zip(rows[0], rows[1])]]


def _fn_group_norm_gate(rows, pars):
    out = []
    for o, z, w in zip(rows[0], rows[1], pars[0]):
        mu = jnp.mean(o, axis=-1, keepdims=True)
        var = jnp.mean(jnp.square(o - mu), axis=-1, keepdims=True)
        out.append((o - mu) * lax.rsqrt(var + EPS) * w * _k_silu(z))
    return [out]


def _fn_gate(rows, pars):
    return [[o * _k_silu(z) for o, z in zip(rows[0], rows[1])]]


def _fn_branch_merge(rows, pars):
    (ya,), (yb,), (yc,), (ma,), (mb,), (mc,) = rows
    (wa,), (wb,), (wc,) = pars
    return [[_k_sigmoid(ma) * _bdot(ya, wa, 1, 0) + _k_sigmoid(mb) * _bdot(yb, wb, 1, 0)
             + _k_sigmoid(mc) * _bdot(yc, wc, 1, 0)]]


def _fn_out_residual(rows, pars):
    (res,), (merged,), (w,), (gate,) = rows[0], rows[1], pars[0], pars[1]
    return [[res + gate * _bdot(merged, w, 1, 0)]]


def _fn_loss(rows, pars):
    (x,), (target,), (w,) = rows[0], rows[1], pars[0]
    y = x * lax.rsqrt(jnp.mean(x * x, axis=-1, keepdims=True) + EPS) * w
    per_row = 0.5 * jnp.mean(jnp.square(y - target), axis=-1, keepdims=True)
    return [[jnp.broadcast_to(per_row, (per_row.shape[0], 128))]]


def _fn_b_rope(rows, pars):
    q, (k, v), (cos,), (sin,) = rows
    rot = lambda x: x * cos + _swap16(x) * sin
    return [[rot(x) for x in q], [rot(k), v]]


def _fn_c_rope(rows, pars):
    q, k, (cos,), (sin,) = rows
    rot = lambda x: x * cos + _swap64(x) * sin
    return [[rot(x) for x in q], [rot(x) * (C_HD ** -0.5) for x in k]]


_norm_mod = _Rowwise(_fn_norm_mod, "norm_mod", [(D_MODEL, D_MODEL)], [D_MODEL] * 3, [(D_MODEL, D_MODEL)])
_out_residual = _Rowwise(_fn_out_residual, "out_residual", [(D_MODEL, D_MODEL)] * 2, [D_MODEL] * 2,
                         [(D_MODEL, D_MODEL)])
_loss_rows = _Rowwise(_fn_loss, "loss", [(D_MODEL, D_MODEL)] * 2, [D_MODEL], [(128, 128)], n_diff=1)
_a_out = _Rowwise(_fn_head_rms_gate, "a_out", [(512, 128)] * 2, [128], [(512, 128)])
_c_out = _Rowwise(_fn_group_norm_gate, "c_out", [(512, 128)] * 2, [128], [(512, 128)])
_b_out = _Rowwise(_fn_gate, "b_out", [(512, 512)] * 2, [], [(512, 512)])
_branch_merge = _Rowwise(_fn_branch_merge, "branch_merge", [(512, 512)] * 3 + [(D_MODEL, D_MODEL)] * 3,
                         [D_MODEL] * 3, [(D_MODEL, D_MODEL)])
_b_rope = _Rowwise(_fn_b_rope, "b_rope", [(512, 128), (256, 128), (128, 128), (128, 128)], [],
                   [(512, 128), (256, 128)], n_diff=2)
_c_rope = _Rowwise(_fn_c_rope, "c_rope", [(512, 128), (512, 128), (128, 128), (128, 128)], [],
                   [(512, 128), (512, 128)], n_diff=2)


HALO = 8
EXT = ROW_BLOCK + 2 * HALO


def _halo_specs(col, width=512):
    last = ROWS // HALO - 1
    per = ROW_BLOCK // HALO
    prev = pl.BlockSpec((HALO, width), lambda i: (jnp.maximum(i * per - 1, 0), col))
    cur = pl.BlockSpec((ROW_BLOCK, width), lambda i: (i, col))
    nxt = pl.BlockSpec((HALO, width), lambda i: (jnp.minimum((i + 1) * per, last), col))
    return [prev, cur, nxt]


def _extended(prev_ref, cur_ref, next_ref):
    i = pl.program_id(0)
    prev_ok = i >= 2
    next_ok = jnp.logical_and(i >= 1, i < ROWS // ROW_BLOCK - 1)
    return jnp.concatenate([jnp.where(prev_ok, prev_ref[...], 0.0), cur_ref[...],
                            jnp.where(next_ok, next_ref[...], 0.0)], axis=0)


def _conv_taps(x_ext, w_ref, flip):
    acc = None
    for j in range(A_CONV):
        shift = (j - 2) if flip else (2 - j)
        term = w_ref[j:j + 1, :] * pltpu.roll(x_ext, shift % EXT, 0)
        acc = term if acc is None else acc + term
    return acc


def _conv_post(pre_pieces, normalize, scale):
    out = []
    for p in pre_pieces:
        y = _k_silu(p)
        if normalize:
            y = y * lax.rsqrt(jnp.sum(y * y, axis=-1, keepdims=True) + EPS) * scale
        out.append(y)
    return out


def _a_prep_fwd(proj, conv8, col, normalize, scale, name):
    def body(prev_ref, cur_ref, next_ref, w_ref, o_ref):
        pre = _conv_taps(_extended(prev_ref, cur_ref, next_ref), w_ref, False)[HALO:HALO + ROW_BLOCK]
        for h, y in enumerate(_conv_post(_pieces(pre, 128), normalize, scale)):
            o_ref[:, h * 128:(h + 1) * 128] = y

    return pl.pallas_call(
        body,
        grid=(ROWS // ROW_BLOCK,),
        in_specs=_halo_specs(col) + [pl.BlockSpec((8, 512), lambda i: (0, col))],
        out_specs=pl.BlockSpec((ROW_BLOCK, 512), lambda i: (i, 0)),
        out_shape=jax.ShapeDtypeStruct((ROWS, 512), F32),
        compiler_params=_cparams(("parallel",)),
        name=name + "_fwd",
    )(proj, proj, proj, conv8)


def _a_prep_bwd(proj, conv8, col, normalize, scale, dout_f, dout_r, name):
    def body(xp, xc, xn, w_ref, fp, fc, fn_, rp, rc, rn, dx_ref, dw_ref):
        @pl.when(pl.program_id(0) == 0)
        def _():
            dw_ref[...] = jnp.zeros_like(dw_ref)

        x_ext = _extended(xp, xc, xn)
        dout = _extended(fp, fc, fn_) + _extended(rp, rc, rn)
        pre = _conv_taps(x_ext, w_ref, False)
        _, vjp = jax.vjp(lambda p: _conv_post(p, normalize, scale), _pieces(pre, 128))
        (dpre,) = vjp(_pieces(dout, 128))
        dpre = jnp.concatenate(dpre, axis=1)
        dx_ref[...] = _conv_taps(dpre, w_ref, True)[HALO:HALO + ROW_BLOCK].astype(BF16)
        own = dpre[HALO:HALO + ROW_BLOCK]
        for j in range(A_CONV):
            shifted = pltpu.roll(x_ext, (2 - j) % EXT, 0)[HALO:HALO + ROW_BLOCK]
            dw_ref[j:j + 1, :] += jnp.sum(own * shifted, axis=0, keepdims=True)

    return pl.pallas_call(
        body,
        grid=(ROWS // ROW_BLOCK,),
        in_specs=_halo_specs(col) + [pl.BlockSpec((8, 512), lambda i: (0, col))] + _halo_specs(0) + _halo_specs(0),
        out_specs=[pl.BlockSpec((ROW_BLOCK, 512), lambda i: (i, 0)), pl.BlockSpec((8, 512), lambda i: (0, 0))],
        out_shape=[jax.ShapeDtypeStruct((ROWS, 512), BF16), jax.ShapeDtypeStruct((8, 512), F32)],
        compiler_params=_cparams(("arbitrary",)),
        name=name + "_bwd",
    )(proj, proj, proj, conv8, dout_f, dout_f, dout_f, dout_r, dout_r, dout_r)


N_CHAIN = 8


def _rev_chunk(s):
    return jnp.where(s < N_CTX_CHUNK, N_CTX_CHUNK - 1 - s, N_CHUNK + N_CTX_CHUNK - 1 - s)


def _scan_specs(step_of, v_col=0):
    cf = step_of
    cr = lambda n: _rev_chunk(step_of(n))

    def pair(shape, index):
        return (pl.BlockSpec(shape, lambda n: index(cf(n))), pl.BlockSpec(shape, lambda n: index(cr(n))))

    return dict(
        tok=pair((CHUNK, 512), lambda c: (c, 0)),
        tokv=pair((CHUNK, 512), lambda c: (c, v_col)),
        col=pair((4, CHUNK, 1), lambda c: (0, c, 0)),
        row=pair((4, 1, 1, CHUNK), lambda c: (0, c, 0, 0)),
        one=pair((4, 1, 1, 1), lambda c: (0, c, 0, 0)),
        state=pair((None, 4, 128, 128), lambda c: (c, 0, 0, 0)),
        tinv=pair((None, 4, CHUNK, CHUNK), lambda c: (c, 0, 0, 0)),
    )


def _both(specs, kinds):
    out = []
    for kind in kinds:
        out += list(specs[kind])
    return out


def _scan_call(body, name, in_specs, out_specs, out_shape, operands, ride):
    body, r_in, r_out, r_shape, r_scratch = _riding(body, len(in_specs), len(out_specs), 1, ride, (N_CHUNK,))
    return pl.pallas_call(
        body,
        grid=(N_CHUNK,),
        in_specs=in_specs + r_in,
        out_specs=out_specs + r_out,
        out_shape=out_shape + r_shape,
        scratch_shapes=[pltpu.VMEM((N_CHAIN, 128, 128), F32)] + r_scratch,
        compiler_params=_cparams(("arbitrary",)),
        name=name,
    )(*operands, *(ride[0] if ride else []))


def _chain_masks():
    ii = lax.broadcasted_iota(jnp.int32, (CHUNK, CHUNK), 0)
    jj = lax.broadcasted_iota(jnp.int32, (CHUNK, CHUNK), 1)
    eye = jnp.where(ii == jj, 1.0, 0.0).astype(F32)
    lower = (ii >= jj, ii > jj)
    upper = (ii <= jj, ii < jj)
    return [lower] * 4 + [upper] * 4, eye


def _tri_inv_all(ls, eye):
    xs = [eye - l for l in ls]
    ps = [_hdot(l, l) for l in ls]
    for i in range(5):
        xs = [x + _hdot(x, p) for x, p in zip(xs, ps)]
        if i < 4:
            ps = [_hdot(p, p) for p in ps]
    return xs


@jax.custom_vjp
def _inv_saved(l, x):
    return x


def _inv_saved_fwd(l, x):
    return x, x


def _inv_saved_bwd(x, dx):
    return -_bdot(x, _bdot(dx, x, 1, 1), 0, 0), jnp.zeros_like(x)


_inv_saved.defvjp(_inv_saved_fwd, _inv_saved_bwd)


def _delta_chains(q, k, v, beta_r, gcr, gl, s, masks, eye, tinv_saved):
    n = range(len(q))
    beta = [jnp.sum(eye * beta_r[i], axis=1, keepdims=True) for i in n]
    gcc = [jnp.sum(eye * gcr[i], axis=1, keepdims=True) for i in n]
    decay = [jnp.exp(jnp.where(masks[i][0], gcc[i] - gcr[i], NEG)) for i in n]
    kb = [k[i] * beta[i] for i in n]
    lmat = [jnp.where(masks[i][1], _bdot(kb[i], k[i], 1, 1) * decay[i], 0.0) for i in n]
    if tinv_saved is None:
        tinv = _tri_inv_all(lmat, eye)
    else:
        tinv = [_inv_saved(lmat[i], tinv_saved[i]) for i in n]
    eg = [jnp.exp(gcc[i]) for i in n]
    u = [_bdot(tinv[i], v[i] * beta[i], 1, 0) for i in n]
    w = [_bdot(tinv[i], kb[i] * eg[i], 1, 0) for i in n]
    qk = [_bdot(q[i], k[i], 1, 1) * decay[i] for i in n]
    v_new = [u[i] - _bdot(w[i], s[i], 1, 0) for i in n]
    o = [_bdot(q[i] * eg[i], s[i], 1, 0) + _bdot(qk[i], v_new[i], 1, 0) for i in n]
    s_new = [s[i] * jnp.exp(gl[i]) + _bdot(k[i] * jnp.exp(gl[i] - gcc[i]), v_new[i], 0, 0) for i in n]
    return (o, s_new), tinv


def _chain_loads(tok_pairs, small_pairs):
    toks = [[pair[i // 4][:, (i % 4) * 128:(i % 4 + 1) * 128] for i in range(N_CHAIN)] for pair in tok_pairs]
    smalls = [[pair[i // 4][i % 4] for i in range(N_CHAIN)] for pair in small_pairs]
    return toks, smalls


def _delta_fwd_call(q, k, v, beta, gc, gl, ride=None):
    sp = _scan_specs(lambda n: n)

    def body(qf, qr, kf, kr, vf, vr, bf, br, gcrf, gcrr, glf, glr, of, orv, ssf, ssr, tsf, tsr, s_scr):
        @pl.when(pl.program_id(0) == 0)
        def _():
            s_scr[...] = jnp.zeros_like(s_scr)

        masks, eye = _chain_masks()
        (qs, ks, vs), _ = _chain_loads([(qf, qr), (kf, kr), (vf, vr)], [])
        bs = [(bf, br)[i // 4][i % 4, 0] for i in range(N_CHAIN)]
        gcrs = [(gcrf, gcrr)[i // 4][i % 4, 0] for i in range(N_CHAIN)]
        gls = [(glf, glr)[i // 4][i % 4, 0] for i in range(N_CHAIN)]
        ss = [s_scr[i] for i in range(N_CHAIN)]
        (o, s_new), tinv = _delta_chains(qs, ks, vs, bs, gcrs, gls, ss, masks, eye, None)
        for i in range(N_CHAIN):
            d, h = i // 4, i % 4
            (ssf, ssr)[d][h] = ss[i]
            (tsf, tsr)[d][h] = tinv[i]
            (of, orv)[d][:, h * 128:(h + 1) * 128] = o[i]
            s_scr[i] = s_new[i]

    return _scan_call(
        body, "delta_fwd",
        _both(sp, ["tok", "tok", "tok", "row", "row", "one"]),
        _both(sp, ["tok", "state", "tinv"]),
        [jax.ShapeDtypeStruct((ROWS, 512), F32)] * 2 + [jax.ShapeDtypeStruct((N_CHUNK, 4, 128, 128), F32)] * 2
        + [jax.ShapeDtypeStruct((N_CHUNK, 4, CHUNK, CHUNK), F32)] * 2,
        [q, q, k, k, v, v, *beta, *gc, *gl], ride)


def _delta_bwd_call(q, k, v, beta, gc, gl, ssave, tsave, do, ride=None):
    sp = _scan_specs(lambda n: N_CHUNK - 1 - n)

    def body(qf, qr, kf, kr, vf, vr, bf, br, gcrf, gcrr, glf, glr, ssf, ssr, tsf, tsr, dof, dor,
             dqf, dqr, dkf, dkr, dvf, dvr, dbf, dbr, dgcrf, dgcrr, dglf, dglr, ds_scr):
        @pl.when(pl.program_id(0) == 0)
        def _():
            ds_scr[...] = jnp.zeros_like(ds_scr)

        masks, eye = _chain_masks()
        (qs, ks, vs, dos), (ss, ts) = _chain_loads(
            [(qf, qr), (kf, kr), (vf, vr), (dof, dor)], [(ssf, ssr), (tsf, tsr)])
        bs = [(bf, br)[i // 4][i % 4, 0] for i in range(N_CHAIN)]
        gcrs = [(gcrf, gcrr)[i // 4][i % 4, 0] for i in range(N_CHAIN)]
        gls = [(glf, glr)[i // 4][i % 4, 0] for i in range(N_CHAIN)]
        fn = lambda *a: _delta_chains(*a, masks, eye, ts)
        _, vjp, _ = jax.vjp(fn, qs, ks, vs, bs, gcrs, gls, ss, has_aux=True)
        dq, dk, dv, db, dgcr, dgl, ds = vjp((dos, [ds_scr[i] for i in range(N_CHAIN)]))
        for i in range(N_CHAIN):
            d, h = i // 4, i % 4
            hs = slice(h * 128, (h + 1) * 128)
            (dqf, dqr)[d][:, hs] = dq[i]
            (dkf, dkr)[d][:, hs] = dk[i]
            (dvf, dvr)[d][:, hs] = dv[i]
            (dbf, dbr)[d][h, 0] = db[i]
            (dgcrf, dgcrr)[d][h, 0] = dgcr[i]
            (dglf, dglr)[d][h, 0] = dgl[i]
            ds_scr[i] = ds[i]

    tok = jax.ShapeDtypeStruct((ROWS, 512), F32)
    return _scan_call(
        body, "delta_bwd",
        _both(sp, ["tok", "tok", "tok", "row", "row", "one", "state", "tinv", "tok"]),
        _both(sp, ["tok", "tok", "tok", "row", "row", "one"]),
        [tok] * 6 + [jax.ShapeDtypeStruct((4, N_CHUNK, 1, CHUNK), F32)] * 4
        + [jax.ShapeDtypeStruct((4, N_CHUNK, 1, 1), F32)] * 2,
        [q, q, k, k, v, v, *beta, *gc, *gl, *ssave, *tsave, do, do], ride)


def _ret_chains(q, k, v, dm, qs, ks, cd, s):
    n = range(len(q))
    a = [_bdot(q[i], k[i], 1, 1) * dm[i] for i in n]
    o = [_bdot(a[i], v[i], 1, 0) + _bdot(q[i] * qs[i], s[i], 1, 0) for i in n]
    s_new = [s[i] * cd[i] + _bdot(k[i] * ks[i], v[i], 0, 0) for i in n]
    return o, s_new


def _ret_const_specs():
    return [pl.BlockSpec((N_CHAIN, CHUNK, CHUNK), lambda n: (0, 0, 0)), pl.BlockSpec((N_CHAIN, CHUNK, 1), lambda n: (0, 0, 0)),
            pl.BlockSpec((N_CHAIN, CHUNK, 1), lambda n: (0, 0, 0)), pl.BlockSpec((N_CHAIN, 1, 1), lambda n: (0, 0, 0))]


def _ret_fwd_call(q, k, v, v_col, dm, qs, ks, cd, ride=None):
    sp = _scan_specs(lambda n: n, v_col)

    def body(qf, qr, kf, kr, vf, vr, dm_ref, qs_ref, ks_ref, cd_ref, of, orv, ssf, ssr, s_scr):
        @pl.when(pl.program_id(0) == 0)
        def _():
            s_scr[...] = jnp.zeros_like(s_scr)

        (qc, kc, vc), _ = _chain_loads([(qf, qr), (kf, kr), (vf, vr)], [])
        ss = [s_scr[i] for i in range(N_CHAIN)]
        consts = [[r[i] for i in range(N_CHAIN)] for r in (dm_ref, qs_ref, ks_ref, cd_ref)]
        o, s_new = _ret_chains(qc, kc, vc, *consts, ss)
        for i in range(N_CHAIN):
            d, h = i // 4, i % 4
            (ssf, ssr)[d][h] = ss[i]
            (of, orv)[d][:, h * 128:(h + 1) * 128] = o[i]
            s_scr[i] = s_new[i]

    return _scan_call(
        body, "ret_fwd",
        _both(sp, ["tok", "tok", "tokv"]) + _ret_const_specs(),
        _both(sp, ["tok", "state"]),
        [jax.ShapeDtypeStruct((ROWS, 512), F32)] * 2 + [jax.ShapeDtypeStruct((N_CHUNK, 4, 128, 128), F32)] * 2,
        [q, q, k, k, v, v, dm, qs, ks, cd], ride)


def _ret_bwd_call(q, k, v, v_col, dm, qs, ks, cd, ssave, do, ride=None):
    sp = _scan_specs(lambda n: N_CHUNK - 1 - n, v_col)

    def body(qf, qr, kf, kr, vf, vr, dm_ref, qs_ref, ks_ref, cd_ref, ssf, ssr, dof, dor,
             dqf, dqr, dkf, dkr, dvf, dvr, ddm_ref, dqs_ref, dks_ref, dcd_ref, ds_scr):
        @pl.when(pl.program_id(0) == 0)
        def _():
            ds_scr[...] = jnp.zeros_like(ds_scr)
            ddm_ref[...] = jnp.zeros_like(ddm_ref)
            dqs_ref[...] = jnp.zeros_like(dqs_ref)
            dks_ref[...] = jnp.zeros_like(dks_ref)
            dcd_ref[...] = jnp.zeros_like(dcd_ref)

        (qc, kc, vc, dos), (ss,) = _chain_loads([(qf, qr), (kf, kr), (vf, vr), (dof, dor)], [(ssf, ssr)])
        consts = [[r[i] for i in range(N_CHAIN)] for r in (dm_ref, qs_ref, ks_ref, cd_ref)]
        _, vjp = jax.vjp(_ret_chains, qc, kc, vc, *consts, ss)
        dq, dk, dv, ddm, dqs, dks, dcd, ds = vjp((dos, [ds_scr[i] for i in range(N_CHAIN)]))
        for i in range(N_CHAIN):
            d, h = i // 4, i % 4
            hs = slice(h * 128, (h + 1) * 128)
            (dqf, dqr)[d][:, hs] = dq[i]
            (dkf, dkr)[d][:, hs] = dk[i]
            (dvf, dvr)[d][:, hs] = dv[i]
            ddm_ref[i] += ddm[i]
            dqs_ref[i] += dqs[i]
            dks_ref[i] += dks[i]
            dcd_ref[i] += dcd[i]
            ds_scr[i] = ds[i]

    tok = jax.ShapeDtypeStruct((ROWS, 512), F32)
    return _scan_call(
        body, "ret_bwd",
        _both(sp, ["tok", "tok", "tokv"]) + _ret_const_specs() + _both(sp, ["state", "tok"]),
        _both(sp, ["tok", "tok", "tok"]) + _ret_const_specs(),
        [tok] * 6 + [jax.ShapeDtypeStruct((N_CHAIN, CHUNK, CHUNK), F32), jax.ShapeDtypeStruct((N_CHAIN, CHUNK, 1), F32),
                     jax.ShapeDtypeStruct((N_CHAIN, CHUNK, 1), F32), jax.ShapeDtypeStruct((N_CHAIN, 1, 1), F32)],
        [q, q, k, k, v, v, dm, qs, ks, cd, *ssave, do, do], ride)


N_QBLK = ROWS // B_BLOCK
CTX_QBLK = CTX_LEN // B_BLOCK


def _attn_heads(q, kc, vc, kw, vw, sink, valid):
    n = range(len(q))
    qs = [q[i] * (B_HD ** -0.5) for i in n]
    s_c = [_bdot(qs[i], kc[i], 1, 1) for i in n]
    s_w = [jnp.where(valid, _bdot(qs[i], kw[i], 1, 1), NEG) for i in n]
    m = [lax.stop_gradient(jnp.maximum(jnp.maximum(jnp.max(s_c[i], axis=-1, keepdims=True), sink[i]),
                                       jnp.max(s_w[i], axis=-1, keepdims=True))) for i in n]
    e_c = [jnp.exp(s_c[i] - m[i]) for i in n]
    e_w = [jnp.exp(s_w[i] - m[i]) for i in n]
    den = [jnp.sum(e_c[i], axis=-1, keepdims=True) + jnp.sum(e_w[i], axis=-1, keepdims=True)
           + jnp.exp(sink[i] - m[i]) for i in n]
    return [(_bdot(e_c[i], vc[i], 1, 0) + _bdot(e_w[i], vw[i], 1, 0)) / den[i] for i in n]


def _attn_loads(q_ref, kv_ref, sink_ref, start):
    q, kc, vc, kw, vw, sink = [], [], [], [], [], []
    for hk in range(B_KV_HEADS):
        ks = slice(hk * B_HD, (hk + 1) * B_HD)
        vs = slice(128 + hk * B_HD, 128 + (hk + 1) * B_HD)
        grp = (kv_ref[0:CTX_LEN, ks], kv_ref[0:CTX_LEN, vs],
               kv_ref[pl.ds(start, 3 * B_BLOCK), ks], kv_ref[pl.ds(start, 3 * B_BLOCK), vs])
        for g in range(4):
            h = hk * 4 + g
            q.append(q_ref[:, h * B_HD:(h + 1) * B_HD])
            for lst, val in zip((kc, vc, kw, vw), grp):
                lst.append(val)
            sink.append(jnp.full((1, 1), sink_ref[h], F32))
    return q, kc, vc, kw, vw, sink


def _window(blk):
    xblk = blk - CTX_QBLK
    first = jnp.clip((xblk - 1) * B_BLOCK, 0, SEQ - 3 * B_BLOCK)
    qpos = xblk * B_BLOCK + lax.broadcasted_iota(jnp.int32, (B_BLOCK, 3 * B_BLOCK), 0)
    kpos = first + lax.broadcasted_iota(jnp.int32, (B_BLOCK, 3 * B_BLOCK), 1)
    far = jnp.where(blk >= CTX_QBLK, 0, 2 * SEQ)
    valid = jnp.abs(kpos - qpos) + far <= WINDOW
    return pl.multiple_of(first + CTX_LEN, B_BLOCK), valid


def _attn_specs():
    qspec = pl.BlockSpec((B_BLOCK, 512), lambda i: (i, 0))
    kvspec = pl.BlockSpec((ROWS, 256), lambda i: (0, 0))
    return qspec, kvspec, pl.BlockSpec(memory_space=pltpu.SMEM)


def _attn_fwd_call(q, kv, sink, ride=None):
    def body(q_ref, kv_ref, sink_ref, o_ref):
        start, valid = _window(pl.program_id(0))
        out = _attn_heads(*_attn_loads(q_ref, kv_ref, sink_ref, start), valid)
        for h in range(B_Q_HEADS):
            o_ref[:, h * B_HD:(h + 1) * B_HD] = out[h]

    qspec, kvspec, sspec = _attn_specs()
    body, r_in, r_out, r_shape, r_scratch = _riding(body, 3, 1, 0, ride, (N_QBLK,))
    return pl.pallas_call(
        body,
        grid=(N_QBLK,),
        in_specs=[qspec, kvspec, sspec] + r_in,
        out_specs=[qspec] + r_out,
        out_shape=[jax.ShapeDtypeStruct((ROWS, 512), F32)] + r_shape,
        scratch_shapes=r_scratch,
        compiler_params=_cparams(("arbitrary",)),
        name="attn_fwd",
    )(q, kv, sink, *(ride[0] if ride else []))


def _attn_bwd_call(q, kv, sink, do, ride=None):
    def body(q_ref, kv_ref, sink_ref, do_ref, dq_ref, dkv_ref, dsink_ref):
        @pl.when(pl.program_id(0) == 0)
        def _():
            dkv_ref[...] = jnp.zeros_like(dkv_ref)
            dsink_ref[...] = jnp.zeros_like(dsink_ref)

        start, valid = _window(pl.program_id(0))
        _, vjp = jax.vjp(functools.partial(_attn_heads, valid=valid), *_attn_loads(q_ref, kv_ref, sink_ref, start))
        dq, dkc, dvc, dkw, dvw, dsink = vjp([do_ref[:, h * B_HD:(h + 1) * B_HD] for h in range(B_Q_HEADS)])
        for h in range(B_Q_HEADS):
            dq_ref[:, h * B_HD:(h + 1) * B_HD] = dq[h]
            dsink_ref[h:h + 1, :] += jnp.broadcast_to(dsink[h], (1, 128))
        for hk in range(B_KV_HEADS):
            ks = slice(hk * B_HD, (hk + 1) * B_HD)
            vs = slice(128 + hk * B_HD, 128 + (hk + 1) * B_HD)
            grp = lambda parts: parts[hk * 4] + parts[hk * 4 + 1] + parts[hk * 4 + 2] + parts[hk * 4 + 3]
            dkv_ref[0:CTX_LEN, ks] += grp(dkc)
            dkv_ref[0:CTX_LEN, vs] += grp(dvc)
            dkv_ref[pl.ds(start, 3 * B_BLOCK), ks] += grp(dkw)
            dkv_ref[pl.ds(start, 3 * B_BLOCK), vs] += grp(dvw)

    qspec, kvspec, sspec = _attn_specs()
    body, r_in, r_out, r_shape, r_scratch = _riding(body, 4, 3, 0, ride, (N_QBLK,))
    return pl.pallas_call(
        body,
        grid=(N_QBLK,),
        in_specs=[qspec, kvspec, sspec, qspec] + r_in,
        out_specs=[qspec, kvspec, pl.BlockSpec((8, 128), lambda i: (0, 0))] + r_out,
        out_shape=[jax.ShapeDtypeStruct((ROWS, 512), F32), jax.ShapeDtypeStruct((ROWS, 256), F32),
                   jax.ShapeDtypeStruct((8, 128), F32)] + r_shape,
        scratch_shapes=r_scratch,
        compiler_params=_cparams(("arbitrary",)),
        name="attn_bwd",
    )(q, kv, sink, do, *(ride[0] if ride else []))


def _my_id():
    return 4 * lax.axis_index("x") + 2 * lax.axis_index("y") + lax.axis_index("c")


def _peer(k):
    x, y, c = lax.axis_index("x"), lax.axis_index("y"), lax.axis_index("c")
    return (1 - x if k & 4 else x, 1 - y if k & 2 else y, 1 - c if k & 1 else c)


SAME_CORE_PEERS = (2, 4, 6)


def _scatter_copies(ins, outs, sems):
    send_sems, recv_sems, local_sems = sems
    me = _my_id()
    own, remote = [], []
    for a in range(len(ins)):
        own.append(pltpu.make_async_copy(ins[a].at[me], outs[a].at[me], local_sems.at[a]))
        for k in range(1, N_DEV):
            peer_slot = jnp.bitwise_xor(me, k)
            common = dict(src_ref=ins[a].at[peer_slot], send_sem=send_sems.at[a, k - 1],
                          recv_sem=recv_sems.at[a, k - 1], device_id=_peer(k), device_id_type=MESH)
            remote.append((pltpu.make_async_remote_copy(dst_ref=outs[a].at[me], **common),
                           pltpu.make_async_remote_copy(dst_ref=outs[a].at[peer_slot], **common)))
    return own, remote


def _gather_copy(outs, sems, a, k, src, slot, to):
    return pltpu.make_async_remote_copy(src_ref=src, dst_ref=outs[a].at[slot], send_sem=sems[0].at[a, k - 1],
                                        recv_sem=sems[1].at[a, k - 1], device_id=_peer(to), device_id_type=MESH)


def _gather_first_copies(ins, outs, sems):
    me = _my_id()
    own = [pltpu.make_async_copy(ins[a], outs[a].at[me], sems[2].at[a]) for a in range(len(ins))]
    direct = [_gather_copy(outs, sems, a, k, ins[a], me, k) for a in range(len(ins)) for k in (1,) + SAME_CORE_PEERS]
    return own, direct


def _exchange_start(ins, outs, sems, gather):
    own, remote = _gather_first_copies(ins, outs, sems) if gather else _scatter_copies(ins, outs, sems)
    for cp in own:
        cp.start()
    for cp in remote:
        (cp if gather else cp[0]).start()


def _exchange_wait(ins, outs, sems, gather):
    if not gather:
        own, remote = _scatter_copies(ins, outs, sems)
        for _, arrival in remote:
            arrival.wait_recv()
        for send, _ in remote:
            send.wait_send()
        for cp in own:
            cp.wait()
        return
    me = _my_id()
    own, direct = _gather_first_copies(ins, outs, sems)
    passed = []
    for a in range(len(ins)):
        for k in SAME_CORE_PEERS:
            origin = jnp.bitwise_xor(me, k)
            _gather_copy(outs, sems, a, k, ins[a], origin, k).wait_recv()
            onward = _gather_copy(outs, sems, a, k + 1, outs[a].at[origin], origin, 1)
            onward.start()
            passed.append(onward)
    for a in range(len(ins)):
        for k in (1, 3, 5, 7):
            _gather_copy(outs, sems, a, k, ins[a], jnp.bitwise_xor(me, k), 1).wait_recv()
    for cp in direct + passed:
        cp.wait_send()
    for cp in own:
        cp.wait()


def _exchange_plumbing(arrays, gather):
    n = len(arrays)
    hbm = [pl.BlockSpec(memory_space=pltpu.HBM)] * n
    out_shape = [jax.ShapeDtypeStruct((N_DEV,) + (a.shape if gather else a.shape[1:]), a.dtype) for a in arrays]
    sems = [pltpu.SemaphoreType.DMA((n, N_DEV - 1)), pltpu.SemaphoreType.DMA((n, N_DEV - 1)),
            pltpu.SemaphoreType.DMA((n,))]
    return hbm, out_shape, sems


def _exchange(arrays, gather, name):
    n = len(arrays)

    def body(*refs):
        ins, outs, sems = refs[:n], refs[n:2 * n], refs[2 * n:]
        _exchange_start(ins, outs, sems, gather)
        _exchange_wait(ins, outs, sems, gather)

    hbm, out_shape, sems = _exchange_plumbing(arrays, gather)
    return pl.pallas_call(
        body,
        in_specs=hbm,
        out_specs=hbm,
        out_shape=out_shape,
        scratch_shapes=sems,
        compiler_params=pltpu.CompilerParams(has_side_effects=True),
        name=name,
    )(*arrays)


N_CHIP = N_DEV // 2


def _pair_swap(blocks, name):
    n = len(blocks)

    def body(*refs):
        ins, outs, (send_sems, recv_sems) = refs[:n], refs[n:2 * n], refs[2 * n:]
        core = lax.axis_index("c")
        copies = [pltpu.make_async_remote_copy(src_ref=ins[a].at[2 * chip + (1 - core)], dst_ref=outs[a].at[chip],
                                               send_sem=send_sems.at[a, chip], recv_sem=recv_sems.at[a, chip],
                                               device_id=_peer(1), device_id_type=MESH)
                  for a in range(n) for chip in range(N_CHIP)]
        for cp in copies:
            cp.start()
        for cp in copies:
            cp.wait_recv()
        for cp in copies:
            cp.wait_send()

    hbm = [pl.BlockSpec(memory_space=pltpu.HBM)] * n
    return pl.pallas_call(
        body,
        in_specs=hbm,
        out_specs=hbm,
        out_shape=[jax.ShapeDtypeStruct((N_CHIP,) + b.shape[1:], b.dtype) for b in blocks],
        scratch_shapes=[pltpu.SemaphoreType.DMA((n, N_CHIP)), pltpu.SemaphoreType.DMA((n, N_CHIP))],
        compiler_params=pltpu.CompilerParams(has_side_effects=True),
        name=name,
    )(*blocks)


def _chip_scatter(pairs, name):
    n = len(pairs)

    def body(*refs):
        ins, outs, (send_sems, recv_sems, local_sems) = refs[:n], refs[n:2 * n], refs[2 * n:]
        chip = 2 * lax.axis_index("x") + lax.axis_index("y")
        own = [pltpu.make_async_copy(ins[a].at[chip], outs[a].at[chip], local_sems.at[a]) for a in range(n)]
        sends, arrivals = [], []
        for a in range(n):
            for k in range(1, N_CHIP):
                other = jnp.bitwise_xor(chip, k)
                common = dict(src_ref=ins[a].at[other], send_sem=send_sems.at[a, k - 1], recv_sem=recv_sems.at[a, k - 1],
                              device_id=_peer(2 * k), device_id_type=MESH)
                sends.append(pltpu.make_async_remote_copy(dst_ref=outs[a].at[chip], **common))
                arrivals.append(pltpu.make_async_remote_copy(dst_ref=outs[a].at[other], **common))
        for cp in own + sends:
            cp.start()
        for cp in arrivals:
            cp.wait_recv()
        for cp in sends:
            cp.wait_send()
        for cp in own:
            cp.wait()

    hbm = [pl.BlockSpec(memory_space=pltpu.HBM)] * n
    return pl.pallas_call(
        body,
        in_specs=hbm,
        out_specs=hbm,
        out_shape=[jax.ShapeDtypeStruct(p.shape, p.dtype) for p in pairs],
        scratch_shapes=[pltpu.SemaphoreType.DMA((n, N_CHIP - 1)), pltpu.SemaphoreType.DMA((n, N_CHIP - 1)),
                        pltpu.SemaphoreType.DMA((n,))],
        compiler_params=pltpu.CompilerParams(has_side_effects=True),
        name=name,
    )(*pairs)


def _scatter_two_level(blocks, name):
    swapped = _pair_swap(blocks, name + "_pair")
    core = lax.axis_index("c")
    pairs = []
    for b, s in zip(blocks, swapped):
        mine = lax.dynamic_index_in_dim(b.reshape((N_CHIP, 2) + b.shape[1:]), core, axis=1, keepdims=False)
        pairs.append((mine.astype(F32) + s.astype(F32)).astype(b.dtype))
    return _chip_scatter(pairs, name + "_chip")


def _riding(body, n_in, n_out, n_scratch, ride, grid):
    if ride is None:
        return body, [], [], [], []
    arrays, gather = ride
    n = len(arrays)

    def at(step_of):
        hit = pl.program_id(0) == step_of(grid[0])
        for d in range(1, len(grid)):
            hit = jnp.logical_and(hit, pl.program_id(d) == step_of(grid[d]))
        return hit

    def wrapped(*refs):
        ins, rin = refs[:n_in], refs[n_in:n_in + n]
        outs = refs[n_in + n:n_in + n + n_out]
        rout = refs[n_in + n + n_out:n_in + 2 * n + n_out]
        scratch = refs[n_in + 2 * n + n_out:n_in + 2 * n + n_out + n_scratch]
        sems = refs[n_in + 2 * n + n_out + n_scratch:]

        @pl.when(at(lambda size: 0))
        def _():
            _exchange_start(rin, rout, sems, gather)

        body(*ins, *outs, *scratch)

        @pl.when(at(lambda size: size - 1))
        def _():
            _exchange_wait(rin, rout, sems, gather)

    hbm, out_shape, sems = _exchange_plumbing(arrays, gather)
    return wrapped, hbm, hbm, out_shape, sems


def _sum_contributions(c_ref):
    g = c_ref[0].astype(F32)
    for j in range(1, c_ref.shape[0]):
        g = g + c_ref[j].astype(F32)
    return g


def _adamw_update(g, w_ref, m_ref, v_ref, g_ref, d_ref, nm_ref, nv_ref):
    m_new = ADAM_B1 * m_ref[...] + (1.0 - ADAM_B1) * g
    v_new = ADAM_B2 * v_ref[...] + (1.0 - ADAM_B2) * (g * g)
    m_hat = m_new / (1.0 - ADAM_B1 ** ADAM_STEP)
    v_hat = v_new / (1.0 - ADAM_B2 ** ADAM_STEP)
    g_ref[...] = g
    d_ref[...] = -ADAM_LR * (m_hat / (jnp.sqrt(v_hat) + ADAM_EPS) + ADAM_WD * w_ref[...])
    nm_ref[...] = m_new
    nv_ref[...] = v_new


def _adamw_layers(w, m, v, contrib0, contrib1, name):
    _, r, c = w.shape
    br = _pick(r, (256, 128, 64, 32, 16, 8))
    nb = r // br

    def body(w_ref, m_ref, v_ref, c0_ref, c1_ref, g_ref, d_ref, nm_ref, nv_ref):
        g = jnp.where(pl.program_id(0) == 0, _sum_contributions(c0_ref), _sum_contributions(c1_ref))
        _adamw_update(g, w_ref, m_ref, v_ref, g_ref, d_ref, nm_ref, nv_ref)

    spec = pl.BlockSpec((None, br, c), lambda l, i: (l, i, 0))
    return pl.pallas_call(
        body,
        grid=(DEPTH, nb),
        in_specs=[spec, spec, spec,
                  pl.BlockSpec((contrib0.shape[0], br, c), lambda l, i: (0, jnp.where(l == 0, i, nb - 1), 0)),
                  pl.BlockSpec((contrib1.shape[0], br, c), lambda l, i: (0, jnp.where(l == 1, i, 0), 0))],
        out_specs=[spec] * 4,
        out_shape=[jax.ShapeDtypeStruct(w.shape, F32)] * 4,
        compiler_params=_cparams(("arbitrary", "arbitrary")),
        name=name,
    )(w, m, v, contrib0, contrib1)


def _adamw(w, m, v, contrib, name):
    r, c = w.shape
    br = _pick(r, (256, 128, 64, 32, 16, 8))

    def body(w_ref, m_ref, v_ref, c_ref, g_ref, d_ref, nm_ref, nv_ref):
        _adamw_update(_sum_contributions(c_ref), w_ref, m_ref, v_ref, g_ref, d_ref, nm_ref, nv_ref)

    spec = pl.BlockSpec((br, c), lambda i: (i, 0))
    cspec = pl.BlockSpec((contrib.shape[0], br, c), lambda i: (0, i, 0))
    return pl.pallas_call(
        body,
        grid=(r // br,),
        in_specs=[spec, spec, spec, cspec],
        out_specs=[spec] * 4,
        out_shape=[jax.ShapeDtypeStruct((r, c), F32)] * 4,
        compiler_params=_cparams(("parallel",)),
        name=name,
    )(w, m, v, contrib)


def _silu(x):
    return x * jax.nn.sigmoid(x)


def _rope_angles(pos, n_freq):
    inv = ROPE_BASE ** (-jnp.arange(n_freq, dtype=F32) / n_freq)
    return pos[:, None] * inv[None, :]


def _with_ctx_rows(cos, sin):
    return (jnp.concatenate([jnp.ones((CTX_LEN, 128), F32), cos], axis=0),
            jnp.concatenate([jnp.zeros((CTX_LEN, 128), F32), sin], axis=0))


def _rope_tables():
    rows_n = SEQ // GRID_W
    rows = jnp.repeat(jnp.arange(rows_n, dtype=F32), GRID_W)
    cols = jnp.tile(jnp.arange(GRID_W, dtype=F32), rows_n)
    ang_r = _rope_angles(rows, B_HD // 4)
    ang_c = _rope_angles(cols, B_HD // 4)
    cos_b = jnp.tile(jnp.concatenate([jnp.cos(ang_r)] * 2 + [jnp.cos(ang_c)] * 2, axis=1), (1, 2))
    sin_b = jnp.tile(jnp.concatenate([-jnp.sin(ang_r), jnp.sin(ang_r), -jnp.sin(ang_c), jnp.sin(ang_c)], axis=1), (1, 2))
    ang = _rope_angles(jnp.arange(SEQ, dtype=F32), C_HD // 2)
    cos_c = jnp.concatenate([jnp.cos(ang)] * 2, axis=1)
    sin_c = jnp.concatenate([-jnp.sin(ang), jnp.sin(ang)], axis=1)
    return _with_ctx_rows(cos_b, sin_b), _with_ctx_rows(cos_c, sin_c)


def _halves(a):
    return a[:4], a[4:]


def _delta_gates(ab, a_log, dt_bias):
    beta = jax.nn.sigmoid(ab[:, :8])
    g = -jnp.exp(a_log)[None, :] * jax.nn.softplus(ab[:, 8:] + dt_bias[None, :])
    gch = g.reshape(N_CHUNK, CHUNK, 8)
    tri = jnp.tril(jnp.ones((CHUNK, CHUNK), F32))
    fwd = jnp.einsum("ij,cjh->cih", tri, gch[..., :4], precision=HIGHEST)
    bwd = jnp.einsum("ji,cjh->cih", tri, gch[..., 4:], precision=HIGHEST)
    gc = jnp.concatenate([fwd, bwd], axis=-1)
    gl = jnp.sum(gch, axis=1)
    rows = lambda a: _halves(a.transpose(2, 0, 1)[:, :, None, :])
    return rows(beta.reshape(N_CHUNK, CHUNK, 8)), rows(gc), _halves(gl.T[:, :, None, None])


def _ret_consts(c_decay):
    lg = jax.nn.log_sigmoid(c_decay)
    idx = jnp.arange(CHUNK, dtype=F32)
    diff = idx[:, None] - idx[None, :]
    lgf, lgb = lg[:4, None, None], lg[4:, None, None]
    dm = jnp.concatenate([jnp.exp(jnp.where(diff >= 0, diff * lgf, -jnp.inf)),
                          jnp.exp(jnp.where(diff <= 0, -diff * lgb, -jnp.inf))], axis=0)
    qs = jnp.concatenate([jnp.exp((idx + 1.0)[None, :] * lg[:4, None]),
                          jnp.exp((CHUNK - idx)[None, :] * lg[4:, None])], axis=0)[:, :, None]
    ks = jnp.concatenate([jnp.exp((CHUNK - 1.0 - idx)[None, :] * lg[:4, None]),
                          jnp.exp(idx[None, :] * lg[4:, None])], axis=0)[:, :, None]
    return dm, qs, ks, jnp.exp(CHUNK * lg)[:, None, None]


A_PIECES = ((0, True, A_DK ** -0.5, "a_q"), (1, True, 1.0, "a_k"), (2, False, 1.0, "a_v"))
B_ROPE_COLS = [C_BQ // 512, C_BKV // 256, 0, 0]
C_ROPE_COLS = [C_CQ // 512, C_CK // 512, 0, 0]
MERGE_COLS = [0, 0, 0, C_MERGE // 1024, C_MERGE // 1024 + 1, C_MERGE // 1024 + 2]


def _conv8(conv_w):
    return jnp.pad(conv_w, ((0, 8 - A_CONV), (0, 0)))


W_IN_TILES = {"nn": (2176, 512, 1024), "nt": (1088, 1024, 2176), "db": (1024, 512, ROWS)}


def _core_forward(h, w16, p, rides):
    res = _matmul(h, w16, "w_in", "nn", W_IN_TILES["nn"], ride=rides.get("w_in"))
    proj, rode = (res[0], {"w_in": res[1:]}) if "w_in" in rides else (res, {})
    wb = p["w_branch"] if "w_in" not in rides else _unshard_layer("w_branch", rode["w_in"][0])
    (cos_b, sin_b), (cos_c, sin_c) = _rope_tables()
    conv8 = _conv8(p["a_conv_w"])
    q, k, v = [_a_prep_fwd(proj, conv8, col, nrm, scl, nm) for col, nrm, scl, nm in A_PIECES]
    gates = _delta_gates(proj[:, C_AB:C_AB + 16], p["a_log"], p["a_dt_bias"])
    res = _delta_fwd_call(q, k, v, *gates, ride=rides.get("delta"))
    (of, orv, ssf, ssr, tsf, tsr), rode["delta"] = res[:6], res[6:]
    (y_a,) = _a_out.fwd([(of, orv), proj], [p["a_norm_w"][None, :]], [0, C_AZ // 512])

    qb, kvb = _b_rope.fwd([proj, proj, cos_b, sin_b], [], B_ROPE_COLS)
    res = _attn_fwd_call(qb, kvb, p["b_sink"], ride=rides.get("attn"))
    ob, rode["attn"] = res[0], res[1:]
    (y_b,) = _b_out.fwd([ob, proj], [], [0, C_BZ // 512])

    qc, kc = _c_rope.fwd([proj, proj, cos_c, sin_c], [], C_ROPE_COLS)
    res = _ret_fwd_call(qc, kc, proj, C_CV // 512, *_ret_consts(p["c_decay"]), ride=rides.get("ret"))
    (cf, cr, csf, csr), rode["ret"] = res[:4], res[4:]
    (y_c,) = _c_out.fwd([(cf, cr), proj], [p["c_norm_w"][None, :]], [0, C_CZ // 512])

    (merged,) = _branch_merge.fwd([y_a, y_b, y_c, proj, proj, proj], [wb[0], wb[1], wb[2]], MERGE_COLS)
    saved = dict(proj=proj, q=q, k=k, v=v, of=of, orv=orv, ss=(ssf, ssr), ts=(tsf, tsr), qb=qb, kvb=kvb, ob=ob,
                 qc=qc, kc=kc, cf=cf, cr=cr, cs=(csf, csr), y=(y_a, y_b, y_c), wb=wb)
    return merged, saved, rode


def _core_backward(h, w16, p, s, dmerged, rides, branch_rides_in_attn=False):
    proj, wb = s["proj"], s["wb"]
    (cos_b, sin_b), (cos_c, sin_c) = _rope_tables()
    conv8 = _conv8(p["a_conv_w"])
    y_a, y_b, y_c = s["y"]
    rode = {}

    (*dy, dma, dmb, dmc), dwb = _branch_merge.bwd([y_a, y_b, y_c, proj, proj, proj], [wb[0], wb[1], wb[2]],
                                                   [dmerged], MERGE_COLS, bf16_rows=(3, 4, 5))
    dwb = jnp.stack(dwb)

    consts, consts_vjp = jax.vjp(_ret_consts, p["c_decay"])
    (do_c, dcz), (dcnw,) = _c_out.bwd([(s["cf"], s["cr"]), proj], [p["c_norm_w"][None, :]], [dy[2]],
                                      [0, C_CZ // 512], bf16_rows=(1,))
    g = _ret_bwd_call(s["qc"], s["kc"], proj, C_CV // 512, *consts, s["cs"], do_c, ride=rides.get("ret"))
    rode["ret"] = g[10:]
    (dcq, dck), _ = _c_rope.bwd([proj, proj, cos_c, sin_c], [], [(g[0], g[1]), (g[2], g[3])], C_ROPE_COLS,
                                bf16_rows=(0, 1))
    dcv = (g[4] + g[5]).astype(BF16)
    (dc_decay,) = consts_vjp(tuple(g[6:10]))

    (dob, dbz), _ = _b_out.bwd([s["ob"], proj], [], [dy[1]], [0, C_BZ // 512], bf16_rows=(1,))
    attn_ride = rides.get("attn")
    if branch_rides_in_attn:
        attn_ride = (list(attn_ride[0]) + [_reshard_layer("w_branch", dwb).astype(BF16)], attn_ride[1])
    res = _attn_bwd_call(s["qb"], s["kvb"], p["b_sink"], dob, ride=attn_ride)
    (dqb, dkvb, dsink), rode["attn"] = res[:3], res[3:]
    (dbq, dbkv), _ = _b_rope.bwd([proj, proj, cos_b, sin_b], [], [dqb, dkvb], B_ROPE_COLS, bf16_rows=(0, 1))

    ab = proj[:, C_AB:C_AB + 16]
    gates, gates_vjp = jax.vjp(_delta_gates, ab, p["a_log"], p["a_dt_bias"])
    (do_a, daz), (danw,) = _a_out.bwd([(s["of"], s["orv"]), proj], [p["a_norm_w"][None, :]], [dy[0]],
                                      [0, C_AZ // 512], bf16_rows=(1,))
    g = _delta_bwd_call(s["q"], s["k"], s["v"], *gates, s["ss"], s["ts"], do_a, ride=rides.get("delta"))
    rode["delta"] = g[12:]
    dgates = ((g[6], g[7]), (g[8], g[9]), (g[10], g[11]))
    dab, da_log, ddt = gates_vjp(dgates)
    dpre, dconv = [], []
    for (col, nrm, scl, nm), df, dr in zip(A_PIECES, (g[0], g[2], g[4]), (g[1], g[3], g[5])):
        dx, dw = _a_prep_bwd(proj, conv8, col, nrm, scl, df, dr, nm)
        dpre.append(dx)
        dconv.append(dw[:A_CONV])

    dproj = jnp.concatenate(dpre + [daz, dbq, dbz, dcq, dck, dcv, dcz, dma, dmb, dmc, dbkv,
                                    jnp.pad(dab, ((0, 0), (0, IN_PAD - C_AB - 16))).astype(BF16)], axis=1)
    dh = _matmul(dproj, w16, "w_in_da", "nt", W_IN_TILES["nt"])
    dw = _matmul(h.T.astype(BF16), dproj, "w_in_db", "nn", W_IN_TILES["db"])
    dp = dict(a_conv_w=jnp.concatenate(dconv, axis=1), a_log=da_log, a_dt_bias=ddt, a_norm_w=danw[0],
              b_sink=dsink[:, 0], c_decay=dc_decay, c_norm_w=dcnw[0], w_branch=dwb)
    return dh, dw, dp, rode


CORE_PARAMS = ("a_conv_w", "a_log", "a_dt_bias", "a_norm_w", "b_sink", "c_decay", "c_norm_w", "w_branch")


W_IN_SHARD = IN_WIDTH // N_DEV
W_IN_RUNS = ((0, 2048, 0), (2064, 512, C_BQ), (2832, 512, C_BZ), (3344, 5120, C_CQ), (2576, 256, C_BKV),
             (2048, 16, C_AB))


def _shard_overlap(start, width, j):
    lo, hi = max(start, j * W_IN_SHARD), min(start + width, (j + 1) * W_IN_SHARD)
    return (lo, hi) if lo < hi else None


def _w_in_from_shards(g):
    parts = []
    for start, width, _ in W_IN_RUNS:
        for j in range(N_DEV):
            span = _shard_overlap(start, width, j)
            if span:
                parts.append(g[j, :, span[0] - j * W_IN_SHARD:span[1] - j * W_IN_SHARD])
    parts.append(jnp.zeros((D_MODEL, IN_PAD - IN_WIDTH), g.dtype))
    return jnp.concatenate(parts, axis=1)


def _w_in_blocks(dw):
    blocks = []
    for j in range(N_DEV):
        parts = []
        for start, width, pad in sorted(W_IN_RUNS):
            span = _shard_overlap(start, width, j)
            if span:
                parts.append(dw[:, pad + span[0] - start:pad + span[1] - start])
        blocks.append(jnp.concatenate(parts, axis=1))
    return jnp.stack(blocks)


LAYER_SHARDED = ("w_ada", "w_in", "w_branch", "w_out")


def _unshard_layer(name, g):
    if name == "w_branch":
        return g.transpose(1, 2, 0, 3).reshape(3, BR_WIDTH, D_MODEL)
    if name == "w_out":
        return g.reshape(D_MODEL, D_MODEL)
    return g.transpose(1, 0, 2).reshape(D_MODEL, -1)


def _reshard_layer(name, w):
    if name == "w_branch":
        return w.reshape(3, BR_WIDTH, N_DEV, D_MODEL // N_DEV).transpose(2, 0, 1, 3)
    if name == "w_out":
        return w.reshape(N_DEV, D_MODEL // N_DEV, D_MODEL)
    return w.reshape(D_MODEL, N_DEV, -1).transpose(1, 0, 2)


def _layer_weights(gathered):
    out = {n: _unshard_layer(n, g) for n, g in gathered.items() if n != "w_in"}
    out["w_in16"] = _w_in_from_shards(gathered["w_in"])
    return out


def _grad_blocks(name, g):
    return (_w_in_blocks(g) if name == "w_in" else _reshard_layer(name, g)).astype(BF16)


def _forward_backward(small, layer0, shards0, shards1, x, c, ctx, loss_target):
    c_ctx = small["c_ctx"]
    sc16 = jnp.zeros((16, D_MODEL), F32).at[0].set(_silu(c)).at[1].set(_silu(c_ctx))
    xs = jnp.concatenate([ctx, x], axis=0)
    weights = [dict(layer0), None]
    layers = []
    for l in range(DEPTH):
        wl = weights[l]
        mod16 = _matmul(sc16, wl["w_ada"], "ada") + small["b_ada"][l][None, :]
        mod_cx = jnp.stack([mod16[1], mod16[0]])
        shift, scale, gate = jnp.split(mod_cx, 3, axis=1)
        nw = small["norm_w"][l][None, :]
        (h,) = _norm_mod.fwd([xs], [nw, shift, scale])
        p = {n: small[n][l] for n in CORE_PARAMS if n != "w_branch"}
        p["w_branch"] = wl.get("w_branch")
        rides = {}
        if l == 0:
            rides = {"w_in": ([shards0["w_branch"], shards0["w_out"]], True), "delta": ([shards1["w_in"]], True),
                     "attn": ([shards1["w_ada"]], True), "ret": ([shards1["w_branch"], shards1["w_out"]], True)}
        merged, saved, rode = _core_forward(h, wl["w_in16"], p, rides)
        if l == 0:
            wl["w_out"] = _unshard_layer("w_out", rode["w_in"][1])
            weights[1] = _layer_weights(dict(w_in=rode["delta"][0], w_ada=rode["attn"][0],
                                             w_branch=rode["ret"][0], w_out=rode["ret"][1]))
        (xs_next,) = _out_residual.fwd([xs, merged], [wl["w_out"], gate])
        layers.append(dict(xs=xs, h=h, p=p, saved=saved, merged=merged, gate=gate, nw=nw, shift=shift, scale=scale))
        xs = xs_next
    fw = small["final_norm_w"][None, :]
    xs = xs[CTX_LEN:]
    (per_row,) = _loss_rows.fwd([xs, loss_target], [fw])
    loss = jnp.sum(per_row[:, 0])

    d_per_row = jnp.zeros((SEQ, 128), F32).at[:, 0].set(1.0)
    (dxs,), (dfw,) = _loss_rows.bwd([xs, loss_target], [fw], [d_per_row])
    dxs = jnp.pad(dxs, ((CTX_LEN, 0), (0, 0)))
    small_names = tuple(n for n in CORE_PARAMS if n != "w_branch") + ("b_ada", "norm_w")
    dsmall = {n: [None] * DEPTH for n in small_names}
    dlayer = [None] * DEPTH
    contrib0 = contrib1 = None
    dsc16 = jnp.zeros((16, D_MODEL), F32)
    for l in reversed(range(DEPTH)):
        s, wl = layers[l], weights[l]
        (dres, dmerged), (dw_out, dgate) = _out_residual.bwd([s["xs"], s["merged"]], [wl["w_out"], s["gate"]], [dxs])
        rides = {}
        if l == 0:
            blocks1 = {n: _grad_blocks(n, g) for n, g in dlayer[1].items()}
            rides = {"ret": ([blocks1["w_branch"], blocks1["w_out"]], False),
                     "attn": ([_reshard_layer("w_out", dw_out).astype(BF16), blocks1["w_ada"]], False),
                     "delta": ([blocks1["w_in"]], False)}
        dh, dw_in, dp, rode = _core_backward(s["h"], wl["w_in16"], s["p"], s["saved"], dmerged, rides,
                                             branch_rides_in_attn=(l == 0))
        if l == 0:
            contrib1 = dict(w_in=rode["delta"][0], w_ada=rode["attn"][1], w_branch=rode["ret"][0],
                            w_out=rode["ret"][1])
            contrib0 = dict(w_out=rode["attn"][0], w_branch=rode["attn"][2])
        (dxn,), (dnw, dshift, dscale) = _norm_mod.bwd([s["xs"]], [s["nw"], s["shift"], s["scale"]], [dh])
        dxs = dres + dxn
        dmod_cx = jnp.concatenate([dshift, dscale, dgate], axis=1)
        dmod16 = jnp.zeros((16, 3 * D_MODEL), F32).at[0].set(dmod_cx[1]).at[1].set(dmod_cx[0])
        dsc16 = dsc16 + _matmul(dmod16, wl["w_ada"], "ada_da", "nt")
        dlayer[l] = dict(w_ada=_matmul(sc16, dmod16, "ada_db", "tn"), w_in=dw_in,
                         w_branch=dp["w_branch"], w_out=dw_out)
        for n in small_names:
            if n in dp:
                dsmall[n][l] = dp[n]
        dsmall["norm_w"][l] = dnw[0]
        dsmall["b_ada"][l] = dmod_cx[0] + dmod_cx[1]
    gsmall = {n: jnp.stack(v) for n, v in dsmall.items()}
    gsmall["final_norm_w"] = dfw[0]
    sig = jax.nn.sigmoid(c_ctx)
    gsmall["c_ctx"] = dsc16[1] * sig * (1.0 + c_ctx * (1.0 - sig))
    return loss, dxs[CTX_LEN:], gsmall, {n: dlayer[0][n] for n in ("w_ada", "w_in")}, contrib0, contrib1


SMALL = ("c_ctx", "b_ada", "norm_w", "a_log", "a_dt_bias", "a_norm_w", "b_sink", "c_decay", "c_norm_w",
         "final_norm_w")
WEIGHTS = ("c_ctx", "w_ada", "b_ada", "norm_w", "w_in", "a_conv_w", "a_log", "a_dt_bias", "a_norm_w", "b_sink",
           "c_decay", "c_norm_w", "w_branch", "w_out", "final_norm_w")
SMALL_PACK = 12288


def _unshard_conv(g):
    return g.transpose(1, 2, 0, 3).reshape(DEPTH, A_CONV, 3 * A_WIDTH)


def _reshard_conv(w):
    return w.reshape(DEPTH, A_CONV, N_DEV, 3 * A_WIDTH // N_DEV).transpose(2, 0, 1, 3)


def _pack_small(tree):
    flat = jnp.concatenate([tree[n].reshape(-1) for n in SMALL])
    return jnp.pad(flat, (0, SMALL_PACK - flat.shape[0])).reshape(SMALL_PACK // 128, 128)


def _unpack_small(packed, like):
    flat = packed.reshape(-1)
    out, off = {}, 0
    for n in SMALL:
        size = math.prod(like[n].shape)
        out[n] = flat[off:off + size].reshape(like[n].shape)
        off += size
    return out


def kernel(x, c, ctx, c_ctx, w_ada, b_ada, norm_w, w_in, a_conv_w, a_log, a_dt_bias, a_norm_w, b_sink, c_decay, c_norm_w, w_branch, w_out, final_norm_w, loss_target, m_c_ctx, m_w_ada, m_b_ada, m_norm_w, m_w_in, m_a_conv_w, m_a_log, m_a_dt_bias, m_a_norm_w, m_b_sink, m_c_decay, m_c_norm_w, m_w_branch, m_w_out, m_final_norm_w, v_c_ctx, v_w_ada, v_b_ada, v_norm_w, v_w_in, v_a_conv_w, v_a_log, v_a_dt_bias, v_a_norm_w, v_b_sink, v_c_decay, v_c_norm_w, v_w_branch, v_w_out, v_final_norm_w):
    w = dict(c_ctx=c_ctx, w_ada=w_ada, b_ada=b_ada, norm_w=norm_w, w_in=w_in, a_conv_w=a_conv_w, a_log=a_log,
             a_dt_bias=a_dt_bias, a_norm_w=a_norm_w, b_sink=b_sink, c_decay=c_decay, c_norm_w=c_norm_w,
             w_branch=w_branch, w_out=w_out, final_norm_w=final_norm_w)
    m = dict(c_ctx=m_c_ctx, w_ada=m_w_ada, b_ada=m_b_ada, norm_w=m_norm_w, w_in=m_w_in, a_conv_w=m_a_conv_w,
             a_log=m_a_log, a_dt_bias=m_a_dt_bias, a_norm_w=m_a_norm_w, b_sink=m_b_sink, c_decay=m_c_decay,
             c_norm_w=m_c_norm_w, w_branch=m_w_branch, w_out=m_w_out, final_norm_w=m_final_norm_w)
    v = dict(c_ctx=v_c_ctx, w_ada=v_w_ada, b_ada=v_b_ada, norm_w=v_norm_w, w_in=v_w_in, a_conv_w=v_a_conv_w,
             a_log=v_a_log, a_dt_bias=v_a_dt_bias, a_norm_w=v_a_norm_w, b_sink=v_b_sink, c_decay=v_c_decay,
             c_norm_w=v_c_norm_w, w_branch=v_w_branch, w_out=v_w_out, final_norm_w=v_final_norm_w)

    shards = {n: w[n].astype(BF16) for n in LAYER_SHARDED}
    first = _exchange([shards["w_ada"][0], shards["w_in"][0], w["a_conv_w"]], True, "gather_layer0")
    layer0 = _layer_weights(dict(w_ada=first[0], w_in=first[1]))
    small_w = {n: w[n] for n in SMALL}
    small_w["a_conv_w"] = _unshard_conv(first[2])
    loss, gx, gw, glayer0, contrib0, contrib1 = _forward_backward(
        small_w, layer0, {n: shards[n][0] for n in ("w_branch", "w_out")}, {n: shards[n][1] for n in LAYER_SHARDED},
        x[0], c[0], ctx[0], loss_target[0])
    loss = lax.psum(loss, ("x", "y", "c"))

    last = _scatter_two_level([_reshard_layer("w_ada", glayer0["w_ada"]).astype(BF16),
                               _grad_blocks("w_in", glayer0["w_in"]), _reshard_conv(gw["a_conv_w"])],
                              "scatter_layer0")
    contrib0["w_ada"], contrib0["w_in"] = last[0], last[1]
    small = _exchange([_pack_small(gw)], True, "gather_small_grads")[0]

    grad, delta, new_m, new_v = {}, {}, {}, {}
    for n in LAYER_SHARDED:
        shp = w[n].shape
        per_layer = (math.prod(shp[1:-1]), shp[-1])
        outs = _adamw_layers(*[a.reshape((DEPTH,) + per_layer) for a in (w[n], m[n], v[n])],
                             *[cb.reshape(cb.shape[:1] + per_layer) for cb in (contrib0[n], contrib1[n])], "adamw_" + n)
        grad[n], delta[n], new_m[n], new_v[n] = [o.reshape(shp) for o in outs]
    shp = a_conv_w.shape
    two_d = (math.prod(shp[:-1]), shp[-1])
    outs = _adamw(*[a.reshape(two_d) for a in (a_conv_w, m_a_conv_w, v_a_conv_w)],
                  last[2].reshape(last[2].shape[:1] + two_d), "adamw_a_conv_w")
    grad["a_conv_w"], delta["a_conv_w"], new_m["a_conv_w"], new_v["a_conv_w"] = [o.reshape(shp) for o in outs]
    outs = _adamw(_pack_small(w), _pack_small(m), _pack_small(v), small, "adamw_small")
    for tree, packed in zip((grad, delta, new_m, new_v), outs):
        tree.update(_unpack_small(packed, w))

    return (loss, gx[None], *[grad[n] for n in WEIGHTS], *[delta[n] for n in WEIGHTS],
            *[new_m[n] for n in WEIGHTS], *[new_v[n] for n in WEIGHTS])
```

```python
import functools
import math

import jax
import jax.numpy as jnp
from jax import lax
from jax.experimental import pallas as pl
from jax.experimental.pallas import tpu as pltpu

F32 = jnp.float32
BF16 = jnp.bfloat16
HIGHEST = lax.Precision.HIGHEST

D_MODEL = 1024
SEQ = 4096
DEPTH = 2
GRID_W = 64
CTX_LEN = 256
EPS = 1e-6
ROPE_BASE = 10000.0
BR_WIDTH = D_MODEL // 2
A_DK = 128
A_HEADS = 4
A_WIDTH = 512
A_CONV = 5
B_HD = 64
B_Q_HEADS = 8
B_KV_HEADS = 2
WINDOW = 128
B_BLOCK = 128
C_HD = 128
C_HEADS = 4
C_WIDTH = 512
CHUNK = 64
RET_CHUNK = 128
ADAM_LR = 0.001
ADAM_B1 = 0.9
ADAM_B2 = 0.999
ADAM_EPS = 1e-08
ADAM_WD = 0.01
ADAM_STEP = 10

N_DEV = 8
ROWS = CTX_LEN + SEQ
N_CHUNK = ROWS // CHUNK
IN_WIDTH = 8464
IN_PAD = 8704
NEG = -1e30

VMEM_LIMIT = 48 * 1024 * 1024
MESH = pl.DeviceIdType.MESH

C_AQ, C_AK, C_AV, C_AZ, C_BQ, C_BZ, C_CQ, C_CK, C_CV, C_CZ = (i * 512 for i in range(10))
C_MERGE = 5120
C_BKV = 8192
C_AB = 8448


def _cparams(sem=None):
    if sem is None:
        return pltpu.CompilerParams(vmem_limit_bytes=VMEM_LIMIT)
    return pltpu.CompilerParams(dimension_semantics=sem, vmem_limit_bytes=VMEM_LIMIT)


def _dg(a, b, ca, cb, prec=None):
    return lax.dot_general(a, b, (((ca,), (cb,)), ((), ())), preferred_element_type=F32, precision=prec)


@functools.partial(jax.custom_vjp, nondiff_argnums=(2, 3))
def _bdot(a, b, ca, cb):
    return _dg(a.astype(BF16), b.astype(BF16), ca, cb)


def _bdot_fwd(a, b, ca, cb):
    return _bdot(a, b, ca, cb), (a, b)


def _bdot_bwd(ca, cb, res, ct):
    a, b = res
    da = _bdot(ct, b, 1, 1 - cb) if ca == 1 else _bdot(b, ct, 1 - cb, 1)
    db = _bdot(a, ct, 1 - ca, 0) if cb == 0 else _bdot(ct, a, 0, 1 - ca)
    return da, db


_bdot.defvjp(_bdot_fwd, _bdot_bwd)


def _hdot(a, b):
    return _dg(a, b, 1, 0, lax.Precision.HIGH)


def _k_silu(x):
    return x / (1.0 + jnp.exp(-x))


def _k_sigmoid(x):
    return 1.0 / (1.0 + jnp.exp(-x))


@jax.custom_vjp
def _swap64(x):
    return pltpu.roll(x, 64, 1)


_swap64.defvjp(lambda x: (pltpu.roll(x, 64, 1), None), lambda _, ct: (pltpu.roll(ct, 64, 1),))


def _swap16_impl(x):
    lane = lax.broadcasted_iota(jnp.int32, x.shape, 1)
    return jnp.where((lane & 16) == 0, pltpu.roll(x, 112, 1), pltpu.roll(x, 16, 1))


@jax.custom_vjp
def _swap16(x):
    return _swap16_impl(x)


_swap16.defvjp(lambda x: (_swap16_impl(x), None), lambda _, ct: (_swap16_impl(ct),))


def _pick(dim, prefs):
    for p in prefs:
        if dim % p == 0:
            return p
    return dim


def _matmul(a, b, name, mode="nn", tiles=None, ride=None):
    ca, cb = {"nn": (1, 0), "nt": (1, 1), "tn": (0, 0)}[mode]
    m, k = a.shape[1 - ca], a.shape[ca]
    n = b.shape[1 - cb]
    if tiles is None:
        tiles = (_pick(m, (1088, 1024, 512, 256, 128)), _pick(n, (512, 256, 128)),
                 _pick(k, (1088, 1024, 512, 256, 128) if mode == "tn" else (2176, 2048, 1024, 512, 256, 128)))
    tm, tn, tk = tiles
    nk = k // tk
    a_spec = (pl.BlockSpec((tm, tk), lambda i, j, kk: (i, kk)) if ca == 1
              else pl.BlockSpec((tk, tm), lambda i, j, kk: (kk, i)))
    b_spec = (pl.BlockSpec((tk, tn), lambda i, j, kk: (kk, j)) if cb == 0
              else pl.BlockSpec((tn, tk), lambda i, j, kk: (j, kk)))

    def body(a_ref, b_ref, o_ref):
        part = _dg(a_ref[...].astype(BF16), b_ref[...].astype(BF16), ca, cb)
        if nk == 1:
            o_ref[...] = part
        else:
            kk = pl.program_id(2)

            @pl.when(kk == 0)
            def _():
                o_ref[...] = part

            @pl.when(kk > 0)
            def _():
                o_ref[...] += part

    grid = (m // tm, n // tn, nk)
    if ride is None:
        return pl.pallas_call(
            body,
            grid=grid,
            in_specs=[a_spec, b_spec],
            out_specs=pl.BlockSpec((tm, tn), lambda i, j, kk: (i, j)),
            out_shape=jax.ShapeDtypeStruct((m, n), F32),
            compiler_params=_cparams(("parallel", "parallel", "arbitrary")),
            name=name,
        )(a, b)
    body, r_in, r_out, r_shape, r_scratch = _riding(body, 2, 1, 0, ride, grid)
    return pl.pallas_call(
        body,
        grid=grid,
        in_specs=[a_spec, b_spec] + r_in,
        out_specs=[pl.BlockSpec((tm, tn), lambda i, j, kk: (i, j))] + r_out,
        out_shape=[jax.ShapeDtypeStruct((m, n), F32)] + r_shape,
        scratch_shapes=r_scratch,
        compiler_params=_cparams(("arbitrary", "arbitrary", "arbitrary")),
        name=name,
    )(a, b, *ride[0])


ROW_BLOCK = 256
ROW_VMEM_BUDGET = 16 * 1024 * 1024


def _pieces(val, pw):
    return [val[:, j * pw:(j + 1) * pw] for j in range(val.shape[1] // pw)]


def _flat(groups):
    arrays, sizes = [], []
    for g in groups:
        g = g if isinstance(g, (tuple, list)) else (g,)
        arrays += list(g)
        sizes.append(len(g))
    return arrays, sizes


def _regroup(refs, sizes):
    out, at = [], 0
    for n in sizes:
        val = refs[at][...]
        for r in refs[at + 1:at + n]:
            val = val + r[...]
        out.append(val)
        at += n
    return out


class _Rowwise:
    def __init__(self, fn, name, row_wpw, par_pw, out_wpw, n_diff=None):
        self.fn, self.name, self.row_wpw, self.par_pw, self.out_wpw = fn, name, row_wpw, par_pw, out_wpw
        self.n_diff = len(row_wpw) if n_diff is None else n_diff

        @jax.custom_vjp
        def call(rows, params):
            return self.fwd(rows, params)

        def call_fwd(rows, params):
            return self.fwd(rows, params), (rows, params)

        def call_bwd(res, douts):
            return self.bwd(res[0], res[1], douts)

        call.defvjp(call_fwd, call_bwd)
        self.call = call

    def _load(self, row_vals, par_refs, br, with_ctx):
        row = pl.program_id(0) * br + lax.broadcasted_iota(jnp.int32, (br, 1), 0)
        is_ctx = (row < (CTX_LEN if with_ctx else 0)).astype(F32)
        rows = [_pieces(v, pw) for v, (_, pw) in zip(row_vals, self.row_wpw)]
        pars = []
        for p, pw in zip(par_refs, self.par_pw):
            val = p[...].astype(F32)
            if p.shape[0] == 2:
                val = is_ctx * val[0:1, :] + (1.0 - is_ctx) * val[1:2, :]
            pars.append(_pieces(val, pw))
        return rows, pars, is_ctx

    def _block_rows(self, n_rows, widths):
        for br in (1088, 1024, 544, 512, 272):
            if n_rows % br == 0 and 2 * 4 * br * sum(widths) <= ROW_VMEM_BUDGET:
                return br
        return ROW_BLOCK

    def _row_specs(self, br, sizes, cols):
        out = []
        for (w, _), n, c in zip(self.row_wpw, sizes, cols):
            out += [pl.BlockSpec((br, w), lambda i, c=c: (i, c))] * n
        return out

    def fwd(self, rows, params, cols=None):
        arrays, sizes = _flat(rows)
        cols = cols or [0] * len(rows)
        n_rows = arrays[0].shape[0]
        n_in = len(arrays)
        br = self._block_rows(n_rows, [w for (w, _), n in zip(self.row_wpw, sizes) for _ in range(n)]
                              + [w for w, _ in self.out_wpw])

        def body(*refs):
            r, p, _ = self._load(_regroup(refs[:n_in], sizes), refs[n_in:n_in + len(params)], br, n_rows == ROWS)
            for o_ref, pieces, (_, pw) in zip(refs[n_in + len(params):], self.fn(r, p), self.out_wpw):
                for j, piece in enumerate(pieces):
                    o_ref[:, j * pw:(j + 1) * pw] = piece

        return pl.pallas_call(
            body,
            grid=(n_rows // br,),
            in_specs=self._row_specs(br, sizes, cols) + [pl.BlockSpec(p.shape, lambda i: (0, 0)) for p in params],
            out_specs=[pl.BlockSpec((br, w), lambda i: (i, 0)) for w, _ in self.out_wpw],
            out_shape=[jax.ShapeDtypeStruct((n_rows, w), F32) for w, _ in self.out_wpw],
            compiler_params=_cparams(("parallel",)),
            name=self.name + "_fwd",
        )(*arrays, *params)

    def bwd(self, rows, params, douts, cols=None, bf16_rows=()):
        arrays, sizes = _flat(rows)
        darrays, dsizes = _flat(douts)
        cols = cols or [0] * len(rows)
        n_rows = arrays[0].shape[0]
        n_in, n_par, n_dout, n_diff = len(arrays), len(params), len(darrays), self.n_diff
        br = self._block_rows(n_rows, [w for (w, _), n in zip(self.row_wpw, sizes) for _ in range(n)]
                              + [w for (w, _), n in zip(self.out_wpw, dsizes) for _ in range(n)]
                              + [w for w, _ in self.row_wpw[:n_diff]])

        def body(*refs):
            par_refs = refs[n_in:n_in + n_par]
            dout_refs = refs[n_in + n_par:n_in + n_par + n_dout]
            drow_refs = refs[n_in + n_par + n_dout:n_in + n_par + n_dout + n_diff]
            dpar_refs = refs[n_in + n_par + n_dout + n_diff:]

            @pl.when(pl.program_id(0) == 0)
            def _():
                for d in dpar_refs:
                    d[...] = jnp.zeros_like(d)

            r, p, is_ctx = self._load(_regroup(refs[:n_in], sizes), par_refs, br, n_rows == ROWS)
            cts = [_pieces(d, pw) for d, (_, pw) in zip(_regroup(dout_refs, dsizes), self.out_wpw)]
            fixed = r[n_diff:]
            _, vjp = jax.vjp(lambda rd, pp: self.fn(rd + fixed, pp), r[:n_diff], p)
            dr, dp = vjp(cts)
            for d_ref, pieces, (_, pw) in zip(drow_refs, dr, self.row_wpw):
                for j, piece in enumerate(pieces):
                    d_ref[:, j * pw:(j + 1) * pw] = piece.astype(d_ref.dtype)
            for d_ref, pieces, pw in zip(dpar_refs, dp, self.par_pw):
                for j, piece in enumerate(pieces):
                    lanes = slice(j * pw, (j + 1) * pw)
                    if d_ref.shape[0] != 2:
                        d_ref[:, lanes] += piece
                    else:
                        d_ref[0:1, lanes] += jnp.sum(is_ctx * piece, axis=0, keepdims=True)
                        d_ref[1:2, lanes] += jnp.sum((1.0 - is_ctx) * piece, axis=0, keepdims=True)

        par_specs = [pl.BlockSpec(p.shape, lambda i: (0, 0)) for p in params]
        dout_specs = []
        for (w, _), n in zip(self.out_wpw, dsizes):
            dout_specs += [pl.BlockSpec((br, w), lambda i: (i, 0))] * n
        drow_w = [w for w, _ in self.row_wpw[:n_diff]]
        g = pl.pallas_call(
            body,
            grid=(n_rows // br,),
            in_specs=self._row_specs(br, sizes, cols) + par_specs + dout_specs,
            out_specs=[pl.BlockSpec((br, w), lambda i: (i, 0)) for w in drow_w] + par_specs,
            out_shape=[jax.ShapeDtypeStruct((n_rows, w), BF16 if a in bf16_rows else F32) for a, w in enumerate(drow_w)]
            + [jax.ShapeDtypeStruct(p.shape, F32) for p in params],
            compiler_params=_cparams(("arbitrary",)),
            name=self.name + "_bwd",
        )(*arrays, *params, *darrays)
        return list(g[:n_diff]), list(g[n_diff:])


def _fn_norm_mod(rows, pars):
    (x,), (nw,), (shift,), (scale,) = rows[0], pars[0], pars[1], pars[2]
    y = x * lax.rsqrt(jnp.mean(x * x, axis=-1, keepdims=True) + EPS) * nw
    return [[y * (1.0 + scale) + shift]]


def _fn_head_rms_gate(rows, pars):
    (w,) = pars[0]
    return [[o * lax.rsqrt(jnp.mean(o * o, axis=-1, keepdims=True) + EPS) * w * _k_silu(z)
             for o, z in zip(rows[0], rows[1])]]


def _fn_group_norm_gate(rows, pars):
    out = []
    for o, z, w in zip(rows[0], rows[1], pars[0]):
        mu = jnp.mean(o, axis=-1, keepdims=True)
        var = jnp.mean(jnp.square(o - mu), axis=-1, keepdims=True)
        out.append((o - mu) * lax.rsqrt(var + EPS) * w * _k_silu(z))
    return [out]


def _fn_gate(rows, pars):
    return [[o * _k_silu(z) for o, z in zip(rows[0], rows[1])]]


def _fn_branch_merge(rows, pars):
    (ya,), (yb,), (yc,), (ma,), (mb,), (mc,) = rows
    (wa,), (wb,), (wc,) = pars
    return [[_k_sigmoid(ma) * _bdot(ya, wa, 1, 0) + _k_sigmoid(mb) * _bdot(yb, wb, 1, 0)
             + _k_sigmoid(mc) * _bdot(yc, wc, 1, 0)]]


def _fn_out_residual(rows, pars):
    (res,), (merged,), (w,), (gate,) = rows[0], rows[1], pars[0], pars[1]
    return [[res + gate * _bdot(merged, w, 1, 0)]]


def _fn_loss(rows, pars):
    (x,), (target,), (w,) = rows[0], rows[1], pars[0]
    y = x * lax.rsqrt(jnp.mean(x * x, axis=-1, keepdims=True) + EPS) * w
    per_row = 0.5 * jnp.mean(jnp.square(y - target), axis=-1, keepdims=True)
    return [[jnp.broadcast_to(per_row, (per_row.shape[0], 128))]]


def _fn_b_rope(rows, pars):
    q, (k, v), (cos,), (sin,) = rows
    rot = lambda x: x * cos + _swap16(x) * sin
    return [[rot(x) for x in q], [rot(k), v]]


def _fn_c_rope(rows, pars):
    q, k, (cos,), (sin,) = rows
    rot = lambda x: x * cos + _swap64(x) * sin
    return [[rot(x) for x in q], [rot(x) * (C_HD ** -0.5) for x in k]]


_norm_mod = _Rowwise(_fn_norm_mod, "norm_mod", [(D_MODEL, D_MODEL)], [D_MODEL] * 3, [(D_MODEL, D_MODEL)])
_out_residual = _Rowwise(_fn_out_residual, "out_residual", [(D_MODEL, D_MODEL)] * 2, [D_MODEL] * 2,
                         [(D_MODEL, D_MODEL)])
_loss_rows = _Rowwise(_fn_loss, "loss", [(D_MODEL, D_MODEL)] * 2, [D_MODEL], [(128, 128)], n_diff=1)
_a_out = _Rowwise(_fn_head_rms_gate, "a_out", [(512, 128)] * 2, [128], [(512, 128)])
_c_out = _Rowwise(_fn_group_norm_gate, "c_out", [(512, 128)] * 2, [128], [(512, 128)])
_b_out = _Rowwise(_fn_gate, "b_out", [(512, 512)] * 2, [], [(512, 512)])
_branch_merge = _Rowwise(_fn_branch_merge, "branch_merge", [(512, 512)] * 3 + [(D_MODEL, D_MODEL)] * 3,
                         [D_MODEL] * 3, [(D_MODEL, D_MODEL)])
_b_rope = _Rowwise(_fn_b_rope, "b_rope", [(512, 128), (256, 128), (128, 128), (128, 128)], [],
                   [(512, 128), (256, 128)], n_diff=2)
_c_rope = _Rowwise(_fn_c_rope, "c_rope", [(512, 128), (512, 128), (128, 128), (128, 128)], [],
                   [(512, 128), (512, 128)], n_diff=2)


HALO = 8
EXT = ROW_BLOCK + 2 * HALO


def _halo_specs(col, width=512):
    last = ROWS // HALO - 1
    per = ROW_BLOCK // HALO
    prev = pl.BlockSpec((HALO, width), lambda i: (jnp.maximum(i * per - 1, 0), col))
    cur = pl.BlockSpec((ROW_BLOCK, width), lambda i: (i, col))
    nxt = pl.BlockSpec((HALO, width), lambda i: (jnp.minimum((i + 1) * per, last), col))
    return [prev, cur, nxt]


def _extended(prev_ref, cur_ref, next_ref):
    i = pl.program_id(0)
    prev_ok = i >= 2
    next_ok = jnp.logical_and(i >= 1, i < ROWS // ROW_BLOCK - 1)
    return jnp.concatenate([jnp.where(prev_ok, prev_ref[...], 0.0), cur_ref[...],
                            jnp.where(next_ok, next_ref[...], 0.0)], axis=0)


def _conv_taps(x_ext, w_ref, flip):
    acc = None
    for j in range(A_CONV):
        shift = (j - 2) if flip else (2 - j)
        term = w_ref[j:j + 1, :] * pltpu.roll(x_ext, shift % EXT, 0)
        acc = term if acc is None else acc + term
    return acc


def _conv_post(pre_pieces, normalize, scale):
    out = []
    for p in pre_pieces:
        y = _k_silu(p)
        if normalize:
            y = y * lax.rsqrt(jnp.sum(y * y, axis=-1, keepdims=True) + EPS) * scale
        out.append(y)
    return out


def _a_prep_fwd(proj, conv8, col, normalize, scale, name):
    def body(prev_ref, cur_ref, next_ref, w_ref, o_ref):
        pre = _conv_taps(_extended(prev_ref, cur_ref, next_ref), w_ref, False)[HALO:HALO + ROW_BLOCK]
        for h, y in enumerate(_conv_post(_pieces(pre, 128), normalize, scale)):
            o_ref[:, h * 128:(h + 1) * 128] = y

    return pl.pallas_call(
        body,
        grid=(ROWS // ROW_BLOCK,),
        in_specs=_halo_specs(col) + [pl.BlockSpec((8, 512), lambda i: (0, col))],
        out_specs=pl.BlockSpec((ROW_BLOCK, 512), lambda i: (i, 0)),
        out_shape=jax.ShapeDtypeStruct((ROWS, 512), F32),
        compiler_params=_cparams(("parallel",)),
        name=name + "_fwd",
    )(proj, proj, proj, conv8)


def _a_prep_bwd(proj, conv8, col, normalize, scale, dout_f, dout_r, name):
    def body(xp, xc, xn, w_ref, fp, fc, fn_, rp, rc, rn, dx_ref, dw_ref):
        @pl.when(pl.program_id(0) == 0)
        def _():
            dw_ref[...] = jnp.zeros_like(dw_ref)

        x_ext = _extended(xp, xc, xn)
        dout = _extended(fp, fc, fn_) + _extended(rp, rc, rn)
        pre = _conv_taps(x_ext, w_ref, False)
        _, vjp = jax.vjp(lambda p: _conv_post(p, normalize, scale), _pieces(pre, 128))
        (dpre,) = vjp(_pieces(dout, 128))
        dpre = jnp.concatenate(dpre, axis=1)
        dx_ref[...] = _conv_taps(dpre, w_ref, True)[HALO:HALO + ROW_BLOCK].astype(BF16)
        own = dpre[HALO:HALO + ROW_BLOCK]
        for j in range(A_CONV):
            shifted = pltpu.roll(x_ext, (2 - j) % EXT, 0)[HALO:HALO + ROW_BLOCK]
            dw_ref[j:j + 1, :] += jnp.sum(own * shifted, axis=0, keepdims=True)

    return pl.pallas_call(
        body,
        grid=(ROWS // ROW_BLOCK,),
        in_specs=_halo_specs(col) + [pl.BlockSpec((8, 512), lambda i: (0, col))] + _halo_specs(0) + _halo_specs(0),
        out_specs=[pl.BlockSpec((ROW_BLOCK, 512), lambda i: (i, 0)), pl.BlockSpec((8, 512), lambda i: (0, 0))],
        out_shape=[jax.ShapeDtypeStruct((ROWS, 512), BF16), jax.ShapeDtypeStruct((8, 512), F32)],
        compiler_params=_cparams(("arbitrary",)),
        name=name + "_bwd",
    )(proj, proj, proj, conv8, dout_f, dout_f, dout_f, dout_r, dout_r, dout_r)


N_CHAIN = 8


def _rev_chunk(s, chunk):
    n_ctx, n_all = CTX_LEN // chunk, ROWS // chunk
    return jnp.where(s < n_ctx, n_ctx - 1 - s, n_all + n_ctx - 1 - s)


def _scan_specs(step_of, chunk, v_col=0):
    cf = step_of
    cr = lambda n: _rev_chunk(step_of(n), chunk)

    def pair(shape, index):
        return (pl.BlockSpec(shape, lambda n: index(cf(n))), pl.BlockSpec(shape, lambda n: index(cr(n))))

    return dict(
        tok=pair((chunk, 512), lambda c: (c, 0)),
        tokv=pair((chunk, 512), lambda c: (c, v_col)),
        row=pair((4, 1, 1, chunk), lambda c: (0, c, 0, 0)),
        one=pair((4, 1, 1, 1), lambda c: (0, c, 0, 0)),
        state=pair((None, 4, 128, 128), lambda c: (c, 0, 0, 0)),
        tinv=pair((None, 4, chunk, chunk), lambda c: (c, 0, 0, 0)),
    )


def _both(specs, kinds):
    out = []
    for kind in kinds:
        out += list(specs[kind])
    return out


def _scan_call(body, name, in_specs, out_specs, out_shape, operands, ride, chunk):
    grid = (ROWS // chunk,)
    body, r_in, r_out, r_shape, r_scratch = _riding(body, len(in_specs), len(out_specs), 1, ride, grid)
    return pl.pallas_call(
        body,
        grid=grid,
        in_specs=in_specs + r_in,
        out_specs=out_specs + r_out,
        out_shape=out_shape + r_shape,
        scratch_shapes=[pltpu.VMEM((N_CHAIN, 128, 128), F32)] + r_scratch,
        compiler_params=_cparams(("arbitrary",)),
        name=name,
    )(*operands, *(ride[0] if ride else []))


def _chain_masks():
    ii = lax.broadcasted_iota(jnp.int32, (CHUNK, CHUNK), 0)
    jj = lax.broadcasted_iota(jnp.int32, (CHUNK, CHUNK), 1)
    eye = jnp.where(ii == jj, 1.0, 0.0).astype(F32)
    lower = (ii >= jj, ii > jj)
    upper = (ii <= jj, ii < jj)
    return [lower] * 4 + [upper] * 4, eye


def _tri_inv_all(ls, eye):
    doublings = CHUNK.bit_length() - 2
    xs = [eye - l for l in ls]
    ps = [_hdot(l, l) for l in ls]
    for i in range(doublings):
        xs = [x + _hdot(x, p) for x, p in zip(xs, ps)]
        if i < doublings - 1:
            ps = [_hdot(p, p) for p in ps]
    return xs


@jax.custom_vjp
def _inv_saved(l, x):
    return x


def _inv_saved_fwd(l, x):
    return x, x


def _inv_saved_bwd(x, dx):
    return -_bdot(x, _bdot(dx, x, 1, 1), 0, 0), jnp.zeros_like(x)


_inv_saved.defvjp(_inv_saved_fwd, _inv_saved_bwd)


def _delta_chains(q, k, v, beta_r, gcr, gl, s, masks, eye, tinv_saved):
    n = range(len(q))
    beta = [jnp.sum(eye * beta_r[i], axis=1, keepdims=True) for i in n]
    gcc = [jnp.sum(eye * gcr[i], axis=1, keepdims=True) for i in n]
    decay = [jnp.exp(jnp.where(masks[i][0], gcc[i] - gcr[i], NEG)) for i in n]
    kb = [k[i] * beta[i] for i in n]
    lmat = [jnp.where(masks[i][1], _bdot(kb[i], k[i], 1, 1) * decay[i], 0.0) for i in n]
    if tinv_saved is None:
        tinv = _tri_inv_all(lmat, eye)
    else:
        tinv = [_inv_saved(lmat[i], tinv_saved[i]) for i in n]
    eg = [jnp.exp(gcc[i]) for i in n]
    u = [_bdot(tinv[i], v[i] * beta[i], 1, 0) for i in n]
    w = [_bdot(tinv[i], kb[i] * eg[i], 1, 0) for i in n]
    qk = [_bdot(q[i], k[i], 1, 1) * decay[i] for i in n]
    v_new = [u[i] - _bdot(w[i], s[i], 1, 0) for i in n]
    o = [_bdot(q[i] * eg[i], s[i], 1, 0) + _bdot(qk[i], v_new[i], 1, 0) for i in n]
    s_new = [s[i] * jnp.exp(gl[i]) + _bdot(k[i] * jnp.exp(gl[i] - gcc[i]), v_new[i], 0, 0) for i in n]
    return (o, s_new), tinv


def _chain_loads(tok_pairs, small_pairs):
    toks = [[pair[i // 4][:, (i % 4) * 128:(i % 4 + 1) * 128] for i in range(N_CHAIN)] for pair in tok_pairs]
    smalls = [[pair[i // 4][i % 4] for i in range(N_CHAIN)] for pair in small_pairs]
    return toks, smalls


def _delta_fwd_call(q, k, v, beta, gc, gl, ride=None):
    sp = _scan_specs(lambda n: n, CHUNK)

    def body(qf, qr, kf, kr, vf, vr, bf, br, gcrf, gcrr, glf, glr, of, orv, ssf, ssr, tsf, tsr, s_scr):
        @pl.when(pl.program_id(0) == 0)
        def _():
            s_scr[...] = jnp.zeros_like(s_scr)

        masks, eye = _chain_masks()
        (qs, ks, vs), _ = _chain_loads([(qf, qr), (kf, kr), (vf, vr)], [])
        bs = [(bf, br)[i // 4][i % 4, 0] for i in range(N_CHAIN)]
        gcrs = [(gcrf, gcrr)[i // 4][i % 4, 0] for i in range(N_CHAIN)]
        gls = [(glf, glr)[i // 4][i % 4, 0] for i in range(N_CHAIN)]
        ss = [s_scr[i] for i in range(N_CHAIN)]
        (o, s_new), tinv = _delta_chains(qs, ks, vs, bs, gcrs, gls, ss, masks, eye, None)
        for i in range(N_CHAIN):
            d, h = i // 4, i % 4
            (ssf, ssr)[d][h] = ss[i]
            (tsf, tsr)[d][h] = tinv[i]
            (of, orv)[d][:, h * 128:(h + 1) * 128] = o[i]
            s_scr[i] = s_new[i]

    return _scan_call(
        body, "delta_fwd",
        _both(sp, ["tok", "tok", "tok", "row", "row", "one"]),
        _both(sp, ["tok", "state", "tinv"]),
        [jax.ShapeDtypeStruct((ROWS, 512), F32)] * 2 + [jax.ShapeDtypeStruct((N_CHUNK, 4, 128, 128), F32)] * 2
        + [jax.ShapeDtypeStruct((N_CHUNK, 4, CHUNK, CHUNK), F32)] * 2,
        [q, q, k, k, v, v, *beta, *gc, *gl], ride, CHUNK)


def _delta_bwd_call(q, k, v, beta, gc, gl, ssave, tsave, do, ride=None):
    sp = _scan_specs(lambda n: N_CHUNK - 1 - n, CHUNK)

    def body(qf, qr, kf, kr, vf, vr, bf, br, gcrf, gcrr, glf, glr, ssf, ssr, tsf, tsr, dof, dor,
             dqf, dqr, dkf, dkr, dvf, dvr, dbf, dbr, dgcrf, dgcrr, dglf, dglr, ds_scr):
        @pl.when(pl.program_id(0) == 0)
        def _():
            ds_scr[...] = jnp.zeros_like(ds_scr)

        masks, eye = _chain_masks()
        (qs, ks, vs, dos), (ss, ts) = _chain_loads(
            [(qf, qr), (kf, kr), (vf, vr), (dof, dor)], [(ssf, ssr), (tsf, tsr)])
        bs = [(bf, br)[i // 4][i % 4, 0] for i in range(N_CHAIN)]
        gcrs = [(gcrf, gcrr)[i // 4][i % 4, 0] for i in range(N_CHAIN)]
        gls = [(glf, glr)[i // 4][i % 4, 0] for i in range(N_CHAIN)]
        fn = lambda *a: _delta_chains(*a, masks, eye, ts)
        _, vjp, _ = jax.vjp(fn, qs, ks, vs, bs, gcrs, gls, ss, has_aux=True)
        dq, dk, dv, db, dgcr, dgl, ds = vjp((dos, [ds_scr[i] for i in range(N_CHAIN)]))
        for i in range(N_CHAIN):
            d, h = i // 4, i % 4
            hs = slice(h * 128, (h + 1) * 128)
            (dqf, dqr)[d][:, hs] = dq[i]
            (dkf, dkr)[d][:, hs] = dk[i]
            (dvf, dvr)[d][:, hs] = dv[i]
            (dbf, dbr)[d][h, 0] = db[i]
            (dgcrf, dgcrr)[d][h, 0] = dgcr[i]
            (dglf, dglr)[d][h, 0] = dgl[i]
            ds_scr[i] = ds[i]

    tok = jax.ShapeDtypeStruct((ROWS, 512), F32)
    return _scan_call(
        body, "delta_bwd",
        _both(sp, ["tok", "tok", "tok", "row", "row", "one", "state", "tinv", "tok"]),
        _both(sp, ["tok", "tok", "tok", "row", "row", "one"]),
        [tok] * 6 + [jax.ShapeDtypeStruct((4, N_CHUNK, 1, CHUNK), F32)] * 4
        + [jax.ShapeDtypeStruct((4, N_CHUNK, 1, 1), F32)] * 2,
        [q, q, k, k, v, v, *beta, *gc, *gl, *ssave, *tsave, do, do], ride, CHUNK)


def _ret_chains(q, k, v, dm, qs, ks, cd, s):
    n = range(len(q))
    a = [_bdot(q[i], k[i], 1, 1) * dm[i] for i in n]
    o = [_bdot(a[i], v[i], 1, 0) + _bdot(q[i] * qs[i], s[i], 1, 0) for i in n]
    s_new = [s[i] * cd[i] + _bdot(k[i] * ks[i], v[i], 0, 0) for i in n]
    return o, s_new


RET_CONST_SHAPES = ((N_CHAIN, RET_CHUNK, RET_CHUNK), (N_CHAIN, RET_CHUNK, 1), (N_CHAIN, RET_CHUNK, 1), (N_CHAIN, 1, 1))


def _ret_const_specs():
    return [pl.BlockSpec(shape, lambda n: (0, 0, 0)) for shape in RET_CONST_SHAPES]


def _ret_fwd_call(q, k, v, v_col, dm, qs, ks, cd, ride=None):
    sp = _scan_specs(lambda n: n, RET_CHUNK, v_col)

    def body(qf, qr, kf, kr, vf, vr, dm_ref, qs_ref, ks_ref, cd_ref, of, orv, ssf, ssr, s_scr):
        @pl.when(pl.program_id(0) == 0)
        def _():
            s_scr[...] = jnp.zeros_like(s_scr)

        (qc, kc, vc), _ = _chain_loads([(qf, qr), (kf, kr), (vf, vr)], [])
        ss = [s_scr[i] for i in range(N_CHAIN)]
        consts = [[r[i] for i in range(N_CHAIN)] for r in (dm_ref, qs_ref, ks_ref, cd_ref)]
        o, s_new = _ret_chains(qc, kc, vc, *consts, ss)
        for i in range(N_CHAIN):
            d, h = i // 4, i % 4
            (ssf, ssr)[d][h] = ss[i]
            (of, orv)[d][:, h * 128:(h + 1) * 128] = o[i]
            s_scr[i] = s_new[i]

    return _scan_call(
        body, "ret_fwd",
        _both(sp, ["tok", "tok", "tokv"]) + _ret_const_specs(),
        _both(sp, ["tok", "state"]),
        [jax.ShapeDtypeStruct((ROWS, 512), F32)] * 2
        + [jax.ShapeDtypeStruct((ROWS // RET_CHUNK, 4, 128, 128), F32)] * 2,
        [q, q, k, k, v, v, dm, qs, ks, cd], ride, RET_CHUNK)


def _ret_bwd_call(q, k, v, v_col, dm, qs, ks, cd, ssave, do, ride=None):
    sp = _scan_specs(lambda n: ROWS // RET_CHUNK - 1 - n, RET_CHUNK, v_col)

    def body(qf, qr, kf, kr, vf, vr, dm_ref, qs_ref, ks_ref, cd_ref, ssf, ssr, dof, dor,
             dqf, dqr, dkf, dkr, dvf, dvr, ddm_ref, dqs_ref, dks_ref, dcd_ref, ds_scr):
        @pl.when(pl.program_id(0) == 0)
        def _():
            ds_scr[...] = jnp.zeros_like(ds_scr)
            ddm_ref[...] = jnp.zeros_like(ddm_ref)
            dqs_ref[...] = jnp.zeros_like(dqs_ref)
            dks_ref[...] = jnp.zeros_like(dks_ref)
            dcd_ref[...] = jnp.zeros_like(dcd_ref)

        (qc, kc, vc, dos), (ss,) = _chain_loads([(qf, qr), (kf, kr), (vf, vr), (dof, dor)], [(ssf, ssr)])
        consts = [[r[i] for i in range(N_CHAIN)] for r in (dm_ref, qs_ref, ks_ref, cd_ref)]
        _, vjp = jax.vjp(_ret_chains, qc, kc, vc, *consts, ss)
        dq, dk, dv, ddm, dqs, dks, dcd, ds = vjp((dos, [ds_scr[i] for i in range(N_CHAIN)]))
        for i in range(N_CHAIN):
            d, h = i // 4, i % 4
            hs = slice(h * 128, (h + 1) * 128)
            (dqf, dqr)[d][:, hs] = dq[i]
            (dkf, dkr)[d][:, hs] = dk[i]
            (dvf, dvr)[d][:, hs] = dv[i]
            ddm_ref[i] += ddm[i]
            dqs_ref[i] += dqs[i]
            dks_ref[i] += dks[i]
            dcd_ref[i] += dcd[i]
            ds_scr[i] = ds[i]

    tok = jax.ShapeDtypeStruct((ROWS, 512), F32)
    return _scan_call(
        body, "ret_bwd",
        _both(sp, ["tok", "tok", "tokv"]) + _ret_const_specs() + _both(sp, ["state", "tok"]),
        _both(sp, ["tok", "tok", "tok"]) + _ret_const_specs(),
        [tok] * 6 + [jax.ShapeDtypeStruct(shape, F32) for shape in RET_CONST_SHAPES],
        [q, q, k, k, v, v, dm, qs, ks, cd, *ssave, do, do], ride, RET_CHUNK)


N_QBLK = ROWS // B_BLOCK
CTX_QBLK = CTX_LEN // B_BLOCK


def _attn_heads(q, kc, vc, kw, vw, sink, valid):
    n = range(len(q))
    qs = [q[i] * (B_HD ** -0.5) for i in n]
    s_c = [_bdot(qs[i], kc[i], 1, 1) for i in n]
    s_w = [jnp.where(valid, _bdot(qs[i], kw[i], 1, 1), NEG) for i in n]
    m = [lax.stop_gradient(jnp.maximum(jnp.maximum(jnp.max(s_c[i], axis=-1, keepdims=True), sink[i]),
                                       jnp.max(s_w[i], axis=-1, keepdims=True))) for i in n]
    e_c = [jnp.exp(s_c[i] - m[i]) for i in n]
    e_w = [jnp.exp(s_w[i] - m[i]) for i in n]
    den = [jnp.sum(e_c[i], axis=-1, keepdims=True) + jnp.sum(e_w[i], axis=-1, keepdims=True)
           + jnp.exp(sink[i] - m[i]) for i in n]
    return [(_bdot(e_c[i], vc[i], 1, 0) + _bdot(e_w[i], vw[i], 1, 0)) / den[i] for i in n]


def _attn_loads(q_ref, kv_ref, sink_ref, start):
    q, kc, vc, kw, vw, sink = [], [], [], [], [], []
    for hk in range(B_KV_HEADS):
        ks = slice(hk * B_HD, (hk + 1) * B_HD)
        vs = slice(128 + hk * B_HD, 128 + (hk + 1) * B_HD)
        grp = (kv_ref[0:CTX_LEN, ks], kv_ref[0:CTX_LEN, vs],
               kv_ref[pl.ds(start, 3 * B_BLOCK), ks], kv_ref[pl.ds(start, 3 * B_BLOCK), vs])
        for g in range(4):
            h = hk * 4 + g
            q.append(q_ref[:, h * B_HD:(h + 1) * B_HD])
            for lst, val in zip((kc, vc, kw, vw), grp):
                lst.append(val)
            sink.append(jnp.full((1, 1), sink_ref[h], F32))
    return q, kc, vc, kw, vw, sink


def _window(blk):
    xblk = blk - CTX_QBLK
    first = jnp.clip((xblk - 1) * B_BLOCK, 0, SEQ - 3 * B_BLOCK)
    qpos = xblk * B_BLOCK + lax.broadcasted_iota(jnp.int32, (B_BLOCK, 3 * B_BLOCK), 0)
    kpos = first + lax.broadcasted_iota(jnp.int32, (B_BLOCK, 3 * B_BLOCK), 1)
    far = jnp.where(blk >= CTX_QBLK, 0, 2 * SEQ)
    valid = jnp.abs(kpos - qpos) + far <= WINDOW
    return pl.multiple_of(first + CTX_LEN, B_BLOCK), valid


def _attn_specs():
    qspec = pl.BlockSpec((B_BLOCK, 512), lambda i: (i, 0))
    kvspec = pl.BlockSpec((ROWS, 256), lambda i: (0, 0))
    return qspec, kvspec, pl.BlockSpec(memory_space=pltpu.SMEM)


def _attn_fwd_call(q, kv, sink, ride=None):
    def body(q_ref, kv_ref, sink_ref, o_ref):
        start, valid = _window(pl.program_id(0))
        out = _attn_heads(*_attn_loads(q_ref, kv_ref, sink_ref, start), valid)
        for h in range(B_Q_HEADS):
            o_ref[:, h * B_HD:(h + 1) * B_HD] = out[h]

    qspec, kvspec, sspec = _attn_specs()
    body, r_in, r_out, r_shape, r_scratch = _riding(body, 3, 1, 0, ride, (N_QBLK,))
    return pl.pallas_call(
        body,
        grid=(N_QBLK,),
        in_specs=[qspec, kvspec, sspec] + r_in,
        out_specs=[qspec] + r_out,
        out_shape=[jax.ShapeDtypeStruct((ROWS, 512), F32)] + r_shape,
        scratch_shapes=r_scratch,
        compiler_params=_cparams(("arbitrary",)),
        name="attn_fwd",
    )(q, kv, sink, *(ride[0] if ride else []))


def _attn_bwd_call(q, kv, sink, do, ride=None):
    def body(q_ref, kv_ref, sink_ref, do_ref, dq_ref, dkv_ref, dsink_ref):
        @pl.when(pl.program_id(0) == 0)
        def _():
            dkv_ref[...] = jnp.zeros_like(dkv_ref)
            dsink_ref[...] = jnp.zeros_like(dsink_ref)

        start, valid = _window(pl.program_id(0))
        _, vjp = jax.vjp(functools.partial(_attn_heads, valid=valid), *_attn_loads(q_ref, kv_ref, sink_ref, start))
        dq, dkc, dvc, dkw, dvw, dsink = vjp([do_ref[:, h * B_HD:(h + 1) * B_HD] for h in range(B_Q_HEADS)])
        for h in range(B_Q_HEADS):
            dq_ref[:, h * B_HD:(h + 1) * B_HD] = dq[h]
            dsink_ref[h:h + 1, :] += jnp.broadcast_to(dsink[h], (1, 128))
        for hk in range(B_KV_HEADS):
            ks = slice(hk * B_HD, (hk + 1) * B_HD)
            vs = slice(128 + hk * B_HD, 128 + (hk + 1) * B_HD)
            grp = lambda parts: parts[hk * 4] + parts[hk * 4 + 1] + parts[hk * 4 + 2] + parts[hk * 4 + 3]
            dkv_ref[0:CTX_LEN, ks] += grp(dkc)
            dkv_ref[0:CTX_LEN, vs] += grp(dvc)
            dkv_ref[pl.ds(start, 3 * B_BLOCK), ks] += grp(dkw)
            dkv_ref[pl.ds(start, 3 * B_BLOCK), vs] += grp(dvw)

    qspec, kvspec, sspec = _attn_specs()
    body, r_in, r_out, r_shape, r_scratch = _riding(body, 4, 3, 0, ride, (N_QBLK,))
    return pl.pallas_call(
        body,
        grid=(N_QBLK,),
        in_specs=[qspec, kvspec, sspec, qspec] + r_in,
        out_specs=[qspec, kvspec, pl.BlockSpec((8, 128), lambda i: (0, 0))] + r_out,
        out_shape=[jax.ShapeDtypeStruct((ROWS, 512), F32), jax.ShapeDtypeStruct((ROWS, 256), F32),
                   jax.ShapeDtypeStruct((8, 128), F32)] + r_shape,
        scratch_shapes=r_scratch,
        compiler_params=_cparams(("arbitrary",)),
        name="attn_bwd",
    )(q, kv, sink, do, *(ride[0] if ride else []))


def _my_id():
    return 4 * lax.axis_index("x") + 2 * lax.axis_index("y") + lax.axis_index("c")


def _peer(k):
    x, y, c = lax.axis_index("x"), lax.axis_index("y"), lax.axis_index("c")
    return (1 - x if k & 4 else x, 1 - y if k & 2 else y, 1 - c if k & 1 else c)


SAME_CORE_PEERS = (2, 4, 6)


def _scatter_copies(ins, outs, sems):
    send_sems, recv_sems, local_sems = sems
    me = _my_id()
    own, remote = [], []
    for a in range(len(ins)):
        own.append(pltpu.make_async_copy(ins[a].at[me], outs[a].at[me], local_sems.at[a]))
        for k in range(1, N_DEV):
            peer_slot = jnp.bitwise_xor(me, k)
            common = dict(src_ref=ins[a].at[peer_slot], send_sem=send_sems.at[a, k - 1],
                          recv_sem=recv_sems.at[a, k - 1], device_id=_peer(k), device_id_type=MESH)
            remote.append((pltpu.make_async_remote_copy(dst_ref=outs[a].at[me], **common),
                           pltpu.make_async_remote_copy(dst_ref=outs[a].at[peer_slot], **common)))
    return own, remote


def _gather_copy(outs, sems, a, k, src, slot, to):
    return pltpu.make_async_remote_copy(src_ref=src, dst_ref=outs[a].at[slot], send_sem=sems[0].at[a, k - 1],
                                        recv_sem=sems[1].at[a, k - 1], device_id=_peer(to), device_id_type=MESH)


def _gather_first_copies(ins, outs, sems):
    me = _my_id()
    own = [pltpu.make_async_copy(ins[a], outs[a].at[me], sems[2].at[a]) for a in range(len(ins))]
    direct = [_gather_copy(outs, sems, a, k, ins[a], me, k) for a in range(len(ins)) for k in (1,) + SAME_CORE_PEERS]
    return own, direct


def _exchange_start(ins, outs, sems, gather):
    own, remote = _gather_first_copies(ins, outs, sems) if gather else _scatter_copies(ins, outs, sems)
    for cp in own:
        cp.start()
    for cp in remote:
        (cp if gather else cp[0]).start()


def _exchange_wait(ins, outs, sems, gather):
    if not gather:
        own, remote = _scatter_copies(ins, outs, sems)
        for _, arrival in remote:
            arrival.wait_recv()
        for send, _ in remote:
            send.wait_send()
        for cp in own:
            cp.wait()
        return
    me = _my_id()
    own, direct = _gather_first_copies(ins, outs, sems)
    passed = []
    for a in range(len(ins)):
        for k in SAME_CORE_PEERS:
            origin = jnp.bitwise_xor(me, k)
            _gather_copy(outs, sems, a, k, ins[a], origin, k).wait_recv()
            onward = _gather_copy(outs, sems, a, k + 1, outs[a].at[origin], origin, 1)
            onward.start()
            passed.append(onward)
    for a in range(len(ins)):
        for k in (1, 3, 5, 7):
            _gather_copy(outs, sems, a, k, ins[a], jnp.bitwise_xor(me, k), 1).wait_recv()
    for cp in direct + passed:
        cp.wait_send()
    for cp in own:
        cp.wait()


def _exchange_plumbing(arrays, gather):
    n = len(arrays)
    hbm = [pl.BlockSpec(memory_space=pltpu.HBM)] * n
    out_shape = [jax.ShapeDtypeStruct((N_DEV,) + (a.shape if gather else a.shape[1:]), a.dtype) for a in arrays]
    sems = [pltpu.SemaphoreType.DMA((n, N_DEV - 1)), pltpu.SemaphoreType.DMA((n, N_DEV - 1)),
            pltpu.SemaphoreType.DMA((n,))]
    return hbm, out_shape, sems


def _exchange(arrays, gather, name):
    n = len(arrays)

    def body(*refs):
        ins, outs, sems = refs[:n], refs[n:2 * n], refs[2 * n:]
        _exchange_start(ins, outs, sems, gather)
        _exchange_wait(ins, outs, sems, gather)

    hbm, out_shape, sems = _exchange_plumbing(arrays, gather)
    return pl.pallas_call(
        body,
        in_specs=hbm,
        out_specs=hbm,
        out_shape=out_shape,
        scratch_shapes=sems,
        compiler_params=pltpu.CompilerParams(has_side_effects=True),
        name=name,
    )(*arrays)


N_CHIP = N_DEV // 2


def _pair_swap(blocks, name):
    n = len(blocks)

    def body(*refs):
        ins, outs, (send_sems, recv_sems) = refs[:n], refs[n:2 * n], refs[2 * n:]
        core = lax.axis_index("c")
        copies = [pltpu.make_async_remote_copy(src_ref=ins[a].at[2 * chip + (1 - core)], dst_ref=outs[a].at[chip],
                                               send_sem=send_sems.at[a, chip], recv_sem=recv_sems.at[a, chip],
                                               device_id=_peer(1), device_id_type=MESH)
                  for a in range(n) for chip in range(N_CHIP)]
        for cp in copies:
            cp.start()
        for cp in copies:
            cp.wait_recv()
        for cp in copies:
            cp.wait_send()

    hbm = [pl.BlockSpec(memory_space=pltpu.HBM)] * n
    return pl.pallas_call(
        body,
        in_specs=hbm,
        out_specs=hbm,
        out_shape=[jax.ShapeDtypeStruct((N_CHIP,) + b.shape[1:], b.dtype) for b in blocks],
        scratch_shapes=[pltpu.SemaphoreType.DMA((n, N_CHIP)), pltpu.SemaphoreType.DMA((n, N_CHIP))],
        compiler_params=pltpu.CompilerParams(has_side_effects=True),
        name=name,
    )(*blocks)


def _chip_scatter(pairs, name):
    n = len(pairs)

    def body(*refs):
        ins, outs, (send_sems, recv_sems, local_sems) = refs[:n], refs[n:2 * n], refs[2 * n:]
        chip = 2 * lax.axis_index("x") + lax.axis_index("y")
        own = [pltpu.make_async_copy(ins[a].at[chip], outs[a].at[chip], local_sems.at[a]) for a in range(n)]
        sends, arrivals = [], []
        for a in range(n):
            for k in range(1, N_CHIP):
                other = jnp.bitwise_xor(chip, k)
                common = dict(src_ref=ins[a].at[other], send_sem=send_sems.at[a, k - 1], recv_sem=recv_sems.at[a, k - 1],
                              device_id=_peer(2 * k), device_id_type=MESH)
                sends.append(pltpu.make_async_remote_copy(dst_ref=outs[a].at[chip], **common))
                arrivals.append(pltpu.make_async_remote_copy(dst_ref=outs[a].at[other], **common))
        for cp in own + sends:
            cp.start()
        for cp in arrivals:
            cp.wait_recv()
        for cp in sends:
            cp.wait_send()
        for cp in own:
            cp.wait()

    hbm = [pl.BlockSpec(memory_space=pltpu.HBM)] * n
    return pl.pallas_call(
        body,
        in_specs=hbm,
        out_specs=hbm,
        out_shape=[jax.ShapeDtypeStruct(p.shape, p.dtype) for p in pairs],
        scratch_shapes=[pltpu.SemaphoreType.DMA((n, N_CHIP - 1)), pltpu.SemaphoreType.DMA((n, N_CHIP - 1)),
                        pltpu.SemaphoreType.DMA((n,))],
        compiler_params=pltpu.CompilerParams(has_side_effects=True),
        name=name,
    )(*pairs)


def _scatter_two_level(blocks, name):
    swapped = _pair_swap(blocks, name + "_pair")
    core = lax.axis_index("c")
    pairs = []
    for b, s in zip(blocks, swapped):
        mine = lax.dynamic_index_in_dim(b.reshape((N_CHIP, 2) + b.shape[1:]), core, axis=1, keepdims=False)
        pairs.append((mine.astype(F32) + s.astype(F32)).astype(b.dtype))
    return _chip_scatter(pairs, name + "_chip")


def _riding(body, n_in, n_out, n_scratch, ride, grid):
    if ride is None:
        return body, [], [], [], []
    arrays, gather = ride
    n = len(arrays)

    def at(step_of):
        hit = pl.program_id(0) == step_of(grid[0])
        for d in range(1, len(grid)):
            hit = jnp.logical_and(hit, pl.program_id(d) == step_of(grid[d]))
        return hit

    def wrapped(*refs):
        ins, rin = refs[:n_in], refs[n_in:n_in + n]
        outs = refs[n_in + n:n_in + n + n_out]
        rout = refs[n_in + n + n_out:n_in + 2 * n + n_out]
        scratch = refs[n_in + 2 * n + n_out:n_in + 2 * n + n_out + n_scratch]
        sems = refs[n_in + 2 * n + n_out + n_scratch:]

        @pl.when(at(lambda size: 0))
        def _():
            _exchange_start(rin, rout, sems, gather)

        body(*ins, *outs, *scratch)

        @pl.when(at(lambda size: size - 1))
        def _():
            _exchange_wait(rin, rout, sems, gather)

    hbm, out_shape, sems = _exchange_plumbing(arrays, gather)
    return wrapped, hbm, hbm, out_shape, sems


def _sum_contributions(c_ref):
    g = c_ref[0].astype(F32)
    for j in range(1, c_ref.shape[0]):
        g = g + c_ref[j].astype(F32)
    return g


def _adamw_update(g, w_ref, m_ref, v_ref, g_ref, d_ref, nm_ref, nv_ref):
    m_new = ADAM_B1 * m_ref[...] + (1.0 - ADAM_B1) * g
    v_new = ADAM_B2 * v_ref[...] + (1.0 - ADAM_B2) * (g * g)
    m_hat = m_new / (1.0 - ADAM_B1 ** ADAM_STEP)
    v_hat = v_new / (1.0 - ADAM_B2 ** ADAM_STEP)
    g_ref[...] = g
    d_ref[...] = -ADAM_LR * (m_hat / (jnp.sqrt(v_hat) + ADAM_EPS) + ADAM_WD * w_ref[...])
    nm_ref[...] = m_new
    nv_ref[...] = v_new


def _adamw_layers(w, m, v, contrib0, contrib1, name):
    _, r, c = w.shape
    br = _pick(r, (256, 128, 64, 32, 16, 8))
    nb = r // br

    def body(w_ref, m_ref, v_ref, c0_ref, c1_ref, g_ref, d_ref, nm_ref, nv_ref):
        g = jnp.where(pl.program_id(0) == 0, _sum_contributions(c0_ref), _sum_contributions(c1_ref))
        _adamw_update(g, w_ref, m_ref, v_ref, g_ref, d_ref, nm_ref, nv_ref)

    spec = pl.BlockSpec((None, br, c), lambda l, i: (l, i, 0))
    return pl.pallas_call(
        body,
        grid=(DEPTH, nb),
        in_specs=[spec, spec, spec,
                  pl.BlockSpec((contrib0.shape[0], br, c), lambda l, i: (0, jnp.where(l == 0, i, nb - 1), 0)),
                  pl.BlockSpec((contrib1.shape[0], br, c), lambda l, i: (0, jnp.where(l == 1, i, 0), 0))],
        out_specs=[spec] * 4,
        out_shape=[jax.ShapeDtypeStruct(w.shape, F32)] * 4,
        compiler_params=_cparams(("arbitrary", "arbitrary")),
        name=name,
    )(w, m, v, contrib0, contrib1)


def _adamw(w, m, v, contrib, name):
    r, c = w.shape
    br = _pick(r, (256, 128, 64, 32, 16, 8))

    def body(w_ref, m_ref, v_ref, c_ref, g_ref, d_ref, nm_ref, nv_ref):
        _adamw_update(_sum_contributions(c_ref), w_ref, m_ref, v_ref, g_ref, d_ref, nm_ref, nv_ref)

    spec = pl.BlockSpec((br, c), lambda i: (i, 0))
    cspec = pl.BlockSpec((contrib.shape[0], br, c), lambda i: (0, i, 0))
    return pl.pallas_call(
        body,
        grid=(r // br,),
        in_specs=[spec, spec, spec, cspec],
        out_specs=[spec] * 4,
        out_shape=[jax.ShapeDtypeStruct((r, c), F32)] * 4,
        compiler_params=_cparams(("parallel",)),
        name=name,
    )(w, m, v, contrib)


def _silu(x):
    return x * jax.nn.sigmoid(x)


def _rope_angles(pos, n_freq):
    inv = ROPE_BASE ** (-jnp.arange(n_freq, dtype=F32) / n_freq)
    return pos[:, None] * inv[None, :]


def _with_ctx_rows(cos, sin):
    return (jnp.concatenate([jnp.ones((CTX_LEN, 128), F32), cos], axis=0),
            jnp.concatenate([jnp.zeros((CTX_LEN, 128), F32), sin], axis=0))


def _rope_tables():
    rows_n = SEQ // GRID_W
    rows = jnp.repeat(jnp.arange(rows_n, dtype=F32), GRID_W)
    cols = jnp.tile(jnp.arange(GRID_W, dtype=F32), rows_n)
    ang_r = _rope_angles(rows, B_HD // 4)
    ang_c = _rope_angles(cols, B_HD // 4)
    cos_b = jnp.tile(jnp.concatenate([jnp.cos(ang_r)] * 2 + [jnp.cos(ang_c)] * 2, axis=1), (1, 2))
    sin_b = jnp.tile(jnp.concatenate([-jnp.sin(ang_r), jnp.sin(ang_r), -jnp.sin(ang_c), jnp.sin(ang_c)], axis=1), (1, 2))
    ang = _rope_angles(jnp.arange(SEQ, dtype=F32), C_HD // 2)
    cos_c = jnp.concatenate([jnp.cos(ang)] * 2, axis=1)
    sin_c = jnp.concatenate([-jnp.sin(ang), jnp.sin(ang)], axis=1)
    return _with_ctx_rows(cos_b, sin_b), _with_ctx_rows(cos_c, sin_c)


def _halves(a):
    return a[:4], a[4:]


def _delta_gates(ab, a_log, dt_bias):
    beta = jax.nn.sigmoid(ab[:, :8])
    g = -jnp.exp(a_log)[None, :] * jax.nn.softplus(ab[:, 8:] + dt_bias[None, :])
    gch = g.reshape(N_CHUNK, CHUNK, 8)
    tri = jnp.tril(jnp.ones((CHUNK, CHUNK), F32))
    fwd = jnp.einsum("ij,cjh->cih", tri, gch[..., :4], precision=HIGHEST)
    bwd = jnp.einsum("ji,cjh->cih", tri, gch[..., 4:], precision=HIGHEST)
    gc = jnp.concatenate([fwd, bwd], axis=-1)
    gl = jnp.sum(gch, axis=1)
    rows = lambda a: _halves(a.transpose(2, 0, 1)[:, :, None, :])
    return rows(beta.reshape(N_CHUNK, CHUNK, 8)), rows(gc), _halves(gl.T[:, :, None, None])


def _ret_consts(c_decay):
    lg = jax.nn.log_sigmoid(c_decay)
    idx = jnp.arange(RET_CHUNK, dtype=F32)
    diff = idx[:, None] - idx[None, :]
    lgf, lgb = lg[:4, None, None], lg[4:, None, None]
    dm = jnp.concatenate([jnp.exp(jnp.where(diff >= 0, diff * lgf, -jnp.inf)),
                          jnp.exp(jnp.where(diff <= 0, -diff * lgb, -jnp.inf))], axis=0)
    qs = jnp.concatenate([jnp.exp((idx + 1.0)[None, :] * lg[:4, None]),
                          jnp.exp((RET_CHUNK - idx)[None, :] * lg[4:, None])], axis=0)[:, :, None]
    ks = jnp.concatenate([jnp.exp((RET_CHUNK - 1.0 - idx)[None, :] * lg[:4, None]),
                          jnp.exp(idx[None, :] * lg[4:, None])], axis=0)[:, :, None]
    return dm, qs, ks, jnp.exp(RET_CHUNK * lg)[:, None, None]


A_PIECES = ((0, True, A_DK ** -0.5, "a_q"), (1, True, 1.0, "a_k"), (2, False, 1.0, "a_v"))
B_ROPE_COLS = [C_BQ // 512, C_BKV // 256, 0, 0]
C_ROPE_COLS = [C_CQ // 512, C_CK // 512, 0, 0]
MERGE_COLS = [0, 0, 0, C_MERGE // 1024, C_MERGE // 1024 + 1, C_MERGE // 1024 + 2]


def _conv8(conv_w):
    return jnp.pad(conv_w, ((0, 8 - A_CONV), (0, 0)))


W_IN_TILES = {"nn": (2176, 512, 1024), "nt": (1088, 1024, 2176), "db": (1024, 512, ROWS)}


def _core_forward(h, w16, p, rides):
    res = _matmul(h, w16, "w_in", "nn", W_IN_TILES["nn"], ride=rides.get("w_in"))
    proj, rode = (res[0], {"w_in": res[1:]}) if "w_in" in rides else (res, {})
    wb = p["w_branch"] if "w_in" not in rides else _unshard_layer("w_branch", rode["w_in"][0])
    (cos_b, sin_b), (cos_c, sin_c) = _rope_tables()
    conv8 = _conv8(p["a_conv_w"])
    q, k, v = [_a_prep_fwd(proj, conv8, col, nrm, scl, nm) for col, nrm, scl, nm in A_PIECES]
    gates = _delta_gates(proj[:, C_AB:C_AB + 16], p["a_log"], p["a_dt_bias"])
    res = _delta_fwd_call(q, k, v, *gates, ride=rides.get("delta"))
    (of, orv, ssf, ssr, tsf, tsr), rode["delta"] = res[:6], res[6:]
    (y_a,) = _a_out.fwd([(of, orv), proj], [p["a_norm_w"][None, :]], [0, C_AZ // 512])

    qb, kvb = _b_rope.fwd([proj, proj, cos_b, sin_b], [], B_ROPE_COLS)
    res = _attn_fwd_call(qb, kvb, p["b_sink"], ride=rides.get("attn"))
    ob, rode["attn"] = res[0], res[1:]
    (y_b,) = _b_out.fwd([ob, proj], [], [0, C_BZ // 512])

    qc, kc = _c_rope.fwd([proj, proj, cos_c, sin_c], [], C_ROPE_COLS)
    res = _ret_fwd_call(qc, kc, proj, C_CV // 512, *_ret_consts(p["c_decay"]), ride=rides.get("ret"))
    (cf, cr, csf, csr), rode["ret"] = res[:4], res[4:]
    (y_c,) = _c_out.fwd([(cf, cr), proj], [p["c_norm_w"][None, :]], [0, C_CZ // 512])

    (merged,) = _branch_merge.fwd([y_a, y_b, y_c, proj, proj, proj], [wb[0], wb[1], wb[2]], MERGE_COLS)
    saved = dict(proj=proj, q=q, k=k, v=v, of=of, orv=orv, ss=(ssf, ssr), ts=(tsf, tsr), qb=qb, kvb=kvb, ob=ob,
                 qc=qc, kc=kc, cf=cf, cr=cr, cs=(csf, csr), y=(y_a, y_b, y_c), wb=wb)
    return merged, saved, rode


def _core_backward(h, w16, p, s, dmerged, rides, branch_rides_in_attn=False):
    proj, wb = s["proj"], s["wb"]
    (cos_b, sin_b), (cos_c, sin_c) = _rope_tables()
    conv8 = _conv8(p["a_conv_w"])
    y_a, y_b, y_c = s["y"]
    rode = {}

    (*dy, dma, dmb, dmc), dwb = _branch_merge.bwd([y_a, y_b, y_c, proj, proj, proj], [wb[0], wb[1], wb[2]],
                                                   [dmerged], MERGE_COLS, bf16_rows=(3, 4, 5))
    dwb = jnp.stack(dwb)

    consts, consts_vjp = jax.vjp(_ret_consts, p["c_decay"])
    (do_c, dcz), (dcnw,) = _c_out.bwd([(s["cf"], s["cr"]), proj], [p["c_norm_w"][None, :]], [dy[2]],
                                      [0, C_CZ // 512], bf16_rows=(1,))
    g = _ret_bwd_call(s["qc"], s["kc"], proj, C_CV // 512, *consts, s["cs"], do_c, ride=rides.get("ret"))
    rode["ret"] = g[10:]
    (dcq, dck), _ = _c_rope.bwd([proj, proj, cos_c, sin_c], [], [(g[0], g[1]), (g[2], g[3])], C_ROPE_COLS,
                                bf16_rows=(0, 1))
    dcv = (g[4] + g[5]).astype(BF16)
    (dc_decay,) = consts_vjp(tuple(g[6:10]))

    (dob, dbz), _ = _b_out.bwd([s["ob"], proj], [], [dy[1]], [0, C_BZ // 512], bf16_rows=(1,))
    attn_ride = rides.get("attn")
    if branch_rides_in_attn:
        attn_ride = (list(attn_ride[0]) + [_reshard_layer("w_branch", dwb).astype(BF16)], attn_ride[1])
    res = _attn_bwd_call(s["qb"], s["kvb"], p["b_sink"], dob, ride=attn_ride)
    (dqb, dkvb, dsink), rode["attn"] = res[:3], res[3:]
    (dbq, dbkv), _ = _b_rope.bwd([proj, proj, cos_b, sin_b], [], [dqb, dkvb], B_ROPE_COLS, bf16_rows=(0, 1))

    ab = proj[:, C_AB:C_AB + 16]
    gates, gates_vjp = jax.vjp(_delta_gates, ab, p["a_log"], p["a_dt_bias"])
    (do_a, daz), (danw,) = _a_out.bwd([(s["of"], s["orv"]), proj], [p["a_norm_w"][None, :]], [dy[0]],
                                      [0, C_AZ // 512], bf16_rows=(1,))
    g = _delta_bwd_call(s["q"], s["k"], s["v"], *gates, s["ss"], s["ts"], do_a, ride=rides.get("delta"))
    rode["delta"] = g[12:]
    dgates = ((g[6], g[7]), (g[8], g[9]), (g[10], g[11]))
    dab, da_log, ddt = gates_vjp(dgates)
    dpre, dconv = [], []
    for (col, nrm, scl, nm), df, dr in zip(A_PIECES, (g[0], g[2], g[4]), (g[1], g[3], g[5])):
        dx, dw = _a_prep_bwd(proj, conv8, col, nrm, scl, df, dr, nm)
        dpre.append(dx)
        dconv.append(dw[:A_CONV])

    dproj = jnp.concatenate(dpre + [daz, dbq, dbz, dcq, dck, dcv, dcz, dma, dmb, dmc, dbkv,
                                    jnp.pad(dab, ((0, 0), (0, IN_PAD - C_AB - 16))).astype(BF16)], axis=1)
    dh = _matmul(dproj, w16, "w_in_da", "nt", W_IN_TILES["nt"])
    dw = _matmul(h.T.astype(BF16), dproj, "w_in_db", "nn", W_IN_TILES["db"])
    dp = dict(a_conv_w=jnp.concatenate(dconv, axis=1), a_log=da_log, a_dt_bias=ddt, a_norm_w=danw[0],
              b_sink=dsink[:, 0], c_decay=dc_decay, c_norm_w=dcnw[0], w_branch=dwb)
    return dh, dw, dp, rode


CORE_PARAMS = ("a_conv_w", "a_log", "a_dt_bias", "a_norm_w", "b_sink", "c_decay", "c_norm_w", "w_branch")


W_IN_SHARD = IN_WIDTH // N_DEV
W_IN_RUNS = ((0, 2048, 0), (2064, 512, C_BQ), (2832, 512, C_BZ), (3344, 5120, C_CQ), (2576, 256, C_BKV),
             (2048, 16, C_AB))


def _shard_overlap(start, width, j):
    lo, hi = max(start, j * W_IN_SHARD), min(start + width, (j + 1) * W_IN_SHARD)
    return (lo, hi) if lo < hi else None


def _w_in_from_shards(g):
    parts = []
    for start, width, _ in W_IN_RUNS:
        for j in range(N_DEV):
            span = _shard_overlap(start, width, j)
            if span:
                parts.append(g[j, :, span[0] - j * W_IN_SHARD:span[1] - j * W_IN_SHARD])
    parts.append(jnp.zeros((D_MODEL, IN_PAD - IN_WIDTH), g.dtype))
    return jnp.concatenate(parts, axis=1)


def _w_in_blocks(dw):
    blocks = []
    for j in range(N_DEV):
        parts = []
        for start, width, pad in sorted(W_IN_RUNS):
            span = _shard_overlap(start, width, j)
            if span:
                parts.append(dw[:, pad + span[0] - start:pad + span[1] - start])
        blocks.append(jnp.concatenate(parts, axis=1))
    return jnp.stack(blocks)


LAYER_SHARDED = ("w_ada", "w_in", "w_branch", "w_out")


def _unshard_layer(name, g):
    if name == "w_branch":
        return g.transpose(1, 2, 0, 3).reshape(3, BR_WIDTH, D_MODEL)
    if name == "w_out":
        return g.reshape(D_MODEL, D_MODEL)
    return g.transpose(1, 0, 2).reshape(D_MODEL, -1)


def _reshard_layer(name, w):
    if name == "w_branch":
        return w.reshape(3, BR_WIDTH, N_DEV, D_MODEL // N_DEV).transpose(2, 0, 1, 3)
    if name == "w_out":
        return w.reshape(N_DEV, D_MODEL // N_DEV, D_MODEL)
    return w.reshape(D_MODEL, N_DEV, -1).transpose(1, 0, 2)


def _layer_weights(gathered):
    out = {n: _unshard_layer(n, g) for n, g in gathered.items() if n != "w_in"}
    out["w_in16"] = _w_in_from_shards(gathered["w_in"])
    return out


def _grad_blocks(name, g):
    return (_w_in_blocks(g) if name == "w_in" else _reshard_layer(name, g)).astype(BF16)


def _forward_backward(small, layer0, shards0, shards1, x, c, ctx, loss_target):
    c_ctx = small["c_ctx"]
    sc16 = jnp.zeros((16, D_MODEL), F32).at[0].set(_silu(c)).at[1].set(_silu(c_ctx))
    xs = jnp.concatenate([ctx, x], axis=0)
    weights = [dict(layer0), None]
    layers = []
    for l in range(DEPTH):
        wl = weights[l]
        mod16 = _matmul(sc16, wl["w_ada"], "ada") + small["b_ada"][l][None, :]
        mod_cx = jnp.stack([mod16[1], mod16[0]])
        shift, scale, gate = jnp.split(mod_cx, 3, axis=1)
        nw = small["norm_w"][l][None, :]
        (h,) = _norm_mod.fwd([xs], [nw, shift, scale])
        p = {n: small[n][l] for n in CORE_PARAMS if n != "w_branch"}
        p["w_branch"] = wl.get("w_branch")
        rides = {}
        if l == 0:
            rides = {"w_in": ([shards0["w_branch"], shards0["w_out"]], True), "delta": ([shards1["w_in"]], True),
                     "attn": ([shards1["w_ada"]], True), "ret": ([shards1["w_branch"], shards1["w_out"]], True)}
        merged, saved, rode = _core_forward(h, wl["w_in16"], p, rides)
        if l == 0:
            wl["w_out"] = _unshard_layer("w_out", rode["w_in"][1])
            weights[1] = _layer_weights(dict(w_in=rode["delta"][0], w_ada=rode["attn"][0],
                                             w_branch=rode["ret"][0], w_out=rode["ret"][1]))
        (xs_next,) = _out_residual.fwd([xs, merged], [wl["w_out"], gate])
        layers.append(dict(xs=xs, h=h, p=p, saved=saved, merged=merged, gate=gate, nw=nw, shift=shift, scale=scale))
        xs = xs_next
    fw = small["final_norm_w"][None, :]
    xs = xs[CTX_LEN:]
    (per_row,) = _loss_rows.fwd([xs, loss_target], [fw])
    loss = jnp.sum(per_row[:, 0])

    d_per_row = jnp.zeros((SEQ, 128), F32).at[:, 0].set(1.0)
    (dxs,), (dfw,) = _loss_rows.bwd([xs, loss_target], [fw], [d_per_row])
    dxs = jnp.pad(dxs, ((CTX_LEN, 0), (0, 0)))
    small_names = tuple(n for n in CORE_PARAMS if n != "w_branch") + ("b_ada", "norm_w")
    dsmall = {n: [None] * DEPTH for n in small_names}
    dlayer = [None] * DEPTH
    contrib0 = contrib1 = None
    dsc16 = jnp.zeros((16, D_MODEL), F32)
    for l in reversed(range(DEPTH)):
        s, wl = layers[l], weights[l]
        (dres, dmerged), (dw_out, dgate) = _out_residual.bwd([s["xs"], s["merged"]], [wl["w_out"], s["gate"]], [dxs])
        rides = {}
        if l == 0:
            blocks1 = {n: _grad_blocks(n, g) for n, g in dlayer[1].items()}
            rides = {"ret": ([blocks1["w_branch"], blocks1["w_out"]], False),
                     "attn": ([_reshard_layer("w_out", dw_out).astype(BF16), blocks1["w_ada"]], False),
                     "delta": ([blocks1["w_in"]], False)}
        dh, dw_in, dp, rode = _core_backward(s["h"], wl["w_in16"], s["p"], s["saved"], dmerged, rides,
                                             branch_rides_in_attn=(l == 0))
        if l == 0:
            contrib1 = dict(w_in=rode["delta"][0], w_ada=rode["attn"][1], w_branch=rode["ret"][0],
                            w_out=rode["ret"][1])
            contrib0 = dict(w_out=rode["attn"][0], w_branch=rode["attn"][2])
        (dxn,), (dnw, dshift, dscale) = _norm_mod.bwd([s["xs"]], [s["nw"], s["shift"], s["scale"]], [dh])
        dxs = dres + dxn
        dmod_cx = jnp.concatenate([dshift, dscale, dgate], axis=1)
        dmod16 = jnp.zeros((16, 3 * D_MODEL), F32).at[0].set(dmod_cx[1]).at[1].set(dmod_cx[0])
        dsc16 = dsc16 + _matmul(dmod16, wl["w_ada"], "ada_da", "nt")
        dlayer[l] = dict(w_ada=_matmul(sc16, dmod16, "ada_db", "tn"), w_in=dw_in,
                         w_branch=dp["w_branch"], w_out=dw_out)
        for n in small_names:
            if n in dp:
                dsmall[n][l] = dp[n]
        dsmall["norm_w"][l] = dnw[0]
        dsmall["b_ada"][l] = dmod_cx[0] + dmod_cx[1]
    gsmall = {n: jnp.stack(v) for n, v in dsmall.items()}
    gsmall["final_norm_w"] = dfw[0]
    sig = jax.nn.sigmoid(c_ctx)
    gsmall["c_ctx"] = dsc16[1] * sig * (1.0 + c_ctx * (1.0 - sig))
    return loss, dxs[CTX_LEN:], gsmall, {n: dlayer[0][n] for n in ("w_ada", "w_in")}, contrib0, contrib1


SMALL = ("c_ctx", "b_ada", "norm_w", "a_log", "a_dt_bias", "a_norm_w", "b_sink", "c_decay", "c_norm_w",
         "final_norm_w")
WEIGHTS = ("c_ctx", "w_ada", "b_ada", "norm_w", "w_in", "a_conv_w", "a_log", "a_dt_bias", "a_norm_w", "b_sink",
           "c_decay", "c_norm_w", "w_branch", "w_out", "final_norm_w")
SMALL_PACK = 12288


def _unshard_conv(g):
    return g.transpose(1, 2, 0, 3).reshape(DEPTH, A_CONV, 3 * A_WIDTH)


def _reshard_conv(w):
    return w.reshape(DEPTH, A_CONV, N_DEV, 3 * A_WIDTH // N_DEV).transpose(2, 0, 1, 3)


def _pack_small(tree):
    flat = jnp.concatenate([tree[n].reshape(-1) for n in SMALL])
    return jnp.pad(flat, (0, SMALL_PACK - flat.shape[0])).reshape(SMALL_PACK // 128, 128)


def _unpack_small(packed, like):
    flat = packed.reshape(-1)
    out, off = {}, 0
    for n in SMALL:
        size = math.prod(like[n].shape)
        out[n] = flat[off:off + size].reshape(like[n].shape)
        off += size
    return out


def kernel(x, c, ctx, c_ctx, w_ada, b_ada, norm_w, w_in, a_conv_w, a_log, a_dt_bias, a_norm_w, b_sink, c_decay, c_norm_w, w_branch, w_out, final_norm_w, loss_target, m_c_ctx, m_w_ada, m_b_ada, m_norm_w, m_w_in, m_a_conv_w, m_a_log, m_a_dt_bias, m_a_norm_w, m_b_sink, m_c_decay, m_c_norm_w, m_w_branch, m_w_out, m_final_norm_w, v_c_ctx, v_w_ada, v_b_ada, v_norm_w, v_w_in, v_a_conv_w, v_a_log, v_a_dt_bias, v_a_norm_w, v_b_sink, v_c_decay, v_c_norm_w, v_w_branch, v_w_out, v_final_norm_w):
    w = dict(c_ctx=c_ctx, w_ada=w_ada, b_ada=b_ada, norm_w=norm_w, w_in=w_in, a_conv_w=a_conv_w, a_log=a_log,
             a_dt_bias=a_dt_bias, a_norm_w=a_norm_w, b_sink=b_sink, c_decay=c_decay, c_norm_w=c_norm_w,
             w_branch=w_branch, w_out=w_out, final_norm_w=final_norm_w)
    m = dict(c_ctx=m_c_ctx, w_ada=m_w_ada, b_ada=m_b_ada, norm_w=m_norm_w, w_in=m_w_in, a_conv_w=m_a_conv_w,
             a_log=m_a_log, a_dt_bias=m_a_dt_bias, a_norm_w=m_a_norm_w, b_sink=m_b_sink, c_decay=m_c_decay,
             c_norm_w=m_c_norm_w, w_branch=m_w_branch, w_out=m_w_out, final_norm_w=m_final_norm_w)
    v = dict(c_ctx=v_c_ctx, w_ada=v_w_ada, b_ada=v_b_ada, norm_w=v_norm_w, w_in=v_w_in, a_conv_w=v_a_conv_w,
             a_log=v_a_log, a_dt_bias=v_a_dt_bias, a_norm_w=v_a_norm_w, b_sink=v_b_sink, c_decay=v_c_decay,
             c_norm_w=v_c_norm_w, w_branch=v_w_branch, w_out=v_w_out, final_norm_w=v_final_norm_w)

    shards = {n: w[n].astype(BF16) for n in LAYER_SHARDED}
    first = _exchange([shards["w_ada"][0], shards["w_in"][0], w["a_conv_w"]], True, "gather_layer0")
    layer0 = _layer_weights(dict(w_ada=first[0], w_in=first[1]))
    small_w = {n: w[n] for n in SMALL}
    small_w["a_conv_w"] = _unshard_conv(first[2])
    loss, gx, gw, glayer0, contrib0, contrib1 = _forward_backward(
        small_w, layer0, {n: shards[n][0] for n in ("w_branch", "w_out")}, {n: shards[n][1] for n in LAYER_SHARDED},
        x[0], c[0], ctx[0], loss_target[0])
    loss = lax.psum(loss, ("x", "y", "c"))

    last = _scatter_two_level([_reshard_layer("w_ada", glayer0["w_ada"]).astype(BF16),
                               _grad_blocks("w_in", glayer0["w_in"]), _reshard_conv(gw["a_conv_w"])],
                              "scatter_layer0")
    contrib0["w_ada"], contrib0["w_in"] = last[0], last[1]
    small = _exchange([_pack_small(gw)], True, "gather_small_grads")[0]

    grad, delta, new_m, new_v = {}, {}, {}, {}
    for n in LAYER_SHARDED:
        shp = w[n].shape
        per_layer = (math.prod(shp[1:-1]), shp[-1])
        outs = _adamw_layers(*[a.reshape((DEPTH,) + per_layer) for a in (w[n], m[n], v[n])],
                             *[cb.reshape(cb.shape[:1] + per_layer) for cb in (contrib0[n], contrib1[n])], "adamw_" + n)
        grad[n], delta[n], new_m[n], new_v[n] = [o.reshape(shp) for o in outs]
    shp = a_conv_w.shape
    two_d = (math.prod(shp[:-1]), shp[-1])
    outs = _adamw(*[a.reshape(two_d) for a in (a_conv_w, m_a_conv_w, v_a_conv_w)],
                  last[2].reshape(last[2].shape[:1] + two_d), "adamw_a_conv_w")
    grad["a_conv_w"], delta["a_conv_w"], new_m["a_conv_w"], new_v["a_conv_w"] = [o.reshape(shp) for o in outs]
    outs = _adamw(_pack_small(w), _pack_small(m), _pack_small(v), small, "adamw_small")
    for tree, packed in zip((grad, delta, new_m, new_v), outs):
        tree.update(_unpack_small(packed, w))

    return (loss, gx[None], *[grad[n] for n in WEIGHTS], *[delta[n] for n in WEIGHTS],
            *[new_m[n] for n in WEIGHTS], *[new_v[n] for n in WEIGHTS])
```

```python
import functools
import math

import jax
import jax.numpy as jnp
from jax import lax
from jax.experimental import pallas as pl
from jax.experimental.pallas import tpu as pltpu

F32 = jnp.float32
BF16 = jnp.bfloat16
HIGHEST = lax.Precision.HIGHEST

D_MODEL = 1024
SEQ = 4096
DEPTH = 2
GRID_W = 64
CTX_LEN = 256
EPS = 1e-6
ROPE_BASE = 10000.0
BR_WIDTH = D_MODEL // 2
A_DK = 128
A_HEADS = 4
A_WIDTH = 512
A_CONV = 5
B_HD = 64
B_Q_HEADS = 8
B_KV_HEADS = 2
WINDOW = 128
B_BLOCK = 128
C_HD = 128
C_HEADS = 4
C_WIDTH = 512
CHUNK = 128
RET_CHUNK = 128
ADAM_LR = 0.001
ADAM_B1 = 0.9
ADAM_B2 = 0.999
ADAM_EPS = 1e-08
ADAM_WD = 0.01
ADAM_STEP = 10

N_DEV = 8
ROWS = CTX_LEN + SEQ
N_CHUNK = ROWS // CHUNK
IN_WIDTH = 8464
IN_PAD = 8704
NEG = -1e30

VMEM_LIMIT = 48 * 1024 * 1024
MESH = pl.DeviceIdType.MESH

C_AQ, C_AK, C_AV, C_AZ, C_BQ, C_BZ, C_CQ, C_CK, C_CV, C_CZ = (i * 512 for i in range(10))
C_MERGE = 5120
C_BKV = 8192
C_AB = 8448


def _cparams(sem=None):
    if sem is None:
        return pltpu.CompilerParams(vmem_limit_bytes=VMEM_LIMIT)
    return pltpu.CompilerParams(dimension_semantics=sem, vmem_limit_bytes=VMEM_LIMIT)


def _dg(a, b, ca, cb, prec=None):
    return lax.dot_general(a, b, (((ca,), (cb,)), ((), ())), preferred_element_type=F32, precision=prec)


@functools.partial(jax.custom_vjp, nondiff_argnums=(2, 3))
def _bdot(a, b, ca, cb):
    return _dg(a.astype(BF16), b.astype(BF16), ca, cb)


def _bdot_fwd(a, b, ca, cb):
    return _bdot(a, b, ca, cb), (a, b)


def _bdot_bwd(ca, cb, res, ct):
    a, b = res
    da = _bdot(ct, b, 1, 1 - cb) if ca == 1 else _bdot(b, ct, 1 - cb, 1)
    db = _bdot(a, ct, 1 - ca, 0) if cb == 0 else _bdot(ct, a, 0, 1 - ca)
    return da, db


_bdot.defvjp(_bdot_fwd, _bdot_bwd)


def _hdot(a, b):
    return _dg(a, b, 1, 0, lax.Precision.HIGH)


def _k_silu(x):
    return x / (1.0 + jnp.exp(-x))


def _k_sigmoid(x):
    return 1.0 / (1.0 + jnp.exp(-x))


@jax.custom_vjp
def _swap64(x):
    return pltpu.roll(x, 64, 1)


_swap64.defvjp(lambda x: (pltpu.roll(x, 64, 1), None), lambda _, ct: (pltpu.roll(ct, 64, 1),))


def _swap16_impl(x):
    lane = lax.broadcasted_iota(jnp.int32, x.shape, 1)
    return jnp.where((lane & 16) == 0, pltpu.roll(x, 112, 1), pltpu.roll(x, 16, 1))


@jax.custom_vjp
def _swap16(x):
    return _swap16_impl(x)


_swap16.defvjp(lambda x: (_swap16_impl(x), None), lambda _, ct: (_swap16_impl(ct),))


def _pick(dim, prefs):
    for p in prefs:
        if dim % p == 0:
            return p
    return dim


def _matmul(a, b, name, mode="nn", tiles=None, ride=None):
    ca, cb = {"nn": (1, 0), "nt": (1, 1), "tn": (0, 0)}[mode]
    m, k = a.shape[1 - ca], a.shape[ca]
    n = b.shape[1 - cb]
    if tiles is None:
        tiles = (_pick(m, (1088, 1024, 512, 256, 128)), _pick(n, (512, 256, 128)),
                 _pick(k, (1088, 1024, 512, 256, 128) if mode == "tn" else (2176, 2048, 1024, 512, 256, 128)))
    tm, tn, tk = tiles
    nk = k // tk
    a_spec = (pl.BlockSpec((tm, tk), lambda i, j, kk: (i, kk)) if ca == 1
              else pl.BlockSpec((tk, tm), lambda i, j, kk: (kk, i)))
    b_spec = (pl.BlockSpec((tk, tn), lambda i, j, kk: (kk, j)) if cb == 0
              else pl.BlockSpec((tn, tk), lambda i, j, kk: (j, kk)))

    def body(a_ref, b_ref, o_ref):
        part = _dg(a_ref[...].astype(BF16), b_ref[...].astype(BF16), ca, cb)
        if nk == 1:
            o_ref[...] = part
        else:
            kk = pl.program_id(2)

            @pl.when(kk == 0)
            def _():
                o_ref[...] = part

            @pl.when(kk > 0)
            def _():
                o_ref[...] += part

    grid = (m // tm, n // tn, nk)
    if ride is None:
        return pl.pallas_call(
            body,
            grid=grid,
            in_specs=[a_spec, b_spec],
            out_specs=pl.BlockSpec((tm, tn), lambda i, j, kk: (i, j)),
            out_shape=jax.ShapeDtypeStruct((m, n), F32),
            compiler_params=_cparams(("parallel", "parallel", "arbitrary")),
            name=name,
        )(a, b)
    body, r_in, r_out, r_shape, r_scratch = _riding(body, 2, 1, 0, ride, grid)
    return pl.pallas_call(
        body,
        grid=grid,
        in_specs=[a_spec, b_spec] + r_in,
        out_specs=[pl.BlockSpec((tm, tn), lambda i, j, kk: (i, j))] + r_out,
        out_shape=[jax.ShapeDtypeStruct((m, n), F32)] + r_shape,
        scratch_shapes=r_scratch,
        compiler_params=_cparams(("arbitrary", "arbitrary", "arbitrary")),
        name=name,
    )(a, b, *ride[0])


ROW_BLOCK = 256
ROW_VMEM_BUDGET = 16 * 1024 * 1024


def _pieces(val, pw):
    return [val[:, j * pw:(j + 1) * pw] for j in range(val.shape[1] // pw)]


def _flat(groups):
    arrays, sizes = [], []
    for g in groups:
        g = g if isinstance(g, (tuple, list)) else (g,)
        arrays += list(g)
        sizes.append(len(g))
    return arrays, sizes


def _regroup(refs, sizes):
    out, at = [], 0
    for n in sizes:
        val = refs[at][...]
        for r in refs[at + 1:at + n]:
            val = val + r[...]
        out.append(val)
        at += n
    return out


class _Rowwise:
    def __init__(self, fn, name, row_wpw, par_pw, out_wpw, n_diff=None):
        self.fn, self.name, self.row_wpw, self.par_pw, self.out_wpw = fn, name, row_wpw, par_pw, out_wpw
        self.n_diff = len(row_wpw) if n_diff is None else n_diff

        @jax.custom_vjp
        def call(rows, params):
            return self.fwd(rows, params)

        def call_fwd(rows, params):
            return self.fwd(rows, params), (rows, params)

        def call_bwd(res, douts):
            return self.bwd(res[0], res[1], douts)

        call.defvjp(call_fwd, call_bwd)
        self.call = call

    def _load(self, row_vals, par_refs, br, with_ctx):
        row = pl.program_id(0) * br + lax.broadcasted_iota(jnp.int32, (br, 1), 0)
        is_ctx = (row < (CTX_LEN if with_ctx else 0)).astype(F32)
        rows = [_pieces(v, pw) for v, (_, pw) in zip(row_vals, self.row_wpw)]
        pars = []
        for p, pw in zip(par_refs, self.par_pw):
            val = p[...].astype(F32)
            if p.shape[0] == 2:
                val = is_ctx * val[0:1, :] + (1.0 - is_ctx) * val[1:2, :]
            pars.append(_pieces(val, pw))
        return rows, pars, is_ctx

    def _block_rows(self, n_rows, widths):
        for br in (1088, 1024, 544, 512, 272):
            if n_rows % br == 0 and 2 * 4 * br * sum(widths) <= ROW_VMEM_BUDGET:
                return br
        return ROW_BLOCK

    def _row_specs(self, br, sizes, cols):
        out = []
        for (w, _), n, c in zip(self.row_wpw, sizes, cols):
            out += [pl.BlockSpec((br, w), lambda i, c=c: (i, c))] * n
        return out

    def fwd(self, rows, params, cols=None):
        arrays, sizes = _flat(rows)
        cols = cols or [0] * len(rows)
        n_rows = arrays[0].shape[0]
        n_in = len(arrays)
        br = self._block_rows(n_rows, [w for (w, _), n in zip(self.row_wpw, sizes) for _ in range(n)]
                              + [w for w, _ in self.out_wpw])

        def body(*refs):
            r, p, _ = self._load(_regroup(refs[:n_in], sizes), refs[n_in:n_in + len(params)], br, n_rows == ROWS)
            for o_ref, pieces, (_, pw) in zip(refs[n_in + len(params):], self.fn(r, p), self.out_wpw):
                for j, piece in enumerate(pieces):
                    o_ref[:, j * pw:(j + 1) * pw] = piece

        return pl.pallas_call(
            body,
            grid=(n_rows // br,),
            in_specs=self._row_specs(br, sizes, cols) + [pl.BlockSpec(p.shape, lambda i: (0, 0)) for p in params],
            out_specs=[pl.BlockSpec((br, w), lambda i: (i, 0)) for w, _ in self.out_wpw],
            out_shape=[jax.ShapeDtypeStruct((n_rows, w), F32) for w, _ in self.out_wpw],
            compiler_params=_cparams(("parallel",)),
            name=self.name + "_fwd",
        )(*arrays, *params)

    def bwd(self, rows, params, douts, cols=None, bf16_rows=()):
        arrays, sizes = _flat(rows)
        darrays, dsizes = _flat(douts)
        cols = cols or [0] * len(rows)
        n_rows = arrays[0].shape[0]
        n_in, n_par, n_dout, n_diff = len(arrays), len(params), len(darrays), self.n_diff
        br = self._block_rows(n_rows, [w for (w, _), n in zip(self.row_wpw, sizes) for _ in range(n)]
                              + [w for (w, _), n in zip(self.out_wpw, dsizes) for _ in range(n)]
                              + [w for w, _ in self.row_wpw[:n_diff]])

        def body(*refs):
            par_refs = refs[n_in:n_in + n_par]
            dout_refs = refs[n_in + n_par:n_in + n_par + n_dout]
            drow_refs = refs[n_in + n_par + n_dout:n_in + n_par + n_dout + n_diff]
            dpar_refs = refs[n_in + n_par + n_dout + n_diff:]

            @pl.when(pl.program_id(0) == 0)
            def _():
                for d in dpar_refs:
                    d[...] = jnp.zeros_like(d)

            r, p, is_ctx = self._load(_regroup(refs[:n_in], sizes), par_refs, br, n_rows == ROWS)
            cts = [_pieces(d, pw) for d, (_, pw) in zip(_regroup(dout_refs, dsizes), self.out_wpw)]
            fixed = r[n_diff:]
            _, vjp = jax.vjp(lambda rd, pp: self.fn(rd + fixed, pp), r[:n_diff], p)
            dr, dp = vjp(cts)
            for d_ref, pieces, (_, pw) in zip(drow_refs, dr, self.row_wpw):
                for j, piece in enumerate(pieces):
                    d_ref[:, j * pw:(j + 1) * pw] = piece.astype(d_ref.dtype)
            for d_ref, pieces, pw in zip(dpar_refs, dp, self.par_pw):
                for j, piece in enumerate(pieces):
                    lanes = slice(j * pw, (j + 1) * pw)
                    if d_ref.shape[0] != 2:
                        d_ref[:, lanes] += piece
                    else:
                        d_ref[0:1, lanes] += jnp.sum(is_ctx * piece, axis=0, keepdims=True)
                        d_ref[1:2, lanes] += jnp.sum((1.0 - is_ctx) * piece, axis=0, keepdims=True)

        par_specs = [pl.BlockSpec(p.shape, lambda i: (0, 0)) for p in params]
        dout_specs = []
        for (w, _), n in zip(self.out_wpw, dsizes):
            dout_specs += [pl.BlockSpec((br, w), lambda i: (i, 0))] * n
        drow_w = [w for w, _ in self.row_wpw[:n_diff]]
        g = pl.pallas_call(
            body,
            grid=(n_rows // br,),
            in_specs=self._row_specs(br, sizes, cols) + par_specs + dout_specs,
            out_specs=[pl.BlockSpec((br, w), lambda i: (i, 0)) for w in drow_w] + par_specs,
            out_shape=[jax.ShapeDtypeStruct((n_rows, w), BF16 if a in bf16_rows else F32) for a, w in enumerate(drow_w)]
            + [jax.ShapeDtypeStruct(p.shape, F32) for p in params],
            compiler_params=_cparams(("arbitrary",)),
            name=self.name + "_bwd",
        )(*arrays, *params, *darrays)
        return list(g[:n_diff]), list(g[n_diff:])


def _fn_norm_mod(rows, pars):
    (x,), (nw,), (shift,), (scale,) = rows[0], pars[0], pars[1], pars[2]
    y = x * lax.rsqrt(jnp.mean(x * x, axis=-1, keepdims=True) + EPS) * nw
    return [[y * (1.0 + scale) + shift]]


def _fn_head_rms_gate(rows, pars):
    (w,) = pars[0]
    return [[o * lax.rsqrt(jnp.mean(o * o, axis=-1, keepdims=True) + EPS) * w * _k_silu(z)
             for o, z in zip(rows[0], rows[1])]]


def _fn_group_norm_gate(rows, pars):
    out = []
    for o, z, w in zip(rows[0], rows[1], pars[0]):
        mu = jnp.mean(o, axis=-1, keepdims=True)
        var = jnp.mean(jnp.square(o - mu), axis=-1, keepdims=True)
        out.append((o - mu) * lax.rsqrt(var + EPS) * w * _k_silu(z))
    return [out]


def _fn_gate(rows, pars):
    return [[o * _k_silu(z) for o, z in zip(rows[0], rows[1])]]


def _fn_branch_merge(rows, pars):
    (ya,), (yb,), (yc,), (ma,), (mb,), (mc,) = rows
    (wa,), (wb,), (wc,) = pars
    return [[_k_sigmoid(ma) * _bdot(ya, wa, 1, 0) + _k_sigmoid(mb) * _bdot(yb, wb, 1, 0)
             + _k_sigmoid(mc) * _bdot(yc, wc, 1, 0)]]


def _fn_out_residual(rows, pars):
    (res,), (merged,), (w,), (gate,) = rows[0], rows[1], pars[0], pars[1]
    return [[res + gate * _bdot(merged, w, 1, 0)]]


def _fn_loss(rows, pars):
    (x,), (target,), (w,) = rows[0], rows[1], pars[0]
    y = x * lax.rsqrt(jnp.mean(x * x, axis=-1, keepdims=True) + EPS) * w
    per_row = 0.5 * jnp.mean(jnp.square(y - target), axis=-1, keepdims=True)
    return [[jnp.broadcast_to(per_row, (per_row.shape[0], 128))]]


def _fn_b_rope(rows, pars):
    q, (k, v), (cos,), (sin,) = rows
    rot = lambda x: x * cos + _swap16(x) * sin
    return [[rot(x) for x in q], [rot(k), v]]


def _fn_c_rope(rows, pars):
    q, k, (cos,), (sin,) = rows
    rot = lambda x: x * cos + _swap64(x) * sin
    return [[rot(x) for x in q], [rot(x) * (C_HD ** -0.5) for x in k]]


_norm_mod = _Rowwise(_fn_norm_mod, "norm_mod", [(D_MODEL, D_MODEL)], [D_MODEL] * 3, [(D_MODEL, D_MODEL)])
_out_residual = _Rowwise(_fn_out_residual, "out_residual", [(D_MODEL, D_MODEL)] * 2, [D_MODEL] * 2,
                         [(D_MODEL, D_MODEL)])
_loss_rows = _Rowwise(_fn_loss, "loss", [(D_MODEL, D_MODEL)] * 2, [D_MODEL], [(128, 128)], n_diff=1)
_a_out = _Rowwise(_fn_head_rms_gate, "a_out", [(512, 128)] * 2, [128], [(512, 128)])
_c_out = _Rowwise(_fn_group_norm_gate, "c_out", [(512, 128)] * 2, [128], [(512, 128)])
_b_out = _Rowwise(_fn_gate, "b_out", [(512, 512)] * 2, [], [(512, 512)])
_branch_merge = _Rowwise(_fn_branch_merge, "branch_merge", [(512, 512)] * 3 + [(D_MODEL, D_MODEL)] * 3,
                         [D_MODEL] * 3, [(D_MODEL, D_MODEL)])
_b_rope = _Rowwise(_fn_b_rope, "b_rope", [(512, 128), (256, 128), (128, 128), (128, 128)], [],
                   [(512, 128), (256, 128)], n_diff=2)
_c_rope = _Rowwise(_fn_c_rope, "c_rope", [(512, 128), (512, 128), (128, 128), (128, 128)], [],
                   [(512, 128), (512, 128)], n_diff=2)


HALO = 8
EXT = ROW_BLOCK + 2 * HALO


def _halo_specs(col, width=512):
    last = ROWS // HALO - 1
    per = ROW_BLOCK // HALO
    prev = pl.BlockSpec((HALO, width), lambda i: (jnp.maximum(i * per - 1, 0), col))
    cur = pl.BlockSpec((ROW_BLOCK, width), lambda i: (i, col))
    nxt = pl.BlockSpec((HALO, width), lambda i: (jnp.minimum((i + 1) * per, last), col))
    return [prev, cur, nxt]


def _extended(prev_ref, cur_ref, next_ref):
    i = pl.program_id(0)
    prev_ok = i >= 2
    next_ok = jnp.logical_and(i >= 1, i < ROWS // ROW_BLOCK - 1)
    return jnp.concatenate([jnp.where(prev_ok, prev_ref[...], 0.0), cur_ref[...],
                            jnp.where(next_ok, next_ref[...], 0.0)], axis=0)


def _conv_taps(x_ext, w_ref, flip):
    acc = None
    for j in range(A_CONV):
        shift = (j - 2) if flip else (2 - j)
        term = w_ref[j:j + 1, :] * pltpu.roll(x_ext, shift % EXT, 0)
        acc = term if acc is None else acc + term
    return acc


def _conv_post(pre_pieces, normalize, scale):
    out = []
    for p in pre_pieces:
        y = _k_silu(p)
        if normalize:
            y = y * lax.rsqrt(jnp.sum(y * y, axis=-1, keepdims=True) + EPS) * scale
        out.append(y)
    return out


def _a_prep_fwd(proj, conv8, col, normalize, scale, name):
    def body(prev_ref, cur_ref, next_ref, w_ref, o_ref):
        pre = _conv_taps(_extended(prev_ref, cur_ref, next_ref), w_ref, False)[HALO:HALO + ROW_BLOCK]
        for h, y in enumerate(_conv_post(_pieces(pre, 128), normalize, scale)):
            o_ref[:, h * 128:(h + 1) * 128] = y

    return pl.pallas_call(
        body,
        grid=(ROWS // ROW_BLOCK,),
        in_specs=_halo_specs(col) + [pl.BlockSpec((8, 512), lambda i: (0, col))],
        out_specs=pl.BlockSpec((ROW_BLOCK, 512), lambda i: (i, 0)),
        out_shape=jax.ShapeDtypeStruct((ROWS, 512), F32),
        compiler_params=_cparams(("parallel",)),
        name=name + "_fwd",
    )(proj, proj, proj, conv8)


def _a_prep_bwd(proj, conv8, col, normalize, scale, dout_f, dout_r, name):
    def body(xp, xc, xn, w_ref, fp, fc, fn_, rp, rc, rn, dx_ref, dw_ref):
        @pl.when(pl.program_id(0) == 0)
        def _():
            dw_ref[...] = jnp.zeros_like(dw_ref)

        x_ext = _extended(xp, xc, xn)
        dout = _extended(fp, fc, fn_) + _extended(rp, rc, rn)
        pre = _conv_taps(x_ext, w_ref, False)
        _, vjp = jax.vjp(lambda p: _conv_post(p, normalize, scale), _pieces(pre, 128))
        (dpre,) = vjp(_pieces(dout, 128))
        dpre = jnp.concatenate(dpre, axis=1)
        dx_ref[...] = _conv_taps(dpre, w_ref, True)[HALO:HALO + ROW_BLOCK].astype(BF16)
        own = dpre[HALO:HALO + ROW_BLOCK]
        for j in range(A_CONV):
            shifted = pltpu.roll(x_ext, (2 - j) % EXT, 0)[HALO:HALO + ROW_BLOCK]
            dw_ref[j:j + 1, :] += jnp.sum(own * shifted, axis=0, keepdims=True)

    return pl.pallas_call(
        body,
        grid=(ROWS // ROW_BLOCK,),
        in_specs=_halo_specs(col) + [pl.BlockSpec((8, 512), lambda i: (0, col))] + _halo_specs(0) + _halo_specs(0),
        out_specs=[pl.BlockSpec((ROW_BLOCK, 512), lambda i: (i, 0)), pl.BlockSpec((8, 512), lambda i: (0, 0))],
        out_shape=[jax.ShapeDtypeStruct((ROWS, 512), BF16), jax.ShapeDtypeStruct((8, 512), F32)],
        compiler_params=_cparams(("arbitrary",)),
        name=name + "_bwd",
    )(proj, proj, proj, conv8, dout_f, dout_f, dout_f, dout_r, dout_r, dout_r)


N_CHAIN = 8


def _rev_chunk(s, chunk):
    n_ctx, n_all = CTX_LEN // chunk, ROWS // chunk
    return jnp.where(s < n_ctx, n_ctx - 1 - s, n_all + n_ctx - 1 - s)


def _scan_specs(step_of, chunk, v_col=0):
    cf = step_of
    cr = lambda n: _rev_chunk(step_of(n), chunk)

    def pair(shape, index):
        return (pl.BlockSpec(shape, lambda n: index(cf(n))), pl.BlockSpec(shape, lambda n: index(cr(n))))

    return dict(
        tok=pair((chunk, 512), lambda c: (c, 0)),
        tokv=pair((chunk, 512), lambda c: (c, v_col)),
        row=pair((4, 1, 1, chunk), lambda c: (0, c, 0, 0)),
        one=pair((4, 1, 1, 1), lambda c: (0, c, 0, 0)),
        state=pair((None, 4, 128, 128), lambda c: (c, 0, 0, 0)),
        tinv=pair((None, 4, chunk, chunk), lambda c: (c, 0, 0, 0)),
    )


def _both(specs, kinds):
    out = []
    for kind in kinds:
        out += list(specs[kind])
    return out


def _scan_call(body, name, in_specs, out_specs, out_shape, operands, ride, chunk):
    grid = (ROWS // chunk,)
    body, r_in, r_out, r_shape, r_scratch = _riding(body, len(in_specs), len(out_specs), 1, ride, grid)
    return pl.pallas_call(
        body,
        grid=grid,
        in_specs=in_specs + r_in,
        out_specs=out_specs + r_out,
        out_shape=out_shape + r_shape,
        scratch_shapes=[pltpu.VMEM((N_CHAIN, 128, 128), F32)] + r_scratch,
        compiler_params=_cparams(("arbitrary",)),
        name=name,
    )(*operands, *(ride[0] if ride else []))


def _chain_masks():
    ii = lax.broadcasted_iota(jnp.int32, (CHUNK, CHUNK), 0)
    jj = lax.broadcasted_iota(jnp.int32, (CHUNK, CHUNK), 1)
    eye = jnp.where(ii == jj, 1.0, 0.0).astype(F32)
    lower = (ii >= jj, ii > jj)
    upper = (ii <= jj, ii < jj)
    return [lower] * 4 + [upper] * 4, eye


INV_BLOCK = 64


def _series_inverse(ls):
    ii = lax.broadcasted_iota(jnp.int32, (INV_BLOCK, INV_BLOCK), 0)
    jj = lax.broadcasted_iota(jnp.int32, (INV_BLOCK, INV_BLOCK), 1)
    eye = jnp.where(ii == jj, 1.0, 0.0).astype(F32)
    doublings = INV_BLOCK.bit_length() - 2
    xs = [eye - l for l in ls]
    ps = [_hdot(l, l) for l in ls]
    for i in range(doublings):
        xs = [x + _hdot(x, p) for x, p in zip(xs, ps)]
        if i < doublings - 1:
            ps = [_hdot(p, p) for p in ps]
    return xs


def _tri_inv_all(ls, upper):
    if CHUNK == INV_BLOCK:
        return _series_inverse(ls)
    n, h = len(ls), INV_BLOCK
    diag = _series_inverse([l[:h, :h] for l in ls] + [l[h:, h:] for l in ls])
    out = []
    zero = jnp.zeros((h, h), F32)
    for i, l in enumerate(ls):
        a, d = diag[i], diag[n + i]
        if upper[i]:
            off = -_hdot(_hdot(a, l[:h, h:]), d)
            out.append(jnp.concatenate([jnp.concatenate([a, off], axis=1), jnp.concatenate([zero, d], axis=1)], axis=0))
        else:
            off = -_hdot(_hdot(d, l[h:, :h]), a)
            out.append(jnp.concatenate([jnp.concatenate([a, zero], axis=1), jnp.concatenate([off, d], axis=1)], axis=0))
    return out


@jax.custom_vjp
def _inv_saved(l, x):
    return x


def _inv_saved_fwd(l, x):
    return x, x


def _inv_saved_bwd(x, dx):
    return -_bdot(x, _bdot(dx, x, 1, 1), 0, 0), jnp.zeros_like(x)


_inv_saved.defvjp(_inv_saved_fwd, _inv_saved_bwd)


def _delta_chains(q, k, v, beta_r, gcr, gl, s, masks, eye, tinv_saved):
    n = range(len(q))
    beta = [jnp.sum(eye * beta_r[i], axis=1, keepdims=True) for i in n]
    gcc = [jnp.sum(eye * gcr[i], axis=1, keepdims=True) for i in n]
    decay = [jnp.exp(jnp.where(masks[i][0], gcc[i] - gcr[i], NEG)) for i in n]
    kb = [k[i] * beta[i] for i in n]
    lmat = [jnp.where(masks[i][1], _bdot(kb[i], k[i], 1, 1) * decay[i], 0.0) for i in n]
    if tinv_saved is None:
        tinv = _tri_inv_all(lmat, [i >= 4 for i in n])
    else:
        tinv = [_inv_saved(lmat[i], tinv_saved[i]) for i in n]
    eg = [jnp.exp(gcc[i]) for i in n]
    u = [_bdot(tinv[i], v[i] * beta[i], 1, 0) for i in n]
    w = [_bdot(tinv[i], kb[i] * eg[i], 1, 0) for i in n]
    qk = [_bdot(q[i], k[i], 1, 1) * decay[i] for i in n]
    v_new = [u[i] - _bdot(w[i], s[i], 1, 0) for i in n]
    o = [_bdot(q[i] * eg[i], s[i], 1, 0) + _bdot(qk[i], v_new[i], 1, 0) for i in n]
    s_new = [s[i] * jnp.exp(gl[i]) + _bdot(k[i] * jnp.exp(gl[i] - gcc[i]), v_new[i], 0, 0) for i in n]
    return (o, s_new), tinv


def _chain_loads(tok_pairs, small_pairs):
    toks = [[pair[i // 4][:, (i % 4) * 128:(i % 4 + 1) * 128] for i in range(N_CHAIN)] for pair in tok_pairs]
    smalls = [[pair[i // 4][i % 4] for i in range(N_CHAIN)] for pair in small_pairs]
    return toks, smalls


def _delta_fwd_call(q, k, v, beta, gc, gl, ride=None):
    sp = _scan_specs(lambda n: n, CHUNK)

    def body(qf, qr, kf, kr, vf, vr, bf, br, gcrf, gcrr, glf, glr, of, orv, ssf, ssr, tsf, tsr, s_scr):
        @pl.when(pl.program_id(0) == 0)
        def _():
            s_scr[...] = jnp.zeros_like(s_scr)

        masks, eye = _chain_masks()
        (qs, ks, vs), _ = _chain_loads([(qf, qr), (kf, kr), (vf, vr)], [])
        bs = [(bf, br)[i // 4][i % 4, 0] for i in range(N_CHAIN)]
        gcrs = [(gcrf, gcrr)[i // 4][i % 4, 0] for i in range(N_CHAIN)]
        gls = [(glf, glr)[i // 4][i % 4, 0] for i in range(N_CHAIN)]
        ss = [s_scr[i] for i in range(N_CHAIN)]
        (o, s_new), tinv = _delta_chains(qs, ks, vs, bs, gcrs, gls, ss, masks, eye, None)
        for i in range(N_CHAIN):
            d, h = i // 4, i % 4
            (ssf, ssr)[d][h] = ss[i]
            (tsf, tsr)[d][h] = tinv[i]
            (of, orv)[d][:, h * 128:(h + 1) * 128] = o[i]
            s_scr[i] = s_new[i]

    return _scan_call(
        body, "delta_fwd",
        _both(sp, ["tok", "tok", "tok", "row", "row", "one"]),
        _both(sp, ["tok", "state", "tinv"]),
        [jax.ShapeDtypeStruct((ROWS, 512), F32)] * 2 + [jax.ShapeDtypeStruct((N_CHUNK, 4, 128, 128), F32)] * 2
        + [jax.ShapeDtypeStruct((N_CHUNK, 4, CHUNK, CHUNK), F32)] * 2,
        [q, q, k, k, v, v, *beta, *gc, *gl], ride, CHUNK)


def _delta_bwd_call(q, k, v, beta, gc, gl, ssave, tsave, do, ride=None):
    sp = _scan_specs(lambda n: N_CHUNK - 1 - n, CHUNK)

    def body(qf, qr, kf, kr, vf, vr, bf, br, gcrf, gcrr, glf, glr, ssf, ssr, tsf, tsr, dof, dor,
             dqf, dqr, dkf, dkr, dvf, dvr, dbf, dbr, dgcrf, dgcrr, dglf, dglr, ds_scr):
        @pl.when(pl.program_id(0) == 0)
        def _():
            ds_scr[...] = jnp.zeros_like(ds_scr)

        masks, eye = _chain_masks()
        (qs, ks, vs, dos), (ss, ts) = _chain_loads(
            [(qf, qr), (kf, kr), (vf, vr), (dof, dor)], [(ssf, ssr), (tsf, tsr)])
        bs = [(bf, br)[i // 4][i % 4, 0] for i in range(N_CHAIN)]
        gcrs = [(gcrf, gcrr)[i // 4][i % 4, 0] for i in range(N_CHAIN)]
        gls = [(glf, glr)[i // 4][i % 4, 0] for i in range(N_CHAIN)]
        fn = lambda *a: _delta_chains(*a, masks, eye, ts)
        _, vjp, _ = jax.vjp(fn, qs, ks, vs, bs, gcrs, gls, ss, has_aux=True)
        dq, dk, dv, db, dgcr, dgl, ds = vjp((dos, [ds_scr[i] for i in range(N_CHAIN)]))
        for i in range(N_CHAIN):
            d, h = i // 4, i % 4
            hs = slice(h * 128, (h + 1) * 128)
            (dqf, dqr)[d][:, hs] = dq[i]
            (dkf, dkr)[d][:, hs] = dk[i]
            (dvf, dvr)[d][:, hs] = dv[i]
            (dbf, dbr)[d][h, 0] = db[i]
            (dgcrf, dgcrr)[d][h, 0] = dgcr[i]
            (dglf, dglr)[d][h, 0] = dgl[i]
            ds_scr[i] = ds[i]

    tok = jax.ShapeDtypeStruct((ROWS, 512), F32)
    return _scan_call(
        body, "delta_bwd",
        _both(sp, ["tok", "tok", "tok", "row", "row", "one", "state", "tinv", "tok"]),
        _both(sp, ["tok", "tok", "tok", "row", "row", "one"]),
        [tok] * 6 + [jax.ShapeDtypeStruct((4, N_CHUNK, 1, CHUNK), F32)] * 4
        + [jax.ShapeDtypeStruct((4, N_CHUNK, 1, 1), F32)] * 2,
        [q, q, k, k, v, v, *beta, *gc, *gl, *ssave, *tsave, do, do], ride, CHUNK)


def _ret_chains(q, k, v, dm, qs, ks, cd, s):
    n = range(len(q))
    a = [_bdot(q[i], k[i], 1, 1) * dm[i] for i in n]
    o = [_bdot(a[i], v[i], 1, 0) + _bdot(q[i] * qs[i], s[i], 1, 0) for i in n]
    s_new = [s[i] * cd[i] + _bdot(k[i] * ks[i], v[i], 0, 0) for i in n]
    return o, s_new


RET_CONST_SHAPES = ((N_CHAIN, RET_CHUNK, RET_CHUNK), (N_CHAIN, RET_CHUNK, 1), (N_CHAIN, RET_CHUNK, 1), (N_CHAIN, 1, 1))


def _ret_const_specs():
    return [pl.BlockSpec(shape, lambda n: (0, 0, 0)) for shape in RET_CONST_SHAPES]


def _ret_fwd_call(q, k, v, v_col, dm, qs, ks, cd, ride=None):
    sp = _scan_specs(lambda n: n, RET_CHUNK, v_col)

    def body(qf, qr, kf, kr, vf, vr, dm_ref, qs_ref, ks_ref, cd_ref, of, orv, ssf, ssr, s_scr):
        @pl.when(pl.program_id(0) == 0)
        def _():
            s_scr[...] = jnp.zeros_like(s_scr)

        (qc, kc, vc), _ = _chain_loads([(qf, qr), (kf, kr), (vf, vr)], [])
        ss = [s_scr[i] for i in range(N_CHAIN)]
        consts = [[r[i] for i in range(N_CHAIN)] for r in (dm_ref, qs_ref, ks_ref, cd_ref)]
        o, s_new = _ret_chains(qc, kc, vc, *consts, ss)
        for i in range(N_CHAIN):
            d, h = i // 4, i % 4
            (ssf, ssr)[d][h] = ss[i]
            (of, orv)[d][:, h * 128:(h + 1) * 128] = o[i]
            s_scr[i] = s_new[i]

    return _scan_call(
        body, "ret_fwd",
        _both(sp, ["tok", "tok", "tokv"]) + _ret_const_specs(),
        _both(sp, ["tok", "state"]),
        [jax.ShapeDtypeStruct((ROWS, 512), F32)] * 2
        + [jax.ShapeDtypeStruct((ROWS // RET_CHUNK, 4, 128, 128), F32)] * 2,
        [q, q, k, k, v, v, dm, qs, ks, cd], ride, RET_CHUNK)


def _ret_bwd_call(q, k, v, v_col, dm, qs, ks, cd, ssave, do, ride=None):
    sp = _scan_specs(lambda n: ROWS // RET_CHUNK - 1 - n, RET_CHUNK, v_col)

    def body(qf, qr, kf, kr, vf, vr, dm_ref, qs_ref, ks_ref, cd_ref, ssf, ssr, dof, dor,
             dqf, dqr, dkf, dkr, dvf, dvr, ddm_ref, dqs_ref, dks_ref, dcd_ref, ds_scr):
        @pl.when(pl.program_id(0) == 0)
        def _():
            ds_scr[...] = jnp.zeros_like(ds_scr)
            ddm_ref[...] = jnp.zeros_like(ddm_ref)
            dqs_ref[...] = jnp.zeros_like(dqs_ref)
            dks_ref[...] = jnp.zeros_like(dks_ref)
            dcd_ref[...] = jnp.zeros_like(dcd_ref)

        (qc, kc, vc, dos), (ss,) = _chain_loads([(qf, qr), (kf, kr), (vf, vr), (dof, dor)], [(ssf, ssr)])
        consts = [[r[i] for i in range(N_CHAIN)] for r in (dm_ref, qs_ref, ks_ref, cd_ref)]
        _, vjp = jax.vjp(_ret_chains, qc, kc, vc, *consts, ss)
        dq, dk, dv, ddm, dqs, dks, dcd, ds = vjp((dos, [ds_scr[i] for i in range(N_CHAIN)]))
        for i in range(N_CHAIN):
            d, h = i // 4, i % 4
            hs = slice(h * 128, (h + 1) * 128)
            (dqf, dqr)[d][:, hs] = dq[i]
            (dkf, dkr)[d][:, hs] = dk[i]
            (dvf, dvr)[d][:, hs] = dv[i]
            ddm_ref[i] += ddm[i]
            dqs_ref[i] += dqs[i]
            dks_ref[i] += dks[i]
            dcd_ref[i] += dcd[i]
            ds_scr[i] = ds[i]

    tok = jax.ShapeDtypeStruct((ROWS, 512), F32)
    return _scan_call(
        body, "ret_bwd",
        _both(sp, ["tok", "tok", "tokv"]) + _ret_const_specs() + _both(sp, ["state", "tok"]),
        _both(sp, ["tok", "tok", "tok"]) + _ret_const_specs(),
        [tok] * 6 + [jax.ShapeDtypeStruct(shape, F32) for shape in RET_CONST_SHAPES],
        [q, q, k, k, v, v, dm, qs, ks, cd, *ssave, do, do], ride, RET_CHUNK)


N_QBLK = ROWS // B_BLOCK
CTX_QBLK = CTX_LEN // B_BLOCK


def _attn_heads(q, kc, vc, kw, vw, sink, valid):
    n = range(len(q))
    qs = [q[i] * (B_HD ** -0.5) for i in n]
    s_c = [_bdot(qs[i], kc[i], 1, 1) for i in n]
    s_w = [jnp.where(valid, _bdot(qs[i], kw[i], 1, 1), NEG) for i in n]
    m = [lax.stop_gradient(jnp.maximum(jnp.maximum(jnp.max(s_c[i], axis=-1, keepdims=True), sink[i]),
                                       jnp.max(s_w[i], axis=-1, keepdims=True))) for i in n]
    e_c = [jnp.exp(s_c[i] - m[i]) for i in n]
    e_w = [jnp.exp(s_w[i] - m[i]) for i in n]
    den = [jnp.sum(e_c[i], axis=-1, keepdims=True) + jnp.sum(e_w[i], axis=-1, keepdims=True)
           + jnp.exp(sink[i] - m[i]) for i in n]
    return [(_bdot(e_c[i], vc[i], 1, 0) + _bdot(e_w[i], vw[i], 1, 0)) / den[i] for i in n]


def _attn_loads(q_ref, kv_ref, sink_ref, start):
    q, kc, vc, kw, vw, sink = [], [], [], [], [], []
    for hk in range(B_KV_HEADS):
        ks = slice(hk * B_HD, (hk + 1) * B_HD)
        vs = slice(128 + hk * B_HD, 128 + (hk + 1) * B_HD)
        grp = (kv_ref[0:CTX_LEN, ks], kv_ref[0:CTX_LEN, vs],
               kv_ref[pl.ds(start, 3 * B_BLOCK), ks], kv_ref[pl.ds(start, 3 * B_BLOCK), vs])
        for g in range(4):
            h = hk * 4 + g
            q.append(q_ref[:, h * B_HD:(h + 1) * B_HD])
            for lst, val in zip((kc, vc, kw, vw), grp):
                lst.append(val)
            sink.append(jnp.full((1, 1), sink_ref[h], F32))
    return q, kc, vc, kw, vw, sink


def _window(blk):
    xblk = blk - CTX_QBLK
    first = jnp.clip((xblk - 1) * B_BLOCK, 0, SEQ - 3 * B_BLOCK)
    qpos = xblk * B_BLOCK + lax.broadcasted_iota(jnp.int32, (B_BLOCK, 3 * B_BLOCK), 0)
    kpos = first + lax.broadcasted_iota(jnp.int32, (B_BLOCK, 3 * B_BLOCK), 1)
    far = jnp.where(blk >= CTX_QBLK, 0, 2 * SEQ)
    valid = jnp.abs(kpos - qpos) + far <= WINDOW
    return pl.multiple_of(first + CTX_LEN, B_BLOCK), valid


def _attn_specs():
    qspec = pl.BlockSpec((B_BLOCK, 512), lambda i: (i, 0))
    kvspec = pl.BlockSpec((ROWS, 256), lambda i: (0, 0))
    return qspec, kvspec, pl.BlockSpec(memory_space=pltpu.SMEM)


def _attn_fwd_call(q, kv, sink, ride=None):
    def body(q_ref, kv_ref, sink_ref, o_ref):
        start, valid = _window(pl.program_id(0))
        out = _attn_heads(*_attn_loads(q_ref, kv_ref, sink_ref, start), valid)
        for h in range(B_Q_HEADS):
            o_ref[:, h * B_HD:(h + 1) * B_HD] = out[h]

    qspec, kvspec, sspec = _attn_specs()
    body, r_in, r_out, r_shape, r_scratch = _riding(body, 3, 1, 0, ride, (N_QBLK,))
    return pl.pallas_call(
        body,
        grid=(N_QBLK,),
        in_specs=[qspec, kvspec, sspec] + r_in,
        out_specs=[qspec] + r_out,
        out_shape=[jax.ShapeDtypeStruct((ROWS, 512), F32)] + r_shape,
        scratch_shapes=r_scratch,
        compiler_params=_cparams(("arbitrary",)),
        name="attn_fwd",
    )(q, kv, sink, *(ride[0] if ride else []))


def _attn_bwd_call(q, kv, sink, do, ride=None):
    def body(q_ref, kv_ref, sink_ref, do_ref, dq_ref, dkv_ref, dsink_ref):
        @pl.when(pl.program_id(0) == 0)
        def _():
            dkv_ref[...] = jnp.zeros_like(dkv_ref)
            dsink_ref[...] = jnp.zeros_like(dsink_ref)

        start, valid = _window(pl.program_id(0))
        _, vjp = jax.vjp(functools.partial(_attn_heads, valid=valid), *_attn_loads(q_ref, kv_ref, sink_ref, start))
        dq, dkc, dvc, dkw, dvw, dsink = vjp([do_ref[:, h * B_HD:(h + 1) * B_HD] for h in range(B_Q_HEADS)])
        for h in range(B_Q_HEADS):
            dq_ref[:, h * B_HD:(h + 1) * B_HD] = dq[h]
            dsink_ref[h:h + 1, :] += jnp.broadcast_to(dsink[h], (1, 128))
        for hk in range(B_KV_HEADS):
            ks = slice(hk * B_HD, (hk + 1) * B_HD)
            vs = slice(128 + hk * B_HD, 128 + (hk + 1) * B_HD)
            grp = lambda parts: parts[hk * 4] + parts[hk * 4 + 1] + parts[hk * 4 + 2] + parts[hk * 4 + 3]
            dkv_ref[0:CTX_LEN, ks] += grp(dkc)
            dkv_ref[0:CTX_LEN, vs] += grp(dvc)
            dkv_ref[pl.ds(start, 3 * B_BLOCK), ks] += grp(dkw)
            dkv_ref[pl.ds(start, 3 * B_BLOCK), vs] += grp(dvw)

    qspec, kvspec, sspec = _attn_specs()
    body, r_in, r_out, r_shape, r_scratch = _riding(body, 4, 3, 0, ride, (N_QBLK,))
    return pl.pallas_call(
        body,
        grid=(N_QBLK,),
        in_specs=[qspec, kvspec, sspec, qspec] + r_in,
        out_specs=[qspec, kvspec, pl.BlockSpec((8, 128), lambda i: (0, 0))] + r_out,
        out_shape=[jax.ShapeDtypeStruct((ROWS, 512), F32), jax.ShapeDtypeStruct((ROWS, 256), F32),
                   jax.ShapeDtypeStruct((8, 128), F32)] + r_shape,
        scratch_shapes=r_scratch,
        compiler_params=_cparams(("arbitrary",)),
        name="attn_bwd",
    )(q, kv, sink, do, *(ride[0] if ride else []))


def _my_id():
    return 4 * lax.axis_index("x") + 2 * lax.axis_index("y") + lax.axis_index("c")


def _peer(k):
    x, y, c = lax.axis_index("x"), lax.axis_index("y"), lax.axis_index("c")
    return (1 - x if k & 4 else x, 1 - y if k & 2 else y, 1 - c if k & 1 else c)


SAME_CORE_PEERS = (2, 4, 6)


def _scatter_copies(ins, outs, sems):
    send_sems, recv_sems, local_sems = sems
    me = _my_id()
    own, remote = [], []
    for a in range(len(ins)):
        own.append(pltpu.make_async_copy(ins[a].at[me], outs[a].at[me], local_sems.at[a]))
        for k in range(1, N_DEV):
            peer_slot = jnp.bitwise_xor(me, k)
            common = dict(src_ref=ins[a].at[peer_slot], send_sem=send_sems.at[a, k - 1],
                          recv_sem=recv_sems.at[a, k - 1], device_id=_peer(k), device_id_type=MESH)
            remote.append((pltpu.make_async_remote_copy(dst_ref=outs[a].at[me], **common),
                           pltpu.make_async_remote_copy(dst_ref=outs[a].at[peer_slot], **common)))
    return own, remote


def _gather_copy(outs, sems, a, k, src, slot, to):
    return pltpu.make_async_remote_copy(src_ref=src, dst_ref=outs[a].at[slot], send_sem=sems[0].at[a, k - 1],
                                        recv_sem=sems[1].at[a, k - 1], device_id=_peer(to), device_id_type=MESH)


def _gather_first_copies(ins, outs, sems):
    me = _my_id()
    own = [pltpu.make_async_copy(ins[a], outs[a].at[me], sems[2].at[a]) for a in range(len(ins))]
    direct = [_gather_copy(outs, sems, a, k, ins[a], me, k) for a in range(len(ins)) for k in (1,) + SAME_CORE_PEERS]
    return own, direct


def _exchange_start(ins, outs, sems, gather):
    own, remote = _gather_first_copies(ins, outs, sems) if gather else _scatter_copies(ins, outs, sems)
    for cp in own:
        cp.start()
    for cp in remote:
        (cp if gather else cp[0]).start()


def _exchange_wait(ins, outs, sems, gather):
    if not gather:
        own, remote = _scatter_copies(ins, outs, sems)
        for _, arrival in remote:
            arrival.wait_recv()
        for send, _ in remote:
            send.wait_send()
        for cp in own:
            cp.wait()
        return
    me = _my_id()
    own, direct = _gather_first_copies(ins, outs, sems)
    passed = []
    for a in range(len(ins)):
        for k in SAME_CORE_PEERS:
            origin = jnp.bitwise_xor(me, k)
            _gather_copy(outs, sems, a, k, ins[a], origin, k).wait_recv()
            onward = _gather_copy(outs, sems, a, k + 1, outs[a].at[origin], origin, 1)
            onward.start()
            passed.append(onward)
    for a in range(len(ins)):
        for k in (1, 3, 5, 7):
            _gather_copy(outs, sems, a, k, ins[a], jnp.bitwise_xor(me, k), 1).wait_recv()
    for cp in direct + passed:
        cp.wait_send()
    for cp in own:
        cp.wait()


def _exchange_plumbing(arrays, gather):
    n = len(arrays)
    hbm = [pl.BlockSpec(memory_space=pltpu.HBM)] * n
    out_shape = [jax.ShapeDtypeStruct((N_DEV,) + (a.shape if gather else a.shape[1:]), a.dtype) for a in arrays]
    sems = [pltpu.SemaphoreType.DMA((n, N_DEV - 1)), pltpu.SemaphoreType.DMA((n, N_DEV - 1)),
            pltpu.SemaphoreType.DMA((n,))]
    return hbm, out_shape, sems


def _exchange(arrays, gather, name):
    n = len(arrays)

    def body(*refs):
        ins, outs, sems = refs[:n], refs[n:2 * n], refs[2 * n:]
        _exchange_start(ins, outs, sems, gather)
        _exchange_wait(ins, outs, sems, gather)

    hbm, out_shape, sems = _exchange_plumbing(arrays, gather)
    return pl.pallas_call(
        body,
        in_specs=hbm,
        out_specs=hbm,
        out_shape=out_shape,
        scratch_shapes=sems,
        compiler_params=pltpu.CompilerParams(has_side_effects=True),
        name=name,
    )(*arrays)


N_CHIP = N_DEV // 2


def _pair_swap(blocks, name):
    n = len(blocks)

    def body(*refs):
        ins, outs, (send_sems, recv_sems) = refs[:n], refs[n:2 * n], refs[2 * n:]
        core = lax.axis_index("c")
        copies = [pltpu.make_async_remote_copy(src_ref=ins[a].at[2 * chip + (1 - core)], dst_ref=outs[a].at[chip],
                                               send_sem=send_sems.at[a, chip], recv_sem=recv_sems.at[a, chip],
                                               device_id=_peer(1), device_id_type=MESH)
                  for a in range(n) for chip in range(N_CHIP)]
        for cp in copies:
            cp.start()
        for cp in copies:
            cp.wait_recv()
        for cp in copies:
            cp.wait_send()

    hbm = [pl.BlockSpec(memory_space=pltpu.HBM)] * n
    return pl.pallas_call(
        body,
        in_specs=hbm,
        out_specs=hbm,
        out_shape=[jax.ShapeDtypeStruct((N_CHIP,) + b.shape[1:], b.dtype) for b in blocks],
        scratch_shapes=[pltpu.SemaphoreType.DMA((n, N_CHIP)), pltpu.SemaphoreType.DMA((n, N_CHIP))],
        compiler_params=pltpu.CompilerParams(has_side_effects=True),
        name=name,
    )(*blocks)


def _chip_scatter(pairs, name):
    n = len(pairs)

    def body(*refs):
        ins, outs, (send_sems, recv_sems, local_sems) = refs[:n], refs[n:2 * n], refs[2 * n:]
        chip = 2 * lax.axis_index("x") + lax.axis_index("y")
        own = [pltpu.make_async_copy(ins[a].at[chip], outs[a].at[chip], local_sems.at[a]) for a in range(n)]
        sends, arrivals = [], []
        for a in range(n):
            for k in range(1, N_CHIP):
                other = jnp.bitwise_xor(chip, k)
                common = dict(src_ref=ins[a].at[other], send_sem=send_sems.at[a, k - 1], recv_sem=recv_sems.at[a, k - 1],
                              device_id=_peer(2 * k), device_id_type=MESH)
                sends.append(pltpu.make_async_remote_copy(dst_ref=outs[a].at[chip], **common))
                arrivals.append(pltpu.make_async_remote_copy(dst_ref=outs[a].at[other], **common))
        for cp in own + sends:
            cp.start()
        for cp in arrivals:
            cp.wait_recv()
        for cp in sends:
            cp.wait_send()
        for cp in own:
            cp.wait()

    hbm = [pl.BlockSpec(memory_space=pltpu.HBM)] * n
    return pl.pallas_call(
        body,
        in_specs=hbm,
        out_specs=hbm,
        out_shape=[jax.ShapeDtypeStruct(p.shape, p.dtype) for p in pairs],
        scratch_shapes=[pltpu.SemaphoreType.DMA((n, N_CHIP - 1)), pltpu.SemaphoreType.DMA((n, N_CHIP - 1)),
                        pltpu.SemaphoreType.DMA((n,))],
        compiler_params=pltpu.CompilerParams(has_side_effects=True),
        name=name,
    )(*pairs)


def _scatter_two_level(blocks, name):
    swapped = _pair_swap(blocks, name + "_pair")
    core = lax.axis_index("c")
    pairs = []
    for b, s in zip(blocks, swapped):
        mine = lax.dynamic_index_in_dim(b.reshape((N_CHIP, 2) + b.shape[1:]), core, axis=1, keepdims=False)
        pairs.append((mine.astype(F32) + s.astype(F32)).astype(b.dtype))
    return _chip_scatter(pairs, name + "_chip")


def _riding(body, n_in, n_out, n_scratch, ride, grid):
    if ride is None:
        return body, [], [], [], []
    arrays, gather = ride
    n = len(arrays)

    def at(step_of):
        hit = pl.program_id(0) == step_of(grid[0])
        for d in range(1, len(grid)):
            hit = jnp.logical_and(hit, pl.program_id(d) == step_of(grid[d]))
        return hit

    def wrapped(*refs):
        ins, rin = refs[:n_in], refs[n_in:n_in + n]
        outs = refs[n_in + n:n_in + n + n_out]
        rout = refs[n_in + n + n_out:n_in + 2 * n + n_out]
        scratch = refs[n_in + 2 * n + n_out:n_in + 2 * n + n_out + n_scratch]
        sems = refs[n_in + 2 * n + n_out + n_scratch:]

        @pl.when(at(lambda size: 0))
        def _():
            _exchange_start(rin, rout, sems, gather)

        body(*ins, *outs, *scratch)

        @pl.when(at(lambda size: size - 1))
        def _():
            _exchange_wait(rin, rout, sems, gather)

    hbm, out_shape, sems = _exchange_plumbing(arrays, gather)
    return wrapped, hbm, hbm, out_shape, sems


def _sum_contributions(c_ref):
    g = c_ref[0].astype(F32)
    for j in range(1, c_ref.shape[0]):
        g = g + c_ref[j].astype(F32)
    return g


def _adamw_update(g, w_ref, m_ref, v_ref, g_ref, d_ref, nm_ref, nv_ref):
    m_new = ADAM_B1 * m_ref[...] + (1.0 - ADAM_B1) * g
    v_new = ADAM_B2 * v_ref[...] + (1.0 - ADAM_B2) * (g * g)
    m_hat = m_new / (1.0 - ADAM_B1 ** ADAM_STEP)
    v_hat = v_new / (1.0 - ADAM_B2 ** ADAM_STEP)
    g_ref[...] = g
    d_ref[...] = -ADAM_LR * (m_hat / (jnp.sqrt(v_hat) + ADAM_EPS) + ADAM_WD * w_ref[...])
    nm_ref[...] = m_new
    nv_ref[...] = v_new


def _adamw_layers(w, m, v, contrib0, contrib1, name):
    _, r, c = w.shape
    br = _pick(r, (256, 128, 64, 32, 16, 8))
    nb = r // br

    def body(w_ref, m_ref, v_ref, c0_ref, c1_ref, g_ref, d_ref, nm_ref, nv_ref):
        g = jnp.where(pl.program_id(0) == 0, _sum_contributions(c0_ref), _sum_contributions(c1_ref))
        _adamw_update(g, w_ref, m_ref, v_ref, g_ref, d_ref, nm_ref, nv_ref)

    spec = pl.BlockSpec((None, br, c), lambda l, i: (l, i, 0))
    return pl.pallas_call(
        body,
        grid=(DEPTH, nb),
        in_specs=[spec, spec, spec,
                  pl.BlockSpec((contrib0.shape[0], br, c), lambda l, i: (0, jnp.where(l == 0, i, nb - 1), 0)),
                  pl.BlockSpec((contrib1.shape[0], br, c), lambda l, i: (0, jnp.where(l == 1, i, 0), 0))],
        out_specs=[spec] * 4,
        out_shape=[jax.ShapeDtypeStruct(w.shape, F32)] * 4,
        compiler_params=_cparams(("arbitrary", "arbitrary")),
        name=name,
    )(w, m, v, contrib0, contrib1)


def _adamw(w, m, v, contrib, name):
    r, c = w.shape
    br = _pick(r, (256, 128, 64, 32, 16, 8))

    def body(w_ref, m_ref, v_ref, c_ref, g_ref, d_ref, nm_ref, nv_ref):
        _adamw_update(_sum_contributions(c_ref), w_ref, m_ref, v_ref, g_ref, d_ref, nm_ref, nv_ref)

    spec = pl.BlockSpec((br, c), lambda i: (i, 0))
    cspec = pl.BlockSpec((contrib.shape[0], br, c), lambda i: (0, i, 0))
    return pl.pallas_call(
        body,
        grid=(r // br,),
        in_specs=[spec, spec, spec, cspec],
        out_specs=[spec] * 4,
        out_shape=[jax.ShapeDtypeStruct((r, c), F32)] * 4,
        compiler_params=_cparams(("parallel",)),
        name=name,
    )(w, m, v, contrib)


def _silu(x):
    return x * jax.nn.sigmoid(x)


def _rope_angles(pos, n_freq):
    inv = ROPE_BASE ** (-jnp.arange(n_freq, dtype=F32) / n_freq)
    return pos[:, None] * inv[None, :]


def _with_ctx_rows(cos, sin):
    return (jnp.concatenate([jnp.ones((CTX_LEN, 128), F32), cos], axis=0),
            jnp.concatenate([jnp.zeros((CTX_LEN, 128), F32), sin], axis=0))


def _rope_tables():
    rows_n = SEQ // GRID_W
    rows = jnp.repeat(jnp.arange(rows_n, dtype=F32), GRID_W)
    cols = jnp.tile(jnp.arange(GRID_W, dtype=F32), rows_n)
    ang_r = _rope_angles(rows, B_HD // 4)
    ang_c = _rope_angles(cols, B_HD // 4)
    cos_b = jnp.tile(jnp.concatenate([jnp.cos(ang_r)] * 2 + [jnp.cos(ang_c)] * 2, axis=1), (1, 2))
    sin_b = jnp.tile(jnp.concatenate([-jnp.sin(ang_r), jnp.sin(ang_r), -jnp.sin(ang_c), jnp.sin(ang_c)], axis=1), (1, 2))
    ang = _rope_angles(jnp.arange(SEQ, dtype=F32), C_HD // 2)
    cos_c = jnp.concatenate([jnp.cos(ang)] * 2, axis=1)
    sin_c = jnp.concatenate([-jnp.sin(ang), jnp.sin(ang)], axis=1)
    return _with_ctx_rows(cos_b, sin_b), _with_ctx_rows(cos_c, sin_c)


def _halves(a):
    return a[:4], a[4:]


def _delta_gates(ab, a_log, dt_bias):
    beta = jax.nn.sigmoid(ab[:, :8])
    g = -jnp.exp(a_log)[None, :] * jax.nn.softplus(ab[:, 8:] + dt_bias[None, :])
    gch = g.reshape(N_CHUNK, CHUNK, 8)
    tri = jnp.tril(jnp.ones((CHUNK, CHUNK), F32))
    fwd = jnp.einsum("ij,cjh->cih", tri, gch[..., :4], precision=HIGHEST)
    bwd = jnp.einsum("ji,cjh->cih", tri, gch[..., 4:], precision=HIGHEST)
    gc = jnp.concatenate([fwd, bwd], axis=-1)
    gl = jnp.sum(gch, axis=1)
    rows = lambda a: _halves(a.transpose(2, 0, 1)[:, :, None, :])
    return rows(beta.reshape(N_CHUNK, CHUNK, 8)), rows(gc), _halves(gl.T[:, :, None, None])


def _ret_consts(c_decay):
    lg = jax.nn.log_sigmoid(c_decay)
    idx = jnp.arange(RET_CHUNK, dtype=F32)
    diff = idx[:, None] - idx[None, :]
    lgf, lgb = lg[:4, None, None], lg[4:, None, None]
    dm = jnp.concatenate([jnp.exp(jnp.where(diff >= 0, diff * lgf, -jnp.inf)),
                          jnp.exp(jnp.where(diff <= 0, -diff * lgb, -jnp.inf))], axis=0)
    qs = jnp.concatenate([jnp.exp((idx + 1.0)[None, :] * lg[:4, None]),
                          jnp.exp((RET_CHUNK - idx)[None, :] * lg[4:, None])], axis=0)[:, :, None]
    ks = jnp.concatenate([jnp.exp((RET_CHUNK - 1.0 - idx)[None, :] * lg[:4, None]),
                          jnp.exp(idx[None, :] * lg[4:, None])], axis=0)[:, :, None]
    return dm, qs, ks, jnp.exp(RET_CHUNK * lg)[:, None, None]


A_PIECES = ((0, True, A_DK ** -0.5, "a_q"), (1, True, 1.0, "a_k"), (2, False, 1.0, "a_v"))
B_ROPE_COLS = [C_BQ // 512, C_BKV // 256, 0, 0]
C_ROPE_COLS = [C_CQ // 512, C_CK // 512, 0, 0]
MERGE_COLS = [0, 0, 0, C_MERGE // 1024, C_MERGE // 1024 + 1, C_MERGE // 1024 + 2]


def _conv8(conv_w):
    return jnp.pad(conv_w, ((0, 8 - A_CONV), (0, 0)))


W_IN_TILES = {"nn": (2176, 512, 1024), "nt": (1088, 1024, 2176), "db": (1024, 512, ROWS)}


def _core_forward(h, w16, p, rides):
    res = _matmul(h, w16, "w_in", "nn", W_IN_TILES["nn"], ride=rides.get("w_in"))
    proj, rode = (res[0], {"w_in": res[1:]}) if "w_in" in rides else (res, {})
    wb = p["w_branch"] if "w_in" not in rides else _unshard_layer("w_branch", rode["w_in"][0])
    (cos_b, sin_b), (cos_c, sin_c) = _rope_tables()
    conv8 = _conv8(p["a_conv_w"])
    q, k, v = [_a_prep_fwd(proj, conv8, col, nrm, scl, nm) for col, nrm, scl, nm in A_PIECES]
    gates = _delta_gates(proj[:, C_AB:C_AB + 16], p["a_log"], p["a_dt_bias"])
    res = _delta_fwd_call(q, k, v, *gates, ride=rides.get("delta"))
    (of, orv, ssf, ssr, tsf, tsr), rode["delta"] = res[:6], res[6:]
    (y_a,) = _a_out.fwd([(of, orv), proj], [p["a_norm_w"][None, :]], [0, C_AZ // 512])

    qb, kvb = _b_rope.fwd([proj, proj, cos_b, sin_b], [], B_ROPE_COLS)
    res = _attn_fwd_call(qb, kvb, p["b_sink"], ride=rides.get("attn"))
    ob, rode["attn"] = res[0], res[1:]
    (y_b,) = _b_out.fwd([ob, proj], [], [0, C_BZ // 512])

    qc, kc = _c_rope.fwd([proj, proj, cos_c, sin_c], [], C_ROPE_COLS)
    res = _ret_fwd_call(qc, kc, proj, C_CV // 512, *_ret_consts(p["c_decay"]), ride=rides.get("ret"))
    (cf, cr, csf, csr), rode["ret"] = res[:4], res[4:]
    (y_c,) = _c_out.fwd([(cf, cr), proj], [p["c_norm_w"][None, :]], [0, C_CZ // 512])

    (merged,) = _branch_merge.fwd([y_a, y_b, y_c, proj, proj, proj], [wb[0], wb[1], wb[2]], MERGE_COLS)
    saved = dict(proj=proj, q=q, k=k, v=v, of=of, orv=orv, ss=(ssf, ssr), ts=(tsf, tsr), qb=qb, kvb=kvb, ob=ob,
                 qc=qc, kc=kc, cf=cf, cr=cr, cs=(csf, csr), y=(y_a, y_b, y_c), wb=wb)
    return merged, saved, rode


def _core_backward(h, w16, p, s, dmerged, rides, branch_rides_in_attn=False):
    proj, wb = s["proj"], s["wb"]
    (cos_b, sin_b), (cos_c, sin_c) = _rope_tables()
    conv8 = _conv8(p["a_conv_w"])
    y_a, y_b, y_c = s["y"]
    rode = {}

    (*dy, dma, dmb, dmc), dwb = _branch_merge.bwd([y_a, y_b, y_c, proj, proj, proj], [wb[0], wb[1], wb[2]],
                                                   [dmerged], MERGE_COLS, bf16_rows=(3, 4, 5))
    dwb = jnp.stack(dwb)

    consts, consts_vjp = jax.vjp(_ret_consts, p["c_decay"])
    (do_c, dcz), (dcnw,) = _c_out.bwd([(s["cf"], s["cr"]), proj], [p["c_norm_w"][None, :]], [dy[2]],
                                      [0, C_CZ // 512], bf16_rows=(1,))
    g = _ret_bwd_call(s["qc"], s["kc"], proj, C_CV // 512, *consts, s["cs"], do_c, ride=rides.get("ret"))
    rode["ret"] = g[10:]
    (dcq, dck), _ = _c_rope.bwd([proj, proj, cos_c, sin_c], [], [(g[0], g[1]), (g[2], g[3])], C_ROPE_COLS,
                                bf16_rows=(0, 1))
    dcv = (g[4] + g[5]).astype(BF16)
    (dc_decay,) = consts_vjp(tuple(g[6:10]))

    (dob, dbz), _ = _b_out.bwd([s["ob"], proj], [], [dy[1]], [0, C_BZ // 512], bf16_rows=(1,))
    attn_ride = rides.get("attn")
    if branch_rides_in_attn:
        attn_ride = (list(attn_ride[0]) + [_reshard_layer("w_branch", dwb).astype(BF16)], attn_ride[1])
    res = _attn_bwd_call(s["qb"], s["kvb"], p["b_sink"], dob, ride=attn_ride)
    (dqb, dkvb, dsink), rode["attn"] = res[:3], res[3:]
    (dbq, dbkv), _ = _b_rope.bwd([proj, proj, cos_b, sin_b], [], [dqb, dkvb], B_ROPE_COLS, bf16_rows=(0, 1))

    ab = proj[:, C_AB:C_AB + 16]
    gates, gates_vjp = jax.vjp(_delta_gates, ab, p["a_log"], p["a_dt_bias"])
    (do_a, daz), (danw,) = _a_out.bwd([(s["of"], s["orv"]), proj], [p["a_norm_w"][None, :]], [dy[0]],
                                      [0, C_AZ // 512], bf16_rows=(1,))
    g = _delta_bwd_call(s["q"], s["k"], s["v"], *gates, s["ss"], s["ts"], do_a, ride=rides.get("delta"))
    rode["delta"] = g[12:]
    dgates = ((g[6], g[7]), (g[8], g[9]), (g[10], g[11]))
    dab, da_log, ddt = gates_vjp(dgates)
    dpre, dconv = [], []
    for (col, nrm, scl, nm), df, dr in zip(A_PIECES, (g[0], g[2], g[4]), (g[1], g[3], g[5])):
        dx, dw = _a_prep_bwd(proj, conv8, col, nrm, scl, df, dr, nm)
        dpre.append(dx)
        dconv.append(dw[:A_CONV])

    dproj = jnp.concatenate(dpre + [daz, dbq, dbz, dcq, dck, dcv, dcz, dma, dmb, dmc, dbkv,
                                    jnp.pad(dab, ((0, 0), (0, IN_PAD - C_AB - 16))).astype(BF16)], axis=1)
    dh = _matmul(dproj, w16, "w_in_da", "nt", W_IN_TILES["nt"])
    dw = _matmul(h.T.astype(BF16), dproj, "w_in_db", "nn", W_IN_TILES["db"])
    dp = dict(a_conv_w=jnp.concatenate(dconv, axis=1), a_log=da_log, a_dt_bias=ddt, a_norm_w=danw[0],
              b_sink=dsink[:, 0], c_decay=dc_decay, c_norm_w=dcnw[0], w_branch=dwb)
    return dh, dw, dp, rode


CORE_PARAMS = ("a_conv_w", "a_log", "a_dt_bias", "a_norm_w", "b_sink", "c_decay", "c_norm_w", "w_branch")


W_IN_SHARD = IN_WIDTH // N_DEV
W_IN_RUNS = ((0, 2048, 0), (2064, 512, C_BQ), (2832, 512, C_BZ), (3344, 5120, C_CQ), (2576, 256, C_BKV),
             (2048, 16, C_AB))


def _shard_overlap(start, width, j):
    lo, hi = max(start, j * W_IN_SHARD), min(start + width, (j + 1) * W_IN_SHARD)
    return (lo, hi) if lo < hi else None


def _w_in_from_shards(g):
    parts = []
    for start, width, _ in W_IN_RUNS:
        for j in range(N_DEV):
            span = _shard_overlap(start, width, j)
            if span:
                parts.append(g[j, :, span[0] - j * W_IN_SHARD:span[1] - j * W_IN_SHARD])
    parts.append(jnp.zeros((D_MODEL, IN_PAD - IN_WIDTH), g.dtype))
    return jnp.concatenate(parts, axis=1)


def _w_in_blocks(dw):
    blocks = []
    for j in range(N_DEV):
        parts = []
        for start, width, pad in sorted(W_IN_RUNS):
            span = _shard_overlap(start, width, j)
            if span:
                parts.append(dw[:, pad + span[0] - start:pad + span[1] - start])
        blocks.append(jnp.concatenate(parts, axis=1))
    return jnp.stack(blocks)


LAYER_SHARDED = ("w_ada", "w_in", "w_branch", "w_out")


def _unshard_layer(name, g):
    if name == "w_branch":
        return g.transpose(1, 2, 0, 3).reshape(3, BR_WIDTH, D_MODEL)
    if name == "w_out":
        return g.reshape(D_MODEL, D_MODEL)
    return g.transpose(1, 0, 2).reshape(D_MODEL, -1)


def _reshard_layer(name, w):
    if name == "w_branch":
        return w.reshape(3, BR_WIDTH, N_DEV, D_MODEL // N_DEV).transpose(2, 0, 1, 3)
    if name == "w_out":
        return w.reshape(N_DEV, D_MODEL // N_DEV, D_MODEL)
    return w.reshape(D_MODEL, N_DEV, -1).transpose(1, 0, 2)


def _layer_weights(gathered):
    out = {n: _unshard_layer(n, g) for n, g in gathered.items() if n != "w_in"}
    out["w_in16"] = _w_in_from_shards(gathered["w_in"])
    return out


def _grad_blocks(name, g):
    return (_w_in_blocks(g) if name == "w_in" else _reshard_layer(name, g)).astype(BF16)


def _forward_backward(small, layer0, shards0, shards1, x, c, ctx, loss_target):
    c_ctx = small["c_ctx"]
    sc16 = jnp.zeros((16, D_MODEL), F32).at[0].set(_silu(c)).at[1].set(_silu(c_ctx))
    xs = jnp.concatenate([ctx, x], axis=0)
    weights = [dict(layer0), None]
    layers = []
    for l in range(DEPTH):
        wl = weights[l]
        mod16 = _matmul(sc16, wl["w_ada"], "ada") + small["b_ada"][l][None, :]
        mod_cx = jnp.stack([mod16[1], mod16[0]])
        shift, scale, gate = jnp.split(mod_cx, 3, axis=1)
        nw = small["norm_w"][l][None, :]
        (h,) = _norm_mod.fwd([xs], [nw, shift, scale])
        p = {n: small[n][l] for n in CORE_PARAMS if n != "w_branch"}
        p["w_branch"] = wl.get("w_branch")
        rides = {}
        if l == 0:
            rides = {"w_in": ([shards0["w_branch"], shards0["w_out"]], True), "delta": ([shards1["w_in"]], True),
                     "attn": ([shards1["w_ada"]], True), "ret": ([shards1["w_branch"], shards1["w_out"]], True)}
        merged, saved, rode = _core_forward(h, wl["w_in16"], p, rides)
        if l == 0:
            wl["w_out"] = _unshard_layer("w_out", rode["w_in"][1])
            weights[1] = _layer_weights(dict(w_in=rode["delta"][0], w_ada=rode["attn"][0],
                                             w_branch=rode["ret"][0], w_out=rode["ret"][1]))
        (xs_next,) = _out_residual.fwd([xs, merged], [wl["w_out"], gate])
        layers.append(dict(xs=xs, h=h, p=p, saved=saved, merged=merged, gate=gate, nw=nw, shift=shift, scale=scale))
        xs = xs_next
    fw = small["final_norm_w"][None, :]
    xs = xs[CTX_LEN:]
    (per_row,) = _loss_rows.fwd([xs, loss_target], [fw])
    loss = jnp.sum(per_row[:, 0])

    d_per_row = jnp.zeros((SEQ, 128), F32).at[:, 0].set(1.0)
    (dxs,), (dfw,) = _loss_rows.bwd([xs, loss_target], [fw], [d_per_row])
    dxs = jnp.pad(dxs, ((CTX_LEN, 0), (0, 0)))
    small_names = tuple(n for n in CORE_PARAMS if n != "w_branch") + ("b_ada", "norm_w")
    dsmall = {n: [None] * DEPTH for n in small_names}
    dlayer = [None] * DEPTH
    contrib0 = contrib1 = None
    dsc16 = jnp.zeros((16, D_MODEL), F32)
    for l in reversed(range(DEPTH)):
        s, wl = layers[l], weights[l]
        (dres, dmerged), (dw_out, dgate) = _out_residual.bwd([s["xs"], s["merged"]], [wl["w_out"], s["gate"]], [dxs])
        rides = {}
        if l == 0:
            blocks1 = {n: _grad_blocks(n, g) for n, g in dlayer[1].items()}
            rides = {"ret": ([blocks1["w_branch"], blocks1["w_out"]], False),
                     "attn": ([_reshard_layer("w_out", dw_out).astype(BF16), blocks1["w_ada"]], False),
                     "delta": ([blocks1["w_in"]], False)}
        dh, dw_in, dp, rode = _core_backward(s["h"], wl["w_in16"], s["p"], s["saved"], dmerged, rides,
                                             branch_rides_in_attn=(l == 0))
        if l == 0:
            contrib1 = dict(w_in=rode["delta"][0], w_ada=rode["attn"][1], w_branch=rode["ret"][0],
                            w_out=rode["ret"][1])
            contrib0 = dict(w_out=rode["attn"][0], w_branch=rode["attn"][2])
        (dxn,), (dnw, dshift, dscale) = _norm_mod.bwd([s["xs"]], [s["nw"], s["shift"], s["scale"]], [dh])
        dxs = dres + dxn
        dmod_cx = jnp.concatenate([dshift, dscale, dgate], axis=1)
        dmod16 = jnp.zeros((16, 3 * D_MODEL), F32).at[0].set(dmod_cx[1]).at[1].set(dmod_cx[0])
        dsc16 = dsc16 + _matmul(dmod16, wl["w_ada"], "ada_da", "nt")
        dlayer[l] = dict(w_ada=_matmul(sc16, dmod16, "ada_db", "tn"), w_in=dw_in,
                         w_branch=dp["w_branch"], w_out=dw_out)
        for n in small_names:
            if n in dp:
                dsmall[n][l] = dp[n]
        dsmall["norm_w"][l] = dnw[0]
        dsmall["b_ada"][l] = dmod_cx[0] + dmod_cx[1]
    gsmall = {n: jnp.stack(v) for n, v in dsmall.items()}
    gsmall["final_norm_w"] = dfw[0]
    sig = jax.nn.sigmoid(c_ctx)
    gsmall["c_ctx"] = dsc16[1] * sig * (1.0 + c_ctx * (1.0 - sig))
    return loss, dxs[CTX_LEN:], gsmall, {n: dlayer[0][n] for n in ("w_ada", "w_in")}, contrib0, contrib1


SMALL = ("c_ctx", "b_ada", "norm_w", "a_log", "a_dt_bias", "a_norm_w", "b_sink", "c_decay", "c_norm_w",
         "final_norm_w")
WEIGHTS = ("c_ctx", "w_ada", "b_ada", "norm_w", "w_in", "a_conv_w", "a_log", "a_dt_bias", "a_norm_w", "b_sink",
           "c_decay", "c_norm_w", "w_branch", "w_out", "final_norm_w")
SMALL_PACK = 12288


def _unshard_conv(g):
    return g.transpose(1, 2, 0, 3).reshape(DEPTH, A_CONV, 3 * A_WIDTH)


def _reshard_conv(w):
    return w.reshape(DEPTH, A_CONV, N_DEV, 3 * A_WIDTH // N_DEV).transpose(2, 0, 1, 3)


def _pack_small(tree):
    flat = jnp.concatenate([tree[n].reshape(-1) for n in SMALL])
    return jnp.pad(flat, (0, SMALL_PACK - flat.shape[0])).reshape(SMALL_PACK // 128, 128)


def _unpack_small(packed, like):
    flat = packed.reshape(-1)
    out, off = {}, 0
    for n in SMALL:
        size = math.prod(like[n].shape)
        out[n] = flat[off:off + size].reshape(like[n].shape)
        off += size
    return out


def kernel(x, c, ctx, c_ctx, w_ada, b_ada, norm_w, w_in, a_conv_w, a_log, a_dt_bias, a_norm_w, b_sink, c_decay, c_norm_w, w_branch, w_out, final_norm_w, loss_target, m_c_ctx, m_w_ada, m_b_ada, m_norm_w, m_w_in, m_a_conv_w, m_a_log, m_a_dt_bias, m_a_norm_w, m_b_sink, m_c_decay, m_c_norm_w, m_w_branch, m_w_out, m_final_norm_w, v_c_ctx, v_w_ada, v_b_ada, v_norm_w, v_w_in, v_a_conv_w, v_a_log, v_a_dt_bias, v_a_norm_w, v_b_sink, v_c_decay, v_c_norm_w, v_w_branch, v_w_out, v_final_norm_w):
    w = dict(c_ctx=c_ctx, w_ada=w_ada, b_ada=b_ada, norm_w=norm_w, w_in=w_in, a_conv_w=a_conv_w, a_log=a_log,
             a_dt_bias=a_dt_bias, a_norm_w=a_norm_w, b_sink=b_sink, c_decay=c_decay, c_norm_w=c_norm_w,
             w_branch=w_branch, w_out=w_out, final_norm_w=final_norm_w)
    m = dict(c_ctx=m_c_ctx, w_ada=m_w_ada, b_ada=m_b_ada, norm_w=m_norm_w, w_in=m_w_in, a_conv_w=m_a_conv_w,
             a_log=m_a_log, a_dt_bias=m_a_dt_bias, a_norm_w=m_a_norm_w, b_sink=m_b_sink, c_decay=m_c_decay,
             c_norm_w=m_c_norm_w, w_branch=m_w_branch, w_out=m_w_out, final_norm_w=m_final_norm_w)
    v = dict(c_ctx=v_c_ctx, w_ada=v_w_ada, b_ada=v_b_ada, norm_w=v_norm_w, w_in=v_w_in, a_conv_w=v_a_conv_w,
             a_log=v_a_log, a_dt_bias=v_a_dt_bias, a_norm_w=v_a_norm_w, b_sink=v_b_sink, c_decay=v_c_decay,
             c_norm_w=v_c_norm_w, w_branch=v_w_branch, w_out=v_w_out, final_norm_w=v_final_norm_w)

    shards = {n: w[n].astype(BF16) for n in LAYER_SHARDED}
    first = _exchange([shards["w_ada"][0], shards["w_in"][0], w["a_conv_w"]], True, "gather_layer0")
    layer0 = _layer_weights(dict(w_ada=first[0], w_in=first[1]))
    small_w = {n: w[n] for n in SMALL}
    small_w["a_conv_w"] = _unshard_conv(first[2])
    loss, gx, gw, glayer0, contrib0, contrib1 = _forward_backward(
        small_w, layer0, {n: shards[n][0] for n in ("w_branch", "w_out")}, {n: shards[n][1] for n in LAYER_SHARDED},
        x[0], c[0], ctx[0], loss_target[0])
    loss = lax.psum(loss, ("x", "y", "c"))

    last = _scatter_two_level([_reshard_layer("w_ada", glayer0["w_ada"]).astype(BF16),
                               _grad_blocks("w_in", glayer0["w_in"]), _reshard_conv(gw["a_conv_w"])],
                              "scatter_layer0")
    contrib0["w_ada"], contrib0["w_in"] = last[0], last[1]
    small = _exchange([_pack_small(gw)], True, "gather_small_grads")[0]

    grad, delta, new_m, new_v = {}, {}, {}, {}
    for n in LAYER_SHARDED:
        shp = w[n].shape
        per_layer = (math.prod(shp[1:-1]), shp[-1])
        outs = _adamw_layers(*[a.reshape((DEPTH,) + per_layer) for a in (w[n], m[n], v[n])],
                             *[cb.reshape(cb.shape[:1] + per_layer) for cb in (contrib0[n], contrib1[n])], "adamw_" + n)
        grad[n], delta[n], new_m[n], new_v[n] = [o.reshape(shp) for o in outs]
    shp = a_conv_w.shape
    two_d = (math.prod(shp[:-1]), shp[-1])
    outs = _adamw(*[a.reshape(two_d) for a in (a_conv_w, m_a_conv_w, v_a_conv_w)],
                  last[2].reshape(last[2].shape[:1] + two_d), "adamw_a_conv_w")
    grad["a_conv_w"], delta["a_conv_w"], new_m["a_conv_w"], new_v["a_conv_w"] = [o.reshape(shp) for o in outs]
    outs = _adamw(_pack_small(w), _pack_small(m), _pack_small(v), small, "adamw_small")
    for tree, packed in zip((grad, delta, new_m, new_v), outs):
        tree.update(_unpack_small(packed, w))

    return (loss, gx[None], *[grad[n] for n in WEIGHTS], *[delta[n] for n in WEIGHTS],
            *[new_m[n] for n in WEIGHTS], *[new_v[n] for n in WEIGHTS])
```

```python
import functools
import math

import jax
import jax.numpy as jnp
from jax import lax
from jax.experimental import pallas as pl
from jax.experimental.pallas import tpu as pltpu

F32 = jnp.float32
BF16 = jnp.bfloat16
HIGHEST = lax.Precision.HIGHEST

D_MODEL = 1024
SEQ = 4096
DEPTH = 2
GRID_W = 64
CTX_LEN = 256
EPS = 1e-6
ROPE_BASE = 10000.0
BR_WIDTH = D_MODEL // 2
A_DK = 128
A_HEADS = 4
A_WIDTH = 512
A_CONV = 5
B_HD = 64
B_Q_HEADS = 8
B_KV_HEADS = 2
WINDOW = 128
B_BLOCK = 128
C_HD = 128
C_HEADS = 4
C_WIDTH = 512
CHUNK = 128
RET_CHUNK = 256
ADAM_LR = 0.001
ADAM_B1 = 0.9
ADAM_B2 = 0.999
ADAM_EPS = 1e-08
ADAM_WD = 0.01
ADAM_STEP = 10

N_DEV = 8
ROWS = CTX_LEN + SEQ
N_CHUNK = ROWS // CHUNK
IN_WIDTH = 8464
IN_PAD = 8704
NEG = -1e30

VMEM_LIMIT = 48 * 1024 * 1024
MESH = pl.DeviceIdType.MESH

C_AQ, C_AK, C_AV, C_AZ, C_BQ, C_BZ, C_CQ, C_CK, C_CV, C_CZ = (i * 512 for i in range(10))
C_MERGE = 5120
C_BKV = 8192
C_AB = 8448


def _cparams(sem=None):
    if sem is None:
        return pltpu.CompilerParams(vmem_limit_bytes=VMEM_LIMIT)
    return pltpu.CompilerParams(dimension_semantics=sem, vmem_limit_bytes=VMEM_LIMIT)


def _dg(a, b, ca, cb, prec=None):
    return lax.dot_general(a, b, (((ca,), (cb,)), ((), ())), preferred_element_type=F32, precision=prec)


@functools.partial(jax.custom_vjp, nondiff_argnums=(2, 3))
def _bdot(a, b, ca, cb):
    return _dg(a.astype(BF16), b.astype(BF16), ca, cb)


def _bdot_fwd(a, b, ca, cb):
    return _bdot(a, b, ca, cb), (a, b)


def _bdot_bwd(ca, cb, res, ct):
    a, b = res
    da = _bdot(ct, b, 1, 1 - cb) if ca == 1 else _bdot(b, ct, 1 - cb, 1)
    db = _bdot(a, ct, 1 - ca, 0) if cb == 0 else _bdot(ct, a, 0, 1 - ca)
    return da, db


_bdot.defvjp(_bdot_fwd, _bdot_bwd)


def _hdot(a, b):
    return _dg(a, b, 1, 0, lax.Precision.HIGH)


def _k_silu(x):
    return x / (1.0 + jnp.exp(-x))


def _k_sigmoid(x):
    return 1.0 / (1.0 + jnp.exp(-x))


@jax.custom_vjp
def _swap64(x):
    return pltpu.roll(x, 64, 1)


_swap64.defvjp(lambda x: (pltpu.roll(x, 64, 1), None), lambda _, ct: (pltpu.roll(ct, 64, 1),))


def _swap16_impl(x):
    lane = lax.broadcasted_iota(jnp.int32, x.shape, 1)
    return jnp.where((lane & 16) == 0, pltpu.roll(x, 112, 1), pltpu.roll(x, 16, 1))


@jax.custom_vjp
def _swap16(x):
    return _swap16_impl(x)


_swap16.defvjp(lambda x: (_swap16_impl(x), None), lambda _, ct: (_swap16_impl(ct),))


def _pick(dim, prefs):
    for p in prefs:
        if dim % p == 0:
            return p
    return dim


def _matmul(a, b, name, mode="nn", tiles=None, ride=None):
    ca, cb = {"nn": (1, 0), "nt": (1, 1), "tn": (0, 0)}[mode]
    m, k = a.shape[1 - ca], a.shape[ca]
    n = b.shape[1 - cb]
    if tiles is None:
        tiles = (_pick(m, (1088, 1024, 512, 256, 128)), _pick(n, (512, 256, 128)),
                 _pick(k, (1088, 1024, 512, 256, 128) if mode == "tn" else (2176, 2048, 1024, 512, 256, 128)))
    tm, tn, tk = tiles
    nk = k // tk
    a_spec = (pl.BlockSpec((tm, tk), lambda i, j, kk: (i, kk)) if ca == 1
              else pl.BlockSpec((tk, tm), lambda i, j, kk: (kk, i)))
    b_spec = (pl.BlockSpec((tk, tn), lambda i, j, kk: (kk, j)) if cb == 0
              else pl.BlockSpec((tn, tk), lambda i, j, kk: (j, kk)))

    def body(a_ref, b_ref, o_ref):
        part = _dg(a_ref[...].astype(BF16), b_ref[...].astype(BF16), ca, cb)
        if nk == 1:
            o_ref[...] = part
        else:
            kk = pl.program_id(2)

            @pl.when(kk == 0)
            def _():
                o_ref[...] = part

            @pl.when(kk > 0)
            def _():
                o_ref[...] += part

    grid = (m // tm, n // tn, nk)
    if ride is None:
        return pl.pallas_call(
            body,
            grid=grid,
            in_specs=[a_spec, b_spec],
            out_specs=pl.BlockSpec((tm, tn), lambda i, j, kk: (i, j)),
            out_shape=jax.ShapeDtypeStruct((m, n), F32),
            compiler_params=_cparams(("parallel", "parallel", "arbitrary")),
            name=name,
        )(a, b)
    body, r_in, r_out, r_shape, r_scratch = _riding(body, 2, 1, 0, ride, grid)
    return pl.pallas_call(
        body,
        grid=grid,
        in_specs=[a_spec, b_spec] + r_in,
        out_specs=[pl.BlockSpec((tm, tn), lambda i, j, kk: (i, j))] + r_out,
        out_shape=[jax.ShapeDtypeStruct((m, n), F32)] + r_shape,
        scratch_shapes=r_scratch,
        compiler_params=_cparams(("arbitrary", "arbitrary", "arbitrary")),
        name=name,
    )(a, b, *ride[0])


ROW_BLOCK = 256
ROW_VMEM_BUDGET = 16 * 1024 * 1024


def _pieces(val, pw):
    return [val[:, j * pw:(j + 1) * pw] for j in range(val.shape[1] // pw)]


def _flat(groups):
    arrays, sizes = [], []
    for g in groups:
        g = g if isinstance(g, (tuple, list)) else (g,)
        arrays += list(g)
        sizes.append(len(g))
    return arrays, sizes


def _regroup(refs, sizes):
    out, at = [], 0
    for n in sizes:
        val = refs[at][...]
        for r in refs[at + 1:at + n]:
            val = val + r[...]
        out.append(val)
        at += n
    return out


class _Rowwise:
    def __init__(self, fn, name, row_wpw, par_pw, out_wpw, n_diff=None):
        self.fn, self.name, self.row_wpw, self.par_pw, self.out_wpw = fn, name, row_wpw, par_pw, out_wpw
        self.n_diff = len(row_wpw) if n_diff is None else n_diff

        @jax.custom_vjp
        def call(rows, params):
            return self.fwd(rows, params)

        def call_fwd(rows, params):
            return self.fwd(rows, params), (rows, params)

        def call_bwd(res, douts):
            return self.bwd(res[0], res[1], douts)

        call.defvjp(call_fwd, call_bwd)
        self.call = call

    def _load(self, row_vals, par_refs, br, with_ctx):
        row = pl.program_id(0) * br + lax.broadcasted_iota(jnp.int32, (br, 1), 0)
        is_ctx = (row < (CTX_LEN if with_ctx else 0)).astype(F32)
        rows = [_pieces(v, pw) for v, (_, pw) in zip(row_vals, self.row_wpw)]
        pars = []
        for p, pw in zip(par_refs, self.par_pw):
            val = p[...].astype(F32)
            if p.shape[0] == 2:
                val = is_ctx * val[0:1, :] + (1.0 - is_ctx) * val[1:2, :]
            pars.append(_pieces(val, pw))
        return rows, pars, is_ctx

    def _block_rows(self, n_rows, widths):
        for br in (1088, 1024, 544, 512, 272):
            if n_rows % br == 0 and 2 * 4 * br * sum(widths) <= ROW_VMEM_BUDGET:
                return br
        return ROW_BLOCK

    def _row_specs(self, br, sizes, cols):
        out = []
        for (w, _), n, c in zip(self.row_wpw, sizes, cols):
            out += [pl.BlockSpec((br, w), lambda i, c=c: (i, c))] * n
        return out

    def fwd(self, rows, params, cols=None):
        arrays, sizes = _flat(rows)
        cols = cols or [0] * len(rows)
        n_rows = arrays[0].shape[0]
        n_in = len(arrays)
        br = self._block_rows(n_rows, [w for (w, _), n in zip(self.row_wpw, sizes) for _ in range(n)]
                              + [w for w, _ in self.out_wpw])

        def body(*refs):
            r, p, _ = self._load(_regroup(refs[:n_in], sizes), refs[n_in:n_in + len(params)], br, n_rows == ROWS)
            for o_ref, pieces, (_, pw) in zip(refs[n_in + len(params):], self.fn(r, p), self.out_wpw):
                for j, piece in enumerate(pieces):
                    o_ref[:, j * pw:(j + 1) * pw] = piece

        return pl.pallas_call(
            body,
            grid=(n_rows // br,),
            in_specs=self._row_specs(br, sizes, cols) + [pl.BlockSpec(p.shape, lambda i: (0, 0)) for p in params],
            out_specs=[pl.BlockSpec((br, w), lambda i: (i, 0)) for w, _ in self.out_wpw],
            out_shape=[jax.ShapeDtypeStruct((n_rows, w), F32) for w, _ in self.out_wpw],
            compiler_params=_cparams(("parallel",)),
            name=self.name + "_fwd",
        )(*arrays, *params)

    def bwd(self, rows, params, douts, cols=None, bf16_rows=()):
        arrays, sizes = _flat(rows)
        darrays, dsizes = _flat(douts)
        cols = cols or [0] * len(rows)
        n_rows = arrays[0].shape[0]
        n_in, n_par, n_dout, n_diff = len(arrays), len(params), len(darrays), self.n_diff
        br = self._block_rows(n_rows, [w for (w, _), n in zip(self.row_wpw, sizes) for _ in range(n)]
                              + [w for (w, _), n in zip(self.out_wpw, dsizes) for _ in range(n)]
                              + [w for w, _ in self.row_wpw[:n_diff]])

        def body(*refs):
            par_refs = refs[n_in:n_in + n_par]
            dout_refs = refs[n_in + n_par:n_in + n_par + n_dout]
            drow_refs = refs[n_in + n_par + n_dout:n_in + n_par + n_dout + n_diff]
            dpar_refs = refs[n_in + n_par + n_dout + n_diff:]

            @pl.when(pl.program_id(0) == 0)
            def _():
                for d in dpar_refs:
                    d[...] = jnp.zeros_like(d)

            r, p, is_ctx = self._load(_regroup(refs[:n_in], sizes), par_refs, br, n_rows == ROWS)
            cts = [_pieces(d, pw) for d, (_, pw) in zip(_regroup(dout_refs, dsizes), self.out_wpw)]
            fixed = r[n_diff:]
            _, vjp = jax.vjp(lambda rd, pp: self.fn(rd + fixed, pp), r[:n_diff], p)
            dr, dp = vjp(cts)
            for d_ref, pieces, (_, pw) in zip(drow_refs, dr, self.row_wpw):
                for j, piece in enumerate(pieces):
                    d_ref[:, j * pw:(j + 1) * pw] = piece.astype(d_ref.dtype)
            for d_ref, pieces, pw in zip(dpar_refs, dp, self.par_pw):
                for j, piece in enumerate(pieces):
                    lanes = slice(j * pw, (j + 1) * pw)
                    if d_ref.shape[0] != 2:
                        d_ref[:, lanes] += piece
                    else:
                        d_ref[0:1, lanes] += jnp.sum(is_ctx * piece, axis=0, keepdims=True)
                        d_ref[1:2, lanes] += jnp.sum((1.0 - is_ctx) * piece, axis=0, keepdims=True)

        par_specs = [pl.BlockSpec(p.shape, lambda i: (0, 0)) for p in params]
        dout_specs = []
        for (w, _), n in zip(self.out_wpw, dsizes):
            dout_specs += [pl.BlockSpec((br, w), lambda i: (i, 0))] * n
        drow_w = [w for w, _ in self.row_wpw[:n_diff]]
        g = pl.pallas_call(
            body,
            grid=(n_rows // br,),
            in_specs=self._row_specs(br, sizes, cols) + par_specs + dout_specs,
            out_specs=[pl.BlockSpec((br, w), lambda i: (i, 0)) for w in drow_w] + par_specs,
            out_shape=[jax.ShapeDtypeStruct((n_rows, w), BF16 if a in bf16_rows else F32) for a, w in enumerate(drow_w)]
            + [jax.ShapeDtypeStruct(p.shape, F32) for p in params],
            compiler_params=_cparams(("arbitrary",)),
            name=self.name + "_bwd",
        )(*arrays, *params, *darrays)
        return list(g[:n_diff]), list(g[n_diff:])


def _fn_norm_mod(rows, pars):
    (x,), (nw,), (shift,), (scale,) = rows[0], pars[0], pars[1], pars[2]
    y = x * lax.rsqrt(jnp.mean(x * x, axis=-1, keepdims=True) + EPS) * nw
    return [[y * (1.0 + scale) + shift]]


def _fn_head_rms_gate(rows, pars):
    (w,) = pars[0]
    return [[o * lax.rsqrt(jnp.mean(o * o, axis=-1, keepdims=True) + EPS) * w * _k_silu(z)
             for o, z in zip(rows[0], rows[1])]]


def _fn_group_norm_gate(rows, pars):
    out = []
    for o, z, w in zip(rows[0], rows[1], pars[0]):
        mu = jnp.mean(o, axis=-1, keepdims=True)
        var = jnp.mean(jnp.square(o - mu), axis=-1, keepdims=True)
        out.append((o - mu) * lax.rsqrt(var + EPS) * w * _k_silu(z))
    return [out]


def _fn_gate(rows, pars):
    return [[o * _k_silu(z) for o, z in zip(rows[0], rows[1])]]


def _fn_branch_merge(rows, pars):
    (ya,), (yb,), (yc,), (ma,), (mb,), (mc,) = rows
    (wa,), (wb,), (wc,) = pars
    return [[_k_sigmoid(ma) * _bdot(ya, wa, 1, 0) + _k_sigmoid(mb) * _bdot(yb, wb, 1, 0)
             + _k_sigmoid(mc) * _bdot(yc, wc, 1, 0)]]


def _fn_out_residual(rows, pars):
    (res,), (merged,), (w,), (gate,) = rows[0], rows[1], pars[0], pars[1]
    return [[res + gate * _bdot(merged, w, 1, 0)]]


def _fn_loss(rows, pars):
    (x,), (target,), (w,) = rows[0], rows[1], pars[0]
    y = x * lax.rsqrt(jnp.mean(x * x, axis=-1, keepdims=True) + EPS) * w
    per_row = 0.5 * jnp.mean(jnp.square(y - target), axis=-1, keepdims=True)
    return [[jnp.broadcast_to(per_row, (per_row.shape[0], 128))]]


def _fn_b_rope(rows, pars):
    q, (k, v), (cos,), (sin,) = rows
    rot = lambda x: x * cos + _swap16(x) * sin
    return [[rot(x) for x in q], [rot(k), v]]


def _fn_c_rope(rows, pars):
    q, k, (cos,), (sin,) = rows
    rot = lambda x: x * cos + _swap64(x) * sin
    return [[rot(x) for x in q], [rot(x) * (C_HD ** -0.5) for x in k]]


_norm_mod = _Rowwise(_fn_norm_mod, "norm_mod", [(D_MODEL, D_MODEL)], [D_MODEL] * 3, [(D_MODEL, D_MODEL)])
_out_residual = _Rowwise(_fn_out_residual, "out_residual", [(D_MODEL, D_MODEL)] * 2, [D_MODEL] * 2,
                         [(D_MODEL, D_MODEL)])
_loss_rows = _Rowwise(_fn_loss, "loss", [(D_MODEL, D_MODEL)] * 2, [D_MODEL], [(128, 128)], n_diff=1)
_a_out = _Rowwise(_fn_head_rms_gate, "a_out", [(512, 128)] * 2, [128], [(512, 128)])
_c_out = _Rowwise(_fn_group_norm_gate, "c_out", [(512, 128)] * 2, [128], [(512, 128)])
_b_out = _Rowwise(_fn_gate, "b_out", [(512, 512)] * 2, [], [(512, 512)])
_branch_merge = _Rowwise(_fn_branch_merge, "branch_merge", [(512, 512)] * 3 + [(D_MODEL, D_MODEL)] * 3,
                         [D_MODEL] * 3, [(D_MODEL, D_MODEL)])
_b_rope = _Rowwise(_fn_b_rope, "b_rope", [(512, 128), (256, 128), (128, 128), (128, 128)], [],
                   [(512, 128), (256, 128)], n_diff=2)
_c_rope = _Rowwise(_fn_c_rope, "c_rope", [(512, 128), (512, 128), (128, 128), (128, 128)], [],
                   [(512, 128), (512, 128)], n_diff=2)


HALO = 8
EXT = ROW_BLOCK + 2 * HALO


def _halo_specs(col, width=512):
    last = ROWS // HALO - 1
    per = ROW_BLOCK // HALO
    prev = pl.BlockSpec((HALO, width), lambda i: (jnp.maximum(i * per - 1, 0), col))
    cur = pl.BlockSpec((ROW_BLOCK, width), lambda i: (i, col))
    nxt = pl.BlockSpec((HALO, width), lambda i: (jnp.minimum((i + 1) * per, last), col))
    return [prev, cur, nxt]


def _extended(prev_ref, cur_ref, next_ref):
    i = pl.program_id(0)
    prev_ok = i >= 2
    next_ok = jnp.logical_and(i >= 1, i < ROWS // ROW_BLOCK - 1)
    return jnp.concatenate([jnp.where(prev_ok, prev_ref[...], 0.0), cur_ref[...],
                            jnp.where(next_ok, next_ref[...], 0.0)], axis=0)


def _conv_taps(x_ext, w_ref, flip):
    acc = None
    for j in range(A_CONV):
        shift = (j - 2) if flip else (2 - j)
        term = w_ref[j:j + 1, :] * pltpu.roll(x_ext, shift % EXT, 0)
        acc = term if acc is None else acc + term
    return acc


def _conv_post(pre_pieces, normalize, scale):
    out = []
    for p in pre_pieces:
        y = _k_silu(p)
        if normalize:
            y = y * lax.rsqrt(jnp.sum(y * y, axis=-1, keepdims=True) + EPS) * scale
        out.append(y)
    return out


def _a_prep_fwd(proj, conv8, col, normalize, scale, name):
    def body(prev_ref, cur_ref, next_ref, w_ref, o_ref):
        pre = _conv_taps(_extended(prev_ref, cur_ref, next_ref), w_ref, False)[HALO:HALO + ROW_BLOCK]
        for h, y in enumerate(_conv_post(_pieces(pre, 128), normalize, scale)):
            o_ref[:, h * 128:(h + 1) * 128] = y

    return pl.pallas_call(
        body,
        grid=(ROWS // ROW_BLOCK,),
        in_specs=_halo_specs(col) + [pl.BlockSpec((8, 512), lambda i: (0, col))],
        out_specs=pl.BlockSpec((ROW_BLOCK, 512), lambda i: (i, 0)),
        out_shape=jax.ShapeDtypeStruct((ROWS, 512), F32),
        compiler_params=_cparams(("parallel",)),
        name=name + "_fwd",
    )(proj, proj, proj, conv8)


def _a_prep_bwd(proj, conv8, col, normalize, scale, dout_f, dout_r, name):
    def body(xp, xc, xn, w_ref, fp, fc, fn_, rp, rc, rn, dx_ref, dw_ref):
        @pl.when(pl.program_id(0) == 0)
        def _():
            dw_ref[...] = jnp.zeros_like(dw_ref)

        x_ext = _extended(xp, xc, xn)
        dout = _extended(fp, fc, fn_) + _extended(rp, rc, rn)
        pre = _conv_taps(x_ext, w_ref, False)
        _, vjp = jax.vjp(lambda p: _conv_post(p, normalize, scale), _pieces(pre, 128))
        (dpre,) = vjp(_pieces(dout, 128))
        dpre = jnp.concatenate(dpre, axis=1)
        dx_ref[...] = _conv_taps(dpre, w_ref, True)[HALO:HALO + ROW_BLOCK].astype(BF16)
        own = dpre[HALO:HALO + ROW_BLOCK]
        for j in range(A_CONV):
            shifted = pltpu.roll(x_ext, (2 - j) % EXT, 0)[HALO:HALO + ROW_BLOCK]
            dw_ref[j:j + 1, :] += jnp.sum(own * shifted, axis=0, keepdims=True)

    return pl.pallas_call(
        body,
        grid=(ROWS // ROW_BLOCK,),
        in_specs=_halo_specs(col) + [pl.BlockSpec((8, 512), lambda i: (0, col))] + _halo_specs(0) + _halo_specs(0),
        out_specs=[pl.BlockSpec((ROW_BLOCK, 512), lambda i: (i, 0)), pl.BlockSpec((8, 512), lambda i: (0, 0))],
        out_shape=[jax.ShapeDtypeStruct((ROWS, 512), BF16), jax.ShapeDtypeStruct((8, 512), F32)],
        compiler_params=_cparams(("arbitrary",)),
        name=name + "_bwd",
    )(proj, proj, proj, conv8, dout_f, dout_f, dout_f, dout_r, dout_r, dout_r)


N_CHAIN = 8


def _rev_chunk(s, chunk):
    n_ctx, n_all = CTX_LEN // chunk, ROWS // chunk
    return jnp.where(s < n_ctx, n_ctx - 1 - s, n_all + n_ctx - 1 - s)


def _scan_specs(step_of, chunk, v_col=0):
    cf = step_of
    cr = lambda n: _rev_chunk(step_of(n), chunk)

    def pair(shape, index):
        return (pl.BlockSpec(shape, lambda n: index(cf(n))), pl.BlockSpec(shape, lambda n: index(cr(n))))

    return dict(
        tok=pair((chunk, 512), lambda c: (c, 0)),
        tokv=pair((chunk, 512), lambda c: (c, v_col)),
        row=pair((4, 1, 1, chunk), lambda c: (0, c, 0, 0)),
        one=pair((4, 1, 1, 1), lambda c: (0, c, 0, 0)),
        state=pair((None, 4, 128, 128), lambda c: (c, 0, 0, 0)),
        tinv=pair((None, 4, chunk, chunk), lambda c: (c, 0, 0, 0)),
    )


def _both(specs, kinds):
    out = []
    for kind in kinds:
        out += list(specs[kind])
    return out


def _scan_call(body, name, in_specs, out_specs, out_shape, operands, ride, chunk):
    grid = (ROWS // chunk,)
    body, r_in, r_out, r_shape, r_scratch = _riding(body, len(in_specs), len(out_specs), 1, ride, grid)
    return pl.pallas_call(
        body,
        grid=grid,
        in_specs=in_specs + r_in,
        out_specs=out_specs + r_out,
        out_shape=out_shape + r_shape,
        scratch_shapes=[pltpu.VMEM((N_CHAIN, 128, 128), F32)] + r_scratch,
        compiler_params=_cparams(("arbitrary",)),
        name=name,
    )(*operands, *(ride[0] if ride else []))


def _chain_masks():
    ii = lax.broadcasted_iota(jnp.int32, (CHUNK, CHUNK), 0)
    jj = lax.broadcasted_iota(jnp.int32, (CHUNK, CHUNK), 1)
    eye = jnp.where(ii == jj, 1.0, 0.0).astype(F32)
    lower = (ii >= jj, ii > jj)
    upper = (ii <= jj, ii < jj)
    return [lower] * 4 + [upper] * 4, eye


INV_BLOCK = 64


def _series_inverse(ls):
    ii = lax.broadcasted_iota(jnp.int32, (INV_BLOCK, INV_BLOCK), 0)
    jj = lax.broadcasted_iota(jnp.int32, (INV_BLOCK, INV_BLOCK), 1)
    eye = jnp.where(ii == jj, 1.0, 0.0).astype(F32)
    doublings = INV_BLOCK.bit_length() - 2
    xs = [eye - l for l in ls]
    ps = [_hdot(l, l) for l in ls]
    for i in range(doublings):
        xs = [x + _hdot(x, p) for x, p in zip(xs, ps)]
        if i < doublings - 1:
            ps = [_hdot(p, p) for p in ps]
    return xs


def _tri_inv_all(ls, upper):
    if CHUNK == INV_BLOCK:
        return _series_inverse(ls)
    n, h = len(ls), INV_BLOCK
    diag = _series_inverse([l[:h, :h] for l in ls] + [l[h:, h:] for l in ls])
    out = []
    zero = jnp.zeros((h, h), F32)
    for i, l in enumerate(ls):
        a, d = diag[i], diag[n + i]
        if upper[i]:
            off = -_hdot(_hdot(a, l[:h, h:]), d)
            out.append(jnp.concatenate([jnp.concatenate([a, off], axis=1), jnp.concatenate([zero, d], axis=1)], axis=0))
        else:
            off = -_hdot(_hdot(d, l[h:, :h]), a)
            out.append(jnp.concatenate([jnp.concatenate([a, zero], axis=1), jnp.concatenate([off, d], axis=1)], axis=0))
    return out


@jax.custom_vjp
def _inv_saved(l, x):
    return x


def _inv_saved_fwd(l, x):
    return x, x


def _inv_saved_bwd(x, dx):
    return -_bdot(x, _bdot(dx, x, 1, 1), 0, 0), jnp.zeros_like(x)


_inv_saved.defvjp(_inv_saved_fwd, _inv_saved_bwd)


def _delta_chains(q, k, v, beta_r, gcr, gl, s, masks, eye, tinv_saved):
    n = range(len(q))
    beta = [jnp.sum(eye * beta_r[i], axis=1, keepdims=True) for i in n]
    gcc = [jnp.sum(eye * gcr[i], axis=1, keepdims=True) for i in n]
    decay = [jnp.exp(jnp.where(masks[i][0], gcc[i] - gcr[i], NEG)) for i in n]
    kb = [k[i] * beta[i] for i in n]
    lmat = [jnp.where(masks[i][1], _bdot(kb[i], k[i], 1, 1) * decay[i], 0.0) for i in n]
    if tinv_saved is None:
        tinv = _tri_inv_all(lmat, [i >= 4 for i in n])
    else:
        tinv = [_inv_saved(lmat[i], tinv_saved[i]) for i in n]
    eg = [jnp.exp(gcc[i]) for i in n]
    u = [_bdot(tinv[i], v[i] * beta[i], 1, 0) for i in n]
    w = [_bdot(tinv[i], kb[i] * eg[i], 1, 0) for i in n]
    qk = [_bdot(q[i], k[i], 1, 1) * decay[i] for i in n]
    v_new = [u[i] - _bdot(w[i], s[i], 1, 0) for i in n]
    o = [_bdot(q[i] * eg[i], s[i], 1, 0) + _bdot(qk[i], v_new[i], 1, 0) for i in n]
    s_new = [s[i] * jnp.exp(gl[i]) + _bdot(k[i] * jnp.exp(gl[i] - gcc[i]), v_new[i], 0, 0) for i in n]
    return (o, s_new), tinv


def _chain_loads(tok_pairs, small_pairs):
    toks = [[pair[i // 4][:, (i % 4) * 128:(i % 4 + 1) * 128] for i in range(N_CHAIN)] for pair in tok_pairs]
    smalls = [[pair[i // 4][i % 4] for i in range(N_CHAIN)] for pair in small_pairs]
    return toks, smalls


def _delta_fwd_call(q, k, v, beta, gc, gl, ride=None):
    sp = _scan_specs(lambda n: n, CHUNK)

    def body(qf, qr, kf, kr, vf, vr, bf, br, gcrf, gcrr, glf, glr, of, orv, ssf, ssr, tsf, tsr, s_scr):
        @pl.when(pl.program_id(0) == 0)
        def _():
            s_scr[...] = jnp.zeros_like(s_scr)

        masks, eye = _chain_masks()
        (qs, ks, vs), _ = _chain_loads([(qf, qr), (kf, kr), (vf, vr)], [])
        bs = [(bf, br)[i // 4][i % 4, 0] for i in range(N_CHAIN)]
        gcrs = [(gcrf, gcrr)[i // 4][i % 4, 0] for i in range(N_CHAIN)]
        gls = [(glf, glr)[i // 4][i % 4, 0] for i in range(N_CHAIN)]
        ss = [s_scr[i] for i in range(N_CHAIN)]
        (o, s_new), tinv = _delta_chains(qs, ks, vs, bs, gcrs, gls, ss, masks, eye, None)
        for i in range(N_CHAIN):
            d, h = i // 4, i % 4
            (ssf, ssr)[d][h] = ss[i]
            (tsf, tsr)[d][h] = tinv[i]
            (of, orv)[d][:, h * 128:(h + 1) * 128] = o[i]
            s_scr[i] = s_new[i]

    return _scan_call(
        body, "delta_fwd",
        _both(sp, ["tok", "tok", "tok", "row", "row", "one"]),
        _both(sp, ["tok", "state", "tinv"]),
        [jax.ShapeDtypeStruct((ROWS, 512), F32)] * 2 + [jax.ShapeDtypeStruct((N_CHUNK, 4, 128, 128), F32)] * 2
        + [jax.ShapeDtypeStruct((N_CHUNK, 4, CHUNK, CHUNK), F32)] * 2,
        [q, q, k, k, v, v, *beta, *gc, *gl], ride, CHUNK)


def _delta_bwd_call(q, k, v, beta, gc, gl, ssave, tsave, do, ride=None):
    sp = _scan_specs(lambda n: N_CHUNK - 1 - n, CHUNK)

    def body(qf, qr, kf, kr, vf, vr, bf, br, gcrf, gcrr, glf, glr, ssf, ssr, tsf, tsr, dof, dor,
             dqf, dqr, dkf, dkr, dvf, dvr, dbf, dbr, dgcrf, dgcrr, dglf, dglr, ds_scr):
        @pl.when(pl.program_id(0) == 0)
        def _():
            ds_scr[...] = jnp.zeros_like(ds_scr)

        masks, eye = _chain_masks()
        (qs, ks, vs, dos), (ss, ts) = _chain_loads(
            [(qf, qr), (kf, kr), (vf, vr), (dof, dor)], [(ssf, ssr), (tsf, tsr)])
        bs = [(bf, br)[i // 4][i % 4, 0] for i in range(N_CHAIN)]
        gcrs = [(gcrf, gcrr)[i // 4][i % 4, 0] for i in range(N_CHAIN)]
        gls = [(glf, glr)[i // 4][i % 4, 0] for i in range(N_CHAIN)]
        fn = lambda *a: _delta_chains(*a, masks, eye, ts)
        _, vjp, _ = jax.vjp(fn, qs, ks, vs, bs, gcrs, gls, ss, has_aux=True)
        dq, dk, dv, db, dgcr, dgl, ds = vjp((dos, [ds_scr[i] for i in range(N_CHAIN)]))
        for i in range(N_CHAIN):
            d, h = i // 4, i % 4
            hs = slice(h * 128, (h + 1) * 128)
            (dqf, dqr)[d][:, hs] = dq[i]
            (dkf, dkr)[d][:, hs] = dk[i]
            (dvf, dvr)[d][:, hs] = dv[i]
            (dbf, dbr)[d][h, 0] = db[i]
            (dgcrf, dgcrr)[d][h, 0] = dgcr[i]
            (dglf, dglr)[d][h, 0] = dgl[i]
            ds_scr[i] = ds[i]

    tok = jax.ShapeDtypeStruct((ROWS, 512), F32)
    return _scan_call(
        body, "delta_bwd",
        _both(sp, ["tok", "tok", "tok", "row", "row", "one", "state", "tinv", "tok"]),
        _both(sp, ["tok", "tok", "tok", "row", "row", "one"]),
        [tok] * 6 + [jax.ShapeDtypeStruct((4, N_CHUNK, 1, CHUNK), F32)] * 4
        + [jax.ShapeDtypeStruct((4, N_CHUNK, 1, 1), F32)] * 2,
        [q, q, k, k, v, v, *beta, *gc, *gl, *ssave, *tsave, do, do], ride, CHUNK)


def _ret_chains(q, k, v, dm, qs, ks, cd, s):
    n = range(len(q))
    a = [_bdot(q[i], k[i], 1, 1) * dm[i] for i in n]
    o = [_bdot(a[i], v[i], 1, 0) + _bdot(q[i] * qs[i], s[i], 1, 0) for i in n]
    s_new = [s[i] * cd[i] + _bdot(k[i] * ks[i], v[i], 0, 0) for i in n]
    return o, s_new


RET_CONST_SHAPES = ((N_CHAIN, RET_CHUNK, RET_CHUNK), (N_CHAIN, RET_CHUNK, 1), (N_CHAIN, RET_CHUNK, 1), (N_CHAIN, 1, 1))


def _ret_const_specs():
    return [pl.BlockSpec(shape, lambda n: (0, 0, 0)) for shape in RET_CONST_SHAPES]


def _ret_fwd_call(q, k, v, v_col, dm, qs, ks, cd, ride=None):
    sp = _scan_specs(lambda n: n, RET_CHUNK, v_col)

    def body(qf, qr, kf, kr, vf, vr, dm_ref, qs_ref, ks_ref, cd_ref, of, orv, ssf, ssr, s_scr):
        @pl.when(pl.program_id(0) == 0)
        def _():
            s_scr[...] = jnp.zeros_like(s_scr)

        (qc, kc, vc), _ = _chain_loads([(qf, qr), (kf, kr), (vf, vr)], [])
        ss = [s_scr[i] for i in range(N_CHAIN)]
        consts = [[r[i] for i in range(N_CHAIN)] for r in (dm_ref, qs_ref, ks_ref, cd_ref)]
        o, s_new = _ret_chains(qc, kc, vc, *consts, ss)
        for i in range(N_CHAIN):
            d, h = i // 4, i % 4
            (ssf, ssr)[d][h] = ss[i]
            (of, orv)[d][:, h * 128:(h + 1) * 128] = o[i]
            s_scr[i] = s_new[i]

    return _scan_call(
        body, "ret_fwd",
        _both(sp, ["tok", "tok", "tokv"]) + _ret_const_specs(),
        _both(sp, ["tok", "state"]),
        [jax.ShapeDtypeStruct((ROWS, 512), F32)] * 2
        + [jax.ShapeDtypeStruct((ROWS // RET_CHUNK, 4, 128, 128), F32)] * 2,
        [q, q, k, k, v, v, dm, qs, ks, cd], ride, RET_CHUNK)


def _ret_bwd_call(q, k, v, v_col, dm, qs, ks, cd, ssave, do, ride=None):
    sp = _scan_specs(lambda n: ROWS // RET_CHUNK - 1 - n, RET_CHUNK, v_col)

    def body(qf, qr, kf, kr, vf, vr, dm_ref, qs_ref, ks_ref, cd_ref, ssf, ssr, dof, dor,
             dqf, dqr, dkf, dkr, dvf, dvr, ddm_ref, dqs_ref, dks_ref, dcd_ref, ds_scr):
        @pl.when(pl.program_id(0) == 0)
        def _():
            ds_scr[...] = jnp.zeros_like(ds_scr)
            ddm_ref[...] = jnp.zeros_like(ddm_ref)
            dqs_ref[...] = jnp.zeros_like(dqs_ref)
            dks_ref[...] = jnp.zeros_like(dks_ref)
            dcd_ref[...] = jnp.zeros_like(dcd_ref)

        (qc, kc, vc, dos), (ss,) = _chain_loads([(qf, qr), (kf, kr), (vf, vr), (dof, dor)], [(ssf, ssr)])
        consts = [[r[i] for i in range(N_CHAIN)] for r in (dm_ref, qs_ref, ks_ref, cd_ref)]
        _, vjp = jax.vjp(_ret_chains, qc, kc, vc, *consts, ss)
        dq, dk, dv, ddm, dqs, dks, dcd, ds = vjp((dos, [ds_scr[i] for i in range(N_CHAIN)]))
        for i in range(N_CHAIN):
            d, h = i // 4, i % 4
            hs = slice(h * 128, (h + 1) * 128)
            (dqf, dqr)[d][:, hs] = dq[i]
            (dkf, dkr)[d][:, hs] = dk[i]
            (dvf, dvr)[d][:, hs] = dv[i]
            ddm_ref[i] += ddm[i]
            dqs_ref[i] += dqs[i]
            dks_ref[i] += dks[i]
            dcd_ref[i] += dcd[i]
            ds_scr[i] = ds[i]

    tok = jax.ShapeDtypeStruct((ROWS, 512), F32)
    return _scan_call(
        body, "ret_bwd",
        _both(sp, ["tok", "tok", "tokv"]) + _ret_const_specs() + _both(sp, ["state", "tok"]),
        _both(sp, ["tok", "tok", "tok"]) + _ret_const_specs(),
        [tok] * 6 + [jax.ShapeDtypeStruct(shape, F32) for shape in RET_CONST_SHAPES],
        [q, q, k, k, v, v, dm, qs, ks, cd, *ssave, do, do], ride, RET_CHUNK)


N_QBLK = ROWS // B_BLOCK
CTX_QBLK = CTX_LEN // B_BLOCK


def _attn_heads(q, kc, vc, kw, vw, sink, valid):
    n = range(len(q))
    qs = [q[i] * (B_HD ** -0.5) for i in n]
    s_c = [_bdot(qs[i], kc[i], 1, 1) for i in n]
    s_w = [jnp.where(valid, _bdot(qs[i], kw[i], 1, 1), NEG) for i in n]
    m = [lax.stop_gradient(jnp.maximum(jnp.maximum(jnp.max(s_c[i], axis=-1, keepdims=True), sink[i]),
                                       jnp.max(s_w[i], axis=-1, keepdims=True))) for i in n]
    e_c = [jnp.exp(s_c[i] - m[i]) for i in n]
    e_w = [jnp.exp(s_w[i] - m[i]) for i in n]
    den = [jnp.sum(e_c[i], axis=-1, keepdims=True) + jnp.sum(e_w[i], axis=-1, keepdims=True)
           + jnp.exp(sink[i] - m[i]) for i in n]
    return [(_bdot(e_c[i], vc[i], 1, 0) + _bdot(e_w[i], vw[i], 1, 0)) / den[i] for i in n]


def _attn_loads(q_ref, kv_ref, sink_ref, start):
    q, kc, vc, kw, vw, sink = [], [], [], [], [], []
    for hk in range(B_KV_HEADS):
        ks = slice(hk * B_HD, (hk + 1) * B_HD)
        vs = slice(128 + hk * B_HD, 128 + (hk + 1) * B_HD)
        grp = (kv_ref[0:CTX_LEN, ks], kv_ref[0:CTX_LEN, vs],
               kv_ref[pl.ds(start, 3 * B_BLOCK), ks], kv_ref[pl.ds(start, 3 * B_BLOCK), vs])
        for g in range(4):
            h = hk * 4 + g
            q.append(q_ref[:, h * B_HD:(h + 1) * B_HD])
            for lst, val in zip((kc, vc, kw, vw), grp):
                lst.append(val)
            sink.append(jnp.full((1, 1), sink_ref[h], F32))
    return q, kc, vc, kw, vw, sink


def _window(blk):
    xblk = blk - CTX_QBLK
    first = jnp.clip((xblk - 1) * B_BLOCK, 0, SEQ - 3 * B_BLOCK)
    qpos = xblk * B_BLOCK + lax.broadcasted_iota(jnp.int32, (B_BLOCK, 3 * B_BLOCK), 0)
    kpos = first + lax.broadcasted_iota(jnp.int32, (B_BLOCK, 3 * B_BLOCK), 1)
    far = jnp.where(blk >= CTX_QBLK, 0, 2 * SEQ)
    valid = jnp.abs(kpos - qpos) + far <= WINDOW
    return pl.multiple_of(first + CTX_LEN, B_BLOCK), valid


def _attn_specs():
    qspec = pl.BlockSpec((B_BLOCK, 512), lambda i: (i, 0))
    kvspec = pl.BlockSpec((ROWS, 256), lambda i: (0, 0))
    return qspec, kvspec, pl.BlockSpec(memory_space=pltpu.SMEM)


def _attn_fwd_call(q, kv, sink, ride=None):
    def body(q_ref, kv_ref, sink_ref, o_ref):
        start, valid = _window(pl.program_id(0))
        out = _attn_heads(*_attn_loads(q_ref, kv_ref, sink_ref, start), valid)
        for h in range(B_Q_HEADS):
            o_ref[:, h * B_HD:(h + 1) * B_HD] = out[h]

    qspec, kvspec, sspec = _attn_specs()
    body, r_in, r_out, r_shape, r_scratch = _riding(body, 3, 1, 0, ride, (N_QBLK,))
    return pl.pallas_call(
        body,
        grid=(N_QBLK,),
        in_specs=[qspec, kvspec, sspec] + r_in,
        out_specs=[qspec] + r_out,
        out_shape=[jax.ShapeDtypeStruct((ROWS, 512), F32)] + r_shape,
        scratch_shapes=r_scratch,
        compiler_params=_cparams(("arbitrary",)),
        name="attn_fwd",
    )(q, kv, sink, *(ride[0] if ride else []))


def _attn_bwd_call(q, kv, sink, do, ride=None):
    def body(q_ref, kv_ref, sink_ref, do_ref, dq_ref, dkv_ref, dsink_ref):
        @pl.when(pl.program_id(0) == 0)
        def _():
            dkv_ref[...] = jnp.zeros_like(dkv_ref)
            dsink_ref[...] = jnp.zeros_like(dsink_ref)

        start, valid = _window(pl.program_id(0))
        _, vjp = jax.vjp(functools.partial(_attn_heads, valid=valid), *_attn_loads(q_ref, kv_ref, sink_ref, start))
        dq, dkc, dvc, dkw, dvw, dsink = vjp([do_ref[:, h * B_HD:(h + 1) * B_HD] for h in range(B_Q_HEADS)])
        for h in range(B_Q_HEADS):
            dq_ref[:, h * B_HD:(h + 1) * B_HD] = dq[h]
            dsink_ref[h:h + 1, :] += jnp.broadcast_to(dsink[h], (1, 128))
        for hk in range(B_KV_HEADS):
            ks = slice(hk * B_HD, (hk + 1) * B_HD)
            vs = slice(128 + hk * B_HD, 128 + (hk + 1) * B_HD)
            grp = lambda parts: parts[hk * 4] + parts[hk * 4 + 1] + parts[hk * 4 + 2] + parts[hk * 4 + 3]
            dkv_ref[0:CTX_LEN, ks] += grp(dkc)
            dkv_ref[0:CTX_LEN, vs] += grp(dvc)
            dkv_ref[pl.ds(start, 3 * B_BLOCK), ks] += grp(dkw)
            dkv_ref[pl.ds(start, 3 * B_BLOCK), vs] += grp(dvw)

    qspec, kvspec, sspec = _attn_specs()
    body, r_in, r_out, r_shape, r_scratch = _riding(body, 4, 3, 0, ride, (N_QBLK,))
    return pl.pallas_call(
        body,
        grid=(N_QBLK,),
        in_specs=[qspec, kvspec, sspec, qspec] + r_in,
        out_specs=[qspec, kvspec, pl.BlockSpec((8, 128), lambda i: (0, 0))] + r_out,
        out_shape=[jax.ShapeDtypeStruct((ROWS, 512), F32), jax.ShapeDtypeStruct((ROWS, 256), F32),
                   jax.ShapeDtypeStruct((8, 128), F32)] + r_shape,
        scratch_shapes=r_scratch,
        compiler_params=_cparams(("arbitrary",)),
        name="attn_bwd",
    )(q, kv, sink, do, *(ride[0] if ride else []))


def _my_id():
    return 4 * lax.axis_index("x") + 2 * lax.axis_index("y") + lax.axis_index("c")


def _peer(k):
    x, y, c = lax.axis_index("x"), lax.axis_index("y"), lax.axis_index("c")
    return (1 - x if k & 4 else x, 1 - y if k & 2 else y, 1 - c if k & 1 else c)


SAME_CORE_PEERS = (2, 4, 6)


def _scatter_copies(ins, outs, sems):
    send_sems, recv_sems, local_sems = sems
    me = _my_id()
    own, remote = [], []
    for a in range(len(ins)):
        own.append(pltpu.make_async_copy(ins[a].at[me], outs[a].at[me], local_sems.at[a]))
        for k in range(1, N_DEV):
            peer_slot = jnp.bitwise_xor(me, k)
            common = dict(src_ref=ins[a].at[peer_slot], send_sem=send_sems.at[a, k - 1],
                          recv_sem=recv_sems.at[a, k - 1], device_id=_peer(k), device_id_type=MESH)
            remote.append((pltpu.make_async_remote_copy(dst_ref=outs[a].at[me], **common),
                           pltpu.make_async_remote_copy(dst_ref=outs[a].at[peer_slot], **common)))
    return own, remote


def _gather_copy(outs, sems, a, k, src, slot, to):
    return pltpu.make_async_remote_copy(src_ref=src, dst_ref=outs[a].at[slot], send_sem=sems[0].at[a, k - 1],
                                        recv_sem=sems[1].at[a, k - 1], device_id=_peer(to), device_id_type=MESH)


def _gather_first_copies(ins, outs, sems):
    me = _my_id()
    own = [pltpu.make_async_copy(ins[a], outs[a].at[me], sems[2].at[a]) for a in range(len(ins))]
    direct = [_gather_copy(outs, sems, a, k, ins[a], me, k) for a in range(len(ins)) for k in (1,) + SAME_CORE_PEERS]
    return own, direct


def _exchange_start(ins, outs, sems, gather):
    own, remote = _gather_first_copies(ins, outs, sems) if gather else _scatter_copies(ins, outs, sems)
    for cp in own:
        cp.start()
    for cp in remote:
        (cp if gather else cp[0]).start()


def _exchange_wait(ins, outs, sems, gather):
    if not gather:
        own, remote = _scatter_copies(ins, outs, sems)
        for _, arrival in remote:
            arrival.wait_recv()
        for send, _ in remote:
            send.wait_send()
        for cp in own:
            cp.wait()
        return
    me = _my_id()
    own, direct = _gather_first_copies(ins, outs, sems)
    passed = []
    for a in range(len(ins)):
        for k in SAME_CORE_PEERS:
            origin = jnp.bitwise_xor(me, k)
            _gather_copy(outs, sems, a, k, ins[a], origin, k).wait_recv()
            onward = _gather_copy(outs, sems, a, k + 1, outs[a].at[origin], origin, 1)
            onward.start()
            passed.append(onward)
    for a in range(len(ins)):
        for k in (1, 3, 5, 7):
            _gather_copy(outs, sems, a, k, ins[a], jnp.bitwise_xor(me, k), 1).wait_recv()
    for cp in direct + passed:
        cp.wait_send()
    for cp in own:
        cp.wait()


def _exchange_plumbing(arrays, gather):
    n = len(arrays)
    hbm = [pl.BlockSpec(memory_space=pltpu.HBM)] * n
    out_shape = [jax.ShapeDtypeStruct((N_DEV,) + (a.shape if gather else a.shape[1:]), a.dtype) for a in arrays]
    sems = [pltpu.SemaphoreType.DMA((n, N_DEV - 1)), pltpu.SemaphoreType.DMA((n, N_DEV - 1)),
            pltpu.SemaphoreType.DMA((n,))]
    return hbm, out_shape, sems


def _exchange(arrays, gather, name):
    n = len(arrays)

    def body(*refs):
        ins, outs, sems = refs[:n], refs[n:2 * n], refs[2 * n:]
        _exchange_start(ins, outs, sems, gather)
        _exchange_wait(ins, outs, sems, gather)

    hbm, out_shape, sems = _exchange_plumbing(arrays, gather)
    return pl.pallas_call(
        body,
        in_specs=hbm,
        out_specs=hbm,
        out_shape=out_shape,
        scratch_shapes=sems,
        compiler_params=pltpu.CompilerParams(has_side_effects=True),
        name=name,
    )(*arrays)


N_CHIP = N_DEV // 2


def _pair_swap(blocks, name):
    n = len(blocks)

    def body(*refs):
        ins, outs, (send_sems, recv_sems) = refs[:n], refs[n:2 * n], refs[2 * n:]
        core = lax.axis_index("c")
        copies = [pltpu.make_async_remote_copy(src_ref=ins[a].at[2 * chip + (1 - core)], dst_ref=outs[a].at[chip],
                                               send_sem=send_sems.at[a, chip], recv_sem=recv_sems.at[a, chip],
                                               device_id=_peer(1), device_id_type=MESH)
                  for a in range(n) for chip in range(N_CHIP)]
        for cp in copies:
            cp.start()
        for cp in copies:
            cp.wait_recv()
        for cp in copies:
            cp.wait_send()

    hbm = [pl.BlockSpec(memory_space=pltpu.HBM)] * n
    return pl.pallas_call(
        body,
        in_specs=hbm,
        out_specs=hbm,
        out_shape=[jax.ShapeDtypeStruct((N_CHIP,) + b.shape[1:], b.dtype) for b in blocks],
        scratch_shapes=[pltpu.SemaphoreType.DMA((n, N_CHIP)), pltpu.SemaphoreType.DMA((n, N_CHIP))],
        compiler_params=pltpu.CompilerParams(has_side_effects=True),
        name=name,
    )(*blocks)


def _chip_scatter(pairs, name):
    n = len(pairs)

    def body(*refs):
        ins, outs, (send_sems, recv_sems, local_sems) = refs[:n], refs[n:2 * n], refs[2 * n:]
        chip = 2 * lax.axis_index("x") + lax.axis_index("y")
        own = [pltpu.make_async_copy(ins[a].at[chip], outs[a].at[chip], local_sems.at[a]) for a in range(n)]
        sends, arrivals = [], []
        for a in range(n):
            for k in range(1, N_CHIP):
                other = jnp.bitwise_xor(chip, k)
                common = dict(src_ref=ins[a].at[other], send_sem=send_sems.at[a, k - 1], recv_sem=recv_sems.at[a, k - 1],
                              device_id=_peer(2 * k), device_id_type=MESH)
                sends.append(pltpu.make_async_remote_copy(dst_ref=outs[a].at[chip], **common))
                arrivals.append(pltpu.make_async_remote_copy(dst_ref=outs[a].at[other], **common))
        for cp in own + sends:
            cp.start()
        for cp in arrivals:
            cp.wait_recv()
        for cp in sends:
            cp.wait_send()
        for cp in own:
            cp.wait()

    hbm = [pl.BlockSpec(memory_space=pltpu.HBM)] * n
    return pl.pallas_call(
        body,
        in_specs=hbm,
        out_specs=hbm,
        out_shape=[jax.ShapeDtypeStruct(p.shape, p.dtype) for p in pairs],
        scratch_shapes=[pltpu.SemaphoreType.DMA((n, N_CHIP - 1)), pltpu.SemaphoreType.DMA((n, N_CHIP - 1)),
                        pltpu.SemaphoreType.DMA((n,))],
        compiler_params=pltpu.CompilerParams(has_side_effects=True),
        name=name,
    )(*pairs)


def _scatter_two_level(blocks, name):
    swapped = _pair_swap(blocks, name + "_pair")
    core = lax.axis_index("c")
    pairs = []
    for b, s in zip(blocks, swapped):
        mine = lax.dynamic_index_in_dim(b.reshape((N_CHIP, 2) + b.shape[1:]), core, axis=1, keepdims=False)
        pairs.append((mine.astype(F32) + s.astype(F32)).astype(b.dtype))
    return _chip_scatter(pairs, name + "_chip")


def _riding(body, n_in, n_out, n_scratch, ride, grid):
    if ride is None:
        return body, [], [], [], []
    arrays, gather = ride
    n = len(arrays)

    def at(step_of):
        hit = pl.program_id(0) == step_of(grid[0])
        for d in range(1, len(grid)):
            hit = jnp.logical_and(hit, pl.program_id(d) == step_of(grid[d]))
        return hit

    def wrapped(*refs):
        ins, rin = refs[:n_in], refs[n_in:n_in + n]
        outs = refs[n_in + n:n_in + n + n_out]
        rout = refs[n_in + n + n_out:n_in + 2 * n + n_out]
        scratch = refs[n_in + 2 * n + n_out:n_in + 2 * n + n_out + n_scratch]
        sems = refs[n_in + 2 * n + n_out + n_scratch:]

        @pl.when(at(lambda size: 0))
        def _():
            _exchange_start(rin, rout, sems, gather)

        body(*ins, *outs, *scratch)

        @pl.when(at(lambda size: size - 1))
        def _():
            _exchange_wait(rin, rout, sems, gather)

    hbm, out_shape, sems = _exchange_plumbing(arrays, gather)
    return wrapped, hbm, hbm, out_shape, sems


def _sum_contributions(c_ref):
    g = c_ref[0].astype(F32)
    for j in range(1, c_ref.shape[0]):
        g = g + c_ref[j].astype(F32)
    return g


def _adamw_update(g, w_ref, m_ref, v_ref, g_ref, d_ref, nm_ref, nv_ref):
    m_new = ADAM_B1 * m_ref[...] + (1.0 - ADAM_B1) * g
    v_new = ADAM_B2 * v_ref[...] + (1.0 - ADAM_B2) * (g * g)
    m_hat = m_new / (1.0 - ADAM_B1 ** ADAM_STEP)
    v_hat = v_new / (1.0 - ADAM_B2 ** ADAM_STEP)
    g_ref[...] = g
    d_ref[...] = -ADAM_LR * (m_hat / (jnp.sqrt(v_hat) + ADAM_EPS) + ADAM_WD * w_ref[...])
    nm_ref[...] = m_new
    nv_ref[...] = v_new


def _adamw_layers(w, m, v, contrib0, contrib1, name):
    _, r, c = w.shape
    br = _pick(r, (256, 128, 64, 32, 16, 8))
    nb = r // br

    def body(w_ref, m_ref, v_ref, c0_ref, c1_ref, g_ref, d_ref, nm_ref, nv_ref):
        g = jnp.where(pl.program_id(0) == 0, _sum_contributions(c0_ref), _sum_contributions(c1_ref))
        _adamw_update(g, w_ref, m_ref, v_ref, g_ref, d_ref, nm_ref, nv_ref)

    spec = pl.BlockSpec((None, br, c), lambda l, i: (l, i, 0))
    return pl.pallas_call(
        body,
        grid=(DEPTH, nb),
        in_specs=[spec, spec, spec,
                  pl.BlockSpec((contrib0.shape[0], br, c), lambda l, i: (0, jnp.where(l == 0, i, nb - 1), 0)),
                  pl.BlockSpec((contrib1.shape[0], br, c), lambda l, i: (0, jnp.where(l == 1, i, 0), 0))],
        out_specs=[spec] * 4,
        out_shape=[jax.ShapeDtypeStruct(w.shape, F32)] * 4,
        compiler_params=_cparams(("arbitrary", "arbitrary")),
        name=name,
    )(w, m, v, contrib0, contrib1)


def _adamw(w, m, v, contrib, name):
    r, c = w.shape
    br = _pick(r, (256, 128, 64, 32, 16, 8))

    def body(w_ref, m_ref, v_ref, c_ref, g_ref, d_ref, nm_ref, nv_ref):
        _adamw_update(_sum_contributions(c_ref), w_ref, m_ref, v_ref, g_ref, d_ref, nm_ref, nv_ref)

    spec = pl.BlockSpec((br, c), lambda i: (i, 0))
    cspec = pl.BlockSpec((contrib.shape[0], br, c), lambda i: (0, i, 0))
    return pl.pallas_call(
        body,
        grid=(r // br,),
        in_specs=[spec, spec, spec, cspec],
        out_specs=[spec] * 4,
        out_shape=[jax.ShapeDtypeStruct((r, c), F32)] * 4,
        compiler_params=_cparams(("parallel",)),
        name=name,
    )(w, m, v, contrib)


def _silu(x):
    return x * jax.nn.sigmoid(x)


def _rope_angles(pos, n_freq):
    inv = ROPE_BASE ** (-jnp.arange(n_freq, dtype=F32) / n_freq)
    return pos[:, None] * inv[None, :]


def _with_ctx_rows(cos, sin):
    return (jnp.concatenate([jnp.ones((CTX_LEN, 128), F32), cos], axis=0),
            jnp.concatenate([jnp.zeros((CTX_LEN, 128), F32), sin], axis=0))


def _rope_tables():
    rows_n = SEQ // GRID_W
    rows = jnp.repeat(jnp.arange(rows_n, dtype=F32), GRID_W)
    cols = jnp.tile(jnp.arange(GRID_W, dtype=F32), rows_n)
    ang_r = _rope_angles(rows, B_HD // 4)
    ang_c = _rope_angles(cols, B_HD // 4)
    cos_b = jnp.tile(jnp.concatenate([jnp.cos(ang_r)] * 2 + [jnp.cos(ang_c)] * 2, axis=1), (1, 2))
    sin_b = jnp.tile(jnp.concatenate([-jnp.sin(ang_r), jnp.sin(ang_r), -jnp.sin(ang_c), jnp.sin(ang_c)], axis=1), (1, 2))
    ang = _rope_angles(jnp.arange(SEQ, dtype=F32), C_HD // 2)
    cos_c = jnp.concatenate([jnp.cos(ang)] * 2, axis=1)
    sin_c = jnp.concatenate([-jnp.sin(ang), jnp.sin(ang)], axis=1)
    return _with_ctx_rows(cos_b, sin_b), _with_ctx_rows(cos_c, sin_c)


def _halves(a):
    return a[:4], a[4:]


def _delta_gates(ab, a_log, dt_bias):
    beta = jax.nn.sigmoid(ab[:, :8])
    g = -jnp.exp(a_log)[None, :] * jax.nn.softplus(ab[:, 8:] + dt_bias[None, :])
    gch = g.reshape(N_CHUNK, CHUNK, 8)
    tri = jnp.tril(jnp.ones((CHUNK, CHUNK), F32))
    fwd = jnp.einsum("ij,cjh->cih", tri, gch[..., :4], precision=HIGHEST)
    bwd = jnp.einsum("ji,cjh->cih", tri, gch[..., 4:], precision=HIGHEST)
    gc = jnp.concatenate([fwd, bwd], axis=-1)
    gl = jnp.sum(gch, axis=1)
    rows = lambda a: _halves(a.transpose(2, 0, 1)[:, :, None, :])
    return rows(beta.reshape(N_CHUNK, CHUNK, 8)), rows(gc), _halves(gl.T[:, :, None, None])


def _ret_consts(c_decay):
    lg = jax.nn.log_sigmoid(c_decay)
    idx = jnp.arange(RET_CHUNK, dtype=F32)
    diff = idx[:, None] - idx[None, :]
    lgf, lgb = lg[:4, None, None], lg[4:, None, None]
    dm = jnp.concatenate([jnp.exp(jnp.where(diff >= 0, diff * lgf, -jnp.inf)),
                          jnp.exp(jnp.where(diff <= 0, -diff * lgb, -jnp.inf))], axis=0)
    qs = jnp.concatenate([jnp.exp((idx + 1.0)[None, :] * lg[:4, None]),
                          jnp.exp((RET_CHUNK - idx)[None, :] * lg[4:, None])], axis=0)[:, :, None]
    ks = jnp.concatenate([jnp.exp((RET_CHUNK - 1.0 - idx)[None, :] * lg[:4, None]),
                          jnp.exp(idx[None, :] * lg[4:, None])], axis=0)[:, :, None]
    return dm, qs, ks, jnp.exp(RET_CHUNK * lg)[:, None, None]


A_PIECES = ((0, True, A_DK ** -0.5, "a_q"), (1, True, 1.0, "a_k"), (2, False, 1.0, "a_v"))
B_ROPE_COLS = [C_BQ // 512, C_BKV // 256, 0, 0]
C_ROPE_COLS = [C_CQ // 512, C_CK // 512, 0, 0]
MERGE_COLS = [0, 0, 0, C_MERGE // 1024, C_MERGE // 1024 + 1, C_MERGE // 1024 + 2]


def _conv8(conv_w):
    return jnp.pad(conv_w, ((0, 8 - A_CONV), (0, 0)))


W_IN_TILES = {"nn": (2176, 512, 1024), "nt": (1088, 1024, 2176), "db": (1024, 512, ROWS)}


def _core_forward(h, w16, p, rides):
    res = _matmul(h, w16, "w_in", "nn", W_IN_TILES["nn"], ride=rides.get("w_in"))
    proj, rode = (res[0], {"w_in": res[1:]}) if "w_in" in rides else (res, {})
    wb = p["w_branch"] if "w_in" not in rides else _unshard_layer("w_branch", rode["w_in"][0])
    (cos_b, sin_b), (cos_c, sin_c) = _rope_tables()
    conv8 = _conv8(p["a_conv_w"])
    q, k, v = [_a_prep_fwd(proj, conv8, col, nrm, scl, nm) for col, nrm, scl, nm in A_PIECES]
    gates = _delta_gates(proj[:, C_AB:C_AB + 16], p["a_log"], p["a_dt_bias"])
    res = _delta_fwd_call(q, k, v, *gates, ride=rides.get("delta"))
    (of, orv, ssf, ssr, tsf, tsr), rode["delta"] = res[:6], res[6:]
    (y_a,) = _a_out.fwd([(of, orv), proj], [p["a_norm_w"][None, :]], [0, C_AZ // 512])

    qb, kvb = _b_rope.fwd([proj, proj, cos_b, sin_b], [], B_ROPE_COLS)
    res = _attn_fwd_call(qb, kvb, p["b_sink"], ride=rides.get("attn"))
    ob, rode["attn"] = res[0], res[1:]
    (y_b,) = _b_out.fwd([ob, proj], [], [0, C_BZ // 512])

    qc, kc = _c_rope.fwd([proj, proj, cos_c, sin_c], [], C_ROPE_COLS)
    res = _ret_fwd_call(qc, kc, proj, C_CV // 512, *_ret_consts(p["c_decay"]), ride=rides.get("ret"))
    (cf, cr, csf, csr), rode["ret"] = res[:4], res[4:]
    (y_c,) = _c_out.fwd([(cf, cr), proj], [p["c_norm_w"][None, :]], [0, C_CZ // 512])

    (merged,) = _branch_merge.fwd([y_a, y_b, y_c, proj, proj, proj], [wb[0], wb[1], wb[2]], MERGE_COLS)
    saved = dict(proj=proj, q=q, k=k, v=v, of=of, orv=orv, ss=(ssf, ssr), ts=(tsf, tsr), qb=qb, kvb=kvb, ob=ob,
                 qc=qc, kc=kc, cf=cf, cr=cr, cs=(csf, csr), y=(y_a, y_b, y_c), wb=wb)
    return merged, saved, rode


def _core_backward(h, w16, p, s, dmerged, rides, branch_rides_in_attn=False):
    proj, wb = s["proj"], s["wb"]
    (cos_b, sin_b), (cos_c, sin_c) = _rope_tables()
    conv8 = _conv8(p["a_conv_w"])
    y_a, y_b, y_c = s["y"]
    rode = {}

    (*dy, dma, dmb, dmc), dwb = _branch_merge.bwd([y_a, y_b, y_c, proj, proj, proj], [wb[0], wb[1], wb[2]],
                                                   [dmerged], MERGE_COLS, bf16_rows=(3, 4, 5))
    dwb = jnp.stack(dwb)

    consts, consts_vjp = jax.vjp(_ret_consts, p["c_decay"])
    (do_c, dcz), (dcnw,) = _c_out.bwd([(s["cf"], s["cr"]), proj], [p["c_norm_w"][None, :]], [dy[2]],
                                      [0, C_CZ // 512], bf16_rows=(1,))
    g = _ret_bwd_call(s["qc"], s["kc"], proj, C_CV // 512, *consts, s["cs"], do_c, ride=rides.get("ret"))
    rode["ret"] = g[10:]
    (dcq, dck), _ = _c_rope.bwd([proj, proj, cos_c, sin_c], [], [(g[0], g[1]), (g[2], g[3])], C_ROPE_COLS,
                                bf16_rows=(0, 1))
    dcv = (g[4] + g[5]).astype(BF16)
    (dc_decay,) = consts_vjp(tuple(g[6:10]))

    (dob, dbz), _ = _b_out.bwd([s["ob"], proj], [], [dy[1]], [0, C_BZ // 512], bf16_rows=(1,))
    attn_ride = rides.get("attn")
    if branch_rides_in_attn:
        attn_ride = (list(attn_ride[0]) + [_reshard_layer("w_branch", dwb).astype(BF16)], attn_ride[1])
    res = _attn_bwd_call(s["qb"], s["kvb"], p["b_sink"], dob, ride=attn_ride)
    (dqb, dkvb, dsink), rode["attn"] = res[:3], res[3:]
    (dbq, dbkv), _ = _b_rope.bwd([proj, proj, cos_b, sin_b], [], [dqb, dkvb], B_ROPE_COLS, bf16_rows=(0, 1))

    ab = proj[:, C_AB:C_AB + 16]
    gates, gates_vjp = jax.vjp(_delta_gates, ab, p["a_log"], p["a_dt_bias"])
    (do_a, daz), (danw,) = _a_out.bwd([(s["of"], s["orv"]), proj], [p["a_norm_w"][None, :]], [dy[0]],
                                      [0, C_AZ // 512], bf16_rows=(1,))
    g = _delta_bwd_call(s["q"], s["k"], s["v"], *gates, s["ss"], s["ts"], do_a, ride=rides.get("delta"))
    rode["delta"] = g[12:]
    dgates = ((g[6], g[7]), (g[8], g[9]), (g[10], g[11]))
    dab, da_log, ddt = gates_vjp(dgates)
    dpre, dconv = [], []
    for (col, nrm, scl, nm), df, dr in zip(A_PIECES, (g[0], g[2], g[4]), (g[1], g[3], g[5])):
        dx, dw = _a_prep_bwd(proj, conv8, col, nrm, scl, df, dr, nm)
        dpre.append(dx)
        dconv.append(dw[:A_CONV])

    dproj = jnp.concatenate(dpre + [daz, dbq, dbz, dcq, dck, dcv, dcz, dma, dmb, dmc, dbkv,
                                    jnp.pad(dab, ((0, 0), (0, IN_PAD - C_AB - 16))).astype(BF16)], axis=1)
    dh = _matmul(dproj, w16, "w_in_da", "nt", W_IN_TILES["nt"])
    dw = _matmul(h.T.astype(BF16), dproj, "w_in_db", "nn", W_IN_TILES["db"])
    dp = dict(a_conv_w=jnp.concatenate(dconv, axis=1), a_log=da_log, a_dt_bias=ddt, a_norm_w=danw[0],
              b_sink=dsink[:, 0], c_decay=dc_decay, c_norm_w=dcnw[0], w_branch=dwb)
    return dh, dw, dp, rode


CORE_PARAMS = ("a_conv_w", "a_log", "a_dt_bias", "a_norm_w", "b_sink", "c_decay", "c_norm_w", "w_branch")


W_IN_SHARD = IN_WIDTH // N_DEV
W_IN_RUNS = ((0, 2048, 0), (2064, 512, C_BQ), (2832, 512, C_BZ), (3344, 5120, C_CQ), (2576, 256, C_BKV),
             (2048, 16, C_AB))


def _shard_overlap(start, width, j):
    lo, hi = max(start, j * W_IN_SHARD), min(start + width, (j + 1) * W_IN_SHARD)
    return (lo, hi) if lo < hi else None


def _w_in_from_shards(g):
    parts = []
    for start, width, _ in W_IN_RUNS:
        for j in range(N_DEV):
            span = _shard_overlap(start, width, j)
            if span:
                parts.append(g[j, :, span[0] - j * W_IN_SHARD:span[1] - j * W_IN_SHARD])
    parts.append(jnp.zeros((D_MODEL, IN_PAD - IN_WIDTH), g.dtype))
    return jnp.concatenate(parts, axis=1)


def _w_in_blocks(dw):
    blocks = []
    for j in range(N_DEV):
        parts = []
        for start, width, pad in sorted(W_IN_RUNS):
            span = _shard_overlap(start, width, j)
            if span:
                parts.append(dw[:, pad + span[0] - start:pad + span[1] - start])
        blocks.append(jnp.concatenate(parts, axis=1))
    return jnp.stack(blocks)


LAYER_SHARDED = ("w_ada", "w_in", "w_branch", "w_out")


def _unshard_layer(name, g):
    if name == "w_branch":
        return g.transpose(1, 2, 0, 3).reshape(3, BR_WIDTH, D_MODEL)
    if name == "w_out":
        return g.reshape(D_MODEL, D_MODEL)
    return g.transpose(1, 0, 2).reshape(D_MODEL, -1)


def _reshard_layer(name, w):
    if name == "w_branch":
        return w.reshape(3, BR_WIDTH, N_DEV, D_MODEL // N_DEV).transpose(2, 0, 1, 3)
    if name == "w_out":
        return w.reshape(N_DEV, D_MODEL // N_DEV, D_MODEL)
    return w.reshape(D_MODEL, N_DEV, -1).transpose(1, 0, 2)


def _layer_weights(gathered):
    out = {n: _unshard_layer(n, g) for n, g in gathered.items() if n != "w_in"}
    out["w_in16"] = _w_in_from_shards(gathered["w_in"])
    return out


def _grad_blocks(name, g):
    return (_w_in_blocks(g) if name == "w_in" else _reshard_layer(name, g)).astype(BF16)


def _forward_backward(small, layer0, shards0, shards1, x, c, ctx, loss_target):
    c_ctx = small["c_ctx"]
    sc16 = jnp.zeros((16, D_MODEL), F32).at[0].set(_silu(c)).at[1].set(_silu(c_ctx))
    xs = jnp.concatenate([ctx, x], axis=0)
    weights = [dict(layer0), None]
    layers = []
    for l in range(DEPTH):
        wl = weights[l]
        mod16 = _matmul(sc16, wl["w_ada"], "ada") + small["b_ada"][l][None, :]
        mod_cx = jnp.stack([mod16[1], mod16[0]])
        shift, scale, gate = jnp.split(mod_cx, 3, axis=1)
        nw = small["norm_w"][l][None, :]
        (h,) = _norm_mod.fwd([xs], [nw, shift, scale])
        p = {n: small[n][l] for n in CORE_PARAMS if n != "w_branch"}
        p["w_branch"] = wl.get("w_branch")
        rides = {}
        if l == 0:
            rides = {"w_in": ([shards0["w_branch"], shards0["w_out"]], True), "delta": ([shards1["w_in"]], True),
                     "attn": ([shards1["w_ada"]], True), "ret": ([shards1["w_branch"], shards1["w_out"]], True)}
        merged, saved, rode = _core_forward(h, wl["w_in16"], p, rides)
        if l == 0:
            wl["w_out"] = _unshard_layer("w_out", rode["w_in"][1])
            weights[1] = _layer_weights(dict(w_in=rode["delta"][0], w_ada=rode["attn"][0],
                                             w_branch=rode["ret"][0], w_out=rode["ret"][1]))
        (xs_next,) = _out_residual.fwd([xs, merged], [wl["w_out"], gate])
        layers.append(dict(xs=xs, h=h, p=p, saved=saved, merged=merged, gate=gate, nw=nw, shift=shift, scale=scale))
        xs = xs_next
    fw = small["final_norm_w"][None, :]
    xs = xs[CTX_LEN:]
    (per_row,) = _loss_rows.fwd([xs, loss_target], [fw])
    loss = jnp.sum(per_row[:, 0])

    d_per_row = jnp.zeros((SEQ, 128), F32).at[:, 0].set(1.0)
    (dxs,), (dfw,) = _loss_rows.bwd([xs, loss_target], [fw], [d_per_row])
    dxs = jnp.pad(dxs, ((CTX_LEN, 0), (0, 0)))
    small_names = tuple(n for n in CORE_PARAMS if n != "w_branch") + ("b_ada", "norm_w")
    dsmall = {n: [None] * DEPTH for n in small_names}
    dlayer = [None] * DEPTH
    contrib0 = contrib1 = None
    dsc16 = jnp.zeros((16, D_MODEL), F32)
    for l in reversed(range(DEPTH)):
        s, wl = layers[l], weights[l]
        (dres, dmerged), (dw_out, dgate) = _out_residual.bwd([s["xs"], s["merged"]], [wl["w_out"], s["gate"]], [dxs])
        rides = {}
        if l == 0:
            blocks1 = {n: _grad_blocks(n, g) for n, g in dlayer[1].items()}
            rides = {"ret": ([blocks1["w_branch"], blocks1["w_out"]], False),
                     "attn": ([_reshard_layer("w_out", dw_out).astype(BF16), blocks1["w_ada"]], False),
                     "delta": ([blocks1["w_in"]], False)}
        dh, dw_in, dp, rode = _core_backward(s["h"], wl["w_in16"], s["p"], s["saved"], dmerged, rides,
                                             branch_rides_in_attn=(l == 0))
        if l == 0:
            contrib1 = dict(w_in=rode["delta"][0], w_ada=rode["attn"][1], w_branch=rode["ret"][0],
                            w_out=rode["ret"][1])
            contrib0 = dict(w_out=rode["attn"][0], w_branch=rode["attn"][2])
        (dxn,), (dnw, dshift, dscale) = _norm_mod.bwd([s["xs"]], [s["nw"], s["shift"], s["scale"]], [dh])
        dxs = dres + dxn
        dmod_cx = jnp.concatenate([dshift, dscale, dgate], axis=1)
        dmod16 = jnp.zeros((16, 3 * D_MODEL), F32).at[0].set(dmod_cx[1]).at[1].set(dmod_cx[0])
        dsc16 = dsc16 + _matmul(dmod16, wl["w_ada"], "ada_da", "nt")
        dlayer[l] = dict(w_ada=_matmul(sc16, dmod16, "ada_db", "tn"), w_in=dw_in,
                         w_branch=dp["w_branch"], w_out=dw_out)
        for n in small_names:
            if n in dp:
                dsmall[n][l] = dp[n]
        dsmall["norm_w"][l] = dnw[0]
        dsmall["b_ada"][l] = dmod_cx[0] + dmod_cx[1]
    gsmall = {n: jnp.stack(v) for n, v in dsmall.items()}
    gsmall["final_norm_w"] = dfw[0]
    sig = jax.nn.sigmoid(c_ctx)
    gsmall["c_ctx"] = dsc16[1] * sig * (1.0 + c_ctx * (1.0 - sig))
    return loss, dxs[CTX_LEN:], gsmall, {n: dlayer[0][n] for n in ("w_ada", "w_in")}, contrib0, contrib1


SMALL = ("c_ctx", "b_ada", "norm_w", "a_log", "a_dt_bias", "a_norm_w", "b_sink", "c_decay", "c_norm_w",
         "final_norm_w")
WEIGHTS = ("c_ctx", "w_ada", "b_ada", "norm_w", "w_in", "a_conv_w", "a_log", "a_dt_bias", "a_norm_w", "b_sink",
           "c_decay", "c_norm_w", "w_branch", "w_out", "final_norm_w")
SMALL_PACK = 12288


def _unshard_conv(g):
    return g.transpose(1, 2, 0, 3).reshape(DEPTH, A_CONV, 3 * A_WIDTH)


def _reshard_conv(w):
    return w.reshape(DEPTH, A_CONV, N_DEV, 3 * A_WIDTH // N_DEV).transpose(2, 0, 1, 3)


def _pack_small(tree):
    flat = jnp.concatenate([tree[n].reshape(-1) for n in SMALL])
    return jnp.pad(flat, (0, SMALL_PACK - flat.shape[0])).reshape(SMALL_PACK // 128, 128)


def _unpack_small(packed, like):
    flat = packed.reshape(-1)
    out, off = {}, 0
    for n in SMALL:
        size = math.prod(like[n].shape)
        out[n] = flat[off:off + size].reshape(like[n].shape)
        off += size
    return out


def kernel(x, c, ctx, c_ctx, w_ada, b_ada, norm_w, w_in, a_conv_w, a_log, a_dt_bias, a_norm_w, b_sink, c_decay, c_norm_w, w_branch, w_out, final_norm_w, loss_target, m_c_ctx, m_w_ada, m_b_ada, m_norm_w, m_w_in, m_a_conv_w, m_a_log, m_a_dt_bias, m_a_norm_w, m_b_sink, m_c_decay, m_c_norm_w, m_w_branch, m_w_out, m_final_norm_w, v_c_ctx, v_w_ada, v_b_ada, v_norm_w, v_w_in, v_a_conv_w, v_a_log, v_a_dt_bias, v_a_norm_w, v_b_sink, v_c_decay, v_c_norm_w, v_w_branch, v_w_out, v_final_norm_w):
    w = dict(c_ctx=c_ctx, w_ada=w_ada, b_ada=b_ada, norm_w=norm_w, w_in=w_in, a_conv_w=a_conv_w, a_log=a_log,
             a_dt_bias=a_dt_bias, a_norm_w=a_norm_w, b_sink=b_sink, c_decay=c_decay, c_norm_w=c_norm_w,
             w_branch=w_branch, w_out=w_out, final_norm_w=final_norm_w)
    m = dict(c_ctx=m_c_ctx, w_ada=m_w_ada, b_ada=m_b_ada, norm_w=m_norm_w, w_in=m_w_in, a_conv_w=m_a_conv_w,
             a_log=m_a_log, a_dt_bias=m_a_dt_bias, a_norm_w=m_a_norm_w, b_sink=m_b_sink, c_decay=m_c_decay,
             c_norm_w=m_c_norm_w, w_branch=m_w_branch, w_out=m_w_out, final_norm_w=m_final_norm_w)
    v = dict(c_ctx=v_c_ctx, w_ada=v_w_ada, b_ada=v_b_ada, norm_w=v_norm_w, w_in=v_w_in, a_conv_w=v_a_conv_w,
             a_log=v_a_log, a_dt_bias=v_a_dt_bias, a_norm_w=v_a_norm_w, b_sink=v_b_sink, c_decay=v_c_decay,
             c_norm_w=v_c_norm_w, w_branch=v_w_branch, w_out=v_w_out, final_norm_w=v_final_norm_w)

    shards = {n: w[n].astype(BF16) for n in LAYER_SHARDED}
    first = _exchange([shards["w_ada"][0], shards["w_in"][0], w["a_conv_w"]], True, "gather_layer0")
    layer0 = _layer_weights(dict(w_ada=first[0], w_in=first[1]))
    small_w = {n: w[n] for n in SMALL}
    small_w["a_conv_w"] = _unshard_conv(first[2])
    loss, gx, gw, glayer0, contrib0, contrib1 = _forward_backward(
        small_w, layer0, {n: shards[n][0] for n in ("w_branch", "w_out")}, {n: shards[n][1] for n in LAYER_SHARDED},
        x[0], c[0], ctx[0], loss_target[0])
    loss = lax.psum(loss, ("x", "y", "c"))

    last = _scatter_two_level([_reshard_layer("w_ada", glayer0["w_ada"]).astype(BF16),
                               _grad_blocks("w_in", glayer0["w_in"]), _reshard_conv(gw["a_conv_w"])],
                              "scatter_layer0")
    contrib0["w_ada"], contrib0["w_in"] = last[0], last[1]
    small = _exchange([_pack_small(gw)], True, "gather_small_grads")[0]

    grad, delta, new_m, new_v = {}, {}, {}, {}
    for n in LAYER_SHARDED:
        shp = w[n].shape
        per_layer = (math.prod(shp[1:-1]), shp[-1])
        outs = _adamw_layers(*[a.reshape((DEPTH,) + per_layer) for a in (w[n], m[n], v[n])],
                             *[cb.reshape(cb.shape[:1] + per_layer) for cb in (contrib0[n], contrib1[n])], "adamw_" + n)
        grad[n], delta[n], new_m[n], new_v[n] = [o.reshape(shp) for o in outs]
    shp = a_conv_w.shape
    two_d = (math.prod(shp[:-1]), shp[-1])
    outs = _adamw(*[a.reshape(two_d) for a in (a_conv_w, m_a_conv_w, v_a_conv_w)],
                  last[2].reshape(last[2].shape[:1] + two_d), "adamw_a_conv_w")
    grad["a_conv_w"], delta["a_conv_w"], new_m["a_conv_w"], new_v["a_conv_w"] = [o.reshape(shp) for o in outs]
    outs = _adamw(_pack_small(w), _pack_small(m), _pack_small(v), small, "adamw_small")
    for tree, packed in zip((grad, delta, new_m, new_v), outs):
        tree.update(_unpack_small(packed, w))

    return (loss, gx[None], *[grad[n] for n in WEIGHTS], *[delta[n] for n in WEIGHTS],
            *[new_m[n] for n in WEIGHTS], *[new_v[n] for n in WEIGHTS])
```

```python
import functools
import math

import jax
import jax.numpy as jnp
from jax import lax
from jax.experimental import pallas as pl
from jax.experimental.pallas import tpu as pltpu

F32 = jnp.float32
BF16 = jnp.bfloat16
HIGHEST = lax.Precision.HIGHEST

D_MODEL = 1024
SEQ = 4096
DEPTH = 2
GRID_W = 64
CTX_LEN = 256
EPS = 1e-6
ROPE_BASE = 10000.0
BR_WIDTH = D_MODEL // 2
A_DK = 128
A_HEADS = 4
A_WIDTH = 512
A_CONV = 5
B_HD = 64
B_Q_HEADS = 8
B_KV_HEADS = 2
WINDOW = 128
B_BLOCK = 128
C_HD = 128
C_HEADS = 4
C_WIDTH = 512
CHUNK = 128
RET_CHUNK = 256
ADAM_LR = 0.001
ADAM_B1 = 0.9
ADAM_B2 = 0.999
ADAM_EPS = 1e-08
ADAM_WD = 0.01
ADAM_STEP = 10

N_DEV = 8
ROWS = CTX_LEN + SEQ
N_CHUNK = ROWS // CHUNK
IN_WIDTH = 8464
IN_PAD = 8704
NEG = -1e30

VMEM_LIMIT = 56 * 1024 * 1024
MESH = pl.DeviceIdType.MESH

C_AQ, C_AK, C_AV, C_AZ, C_BQ, C_BZ, C_CQ, C_CK, C_CV, C_CZ = (i * 512 for i in range(10))
C_MERGE = 5120
C_BKV = 8192
C_AB = 8448


def _cparams(sem=None):
    if sem is None:
        return pltpu.CompilerParams(vmem_limit_bytes=VMEM_LIMIT)
    return pltpu.CompilerParams(dimension_semantics=sem, vmem_limit_bytes=VMEM_LIMIT)


def _dg(a, b, ca, cb, prec=None):
    return lax.dot_general(a, b, (((ca,), (cb,)), ((), ())), preferred_element_type=F32, precision=prec)


@functools.partial(jax.custom_vjp, nondiff_argnums=(2, 3))
def _bdot(a, b, ca, cb):
    return _dg(a.astype(BF16), b.astype(BF16), ca, cb)


def _bdot_fwd(a, b, ca, cb):
    return _bdot(a, b, ca, cb), (a, b)


def _bdot_bwd(ca, cb, res, ct):
    a, b = res
    da = _bdot(ct, b, 1, 1 - cb) if ca == 1 else _bdot(b, ct, 1 - cb, 1)
    db = _bdot(a, ct, 1 - ca, 0) if cb == 0 else _bdot(ct, a, 0, 1 - ca)
    return da, db


_bdot.defvjp(_bdot_fwd, _bdot_bwd)


def _hdot(a, b):
    return _dg(a, b, 1, 0, lax.Precision.HIGH)


def _k_silu(x):
    return x / (1.0 + jnp.exp(-x))


def _k_sigmoid(x):
    return 1.0 / (1.0 + jnp.exp(-x))


@jax.custom_vjp
def _swap64(x):
    return pltpu.roll(x, 64, 1)


_swap64.defvjp(lambda x: (pltpu.roll(x, 64, 1), None), lambda _, ct: (pltpu.roll(ct, 64, 1),))


def _swap16_impl(x):
    lane = lax.broadcasted_iota(jnp.int32, x.shape, 1)
    return jnp.where((lane & 16) == 0, pltpu.roll(x, 112, 1), pltpu.roll(x, 16, 1))


@jax.custom_vjp
def _swap16(x):
    return _swap16_impl(x)


_swap16.defvjp(lambda x: (_swap16_impl(x), None), lambda _, ct: (_swap16_impl(ct),))


def _pick(dim, prefs):
    for p in prefs:
        if dim % p == 0:
            return p
    return dim


def _matmul(a, b, name, mode="nn", tiles=None, ride=None):
    ca, cb = {"nn": (1, 0), "nt": (1, 1), "tn": (0, 0)}[mode]
    m, k = a.shape[1 - ca], a.shape[ca]
    n = b.shape[1 - cb]
    if tiles is None:
        tiles = (_pick(m, (1088, 1024, 512, 256, 128)), _pick(n, (512, 256, 128)),
                 _pick(k, (1088, 1024, 512, 256, 128) if mode == "tn" else (2176, 2048, 1024, 512, 256, 128)))
    tm, tn, tk = tiles
    nk = k // tk
    a_spec = (pl.BlockSpec((tm, tk), lambda i, j, kk: (i, kk)) if ca == 1
              else pl.BlockSpec((tk, tm), lambda i, j, kk: (kk, i)))
    b_spec = (pl.BlockSpec((tk, tn), lambda i, j, kk: (kk, j)) if cb == 0
              else pl.BlockSpec((tn, tk), lambda i, j, kk: (j, kk)))

    def body(a_ref, b_ref, o_ref):
        part = _dg(a_ref[...].astype(BF16), b_ref[...].astype(BF16), ca, cb)
        if nk == 1:
            o_ref[...] = part
        else:
            kk = pl.program_id(2)

            @pl.when(kk == 0)
            def _():
                o_ref[...] = part

            @pl.when(kk > 0)
            def _():
                o_ref[...] += part

    grid = (m // tm, n // tn, nk)
    if ride is None:
        return pl.pallas_call(
            body,
            grid=grid,
            in_specs=[a_spec, b_spec],
            out_specs=pl.BlockSpec((tm, tn), lambda i, j, kk: (i, j)),
            out_shape=jax.ShapeDtypeStruct((m, n), F32),
            compiler_params=_cparams(("parallel", "parallel", "arbitrary")),
            name=name,
        )(a, b)
    body, r_in, r_out, r_shape, r_scratch = _riding(body, 2, 1, 0, ride, grid)
    return pl.pallas_call(
        body,
        grid=grid,
        in_specs=[a_spec, b_spec] + r_in,
        out_specs=[pl.BlockSpec((tm, tn), lambda i, j, kk: (i, j))] + r_out,
        out_shape=[jax.ShapeDtypeStruct((m, n), F32)] + r_shape,
        scratch_shapes=r_scratch,
        compiler_params=_cparams(("arbitrary", "arbitrary", "arbitrary")),
        name=name,
    )(a, b, *ride[0])


ROW_BLOCK = 256
ROW_VMEM_BUDGET = 16 * 1024 * 1024


def _pieces(val, pw):
    return [val[:, j * pw:(j + 1) * pw] for j in range(val.shape[1] // pw)]


def _flat(groups):
    arrays, sizes = [], []
    for g in groups:
        g = g if isinstance(g, (tuple, list)) else (g,)
        arrays += list(g)
        sizes.append(len(g))
    return arrays, sizes


def _regroup(refs, sizes):
    out, at = [], 0
    for n in sizes:
        val = refs[at][...]
        for r in refs[at + 1:at + n]:
            val = val + r[...]
        out.append(val)
        at += n
    return out


class _Rowwise:
    def __init__(self, fn, name, row_wpw, par_pw, out_wpw, n_diff=None, block_rows=None):
        self.fn, self.name, self.row_wpw, self.par_pw, self.out_wpw = fn, name, row_wpw, par_pw, out_wpw
        self.n_diff = len(row_wpw) if n_diff is None else n_diff
        self.block_rows = block_rows

    def _load(self, row_vals, par_refs, br, with_ctx):
        row = pl.program_id(0) * br + lax.broadcasted_iota(jnp.int32, (br, 1), 0)
        is_ctx = (row < (CTX_LEN if with_ctx else 0)).astype(F32)
        rows = [_pieces(v, pw) for v, (_, pw) in zip(row_vals, self.row_wpw)]
        pars = []
        for p, pw in zip(par_refs, self.par_pw):
            val = p[...].astype(F32)
            if p.shape[0] == 2:
                val = is_ctx * val[0:1, :] + (1.0 - is_ctx) * val[1:2, :]
            pars.append(_pieces(val, pw))
        return rows, pars, is_ctx

    def _block_rows(self, n_rows, widths):
        if self.block_rows:
            return self.block_rows
        for br in (1088, 1024, 544, 512, 272):
            if n_rows % br == 0 and 2 * 4 * br * sum(widths) <= ROW_VMEM_BUDGET:
                return br
        return ROW_BLOCK

    def _row_specs(self, br, sizes, cols):
        out = []
        for (w, _), n, c in zip(self.row_wpw, sizes, cols):
            out += [pl.BlockSpec((br, w), lambda i, c=c: (i, c))] * n
        return out

    def fwd(self, rows, params, cols=None):
        arrays, sizes = _flat(rows)
        cols = cols or [0] * len(rows)
        n_rows = arrays[0].shape[0]
        n_in = len(arrays)
        br = self._block_rows(n_rows, [w for (w, _), n in zip(self.row_wpw, sizes) for _ in range(n)]
                              + [w for w, _ in self.out_wpw])

        def body(*refs):
            r, p, _ = self._load(_regroup(refs[:n_in], sizes), refs[n_in:n_in + len(params)], br, n_rows == ROWS)
            for o_ref, pieces, (_, pw) in zip(refs[n_in + len(params):], self.fn(r, p), self.out_wpw):
                for j, piece in enumerate(pieces):
                    o_ref[:, j * pw:(j + 1) * pw] = piece

        return pl.pallas_call(
            body,
            grid=(n_rows // br,),
            in_specs=self._row_specs(br, sizes, cols) + [pl.BlockSpec(p.shape, lambda i: (0, 0)) for p in params],
            out_specs=[pl.BlockSpec((br, w), lambda i: (i, 0)) for w, _ in self.out_wpw],
            out_shape=[jax.ShapeDtypeStruct((n_rows, w), F32) for w, _ in self.out_wpw],
            compiler_params=_cparams(("parallel",)),
            name=self.name + "_fwd",
        )(*arrays, *params)

    def bwd(self, rows, params, douts, cols=None, bf16_rows=()):
        arrays, sizes = _flat(rows)
        darrays, dsizes = _flat(douts)
        cols = cols or [0] * len(rows)
        n_rows = arrays[0].shape[0]
        n_in, n_par, n_dout, n_diff = len(arrays), len(params), len(darrays), self.n_diff
        br = self._block_rows(n_rows, [w for (w, _), n in zip(self.row_wpw, sizes) for _ in range(n)]
                              + [w for (w, _), n in zip(self.out_wpw, dsizes) for _ in range(n)]
                              + [w for w, _ in self.row_wpw[:n_diff]])

        def body(*refs):
            par_refs = refs[n_in:n_in + n_par]
            dout_refs = refs[n_in + n_par:n_in + n_par + n_dout]
            drow_refs = refs[n_in + n_par + n_dout:n_in + n_par + n_dout + n_diff]
            dpar_refs = refs[n_in + n_par + n_dout + n_diff:]

            @pl.when(pl.program_id(0) == 0)
            def _():
                for d in dpar_refs:
                    d[...] = jnp.zeros_like(d)

            r, p, is_ctx = self._load(_regroup(refs[:n_in], sizes), par_refs, br, n_rows == ROWS)
            cts = [_pieces(d, pw) for d, (_, pw) in zip(_regroup(dout_refs, dsizes), self.out_wpw)]
            fixed = r[n_diff:]
            _, vjp = jax.vjp(lambda rd, pp: self.fn(rd + fixed, pp), r[:n_diff], p)
            dr, dp = vjp(cts)
            for d_ref, pieces, (_, pw) in zip(drow_refs, dr, self.row_wpw):
                for j, piece in enumerate(pieces):
                    d_ref[:, j * pw:(j + 1) * pw] = piece.astype(d_ref.dtype)
            for d_ref, pieces, pw in zip(dpar_refs, dp, self.par_pw):
                for j, piece in enumerate(pieces):
                    lanes = slice(j * pw, (j + 1) * pw)
                    if d_ref.shape[0] != 2:
                        d_ref[:, lanes] += piece
                    else:
                        d_ref[0:1, lanes] += jnp.sum(is_ctx * piece, axis=0, keepdims=True)
                        d_ref[1:2, lanes] += jnp.sum((1.0 - is_ctx) * piece, axis=0, keepdims=True)

        par_specs = [pl.BlockSpec(p.shape, lambda i: (0, 0)) for p in params]
        dout_specs = []
        for (w, _), n in zip(self.out_wpw, dsizes):
            dout_specs += [pl.BlockSpec((br, w), lambda i: (i, 0))] * n
        drow_w = [w for w, _ in self.row_wpw[:n_diff]]
        g = pl.pallas_call(
            body,
            grid=(n_rows // br,),
            in_specs=self._row_specs(br, sizes, cols) + par_specs + dout_specs,
            out_specs=[pl.BlockSpec((br, w), lambda i: (i, 0)) for w in drow_w] + par_specs,
            out_shape=[jax.ShapeDtypeStruct((n_rows, w), BF16 if a in bf16_rows else F32) for a, w in enumerate(drow_w)]
            + [jax.ShapeDtypeStruct(p.shape, F32) for p in params],
            compiler_params=_cparams(("arbitrary",)),
            name=self.name + "_bwd",
        )(*arrays, *params, *darrays)
        return list(g[:n_diff]), list(g[n_diff:])


def _fn_norm_mod(rows, pars):
    (x,), (nw,), (shift,), (scale,) = rows[0], pars[0], pars[1], pars[2]
    y = x * lax.rsqrt(jnp.mean(x * x, axis=-1, keepdims=True) + EPS) * nw
    return [[y * (1.0 + scale) + shift]]


def _fn_head_rms_gate(rows, pars):
    (w,) = pars[0]
    return [[o * lax.rsqrt(jnp.mean(o * o, axis=-1, keepdims=True) + EPS) * w * _k_silu(z)
             for o, z in zip(rows[0], rows[1])]]


def _fn_group_norm_gate(rows, pars):
    out = []
    for o, z, w in zip(rows[0], rows[1], pars[0]):
        mu = jnp.mean(o, axis=-1, keepdims=True)
        var = jnp.mean(jnp.square(o - mu), axis=-1, keepdims=True)
        out.append((o - mu) * lax.rsqrt(var + EPS) * w * _k_silu(z))
    return [out]


def _fn_mix_merge(rows, pars):
    oa, za, (ob,), (zb,), oc, zc, (ma,), (mb,), (mc,) = rows
    na, nc, (wa,), (wb,), (wc,) = pars
    ya = jnp.concatenate(_fn_head_rms_gate([oa, za], [na])[0], axis=1)
    yb = ob * _k_silu(zb)
    yc = jnp.concatenate(_fn_group_norm_gate([oc, zc], [nc])[0], axis=1)
    return [[_k_sigmoid(ma) * _bdot(ya, wa, 1, 0) + _k_sigmoid(mb) * _bdot(yb, wb, 1, 0)
             + _k_sigmoid(mc) * _bdot(yc, wc, 1, 0)]]


def _fn_out_residual(rows, pars):
    (res,), (merged,), (w,), (gate,) = rows[0], rows[1], pars[0], pars[1]
    return [[res + gate * _bdot(merged, w, 1, 0)]]


def _fn_loss(rows, pars):
    (x,), (target,), (w,) = rows[0], rows[1], pars[0]
    y = x * lax.rsqrt(jnp.mean(x * x, axis=-1, keepdims=True) + EPS) * w
    per_row = 0.5 * jnp.mean(jnp.square(y - target), axis=-1, keepdims=True)
    return [[jnp.broadcast_to(per_row, (per_row.shape[0], 128))]]


def _fn_b_rope(rows, pars):
    q, (k, v), (cos,), (sin,) = rows
    rot = lambda x: x * cos + _swap16(x) * sin
    return [[rot(x) for x in q], [rot(k), v]]


def _fn_c_rope(rows, pars):
    q, k, (cos,), (sin,) = rows
    rot = lambda x: x * cos + _swap64(x) * sin
    return [[rot(x) for x in q], [rot(x) * (C_HD ** -0.5) for x in k]]


_norm_mod = _Rowwise(_fn_norm_mod, "norm_mod", [(D_MODEL, D_MODEL)], [D_MODEL] * 3, [(D_MODEL, D_MODEL)])
_out_residual = _Rowwise(_fn_out_residual, "out_residual", [(D_MODEL, D_MODEL)] * 2, [D_MODEL] * 2,
                         [(D_MODEL, D_MODEL)])
_loss_rows = _Rowwise(_fn_loss, "loss", [(D_MODEL, D_MODEL)] * 2, [D_MODEL], [(128, 128)], n_diff=1)
_mix_merge = _Rowwise(_fn_mix_merge, "mix_merge",
                      [(512, 128), (512, 128), (512, 512), (512, 512), (512, 128), (512, 128)] + [(D_MODEL, D_MODEL)] * 3,
                      [128, 128] + [D_MODEL] * 3, [(D_MODEL, D_MODEL)], block_rows=256)
MIX_MERGE_COLS = [0, C_AZ // 512, 0, C_BZ // 512, 0, C_CZ // 512] + [C_MERGE // 1024 + i for i in range(3)]
_b_rope = _Rowwise(_fn_b_rope, "b_rope", [(512, 128), (256, 128), (128, 128), (128, 128)], [],
                   [(512, 128), (256, 128)], n_diff=2)
_c_rope = _Rowwise(_fn_c_rope, "c_rope", [(512, 128), (512, 128), (128, 128), (128, 128)], [],
                   [(512, 128), (512, 128)], n_diff=2)


HALO = 8
EXT = ROW_BLOCK + 2 * HALO


def _halo_specs(col, width=512):
    last = ROWS // HALO - 1
    per = ROW_BLOCK // HALO
    prev = pl.BlockSpec((HALO, width), lambda i: (jnp.maximum(i * per - 1, 0), col))
    cur = pl.BlockSpec((ROW_BLOCK, width), lambda i: (i, col))
    nxt = pl.BlockSpec((HALO, width), lambda i: (jnp.minimum((i + 1) * per, last), col))
    return [prev, cur, nxt]


def _extended(prev_ref, cur_ref, next_ref):
    i = pl.program_id(0)
    prev_ok = i >= 2
    next_ok = jnp.logical_and(i >= 1, i < ROWS // ROW_BLOCK - 1)
    return jnp.concatenate([jnp.where(prev_ok, prev_ref[...], 0.0), cur_ref[...],
                            jnp.where(next_ok, next_ref[...], 0.0)], axis=0)


def _conv_taps(x_ext, w_ref, flip):
    acc = None
    for j in range(A_CONV):
        shift = (j - 2) if flip else (2 - j)
        term = w_ref[j:j + 1, :] * pltpu.roll(x_ext, shift % EXT, 0)
        acc = term if acc is None else acc + term
    return acc


def _conv_post(pre_pieces, normalize, scale):
    out = []
    for p in pre_pieces:
        y = _k_silu(p)
        if normalize:
            y = y * lax.rsqrt(jnp.sum(y * y, axis=-1, keepdims=True) + EPS) * scale
        out.append(y)
    return out


def _a_prep_fwd(proj, conv8, col, normalize, scale, name):
    def body(prev_ref, cur_ref, next_ref, w_ref, o_ref):
        pre = _conv_taps(_extended(prev_ref, cur_ref, next_ref), w_ref, False)[HALO:HALO + ROW_BLOCK]
        for h, y in enumerate(_conv_post(_pieces(pre, 128), normalize, scale)):
            o_ref[:, h * 128:(h + 1) * 128] = y

    return pl.pallas_call(
        body,
        grid=(ROWS // ROW_BLOCK,),
        in_specs=_halo_specs(col) + [pl.BlockSpec((8, 512), lambda i: (0, col))],
        out_specs=pl.BlockSpec((ROW_BLOCK, 512), lambda i: (i, 0)),
        out_shape=jax.ShapeDtypeStruct((ROWS, 512), F32),
        compiler_params=_cparams(("parallel",)),
        name=name + "_fwd",
    )(proj, proj, proj, conv8)


def _a_prep_bwd(proj, conv8, col, normalize, scale, dout_f, dout_r, name):
    def body(xp, xc, xn, w_ref, fp, fc, fn_, rp, rc, rn, dx_ref, dw_ref):
        @pl.when(pl.program_id(0) == 0)
        def _():
            dw_ref[...] = jnp.zeros_like(dw_ref)

        x_ext = _extended(xp, xc, xn)
        dout = _extended(fp, fc, fn_) + _extended(rp, rc, rn)
        pre = _conv_taps(x_ext, w_ref, False)
        _, vjp = jax.vjp(lambda p: _conv_post(p, normalize, scale), _pieces(pre, 128))
        (dpre,) = vjp(_pieces(dout, 128))
        dpre = jnp.concatenate(dpre, axis=1)
        dx_ref[...] = _conv_taps(dpre, w_ref, True)[HALO:HALO + ROW_BLOCK].astype(BF16)
        own = dpre[HALO:HALO + ROW_BLOCK]
        for j in range(A_CONV):
            shifted = pltpu.roll(x_ext, (2 - j) % EXT, 0)[HALO:HALO + ROW_BLOCK]
            dw_ref[j:j + 1, :] += jnp.sum(own * shifted, axis=0, keepdims=True)

    return pl.pallas_call(
        body,
        grid=(ROWS // ROW_BLOCK,),
        in_specs=_halo_specs(col) + [pl.BlockSpec((8, 512), lambda i: (0, col))] + _halo_specs(0) + _halo_specs(0),
        out_specs=[pl.BlockSpec((ROW_BLOCK, 512), lambda i: (i, 0)), pl.BlockSpec((8, 512), lambda i: (0, 0))],
        out_shape=[jax.ShapeDtypeStruct((ROWS, 512), BF16), jax.ShapeDtypeStruct((8, 512), F32)],
        compiler_params=_cparams(("arbitrary",)),
        name=name + "_bwd",
    )(proj, proj, proj, conv8, dout_f, dout_f, dout_f, dout_r, dout_r, dout_r)


N_CHAIN = 8


def _rev_chunk(s, chunk):
    n_ctx, n_all = CTX_LEN // chunk, ROWS // chunk
    return jnp.where(s < n_ctx, n_ctx - 1 - s, n_all + n_ctx - 1 - s)


def _scan_specs(step_of, chunk, v_col=0):
    cf = step_of
    cr = lambda n: _rev_chunk(step_of(n), chunk)

    def pair(shape, index):
        return (pl.BlockSpec(shape, lambda n: index(cf(n))), pl.BlockSpec(shape, lambda n: index(cr(n))))

    return dict(
        tok=pair((chunk, 512), lambda c: (c, 0)),
        tokv=pair((chunk, 512), lambda c: (c, v_col)),
        row=pair((4, 1, 1, chunk), lambda c: (0, c, 0, 0)),
        one=pair((4, 1, 1, 1), lambda c: (0, c, 0, 0)),
        state=pair((None, 4, 128, 128), lambda c: (c, 0, 0, 0)),
        tinv=pair((None, 4, chunk, chunk), lambda c: (c, 0, 0, 0)),
    )


def _both(specs, kinds):
    out = []
    for kind in kinds:
        out += list(specs[kind])
    return out


def _scan_call(body, name, in_specs, out_specs, out_shape, operands, ride, chunk):
    grid = (ROWS // chunk,)
    body, r_in, r_out, r_shape, r_scratch = _riding(body, len(in_specs), len(out_specs), 1, ride, grid)
    return pl.pallas_call(
        body,
        grid=grid,
        in_specs=in_specs + r_in,
        out_specs=out_specs + r_out,
        out_shape=out_shape + r_shape,
        scratch_shapes=[pltpu.VMEM((N_CHAIN, 128, 128), F32)] + r_scratch,
        compiler_params=_cparams(("arbitrary",)),
        name=name,
    )(*operands, *(ride[0] if ride else []))


def _chain_masks():
    ii = lax.broadcasted_iota(jnp.int32, (CHUNK, CHUNK), 0)
    jj = lax.broadcasted_iota(jnp.int32, (CHUNK, CHUNK), 1)
    eye = jnp.where(ii == jj, 1.0, 0.0).astype(F32)
    lower = (ii >= jj, ii > jj)
    upper = (ii <= jj, ii < jj)
    return [lower] * 4 + [upper] * 4, eye


INV_BLOCK = 64


def _series_inverse(ls):
    ii = lax.broadcasted_iota(jnp.int32, (INV_BLOCK, INV_BLOCK), 0)
    jj = lax.broadcasted_iota(jnp.int32, (INV_BLOCK, INV_BLOCK), 1)
    eye = jnp.where(ii == jj, 1.0, 0.0).astype(F32)
    doublings = INV_BLOCK.bit_length() - 2
    xs = [eye - l for l in ls]
    ps = [_hdot(l, l) for l in ls]
    for i in range(doublings):
        xs = [x + _hdot(x, p) for x, p in zip(xs, ps)]
        if i < doublings - 1:
            ps = [_hdot(p, p) for p in ps]
    return xs


def _tri_inv_all(ls, upper):
    size = ls[0].shape[0]
    if size == INV_BLOCK:
        return _series_inverse(ls)
    n, h = len(ls), size // 2
    diag = _tri_inv_all([l[:h, :h] for l in ls] + [l[h:, h:] for l in ls], list(upper) * 2)
    out = []
    zero = jnp.zeros((h, h), F32)
    for i, l in enumerate(ls):
        a, d = diag[i], diag[n + i]
        if upper[i]:
            off = -_hdot(_hdot(a, l[:h, h:]), d)
            out.append(jnp.concatenate([jnp.concatenate([a, off], axis=1), jnp.concatenate([zero, d], axis=1)], axis=0))
        else:
            off = -_hdot(_hdot(d, l[h:, :h]), a)
            out.append(jnp.concatenate([jnp.concatenate([a, zero], axis=1), jnp.concatenate([off, d], axis=1)], axis=0))
    return out


@jax.custom_vjp
def _inv_saved(l, x):
    return x


def _inv_saved_fwd(l, x):
    return x, x


def _inv_saved_bwd(x, dx):
    return -_bdot(x, _bdot(dx, x, 1, 1), 0, 0), jnp.zeros_like(x)


_inv_saved.defvjp(_inv_saved_fwd, _inv_saved_bwd)


def _delta_chains(q, k, v, beta_r, gcr, gl, s, masks, eye, tinv_saved):
    n = range(len(q))
    beta = [jnp.sum(eye * beta_r[i], axis=1, keepdims=True) for i in n]
    gcc = [jnp.sum(eye * gcr[i], axis=1, keepdims=True) for i in n]
    decay = [jnp.exp(jnp.where(masks[i][0], gcc[i] - gcr[i], NEG)) for i in n]
    kb = [k[i] * beta[i] for i in n]
    lmat = [jnp.where(masks[i][1], _bdot(kb[i], k[i], 1, 1) * decay[i], 0.0) for i in n]
    if tinv_saved is None:
        tinv = _tri_inv_all(lmat, [i >= 4 for i in n])
    else:
        tinv = [_inv_saved(lmat[i], tinv_saved[i]) for i in n]
    eg = [jnp.exp(gcc[i]) for i in n]
    u = [_bdot(tinv[i], v[i] * beta[i], 1, 0) for i in n]
    w = [_bdot(tinv[i], kb[i] * eg[i], 1, 0) for i in n]
    qk = [_bdot(q[i], k[i], 1, 1) * decay[i] for i in n]
    v_new = [u[i] - _bdot(w[i], s[i], 1, 0) for i in n]
    o = [_bdot(q[i] * eg[i], s[i], 1, 0) + _bdot(qk[i], v_new[i], 1, 0) for i in n]
    s_new = [s[i] * jnp.exp(gl[i]) + _bdot(k[i] * jnp.exp(gl[i] - gcc[i]), v_new[i], 0, 0) for i in n]
    return (o, s_new), tinv


def _chain_loads(tok_pairs, small_pairs):
    toks = [[pair[i // 4][:, (i % 4) * 128:(i % 4 + 1) * 128] for i in range(N_CHAIN)] for pair in tok_pairs]
    smalls = [[pair[i // 4][i % 4] for i in range(N_CHAIN)] for pair in small_pairs]
    return toks, smalls


def _delta_fwd_call(q, k, v, beta, gc, gl, ride=None):
    sp = _scan_specs(lambda n: n, CHUNK)

    def body(qf, qr, kf, kr, vf, vr, bf, br, gcrf, gcrr, glf, glr, of, orv, ssf, ssr, tsf, tsr, s_scr):
        @pl.when(pl.program_id(0) == 0)
        def _():
            s_scr[...] = jnp.zeros_like(s_scr)

        masks, eye = _chain_masks()
        (qs, ks, vs), _ = _chain_loads([(qf, qr), (kf, kr), (vf, vr)], [])
        bs = [(bf, br)[i // 4][i % 4, 0] for i in range(N_CHAIN)]
        gcrs = [(gcrf, gcrr)[i // 4][i % 4, 0] for i in range(N_CHAIN)]
        gls = [(glf, glr)[i // 4][i % 4, 0] for i in range(N_CHAIN)]
        ss = [s_scr[i] for i in range(N_CHAIN)]
        (o, s_new), tinv = _delta_chains(qs, ks, vs, bs, gcrs, gls, ss, masks, eye, None)
        for i in range(N_CHAIN):
            d, h = i // 4, i % 4
            (ssf, ssr)[d][h] = ss[i]
            (tsf, tsr)[d][h] = tinv[i]
            (of, orv)[d][:, h * 128:(h + 1) * 128] = o[i]
            s_scr[i] = s_new[i]

    return _scan_call(
        body, "delta_fwd",
        _both(sp, ["tok", "tok", "tok", "row", "row", "one"]),
        _both(sp, ["tok", "state", "tinv"]),
        [jax.ShapeDtypeStruct((ROWS, 512), F32)] * 2 + [jax.ShapeDtypeStruct((N_CHUNK, 4, 128, 128), F32)] * 2
        + [jax.ShapeDtypeStruct((N_CHUNK, 4, CHUNK, CHUNK), F32)] * 2,
        [q, q, k, k, v, v, *beta, *gc, *gl], ride, CHUNK)


def _delta_bwd_call(q, k, v, beta, gc, gl, ssave, tsave, do, ride=None):
    sp = _scan_specs(lambda n: N_CHUNK - 1 - n, CHUNK)

    def body(qf, qr, kf, kr, vf, vr, bf, br, gcrf, gcrr, glf, glr, ssf, ssr, tsf, tsr, dof, dor,
             dqf, dqr, dkf, dkr, dvf, dvr, dbf, dbr, dgcrf, dgcrr, dglf, dglr, ds_scr):
        @pl.when(pl.program_id(0) == 0)
        def _():
            ds_scr[...] = jnp.zeros_like(ds_scr)

        masks, eye = _chain_masks()
        (qs, ks, vs, dos), (ss, ts) = _chain_loads(
            [(qf, qr), (kf, kr), (vf, vr), (dof, dor)], [(ssf, ssr), (tsf, tsr)])
        bs = [(bf, br)[i // 4][i % 4, 0] for i in range(N_CHAIN)]
        gcrs = [(gcrf, gcrr)[i // 4][i % 4, 0] for i in range(N_CHAIN)]
        gls = [(glf, glr)[i // 4][i % 4, 0] for i in range(N_CHAIN)]
        fn = lambda *a: _delta_chains(*a, masks, eye, ts)
        _, vjp, _ = jax.vjp(fn, qs, ks, vs, bs, gcrs, gls, ss, has_aux=True)
        dq, dk, dv, db, dgcr, dgl, ds = vjp((dos, [ds_scr[i] for i in range(N_CHAIN)]))
        for i in range(N_CHAIN):
            d, h = i // 4, i % 4
            hs = slice(h * 128, (h + 1) * 128)
            (dqf, dqr)[d][:, hs] = dq[i]
            (dkf, dkr)[d][:, hs] = dk[i]
            (dvf, dvr)[d][:, hs] = dv[i]
            (dbf, dbr)[d][h, 0] = db[i]
            (dgcrf, dgcrr)[d][h, 0] = dgcr[i]
            (dglf, dglr)[d][h, 0] = dgl[i]
            ds_scr[i] = ds[i]

    tok = jax.ShapeDtypeStruct((ROWS, 512), F32)
    return _scan_call(
        body, "delta_bwd",
        _both(sp, ["tok", "tok", "tok", "row", "row", "one", "state", "tinv", "tok"]),
        _both(sp, ["tok", "tok", "tok", "row", "row", "one"]),
        [tok] * 6 + [jax.ShapeDtypeStruct((4, N_CHUNK, 1, CHUNK), F32)] * 4
        + [jax.ShapeDtypeStruct((4, N_CHUNK, 1, 1), F32)] * 2,
        [q, q, k, k, v, v, *beta, *gc, *gl, *ssave, *tsave, do, do], ride, CHUNK)


def _ret_chains(q, k, v, dm, qs, ks, cd, s):
    n = range(len(q))
    a = [_bdot(q[i], k[i], 1, 1) * dm[i] for i in n]
    o = [_bdot(a[i], v[i], 1, 0) + _bdot(q[i] * qs[i], s[i], 1, 0) for i in n]
    s_new = [s[i] * cd[i] + _bdot(k[i] * ks[i], v[i], 0, 0) for i in n]
    return o, s_new


RET_CONST_SHAPES = ((N_CHAIN, RET_CHUNK, RET_CHUNK), (N_CHAIN, RET_CHUNK, 1), (N_CHAIN, RET_CHUNK, 1), (N_CHAIN, 1, 1))


def _ret_const_specs():
    return [pl.BlockSpec(shape, lambda n: (0, 0, 0)) for shape in RET_CONST_SHAPES]


def _ret_fwd_call(q, k, v, v_col, dm, qs, ks, cd, ride=None):
    sp = _scan_specs(lambda n: n, RET_CHUNK, v_col)

    def body(qf, qr, kf, kr, vf, vr, dm_ref, qs_ref, ks_ref, cd_ref, of, orv, ssf, ssr, s_scr):
        @pl.when(pl.program_id(0) == 0)
        def _():
            s_scr[...] = jnp.zeros_like(s_scr)

        (qc, kc, vc), _ = _chain_loads([(qf, qr), (kf, kr), (vf, vr)], [])
        ss = [s_scr[i] for i in range(N_CHAIN)]
        consts = [[r[i] for i in range(N_CHAIN)] for r in (dm_ref, qs_ref, ks_ref, cd_ref)]
        o, s_new = _ret_chains(qc, kc, vc, *consts, ss)
        for i in range(N_CHAIN):
            d, h = i // 4, i % 4
            (ssf, ssr)[d][h] = ss[i]
            (of, orv)[d][:, h * 128:(h + 1) * 128] = o[i]
            s_scr[i] = s_new[i]

    return _scan_call(
        body, "ret_fwd",
        _both(sp, ["tok", "tok", "tokv"]) + _ret_const_specs(),
        _both(sp, ["tok", "state"]),
        [jax.ShapeDtypeStruct((ROWS, 512), F32)] * 2
        + [jax.ShapeDtypeStruct((ROWS // RET_CHUNK, 4, 128, 128), F32)] * 2,
        [q, q, k, k, v, v, dm, qs, ks, cd], ride, RET_CHUNK)


def _ret_bwd_call(q, k, v, v_col, dm, qs, ks, cd, ssave, do, ride=None):
    sp = _scan_specs(lambda n: ROWS // RET_CHUNK - 1 - n, RET_CHUNK, v_col)

    def body(qf, qr, kf, kr, vf, vr, dm_ref, qs_ref, ks_ref, cd_ref, ssf, ssr, dof, dor,
             dqf, dqr, dkf, dkr, dvf, dvr, ddm_ref, dqs_ref, dks_ref, dcd_ref, ds_scr):
        @pl.when(pl.program_id(0) == 0)
        def _():
            ds_scr[...] = jnp.zeros_like(ds_scr)
            ddm_ref[...] = jnp.zeros_like(ddm_ref)
            dqs_ref[...] = jnp.zeros_like(dqs_ref)
            dks_ref[...] = jnp.zeros_like(dks_ref)
            dcd_ref[...] = jnp.zeros_like(dcd_ref)

        (qc, kc, vc, dos), (ss,) = _chain_loads([(qf, qr), (kf, kr), (vf, vr), (dof, dor)], [(ssf, ssr)])
        consts = [[r[i] for i in range(N_CHAIN)] for r in (dm_ref, qs_ref, ks_ref, cd_ref)]
        _, vjp = jax.vjp(_ret_chains, qc, kc, vc, *consts, ss)
        dq, dk, dv, ddm, dqs, dks, dcd, ds = vjp((dos, [ds_scr[i] for i in range(N_CHAIN)]))
        for i in range(N_CHAIN):
            d, h = i // 4, i % 4
            hs = slice(h * 128, (h + 1) * 128)
            (dqf, dqr)[d][:, hs] = dq[i]
            (dkf, dkr)[d][:, hs] = dk[i]
            (dvf, dvr)[d][:, hs] = dv[i]
            ddm_ref[i] += ddm[i]
            dqs_ref[i] += dqs[i]
            dks_ref[i] += dks[i]
            dcd_ref[i] += dcd[i]
            ds_scr[i] = ds[i]

    tok = jax.ShapeDtypeStruct((ROWS, 512), F32)
    return _scan_call(
        body, "ret_bwd",
        _both(sp, ["tok", "tok", "tokv"]) + _ret_const_specs() + _both(sp, ["state", "tok"]),
        _both(sp, ["tok", "tok", "tok"]) + _ret_const_specs(),
        [tok] * 6 + [jax.ShapeDtypeStruct(shape, F32) for shape in RET_CONST_SHAPES],
        [q, q, k, k, v, v, dm, qs, ks, cd, *ssave, do, do], ride, RET_CHUNK)


N_QBLK = ROWS // B_BLOCK
CTX_QBLK = CTX_LEN // B_BLOCK


def _attn_heads(q, kc, vc, kw, vw, sink, valid):
    n = range(len(q))
    qs = [q[i] * (B_HD ** -0.5) for i in n]
    s_c = [_bdot(qs[i], kc[i], 1, 1) for i in n]
    s_w = [jnp.where(valid, _bdot(qs[i], kw[i], 1, 1), NEG) for i in n]
    m = [lax.stop_gradient(jnp.maximum(jnp.maximum(jnp.max(s_c[i], axis=-1, keepdims=True), sink[i]),
                                       jnp.max(s_w[i], axis=-1, keepdims=True))) for i in n]
    e_c = [jnp.exp(s_c[i] - m[i]) for i in n]
    e_w = [jnp.exp(s_w[i] - m[i]) for i in n]
    den = [jnp.sum(e_c[i], axis=-1, keepdims=True) + jnp.sum(e_w[i], axis=-1, keepdims=True)
           + jnp.exp(sink[i] - m[i]) for i in n]
    return [(_bdot(e_c[i], vc[i], 1, 0) + _bdot(e_w[i], vw[i], 1, 0)) / den[i] for i in n]


def _attn_loads(q_ref, kv_ref, sink_ref, start):
    q, kc, vc, kw, vw, sink = [], [], [], [], [], []
    for hk in range(B_KV_HEADS):
        ks = slice(hk * B_HD, (hk + 1) * B_HD)
        vs = slice(128 + hk * B_HD, 128 + (hk + 1) * B_HD)
        grp = (kv_ref[0:CTX_LEN, ks], kv_ref[0:CTX_LEN, vs],
               kv_ref[pl.ds(start, 3 * B_BLOCK), ks], kv_ref[pl.ds(start, 3 * B_BLOCK), vs])
        for g in range(4):
            h = hk * 4 + g
            q.append(q_ref[:, h * B_HD:(h + 1) * B_HD])
            for lst, val in zip((kc, vc, kw, vw), grp):
                lst.append(val)
            sink.append(jnp.full((1, 1), sink_ref[h], F32))
    return q, kc, vc, kw, vw, sink


def _window(blk):
    xblk = blk - CTX_QBLK
    first = jnp.clip((xblk - 1) * B_BLOCK, 0, SEQ - 3 * B_BLOCK)
    qpos = xblk * B_BLOCK + lax.broadcasted_iota(jnp.int32, (B_BLOCK, 3 * B_BLOCK), 0)
    kpos = first + lax.broadcasted_iota(jnp.int32, (B_BLOCK, 3 * B_BLOCK), 1)
    far = jnp.where(blk >= CTX_QBLK, 0, 2 * SEQ)
    valid = jnp.abs(kpos - qpos) + far <= WINDOW
    return pl.multiple_of(first + CTX_LEN, B_BLOCK), valid


def _attn_specs():
    qspec = pl.BlockSpec((B_BLOCK, 512), lambda i: (i, 0))
    kvspec = pl.BlockSpec((ROWS, 256), lambda i: (0, 0))
    return qspec, kvspec, pl.BlockSpec(memory_space=pltpu.SMEM)


def _attn_fwd_call(q, kv, sink, ride=None):
    def body(q_ref, kv_ref, sink_ref, o_ref):
        start, valid = _window(pl.program_id(0))
        out = _attn_heads(*_attn_loads(q_ref, kv_ref, sink_ref, start), valid)
        for h in range(B_Q_HEADS):
            o_ref[:, h * B_HD:(h + 1) * B_HD] = out[h]

    qspec, kvspec, sspec = _attn_specs()
    body, r_in, r_out, r_shape, r_scratch = _riding(body, 3, 1, 0, ride, (N_QBLK,))
    return pl.pallas_call(
        body,
        grid=(N_QBLK,),
        in_specs=[qspec, kvspec, sspec] + r_in,
        out_specs=[qspec] + r_out,
        out_shape=[jax.ShapeDtypeStruct((ROWS, 512), F32)] + r_shape,
        scratch_shapes=r_scratch,
        compiler_params=_cparams(("arbitrary",)),
        name="attn_fwd",
    )(q, kv, sink, *(ride[0] if ride else []))


def _attn_bwd_call(q, kv, sink, do, ride=None):
    def body(q_ref, kv_ref, sink_ref, do_ref, dq_ref, dkv_ref, dsink_ref):
        @pl.when(pl.program_id(0) == 0)
        def _():
            dkv_ref[...] = jnp.zeros_like(dkv_ref)
            dsink_ref[...] = jnp.zeros_like(dsink_ref)

        start, valid = _window(pl.program_id(0))
        _, vjp = jax.vjp(functools.partial(_attn_heads, valid=valid), *_attn_loads(q_ref, kv_ref, sink_ref, start))
        dq, dkc, dvc, dkw, dvw, dsink = vjp([do_ref[:, h * B_HD:(h + 1) * B_HD] for h in range(B_Q_HEADS)])
        for h in range(B_Q_HEADS):
            dq_ref[:, h * B_HD:(h + 1) * B_HD] = dq[h]
            dsink_ref[h:h + 1, :] += jnp.broadcast_to(dsink[h], (1, 128))
        for hk in range(B_KV_HEADS):
            ks = slice(hk * B_HD, (hk + 1) * B_HD)
            vs = slice(128 + hk * B_HD, 128 + (hk + 1) * B_HD)
            grp = lambda parts: parts[hk * 4] + parts[hk * 4 + 1] + parts[hk * 4 + 2] + parts[hk * 4 + 3]
            dkv_ref[0:CTX_LEN, ks] += grp(dkc)
            dkv_ref[0:CTX_LEN, vs] += grp(dvc)
            dkv_ref[pl.ds(start, 3 * B_BLOCK), ks] += grp(dkw)
            dkv_ref[pl.ds(start, 3 * B_BLOCK), vs] += grp(dvw)

    qspec, kvspec, sspec = _attn_specs()
    body, r_in, r_out, r_shape, r_scratch = _riding(body, 4, 3, 0, ride, (N_QBLK,))
    return pl.pallas_call(
        body,
        grid=(N_QBLK,),
        in_specs=[qspec, kvspec, sspec, qspec] + r_in,
        out_specs=[qspec, kvspec, pl.BlockSpec((8, 128), lambda i: (0, 0))] + r_out,
        out_shape=[jax.ShapeDtypeStruct((ROWS, 512), F32), jax.ShapeDtypeStruct((ROWS, 256), F32),
                   jax.ShapeDtypeStruct((8, 128), F32)] + r_shape,
        scratch_shapes=r_scratch,
        compiler_params=_cparams(("arbitrary",)),
        name="attn_bwd",
    )(q, kv, sink, do, *(ride[0] if ride else []))


def _my_id():
    return 4 * lax.axis_index("x") + 2 * lax.axis_index("y") + lax.axis_index("c")


def _peer(k):
    x, y, c = lax.axis_index("x"), lax.axis_index("y"), lax.axis_index("c")
    return (1 - x if k & 4 else x, 1 - y if k & 2 else y, 1 - c if k & 1 else c)


SAME_CORE_PEERS = (2, 4, 6)


def _scatter_copies(ins, outs, sems):
    send_sems, recv_sems, local_sems = sems
    me = _my_id()
    own, remote = [], []
    for a in range(len(ins)):
        own.append(pltpu.make_async_copy(ins[a].at[me], outs[a].at[me], local_sems.at[a]))
        for k in range(1, N_DEV):
            peer_slot = jnp.bitwise_xor(me, k)
            common = dict(src_ref=ins[a].at[peer_slot], send_sem=send_sems.at[a, k - 1],
                          recv_sem=recv_sems.at[a, k - 1], device_id=_peer(k), device_id_type=MESH)
            remote.append((pltpu.make_async_remote_copy(dst_ref=outs[a].at[me], **common),
                           pltpu.make_async_remote_copy(dst_ref=outs[a].at[peer_slot], **common)))
    return own, remote


def _gather_copy(outs, sems, a, k, src, slot, to):
    return pltpu.make_async_remote_copy(src_ref=src, dst_ref=outs[a].at[slot], send_sem=sems[0].at[a, k - 1],
                                        recv_sem=sems[1].at[a, k - 1], device_id=_peer(to), device_id_type=MESH)


def _gather_first_copies(ins, outs, sems):
    me = _my_id()
    own = [pltpu.make_async_copy(ins[a], outs[a].at[me], sems[2].at[a]) for a in range(len(ins))]
    direct = [_gather_copy(outs, sems, a, k, ins[a], me, k) for a in range(len(ins)) for k in (1,) + SAME_CORE_PEERS]
    return own, direct


def _exchange_start(ins, outs, sems, gather):
    own, remote = _gather_first_copies(ins, outs, sems) if gather else _scatter_copies(ins, outs, sems)
    for cp in own:
        cp.start()
    for cp in remote:
        (cp if gather else cp[0]).start()


def _exchange_wait(ins, outs, sems, gather):
    if not gather:
        own, remote = _scatter_copies(ins, outs, sems)
        for _, arrival in remote:
            arrival.wait_recv()
        for send, _ in remote:
            send.wait_send()
        for cp in own:
            cp.wait()
        return
    me = _my_id()
    own, direct = _gather_first_copies(ins, outs, sems)
    passed = []
    for a in range(len(ins)):
        for k in SAME_CORE_PEERS:
            origin = jnp.bitwise_xor(me, k)
            _gather_copy(outs, sems, a, k, ins[a], origin, k).wait_recv()
            onward = _gather_copy(outs, sems, a, k + 1, outs[a].at[origin], origin, 1)
            onward.start()
            passed.append(onward)
    for a in range(len(ins)):
        for k in (1, 3, 5, 7):
            _gather_copy(outs, sems, a, k, ins[a], jnp.bitwise_xor(me, k), 1).wait_recv()
    for cp in direct + passed:
        cp.wait_send()
    for cp in own:
        cp.wait()


def _exchange_plumbing(arrays, gather):
    n = len(arrays)
    hbm = [pl.BlockSpec(memory_space=pltpu.HBM)] * n
    out_shape = [jax.ShapeDtypeStruct((N_DEV,) + (a.shape if gather else a.shape[1:]), a.dtype) for a in arrays]
    sems = [pltpu.SemaphoreType.DMA((n, N_DEV - 1)), pltpu.SemaphoreType.DMA((n, N_DEV - 1)),
            pltpu.SemaphoreType.DMA((n,))]
    return hbm, out_shape, sems


def _exchange(arrays, gather, name):
    n = len(arrays)

    def body(*refs):
        ins, outs, sems = refs[:n], refs[n:2 * n], refs[2 * n:]
        _exchange_start(ins, outs, sems, gather)
        _exchange_wait(ins, outs, sems, gather)

    hbm, out_shape, sems = _exchange_plumbing(arrays, gather)
    return pl.pallas_call(
        body,
        in_specs=hbm,
        out_specs=hbm,
        out_shape=out_shape,
        scratch_shapes=sems,
        compiler_params=pltpu.CompilerParams(has_side_effects=True),
        name=name,
    )(*arrays)


N_CHIP = N_DEV // 2


def _pair_swap(blocks, name):
    n = len(blocks)

    def body(*refs):
        ins, outs, (send_sems, recv_sems) = refs[:n], refs[n:2 * n], refs[2 * n:]
        core = lax.axis_index("c")
        copies = [pltpu.make_async_remote_copy(src_ref=ins[a].at[2 * chip + (1 - core)], dst_ref=outs[a].at[chip],
                                               send_sem=send_sems.at[a, chip], recv_sem=recv_sems.at[a, chip],
                                               device_id=_peer(1), device_id_type=MESH)
                  for a in range(n) for chip in range(N_CHIP)]
        for cp in copies:
            cp.start()
        for cp in copies:
            cp.wait_recv()
        for cp in copies:
            cp.wait_send()

    hbm = [pl.BlockSpec(memory_space=pltpu.HBM)] * n
    return pl.pallas_call(
        body,
        in_specs=hbm,
        out_specs=hbm,
        out_shape=[jax.ShapeDtypeStruct((N_CHIP,) + b.shape[1:], b.dtype) for b in blocks],
        scratch_shapes=[pltpu.SemaphoreType.DMA((n, N_CHIP)), pltpu.SemaphoreType.DMA((n, N_CHIP))],
        compiler_params=pltpu.CompilerParams(has_side_effects=True),
        name=name,
    )(*blocks)


def _chip_scatter(pairs, name):
    n = len(pairs)

    def body(*refs):
        ins, outs, (send_sems, recv_sems, local_sems) = refs[:n], refs[n:2 * n], refs[2 * n:]
        chip = 2 * lax.axis_index("x") + lax.axis_index("y")
        own = [pltpu.make_async_copy(ins[a].at[chip], outs[a].at[chip], local_sems.at[a]) for a in range(n)]
        sends, arrivals = [], []
        for a in range(n):
            for k in range(1, N_CHIP):
                other = jnp.bitwise_xor(chip, k)
                common = dict(src_ref=ins[a].at[other], send_sem=send_sems.at[a, k - 1], recv_sem=recv_sems.at[a, k - 1],
                              device_id=_peer(2 * k), device_id_type=MESH)
                sends.append(pltpu.make_async_remote_copy(dst_ref=outs[a].at[chip], **common))
                arrivals.append(pltpu.make_async_remote_copy(dst_ref=outs[a].at[other], **common))
        for cp in own + sends:
            cp.start()
        for cp in arrivals:
            cp.wait_recv()
        for cp in sends:
            cp.wait_send()
        for cp in own:
            cp.wait()

    hbm = [pl.BlockSpec(memory_space=pltpu.HBM)] * n
    return pl.pallas_call(
        body,
        in_specs=hbm,
        out_specs=hbm,
        out_shape=[jax.ShapeDtypeStruct(p.shape, p.dtype) for p in pairs],
        scratch_shapes=[pltpu.SemaphoreType.DMA((n, N_CHIP - 1)), pltpu.SemaphoreType.DMA((n, N_CHIP - 1)),
                        pltpu.SemaphoreType.DMA((n,))],
        compiler_params=pltpu.CompilerParams(has_side_effects=True),
        name=name,
    )(*pairs)


def _scatter_two_level(blocks, name):
    swapped = _pair_swap(blocks, name + "_pair")
    core = lax.axis_index("c")
    pairs = []
    for b, s in zip(blocks, swapped):
        mine = lax.dynamic_index_in_dim(b.reshape((N_CHIP, 2) + b.shape[1:]), core, axis=1, keepdims=False)
        pairs.append((mine.astype(F32) + s.astype(F32)).astype(b.dtype))
    return _chip_scatter(pairs, name + "_chip")


def _riding(body, n_in, n_out, n_scratch, ride, grid):
    if ride is None:
        return body, [], [], [], []
    arrays, gather = ride
    n = len(arrays)

    def at(step_of):
        hit = pl.program_id(0) == step_of(grid[0])
        for d in range(1, len(grid)):
            hit = jnp.logical_and(hit, pl.program_id(d) == step_of(grid[d]))
        return hit

    def wrapped(*refs):
        ins, rin = refs[:n_in], refs[n_in:n_in + n]
        outs = refs[n_in + n:n_in + n + n_out]
        rout = refs[n_in + n + n_out:n_in + 2 * n + n_out]
        scratch = refs[n_in + 2 * n + n_out:n_in + 2 * n + n_out + n_scratch]
        sems = refs[n_in + 2 * n + n_out + n_scratch:]

        @pl.when(at(lambda size: 0))
        def _():
            _exchange_start(rin, rout, sems, gather)

        body(*ins, *outs, *scratch)

        @pl.when(at(lambda size: size - 1))
        def _():
            _exchange_wait(rin, rout, sems, gather)

    hbm, out_shape, sems = _exchange_plumbing(arrays, gather)
    return wrapped, hbm, hbm, out_shape, sems


def _sum_contributions(c_ref):
    g = c_ref[0].astype(F32)
    for j in range(1, c_ref.shape[0]):
        g = g + c_ref[j].astype(F32)
    return g


def _adamw_update(g, w_ref, m_ref, v_ref, g_ref, d_ref, nm_ref, nv_ref):
    m_new = ADAM_B1 * m_ref[...] + (1.0 - ADAM_B1) * g
    v_new = ADAM_B2 * v_ref[...] + (1.0 - ADAM_B2) * (g * g)
    m_hat = m_new / (1.0 - ADAM_B1 ** ADAM_STEP)
    v_hat = v_new / (1.0 - ADAM_B2 ** ADAM_STEP)
    g_ref[...] = g
    d_ref[...] = -ADAM_LR * (m_hat / (jnp.sqrt(v_hat) + ADAM_EPS) + ADAM_WD * w_ref[...])
    nm_ref[...] = m_new
    nv_ref[...] = v_new


def _adamw_layers(w, m, v, contrib0, contrib1, name):
    _, r, c = w.shape
    br = _pick(r, (256, 128, 64, 32, 16, 8))
    nb = r // br

    def body(w_ref, m_ref, v_ref, c0_ref, c1_ref, g_ref, d_ref, nm_ref, nv_ref):
        g = jnp.where(pl.program_id(0) == 0, _sum_contributions(c0_ref), _sum_contributions(c1_ref))
        _adamw_update(g, w_ref, m_ref, v_ref, g_ref, d_ref, nm_ref, nv_ref)

    spec = pl.BlockSpec((None, br, c), lambda l, i: (l, i, 0))
    return pl.pallas_call(
        body,
        grid=(DEPTH, nb),
        in_specs=[spec, spec, spec,
                  pl.BlockSpec((contrib0.shape[0], br, c), lambda l, i: (0, jnp.where(l == 0, i, nb - 1), 0)),
                  pl.BlockSpec((contrib1.shape[0], br, c), lambda l, i: (0, jnp.where(l == 1, i, 0), 0))],
        out_specs=[spec] * 4,
        out_shape=[jax.ShapeDtypeStruct(w.shape, F32)] * 4,
        compiler_params=_cparams(("arbitrary", "arbitrary")),
        name=name,
    )(w, m, v, contrib0, contrib1)


def _adamw(w, m, v, contrib, name):
    r, c = w.shape
    br = _pick(r, (256, 128, 64, 32, 16, 8))

    def body(w_ref, m_ref, v_ref, c_ref, g_ref, d_ref, nm_ref, nv_ref):
        _adamw_update(_sum_contributions(c_ref), w_ref, m_ref, v_ref, g_ref, d_ref, nm_ref, nv_ref)

    spec = pl.BlockSpec((br, c), lambda i: (i, 0))
    cspec = pl.BlockSpec((contrib.shape[0], br, c), lambda i: (0, i, 0))
    return pl.pallas_call(
        body,
        grid=(r // br,),
        in_specs=[spec, spec, spec, cspec],
        out_specs=[spec] * 4,
        out_shape=[jax.ShapeDtypeStruct((r, c), F32)] * 4,
        compiler_params=_cparams(("parallel",)),
        name=name,
    )(w, m, v, contrib)


def _silu(x):
    return x * jax.nn.sigmoid(x)


def _rope_angles(pos, n_freq):
    inv = ROPE_BASE ** (-jnp.arange(n_freq, dtype=F32) / n_freq)
    return pos[:, None] * inv[None, :]


def _with_ctx_rows(cos, sin):
    return (jnp.concatenate([jnp.ones((CTX_LEN, 128), F32), cos], axis=0),
            jnp.concatenate([jnp.zeros((CTX_LEN, 128), F32), sin], axis=0))


def _rope_tables():
    rows_n = SEQ // GRID_W
    rows = jnp.repeat(jnp.arange(rows_n, dtype=F32), GRID_W)
    cols = jnp.tile(jnp.arange(GRID_W, dtype=F32), rows_n)
    ang_r = _rope_angles(rows, B_HD // 4)
    ang_c = _rope_angles(cols, B_HD // 4)
    cos_b = jnp.tile(jnp.concatenate([jnp.cos(ang_r)] * 2 + [jnp.cos(ang_c)] * 2, axis=1), (1, 2))
    sin_b = jnp.tile(jnp.concatenate([-jnp.sin(ang_r), jnp.sin(ang_r), -jnp.sin(ang_c), jnp.sin(ang_c)], axis=1), (1, 2))
    ang = _rope_angles(jnp.arange(SEQ, dtype=F32), C_HD // 2)
    cos_c = jnp.concatenate([jnp.cos(ang)] * 2, axis=1)
    sin_c = jnp.concatenate([-jnp.sin(ang), jnp.sin(ang)], axis=1)
    return _with_ctx_rows(cos_b, sin_b), _with_ctx_rows(cos_c, sin_c)


def _halves(a):
    return a[:4], a[4:]


def _delta_gates(ab, a_log, dt_bias):
    beta = jax.nn.sigmoid(ab[:, :8])
    g = -jnp.exp(a_log)[None, :] * jax.nn.softplus(ab[:, 8:] + dt_bias[None, :])
    gch = g.reshape(N_CHUNK, CHUNK, 8)
    tri = jnp.tril(jnp.ones((CHUNK, CHUNK), F32))
    fwd = jnp.einsum("ij,cjh->cih", tri, gch[..., :4], precision=HIGHEST)
    bwd = jnp.einsum("ji,cjh->cih", tri, gch[..., 4:], precision=HIGHEST)
    gc = jnp.concatenate([fwd, bwd], axis=-1)
    gl = jnp.sum(gch, axis=1)
    rows = lambda a: _halves(a.transpose(2, 0, 1)[:, :, None, :])
    return rows(beta.reshape(N_CHUNK, CHUNK, 8)), rows(gc), _halves(gl.T[:, :, None, None])


def _ret_consts(c_decay):
    lg = jax.nn.log_sigmoid(c_decay)
    idx = jnp.arange(RET_CHUNK, dtype=F32)
    diff = idx[:, None] - idx[None, :]
    lgf, lgb = lg[:4, None, None], lg[4:, None, None]
    dm = jnp.concatenate([jnp.exp(jnp.where(diff >= 0, diff * lgf, -jnp.inf)),
                          jnp.exp(jnp.where(diff <= 0, -diff * lgb, -jnp.inf))], axis=0)
    qs = jnp.concatenate([jnp.exp((idx + 1.0)[None, :] * lg[:4, None]),
                          jnp.exp((RET_CHUNK - idx)[None, :] * lg[4:, None])], axis=0)[:, :, None]
    ks = jnp.concatenate([jnp.exp((RET_CHUNK - 1.0 - idx)[None, :] * lg[:4, None]),
                          jnp.exp(idx[None, :] * lg[4:, None])], axis=0)[:, :, None]
    return dm, qs, ks, jnp.exp(RET_CHUNK * lg)[:, None, None]


A_PIECES = ((0, True, A_DK ** -0.5, "a_q"), (1, True, 1.0, "a_k"), (2, False, 1.0, "a_v"))
B_ROPE_COLS = [C_BQ // 512, C_BKV // 256, 0, 0]
C_ROPE_COLS = [C_CQ // 512, C_CK // 512, 0, 0]


def _conv8(conv_w):
    return jnp.pad(conv_w, ((0, 8 - A_CONV), (0, 0)))


W_IN_TILES = {"nn": (2176, 512, 1024), "nt": (1088, 1024, 2176), "db": (1024, 512, ROWS)}


def _core_forward(h, w16, p, rides):
    res = _matmul(h, w16, "w_in", "nn", W_IN_TILES["nn"], ride=rides.get("w_in"))
    proj, rode = (res[0], {"w_in": res[1:]}) if "w_in" in rides else (res, {})
    wb = p["w_branch"] if "w_in" not in rides else _unshard_layer("w_branch", rode["w_in"][0])
    (cos_b, sin_b), (cos_c, sin_c) = _rope_tables()
    conv8 = _conv8(p["a_conv_w"])
    q, k, v = [_a_prep_fwd(proj, conv8, col, nrm, scl, nm) for col, nrm, scl, nm in A_PIECES]
    gates = _delta_gates(proj[:, C_AB:C_AB + 16], p["a_log"], p["a_dt_bias"])
    res = _delta_fwd_call(q, k, v, *gates, ride=rides.get("delta"))
    (of, orv, ssf, ssr, tsf, tsr), rode["delta"] = res[:6], res[6:]

    qb, kvb = _b_rope.fwd([proj, proj, cos_b, sin_b], [], B_ROPE_COLS)
    res = _attn_fwd_call(qb, kvb, p["b_sink"], ride=rides.get("attn"))
    ob, rode["attn"] = res[0], res[1:]

    qc, kc = _c_rope.fwd([proj, proj, cos_c, sin_c], [], C_ROPE_COLS)
    res = _ret_fwd_call(qc, kc, proj, C_CV // 512, *_ret_consts(p["c_decay"]), ride=rides.get("ret"))
    (cf, cr, csf, csr), rode["ret"] = res[:4], res[4:]

    (merged,) = _mix_merge.fwd([(of, orv), proj, ob, proj, (cf, cr), proj, proj, proj, proj],
                               [p["a_norm_w"][None, :], p["c_norm_w"][None, :], wb[0], wb[1], wb[2]], MIX_MERGE_COLS)
    saved = dict(proj=proj, q=q, k=k, v=v, of=of, orv=orv, ss=(ssf, ssr), ts=(tsf, tsr), qb=qb, kvb=kvb, ob=ob,
                 qc=qc, kc=kc, cf=cf, cr=cr, cs=(csf, csr), wb=wb)
    return merged, saved, rode


def _core_backward(h, w16, p, s, dmerged, rides, branch_rides_in_attn=False):
    proj, wb = s["proj"], s["wb"]
    (cos_b, sin_b), (cos_c, sin_c) = _rope_tables()
    conv8 = _conv8(p["a_conv_w"])
    rode = {}

    (do_a, daz, dob, dbz, do_c, dcz, dma, dmb, dmc), (danw, dcnw, *dwb) = _mix_merge.bwd(
        [(s["of"], s["orv"]), proj, s["ob"], proj, (s["cf"], s["cr"]), proj, proj, proj, proj],
        [p["a_norm_w"][None, :], p["c_norm_w"][None, :], wb[0], wb[1], wb[2]], [dmerged], MIX_MERGE_COLS,
        bf16_rows=(1, 3, 5, 6, 7, 8))
    dwb = jnp.stack(dwb)

    consts, consts_vjp = jax.vjp(_ret_consts, p["c_decay"])
    g = _ret_bwd_call(s["qc"], s["kc"], proj, C_CV // 512, *consts, s["cs"], do_c, ride=rides.get("ret"))
    rode["ret"] = g[10:]
    (dcq, dck), _ = _c_rope.bwd([proj, proj, cos_c, sin_c], [], [(g[0], g[1]), (g[2], g[3])], C_ROPE_COLS,
                                bf16_rows=(0, 1))
    dcv = (g[4] + g[5]).astype(BF16)
    (dc_decay,) = consts_vjp(tuple(g[6:10]))

    attn_ride = rides.get("attn")
    if branch_rides_in_attn:
        attn_ride = (list(attn_ride[0]) + [_reshard_layer("w_branch", dwb).astype(BF16)], attn_ride[1])
    res = _attn_bwd_call(s["qb"], s["kvb"], p["b_sink"], dob, ride=attn_ride)
    (dqb, dkvb, dsink), rode["attn"] = res[:3], res[3:]
    (dbq, dbkv), _ = _b_rope.bwd([proj, proj, cos_b, sin_b], [], [dqb, dkvb], B_ROPE_COLS, bf16_rows=(0, 1))

    ab = proj[:, C_AB:C_AB + 16]
    gates, gates_vjp = jax.vjp(_delta_gates, ab, p["a_log"], p["a_dt_bias"])
    g = _delta_bwd_call(s["q"], s["k"], s["v"], *gates, s["ss"], s["ts"], do_a, ride=rides.get("delta"))
    rode["delta"] = g[12:]
    dgates = ((g[6], g[7]), (g[8], g[9]), (g[10], g[11]))
    dab, da_log, ddt = gates_vjp(dgates)
    dpre, dconv = [], []
    for (col, nrm, scl, nm), df, dr in zip(A_PIECES, (g[0], g[2], g[4]), (g[1], g[3], g[5])):
        dx, dw = _a_prep_bwd(proj, conv8, col, nrm, scl, df, dr, nm)
        dpre.append(dx)
        dconv.append(dw[:A_CONV])

    dproj = jnp.concatenate(dpre + [daz, dbq, dbz, dcq, dck, dcv, dcz, dma, dmb, dmc, dbkv,
                                    jnp.pad(dab, ((0, 0), (0, IN_PAD - C_AB - 16))).astype(BF16)], axis=1)
    dh = _matmul(dproj, w16, "w_in_da", "nt", W_IN_TILES["nt"])
    dw = _matmul(h.T.astype(BF16), dproj, "w_in_db", "nn", W_IN_TILES["db"])
    dp = dict(a_conv_w=jnp.concatenate(dconv, axis=1), a_log=da_log, a_dt_bias=ddt, a_norm_w=danw[0],
              b_sink=dsink[:, 0], c_decay=dc_decay, c_norm_w=dcnw[0], w_branch=dwb)
    return dh, dw, dp, rode


CORE_PARAMS = ("a_conv_w", "a_log", "a_dt_bias", "a_norm_w", "b_sink", "c_decay", "c_norm_w", "w_branch")


W_IN_SHARD = IN_WIDTH // N_DEV
W_IN_RUNS = ((0, 2048, 0), (2064, 512, C_BQ), (2832, 512, C_BZ), (3344, 5120, C_CQ), (2576, 256, C_BKV),
             (2048, 16, C_AB))


def _shard_overlap(start, width, j):
    lo, hi = max(start, j * W_IN_SHARD), min(start + width, (j + 1) * W_IN_SHARD)
    return (lo, hi) if lo < hi else None


def _w_in_from_shards(g):
    parts = []
    for start, width, _ in W_IN_RUNS:
        for j in range(N_DEV):
            span = _shard_overlap(start, width, j)
            if span:
                parts.append(g[j, :, span[0] - j * W_IN_SHARD:span[1] - j * W_IN_SHARD])
    parts.append(jnp.zeros((D_MODEL, IN_PAD - IN_WIDTH), g.dtype))
    return jnp.concatenate(parts, axis=1)


def _w_in_blocks(dw):
    blocks = []
    for j in range(N_DEV):
        parts = []
        for start, width, pad in sorted(W_IN_RUNS):
            span = _shard_overlap(start, width, j)
            if span:
                parts.append(dw[:, pad + span[0] - start:pad + span[1] - start])
        blocks.append(jnp.concatenate(parts, axis=1))
    return jnp.stack(blocks)


LAYER_SHARDED = ("w_ada", "w_in", "w_branch", "w_out")


def _unshard_layer(name, g):
    if name == "w_branch":
        return g.transpose(1, 2, 0, 3).reshape(3, BR_WIDTH, D_MODEL)
    if name == "w_out":
        return g.reshape(D_MODEL, D_MODEL)
    return g.transpose(1, 0, 2).reshape(D_MODEL, -1)


def _reshard_layer(name, w):
    if name == "w_branch":
        return w.reshape(3, BR_WIDTH, N_DEV, D_MODEL // N_DEV).transpose(2, 0, 1, 3)
    if name == "w_out":
        return w.reshape(N_DEV, D_MODEL // N_DEV, D_MODEL)
    return w.reshape(D_MODEL, N_DEV, -1).transpose(1, 0, 2)


def _layer_weights(gathered):
    out = {n: _unshard_layer(n, g) for n, g in gathered.items() if n != "w_in"}
    out["w_in16"] = _w_in_from_shards(gathered["w_in"])
    return out


def _grad_blocks(name, g):
    return (_w_in_blocks(g) if name == "w_in" else _reshard_layer(name, g)).astype(BF16)


def _forward_backward(small, layer0, shards0, shards1, x, c, ctx, loss_target):
    c_ctx = small["c_ctx"]
    sc16 = jnp.zeros((16, D_MODEL), F32).at[0].set(_silu(c)).at[1].set(_silu(c_ctx))
    xs = jnp.concatenate([ctx, x], axis=0)
    weights = [dict(layer0), None]
    layers = []
    for l in range(DEPTH):
        wl = weights[l]
        mod16 = _matmul(sc16, wl["w_ada"], "ada") + small["b_ada"][l][None, :]
        mod_cx = jnp.stack([mod16[1], mod16[0]])
        shift, scale, gate = jnp.split(mod_cx, 3, axis=1)
        nw = small["norm_w"][l][None, :]
        (h,) = _norm_mod.fwd([xs], [nw, shift, scale])
        p = {n: small[n][l] for n in CORE_PARAMS if n != "w_branch"}
        p["w_branch"] = wl.get("w_branch")
        rides = {}
        if l == 0:
            rides = {"w_in": ([shards0["w_branch"], shards0["w_out"]], True), "delta": ([shards1["w_in"]], True),
                     "attn": ([shards1["w_ada"]], True), "ret": ([shards1["w_branch"], shards1["w_out"]], True)}
        merged, saved, rode = _core_forward(h, wl["w_in16"], p, rides)
        if l == 0:
            wl["w_out"] = _unshard_layer("w_out", rode["w_in"][1])
            weights[1] = _layer_weights(dict(w_in=rode["delta"][0], w_ada=rode["attn"][0],
                                             w_branch=rode["ret"][0], w_out=rode["ret"][1]))
        (xs_next,) = _out_residual.fwd([xs, merged], [wl["w_out"], gate])
        layers.append(dict(xs=xs, h=h, p=p, saved=saved, merged=merged, gate=gate, nw=nw, shift=shift, scale=scale))
        xs = xs_next
    fw = small["final_norm_w"][None, :]
    xs = xs[CTX_LEN:]
    (per_row,) = _loss_rows.fwd([xs, loss_target], [fw])
    loss = jnp.sum(per_row[:, 0])

    d_per_row = jnp.zeros((SEQ, 128), F32).at[:, 0].set(1.0)
    (dxs,), (dfw,) = _loss_rows.bwd([xs, loss_target], [fw], [d_per_row])
    dxs = jnp.pad(dxs, ((CTX_LEN, 0), (0, 0)))
    small_names = tuple(n for n in CORE_PARAMS if n != "w_branch") + ("b_ada", "norm_w")
    dsmall = {n: [None] * DEPTH for n in small_names}
    dlayer = [None] * DEPTH
    contrib0 = contrib1 = None
    dsc16 = jnp.zeros((16, D_MODEL), F32)
    for l in reversed(range(DEPTH)):
        s, wl = layers[l], weights[l]
        (dres, dmerged), (dw_out, dgate) = _out_residual.bwd([s["xs"], s["merged"]], [wl["w_out"], s["gate"]], [dxs])
        rides = {}
        if l == 0:
            blocks1 = {n: _grad_blocks(n, g) for n, g in dlayer[1].items()}
            rides = {"ret": ([blocks1["w_branch"], blocks1["w_out"]], False),
                     "attn": ([_reshard_layer("w_out", dw_out).astype(BF16), blocks1["w_ada"]], False),
                     "delta": ([blocks1["w_in"]], False)}
        dh, dw_in, dp, rode = _core_backward(s["h"], wl["w_in16"], s["p"], s["saved"], dmerged, rides,
                                             branch_rides_in_attn=(l == 0))
        if l == 0:
            contrib1 = dict(w_in=rode["delta"][0], w_ada=rode["attn"][1], w_branch=rode["ret"][0],
                            w_out=rode["ret"][1])
            contrib0 = dict(w_out=rode["attn"][0], w_branch=rode["attn"][2])
        (dxn,), (dnw, dshift, dscale) = _norm_mod.bwd([s["xs"]], [s["nw"], s["shift"], s["scale"]], [dh])
        dxs = dres + dxn
        dmod_cx = jnp.concatenate([dshift, dscale, dgate], axis=1)
        dmod16 = jnp.zeros((16, 3 * D_MODEL), F32).at[0].set(dmod_cx[1]).at[1].set(dmod_cx[0])
        dsc16 = dsc16 + _matmul(dmod16, wl["w_ada"], "ada_da", "nt")
        dlayer[l] = dict(w_ada=_matmul(sc16, dmod16, "ada_db", "tn"), w_in=dw_in,
                         w_branch=dp["w_branch"], w_out=dw_out)
        for n in small_names:
            if n in dp:
                dsmall[n][l] = dp[n]
        dsmall["norm_w"][l] = dnw[0]
        dsmall["b_ada"][l] = dmod_cx[0] + dmod_cx[1]
    gsmall = {n: jnp.stack(v) for n, v in dsmall.items()}
    gsmall["final_norm_w"] = dfw[0]
    sig = jax.nn.sigmoid(c_ctx)
    gsmall["c_ctx"] = dsc16[1] * sig * (1.0 + c_ctx * (1.0 - sig))
    return loss, dxs[CTX_LEN:], gsmall, {n: dlayer[0][n] for n in ("w_ada", "w_in")}, contrib0, contrib1


SMALL = ("c_ctx", "b_ada", "norm_w", "a_log", "a_dt_bias", "a_norm_w", "b_sink", "c_decay", "c_norm_w",
         "final_norm_w")
WEIGHTS = ("c_ctx", "w_ada", "b_ada", "norm_w", "w_in", "a_conv_w", "a_log", "a_dt_bias", "a_norm_w", "b_sink",
           "c_decay", "c_norm_w", "w_branch", "w_out", "final_norm_w")
SMALL_PACK = 12288


def _unshard_conv(g):
    return g.transpose(1, 2, 0, 3).reshape(DEPTH, A_CONV, 3 * A_WIDTH)


def _reshard_conv(w):
    return w.reshape(DEPTH, A_CONV, N_DEV, 3 * A_WIDTH // N_DEV).transpose(2, 0, 1, 3)


def _pack_small(tree):
    flat = jnp.concatenate([tree[n].reshape(-1) for n in SMALL])
    return jnp.pad(flat, (0, SMALL_PACK - flat.shape[0])).reshape(SMALL_PACK // 128, 128)


def _unpack_small(packed, like):
    flat = packed.reshape(-1)
    out, off = {}, 0
    for n in SMALL:
        size = math.prod(like[n].shape)
        out[n] = flat[off:off + size].reshape(like[n].shape)
        off += size
    return out


def kernel(x, c, ctx, c_ctx, w_ada, b_ada, norm_w, w_in, a_conv_w, a_log, a_dt_bias, a_norm_w, b_sink, c_decay, c_norm_w, w_branch, w_out, final_norm_w, loss_target, m_c_ctx, m_w_ada, m_b_ada, m_norm_w, m_w_in, m_a_conv_w, m_a_log, m_a_dt_bias, m_a_norm_w, m_b_sink, m_c_decay, m_c_norm_w, m_w_branch, m_w_out, m_final_norm_w, v_c_ctx, v_w_ada, v_b_ada, v_norm_w, v_w_in, v_a_conv_w, v_a_log, v_a_dt_bias, v_a_norm_w, v_b_sink, v_c_decay, v_c_norm_w, v_w_branch, v_w_out, v_final_norm_w):
    w = dict(c_ctx=c_ctx, w_ada=w_ada, b_ada=b_ada, norm_w=norm_w, w_in=w_in, a_conv_w=a_conv_w, a_log=a_log,
             a_dt_bias=a_dt_bias, a_norm_w=a_norm_w, b_sink=b_sink, c_decay=c_decay, c_norm_w=c_norm_w,
             w_branch=w_branch, w_out=w_out, final_norm_w=final_norm_w)
    m = dict(c_ctx=m_c_ctx, w_ada=m_w_ada, b_ada=m_b_ada, norm_w=m_norm_w, w_in=m_w_in, a_conv_w=m_a_conv_w,
             a_log=m_a_log, a_dt_bias=m_a_dt_bias, a_norm_w=m_a_norm_w, b_sink=m_b_sink, c_decay=m_c_decay,
             c_norm_w=m_c_norm_w, w_branch=m_w_branch, w_out=m_w_out, final_norm_w=m_final_norm_w)
    v = dict(c_ctx=v_c_ctx, w_ada=v_w_ada, b_ada=v_b_ada, norm_w=v_norm_w, w_in=v_w_in, a_conv_w=v_a_conv_w,
             a_log=v_a_log, a_dt_bias=v_a_dt_bias, a_norm_w=v_a_norm_w, b_sink=v_b_sink, c_decay=v_c_decay,
             c_norm_w=v_c_norm_w, w_branch=v_w_branch, w_out=v_w_out, final_norm_w=v_final_norm_w)

    shards = {n: w[n].astype(BF16) for n in LAYER_SHARDED}
    first = _exchange([shards["w_ada"][0], shards["w_in"][0], w["a_conv_w"]], True, "gather_layer0")
    layer0 = _layer_weights(dict(w_ada=first[0], w_in=first[1]))
    small_w = {n: w[n] for n in SMALL}
    small_w["a_conv_w"] = _unshard_conv(first[2])
    loss, gx, gw, glayer0, contrib0, contrib1 = _forward_backward(
        small_w, layer0, {n: shards[n][0] for n in ("w_branch", "w_out")}, {n: shards[n][1] for n in LAYER_SHARDED},
        x[0], c[0], ctx[0], loss_target[0])
    loss = lax.psum(loss, ("x", "y", "c"))

    last = _scatter_two_level([_reshard_layer("w_ada", glayer0["w_ada"]).astype(BF16),
                               _grad_blocks("w_in", glayer0["w_in"]), _reshard_conv(gw["a_conv_w"])],
                              "scatter_layer0")
    contrib0["w_ada"], contrib0["w_in"] = last[0], last[1]
    small = _exchange([_pack_small(gw)], True, "gather_small_grads")[0]

    grad, delta, new_m, new_v = {}, {}, {}, {}
    for n in LAYER_SHARDED:
        shp = w[n].shape
        per_layer = (math.prod(shp[1:-1]), shp[-1])
        outs = _adamw_layers(*[a.reshape((DEPTH,) + per_layer) for a in (w[n], m[n], v[n])],
                             *[cb.reshape(cb.shape[:1] + per_layer) for cb in (contrib0[n], contrib1[n])], "adamw_" + n)
        grad[n], delta[n], new_m[n], new_v[n] = [o.reshape(shp) for o in outs]
    shp = a_conv_w.shape
    two_d = (math.prod(shp[:-1]), shp[-1])
    outs = _adamw(*[a.reshape(two_d) for a in (a_conv_w, m_a_conv_w, v_a_conv_w)],
                  last[2].reshape(last[2].shape[:1] + two_d), "adamw_a_conv_w")
    grad["a_conv_w"], delta["a_conv_w"], new_m["a_conv_w"], new_v["a_conv_w"] = [o.reshape(shp) for o in outs]
    outs = _adamw(_pack_small(w), _pack_small(m), _pack_small(v), small, "adamw_small")
    for tree, packed in zip((grad, delta, new_m, new_v), outs):
        tree.update(_unpack_small(packed, w))

    return (loss, gx[None], *[grad[n] for n in WEIGHTS], *[delta[n] for n in WEIGHTS],
            *[new_m[n] for n in WEIGHTS], *[new_v[n] for n in WEIGHTS])
```

```python
import functools
import math

import jax
import jax.numpy as jnp
from jax import lax
from jax.experimental import pallas as pl
from jax.experimental.pallas import tpu as pltpu

F32 = jnp.float32
BF16 = jnp.bfloat16
HIGHEST = lax.Precision.HIGHEST

D_MODEL = 1024
SEQ = 4096
DEPTH = 2
GRID_W = 64
CTX_LEN = 256
EPS = 1e-6
ROPE_BASE = 10000.0
BR_WIDTH = D_MODEL // 2
A_DK = 128
A_HEADS = 4
A_WIDTH = 512
A_CONV = 5
B_HD = 64
B_Q_HEADS = 8
B_KV_HEADS = 2
WINDOW = 128
B_BLOCK = 128
C_HD = 128
C_HEADS = 4
C_WIDTH = 512
CHUNK = 128
RET_CHUNK = 256
ADAM_LR = 0.001
ADAM_B1 = 0.9
ADAM_B2 = 0.999
ADAM_EPS = 1e-08
ADAM_WD = 0.01
ADAM_STEP = 10

N_DEV = 8
ROWS = CTX_LEN + SEQ
N_CHUNK = ROWS // CHUNK
IN_WIDTH = 8464
IN_PAD = 8704
NEG = -1e30

VMEM_LIMIT = 56 * 1024 * 1024
MESH = pl.DeviceIdType.MESH

C_AQ, C_AK, C_AV, C_AZ, C_BQ, C_BZ, C_CQ, C_CK, C_CV, C_CZ = (i * 512 for i in range(10))
C_MERGE = 5120
C_BKV = 8192
C_AB = 8448


def _cparams(sem=None):
    if sem is None:
        return pltpu.CompilerParams(vmem_limit_bytes=VMEM_LIMIT)
    return pltpu.CompilerParams(dimension_semantics=sem, vmem_limit_bytes=VMEM_LIMIT)


def _dg(a, b, ca, cb, prec=None):
    return lax.dot_general(a, b, (((ca,), (cb,)), ((), ())), preferred_element_type=F32, precision=prec)


@functools.partial(jax.custom_vjp, nondiff_argnums=(2, 3))
def _bdot(a, b, ca, cb):
    return _dg(a.astype(BF16), b.astype(BF16), ca, cb)


def _bdot_fwd(a, b, ca, cb):
    return _bdot(a, b, ca, cb), (a, b)


def _bdot_bwd(ca, cb, res, ct):
    a, b = res
    da = _bdot(ct, b, 1, 1 - cb) if ca == 1 else _bdot(b, ct, 1 - cb, 1)
    db = _bdot(a, ct, 1 - ca, 0) if cb == 0 else _bdot(ct, a, 0, 1 - ca)
    return da, db


_bdot.defvjp(_bdot_fwd, _bdot_bwd)


def _hdot(a, b):
    return _dg(a, b, 1, 0, lax.Precision.HIGH)


def _k_silu(x):
    return x / (1.0 + jnp.exp(-x))


def _k_sigmoid(x):
    return 1.0 / (1.0 + jnp.exp(-x))


@jax.custom_vjp
def _swap64(x):
    return pltpu.roll(x, 64, 1)


_swap64.defvjp(lambda x: (pltpu.roll(x, 64, 1), None), lambda _, ct: (pltpu.roll(ct, 64, 1),))


def _swap16_impl(x):
    lane = lax.broadcasted_iota(jnp.int32, x.shape, 1)
    return jnp.where((lane & 16) == 0, pltpu.roll(x, 112, 1), pltpu.roll(x, 16, 1))


@jax.custom_vjp
def _swap16(x):
    return _swap16_impl(x)


_swap16.defvjp(lambda x: (_swap16_impl(x), None), lambda _, ct: (_swap16_impl(ct),))


def _pick(dim, prefs):
    for p in prefs:
        if dim % p == 0:
            return p
    return dim


def _matmul(a, b, name, mode="nn", tiles=None, ride=None):
    ca, cb = {"nn": (1, 0), "nt": (1, 1), "tn": (0, 0)}[mode]
    m, k = a.shape[1 - ca], a.shape[ca]
    n = b.shape[1 - cb]
    if tiles is None:
        tiles = (_pick(m, (1088, 1024, 512, 256, 128)), _pick(n, (512, 256, 128)),
                 _pick(k, (1088, 1024, 512, 256, 128) if mode == "tn" else (2176, 2048, 1024, 512, 256, 128)))
    tm, tn, tk = tiles
    nk = k // tk
    a_spec = (pl.BlockSpec((tm, tk), lambda i, j, kk: (i, kk)) if ca == 1
              else pl.BlockSpec((tk, tm), lambda i, j, kk: (kk, i)))
    b_spec = (pl.BlockSpec((tk, tn), lambda i, j, kk: (kk, j)) if cb == 0
              else pl.BlockSpec((tn, tk), lambda i, j, kk: (j, kk)))

    def body(a_ref, b_ref, o_ref):
        part = _dg(a_ref[...].astype(BF16), b_ref[...].astype(BF16), ca, cb)
        if nk == 1:
            o_ref[...] = part
        else:
            kk = pl.program_id(2)

            @pl.when(kk == 0)
            def _():
                o_ref[...] = part

            @pl.when(kk > 0)
            def _():
                o_ref[...] += part

    grid = (m // tm, n // tn, nk)
    if ride is None:
        return pl.pallas_call(
            body,
            grid=grid,
            in_specs=[a_spec, b_spec],
            out_specs=pl.BlockSpec((tm, tn), lambda i, j, kk: (i, j)),
            out_shape=jax.ShapeDtypeStruct((m, n), F32),
            compiler_params=_cparams(("parallel", "parallel", "arbitrary")),
            name=name,
        )(a, b)
    body, r_in, r_out, r_shape, r_scratch = _riding(body, 2, 1, 0, ride, grid)
    return pl.pallas_call(
        body,
        grid=grid,
        in_specs=[a_spec, b_spec] + r_in,
        out_specs=[pl.BlockSpec((tm, tn), lambda i, j, kk: (i, j))] + r_out,
        out_shape=[jax.ShapeDtypeStruct((m, n), F32)] + r_shape,
        scratch_shapes=r_scratch,
        compiler_params=_cparams(("arbitrary", "arbitrary", "arbitrary")),
        name=name,
    )(a, b, *ride[0])


ROW_BLOCK = 256
ROW_VMEM_BUDGET = 16 * 1024 * 1024


def _pieces(val, pw):
    return [val[:, j * pw:(j + 1) * pw] for j in range(val.shape[1] // pw)]


def _flat(groups):
    arrays, sizes = [], []
    for g in groups:
        g = g if isinstance(g, (tuple, list)) else (g,)
        arrays += list(g)
        sizes.append(len(g))
    return arrays, sizes


def _regroup(refs, sizes):
    out, at = [], 0
    for n in sizes:
        val = refs[at][...]
        for r in refs[at + 1:at + n]:
            val = val + r[...]
        out.append(val)
        at += n
    return out


class _Rowwise:
    def __init__(self, fn, name, row_wpw, par_pw, out_wpw, n_diff=None, block_rows=None):
        self.fn, self.name, self.row_wpw, self.par_pw, self.out_wpw = fn, name, row_wpw, par_pw, out_wpw
        self.n_diff = len(row_wpw) if n_diff is None else n_diff
        self.block_rows = block_rows

    def _load(self, row_vals, par_refs, br, with_ctx):
        row = pl.program_id(0) * br + lax.broadcasted_iota(jnp.int32, (br, 1), 0)
        is_ctx = (row < (CTX_LEN if with_ctx else 0)).astype(F32)
        rows = [_pieces(v, pw) for v, (_, pw) in zip(row_vals, self.row_wpw)]
        pars = []
        for p, pw in zip(par_refs, self.par_pw):
            val = p[...].astype(F32)
            if p.shape[0] == 2:
                val = is_ctx * val[0:1, :] + (1.0 - is_ctx) * val[1:2, :]
            pars.append(_pieces(val, pw))
        return rows, pars, is_ctx

    def _block_rows(self, n_rows, widths):
        if self.block_rows:
            return self.block_rows
        for br in (1088, 1024, 544, 512, 272):
            if n_rows % br == 0 and 2 * 4 * br * sum(widths) <= ROW_VMEM_BUDGET:
                return br
        return ROW_BLOCK

    def _row_specs(self, br, sizes, cols):
        out = []
        for (w, _), n, c in zip(self.row_wpw, sizes, cols):
            out += [pl.BlockSpec((br, w), lambda i, c=c: (i, c))] * n
        return out

    def fwd(self, rows, params, cols=None):
        arrays, sizes = _flat(rows)
        cols = cols or [0] * len(rows)
        n_rows = arrays[0].shape[0]
        n_in = len(arrays)
        br = self._block_rows(n_rows, [w for (w, _), n in zip(self.row_wpw, sizes) for _ in range(n)]
                              + [w for w, _ in self.out_wpw])

        def body(*refs):
            r, p, _ = self._load(_regroup(refs[:n_in], sizes), refs[n_in:n_in + len(params)], br, n_rows == ROWS)
            for o_ref, pieces, (_, pw) in zip(refs[n_in + len(params):], self.fn(r, p), self.out_wpw):
                for j, piece in enumerate(pieces):
                    o_ref[:, j * pw:(j + 1) * pw] = piece

        return pl.pallas_call(
            body,
            grid=(n_rows // br,),
            in_specs=self._row_specs(br, sizes, cols) + [pl.BlockSpec(p.shape, lambda i: (0, 0)) for p in params],
            out_specs=[pl.BlockSpec((br, w), lambda i: (i, 0)) for w, _ in self.out_wpw],
            out_shape=[jax.ShapeDtypeStruct((n_rows, w), F32) for w, _ in self.out_wpw],
            compiler_params=_cparams(("parallel",)),
            name=self.name + "_fwd",
        )(*arrays, *params)

    def bwd(self, rows, params, douts, cols=None, bf16_rows=()):
        arrays, sizes = _flat(rows)
        darrays, dsizes = _flat(douts)
        cols = cols or [0] * len(rows)
        n_rows = arrays[0].shape[0]
        n_in, n_par, n_dout, n_diff = len(arrays), len(params), len(darrays), self.n_diff
        br = self._block_rows(n_rows, [w for (w, _), n in zip(self.row_wpw, sizes) for _ in range(n)]
                              + [w for (w, _), n in zip(self.out_wpw, dsizes) for _ in range(n)]
                              + [w for w, _ in self.row_wpw[:n_diff]])

        def body(*refs):
            par_refs = refs[n_in:n_in + n_par]
            dout_refs = refs[n_in + n_par:n_in + n_par + n_dout]
            drow_refs = refs[n_in + n_par + n_dout:n_in + n_par + n_dout + n_diff]
            dpar_refs = refs[n_in + n_par + n_dout + n_diff:]

            @pl.when(pl.program_id(0) == 0)
            def _():
                for d in dpar_refs:
                    d[...] = jnp.zeros_like(d)

            r, p, is_ctx = self._load(_regroup(refs[:n_in], sizes), par_refs, br, n_rows == ROWS)
            cts = [_pieces(d, pw) for d, (_, pw) in zip(_regroup(dout_refs, dsizes), self.out_wpw)]
            fixed = r[n_diff:]
            _, vjp = jax.vjp(lambda rd, pp: self.fn(rd + fixed, pp), r[:n_diff], p)
            dr, dp = vjp(cts)
            for d_ref, pieces, (_, pw) in zip(drow_refs, dr, self.row_wpw):
                for j, piece in enumerate(pieces):
                    d_ref[:, j * pw:(j + 1) * pw] = piece.astype(d_ref.dtype)
            for d_ref, pieces, pw in zip(dpar_refs, dp, self.par_pw):
                for j, piece in enumerate(pieces):
                    lanes = slice(j * pw, (j + 1) * pw)
                    if d_ref.shape[0] != 2:
                        d_ref[:, lanes] += piece
                    else:
                        d_ref[0:1, lanes] += jnp.sum(is_ctx * piece, axis=0, keepdims=True)
                        d_ref[1:2, lanes] += jnp.sum((1.0 - is_ctx) * piece, axis=0, keepdims=True)

        par_specs = [pl.BlockSpec(p.shape, lambda i: (0, 0)) for p in params]
        dout_specs = []
        for (w, _), n in zip(self.out_wpw, dsizes):
            dout_specs += [pl.BlockSpec((br, w), lambda i: (i, 0))] * n
        drow_w = [w for w, _ in self.row_wpw[:n_diff]]
        g = pl.pallas_call(
            body,
            grid=(n_rows // br,),
            in_specs=self._row_specs(br, sizes, cols) + par_specs + dout_specs,
            out_specs=[pl.BlockSpec((br, w), lambda i: (i, 0)) for w in drow_w] + par_specs,
            out_shape=[jax.ShapeDtypeStruct((n_rows, w), BF16 if a in bf16_rows else F32) for a, w in enumerate(drow_w)]
            + [jax.ShapeDtypeStruct(p.shape, F32) for p in params],
            compiler_params=_cparams(("arbitrary",)),
            name=self.name + "_bwd",
        )(*arrays, *params, *darrays)
        return list(g[:n_diff]), list(g[n_diff:])


def _fn_norm_mod(rows, pars):
    (x,), (nw,), (shift,), (scale,) = rows[0], pars[0], pars[1], pars[2]
    y = x * lax.rsqrt(jnp.mean(x * x, axis=-1, keepdims=True) + EPS) * nw
    return [[y * (1.0 + scale) + shift]]


def _fn_head_rms_gate(rows, pars):
    (w,) = pars[0]
    return [[o * lax.rsqrt(jnp.mean(o * o, axis=-1, keepdims=True) + EPS) * w * _k_silu(z)
             for o, z in zip(rows[0], rows[1])]]


def _fn_group_norm_gate(rows, pars):
    out = []
    for o, z, w in zip(rows[0], rows[1], pars[0]):
        mu = jnp.mean(o, axis=-1, keepdims=True)
        var = jnp.mean(jnp.square(o - mu), axis=-1, keepdims=True)
        out.append((o - mu) * lax.rsqrt(var + EPS) * w * _k_silu(z))
    return [out]


def _fn_mix_merge(rows, pars):
    oa, za, (ob,), (zb,), oc, zc, (ma,), (mb,), (mc,) = rows
    na, nc, (wa,), (wb,), (wc,) = pars
    ya = jnp.concatenate(_fn_head_rms_gate([oa, za], [na])[0], axis=1)
    yb = ob * _k_silu(zb)
    yc = jnp.concatenate(_fn_group_norm_gate([oc, zc], [nc])[0], axis=1)
    return [[_k_sigmoid(ma) * _bdot(ya, wa, 1, 0) + _k_sigmoid(mb) * _bdot(yb, wb, 1, 0)
             + _k_sigmoid(mc) * _bdot(yc, wc, 1, 0)]]


def _fn_out_residual(rows, pars):
    (res,), (merged,), (w,), (gate,) = rows[0], rows[1], pars[0], pars[1]
    return [[res + gate * _bdot(merged, w, 1, 0)]]


def _fn_loss(rows, pars):
    (x,), (target,), (w,) = rows[0], rows[1], pars[0]
    y = x * lax.rsqrt(jnp.mean(x * x, axis=-1, keepdims=True) + EPS) * w
    per_row = 0.5 * jnp.mean(jnp.square(y - target), axis=-1, keepdims=True)
    return [[jnp.broadcast_to(per_row, (per_row.shape[0], 128))]]


def _fn_b_rope(rows, pars):
    q, (k, v), (cos,), (sin,) = rows
    rot = lambda x: x * cos + _swap16(x) * sin
    return [[rot(x) for x in q], [rot(k), v]]


def _fn_c_rope(rows, pars):
    q, k, (cos,), (sin,) = rows
    rot = lambda x: x * cos + _swap64(x) * sin
    return [[rot(x) for x in q], [rot(x) * (C_HD ** -0.5) for x in k]]


_norm_mod = _Rowwise(_fn_norm_mod, "norm_mod", [(D_MODEL, D_MODEL)], [D_MODEL] * 3, [(D_MODEL, D_MODEL)])
_out_residual = _Rowwise(_fn_out_residual, "out_residual", [(D_MODEL, D_MODEL)] * 2, [D_MODEL] * 2,
                         [(D_MODEL, D_MODEL)], block_rows=544)
_loss_rows = _Rowwise(_fn_loss, "loss", [(D_MODEL, D_MODEL)] * 2, [D_MODEL], [(128, 128)], n_diff=1)
_mix_merge = _Rowwise(_fn_mix_merge, "mix_merge",
                      [(512, 128), (512, 128), (512, 512), (512, 512), (512, 128), (512, 128)] + [(D_MODEL, D_MODEL)] * 3,
                      [128, 128] + [D_MODEL] * 3, [(D_MODEL, D_MODEL)], block_rows=256)
MIX_MERGE_COLS = [0, C_AZ // 512, 0, C_BZ // 512, 0, C_CZ // 512] + [C_MERGE // 1024 + i for i in range(3)]
_b_rope = _Rowwise(_fn_b_rope, "b_rope", [(512, 128), (256, 128), (128, 128), (128, 128)], [],
                   [(512, 128), (256, 128)], n_diff=2)
_c_rope = _Rowwise(_fn_c_rope, "c_rope", [(512, 128), (512, 128), (128, 128), (128, 128)], [],
                   [(512, 128), (512, 128)], n_diff=2)


HALO = 8
EXT = ROW_BLOCK + 2 * HALO


def _halo_specs(col, width=512):
    last = ROWS // HALO - 1
    per = ROW_BLOCK // HALO
    prev = pl.BlockSpec((HALO, width), lambda i: (jnp.maximum(i * per - 1, 0), col))
    cur = pl.BlockSpec((ROW_BLOCK, width), lambda i: (i, col))
    nxt = pl.BlockSpec((HALO, width), lambda i: (jnp.minimum((i + 1) * per, last), col))
    return [prev, cur, nxt]


def _extended(prev_ref, cur_ref, next_ref):
    i = pl.program_id(0)
    prev_ok = i >= 2
    next_ok = jnp.logical_and(i >= 1, i < ROWS // ROW_BLOCK - 1)
    return jnp.concatenate([jnp.where(prev_ok, prev_ref[...], 0.0), cur_ref[...],
                            jnp.where(next_ok, next_ref[...], 0.0)], axis=0)


def _conv_taps(x_ext, w_ref, flip):
    acc = None
    for j in range(A_CONV):
        shift = (j - 2) if flip else (2 - j)
        term = w_ref[j:j + 1, :] * pltpu.roll(x_ext, shift % EXT, 0)
        acc = term if acc is None else acc + term
    return acc


def _conv_post(pre_pieces, normalize, scale):
    out = []
    for p in pre_pieces:
        y = _k_silu(p)
        if normalize:
            y = y * lax.rsqrt(jnp.sum(y * y, axis=-1, keepdims=True) + EPS) * scale
        out.append(y)
    return out


def _a_prep_fwd(proj, conv8, col, normalize, scale, name):
    def body(prev_ref, cur_ref, next_ref, w_ref, o_ref):
        pre = _conv_taps(_extended(prev_ref, cur_ref, next_ref), w_ref, False)[HALO:HALO + ROW_BLOCK]
        for h, y in enumerate(_conv_post(_pieces(pre, 128), normalize, scale)):
            o_ref[:, h * 128:(h + 1) * 128] = y

    return pl.pallas_call(
        body,
        grid=(ROWS // ROW_BLOCK,),
        in_specs=_halo_specs(col) + [pl.BlockSpec((8, 512), lambda i: (0, col))],
        out_specs=pl.BlockSpec((ROW_BLOCK, 512), lambda i: (i, 0)),
        out_shape=jax.ShapeDtypeStruct((ROWS, 512), F32),
        compiler_params=_cparams(("parallel",)),
        name=name + "_fwd",
    )(proj, proj, proj, conv8)


def _a_prep_bwd(proj, conv8, col, normalize, scale, dout_f, dout_r, name):
    def body(xp, xc, xn, w_ref, fp, fc, fn_, rp, rc, rn, dx_ref, dw_ref):
        @pl.when(pl.program_id(0) == 0)
        def _():
            dw_ref[...] = jnp.zeros_like(dw_ref)

        x_ext = _extended(xp, xc, xn)
        dout = _extended(fp, fc, fn_) + _extended(rp, rc, rn)
        pre = _conv_taps(x_ext, w_ref, False)
        _, vjp = jax.vjp(lambda p: _conv_post(p, normalize, scale), _pieces(pre, 128))
        (dpre,) = vjp(_pieces(dout, 128))
        dpre = jnp.concatenate(dpre, axis=1)
        dx_ref[...] = _conv_taps(dpre, w_ref, True)[HALO:HALO + ROW_BLOCK].astype(BF16)
        own = dpre[HALO:HALO + ROW_BLOCK]
        for j in range(A_CONV):
            shifted = pltpu.roll(x_ext, (2 - j) % EXT, 0)[HALO:HALO + ROW_BLOCK]
            dw_ref[j:j + 1, :] += jnp.sum(own * shifted, axis=0, keepdims=True)

    return pl.pallas_call(
        body,
        grid=(ROWS // ROW_BLOCK,),
        in_specs=_halo_specs(col) + [pl.BlockSpec((8, 512), lambda i: (0, col))] + _halo_specs(0) + _halo_specs(0),
        out_specs=[pl.BlockSpec((ROW_BLOCK, 512), lambda i: (i, 0)), pl.BlockSpec((8, 512), lambda i: (0, 0))],
        out_shape=[jax.ShapeDtypeStruct((ROWS, 512), BF16), jax.ShapeDtypeStruct((8, 512), F32)],
        compiler_params=_cparams(("arbitrary",)),
        name=name + "_bwd",
    )(proj, proj, proj, conv8, dout_f, dout_f, dout_f, dout_r, dout_r, dout_r)


N_CHAIN = 8


def _rev_chunk(s, chunk):
    n_ctx, n_all = CTX_LEN // chunk, ROWS // chunk
    return jnp.where(s < n_ctx, n_ctx - 1 - s, n_all + n_ctx - 1 - s)


def _scan_specs(step_of, chunk, v_col=0):
    cf = step_of
    cr = lambda n: _rev_chunk(step_of(n), chunk)

    def pair(shape, index):
        return (pl.BlockSpec(shape, lambda n: index(cf(n))), pl.BlockSpec(shape, lambda n: index(cr(n))))

    return dict(
        tok=pair((chunk, 512), lambda c: (c, 0)),
        tokv=pair((chunk, 512), lambda c: (c, v_col)),
        row=pair((4, 1, 1, chunk), lambda c: (0, c, 0, 0)),
        one=pair((4, 1, 1, 1), lambda c: (0, c, 0, 0)),
        state=pair((None, 4, 128, 128), lambda c: (c, 0, 0, 0)),
        tinv=pair((None, 4, chunk, chunk), lambda c: (c, 0, 0, 0)),
    )


def _both(specs, kinds):
    out = []
    for kind in kinds:
        out += list(specs[kind])
    return out


def _scan_call(body, name, in_specs, out_specs, out_shape, operands, ride, chunk):
    grid = (ROWS // chunk,)
    body, r_in, r_out, r_shape, r_scratch = _riding(body, len(in_specs), len(out_specs), 1, ride, grid)
    return pl.pallas_call(
        body,
        grid=grid,
        in_specs=in_specs + r_in,
        out_specs=out_specs + r_out,
        out_shape=out_shape + r_shape,
        scratch_shapes=[pltpu.VMEM((N_CHAIN, 128, 128), F32)] + r_scratch,
        compiler_params=_cparams(("arbitrary",)),
        name=name,
    )(*operands, *(ride[0] if ride else []))


def _chain_masks():
    ii = lax.broadcasted_iota(jnp.int32, (CHUNK, CHUNK), 0)
    jj = lax.broadcasted_iota(jnp.int32, (CHUNK, CHUNK), 1)
    eye = jnp.where(ii == jj, 1.0, 0.0).astype(F32)
    lower = (ii >= jj, ii > jj)
    upper = (ii <= jj, ii < jj)
    return [lower] * 4 + [upper] * 4, eye


INV_BLOCK = 64


def _series_inverse(ls):
    ii = lax.broadcasted_iota(jnp.int32, (INV_BLOCK, INV_BLOCK), 0)
    jj = lax.broadcasted_iota(jnp.int32, (INV_BLOCK, INV_BLOCK), 1)
    eye = jnp.where(ii == jj, 1.0, 0.0).astype(F32)
    doublings = INV_BLOCK.bit_length() - 2
    xs = [eye - l for l in ls]
    ps = [_hdot(l, l) for l in ls]
    for i in range(doublings):
        xs = [x + _hdot(x, p) for x, p in zip(xs, ps)]
        if i < doublings - 1:
            ps = [_hdot(p, p) for p in ps]
    return xs


def _tri_inv_all(ls, upper):
    size = ls[0].shape[0]
    if size == INV_BLOCK:
        return _series_inverse(ls)
    n, h = len(ls), size // 2
    diag = _tri_inv_all([l[:h, :h] for l in ls] + [l[h:, h:] for l in ls], list(upper) * 2)
    out = []
    zero = jnp.zeros((h, h), F32)
    for i, l in enumerate(ls):
        a, d = diag[i], diag[n + i]
        if upper[i]:
            off = -_hdot(_hdot(a, l[:h, h:]), d)
            out.append(jnp.concatenate([jnp.concatenate([a, off], axis=1), jnp.concatenate([zero, d], axis=1)], axis=0))
        else:
            off = -_hdot(_hdot(d, l[h:, :h]), a)
            out.append(jnp.concatenate([jnp.concatenate([a, zero], axis=1), jnp.concatenate([off, d], axis=1)], axis=0))
    return out


@jax.custom_vjp
def _inv_saved(l, x):
    return x


def _inv_saved_fwd(l, x):
    return x, x


def _inv_saved_bwd(x, dx):
    return -_bdot(x, _bdot(dx, x, 1, 1), 0, 0), jnp.zeros_like(x)


_inv_saved.defvjp(_inv_saved_fwd, _inv_saved_bwd)


def _delta_chains(q, k, v, beta_r, gcr, gl, s, masks, eye, tinv_saved):
    n = range(len(q))
    beta = [jnp.sum(eye * beta_r[i], axis=1, keepdims=True) for i in n]
    gcc = [jnp.sum(eye * gcr[i], axis=1, keepdims=True) for i in n]
    decay = [jnp.exp(jnp.where(masks[i][0], gcc[i] - gcr[i], NEG)) for i in n]
    kb = [k[i] * beta[i] for i in n]
    lmat = [jnp.where(masks[i][1], _bdot(kb[i], k[i], 1, 1) * decay[i], 0.0) for i in n]
    if tinv_saved is None:
        tinv = _tri_inv_all(lmat, [i >= 4 for i in n])
    else:
        tinv = [_inv_saved(lmat[i], tinv_saved[i]) for i in n]
    eg = [jnp.exp(gcc[i]) for i in n]
    u = [_bdot(tinv[i], v[i] * beta[i], 1, 0) for i in n]
    w = [_bdot(tinv[i], kb[i] * eg[i], 1, 0) for i in n]
    qk = [_bdot(q[i], k[i], 1, 1) * decay[i] for i in n]
    v_new = [u[i] - _bdot(w[i], s[i], 1, 0) for i in n]
    o = [_bdot(q[i] * eg[i], s[i], 1, 0) + _bdot(qk[i], v_new[i], 1, 0) for i in n]
    s_new = [s[i] * jnp.exp(gl[i]) + _bdot(k[i] * jnp.exp(gl[i] - gcc[i]), v_new[i], 0, 0) for i in n]
    return (o, s_new), tinv


def _chain_loads(tok_pairs, small_pairs):
    toks = [[pair[i // 4][:, (i % 4) * 128:(i % 4 + 1) * 128] for i in range(N_CHAIN)] for pair in tok_pairs]
    smalls = [[pair[i // 4][i % 4] for i in range(N_CHAIN)] for pair in small_pairs]
    return toks, smalls


def _delta_fwd_call(q, k, v, beta, gc, gl, ride=None):
    sp = _scan_specs(lambda n: n, CHUNK)

    def body(qf, qr, kf, kr, vf, vr, bf, br, gcrf, gcrr, glf, glr, of, orv, ssf, ssr, tsf, tsr, s_scr):
        @pl.when(pl.program_id(0) == 0)
        def _():
            s_scr[...] = jnp.zeros_like(s_scr)

        masks, eye = _chain_masks()
        (qs, ks, vs), _ = _chain_loads([(qf, qr), (kf, kr), (vf, vr)], [])
        bs = [(bf, br)[i // 4][i % 4, 0] for i in range(N_CHAIN)]
        gcrs = [(gcrf, gcrr)[i // 4][i % 4, 0] for i in range(N_CHAIN)]
        gls = [(glf, glr)[i // 4][i % 4, 0] for i in range(N_CHAIN)]
        ss = [s_scr[i] for i in range(N_CHAIN)]
        (o, s_new), tinv = _delta_chains(qs, ks, vs, bs, gcrs, gls, ss, masks, eye, None)
        for i in range(N_CHAIN):
            d, h = i // 4, i % 4
            (ssf, ssr)[d][h] = ss[i]
            (tsf, tsr)[d][h] = tinv[i]
            (of, orv)[d][:, h * 128:(h + 1) * 128] = o[i]
            s_scr[i] = s_new[i]

    return _scan_call(
        body, "delta_fwd",
        _both(sp, ["tok", "tok", "tok", "row", "row", "one"]),
        _both(sp, ["tok", "state", "tinv"]),
        [jax.ShapeDtypeStruct((ROWS, 512), F32)] * 2 + [jax.ShapeDtypeStruct((N_CHUNK, 4, 128, 128), F32)] * 2
        + [jax.ShapeDtypeStruct((N_CHUNK, 4, CHUNK, CHUNK), F32)] * 2,
        [q, q, k, k, v, v, *beta, *gc, *gl], ride, CHUNK)


def _delta_bwd_call(q, k, v, beta, gc, gl, ssave, tsave, do, ride=None):
    sp = _scan_specs(lambda n: N_CHUNK - 1 - n, CHUNK)

    def body(qf, qr, kf, kr, vf, vr, bf, br, gcrf, gcrr, glf, glr, ssf, ssr, tsf, tsr, dof, dor,
             dqf, dqr, dkf, dkr, dvf, dvr, dbf, dbr, dgcrf, dgcrr, dglf, dglr, ds_scr):
        @pl.when(pl.program_id(0) == 0)
        def _():
            ds_scr[...] = jnp.zeros_like(ds_scr)

        masks, eye = _chain_masks()
        (qs, ks, vs, dos), (ss, ts) = _chain_loads(
            [(qf, qr), (kf, kr), (vf, vr), (dof, dor)], [(ssf, ssr), (tsf, tsr)])
        bs = [(bf, br)[i // 4][i % 4, 0] for i in range(N_CHAIN)]
        gcrs = [(gcrf, gcrr)[i // 4][i % 4, 0] for i in range(N_CHAIN)]
        gls = [(glf, glr)[i // 4][i % 4, 0] for i in range(N_CHAIN)]
        fn = lambda *a: _delta_chains(*a, masks, eye, ts)
        _, vjp, _ = jax.vjp(fn, qs, ks, vs, bs, gcrs, gls, ss, has_aux=True)
        dq, dk, dv, db, dgcr, dgl, ds = vjp((dos, [ds_scr[i] for i in range(N_CHAIN)]))
        for i in range(N_CHAIN):
            d, h = i // 4, i % 4
            hs = slice(h * 128, (h + 1) * 128)
            (dqf, dqr)[d][:, hs] = dq[i]
            (dkf, dkr)[d][:, hs] = dk[i]
            (dvf, dvr)[d][:, hs] = dv[i]
            (dbf, dbr)[d][h, 0] = db[i]
            (dgcrf, dgcrr)[d][h, 0] = dgcr[i]
            (dglf, dglr)[d][h, 0] = dgl[i]
            ds_scr[i] = ds[i]

    tok = jax.ShapeDtypeStruct((ROWS, 512), F32)
    return _scan_call(
        body, "delta_bwd",
        _both(sp, ["tok", "tok", "tok", "row", "row", "one", "state", "tinv", "tok"]),
        _both(sp, ["tok", "tok", "tok", "row", "row", "one"]),
        [tok] * 6 + [jax.ShapeDtypeStruct((4, N_CHUNK, 1, CHUNK), F32)] * 4
        + [jax.ShapeDtypeStruct((4, N_CHUNK, 1, 1), F32)] * 2,
        [q, q, k, k, v, v, *beta, *gc, *gl, *ssave, *tsave, do, do], ride, CHUNK)


def _ret_chains(q, k, v, dm, qs, ks, cd, s):
    n = range(len(q))
    a = [_bdot(q[i], k[i], 1, 1) * dm[i] for i in n]
    o = [_bdot(a[i], v[i], 1, 0) + _bdot(q[i] * qs[i], s[i], 1, 0) for i in n]
    s_new = [s[i] * cd[i] + _bdot(k[i] * ks[i], v[i], 0, 0) for i in n]
    return o, s_new


RET_CONST_SHAPES = ((N_CHAIN, RET_CHUNK, RET_CHUNK), (N_CHAIN, RET_CHUNK, 1), (N_CHAIN, RET_CHUNK, 1), (N_CHAIN, 1, 1))


def _ret_const_specs():
    return [pl.BlockSpec(shape, lambda n: (0, 0, 0)) for shape in RET_CONST_SHAPES]


def _ret_fwd_call(q, k, v, v_col, dm, qs, ks, cd, ride=None):
    sp = _scan_specs(lambda n: n, RET_CHUNK, v_col)

    def body(qf, qr, kf, kr, vf, vr, dm_ref, qs_ref, ks_ref, cd_ref, of, orv, ssf, ssr, s_scr):
        @pl.when(pl.program_id(0) == 0)
        def _():
            s_scr[...] = jnp.zeros_like(s_scr)

        (qc, kc, vc), _ = _chain_loads([(qf, qr), (kf, kr), (vf, vr)], [])
        ss = [s_scr[i] for i in range(N_CHAIN)]
        consts = [[r[i] for i in range(N_CHAIN)] for r in (dm_ref, qs_ref, ks_ref, cd_ref)]
        o, s_new = _ret_chains(qc, kc, vc, *consts, ss)
        for i in range(N_CHAIN):
            d, h = i // 4, i % 4
            (ssf, ssr)[d][h] = ss[i]
            (of, orv)[d][:, h * 128:(h + 1) * 128] = o[i]
            s_scr[i] = s_new[i]

    return _scan_call(
        body, "ret_fwd",
        _both(sp, ["tok", "tok", "tokv"]) + _ret_const_specs(),
        _both(sp, ["tok", "state"]),
        [jax.ShapeDtypeStruct((ROWS, 512), F32)] * 2
        + [jax.ShapeDtypeStruct((ROWS // RET_CHUNK, 4, 128, 128), F32)] * 2,
        [q, q, k, k, v, v, dm, qs, ks, cd], ride, RET_CHUNK)


def _ret_bwd_call(q, k, v, v_col, dm, qs, ks, cd, ssave, do, ride=None):
    sp = _scan_specs(lambda n: ROWS // RET_CHUNK - 1 - n, RET_CHUNK, v_col)

    def body(qf, qr, kf, kr, vf, vr, dm_ref, qs_ref, ks_ref, cd_ref, ssf, ssr, dof, dor,
             dqf, dqr, dkf, dkr, dvf, dvr, ddm_ref, dqs_ref, dks_ref, dcd_ref, ds_scr):
        @pl.when(pl.program_id(0) == 0)
        def _():
            ds_scr[...] = jnp.zeros_like(ds_scr)
            ddm_ref[...] = jnp.zeros_like(ddm_ref)
            dqs_ref[...] = jnp.zeros_like(dqs_ref)
            dks_ref[...] = jnp.zeros_like(dks_ref)
            dcd_ref[...] = jnp.zeros_like(dcd_ref)

        (qc, kc, vc, dos), (ss,) = _chain_loads([(qf, qr), (kf, kr), (vf, vr), (dof, dor)], [(ssf, ssr)])
        consts = [[r[i] for i in range(N_CHAIN)] for r in (dm_ref, qs_ref, ks_ref, cd_ref)]
        _, vjp = jax.vjp(_ret_chains, qc, kc, vc, *consts, ss)
        dq, dk, dv, ddm, dqs, dks, dcd, ds = vjp((dos, [ds_scr[i] for i in range(N_CHAIN)]))
        for i in range(N_CHAIN):
            d, h = i // 4, i % 4
            hs = slice(h * 128, (h + 1) * 128)
            (dqf, dqr)[d][:, hs] = dq[i]
            (dkf, dkr)[d][:, hs] = dk[i]
            (dvf, dvr)[d][:, hs] = dv[i]
            ddm_ref[i] += ddm[i]
            dqs_ref[i] += dqs[i]
            dks_ref[i] += dks[i]
            dcd_ref[i] += dcd[i]
            ds_scr[i] = ds[i]

    tok = jax.ShapeDtypeStruct((ROWS, 512), F32)
    return _scan_call(
        body, "ret_bwd",
        _both(sp, ["tok", "tok", "tokv"]) + _ret_const_specs() + _both(sp, ["state", "tok"]),
        _both(sp, ["tok", "tok", "tok"]) + _ret_const_specs(),
        [tok] * 6 + [jax.ShapeDtypeStruct(shape, F32) for shape in RET_CONST_SHAPES],
        [q, q, k, k, v, v, dm, qs, ks, cd, *ssave, do, do], ride, RET_CHUNK)


N_QBLK = ROWS // B_BLOCK
CTX_QBLK = CTX_LEN // B_BLOCK


def _attn_heads(q, kc, vc, kw, vw, sink, valid):
    n = range(len(q))
    qs = [q[i] * (B_HD ** -0.5) for i in n]
    s_c = [_bdot(qs[i], kc[i], 1, 1) for i in n]
    s_w = [jnp.where(valid, _bdot(qs[i], kw[i], 1, 1), NEG) for i in n]
    m = [lax.stop_gradient(jnp.maximum(jnp.maximum(jnp.max(s_c[i], axis=-1, keepdims=True), sink[i]),
                                       jnp.max(s_w[i], axis=-1, keepdims=True))) for i in n]
    e_c = [jnp.exp(s_c[i] - m[i]) for i in n]
    e_w = [jnp.exp(s_w[i] - m[i]) for i in n]
    den = [jnp.sum(e_c[i], axis=-1, keepdims=True) + jnp.sum(e_w[i], axis=-1, keepdims=True)
           + jnp.exp(sink[i] - m[i]) for i in n]
    return [(_bdot(e_c[i], vc[i], 1, 0) + _bdot(e_w[i], vw[i], 1, 0)) / den[i] for i in n]


def _attn_loads(q_ref, kv_ref, sink_ref, start):
    q, kc, vc, kw, vw, sink = [], [], [], [], [], []
    for hk in range(B_KV_HEADS):
        ks = slice(hk * B_HD, (hk + 1) * B_HD)
        vs = slice(128 + hk * B_HD, 128 + (hk + 1) * B_HD)
        grp = (kv_ref[0:CTX_LEN, ks], kv_ref[0:CTX_LEN, vs],
               kv_ref[pl.ds(start, 3 * B_BLOCK), ks], kv_ref[pl.ds(start, 3 * B_BLOCK), vs])
        for g in range(4):
            h = hk * 4 + g
            q.append(q_ref[:, h * B_HD:(h + 1) * B_HD])
            for lst, val in zip((kc, vc, kw, vw), grp):
                lst.append(val)
            sink.append(jnp.full((1, 1), sink_ref[h], F32))
    return q, kc, vc, kw, vw, sink


def _window(blk):
    xblk = blk - CTX_QBLK
    first = jnp.clip((xblk - 1) * B_BLOCK, 0, SEQ - 3 * B_BLOCK)
    qpos = xblk * B_BLOCK + lax.broadcasted_iota(jnp.int32, (B_BLOCK, 3 * B_BLOCK), 0)
    kpos = first + lax.broadcasted_iota(jnp.int32, (B_BLOCK, 3 * B_BLOCK), 1)
    far = jnp.where(blk >= CTX_QBLK, 0, 2 * SEQ)
    valid = jnp.abs(kpos - qpos) + far <= WINDOW
    return pl.multiple_of(first + CTX_LEN, B_BLOCK), valid


def _attn_specs():
    qspec = pl.BlockSpec((B_BLOCK, 512), lambda i: (i, 0))
    kvspec = pl.BlockSpec((ROWS, 256), lambda i: (0, 0))
    return qspec, kvspec, pl.BlockSpec(memory_space=pltpu.SMEM)


def _attn_fwd_call(q, kv, sink, ride=None):
    def body(q_ref, kv_ref, sink_ref, o_ref):
        start, valid = _window(pl.program_id(0))
        out = _attn_heads(*_attn_loads(q_ref, kv_ref, sink_ref, start), valid)
        for h in range(B_Q_HEADS):
            o_ref[:, h * B_HD:(h + 1) * B_HD] = out[h]

    qspec, kvspec, sspec = _attn_specs()
    body, r_in, r_out, r_shape, r_scratch = _riding(body, 3, 1, 0, ride, (N_QBLK,))
    return pl.pallas_call(
        body,
        grid=(N_QBLK,),
        in_specs=[qspec, kvspec, sspec] + r_in,
        out_specs=[qspec] + r_out,
        out_shape=[jax.ShapeDtypeStruct((ROWS, 512), F32)] + r_shape,
        scratch_shapes=r_scratch,
        compiler_params=_cparams(("arbitrary",)),
        name="attn_fwd",
    )(q, kv, sink, *(ride[0] if ride else []))


def _attn_bwd_call(q, kv, sink, do, ride=None):
    def body(q_ref, kv_ref, sink_ref, do_ref, dq_ref, dkv_ref, dsink_ref):
        @pl.when(pl.program_id(0) == 0)
        def _():
            dkv_ref[...] = jnp.zeros_like(dkv_ref)
            dsink_ref[...] = jnp.zeros_like(dsink_ref)

        start, valid = _window(pl.program_id(0))
        _, vjp = jax.vjp(functools.partial(_attn_heads, valid=valid), *_attn_loads(q_ref, kv_ref, sink_ref, start))
        dq, dkc, dvc, dkw, dvw, dsink = vjp([do_ref[:, h * B_HD:(h + 1) * B_HD] for h in range(B_Q_HEADS)])
        for h in range(B_Q_HEADS):
            dq_ref[:, h * B_HD:(h + 1) * B_HD] = dq[h]
            dsink_ref[h:h + 1, :] += jnp.broadcast_to(dsink[h], (1, 128))
        for hk in range(B_KV_HEADS):
            ks = slice(hk * B_HD, (hk + 1) * B_HD)
            vs = slice(128 + hk * B_HD, 128 + (hk + 1) * B_HD)
            grp = lambda parts: parts[hk * 4] + parts[hk * 4 + 1] + parts[hk * 4 + 2] + parts[hk * 4 + 3]
            dkv_ref[0:CTX_LEN, ks] += grp(dkc)
            dkv_ref[0:CTX_LEN, vs] += grp(dvc)
            dkv_ref[pl.ds(start, 3 * B_BLOCK), ks] += grp(dkw)
            dkv_ref[pl.ds(start, 3 * B_BLOCK), vs] += grp(dvw)

    qspec, kvspec, sspec = _attn_specs()
    body, r_in, r_out, r_shape, r_scratch = _riding(body, 4, 3, 0, ride, (N_QBLK,))
    return pl.pallas_call(
        body,
        grid=(N_QBLK,),
        in_specs=[qspec, kvspec, sspec, qspec] + r_in,
        out_specs=[qspec, kvspec, pl.BlockSpec((8, 128), lambda i: (0, 0))] + r_out,
        out_shape=[jax.ShapeDtypeStruct((ROWS, 512), F32), jax.ShapeDtypeStruct((ROWS, 256), F32),
                   jax.ShapeDtypeStruct((8, 128), F32)] + r_shape,
        scratch_shapes=r_scratch,
        compiler_params=_cparams(("arbitrary",)),
        name="attn_bwd",
    )(q, kv, sink, do, *(ride[0] if ride else []))


def _my_id():
    return 4 * lax.axis_index("x") + 2 * lax.axis_index("y") + lax.axis_index("c")


def _peer(k):
    x, y, c = lax.axis_index("x"), lax.axis_index("y"), lax.axis_index("c")
    return (1 - x if k & 4 else x, 1 - y if k & 2 else y, 1 - c if k & 1 else c)


SAME_CORE_PEERS = (2, 4, 6)


def _scatter_copies(ins, outs, sems):
    send_sems, recv_sems, local_sems = sems
    me = _my_id()
    own, remote = [], []
    for a in range(len(ins)):
        own.append(pltpu.make_async_copy(ins[a].at[me], outs[a].at[me], local_sems.at[a]))
        for k in range(1, N_DEV):
            peer_slot = jnp.bitwise_xor(me, k)
            common = dict(src_ref=ins[a].at[peer_slot], send_sem=send_sems.at[a, k - 1],
                          recv_sem=recv_sems.at[a, k - 1], device_id=_peer(k), device_id_type=MESH)
            remote.append((pltpu.make_async_remote_copy(dst_ref=outs[a].at[me], **common),
                           pltpu.make_async_remote_copy(dst_ref=outs[a].at[peer_slot], **common)))
    return own, remote


def _gather_copy(outs, sems, a, k, src, slot, to):
    return pltpu.make_async_remote_copy(src_ref=src, dst_ref=outs[a].at[slot], send_sem=sems[0].at[a, k - 1],
                                        recv_sem=sems[1].at[a, k - 1], device_id=_peer(to), device_id_type=MESH)


def _gather_first_copies(ins, outs, sems):
    me = _my_id()
    own = [pltpu.make_async_copy(ins[a], outs[a].at[me], sems[2].at[a]) for a in range(len(ins))]
    direct = [_gather_copy(outs, sems, a, k, ins[a], me, k) for a in range(len(ins)) for k in (1,) + SAME_CORE_PEERS]
    return own, direct


def _exchange_start(ins, outs, sems, gather):
    own, remote = _gather_first_copies(ins, outs, sems) if gather else _scatter_copies(ins, outs, sems)
    for cp in own:
        cp.start()
    for cp in remote:
        (cp if gather else cp[0]).start()


def _exchange_wait(ins, outs, sems, gather):
    if not gather:
        own, remote = _scatter_copies(ins, outs, sems)
        for _, arrival in remote:
            arrival.wait_recv()
        for send, _ in remote:
            send.wait_send()
        for cp in own:
            cp.wait()
        return
    me = _my_id()
    own, direct = _gather_first_copies(ins, outs, sems)
    passed = []
    for a in range(len(ins)):
        for k in SAME_CORE_PEERS:
            origin = jnp.bitwise_xor(me, k)
            _gather_copy(outs, sems, a, k, ins[a], origin, k).wait_recv()
            onward = _gather_copy(outs, sems, a, k + 1, outs[a].at[origin], origin, 1)
            onward.start()
            passed.append(onward)
    for a in range(len(ins)):
        for k in (1, 3, 5, 7):
            _gather_copy(outs, sems, a, k, ins[a], jnp.bitwise_xor(me, k), 1).wait_recv()
    for cp in direct + passed:
        cp.wait_send()
    for cp in own:
        cp.wait()


def _exchange_plumbing(arrays, gather):
    n = len(arrays)
    hbm = [pl.BlockSpec(memory_space=pltpu.HBM)] * n
    out_shape = [jax.ShapeDtypeStruct((N_DEV,) + (a.shape if gather else a.shape[1:]), a.dtype) for a in arrays]
    sems = [pltpu.SemaphoreType.DMA((n, N_DEV - 1)), pltpu.SemaphoreType.DMA((n, N_DEV - 1)),
            pltpu.SemaphoreType.DMA((n,))]
    return hbm, out_shape, sems


def _exchange(arrays, gather, name):
    n = len(arrays)

    def body(*refs):
        ins, outs, sems = refs[:n], refs[n:2 * n], refs[2 * n:]
        _exchange_start(ins, outs, sems, gather)
        _exchange_wait(ins, outs, sems, gather)

    hbm, out_shape, sems = _exchange_plumbing(arrays, gather)
    return pl.pallas_call(
        body,
        in_specs=hbm,
        out_specs=hbm,
        out_shape=out_shape,
        scratch_shapes=sems,
        compiler_params=pltpu.CompilerParams(has_side_effects=True),
        name=name,
    )(*arrays)


N_CHIP = N_DEV // 2


def _pair_swap(blocks, name):
    n = len(blocks)

    def body(*refs):
        ins, outs, (send_sems, recv_sems) = refs[:n], refs[n:2 * n], refs[2 * n:]
        core = lax.axis_index("c")
        copies = [pltpu.make_async_remote_copy(src_ref=ins[a].at[2 * chip + (1 - core)], dst_ref=outs[a].at[chip],
                                               send_sem=send_sems.at[a, chip], recv_sem=recv_sems.at[a, chip],
                                               device_id=_peer(1), device_id_type=MESH)
                  for a in range(n) for chip in range(N_CHIP)]
        for cp in copies:
            cp.start()
        for cp in copies:
            cp.wait_recv()
        for cp in copies:
            cp.wait_send()

    hbm = [pl.BlockSpec(memory_space=pltpu.HBM)] * n
    return pl.pallas_call(
        body,
        in_specs=hbm,
        out_specs=hbm,
        out_shape=[jax.ShapeDtypeStruct((N_CHIP,) + b.shape[1:], b.dtype) for b in blocks],
        scratch_shapes=[pltpu.SemaphoreType.DMA((n, N_CHIP)), pltpu.SemaphoreType.DMA((n, N_CHIP))],
        compiler_params=pltpu.CompilerParams(has_side_effects=True),
        name=name,
    )(*blocks)


def _chip_scatter(pairs, name):
    n = len(pairs)

    def body(*refs):
        ins, outs, (send_sems, recv_sems, local_sems) = refs[:n], refs[n:2 * n], refs[2 * n:]
        chip = 2 * lax.axis_index("x") + lax.axis_index("y")
        own = [pltpu.make_async_copy(ins[a].at[chip], outs[a].at[chip], local_sems.at[a]) for a in range(n)]
        sends, arrivals = [], []
        for a in range(n):
            for k in range(1, N_CHIP):
                other = jnp.bitwise_xor(chip, k)
                common = dict(src_ref=ins[a].at[other], send_sem=send_sems.at[a, k - 1], recv_sem=recv_sems.at[a, k - 1],
                              device_id=_peer(2 * k), device_id_type=MESH)
                sends.append(pltpu.make_async_remote_copy(dst_ref=outs[a].at[chip], **common))
                arrivals.append(pltpu.make_async_remote_copy(dst_ref=outs[a].at[other], **common))
        for cp in own + sends:
            cp.start()
        for cp in arrivals:
            cp.wait_recv()
        for cp in sends:
            cp.wait_send()
        for cp in own:
            cp.wait()

    hbm = [pl.BlockSpec(memory_space=pltpu.HBM)] * n
    return pl.pallas_call(
        body,
        in_specs=hbm,
        out_specs=hbm,
        out_shape=[jax.ShapeDtypeStruct(p.shape, p.dtype) for p in pairs],
        scratch_shapes=[pltpu.SemaphoreType.DMA((n, N_CHIP - 1)), pltpu.SemaphoreType.DMA((n, N_CHIP - 1)),
                        pltpu.SemaphoreType.DMA((n,))],
        compiler_params=pltpu.CompilerParams(has_side_effects=True),
        name=name,
    )(*pairs)


def _scatter_two_level(blocks, name):
    swapped = _pair_swap(blocks, name + "_pair")
    core = lax.axis_index("c")
    pairs = []
    for b, s in zip(blocks, swapped):
        mine = lax.dynamic_index_in_dim(b.reshape((N_CHIP, 2) + b.shape[1:]), core, axis=1, keepdims=False)
        pairs.append((mine.astype(F32) + s.astype(F32)).astype(b.dtype))
    return _chip_scatter(pairs, name + "_chip")


def _riding(body, n_in, n_out, n_scratch, ride, grid):
    if ride is None:
        return body, [], [], [], []
    arrays, gather = ride
    n = len(arrays)

    def at(step_of):
        hit = pl.program_id(0) == step_of(grid[0])
        for d in range(1, len(grid)):
            hit = jnp.logical_and(hit, pl.program_id(d) == step_of(grid[d]))
        return hit

    def wrapped(*refs):
        ins, rin = refs[:n_in], refs[n_in:n_in + n]
        outs = refs[n_in + n:n_in + n + n_out]
        rout = refs[n_in + n + n_out:n_in + 2 * n + n_out]
        scratch = refs[n_in + 2 * n + n_out:n_in + 2 * n + n_out + n_scratch]
        sems = refs[n_in + 2 * n + n_out + n_scratch:]

        @pl.when(at(lambda size: 0))
        def _():
            _exchange_start(rin, rout, sems, gather)

        body(*ins, *outs, *scratch)

        @pl.when(at(lambda size: size - 1))
        def _():
            _exchange_wait(rin, rout, sems, gather)

    hbm, out_shape, sems = _exchange_plumbing(arrays, gather)
    return wrapped, hbm, hbm, out_shape, sems


def _sum_contributions(c_ref):
    g = c_ref[0].astype(F32)
    for j in range(1, c_ref.shape[0]):
        g = g + c_ref[j].astype(F32)
    return g


def _adamw_update(g, w_ref, m_ref, v_ref, g_ref, d_ref, nm_ref, nv_ref):
    m_new = ADAM_B1 * m_ref[...] + (1.0 - ADAM_B1) * g
    v_new = ADAM_B2 * v_ref[...] + (1.0 - ADAM_B2) * (g * g)
    m_hat = m_new / (1.0 - ADAM_B1 ** ADAM_STEP)
    v_hat = v_new / (1.0 - ADAM_B2 ** ADAM_STEP)
    g_ref[...] = g
    d_ref[...] = -ADAM_LR * (m_hat / (jnp.sqrt(v_hat) + ADAM_EPS) + ADAM_WD * w_ref[...])
    nm_ref[...] = m_new
    nv_ref[...] = v_new


def _adamw_layers(w, m, v, contrib0, contrib1, name):
    _, r, c = w.shape
    br = _pick(r, (256, 128, 64, 32, 16, 8))
    nb = r // br

    def body(w_ref, m_ref, v_ref, c0_ref, c1_ref, g_ref, d_ref, nm_ref, nv_ref):
        g = jnp.where(pl.program_id(0) == 0, _sum_contributions(c0_ref), _sum_contributions(c1_ref))
        _adamw_update(g, w_ref, m_ref, v_ref, g_ref, d_ref, nm_ref, nv_ref)

    spec = pl.BlockSpec((None, br, c), lambda l, i: (l, i, 0))
    return pl.pallas_call(
        body,
        grid=(DEPTH, nb),
        in_specs=[spec, spec, spec,
                  pl.BlockSpec((contrib0.shape[0], br, c), lambda l, i: (0, jnp.where(l == 0, i, nb - 1), 0)),
                  pl.BlockSpec((contrib1.shape[0], br, c), lambda l, i: (0, jnp.where(l == 1, i, 0), 0))],
        out_specs=[spec] * 4,
        out_shape=[jax.ShapeDtypeStruct(w.shape, F32)] * 4,
        compiler_params=_cparams(("arbitrary", "arbitrary")),
        name=name,
    )(w, m, v, contrib0, contrib1)


def _adamw(w, m, v, contrib, name):
    r, c = w.shape
    br = _pick(r, (256, 128, 64, 32, 16, 8))

    def body(w_ref, m_ref, v_ref, c_ref, g_ref, d_ref, nm_ref, nv_ref):
        _adamw_update(_sum_contributions(c_ref), w_ref, m_ref, v_ref, g_ref, d_ref, nm_ref, nv_ref)

    spec = pl.BlockSpec((br, c), lambda i: (i, 0))
    cspec = pl.BlockSpec((contrib.shape[0], br, c), lambda i: (0, i, 0))
    return pl.pallas_call(
        body,
        grid=(r // br,),
        in_specs=[spec, spec, spec, cspec],
        out_specs=[spec] * 4,
        out_shape=[jax.ShapeDtypeStruct((r, c), F32)] * 4,
        compiler_params=_cparams(("parallel",)),
        name=name,
    )(w, m, v, contrib)


def _silu(x):
    return x * jax.nn.sigmoid(x)


def _rope_angles(pos, n_freq):
    inv = ROPE_BASE ** (-jnp.arange(n_freq, dtype=F32) / n_freq)
    return pos[:, None] * inv[None, :]


def _with_ctx_rows(cos, sin):
    return (jnp.concatenate([jnp.ones((CTX_LEN, 128), F32), cos], axis=0),
            jnp.concatenate([jnp.zeros((CTX_LEN, 128), F32), sin], axis=0))


def _rope_tables():
    rows_n = SEQ // GRID_W
    rows = jnp.repeat(jnp.arange(rows_n, dtype=F32), GRID_W)
    cols = jnp.tile(jnp.arange(GRID_W, dtype=F32), rows_n)
    ang_r = _rope_angles(rows, B_HD // 4)
    ang_c = _rope_angles(cols, B_HD // 4)
    cos_b = jnp.tile(jnp.concatenate([jnp.cos(ang_r)] * 2 + [jnp.cos(ang_c)] * 2, axis=1), (1, 2))
    sin_b = jnp.tile(jnp.concatenate([-jnp.sin(ang_r), jnp.sin(ang_r), -jnp.sin(ang_c), jnp.sin(ang_c)], axis=1), (1, 2))
    ang = _rope_angles(jnp.arange(SEQ, dtype=F32), C_HD // 2)
    cos_c = jnp.concatenate([jnp.cos(ang)] * 2, axis=1)
    sin_c = jnp.concatenate([-jnp.sin(ang), jnp.sin(ang)], axis=1)
    return _with_ctx_rows(cos_b, sin_b), _with_ctx_rows(cos_c, sin_c)


def _halves(a):
    return a[:4], a[4:]


def _delta_gates(ab, a_log, dt_bias):
    beta = jax.nn.sigmoid(ab[:, :8])
    g = -jnp.exp(a_log)[None, :] * jax.nn.softplus(ab[:, 8:] + dt_bias[None, :])
    gch = g.reshape(N_CHUNK, CHUNK, 8)
    tri = jnp.tril(jnp.ones((CHUNK, CHUNK), F32))
    fwd = jnp.einsum("ij,cjh->cih", tri, gch[..., :4], precision=HIGHEST)
    bwd = jnp.einsum("ji,cjh->cih", tri, gch[..., 4:], precision=HIGHEST)
    gc = jnp.concatenate([fwd, bwd], axis=-1)
    gl = jnp.sum(gch, axis=1)
    rows = lambda a: _halves(a.transpose(2, 0, 1)[:, :, None, :])
    return rows(beta.reshape(N_CHUNK, CHUNK, 8)), rows(gc), _halves(gl.T[:, :, None, None])


def _ret_consts(c_decay):
    lg = jax.nn.log_sigmoid(c_decay)
    idx = jnp.arange(RET_CHUNK, dtype=F32)
    diff = idx[:, None] - idx[None, :]
    lgf, lgb = lg[:4, None, None], lg[4:, None, None]
    dm = jnp.concatenate([jnp.exp(jnp.where(diff >= 0, diff * lgf, -jnp.inf)),
                          jnp.exp(jnp.where(diff <= 0, -diff * lgb, -jnp.inf))], axis=0)
    qs = jnp.concatenate([jnp.exp((idx + 1.0)[None, :] * lg[:4, None]),
                          jnp.exp((RET_CHUNK - idx)[None, :] * lg[4:, None])], axis=0)[:, :, None]
    ks = jnp.concatenate([jnp.exp((RET_CHUNK - 1.0 - idx)[None, :] * lg[:4, None]),
                          jnp.exp(idx[None, :] * lg[4:, None])], axis=0)[:, :, None]
    return dm, qs, ks, jnp.exp(RET_CHUNK * lg)[:, None, None]


A_PIECES = ((0, True, A_DK ** -0.5, "a_q"), (1, True, 1.0, "a_k"), (2, False, 1.0, "a_v"))
B_ROPE_COLS = [C_BQ // 512, C_BKV // 256, 0, 0]
C_ROPE_COLS = [C_CQ // 512, C_CK // 512, 0, 0]


def _conv8(conv_w):
    return jnp.pad(conv_w, ((0, 8 - A_CONV), (0, 0)))


W_IN_TILES = {"nn": (2176, 512, 1024), "nt": (1088, 1024, 4352), "db": (1024, 512, ROWS)}


def _core_forward(h, w16, p, rides):
    res = _matmul(h, w16, "w_in", "nn", W_IN_TILES["nn"], ride=rides.get("w_in"))
    proj, rode = (res[0], {"w_in": res[1:]}) if "w_in" in rides else (res, {})
    wb = p["w_branch"] if "w_in" not in rides else _unshard_layer("w_branch", rode["w_in"][0])
    (cos_b, sin_b), (cos_c, sin_c) = _rope_tables()
    conv8 = _conv8(p["a_conv_w"])
    q, k, v = [_a_prep_fwd(proj, conv8, col, nrm, scl, nm) for col, nrm, scl, nm in A_PIECES]
    gates = _delta_gates(proj[:, C_AB:C_AB + 16], p["a_log"], p["a_dt_bias"])
    res = _delta_fwd_call(q, k, v, *gates, ride=rides.get("delta"))
    (of, orv, ssf, ssr, tsf, tsr), rode["delta"] = res[:6], res[6:]

    qb, kvb = _b_rope.fwd([proj, proj, cos_b, sin_b], [], B_ROPE_COLS)
    res = _attn_fwd_call(qb, kvb, p["b_sink"], ride=rides.get("attn"))
    ob, rode["attn"] = res[0], res[1:]

    qc, kc = _c_rope.fwd([proj, proj, cos_c, sin_c], [], C_ROPE_COLS)
    res = _ret_fwd_call(qc, kc, proj, C_CV // 512, *_ret_consts(p["c_decay"]), ride=rides.get("ret"))
    (cf, cr, csf, csr), rode["ret"] = res[:4], res[4:]

    (merged,) = _mix_merge.fwd([(of, orv), proj, ob, proj, (cf, cr), proj, proj, proj, proj],
                               [p["a_norm_w"][None, :], p["c_norm_w"][None, :], wb[0], wb[1], wb[2]], MIX_MERGE_COLS)
    saved = dict(proj=proj, q=q, k=k, v=v, of=of, orv=orv, ss=(ssf, ssr), ts=(tsf, tsr), qb=qb, kvb=kvb, ob=ob,
                 qc=qc, kc=kc, cf=cf, cr=cr, cs=(csf, csr), wb=wb)
    return merged, saved, rode


def _core_backward(h, w16, p, s, dmerged, rides, branch_rides_in_attn=False):
    proj, wb = s["proj"], s["wb"]
    (cos_b, sin_b), (cos_c, sin_c) = _rope_tables()
    conv8 = _conv8(p["a_conv_w"])
    rode = {}

    (do_a, daz, dob, dbz, do_c, dcz, dma, dmb, dmc), (danw, dcnw, *dwb) = _mix_merge.bwd(
        [(s["of"], s["orv"]), proj, s["ob"], proj, (s["cf"], s["cr"]), proj, proj, proj, proj],
        [p["a_norm_w"][None, :], p["c_norm_w"][None, :], wb[0], wb[1], wb[2]], [dmerged], MIX_MERGE_COLS,
        bf16_rows=(1, 3, 5, 6, 7, 8))
    dwb = jnp.stack(dwb)

    consts, consts_vjp = jax.vjp(_ret_consts, p["c_decay"])
    g = _ret_bwd_call(s["qc"], s["kc"], proj, C_CV // 512, *consts, s["cs"], do_c, ride=rides.get("ret"))
    rode["ret"] = g[10:]
    (dcq, dck), _ = _c_rope.bwd([proj, proj, cos_c, sin_c], [], [(g[0], g[1]), (g[2], g[3])], C_ROPE_COLS,
                                bf16_rows=(0, 1))
    dcv = (g[4] + g[5]).astype(BF16)
    (dc_decay,) = consts_vjp(tuple(g[6:10]))

    attn_ride = rides.get("attn")
    if branch_rides_in_attn:
        attn_ride = (list(attn_ride[0]) + [_reshard_layer("w_branch", dwb).astype(BF16)], attn_ride[1])
    res = _attn_bwd_call(s["qb"], s["kvb"], p["b_sink"], dob, ride=attn_ride)
    (dqb, dkvb, dsink), rode["attn"] = res[:3], res[3:]
    (dbq, dbkv), _ = _b_rope.bwd([proj, proj, cos_b, sin_b], [], [dqb, dkvb], B_ROPE_COLS, bf16_rows=(0, 1))

    ab = proj[:, C_AB:C_AB + 16]
    gates, gates_vjp = jax.vjp(_delta_gates, ab, p["a_log"], p["a_dt_bias"])
    g = _delta_bwd_call(s["q"], s["k"], s["v"], *gates, s["ss"], s["ts"], do_a, ride=rides.get("delta"))
    rode["delta"] = g[12:]
    dgates = ((g[6], g[7]), (g[8], g[9]), (g[10], g[11]))
    dab, da_log, ddt = gates_vjp(dgates)
    dpre, dconv = [], []
    for (col, nrm, scl, nm), df, dr in zip(A_PIECES, (g[0], g[2], g[4]), (g[1], g[3], g[5])):
        dx, dw = _a_prep_bwd(proj, conv8, col, nrm, scl, df, dr, nm)
        dpre.append(dx)
        dconv.append(dw[:A_CONV])

    dproj = jnp.concatenate(dpre + [daz, dbq, dbz, dcq, dck, dcv, dcz, dma, dmb, dmc, dbkv,
                                    jnp.pad(dab, ((0, 0), (0, IN_PAD - C_AB - 16))).astype(BF16)], axis=1)
    dh = _matmul(dproj, w16, "w_in_da", "nt", W_IN_TILES["nt"])
    dw = _matmul(h.T.astype(BF16), dproj, "w_in_db", "nn", W_IN_TILES["db"])
    dp = dict(a_conv_w=jnp.concatenate(dconv, axis=1), a_log=da_log, a_dt_bias=ddt, a_norm_w=danw[0],
              b_sink=dsink[:, 0], c_decay=dc_decay, c_norm_w=dcnw[0], w_branch=dwb)
    return dh, dw, dp, rode


CORE_PARAMS = ("a_conv_w", "a_log", "a_dt_bias", "a_norm_w", "b_sink", "c_decay", "c_norm_w", "w_branch")


W_IN_SHARD = IN_WIDTH // N_DEV
W_IN_RUNS = ((0, 2048, 0), (2064, 512, C_BQ), (2832, 512, C_BZ), (3344, 5120, C_CQ), (2576, 256, C_BKV),
             (2048, 16, C_AB))


def _shard_overlap(start, width, j):
    lo, hi = max(start, j * W_IN_SHARD), min(start + width, (j + 1) * W_IN_SHARD)
    return (lo, hi) if lo < hi else None


def _w_in_from_shards(g):
    parts = []
    for start, width, _ in W_IN_RUNS:
        for j in range(N_DEV):
            span = _shard_overlap(start, width, j)
            if span:
                parts.append(g[j, :, span[0] - j * W_IN_SHARD:span[1] - j * W_IN_SHARD])
    parts.append(jnp.zeros((D_MODEL, IN_PAD - IN_WIDTH), g.dtype))
    return jnp.concatenate(parts, axis=1)


def _w_in_blocks(dw):
    blocks = []
    for j in range(N_DEV):
        parts = []
        for start, width, pad in sorted(W_IN_RUNS):
            span = _shard_overlap(start, width, j)
            if span:
                parts.append(dw[:, pad + span[0] - start:pad + span[1] - start])
        blocks.append(jnp.concatenate(parts, axis=1))
    return jnp.stack(blocks)


LAYER_SHARDED = ("w_ada", "w_in", "w_branch", "w_out")


def _unshard_layer(name, g):
    if name == "w_branch":
        return g.transpose(1, 2, 0, 3).reshape(3, BR_WIDTH, D_MODEL)
    if name == "w_out":
        return g.reshape(D_MODEL, D_MODEL)
    return g.transpose(1, 0, 2).reshape(D_MODEL, -1)


def _reshard_layer(name, w):
    if name == "w_branch":
        return w.reshape(3, BR_WIDTH, N_DEV, D_MODEL // N_DEV).transpose(2, 0, 1, 3)
    if name == "w_out":
        return w.reshape(N_DEV, D_MODEL // N_DEV, D_MODEL)
    return w.reshape(D_MODEL, N_DEV, -1).transpose(1, 0, 2)


def _layer_weights(gathered):
    out = {n: _unshard_layer(n, g) for n, g in gathered.items() if n != "w_in"}
    out["w_in16"] = _w_in_from_shards(gathered["w_in"])
    return out


def _grad_blocks(name, g):
    return (_w_in_blocks(g) if name == "w_in" else _reshard_layer(name, g)).astype(BF16)


def _forward_backward(small, layer0, shards0, shards1, x, c, ctx, loss_target):
    c_ctx = small["c_ctx"]
    sc16 = jnp.zeros((16, D_MODEL), F32).at[0].set(_silu(c)).at[1].set(_silu(c_ctx))
    xs = jnp.concatenate([ctx, x], axis=0)
    weights = [dict(layer0), None]
    layers = []
    for l in range(DEPTH):
        wl = weights[l]
        mod16 = _matmul(sc16, wl["w_ada"], "ada") + small["b_ada"][l][None, :]
        mod_cx = jnp.stack([mod16[1], mod16[0]])
        shift, scale, gate = jnp.split(mod_cx, 3, axis=1)
        nw = small["norm_w"][l][None, :]
        (h,) = _norm_mod.fwd([xs], [nw, shift, scale])
        p = {n: small[n][l] for n in CORE_PARAMS if n != "w_branch"}
        p["w_branch"] = wl.get("w_branch")
        rides = {}
        if l == 0:
            rides = {"w_in": ([shards0["w_branch"], shards0["w_out"]], True), "delta": ([shards1["w_in"]], True),
                     "attn": ([shards1["w_ada"]], True), "ret": ([shards1["w_branch"], shards1["w_out"]], True)}
        merged, saved, rode = _core_forward(h, wl["w_in16"], p, rides)
        if l == 0:
            wl["w_out"] = _unshard_layer("w_out", rode["w_in"][1])
            weights[1] = _layer_weights(dict(w_in=rode["delta"][0], w_ada=rode["attn"][0],
                                             w_branch=rode["ret"][0], w_out=rode["ret"][1]))
        (xs_next,) = _out_residual.fwd([xs, merged], [wl["w_out"], gate])
        layers.append(dict(xs=xs, h=h, p=p, saved=saved, merged=merged, gate=gate, nw=nw, shift=shift, scale=scale))
        xs = xs_next
    fw = small["final_norm_w"][None, :]
    xs = xs[CTX_LEN:]
    (per_row,) = _loss_rows.fwd([xs, loss_target], [fw])
    loss = jnp.sum(per_row[:, 0])

    d_per_row = jnp.zeros((SEQ, 128), F32).at[:, 0].set(1.0)
    (dxs,), (dfw,) = _loss_rows.bwd([xs, loss_target], [fw], [d_per_row])
    dxs = jnp.pad(dxs, ((CTX_LEN, 0), (0, 0)))
    small_names = tuple(n for n in CORE_PARAMS if n != "w_branch") + ("b_ada", "norm_w")
    dsmall = {n: [None] * DEPTH for n in small_names}
    dlayer = [None] * DEPTH
    contrib0 = contrib1 = None
    dsc16 = jnp.zeros((16, D_MODEL), F32)
    for l in reversed(range(DEPTH)):
        s, wl = layers[l], weights[l]
        (dres, dmerged), (dw_out, dgate) = _out_residual.bwd([s["xs"], s["merged"]], [wl["w_out"], s["gate"]], [dxs])
        rides = {}
        if l == 0:
            blocks1 = {n: _grad_blocks(n, g) for n, g in dlayer[1].items()}
            rides = {"ret": ([blocks1["w_branch"], blocks1["w_out"]], False),
                     "attn": ([_reshard_layer("w_out", dw_out).astype(BF16), blocks1["w_ada"]], False),
                     "delta": ([blocks1["w_in"]], False)}
        dh, dw_in, dp, rode = _core_backward(s["h"], wl["w_in16"], s["p"], s["saved"], dmerged, rides,
                                             branch_rides_in_attn=(l == 0))
        if l == 0:
            contrib1 = dict(w_in=rode["delta"][0], w_ada=rode["attn"][1], w_branch=rode["ret"][0],
                            w_out=rode["ret"][1])
            contrib0 = dict(w_out=rode["attn"][0], w_branch=rode["attn"][2])
        (dxn,), (dnw, dshift, dscale) = _norm_mod.bwd([s["xs"]], [s["nw"], s["shift"], s["scale"]], [dh])
        dxs = dres + dxn
        dmod_cx = jnp.concatenate([dshift, dscale, dgate], axis=1)
        dmod16 = jnp.zeros((16, 3 * D_MODEL), F32).at[0].set(dmod_cx[1]).at[1].set(dmod_cx[0])
        dsc16 = dsc16 + _matmul(dmod16, wl["w_ada"], "ada_da", "nt")
        dlayer[l] = dict(w_ada=_matmul(sc16, dmod16, "ada_db", "tn"), w_in=dw_in,
                         w_branch=dp["w_branch"], w_out=dw_out)
        for n in small_names:
            if n in dp:
                dsmall[n][l] = dp[n]
        dsmall["norm_w"][l] = dnw[0]
        dsmall["b_ada"][l] = dmod_cx[0] + dmod_cx[1]
    gsmall = {n: jnp.stack(v) for n, v in dsmall.items()}
    gsmall["final_norm_w"] = dfw[0]
    sig = jax.nn.sigmoid(c_ctx)
    gsmall["c_ctx"] = dsc16[1] * sig * (1.0 + c_ctx * (1.0 - sig))
    return loss, dxs[CTX_LEN:], gsmall, {n: dlayer[0][n] for n in ("w_ada", "w_in")}, contrib0, contrib1


SMALL = ("c_ctx", "b_ada", "norm_w", "a_log", "a_dt_bias", "a_norm_w", "b_sink", "c_decay", "c_norm_w",
         "final_norm_w")
WEIGHTS = ("c_ctx", "w_ada", "b_ada", "norm_w", "w_in", "a_conv_w", "a_log", "a_dt_bias", "a_norm_w", "b_sink",
           "c_decay", "c_norm_w", "w_branch", "w_out", "final_norm_w")
SMALL_PACK = 12288


def _unshard_conv(g):
    return g.transpose(1, 2, 0, 3).reshape(DEPTH, A_CONV, 3 * A_WIDTH)


def _reshard_conv(w):
    return w.reshape(DEPTH, A_CONV, N_DEV, 3 * A_WIDTH // N_DEV).transpose(2, 0, 1, 3)


def _pack_small(tree):
    flat = jnp.concatenate([tree[n].reshape(-1) for n in SMALL])
    return jnp.pad(flat, (0, SMALL_PACK - flat.shape[0])).reshape(SMALL_PACK // 128, 128)


def _unpack_small(packed, like):
    flat = packed.reshape(-1)
    out, off = {}, 0
    for n in SMALL:
        size = math.prod(like[n].shape)
        out[n] = flat[off:off + size].reshape(like[n].shape)
        off += size
    return out


def kernel(x, c, ctx, c_ctx, w_ada, b_ada, norm_w, w_in, a_conv_w, a_log, a_dt_bias, a_norm_w, b_sink, c_decay, c_norm_w, w_branch, w_out, final_norm_w, loss_target, m_c_ctx, m_w_ada, m_b_ada, m_norm_w, m_w_in, m_a_conv_w, m_a_log, m_a_dt_bias, m_a_norm_w, m_b_sink, m_c_decay, m_c_norm_w, m_w_branch, m_w_out, m_final_norm_w, v_c_ctx, v_w_ada, v_b_ada, v_norm_w, v_w_in, v_a_conv_w, v_a_log, v_a_dt_bias, v_a_norm_w, v_b_sink, v_c_decay, v_c_norm_w, v_w_branch, v_w_out, v_final_norm_w):
    w = dict(c_ctx=c_ctx, w_ada=w_ada, b_ada=b_ada, norm_w=norm_w, w_in=w_in, a_conv_w=a_conv_w, a_log=a_log,
             a_dt_bias=a_dt_bias, a_norm_w=a_norm_w, b_sink=b_sink, c_decay=c_decay, c_norm_w=c_norm_w,
             w_branch=w_branch, w_out=w_out, final_norm_w=final_norm_w)
    m = dict(c_ctx=m_c_ctx, w_ada=m_w_ada, b_ada=m_b_ada, norm_w=m_norm_w, w_in=m_w_in, a_conv_w=m_a_conv_w,
             a_log=m_a_log, a_dt_bias=m_a_dt_bias, a_norm_w=m_a_norm_w, b_sink=m_b_sink, c_decay=m_c_decay,
             c_norm_w=m_c_norm_w, w_branch=m_w_branch, w_out=m_w_out, final_norm_w=m_final_norm_w)
    v = dict(c_ctx=v_c_ctx, w_ada=v_w_ada, b_ada=v_b_ada, norm_w=v_norm_w, w_in=v_w_in, a_conv_w=v_a_conv_w,
             a_log=v_a_log, a_dt_bias=v_a_dt_bias, a_norm_w=v_a_norm_w, b_sink=v_b_sink, c_decay=v_c_decay,
             c_norm_w=v_c_norm_w, w_branch=v_w_branch, w_out=v_w_out, final_norm_w=v_final_norm_w)

    shards = {n: w[n].astype(BF16) for n in LAYER_SHARDED}
    first = _exchange([shards["w_ada"][0], shards["w_in"][0], w["a_conv_w"]], True, "gather_layer0")
    layer0 = _layer_weights(dict(w_ada=first[0], w_in=first[1]))
    small_w = {n: w[n] for n in SMALL}
    small_w["a_conv_w"] = _unshard_conv(first[2])
    loss, gx, gw, glayer0, contrib0, contrib1 = _forward_backward(
        small_w, layer0, {n: shards[n][0] for n in ("w_branch", "w_out")}, {n: shards[n][1] for n in LAYER_SHARDED},
        x[0], c[0], ctx[0], loss_target[0])
    loss = lax.psum(loss, ("x", "y", "c"))

    last = _scatter_two_level([_reshard_layer("w_ada", glayer0["w_ada"]).astype(BF16),
                               _grad_blocks("w_in", glayer0["w_in"]), _reshard_conv(gw["a_conv_w"])],
                              "scatter_layer0")
    contrib0["w_ada"], contrib0["w_in"] = last[0], last[1]
    small = _exchange([_pack_small(gw)], True, "gather_small_grads")[0]

    grad, delta, new_m, new_v = {}, {}, {}, {}
    for n in LAYER_SHARDED:
        shp = w[n].shape
        per_layer = (math.prod(shp[1:-1]), shp[-1])
        outs = _adamw_layers(*[a.reshape((DEPTH,) + per_layer) for a in (w[n], m[n], v[n])],
                             *[cb.reshape(cb.shape[:1] + per_layer) for cb in (contrib0[n], contrib1[n])], "adamw_" + n)
        grad[n], delta[n], new_m[n], new_v[n] = [o.reshape(shp) for o in outs]
    shp = a_conv_w.shape
    two_d = (math.prod(shp[:-1]), shp[-1])
    outs = _adamw(*[a.reshape(two_d) for a in (a_conv_w, m_a_conv_w, v_a_conv_w)],
                  last[2].reshape(last[2].shape[:1] + two_d), "adamw_a_conv_w")
    grad["a_conv_w"], delta["a_conv_w"], new_m["a_conv_w"], new_v["a_conv_w"] = [o.reshape(shp) for o in outs]
    outs = _adamw(_pack_small(w), _pack_small(m), _pack_small(v), small, "adamw_small")
    for tree, packed in zip((grad, delta, new_m, new_v), outs):
        tree.update(_unpack_small(packed, w))

    return (loss, gx[None], *[grad[n] for n in WEIGHTS], *[delta[n] for n in WEIGHTS],
            *[new_m[n] for n in WEIGHTS], *[new_v[n] for n in WEIGHTS])
```

```python
import functools
import math

import jax
import jax.numpy as jnp
from jax import lax
from jax.experimental import pallas as pl
from jax.experimental.pallas import tpu as pltpu

F32 = jnp.float32
BF16 = jnp.bfloat16
HIGHEST = lax.Precision.HIGHEST

D_MODEL = 1024
SEQ = 4096
DEPTH = 2
GRID_W = 64
CTX_LEN = 256
EPS = 1e-6
ROPE_BASE = 10000.0
BR_WIDTH = D_MODEL // 2
A_DK = 128
A_HEADS = 4
A_WIDTH = 512
A_CONV = 5
B_HD = 64
B_Q_HEADS = 8
B_KV_HEADS = 2
WINDOW = 128
B_BLOCK = 128
C_HD = 128
C_HEADS = 4
C_WIDTH = 512
CHUNK = 128
RET_CHUNK = 256
ADAM_LR = 0.001
ADAM_B1 = 0.9
ADAM_B2 = 0.999
ADAM_EPS = 1e-08
ADAM_WD = 0.01
ADAM_STEP = 10

N_DEV = 8
ROWS = CTX_LEN + SEQ
N_CHUNK = ROWS // CHUNK
IN_WIDTH = 8464
IN_PAD = 8704
NEG = -1e30

VMEM_LIMIT = 56 * 1024 * 1024
MESH = pl.DeviceIdType.MESH

C_AQ, C_AK, C_AV, C_AZ, C_BQ, C_BZ, C_CQ, C_CK, C_CV, C_CZ = (i * 512 for i in range(10))
C_MERGE = 5120
C_BKV = 8192
C_AB = 8448


def _cparams(sem=None):
    if sem is None:
        return pltpu.CompilerParams(vmem_limit_bytes=VMEM_LIMIT)
    return pltpu.CompilerParams(dimension_semantics=sem, vmem_limit_bytes=VMEM_LIMIT)


def _dg(a, b, ca, cb, prec=None):
    return lax.dot_general(a, b, (((ca,), (cb,)), ((), ())), preferred_element_type=F32, precision=prec)


@functools.partial(jax.custom_vjp, nondiff_argnums=(2, 3))
def _bdot(a, b, ca, cb):
    return _dg(a.astype(BF16), b.astype(BF16), ca, cb)


def _bdot_fwd(a, b, ca, cb):
    return _bdot(a, b, ca, cb), (a, b)


def _bdot_bwd(ca, cb, res, ct):
    a, b = res
    da = _bdot(ct, b, 1, 1 - cb) if ca == 1 else _bdot(b, ct, 1 - cb, 1)
    db = _bdot(a, ct, 1 - ca, 0) if cb == 0 else _bdot(ct, a, 0, 1 - ca)
    return da, db


_bdot.defvjp(_bdot_fwd, _bdot_bwd)


def _hdot(a, b):
    return _dg(a, b, 1, 0, lax.Precision.HIGH)


def _k_silu(x):
    return x / (1.0 + jnp.exp(-x))


def _k_sigmoid(x):
    return 1.0 / (1.0 + jnp.exp(-x))


@jax.custom_vjp
def _swap64(x):
    return pltpu.roll(x, 64, 1)


_swap64.defvjp(lambda x: (pltpu.roll(x, 64, 1), None), lambda _, ct: (pltpu.roll(ct, 64, 1),))


def _swap16_impl(x):
    lane = lax.broadcasted_iota(jnp.int32, x.shape, 1)
    return jnp.where((lane & 16) == 0, pltpu.roll(x, 112, 1), pltpu.roll(x, 16, 1))


@jax.custom_vjp
def _swap16(x):
    return _swap16_impl(x)


_swap16.defvjp(lambda x: (_swap16_impl(x), None), lambda _, ct: (_swap16_impl(ct),))


def _pick(dim, prefs):
    for p in prefs:
        if dim % p == 0:
            return p
    return dim


def _matmul(a, b, name, mode="nn", tiles=None, ride=None):
    ca, cb = {"nn": (1, 0), "nt": (1, 1), "tn": (0, 0)}[mode]
    m, k = a.shape[1 - ca], a.shape[ca]
    n = b.shape[1 - cb]
    if tiles is None:
        tiles = (_pick(m, (1088, 1024, 512, 256, 128)), _pick(n, (512, 256, 128)),
                 _pick(k, (1088, 1024, 512, 256, 128) if mode == "tn" else (2176, 2048, 1024, 512, 256, 128)))
    tm, tn, tk = tiles
    nk = k // tk
    a_spec = (pl.BlockSpec((tm, tk), lambda i, j, kk: (i, kk)) if ca == 1
              else pl.BlockSpec((tk, tm), lambda i, j, kk: (kk, i)))
    b_spec = (pl.BlockSpec((tk, tn), lambda i, j, kk: (kk, j)) if cb == 0
              else pl.BlockSpec((tn, tk), lambda i, j, kk: (j, kk)))

    def body(a_ref, b_ref, o_ref):
        part = _dg(a_ref[...].astype(BF16), b_ref[...].astype(BF16), ca, cb)
        if nk == 1:
            o_ref[...] = part
        else:
            kk = pl.program_id(2)

            @pl.when(kk == 0)
            def _():
                o_ref[...] = part

            @pl.when(kk > 0)
            def _():
                o_ref[...] += part

    grid = (m // tm, n // tn, nk)
    if ride is None:
        return pl.pallas_call(
            body,
            grid=grid,
            in_specs=[a_spec, b_spec],
            out_specs=pl.BlockSpec((tm, tn), lambda i, j, kk: (i, j)),
            out_shape=jax.ShapeDtypeStruct((m, n), F32),
            compiler_params=_cparams(("parallel", "parallel", "arbitrary")),
            name=name,
        )(a, b)
    body, r_in, r_out, r_shape, r_scratch = _riding(body, 2, 1, 0, ride, grid)
    return pl.pallas_call(
        body,
        grid=grid,
        in_specs=[a_spec, b_spec] + r_in,
        out_specs=[pl.BlockSpec((tm, tn), lambda i, j, kk: (i, j))] + r_out,
        out_shape=[jax.ShapeDtypeStruct((m, n), F32)] + r_shape,
        scratch_shapes=r_scratch,
        compiler_params=_cparams(("arbitrary", "arbitrary", "arbitrary")),
        name=name,
    )(a, b, *ride[0])


ROW_BLOCK = 256
ROW_VMEM_BUDGET = 16 * 1024 * 1024


def _pieces(val, pw):
    return [val[:, j * pw:(j + 1) * pw] for j in range(val.shape[1] // pw)]


def _flat(groups):
    arrays, sizes = [], []
    for g in groups:
        g = g if isinstance(g, (tuple, list)) else (g,)
        arrays += list(g)
        sizes.append(len(g))
    return arrays, sizes


def _regroup(refs, sizes):
    out, at = [], 0
    for n in sizes:
        val = refs[at][...]
        for r in refs[at + 1:at + n]:
            val = val + r[...]
        out.append(val)
        at += n
    return out


class _Rowwise:
    def __init__(self, fn, name, row_wpw, par_pw, out_wpw, n_diff=None, block_rows=None):
        self.fn, self.name, self.row_wpw, self.par_pw, self.out_wpw = fn, name, row_wpw, par_pw, out_wpw
        self.n_diff = len(row_wpw) if n_diff is None else n_diff
        self.block_rows = block_rows

    def _load(self, row_vals, par_refs, br, with_ctx):
        row = pl.program_id(0) * br + lax.broadcasted_iota(jnp.int32, (br, 1), 0)
        is_ctx = (row < (CTX_LEN if with_ctx else 0)).astype(F32)
        rows = [_pieces(v, pw) for v, (_, pw) in zip(row_vals, self.row_wpw)]
        pars = []
        for p, pw in zip(par_refs, self.par_pw):
            val = p[...].astype(F32)
            if p.shape[0] == 2:
                val = is_ctx * val[0:1, :] + (1.0 - is_ctx) * val[1:2, :]
            pars.append(_pieces(val, pw))
        return rows, pars, is_ctx

    def _block_rows(self, n_rows, widths):
        if self.block_rows:
            return self.block_rows
        for br in (1088, 1024, 544, 512, 272):
            if n_rows % br == 0 and 2 * 4 * br * sum(widths) <= ROW_VMEM_BUDGET:
                return br
        return ROW_BLOCK

    def _row_specs(self, br, sizes, cols):
        out = []
        for (w, _), n, c in zip(self.row_wpw, sizes, cols):
            out += [pl.BlockSpec((br, w), lambda i, c=c: (i, c))] * n
        return out

    def fwd(self, rows, params, cols=None):
        arrays, sizes = _flat(rows)
        cols = cols or [0] * len(rows)
        n_rows = arrays[0].shape[0]
        n_in = len(arrays)
        br = self._block_rows(n_rows, [w for (w, _), n in zip(self.row_wpw, sizes) for _ in range(n)]
                              + [w for w, _ in self.out_wpw])

        def body(*refs):
            r, p, _ = self._load(_regroup(refs[:n_in], sizes), refs[n_in:n_in + len(params)], br, n_rows == ROWS)
            for o_ref, pieces, (_, pw) in zip(refs[n_in + len(params):], self.fn(r, p), self.out_wpw):
                for j, piece in enumerate(pieces):
                    o_ref[:, j * pw:(j + 1) * pw] = piece

        return pl.pallas_call(
            body,
            grid=(n_rows // br,),
            in_specs=self._row_specs(br, sizes, cols) + [pl.BlockSpec(p.shape, lambda i: (0, 0)) for p in params],
            out_specs=[pl.BlockSpec((br, w), lambda i: (i, 0)) for w, _ in self.out_wpw],
            out_shape=[jax.ShapeDtypeStruct((n_rows, w), F32) for w, _ in self.out_wpw],
            compiler_params=_cparams(("parallel",)),
            name=self.name + "_fwd",
        )(*arrays, *params)

    def bwd(self, rows, params, douts, cols=None, bf16_rows=()):
        arrays, sizes = _flat(rows)
        darrays, dsizes = _flat(douts)
        cols = cols or [0] * len(rows)
        n_rows = arrays[0].shape[0]
        n_in, n_par, n_dout, n_diff = len(arrays), len(params), len(darrays), self.n_diff
        br = self._block_rows(n_rows, [w for (w, _), n in zip(self.row_wpw, sizes) for _ in range(n)]
                              + [w for (w, _), n in zip(self.out_wpw, dsizes) for _ in range(n)]
                              + [w for w, _ in self.row_wpw[:n_diff]])

        def body(*refs):
            par_refs = refs[n_in:n_in + n_par]
            dout_refs = refs[n_in + n_par:n_in + n_par + n_dout]
            drow_refs = refs[n_in + n_par + n_dout:n_in + n_par + n_dout + n_diff]
            dpar_refs = refs[n_in + n_par + n_dout + n_diff:]

            @pl.when(pl.program_id(0) == 0)
            def _():
                for d in dpar_refs:
                    d[...] = jnp.zeros_like(d)

            r, p, is_ctx = self._load(_regroup(refs[:n_in], sizes), par_refs, br, n_rows == ROWS)
            cts = [_pieces(d, pw) for d, (_, pw) in zip(_regroup(dout_refs, dsizes), self.out_wpw)]
            fixed = r[n_diff:]
            _, vjp = jax.vjp(lambda rd, pp: self.fn(rd + fixed, pp), r[:n_diff], p)
            dr, dp = vjp(cts)
            for d_ref, pieces, (_, pw) in zip(drow_refs, dr, self.row_wpw):
                for j, piece in enumerate(pieces):
                    d_ref[:, j * pw:(j + 1) * pw] = piece.astype(d_ref.dtype)
            for d_ref, pieces, pw in zip(dpar_refs, dp, self.par_pw):
                for j, piece in enumerate(pieces):
                    lanes = slice(j * pw, (j + 1) * pw)
                    if d_ref.shape[0] != 2:
                        d_ref[:, lanes] += piece
                    else:
                        d_ref[0:1, lanes] += jnp.sum(is_ctx * piece, axis=0, keepdims=True)
                        d_ref[1:2, lanes] += jnp.sum((1.0 - is_ctx) * piece, axis=0, keepdims=True)

        par_specs = [pl.BlockSpec(p.shape, lambda i: (0, 0)) for p in params]
        dout_specs = []
        for (w, _), n in zip(self.out_wpw, dsizes):
            dout_specs += [pl.BlockSpec((br, w), lambda i: (i, 0))] * n
        drow_w = [w for w, _ in self.row_wpw[:n_diff]]
        g = pl.pallas_call(
            body,
            grid=(n_rows // br,),
            in_specs=self._row_specs(br, sizes, cols) + par_specs + dout_specs,
            out_specs=[pl.BlockSpec((br, w), lambda i: (i, 0)) for w in drow_w] + par_specs,
            out_shape=[jax.ShapeDtypeStruct((n_rows, w), BF16 if a in bf16_rows else F32) for a, w in enumerate(drow_w)]
            + [jax.ShapeDtypeStruct(p.shape, F32) for p in params],
            compiler_params=_cparams(("arbitrary",)),
            name=self.name + "_bwd",
        )(*arrays, *params, *darrays)
        return list(g[:n_diff]), list(g[n_diff:])


def _fn_norm_mod(rows, pars):
    (x,), (nw,), (shift,), (scale,) = rows[0], pars[0], pars[1], pars[2]
    y = x * lax.rsqrt(jnp.mean(x * x, axis=-1, keepdims=True) + EPS) * nw
    return [[y * (1.0 + scale) + shift]]


def _fn_head_rms_gate(rows, pars):
    (w,) = pars[0]
    return [[o * lax.rsqrt(jnp.mean(o * o, axis=-1, keepdims=True) + EPS) * w * _k_silu(z)
             for o, z in zip(rows[0], rows[1])]]


def _fn_group_norm_gate(rows, pars):
    out = []
    for o, z, w in zip(rows[0], rows[1], pars[0]):
        mu = jnp.mean(o, axis=-1, keepdims=True)
        var = jnp.mean(jnp.square(o - mu), axis=-1, keepdims=True)
        out.append((o - mu) * lax.rsqrt(var + EPS) * w * _k_silu(z))
    return [out]


def _fn_mix_merge(rows, pars):
    oa, za, (ob,), (zb,), oc, zc, (ma,), (mb,), (mc,) = rows
    na, nc, (wa,), (wb,), (wc,) = pars
    ya = jnp.concatenate(_fn_head_rms_gate([oa, za], [na])[0], axis=1)
    yb = ob * _k_silu(zb)
    yc = jnp.concatenate(_fn_group_norm_gate([oc, zc], [nc])[0], axis=1)
    return [[_k_sigmoid(ma) * _bdot(ya, wa, 1, 0) + _k_sigmoid(mb) * _bdot(yb, wb, 1, 0)
             + _k_sigmoid(mc) * _bdot(yc, wc, 1, 0)]]


def _fn_out_residual(rows, pars):
    (res,), (merged,), (w,), (gate,) = rows[0], rows[1], pars[0], pars[1]
    return [[res + gate * _bdot(merged, w, 1, 0)]]


def _fn_loss(rows, pars):
    (x,), (target,), (w,) = rows[0], rows[1], pars[0]
    y = x * lax.rsqrt(jnp.mean(x * x, axis=-1, keepdims=True) + EPS) * w
    per_row = 0.5 * jnp.mean(jnp.square(y - target), axis=-1, keepdims=True)
    return [[jnp.broadcast_to(per_row, (per_row.shape[0], 128))]]


def _fn_b_rope(rows, pars):
    q, (k, v), (cos,), (sin,) = rows
    rot = lambda x: x * cos + _swap16(x) * sin
    return [[rot(x) for x in q], [rot(k), v]]


def _fn_c_rope(rows, pars):
    q, k, (cos,), (sin,) = rows
    rot = lambda x: x * cos + _swap64(x) * sin
    return [[rot(x) for x in q], [rot(x) * (C_HD ** -0.5) for x in k]]


_norm_mod = _Rowwise(_fn_norm_mod, "norm_mod", [(D_MODEL, D_MODEL)], [D_MODEL] * 3, [(D_MODEL, D_MODEL)])
_out_residual = _Rowwise(_fn_out_residual, "out_residual", [(D_MODEL, D_MODEL)] * 2, [D_MODEL] * 2,
                         [(D_MODEL, D_MODEL)], block_rows=544)
_loss_rows = _Rowwise(_fn_loss, "loss", [(D_MODEL, D_MODEL)] * 2, [D_MODEL], [(128, 128)], n_diff=1)
_mix_merge = _Rowwise(_fn_mix_merge, "mix_merge",
                      [(512, 128), (512, 128), (512, 512), (512, 512), (512, 128), (512, 128)] + [(D_MODEL, D_MODEL)] * 3,
                      [128, 128] + [D_MODEL] * 3, [(D_MODEL, D_MODEL)], block_rows=256)
MIX_MERGE_COLS = [0, C_AZ // 512, 0, C_BZ // 512, 0, C_CZ // 512] + [C_MERGE // 1024 + i for i in range(3)]
_b_rope = _Rowwise(_fn_b_rope, "b_rope", [(512, 128), (256, 128), (128, 128), (128, 128)], [],
                   [(512, 128), (256, 128)], n_diff=2)
_c_rope = _Rowwise(_fn_c_rope, "c_rope", [(512, 128), (512, 128), (128, 128), (128, 128)], [],
                   [(512, 128), (512, 128)], n_diff=2)


HALO = 8
EXT = ROW_BLOCK + 2 * HALO


def _halo_specs(col, width=512):
    last = ROWS // HALO - 1
    per = ROW_BLOCK // HALO
    prev = pl.BlockSpec((HALO, width), lambda i: (jnp.maximum(i * per - 1, 0), col))
    cur = pl.BlockSpec((ROW_BLOCK, width), lambda i: (i, col))
    nxt = pl.BlockSpec((HALO, width), lambda i: (jnp.minimum((i + 1) * per, last), col))
    return [prev, cur, nxt]


def _extended(prev_ref, cur_ref, next_ref):
    i = pl.program_id(0)
    prev_ok = i >= 2
    next_ok = jnp.logical_and(i >= 1, i < ROWS // ROW_BLOCK - 1)
    return jnp.concatenate([jnp.where(prev_ok, prev_ref[...], 0.0), cur_ref[...],
                            jnp.where(next_ok, next_ref[...], 0.0)], axis=0)


def _conv_taps(x_ext, w_ref, flip):
    acc = None
    for j in range(A_CONV):
        shift = (j - 2) if flip else (2 - j)
        term = w_ref[j:j + 1, :] * pltpu.roll(x_ext, shift % EXT, 0)
        acc = term if acc is None else acc + term
    return acc


def _conv_post(pre_pieces, normalize, scale):
    out = []
    for p in pre_pieces:
        y = _k_silu(p)
        if normalize:
            y = y * lax.rsqrt(jnp.sum(y * y, axis=-1, keepdims=True) + EPS) * scale
        out.append(y)
    return out


def _a_prep_fwd(proj, conv8, col, normalize, scale, name):
    def body(prev_ref, cur_ref, next_ref, w_ref, o_ref):
        pre = _conv_taps(_extended(prev_ref, cur_ref, next_ref), w_ref, False)[HALO:HALO + ROW_BLOCK]
        for h, y in enumerate(_conv_post(_pieces(pre, 128), normalize, scale)):
            o_ref[:, h * 128:(h + 1) * 128] = y

    return pl.pallas_call(
        body,
        grid=(ROWS // ROW_BLOCK,),
        in_specs=_halo_specs(col) + [pl.BlockSpec((8, 512), lambda i: (0, col))],
        out_specs=pl.BlockSpec((ROW_BLOCK, 512), lambda i: (i, 0)),
        out_shape=jax.ShapeDtypeStruct((ROWS, 512), F32),
        compiler_params=_cparams(("parallel",)),
        name=name + "_fwd",
    )(proj, proj, proj, conv8)


def _a_prep_bwd(proj, conv8, col, normalize, scale, dout_f, dout_r, name):
    def body(xp, xc, xn, w_ref, fp, fc, fn_, rp, rc, rn, dx_ref, dw_ref):
        @pl.when(pl.program_id(0) == 0)
        def _():
            dw_ref[...] = jnp.zeros_like(dw_ref)

        x_ext = _extended(xp, xc, xn)
        dout = _extended(fp, fc, fn_) + _extended(rp, rc, rn)
        pre = _conv_taps(x_ext, w_ref, False)
        _, vjp = jax.vjp(lambda p: _conv_post(p, normalize, scale), _pieces(pre, 128))
        (dpre,) = vjp(_pieces(dout, 128))
        dpre = jnp.concatenate(dpre, axis=1)
        dx_ref[...] = _conv_taps(dpre, w_ref, True)[HALO:HALO + ROW_BLOCK].astype(BF16)
        own = dpre[HALO:HALO + ROW_BLOCK]
        for j in range(A_CONV):
            shifted = pltpu.roll(x_ext, (2 - j) % EXT, 0)[HALO:HALO + ROW_BLOCK]
            dw_ref[j:j + 1, :] += jnp.sum(own * shifted, axis=0, keepdims=True)

    return pl.pallas_call(
        body,
        grid=(ROWS // ROW_BLOCK,),
        in_specs=_halo_specs(col) + [pl.BlockSpec((8, 512), lambda i: (0, col))] + _halo_specs(0) + _halo_specs(0),
        out_specs=[pl.BlockSpec((ROW_BLOCK, 512), lambda i: (i, 0)), pl.BlockSpec((8, 512), lambda i: (0, 0))],
        out_shape=[jax.ShapeDtypeStruct((ROWS, 512), BF16), jax.ShapeDtypeStruct((8, 512), F32)],
        compiler_params=_cparams(("arbitrary",)),
        name=name + "_bwd",
    )(proj, proj, proj, conv8, dout_f, dout_f, dout_f, dout_r, dout_r, dout_r)


N_CHAIN = 8


def _rev_chunk(s, chunk):
    n_ctx, n_all = CTX_LEN // chunk, ROWS // chunk
    return jnp.where(s < n_ctx, n_ctx - 1 - s, n_all + n_ctx - 1 - s)


def _scan_specs(step_of, chunk, v_col=0):
    cf = step_of
    cr = lambda n: _rev_chunk(step_of(n), chunk)

    def pair(shape, index):
        return (pl.BlockSpec(shape, lambda n: index(cf(n))), pl.BlockSpec(shape, lambda n: index(cr(n))))

    return dict(
        tok=pair((chunk, 512), lambda c: (c, 0)),
        tokv=pair((chunk, 512), lambda c: (c, v_col)),
        row=pair((4, 1, 1, chunk), lambda c: (0, c, 0, 0)),
        one=pair((4, 1, 1, 1), lambda c: (0, c, 0, 0)),
        state=pair((None, 4, 128, 128), lambda c: (c, 0, 0, 0)),
        tinv=pair((None, 4, chunk, chunk), lambda c: (c, 0, 0, 0)),
    )


def _both(specs, kinds):
    out = []
    for kind in kinds:
        out += list(specs[kind])
    return out


def _scan_call(body, name, in_specs, out_specs, out_shape, operands, ride, chunk):
    grid = (ROWS // chunk,)
    body, r_in, r_out, r_shape, r_scratch = _riding(body, len(in_specs), len(out_specs), 1, ride, grid)
    return pl.pallas_call(
        body,
        grid=grid,
        in_specs=in_specs + r_in,
        out_specs=out_specs + r_out,
        out_shape=out_shape + r_shape,
        scratch_shapes=[pltpu.VMEM((N_CHAIN, 128, 128), F32)] + r_scratch,
        compiler_params=_cparams(("arbitrary",)),
        name=name,
    )(*operands, *(ride[0] if ride else []))


def _chain_masks():
    ii = lax.broadcasted_iota(jnp.int32, (CHUNK, CHUNK), 0)
    jj = lax.broadcasted_iota(jnp.int32, (CHUNK, CHUNK), 1)
    eye = jnp.where(ii == jj, 1.0, 0.0).astype(F32)
    lower = (ii >= jj, ii > jj)
    upper = (ii <= jj, ii < jj)
    return [lower] * 4 + [upper] * 4, eye


INV_BLOCK = 64


def _series_inverse(ls):
    ii = lax.broadcasted_iota(jnp.int32, (INV_BLOCK, INV_BLOCK), 0)
    jj = lax.broadcasted_iota(jnp.int32, (INV_BLOCK, INV_BLOCK), 1)
    eye = jnp.where(ii == jj, 1.0, 0.0).astype(F32)
    doublings = INV_BLOCK.bit_length() - 2
    xs = [eye - l for l in ls]
    ps = [_hdot(l, l) for l in ls]
    for i in range(doublings):
        xs = [x + _hdot(x, p) for x, p in zip(xs, ps)]
        if i < doublings - 1:
            ps = [_hdot(p, p) for p in ps]
    return xs


def _tri_inv_all(ls, upper):
    size = ls[0].shape[0]
    if size == INV_BLOCK:
        return _series_inverse(ls)
    n, h = len(ls), size // 2
    diag = _tri_inv_all([l[:h, :h] for l in ls] + [l[h:, h:] for l in ls], list(upper) * 2)
    out = []
    zero = jnp.zeros((h, h), F32)
    for i, l in enumerate(ls):
        a, d = diag[i], diag[n + i]
        if upper[i]:
            off = -_hdot(_hdot(a, l[:h, h:]), d)
            out.append(jnp.concatenate([jnp.concatenate([a, off], axis=1), jnp.concatenate([zero, d], axis=1)], axis=0))
        else:
            off = -_hdot(_hdot(d, l[h:, :h]), a)
            out.append(jnp.concatenate([jnp.concatenate([a, zero], axis=1), jnp.concatenate([off, d], axis=1)], axis=0))
    return out


@jax.custom_vjp
def _inv_saved(l, x):
    return x


def _inv_saved_fwd(l, x):
    return x, x


def _inv_saved_bwd(x, dx):
    return -_bdot(x, _bdot(dx, x, 1, 1), 0, 0), jnp.zeros_like(x)


_inv_saved.defvjp(_inv_saved_fwd, _inv_saved_bwd)


def _delta_chains(q, k, v, beta_r, gcr, gl, s, masks, eye, tinv_saved):
    n = range(len(q))
    beta = [jnp.sum(eye * beta_r[i], axis=1, keepdims=True) for i in n]
    gcc = [jnp.sum(eye * gcr[i], axis=1, keepdims=True) for i in n]
    decay = [jnp.exp(jnp.where(masks[i][0], gcc[i] - gcr[i], NEG)) for i in n]
    kb = [k[i] * beta[i] for i in n]
    lmat = [jnp.where(masks[i][1], _bdot(kb[i], k[i], 1, 1) * decay[i], 0.0) for i in n]
    if tinv_saved is None:
        tinv = _tri_inv_all(lmat, [i >= 4 for i in n])
    else:
        tinv = [_inv_saved(lmat[i], tinv_saved[i]) for i in n]
    eg = [jnp.exp(gcc[i]) for i in n]
    u = [_bdot(tinv[i], v[i] * beta[i], 1, 0) for i in n]
    w = [_bdot(tinv[i], kb[i] * eg[i], 1, 0) for i in n]
    qk = [_bdot(q[i], k[i], 1, 1) * decay[i] for i in n]
    v_new = [u[i] - _bdot(w[i], s[i], 1, 0) for i in n]
    o = [_bdot(q[i] * eg[i], s[i], 1, 0) + _bdot(qk[i], v_new[i], 1, 0) for i in n]
    s_new = [s[i] * jnp.exp(gl[i]) + _bdot(k[i] * jnp.exp(gl[i] - gcc[i]), v_new[i], 0, 0) for i in n]
    return (o, s_new), tinv


def _chain_loads(tok_pairs, small_pairs):
    toks = [[pair[i // 4][:, (i % 4) * 128:(i % 4 + 1) * 128] for i in range(N_CHAIN)] for pair in tok_pairs]
    smalls = [[pair[i // 4][i % 4] for i in range(N_CHAIN)] for pair in small_pairs]
    return toks, smalls


def _delta_fwd_call(q, k, v, beta, gc, gl, ride=None):
    sp = _scan_specs(lambda n: n, CHUNK)

    def body(qf, qr, kf, kr, vf, vr, bf, br, gcrf, gcrr, glf, glr, of, orv, ssf, ssr, tsf, tsr, s_scr):
        @pl.when(pl.program_id(0) == 0)
        def _():
            s_scr[...] = jnp.zeros_like(s_scr)

        masks, eye = _chain_masks()
        (qs, ks, vs), _ = _chain_loads([(qf, qr), (kf, kr), (vf, vr)], [])
        bs = [(bf, br)[i // 4][i % 4, 0] for i in range(N_CHAIN)]
        gcrs = [(gcrf, gcrr)[i // 4][i % 4, 0] for i in range(N_CHAIN)]
        gls = [(glf, glr)[i // 4][i % 4, 0] for i in range(N_CHAIN)]
        ss = [s_scr[i] for i in range(N_CHAIN)]
        (o, s_new), tinv = _delta_chains(qs, ks, vs, bs, gcrs, gls, ss, masks, eye, None)
        for i in range(N_CHAIN):
            d, h = i // 4, i % 4
            (ssf, ssr)[d][h] = ss[i]
            (tsf, tsr)[d][h] = tinv[i]
            (of, orv)[d][:, h * 128:(h + 1) * 128] = o[i]
            s_scr[i] = s_new[i]

    return _scan_call(
        body, "delta_fwd",
        _both(sp, ["tok", "tok", "tok", "row", "row", "one"]),
        _both(sp, ["tok", "state", "tinv"]),
        [jax.ShapeDtypeStruct((ROWS, 512), F32)] * 2 + [jax.ShapeDtypeStruct((N_CHUNK, 4, 128, 128), F32)] * 2
        + [jax.ShapeDtypeStruct((N_CHUNK, 4, CHUNK, CHUNK), F32)] * 2,
        [q, q, k, k, v, v, *beta, *gc, *gl], ride, CHUNK)


def _delta_bwd_call(q, k, v, beta, gc, gl, ssave, tsave, do, ride=None):
    sp = _scan_specs(lambda n: N_CHUNK - 1 - n, CHUNK)

    def body(qf, qr, kf, kr, vf, vr, bf, br, gcrf, gcrr, glf, glr, ssf, ssr, tsf, tsr, dof, dor,
             dqf, dqr, dkf, dkr, dvf, dvr, dbf, dbr, dgcrf, dgcrr, dglf, dglr, ds_scr):
        @pl.when(pl.program_id(0) == 0)
        def _():
            ds_scr[...] = jnp.zeros_like(ds_scr)

        masks, eye = _chain_masks()
        (qs, ks, vs, dos), (ss, ts) = _chain_loads(
            [(qf, qr), (kf, kr), (vf, vr), (dof, dor)], [(ssf, ssr), (tsf, tsr)])
        bs = [(bf, br)[i // 4][i % 4, 0] for i in range(N_CHAIN)]
        gcrs = [(gcrf, gcrr)[i // 4][i % 4, 0] for i in range(N_CHAIN)]
        gls = [(glf, glr)[i // 4][i % 4, 0] for i in range(N_CHAIN)]
        fn = lambda *a: _delta_chains(*a, masks, eye, ts)
        _, vjp, _ = jax.vjp(fn, qs, ks, vs, bs, gcrs, gls, ss, has_aux=True)
        dq, dk, dv, db, dgcr, dgl, ds = vjp((dos, [ds_scr[i] for i in range(N_CHAIN)]))
        for i in range(N_CHAIN):
            d, h = i // 4, i % 4
            hs = slice(h * 128, (h + 1) * 128)
            (dqf, dqr)[d][:, hs] = dq[i]
            (dkf, dkr)[d][:, hs] = dk[i]
            (dvf, dvr)[d][:, hs] = dv[i]
            (dbf, dbr)[d][h, 0] = db[i]
            (dgcrf, dgcrr)[d][h, 0] = dgcr[i]
            (dglf, dglr)[d][h, 0] = dgl[i]
            ds_scr[i] = ds[i]

    tok = jax.ShapeDtypeStruct((ROWS, 512), F32)
    return _scan_call(
        body, "delta_bwd",
        _both(sp, ["tok", "tok", "tok", "row", "row", "one", "state", "tinv", "tok"]),
        _both(sp, ["tok", "tok", "tok", "row", "row", "one"]),
        [tok] * 6 + [jax.ShapeDtypeStruct((4, N_CHUNK, 1, CHUNK), F32)] * 4
        + [jax.ShapeDtypeStruct((4, N_CHUNK, 1, 1), F32)] * 2,
        [q, q, k, k, v, v, *beta, *gc, *gl, *ssave, *tsave, do, do], ride, CHUNK)


def _ret_chains(q, k, v, dm, qs, ks, cd, s):
    n = range(len(q))
    a = [_bdot(q[i], k[i], 1, 1) * dm[i] for i in n]
    o = [_bdot(a[i], v[i], 1, 0) + _bdot(q[i] * qs[i], s[i], 1, 0) for i in n]
    s_new = [s[i] * cd[i] + _bdot(k[i] * ks[i], v[i], 0, 0) for i in n]
    return o, s_new


RET_CONST_SHAPES = ((N_CHAIN, RET_CHUNK, RET_CHUNK), (N_CHAIN, RET_CHUNK, 1), (N_CHAIN, RET_CHUNK, 1), (N_CHAIN, 1, 1))


def _ret_const_specs():
    return [pl.BlockSpec(shape, lambda n: (0, 0, 0)) for shape in RET_CONST_SHAPES]


def _ret_fwd_call(q, k, v, v_col, dm, qs, ks, cd, ride=None):
    sp = _scan_specs(lambda n: n, RET_CHUNK, v_col)

    def body(qf, qr, kf, kr, vf, vr, dm_ref, qs_ref, ks_ref, cd_ref, of, orv, ssf, ssr, s_scr):
        @pl.when(pl.program_id(0) == 0)
        def _():
            s_scr[...] = jnp.zeros_like(s_scr)

        (qc, kc, vc), _ = _chain_loads([(qf, qr), (kf, kr), (vf, vr)], [])
        ss = [s_scr[i] for i in range(N_CHAIN)]
        consts = [[r[i] for i in range(N_CHAIN)] for r in (dm_ref, qs_ref, ks_ref, cd_ref)]
        o, s_new = _ret_chains(qc, kc, vc, *consts, ss)
        for i in range(N_CHAIN):
            d, h = i // 4, i % 4
            (ssf, ssr)[d][h] = ss[i]
            (of, orv)[d][:, h * 128:(h + 1) * 128] = o[i]
            s_scr[i] = s_new[i]

    return _scan_call(
        body, "ret_fwd",
        _both(sp, ["tok", "tok", "tokv"]) + _ret_const_specs(),
        _both(sp, ["tok", "state"]),
        [jax.ShapeDtypeStruct((ROWS, 512), F32)] * 2
        + [jax.ShapeDtypeStruct((ROWS // RET_CHUNK, 4, 128, 128), F32)] * 2,
        [q, q, k, k, v, v, dm, qs, ks, cd], ride, RET_CHUNK)


def _ret_bwd_call(q, k, v, v_col, dm, qs, ks, cd, ssave, do, ride=None):
    sp = _scan_specs(lambda n: ROWS // RET_CHUNK - 1 - n, RET_CHUNK, v_col)

    def body(qf, qr, kf, kr, vf, vr, dm_ref, qs_ref, ks_ref, cd_ref, ssf, ssr, dof, dor,
             dqf, dqr, dkf, dkr, dvf, dvr, ddm_ref, dqs_ref, dks_ref, dcd_ref, ds_scr):
        @pl.when(pl.program_id(0) == 0)
        def _():
            ds_scr[...] = jnp.zeros_like(ds_scr)
            ddm_ref[...] = jnp.zeros_like(ddm_ref)
            dqs_ref[...] = jnp.zeros_like(dqs_ref)
            dks_ref[...] = jnp.zeros_like(dks_ref)
            dcd_ref[...] = jnp.zeros_like(dcd_ref)

        (qc, kc, vc, dos), (ss,) = _chain_loads([(qf, qr), (kf, kr), (vf, vr), (dof, dor)], [(ssf, ssr)])
        consts = [[r[i] for i in range(N_CHAIN)] for r in (dm_ref, qs_ref, ks_ref, cd_ref)]
        _, vjp = jax.vjp(_ret_chains, qc, kc, vc, *consts, ss)
        dq, dk, dv, ddm, dqs, dks, dcd, ds = vjp((dos, [ds_scr[i] for i in range(N_CHAIN)]))
        for i in range(N_CHAIN):
            d, h = i // 4, i % 4
            hs = slice(h * 128, (h + 1) * 128)
            (dqf, dqr)[d][:, hs] = dq[i]
            (dkf, dkr)[d][:, hs] = dk[i]
            (dvf, dvr)[d][:, hs] = dv[i]
            ddm_ref[i] += ddm[i]
            dqs_ref[i] += dqs[i]
            dks_ref[i] += dks[i]
            dcd_ref[i] += dcd[i]
            ds_scr[i] = ds[i]

    tok = jax.ShapeDtypeStruct((ROWS, 512), F32)
    return _scan_call(
        body, "ret_bwd",
        _both(sp, ["tok", "tok", "tokv"]) + _ret_const_specs() + _both(sp, ["state", "tok"]),
        _both(sp, ["tok", "tok", "tok"]) + _ret_const_specs(),
        [tok] * 6 + [jax.ShapeDtypeStruct(shape, F32) for shape in RET_CONST_SHAPES],
        [q, q, k, k, v, v, dm, qs, ks, cd, *ssave, do, do], ride, RET_CHUNK)


N_QBLK = ROWS // B_BLOCK
CTX_QBLK = CTX_LEN // B_BLOCK


def _attn_heads(q, kc, vc, kw, vw, sink, valid):
    n = range(len(q))
    qs = [q[i] * (B_HD ** -0.5) for i in n]
    s_c = [_bdot(qs[i], kc[i], 1, 1) for i in n]
    s_w = [jnp.where(valid, _bdot(qs[i], kw[i], 1, 1), NEG) for i in n]
    m = [lax.stop_gradient(jnp.maximum(jnp.maximum(jnp.max(s_c[i], axis=-1, keepdims=True), sink[i]),
                                       jnp.max(s_w[i], axis=-1, keepdims=True))) for i in n]
    e_c = [jnp.exp(s_c[i] - m[i]) for i in n]
    e_w = [jnp.exp(s_w[i] - m[i]) for i in n]
    den = [jnp.sum(e_c[i], axis=-1, keepdims=True) + jnp.sum(e_w[i], axis=-1, keepdims=True)
           + jnp.exp(sink[i] - m[i]) for i in n]
    return [(_bdot(e_c[i], vc[i], 1, 0) + _bdot(e_w[i], vw[i], 1, 0)) / den[i] for i in n]


def _attn_loads(q_ref, kv_ref, sink_ref, start):
    q, kc, vc, kw, vw, sink = [], [], [], [], [], []
    for hk in range(B_KV_HEADS):
        ks = slice(hk * B_HD, (hk + 1) * B_HD)
        vs = slice(128 + hk * B_HD, 128 + (hk + 1) * B_HD)
        grp = (kv_ref[0:CTX_LEN, ks], kv_ref[0:CTX_LEN, vs],
               kv_ref[pl.ds(start, 3 * B_BLOCK), ks], kv_ref[pl.ds(start, 3 * B_BLOCK), vs])
        for g in range(4):
            h = hk * 4 + g
            q.append(q_ref[:, h * B_HD:(h + 1) * B_HD])
            for lst, val in zip((kc, vc, kw, vw), grp):
                lst.append(val)
            sink.append(jnp.full((1, 1), sink_ref[h], F32))
    return q, kc, vc, kw, vw, sink


def _window(blk):
    xblk = blk - CTX_QBLK
    first = jnp.clip((xblk - 1) * B_BLOCK, 0, SEQ - 3 * B_BLOCK)
    qpos = xblk * B_BLOCK + lax.broadcasted_iota(jnp.int32, (B_BLOCK, 3 * B_BLOCK), 0)
    kpos = first + lax.broadcasted_iota(jnp.int32, (B_BLOCK, 3 * B_BLOCK), 1)
    far = jnp.where(blk >= CTX_QBLK, 0, 2 * SEQ)
    valid = jnp.abs(kpos - qpos) + far <= WINDOW
    return pl.multiple_of(first + CTX_LEN, B_BLOCK), valid


def _attn_specs():
    qspec = pl.BlockSpec((B_BLOCK, 512), lambda i: (i, 0))
    kvspec = pl.BlockSpec((ROWS, 256), lambda i: (0, 0))
    return qspec, kvspec, pl.BlockSpec(memory_space=pltpu.SMEM)


def _attn_fwd_call(q, kv, sink, ride=None):
    def body(q_ref, kv_ref, sink_ref, o_ref):
        start, valid = _window(pl.program_id(0))
        out = _attn_heads(*_attn_loads(q_ref, kv_ref, sink_ref, start), valid)
        for h in range(B_Q_HEADS):
            o_ref[:, h * B_HD:(h + 1) * B_HD] = out[h]

    qspec, kvspec, sspec = _attn_specs()
    body, r_in, r_out, r_shape, r_scratch = _riding(body, 3, 1, 0, ride, (N_QBLK,))
    return pl.pallas_call(
        body,
        grid=(N_QBLK,),
        in_specs=[qspec, kvspec, sspec] + r_in,
        out_specs=[qspec] + r_out,
        out_shape=[jax.ShapeDtypeStruct((ROWS, 512), F32)] + r_shape,
        scratch_shapes=r_scratch,
        compiler_params=_cparams(("arbitrary",)),
        name="attn_fwd",
    )(q, kv, sink, *(ride[0] if ride else []))


def _attn_bwd_call(q, kv, sink, do, ride=None):
    def body(q_ref, kv_ref, sink_ref, do_ref, dq_ref, dkv_ref, dsink_ref):
        @pl.when(pl.program_id(0) == 0)
        def _():
            dkv_ref[...] = jnp.zeros_like(dkv_ref)
            dsink_ref[...] = jnp.zeros_like(dsink_ref)

        start, valid = _window(pl.program_id(0))
        _, vjp = jax.vjp(functools.partial(_attn_heads, valid=valid), *_attn_loads(q_ref, kv_ref, sink_ref, start))
        dq, dkc, dvc, dkw, dvw, dsink = vjp([do_ref[:, h * B_HD:(h + 1) * B_HD] for h in range(B_Q_HEADS)])
        for h in range(B_Q_HEADS):
            dq_ref[:, h * B_HD:(h + 1) * B_HD] = dq[h]
            dsink_ref[h:h + 1, :] += jnp.broadcast_to(dsink[h], (1, 128))
        for hk in range(B_KV_HEADS):
            ks = slice(hk * B_HD, (hk + 1) * B_HD)
            vs = slice(128 + hk * B_HD, 128 + (hk + 1) * B_HD)
            grp = lambda parts: parts[hk * 4] + parts[hk * 4 + 1] + parts[hk * 4 + 2] + parts[hk * 4 + 3]
            dkv_ref[0:CTX_LEN, ks] += grp(dkc)
            dkv_ref[0:CTX_LEN, vs] += grp(dvc)
            dkv_ref[pl.ds(start, 3 * B_BLOCK), ks] += grp(dkw)
            dkv_ref[pl.ds(start, 3 * B_BLOCK), vs] += grp(dvw)

    qspec, kvspec, sspec = _attn_specs()
    body, r_in, r_out, r_shape, r_scratch = _riding(body, 4, 3, 0, ride, (N_QBLK,))
    return pl.pallas_call(
        body,
        grid=(N_QBLK,),
        in_specs=[qspec, kvspec, sspec, qspec] + r_in,
        out_specs=[qspec, kvspec, pl.BlockSpec((8, 128), lambda i: (0, 0))] + r_out,
        out_shape=[jax.ShapeDtypeStruct((ROWS, 512), F32), jax.ShapeDtypeStruct((ROWS, 256), F32),
                   jax.ShapeDtypeStruct((8, 128), F32)] + r_shape,
        scratch_shapes=r_scratch,
        compiler_params=_cparams(("arbitrary",)),
        name="attn_bwd",
    )(q, kv, sink, do, *(ride[0] if ride else []))


def _my_id():
    return 4 * lax.axis_index("x") + 2 * lax.axis_index("y") + lax.axis_index("c")


def _peer(k):
    x, y, c = lax.axis_index("x"), lax.axis_index("y"), lax.axis_index("c")
    return (1 - x if k & 4 else x, 1 - y if k & 2 else y, 1 - c if k & 1 else c)


SAME_CORE_PEERS = (2, 4, 6)


def _scatter_copies(ins, outs, sems):
    send_sems, recv_sems, local_sems = sems
    me = _my_id()
    own, remote = [], []
    for a in range(len(ins)):
        own.append(pltpu.make_async_copy(ins[a].at[me], outs[a].at[me], local_sems.at[a]))
        for k in range(1, N_DEV):
            peer_slot = jnp.bitwise_xor(me, k)
            common = dict(src_ref=ins[a].at[peer_slot], send_sem=send_sems.at[a, k - 1],
                          recv_sem=recv_sems.at[a, k - 1], device_id=_peer(k), device_id_type=MESH)
            remote.append((pltpu.make_async_remote_copy(dst_ref=outs[a].at[me], **common),
                           pltpu.make_async_remote_copy(dst_ref=outs[a].at[peer_slot], **common)))
    return own, remote


def _gather_copy(outs, sems, a, k, src, slot, to):
    return pltpu.make_async_remote_copy(src_ref=src, dst_ref=outs[a].at[slot], send_sem=sems[0].at[a, k - 1],
                                        recv_sem=sems[1].at[a, k - 1], device_id=_peer(to), device_id_type=MESH)


def _gather_first_copies(ins, outs, sems):
    me = _my_id()
    own = [pltpu.make_async_copy(ins[a], outs[a].at[me], sems[2].at[a]) for a in range(len(ins))]
    direct = [_gather_copy(outs, sems, a, k, ins[a], me, k) for a in range(len(ins)) for k in (1,) + SAME_CORE_PEERS]
    return own, direct


CHIP = "chip"
N_CHIP = N_DEV // 2


def _chip_copies(ins, outs, sems):
    send_sems, recv_sems, local_sems = sems
    chip = 2 * lax.axis_index("x") + lax.axis_index("y")
    own = [pltpu.make_async_copy(ins[a].at[chip], outs[a].at[chip], local_sems.at[a]) for a in range(len(ins))]
    remote = []
    for a in range(len(ins)):
        for k in range(1, N_CHIP):
            other = jnp.bitwise_xor(chip, k)
            common = dict(src_ref=ins[a].at[other], send_sem=send_sems.at[a, k - 1], recv_sem=recv_sems.at[a, k - 1],
                          device_id=_peer(2 * k), device_id_type=MESH)
            remote.append((pltpu.make_async_remote_copy(dst_ref=outs[a].at[chip], **common),
                           pltpu.make_async_remote_copy(dst_ref=outs[a].at[other], **common)))
    return own, remote


def _exchange_start(ins, outs, sems, gather):
    if gather is CHIP:
        own, remote = _chip_copies(ins, outs, sems)
    else:
        own, remote = _gather_first_copies(ins, outs, sems) if gather else _scatter_copies(ins, outs, sems)
    for cp in own:
        cp.start()
    for cp in remote:
        (cp if gather is True else cp[0]).start()


def _exchange_wait(ins, outs, sems, gather):
    if gather is not True:
        own, remote = _chip_copies(ins, outs, sems) if gather is CHIP else _scatter_copies(ins, outs, sems)
        for _, arrival in remote:
            arrival.wait_recv()
        for send, _ in remote:
            send.wait_send()
        for cp in own:
            cp.wait()
        return
    me = _my_id()
    own, direct = _gather_first_copies(ins, outs, sems)
    passed = []
    for a in range(len(ins)):
        for k in SAME_CORE_PEERS:
            origin = jnp.bitwise_xor(me, k)
            _gather_copy(outs, sems, a, k, ins[a], origin, k).wait_recv()
            onward = _gather_copy(outs, sems, a, k + 1, outs[a].at[origin], origin, 1)
            onward.start()
            passed.append(onward)
    for a in range(len(ins)):
        for k in (1, 3, 5, 7):
            _gather_copy(outs, sems, a, k, ins[a], jnp.bitwise_xor(me, k), 1).wait_recv()
    for cp in direct + passed:
        cp.wait_send()
    for cp in own:
        cp.wait()


def _exchange_plumbing(arrays, gather):
    n = len(arrays)
    hbm = [pl.BlockSpec(memory_space=pltpu.HBM)] * n
    if gather is CHIP:
        out_shape = [jax.ShapeDtypeStruct(a.shape, a.dtype) for a in arrays]
        peers = N_CHIP - 1
    else:
        out_shape = [jax.ShapeDtypeStruct((N_DEV,) + (a.shape if gather else a.shape[1:]), a.dtype) for a in arrays]
        peers = N_DEV - 1
    sems = [pltpu.SemaphoreType.DMA((n, peers)), pltpu.SemaphoreType.DMA((n, peers)), pltpu.SemaphoreType.DMA((n,))]
    return hbm, out_shape, sems


def _exchange(arrays, gather, name):
    n = len(arrays)

    def body(*refs):
        ins, outs, sems = refs[:n], refs[n:2 * n], refs[2 * n:]
        _exchange_start(ins, outs, sems, gather)
        _exchange_wait(ins, outs, sems, gather)

    hbm, out_shape, sems = _exchange_plumbing(arrays, gather)
    return pl.pallas_call(
        body,
        in_specs=hbm,
        out_specs=hbm,
        out_shape=out_shape,
        scratch_shapes=sems,
        compiler_params=pltpu.CompilerParams(has_side_effects=True),
        name=name,
    )(*arrays)


def _pair_swap(blocks, name):
    n = len(blocks)

    def body(*refs):
        ins, outs, (send_sems, recv_sems) = refs[:n], refs[n:2 * n], refs[2 * n:]
        core = lax.axis_index("c")
        copies = [pltpu.make_async_remote_copy(src_ref=ins[a].at[2 * chip + (1 - core)], dst_ref=outs[a].at[chip],
                                               send_sem=send_sems.at[a, chip], recv_sem=recv_sems.at[a, chip],
                                               device_id=_peer(1), device_id_type=MESH)
                  for a in range(n) for chip in range(N_CHIP)]
        for cp in copies:
            cp.start()
        for cp in copies:
            cp.wait_recv()
        for cp in copies:
            cp.wait_send()

    hbm = [pl.BlockSpec(memory_space=pltpu.HBM)] * n
    return pl.pallas_call(
        body,
        in_specs=hbm,
        out_specs=hbm,
        out_shape=[jax.ShapeDtypeStruct((N_CHIP,) + b.shape[1:], b.dtype) for b in blocks],
        scratch_shapes=[pltpu.SemaphoreType.DMA((n, N_CHIP)), pltpu.SemaphoreType.DMA((n, N_CHIP))],
        compiler_params=pltpu.CompilerParams(has_side_effects=True),
        name=name,
    )(*blocks)


def _chip_scatter(pairs, name):
    return _exchange(pairs, CHIP, name)


def _pair_sums(blocks, name):
    swapped = _pair_swap(blocks, name)
    core = lax.axis_index("c")
    pairs = []
    for b, s in zip(blocks, swapped):
        mine = lax.dynamic_index_in_dim(b.reshape((N_CHIP, 2) + b.shape[1:]), core, axis=1, keepdims=False)
        pairs.append((mine.astype(F32) + s.astype(F32)).astype(b.dtype))
    return pairs


def _scatter_two_level(blocks, name):
    return _chip_scatter(_pair_sums(blocks, name + "_pair"), name + "_chip")


def _riding(body, n_in, n_out, n_scratch, ride, grid):
    if ride is None:
        return body, [], [], [], []
    arrays, gather = ride
    n = len(arrays)

    def at(step_of):
        hit = pl.program_id(0) == step_of(grid[0])
        for d in range(1, len(grid)):
            hit = jnp.logical_and(hit, pl.program_id(d) == step_of(grid[d]))
        return hit

    def wrapped(*refs):
        ins, rin = refs[:n_in], refs[n_in:n_in + n]
        outs = refs[n_in + n:n_in + n + n_out]
        rout = refs[n_in + n + n_out:n_in + 2 * n + n_out]
        scratch = refs[n_in + 2 * n + n_out:n_in + 2 * n + n_out + n_scratch]
        sems = refs[n_in + 2 * n + n_out + n_scratch:]

        @pl.when(at(lambda size: 0))
        def _():
            _exchange_start(rin, rout, sems, gather)

        body(*ins, *outs, *scratch)

        @pl.when(at(lambda size: size - 1))
        def _():
            _exchange_wait(rin, rout, sems, gather)

    hbm, out_shape, sems = _exchange_plumbing(arrays, gather)
    return wrapped, hbm, hbm, out_shape, sems


def _sum_contributions(c_ref):
    g = c_ref[0].astype(F32)
    for j in range(1, c_ref.shape[0]):
        g = g + c_ref[j].astype(F32)
    return g


def _adamw_update(g, w_ref, m_ref, v_ref, g_ref, d_ref, nm_ref, nv_ref):
    m_new = ADAM_B1 * m_ref[...] + (1.0 - ADAM_B1) * g
    v_new = ADAM_B2 * v_ref[...] + (1.0 - ADAM_B2) * (g * g)
    m_hat = m_new / (1.0 - ADAM_B1 ** ADAM_STEP)
    v_hat = v_new / (1.0 - ADAM_B2 ** ADAM_STEP)
    g_ref[...] = g
    d_ref[...] = -ADAM_LR * (m_hat / (jnp.sqrt(v_hat) + ADAM_EPS) + ADAM_WD * w_ref[...])
    nm_ref[...] = m_new
    nv_ref[...] = v_new


def _adamw_layers(w, m, v, contrib0, contrib1, name):
    _, r, c = w.shape
    br = _pick(r, (256, 128, 64, 32, 16, 8))
    nb = r // br

    def body(w_ref, m_ref, v_ref, c0_ref, c1_ref, g_ref, d_ref, nm_ref, nv_ref):
        g = jnp.where(pl.program_id(0) == 0, _sum_contributions(c0_ref), _sum_contributions(c1_ref))
        _adamw_update(g, w_ref, m_ref, v_ref, g_ref, d_ref, nm_ref, nv_ref)

    spec = pl.BlockSpec((None, br, c), lambda l, i: (l, i, 0))
    return pl.pallas_call(
        body,
        grid=(DEPTH, nb),
        in_specs=[spec, spec, spec,
                  pl.BlockSpec((contrib0.shape[0], br, c), lambda l, i: (0, jnp.where(l == 0, i, nb - 1), 0)),
                  pl.BlockSpec((contrib1.shape[0], br, c), lambda l, i: (0, jnp.where(l == 1, i, 0), 0))],
        out_specs=[spec] * 4,
        out_shape=[jax.ShapeDtypeStruct(w.shape, F32)] * 4,
        compiler_params=_cparams(("arbitrary", "arbitrary")),
        name=name,
    )(w, m, v, contrib0, contrib1)


def _adamw(w, m, v, contrib, name):
    r, c = w.shape
    br = _pick(r, (256, 128, 64, 32, 16, 8))

    def body(w_ref, m_ref, v_ref, c_ref, g_ref, d_ref, nm_ref, nv_ref):
        _adamw_update(_sum_contributions(c_ref), w_ref, m_ref, v_ref, g_ref, d_ref, nm_ref, nv_ref)

    spec = pl.BlockSpec((br, c), lambda i: (i, 0))
    cspec = pl.BlockSpec((contrib.shape[0], br, c), lambda i: (0, i, 0))
    return pl.pallas_call(
        body,
        grid=(r // br,),
        in_specs=[spec, spec, spec, cspec],
        out_specs=[spec] * 4,
        out_shape=[jax.ShapeDtypeStruct((r, c), F32)] * 4,
        compiler_params=_cparams(("parallel",)),
        name=name,
    )(w, m, v, contrib)


def _silu(x):
    return x * jax.nn.sigmoid(x)


def _rope_angles(pos, n_freq):
    inv = ROPE_BASE ** (-jnp.arange(n_freq, dtype=F32) / n_freq)
    return pos[:, None] * inv[None, :]


def _with_ctx_rows(cos, sin):
    return (jnp.concatenate([jnp.ones((CTX_LEN, 128), F32), cos], axis=0),
            jnp.concatenate([jnp.zeros((CTX_LEN, 128), F32), sin], axis=0))


def _rope_tables():
    rows_n = SEQ // GRID_W
    rows = jnp.repeat(jnp.arange(rows_n, dtype=F32), GRID_W)
    cols = jnp.tile(jnp.arange(GRID_W, dtype=F32), rows_n)
    ang_r = _rope_angles(rows, B_HD // 4)
    ang_c = _rope_angles(cols, B_HD // 4)
    cos_b = jnp.tile(jnp.concatenate([jnp.cos(ang_r)] * 2 + [jnp.cos(ang_c)] * 2, axis=1), (1, 2))
    sin_b = jnp.tile(jnp.concatenate([-jnp.sin(ang_r), jnp.sin(ang_r), -jnp.sin(ang_c), jnp.sin(ang_c)], axis=1), (1, 2))
    ang = _rope_angles(jnp.arange(SEQ, dtype=F32), C_HD // 2)
    cos_c = jnp.concatenate([jnp.cos(ang)] * 2, axis=1)
    sin_c = jnp.concatenate([-jnp.sin(ang), jnp.sin(ang)], axis=1)
    return _with_ctx_rows(cos_b, sin_b), _with_ctx_rows(cos_c, sin_c)


def _halves(a):
    return a[:4], a[4:]


def _delta_gates(ab, a_log, dt_bias):
    beta = jax.nn.sigmoid(ab[:, :8])
    g = -jnp.exp(a_log)[None, :] * jax.nn.softplus(ab[:, 8:] + dt_bias[None, :])
    gch = g.reshape(N_CHUNK, CHUNK, 8)
    tri = jnp.tril(jnp.ones((CHUNK, CHUNK), F32))
    fwd = jnp.einsum("ij,cjh->cih", tri, gch[..., :4], precision=HIGHEST)
    bwd = jnp.einsum("ji,cjh->cih", tri, gch[..., 4:], precision=HIGHEST)
    gc = jnp.concatenate([fwd, bwd], axis=-1)
    gl = jnp.sum(gch, axis=1)
    rows = lambda a: _halves(a.transpose(2, 0, 1)[:, :, None, :])
    return rows(beta.reshape(N_CHUNK, CHUNK, 8)), rows(gc), _halves(gl.T[:, :, None, None])


def _ret_consts(c_decay):
    lg = jax.nn.log_sigmoid(c_decay)
    idx = jnp.arange(RET_CHUNK, dtype=F32)
    diff = idx[:, None] - idx[None, :]
    lgf, lgb = lg[:4, None, None], lg[4:, None, None]
    dm = jnp.concatenate([jnp.exp(jnp.where(diff >= 0, diff * lgf, -jnp.inf)),
                          jnp.exp(jnp.where(diff <= 0, -diff * lgb, -jnp.inf))], axis=0)
    qs = jnp.concatenate([jnp.exp((idx + 1.0)[None, :] * lg[:4, None]),
                          jnp.exp((RET_CHUNK - idx)[None, :] * lg[4:, None])], axis=0)[:, :, None]
    ks = jnp.concatenate([jnp.exp((RET_CHUNK - 1.0 - idx)[None, :] * lg[:4, None]),
                          jnp.exp(idx[None, :] * lg[4:, None])], axis=0)[:, :, None]
    return dm, qs, ks, jnp.exp(RET_CHUNK * lg)[:, None, None]


A_PIECES = ((0, True, A_DK ** -0.5, "a_q"), (1, True, 1.0, "a_k"), (2, False, 1.0, "a_v"))
B_ROPE_COLS = [C_BQ // 512, C_BKV // 256, 0, 0]
C_ROPE_COLS = [C_CQ // 512, C_CK // 512, 0, 0]


def _conv8(conv_w):
    return jnp.pad(conv_w, ((0, 8 - A_CONV), (0, 0)))


W_IN_TILES = {"nn": (2176, 512, 1024), "nt": (1088, 1024, 4352), "db": (1024, 512, ROWS)}


def _core_forward(h, w16, p, rides):
    res = _matmul(h, w16, "w_in", "nn", W_IN_TILES["nn"], ride=rides.get("w_in"))
    proj, rode = (res[0], {"w_in": res[1:]}) if "w_in" in rides else (res, {})
    wb = p["w_branch"] if "w_in" not in rides else _unshard_layer("w_branch", rode["w_in"][0])
    (cos_b, sin_b), (cos_c, sin_c) = _rope_tables()
    conv8 = _conv8(p["a_conv_w"])
    q, k, v = [_a_prep_fwd(proj, conv8, col, nrm, scl, nm) for col, nrm, scl, nm in A_PIECES]
    gates = _delta_gates(proj[:, C_AB:C_AB + 16], p["a_log"], p["a_dt_bias"])
    res = _delta_fwd_call(q, k, v, *gates, ride=rides.get("delta"))
    (of, orv, ssf, ssr, tsf, tsr), rode["delta"] = res[:6], res[6:]

    qb, kvb = _b_rope.fwd([proj, proj, cos_b, sin_b], [], B_ROPE_COLS)
    res = _attn_fwd_call(qb, kvb, p["b_sink"], ride=rides.get("attn"))
    ob, rode["attn"] = res[0], res[1:]

    qc, kc = _c_rope.fwd([proj, proj, cos_c, sin_c], [], C_ROPE_COLS)
    res = _ret_fwd_call(qc, kc, proj, C_CV // 512, *_ret_consts(p["c_decay"]), ride=rides.get("ret"))
    (cf, cr, csf, csr), rode["ret"] = res[:4], res[4:]

    (merged,) = _mix_merge.fwd([(of, orv), proj, ob, proj, (cf, cr), proj, proj, proj, proj],
                               [p["a_norm_w"][None, :], p["c_norm_w"][None, :], wb[0], wb[1], wb[2]], MIX_MERGE_COLS)
    saved = dict(proj=proj, q=q, k=k, v=v, of=of, orv=orv, ss=(ssf, ssr), ts=(tsf, tsr), qb=qb, kvb=kvb, ob=ob,
                 qc=qc, kc=kc, cf=cf, cr=cr, cs=(csf, csr), wb=wb)
    return merged, saved, rode


def _core_backward(h, w16, p, s, dmerged, rides, branch_rides_in_attn=False):
    proj, wb = s["proj"], s["wb"]
    (cos_b, sin_b), (cos_c, sin_c) = _rope_tables()
    conv8 = _conv8(p["a_conv_w"])
    rode = {}

    (do_a, daz, dob, dbz, do_c, dcz, dma, dmb, dmc), (danw, dcnw, *dwb) = _mix_merge.bwd(
        [(s["of"], s["orv"]), proj, s["ob"], proj, (s["cf"], s["cr"]), proj, proj, proj, proj],
        [p["a_norm_w"][None, :], p["c_norm_w"][None, :], wb[0], wb[1], wb[2]], [dmerged], MIX_MERGE_COLS,
        bf16_rows=(1, 3, 5, 6, 7, 8))
    dwb = jnp.stack(dwb)

    consts, consts_vjp = jax.vjp(_ret_consts, p["c_decay"])
    g = _ret_bwd_call(s["qc"], s["kc"], proj, C_CV // 512, *consts, s["cs"], do_c, ride=rides.get("ret"))
    rode["ret"] = g[10:]
    (dcq, dck), _ = _c_rope.bwd([proj, proj, cos_c, sin_c], [], [(g[0], g[1]), (g[2], g[3])], C_ROPE_COLS,
                                bf16_rows=(0, 1))
    dcv = (g[4] + g[5]).astype(BF16)
    (dc_decay,) = consts_vjp(tuple(g[6:10]))

    attn_ride = rides.get("attn")
    if branch_rides_in_attn:
        attn_ride = (list(attn_ride[0]) + [_reshard_layer("w_branch", dwb).astype(BF16)], attn_ride[1])
    res = _attn_bwd_call(s["qb"], s["kvb"], p["b_sink"], dob, ride=attn_ride)
    (dqb, dkvb, dsink), rode["attn"] = res[:3], res[3:]
    (dbq, dbkv), _ = _b_rope.bwd([proj, proj, cos_b, sin_b], [], [dqb, dkvb], B_ROPE_COLS, bf16_rows=(0, 1))

    ab = proj[:, C_AB:C_AB + 16]
    gates, gates_vjp = jax.vjp(_delta_gates, ab, p["a_log"], p["a_dt_bias"])
    g = _delta_bwd_call(s["q"], s["k"], s["v"], *gates, s["ss"], s["ts"], do_a, ride=rides.get("delta"))
    rode["delta"] = g[12:]
    dgates = ((g[6], g[7]), (g[8], g[9]), (g[10], g[11]))
    dab, da_log, ddt = gates_vjp(dgates)
    dpre, dconv = [], []
    for (col, nrm, scl, nm), df, dr in zip(A_PIECES, (g[0], g[2], g[4]), (g[1], g[3], g[5])):
        dx, dw = _a_prep_bwd(proj, conv8, col, nrm, scl, df, dr, nm)
        dpre.append(dx)
        dconv.append(dw[:A_CONV])

    dproj = jnp.concatenate(dpre + [daz, dbq, dbz, dcq, dck, dcv, dcz, dma, dmb, dmc, dbkv,
                                    jnp.pad(dab, ((0, 0), (0, IN_PAD - C_AB - 16))).astype(BF16)], axis=1)
    dw = _matmul(h.T.astype(BF16), dproj, "w_in_db", "nn", W_IN_TILES["db"])
    if branch_rides_in_attn:
        pair = _pair_sums([_grad_blocks("w_in", dw)], "scatter_w_in_pair")
        res = _matmul(dproj, w16, "w_in_da", "nt", W_IN_TILES["nt"], ride=(pair, CHIP))
        dh, rode["w_in_da"] = res[0], res[1:]
    else:
        dh = _matmul(dproj, w16, "w_in_da", "nt", W_IN_TILES["nt"])
    dp = dict(a_conv_w=jnp.concatenate(dconv, axis=1), a_log=da_log, a_dt_bias=ddt, a_norm_w=danw[0],
              b_sink=dsink[:, 0], c_decay=dc_decay, c_norm_w=dcnw[0], w_branch=dwb)
    return dh, dw, dp, rode


CORE_PARAMS = ("a_conv_w", "a_log", "a_dt_bias", "a_norm_w", "b_sink", "c_decay", "c_norm_w", "w_branch")


W_IN_SHARD = IN_WIDTH // N_DEV
W_IN_RUNS = ((0, 2048, 0), (2064, 512, C_BQ), (2832, 512, C_BZ), (3344, 5120, C_CQ), (2576, 256, C_BKV),
             (2048, 16, C_AB))


def _shard_overlap(start, width, j):
    lo, hi = max(start, j * W_IN_SHARD), min(start + width, (j + 1) * W_IN_SHARD)
    return (lo, hi) if lo < hi else None


def _w_in_from_shards(g):
    parts = []
    for start, width, _ in W_IN_RUNS:
        for j in range(N_DEV):
            span = _shard_overlap(start, width, j)
            if span:
                parts.append(g[j, :, span[0] - j * W_IN_SHARD:span[1] - j * W_IN_SHARD])
    parts.append(jnp.zeros((D_MODEL, IN_PAD - IN_WIDTH), g.dtype))
    return jnp.concatenate(parts, axis=1)


def _w_in_blocks(dw):
    blocks = []
    for j in range(N_DEV):
        parts = []
        for start, width, pad in sorted(W_IN_RUNS):
            span = _shard_overlap(start, width, j)
            if span:
                parts.append(dw[:, pad + span[0] - start:pad + span[1] - start])
        blocks.append(jnp.concatenate(parts, axis=1))
    return jnp.stack(blocks)


LAYER_SHARDED = ("w_ada", "w_in", "w_branch", "w_out")


def _unshard_layer(name, g):
    if name == "w_branch":
        return g.transpose(1, 2, 0, 3).reshape(3, BR_WIDTH, D_MODEL)
    if name == "w_out":
        return g.reshape(D_MODEL, D_MODEL)
    return g.transpose(1, 0, 2).reshape(D_MODEL, -1)


def _reshard_layer(name, w):
    if name == "w_branch":
        return w.reshape(3, BR_WIDTH, N_DEV, D_MODEL // N_DEV).transpose(2, 0, 1, 3)
    if name == "w_out":
        return w.reshape(N_DEV, D_MODEL // N_DEV, D_MODEL)
    return w.reshape(D_MODEL, N_DEV, -1).transpose(1, 0, 2)


def _layer_weights(gathered):
    out = {n: _unshard_layer(n, g) for n, g in gathered.items() if n != "w_in"}
    out["w_in16"] = _w_in_from_shards(gathered["w_in"])
    return out


def _grad_blocks(name, g):
    return (_w_in_blocks(g) if name == "w_in" else _reshard_layer(name, g)).astype(BF16)


def _forward_backward(small, layer0, shards0, shards1, x, c, ctx, loss_target):
    c_ctx = small["c_ctx"]
    sc16 = jnp.zeros((16, D_MODEL), F32).at[0].set(_silu(c)).at[1].set(_silu(c_ctx))
    xs = jnp.concatenate([ctx, x], axis=0)
    weights = [dict(layer0), None]
    layers = []
    for l in range(DEPTH):
        wl = weights[l]
        mod16 = _matmul(sc16, wl["w_ada"], "ada") + small["b_ada"][l][None, :]
        mod_cx = jnp.stack([mod16[1], mod16[0]])
        shift, scale, gate = jnp.split(mod_cx, 3, axis=1)
        nw = small["norm_w"][l][None, :]
        (h,) = _norm_mod.fwd([xs], [nw, shift, scale])
        p = {n: small[n][l] for n in CORE_PARAMS if n != "w_branch"}
        p["w_branch"] = wl.get("w_branch")
        rides = {}
        if l == 0:
            rides = {"w_in": ([shards0["w_branch"], shards0["w_out"]], True), "delta": ([shards1["w_in"]], True),
                     "attn": ([shards1["w_ada"]], True), "ret": ([shards1["w_branch"], shards1["w_out"]], True)}
        merged, saved, rode = _core_forward(h, wl["w_in16"], p, rides)
        if l == 0:
            wl["w_out"] = _unshard_layer("w_out", rode["w_in"][1])
            weights[1] = _layer_weights(dict(w_in=rode["delta"][0], w_ada=rode["attn"][0],
                                             w_branch=rode["ret"][0], w_out=rode["ret"][1]))
        (xs_next,) = _out_residual.fwd([xs, merged], [wl["w_out"], gate])
        layers.append(dict(xs=xs, h=h, p=p, saved=saved, merged=merged, gate=gate, nw=nw, shift=shift, scale=scale))
        xs = xs_next
    fw = small["final_norm_w"][None, :]
    xs = xs[CTX_LEN:]
    (per_row,) = _loss_rows.fwd([xs, loss_target], [fw])
    loss = jnp.sum(per_row[:, 0])

    d_per_row = jnp.zeros((SEQ, 128), F32).at[:, 0].set(1.0)
    (dxs,), (dfw,) = _loss_rows.bwd([xs, loss_target], [fw], [d_per_row])
    dxs = jnp.pad(dxs, ((CTX_LEN, 0), (0, 0)))
    small_names = tuple(n for n in CORE_PARAMS if n != "w_branch") + ("b_ada", "norm_w")
    dsmall = {n: [None] * DEPTH for n in small_names}
    dlayer = [None] * DEPTH
    contrib0 = contrib1 = None
    dsc16 = jnp.zeros((16, D_MODEL), F32)
    for l in reversed(range(DEPTH)):
        s, wl = layers[l], weights[l]
        (dres, dmerged), (dw_out, dgate) = _out_residual.bwd([s["xs"], s["merged"]], [wl["w_out"], s["gate"]], [dxs])
        rides = {}
        if l == 0:
            blocks1 = {n: _grad_blocks(n, g) for n, g in dlayer[1].items()}
            rides = {"ret": ([blocks1["w_branch"], blocks1["w_out"]], False),
                     "attn": ([_reshard_layer("w_out", dw_out).astype(BF16), blocks1["w_ada"]], False),
                     "delta": ([blocks1["w_in"]], False)}
        dh, dw_in, dp, rode = _core_backward(s["h"], wl["w_in16"], s["p"], s["saved"], dmerged, rides,
                                             branch_rides_in_attn=(l == 0))
        if l == 0:
            contrib1 = dict(w_in=rode["delta"][0], w_ada=rode["attn"][1], w_branch=rode["ret"][0],
                            w_out=rode["ret"][1])
            contrib0 = dict(w_out=rode["attn"][0], w_branch=rode["attn"][2], w_in=rode["w_in_da"][0])
        (dxn,), (dnw, dshift, dscale) = _norm_mod.bwd([s["xs"]], [s["nw"], s["shift"], s["scale"]], [dh])
        dxs = dres + dxn
        dmod_cx = jnp.concatenate([dshift, dscale, dgate], axis=1)
        dmod16 = jnp.zeros((16, 3 * D_MODEL), F32).at[0].set(dmod_cx[1]).at[1].set(dmod_cx[0])
        dsc16 = dsc16 + _matmul(dmod16, wl["w_ada"], "ada_da", "nt")
        dlayer[l] = dict(w_ada=_matmul(sc16, dmod16, "ada_db", "tn"), w_in=dw_in,
                         w_branch=dp["w_branch"], w_out=dw_out)
        for n in small_names:
            if n in dp:
                dsmall[n][l] = dp[n]
        dsmall["norm_w"][l] = dnw[0]
        dsmall["b_ada"][l] = dmod_cx[0] + dmod_cx[1]
    gsmall = {n: jnp.stack(v) for n, v in dsmall.items()}
    gsmall["final_norm_w"] = dfw[0]
    sig = jax.nn.sigmoid(c_ctx)
    gsmall["c_ctx"] = dsc16[1] * sig * (1.0 + c_ctx * (1.0 - sig))
    return loss, dxs[CTX_LEN:], gsmall, {"w_ada": dlayer[0]["w_ada"]}, contrib0, contrib1


SMALL = ("c_ctx", "b_ada", "norm_w", "a_log", "a_dt_bias", "a_norm_w", "b_sink", "c_decay", "c_norm_w",
         "final_norm_w")
WEIGHTS = ("c_ctx", "w_ada", "b_ada", "norm_w", "w_in", "a_conv_w", "a_log", "a_dt_bias", "a_norm_w", "b_sink",
           "c_decay", "c_norm_w", "w_branch", "w_out", "final_norm_w")
SMALL_PACK = 12288


def _unshard_conv(g):
    return g.transpose(1, 2, 0, 3).reshape(DEPTH, A_CONV, 3 * A_WIDTH)


def _reshard_conv(w):
    return w.reshape(DEPTH, A_CONV, N_DEV, 3 * A_WIDTH // N_DEV).transpose(2, 0, 1, 3)


def _pack_small(tree):
    flat = jnp.concatenate([tree[n].reshape(-1) for n in SMALL])
    return jnp.pad(flat, (0, SMALL_PACK - flat.shape[0])).reshape(SMALL_PACK // 128, 128)


def _unpack_small(packed, like):
    flat = packed.reshape(-1)
    out, off = {}, 0
    for n in SMALL:
        size = math.prod(like[n].shape)
        out[n] = flat[off:off + size].reshape(like[n].shape)
        off += size
    return out


def kernel(x, c, ctx, c_ctx, w_ada, b_ada, norm_w, w_in, a_conv_w, a_log, a_dt_bias, a_norm_w, b_sink, c_decay, c_norm_w, w_branch, w_out, final_norm_w, loss_target, m_c_ctx, m_w_ada, m_b_ada, m_norm_w, m_w_in, m_a_conv_w, m_a_log, m_a_dt_bias, m_a_norm_w, m_b_sink, m_c_decay, m_c_norm_w, m_w_branch, m_w_out, m_final_norm_w, v_c_ctx, v_w_ada, v_b_ada, v_norm_w, v_w_in, v_a_conv_w, v_a_log, v_a_dt_bias, v_a_norm_w, v_b_sink, v_c_decay, v_c_norm_w, v_w_branch, v_w_out, v_final_norm_w):
    w = dict(c_ctx=c_ctx, w_ada=w_ada, b_ada=b_ada, norm_w=norm_w, w_in=w_in, a_conv_w=a_conv_w, a_log=a_log,
             a_dt_bias=a_dt_bias, a_norm_w=a_norm_w, b_sink=b_sink, c_decay=c_decay, c_norm_w=c_norm_w,
             w_branch=w_branch, w_out=w_out, final_norm_w=final_norm_w)
    m = dict(c_ctx=m_c_ctx, w_ada=m_w_ada, b_ada=m_b_ada, norm_w=m_norm_w, w_in=m_w_in, a_conv_w=m_a_conv_w,
             a_log=m_a_log, a_dt_bias=m_a_dt_bias, a_norm_w=m_a_norm_w, b_sink=m_b_sink, c_decay=m_c_decay,
             c_norm_w=m_c_norm_w, w_branch=m_w_branch, w_out=m_w_out, final_norm_w=m_final_norm_w)
    v = dict(c_ctx=v_c_ctx, w_ada=v_w_ada, b_ada=v_b_ada, norm_w=v_norm_w, w_in=v_w_in, a_conv_w=v_a_conv_w,
             a_log=v_a_log, a_dt_bias=v_a_dt_bias, a_norm_w=v_a_norm_w, b_sink=v_b_sink, c_decay=v_c_decay,
             c_norm_w=v_c_norm_w, w_branch=v_w_branch, w_out=v_w_out, final_norm_w=v_final_norm_w)

    shards = {n: w[n].astype(BF16) for n in LAYER_SHARDED}
    first = _exchange([shards["w_ada"][0], shards["w_in"][0], w["a_conv_w"]], True, "gather_layer0")
    layer0 = _layer_weights(dict(w_ada=first[0], w_in=first[1]))
    small_w = {n: w[n] for n in SMALL}
    small_w["a_conv_w"] = _unshard_conv(first[2])
    loss, gx, gw, glayer0, contrib0, contrib1 = _forward_backward(
        small_w, layer0, {n: shards[n][0] for n in ("w_branch", "w_out")}, {n: shards[n][1] for n in LAYER_SHARDED},
        x[0], c[0], ctx[0], loss_target[0])
    loss = lax.psum(loss, ("x", "y", "c"))

    last = _scatter_two_level([_reshard_layer("w_ada", glayer0["w_ada"]).astype(BF16), _reshard_conv(gw["a_conv_w"])],
                              "scatter_layer0")
    contrib0["w_ada"] = last[0]
    small = _exchange([_pack_small(gw)], True, "gather_small_grads")[0]

    grad, delta, new_m, new_v = {}, {}, {}, {}
    for n in LAYER_SHARDED:
        shp = w[n].shape
        per_layer = (math.prod(shp[1:-1]), shp[-1])
        outs = _adamw_layers(*[a.reshape((DEPTH,) + per_layer) for a in (w[n], m[n], v[n])],
                             *[cb.reshape(cb.shape[:1] + per_layer) for cb in (contrib0[n], contrib1[n])], "adamw_" + n)
        grad[n], delta[n], new_m[n], new_v[n] = [o.reshape(shp) for o in outs]
    shp = a_conv_w.shape
    two_d = (math.prod(shp[:-1]), shp[-1])
    outs = _adamw(*[a.reshape(two_d) for a in (a_conv_w, m_a_conv_w, v_a_conv_w)],
                  last[1].reshape(last[1].shape[:1] + two_d), "adamw_a_conv_w")
    grad["a_conv_w"], delta["a_conv_w"], new_m["a_conv_w"], new_v["a_conv_w"] = [o.reshape(shp) for o in outs]
    outs = _adamw(_pack_small(w), _pack_small(m), _pack_small(v), small, "adamw_small")
    for tree, packed in zip((grad, delta, new_m, new_v), outs):
        tree.update(_unpack_small(packed, w))

    return (loss, gx[None], *[grad[n] for n in WEIGHTS], *[delta[n] for n in WEIGHTS],
            *[new_m[n] for n in WEIGHTS], *[new_v[n] for n in WEIGHTS])
```

```python
import functools
import math

import jax
import jax.numpy as jnp
from jax import lax
from jax.experimental import pallas as pl
from jax.experimental.pallas import tpu as pltpu

F32 = jnp.float32
BF16 = jnp.bfloat16
HIGHEST = lax.Precision.HIGHEST

D_MODEL = 1024
SEQ = 4096
DEPTH = 2
GRID_W = 64
CTX_LEN = 256
EPS = 1e-6
ROPE_BASE = 10000.0
BR_WIDTH = D_MODEL // 2
A_DK = 128
A_HEADS = 4
A_WIDTH = 512
A_CONV = 5
B_HD = 64
B_Q_HEADS = 8
B_KV_HEADS = 2
WINDOW = 128
B_BLOCK = 128
C_HD = 128
C_HEADS = 4
C_WIDTH = 512
CHUNK = 128
RET_CHUNK = 256
ADAM_LR = 0.001
ADAM_B1 = 0.9
ADAM_B2 = 0.999
ADAM_EPS = 1e-08
ADAM_WD = 0.01
ADAM_STEP = 10

N_DEV = 8
ROWS = CTX_LEN + SEQ
N_CHUNK = ROWS // CHUNK
IN_WIDTH = 8464
IN_PAD = 8704
NEG = -1e30

VMEM_LIMIT = 56 * 1024 * 1024
MESH = pl.DeviceIdType.MESH

C_AQ, C_AK, C_AV, C_AZ, C_BQ, C_BZ, C_CQ, C_CK, C_CV, C_CZ = (i * 512 for i in range(10))
C_MERGE = 5120
C_BKV = 8192
C_AB = 8448


def _cparams(sem=None):
    if sem is None:
        return pltpu.CompilerParams(vmem_limit_bytes=VMEM_LIMIT)
    return pltpu.CompilerParams(dimension_semantics=sem, vmem_limit_bytes=VMEM_LIMIT)


def _dg(a, b, ca, cb, prec=None):
    return lax.dot_general(a, b, (((ca,), (cb,)), ((), ())), preferred_element_type=F32, precision=prec)


@functools.partial(jax.custom_vjp, nondiff_argnums=(2, 3))
def _bdot(a, b, ca, cb):
    return _dg(a.astype(BF16), b.astype(BF16), ca, cb)


def _bdot_fwd(a, b, ca, cb):
    return _bdot(a, b, ca, cb), (a, b)


def _bdot_bwd(ca, cb, res, ct):
    a, b = res
    da = _bdot(ct, b, 1, 1 - cb) if ca == 1 else _bdot(b, ct, 1 - cb, 1)
    db = _bdot(a, ct, 1 - ca, 0) if cb == 0 else _bdot(ct, a, 0, 1 - ca)
    return da, db


_bdot.defvjp(_bdot_fwd, _bdot_bwd)


def _hdot(a, b):
    return _dg(a, b, 1, 0, lax.Precision.HIGH)


def _k_silu(x):
    return x / (1.0 + jnp.exp(-x))


def _k_sigmoid(x):
    return 1.0 / (1.0 + jnp.exp(-x))


@jax.custom_vjp
def _swap64(x):
    return pltpu.roll(x, 64, 1)


_swap64.defvjp(lambda x: (pltpu.roll(x, 64, 1), None), lambda _, ct: (pltpu.roll(ct, 64, 1),))


def _swap16_impl(x):
    lane = lax.broadcasted_iota(jnp.int32, x.shape, 1)
    return jnp.where((lane & 16) == 0, pltpu.roll(x, 112, 1), pltpu.roll(x, 16, 1))


@jax.custom_vjp
def _swap16(x):
    return _swap16_impl(x)


_swap16.defvjp(lambda x: (_swap16_impl(x), None), lambda _, ct: (_swap16_impl(ct),))


def _pick(dim, prefs):
    for p in prefs:
        if dim % p == 0:
            return p
    return dim


def _matmul(a, b, name, mode="nn", tiles=None, ride=None):
    ca, cb = {"nn": (1, 0), "nt": (1, 1), "tn": (0, 0)}[mode]
    m, k = a.shape[1 - ca], a.shape[ca]
    n = b.shape[1 - cb]
    if tiles is None:
        tiles = (_pick(m, (1088, 1024, 512, 256, 128)), _pick(n, (512, 256, 128)),
                 _pick(k, (1088, 1024, 512, 256, 128) if mode == "tn" else (2176, 2048, 1024, 512, 256, 128)))
    tm, tn, tk = tiles
    nk = k // tk
    a_spec = (pl.BlockSpec((tm, tk), lambda i, j, kk: (i, kk)) if ca == 1
              else pl.BlockSpec((tk, tm), lambda i, j, kk: (kk, i)))
    b_spec = (pl.BlockSpec((tk, tn), lambda i, j, kk: (kk, j)) if cb == 0
              else pl.BlockSpec((tn, tk), lambda i, j, kk: (j, kk)))

    def body(a_ref, b_ref, o_ref):
        part = _dg(a_ref[...].astype(BF16), b_ref[...].astype(BF16), ca, cb)
        if nk == 1:
            o_ref[...] = part
        else:
            kk = pl.program_id(2)

            @pl.when(kk == 0)
            def _():
                o_ref[...] = part

            @pl.when(kk > 0)
            def _():
                o_ref[...] += part

    grid = (m // tm, n // tn, nk)
    if ride is None:
        return pl.pallas_call(
            body,
            grid=grid,
            in_specs=[a_spec, b_spec],
            out_specs=pl.BlockSpec((tm, tn), lambda i, j, kk: (i, j)),
            out_shape=jax.ShapeDtypeStruct((m, n), F32),
            compiler_params=_cparams(("parallel", "parallel", "arbitrary")),
            name=name,
        )(a, b)
    body, r_in, r_out, r_shape, r_scratch = _riding(body, 2, 1, 0, ride, grid)
    return pl.pallas_call(
        body,
        grid=grid,
        in_specs=[a_spec, b_spec] + r_in,
        out_specs=[pl.BlockSpec((tm, tn), lambda i, j, kk: (i, j))] + r_out,
        out_shape=[jax.ShapeDtypeStruct((m, n), F32)] + r_shape,
        scratch_shapes=r_scratch,
        compiler_params=_cparams(("arbitrary", "arbitrary", "arbitrary")),
        name=name,
    )(a, b, *ride[0])


ROW_BLOCK = 256
ROW_VMEM_BUDGET = 16 * 1024 * 1024


def _pieces(val, pw):
    return [val[:, j * pw:(j + 1) * pw] for j in range(val.shape[1] // pw)]


def _flat(groups):
    arrays, sizes = [], []
    for g in groups:
        g = g if isinstance(g, (tuple, list)) else (g,)
        arrays += list(g)
        sizes.append(len(g))
    return arrays, sizes


def _regroup(refs, sizes):
    out, at = [], 0
    for n in sizes:
        val = refs[at][...]
        for r in refs[at + 1:at + n]:
            val = val + r[...]
        out.append(val)
        at += n
    return out


class _Rowwise:
    def __init__(self, fn, name, row_wpw, par_pw, out_wpw, n_diff=None, block_rows=None):
        self.fn, self.name, self.row_wpw, self.par_pw, self.out_wpw = fn, name, row_wpw, par_pw, out_wpw
        self.n_diff = len(row_wpw) if n_diff is None else n_diff
        self.block_rows = block_rows

    def _load(self, row_vals, par_refs, br, with_ctx):
        row = pl.program_id(0) * br + lax.broadcasted_iota(jnp.int32, (br, 1), 0)
        is_ctx = (row < (CTX_LEN if with_ctx else 0)).astype(F32)
        rows = [_pieces(v, pw) for v, (_, pw) in zip(row_vals, self.row_wpw)]
        pars = []
        for p, pw in zip(par_refs, self.par_pw):
            val = p[...].astype(F32)
            if p.shape[0] == 2:
                val = is_ctx * val[0:1, :] + (1.0 - is_ctx) * val[1:2, :]
            pars.append(_pieces(val, pw))
        return rows, pars, is_ctx

    def _block_rows(self, n_rows, widths):
        if self.block_rows:
            return self.block_rows
        for br in (1088, 1024, 544, 512, 272):
            if n_rows % br == 0 and 2 * 4 * br * sum(widths) <= ROW_VMEM_BUDGET:
                return br
        return ROW_BLOCK

    def _row_specs(self, br, sizes, cols):
        out = []
        for (w, _), n, c in zip(self.row_wpw, sizes, cols):
            out += [pl.BlockSpec((br, w), lambda i, c=c: (i, c))] * n
        return out

    def fwd(self, rows, params, cols=None):
        arrays, sizes = _flat(rows)
        cols = cols or [0] * len(rows)
        n_rows = arrays[0].shape[0]
        n_in = len(arrays)
        br = self._block_rows(n_rows, [w for (w, _), n in zip(self.row_wpw, sizes) for _ in range(n)]
                              + [w for w, _ in self.out_wpw])

        def body(*refs):
            r, p, _ = self._load(_regroup(refs[:n_in], sizes), refs[n_in:n_in + len(params)], br, n_rows == ROWS)
            for o_ref, pieces, (_, pw) in zip(refs[n_in + len(params):], self.fn(r, p), self.out_wpw):
                for j, piece in enumerate(pieces):
                    o_ref[:, j * pw:(j + 1) * pw] = piece

        return pl.pallas_call(
            body,
            grid=(n_rows // br,),
            in_specs=self._row_specs(br, sizes, cols) + [pl.BlockSpec(p.shape, lambda i: (0, 0)) for p in params],
            out_specs=[pl.BlockSpec((br, w), lambda i: (i, 0)) for w, _ in self.out_wpw],
            out_shape=[jax.ShapeDtypeStruct((n_rows, w), F32) for w, _ in self.out_wpw],
            compiler_params=_cparams(("parallel",)),
            name=self.name + "_fwd",
        )(*arrays, *params)

    def bwd(self, rows, params, douts, cols=None, bf16_rows=()):
        arrays, sizes = _flat(rows)
        darrays, dsizes = _flat(douts)
        cols = cols or [0] * len(rows)
        n_rows = arrays[0].shape[0]
        n_in, n_par, n_dout, n_diff = len(arrays), len(params), len(darrays), self.n_diff
        br = self._block_rows(n_rows, [w for (w, _), n in zip(self.row_wpw, sizes) for _ in range(n)]
                              + [w for (w, _), n in zip(self.out_wpw, dsizes) for _ in range(n)]
                              + [w for w, _ in self.row_wpw[:n_diff]])

        def body(*refs):
            par_refs = refs[n_in:n_in + n_par]
            dout_refs = refs[n_in + n_par:n_in + n_par + n_dout]
            drow_refs = refs[n_in + n_par + n_dout:n_in + n_par + n_dout + n_diff]
            dpar_refs = refs[n_in + n_par + n_dout + n_diff:]

            @pl.when(pl.program_id(0) == 0)
            def _():
                for d in dpar_refs:
                    d[...] = jnp.zeros_like(d)

            r, p, is_ctx = self._load(_regroup(refs[:n_in], sizes), par_refs, br, n_rows == ROWS)
            cts = [_pieces(d, pw) for d, (_, pw) in zip(_regroup(dout_refs, dsizes), self.out_wpw)]
            fixed = r[n_diff:]
            _, vjp = jax.vjp(lambda rd, pp: self.fn(rd + fixed, pp), r[:n_diff], p)
            dr, dp = vjp(cts)
            for d_ref, pieces, (_, pw) in zip(drow_refs, dr, self.row_wpw):
                for j, piece in enumerate(pieces):
                    d_ref[:, j * pw:(j + 1) * pw] = piece.astype(d_ref.dtype)
            for d_ref, pieces, pw in zip(dpar_refs, dp, self.par_pw):
                for j, piece in enumerate(pieces):
                    lanes = slice(j * pw, (j + 1) * pw)
                    if d_ref.shape[0] != 2:
                        d_ref[:, lanes] += piece
                    else:
                        d_ref[0:1, lanes] += jnp.sum(is_ctx * piece, axis=0, keepdims=True)
                        d_ref[1:2, lanes] += jnp.sum((1.0 - is_ctx) * piece, axis=0, keepdims=True)

        par_specs = [pl.BlockSpec(p.shape, lambda i: (0, 0)) for p in params]
        dout_specs = []
        for (w, _), n in zip(self.out_wpw, dsizes):
            dout_specs += [pl.BlockSpec((br, w), lambda i: (i, 0))] * n
        drow_w = [w for w, _ in self.row_wpw[:n_diff]]
        g = pl.pallas_call(
            body,
            grid=(n_rows // br,),
            in_specs=self._row_specs(br, sizes, cols) + par_specs + dout_specs,
            out_specs=[pl.BlockSpec((br, w), lambda i: (i, 0)) for w in drow_w] + par_specs,
            out_shape=[jax.ShapeDtypeStruct((n_rows, w), BF16 if a in bf16_rows else F32) for a, w in enumerate(drow_w)]
            + [jax.ShapeDtypeStruct(p.shape, F32) for p in params],
            compiler_params=_cparams(("arbitrary",)),
            name=self.name + "_bwd",
        )(*arrays, *params, *darrays)
        return list(g[:n_diff]), list(g[n_diff:])


def _fn_norm_mod(rows, pars):
    (x,), (nw,), (shift,), (scale,) = rows[0], pars[0], pars[1], pars[2]
    y = x * lax.rsqrt(jnp.mean(x * x, axis=-1, keepdims=True) + EPS) * nw
    return [[y * (1.0 + scale) + shift]]


def _fn_head_rms_gate(rows, pars):
    (w,) = pars[0]
    return [[o * lax.rsqrt(jnp.mean(o * o, axis=-1, keepdims=True) + EPS) * w * _k_silu(z)
             for o, z in zip(rows[0], rows[1])]]


def _fn_group_norm_gate(rows, pars):
    out = []
    for o, z, w in zip(rows[0], rows[1], pars[0]):
        mu = jnp.mean(o, axis=-1, keepdims=True)
        var = jnp.mean(jnp.square(o - mu), axis=-1, keepdims=True)
        out.append((o - mu) * lax.rsqrt(var + EPS) * w * _k_silu(z))
    return [out]


def _fn_mix_merge(rows, pars):
    oa, za, (ob,), (zb,), oc, zc, (ma,), (mb,), (mc,) = rows
    na, nc, (wa,), (wb,), (wc,) = pars
    ya = jnp.concatenate(_fn_head_rms_gate([oa, za], [na])[0], axis=1)
    yb = ob * _k_silu(zb)
    yc = jnp.concatenate(_fn_group_norm_gate([oc, zc], [nc])[0], axis=1)
    return [[_k_sigmoid(ma) * _bdot(ya, wa, 1, 0) + _k_sigmoid(mb) * _bdot(yb, wb, 1, 0)
             + _k_sigmoid(mc) * _bdot(yc, wc, 1, 0)]]


def _fn_out_residual(rows, pars):
    (res,), (merged,), (w,), (gate,) = rows[0], rows[1], pars[0], pars[1]
    return [[res + gate * _bdot(merged, w, 1, 0)]]


def _fn_loss(rows, pars):
    (x,), (target,), (w,) = rows[0], rows[1], pars[0]
    y = x * lax.rsqrt(jnp.mean(x * x, axis=-1, keepdims=True) + EPS) * w
    per_row = 0.5 * jnp.mean(jnp.square(y - target), axis=-1, keepdims=True)
    return [[jnp.broadcast_to(per_row, (per_row.shape[0], 128))]]


def _fn_b_rope(rows, pars):
    q, (k, v), (cos,), (sin,) = rows
    rot = lambda x: x * cos + _swap16(x) * sin
    return [[rot(x) for x in q], [rot(k), v]]


def _fn_c_rope(rows, pars):
    q, k, (cos,), (sin,) = rows
    rot = lambda x: x * cos + _swap64(x) * sin
    return [[rot(x) for x in q], [rot(x) * (C_HD ** -0.5) for x in k]]


_norm_mod = _Rowwise(_fn_norm_mod, "norm_mod", [(D_MODEL, D_MODEL)], [D_MODEL] * 3, [(D_MODEL, D_MODEL)])
_out_residual = _Rowwise(_fn_out_residual, "out_residual", [(D_MODEL, D_MODEL)] * 2, [D_MODEL] * 2,
                         [(D_MODEL, D_MODEL)], block_rows=544)
_loss_rows = _Rowwise(_fn_loss, "loss", [(D_MODEL, D_MODEL)] * 2, [D_MODEL], [(128, 128)], n_diff=1)
_mix_merge = _Rowwise(_fn_mix_merge, "mix_merge",
                      [(512, 128), (512, 128), (512, 512), (512, 512), (512, 128), (512, 128)] + [(D_MODEL, D_MODEL)] * 3,
                      [128, 128] + [D_MODEL] * 3, [(D_MODEL, D_MODEL)], block_rows=256)
MIX_MERGE_COLS = [0, C_AZ // 512, 0, C_BZ // 512, 0, C_CZ // 512] + [C_MERGE // 1024 + i for i in range(3)]
_b_rope = _Rowwise(_fn_b_rope, "b_rope", [(512, 128), (256, 128), (128, 128), (128, 128)], [],
                   [(512, 128), (256, 128)], n_diff=2)
_c_rope = _Rowwise(_fn_c_rope, "c_rope", [(512, 128), (512, 128), (128, 128), (128, 128)], [],
                   [(512, 128), (512, 128)], n_diff=2)


HALO = 8
EXT = ROW_BLOCK + 2 * HALO


def _halo_specs(col, width=512):
    last = ROWS // HALO - 1
    per = ROW_BLOCK // HALO
    prev = pl.BlockSpec((HALO, width), lambda i: (jnp.maximum(i * per - 1, 0), col))
    cur = pl.BlockSpec((ROW_BLOCK, width), lambda i: (i, col))
    nxt = pl.BlockSpec((HALO, width), lambda i: (jnp.minimum((i + 1) * per, last), col))
    return [prev, cur, nxt]


def _extended(prev_ref, cur_ref, next_ref):
    i = pl.program_id(0)
    prev_ok = i >= 2
    next_ok = jnp.logical_and(i >= 1, i < ROWS // ROW_BLOCK - 1)
    return jnp.concatenate([jnp.where(prev_ok, prev_ref[...], 0.0), cur_ref[...],
                            jnp.where(next_ok, next_ref[...], 0.0)], axis=0)


def _conv_taps(x_ext, w_ref, flip):
    acc = None
    for j in range(A_CONV):
        shift = (j - 2) if flip else (2 - j)
        term = w_ref[j:j + 1, :] * pltpu.roll(x_ext, shift % EXT, 0)
        acc = term if acc is None else acc + term
    return acc


def _conv_post(pre_pieces, normalize, scale):
    out = []
    for p in pre_pieces:
        y = _k_silu(p)
        if normalize:
            y = y * lax.rsqrt(jnp.sum(y * y, axis=-1, keepdims=True) + EPS) * scale
        out.append(y)
    return out


def _a_prep_fwd(proj, conv8, col, normalize, scale, name):
    def body(prev_ref, cur_ref, next_ref, w_ref, o_ref):
        pre = _conv_taps(_extended(prev_ref, cur_ref, next_ref), w_ref, False)[HALO:HALO + ROW_BLOCK]
        for h, y in enumerate(_conv_post(_pieces(pre, 128), normalize, scale)):
            o_ref[:, h * 128:(h + 1) * 128] = y

    return pl.pallas_call(
        body,
        grid=(ROWS // ROW_BLOCK,),
        in_specs=_halo_specs(col) + [pl.BlockSpec((8, 512), lambda i: (0, col))],
        out_specs=pl.BlockSpec((ROW_BLOCK, 512), lambda i: (i, 0)),
        out_shape=jax.ShapeDtypeStruct((ROWS, 512), F32),
        compiler_params=_cparams(("parallel",)),
        name=name + "_fwd",
    )(proj, proj, proj, conv8)


def _a_prep_bwd(proj, conv8, col, normalize, scale, dout_f, dout_r, name):
    def body(xp, xc, xn, w_ref, fp, fc, fn_, rp, rc, rn, dx_ref, dw_ref):
        @pl.when(pl.program_id(0) == 0)
        def _():
            dw_ref[...] = jnp.zeros_like(dw_ref)

        x_ext = _extended(xp, xc, xn)
        dout = _extended(fp, fc, fn_) + _extended(rp, rc, rn)
        pre = _conv_taps(x_ext, w_ref, False)
        _, vjp = jax.vjp(lambda p: _conv_post(p, normalize, scale), _pieces(pre, 128))
        (dpre,) = vjp(_pieces(dout, 128))
        dpre = jnp.concatenate(dpre, axis=1)
        dx_ref[...] = _conv_taps(dpre, w_ref, True)[HALO:HALO + ROW_BLOCK].astype(BF16)
        own = dpre[HALO:HALO + ROW_BLOCK]
        for j in range(A_CONV):
            shifted = pltpu.roll(x_ext, (2 - j) % EXT, 0)[HALO:HALO + ROW_BLOCK]
            dw_ref[j:j + 1, :] += jnp.sum(own * shifted, axis=0, keepdims=True)

    return pl.pallas_call(
        body,
        grid=(ROWS // ROW_BLOCK,),
        in_specs=_halo_specs(col) + [pl.BlockSpec((8, 512), lambda i: (0, col))] + _halo_specs(0) + _halo_specs(0),
        out_specs=[pl.BlockSpec((ROW_BLOCK, 512), lambda i: (i, 0)), pl.BlockSpec((8, 512), lambda i: (0, 0))],
        out_shape=[jax.ShapeDtypeStruct((ROWS, 512), BF16), jax.ShapeDtypeStruct((8, 512), F32)],
        compiler_params=_cparams(("arbitrary",)),
        name=name + "_bwd",
    )(proj, proj, proj, conv8, dout_f, dout_f, dout_f, dout_r, dout_r, dout_r)


N_CHAIN = 8


def _rev_chunk(s, chunk):
    n_ctx, n_all = CTX_LEN // chunk, ROWS // chunk
    return jnp.where(s < n_ctx, n_ctx - 1 - s, n_all + n_ctx - 1 - s)


def _scan_specs(step_of, chunk, v_col=0):
    cf = step_of
    cr = lambda n: _rev_chunk(step_of(n), chunk)

    def pair(shape, index):
        return (pl.BlockSpec(shape, lambda n: index(cf(n))), pl.BlockSpec(shape, lambda n: index(cr(n))))

    return dict(
        tok=pair((chunk, 512), lambda c: (c, 0)),
        tokv=pair((chunk, 512), lambda c: (c, v_col)),
        row=pair((4, 1, 1, chunk), lambda c: (0, c, 0, 0)),
        one=pair((4, 1, 1, 1), lambda c: (0, c, 0, 0)),
        state=pair((None, 4, 128, 128), lambda c: (c, 0, 0, 0)),
        tinv=pair((None, 4, chunk, chunk), lambda c: (c, 0, 0, 0)),
    )


def _both(specs, kinds):
    out = []
    for kind in kinds:
        out += list(specs[kind])
    return out


def _scan_call(body, name, in_specs, out_specs, out_shape, operands, ride, chunk):
    grid = (ROWS // chunk,)
    body, r_in, r_out, r_shape, r_scratch = _riding(body, len(in_specs), len(out_specs), 1, ride, grid)
    return pl.pallas_call(
        body,
        grid=grid,
        in_specs=in_specs + r_in,
        out_specs=out_specs + r_out,
        out_shape=out_shape + r_shape,
        scratch_shapes=[pltpu.VMEM((N_CHAIN, 128, 128), F32)] + r_scratch,
        compiler_params=_cparams(("arbitrary",)),
        name=name,
    )(*operands, *(ride[0] if ride else []))


def _chain_masks():
    ii = lax.broadcasted_iota(jnp.int32, (CHUNK, CHUNK), 0)
    jj = lax.broadcasted_iota(jnp.int32, (CHUNK, CHUNK), 1)
    eye = jnp.where(ii == jj, 1.0, 0.0).astype(F32)
    lower = (ii >= jj, ii > jj)
    upper = (ii <= jj, ii < jj)
    return [lower] * 4 + [upper] * 4, eye


INV_BLOCK = 64


def _series_inverse(ls):
    ii = lax.broadcasted_iota(jnp.int32, (INV_BLOCK, INV_BLOCK), 0)
    jj = lax.broadcasted_iota(jnp.int32, (INV_BLOCK, INV_BLOCK), 1)
    eye = jnp.where(ii == jj, 1.0, 0.0).astype(F32)
    doublings = INV_BLOCK.bit_length() - 2
    xs = [eye - l for l in ls]
    ps = [_hdot(l, l) for l in ls]
    for i in range(doublings):
        xs = [x + _hdot(x, p) for x, p in zip(xs, ps)]
        if i < doublings - 1:
            ps = [_hdot(p, p) for p in ps]
    return xs


def _tri_inv_all(ls, upper):
    size = ls[0].shape[0]
    if size == INV_BLOCK:
        return _series_inverse(ls)
    n, h = len(ls), size // 2
    diag = _tri_inv_all([l[:h, :h] for l in ls] + [l[h:, h:] for l in ls], list(upper) * 2)
    out = []
    zero = jnp.zeros((h, h), F32)
    for i, l in enumerate(ls):
        a, d = diag[i], diag[n + i]
        if upper[i]:
            off = -_hdot(_hdot(a, l[:h, h:]), d)
            out.append(jnp.concatenate([jnp.concatenate([a, off], axis=1), jnp.concatenate([zero, d], axis=1)], axis=0))
        else:
            off = -_hdot(_hdot(d, l[h:, :h]), a)
            out.append(jnp.concatenate([jnp.concatenate([a, zero], axis=1), jnp.concatenate([off, d], axis=1)], axis=0))
    return out


@jax.custom_vjp
def _inv_saved(l, x):
    return x


def _inv_saved_fwd(l, x):
    return x, x


def _inv_saved_bwd(x, dx):
    return -_bdot(x, _bdot(dx, x, 1, 1), 0, 0), jnp.zeros_like(x)


_inv_saved.defvjp(_inv_saved_fwd, _inv_saved_bwd)


def _delta_chains(q, k, v, beta_r, gcr, gl, s, masks, eye, tinv_saved):
    n = range(len(q))
    beta = [jnp.sum(eye * beta_r[i], axis=1, keepdims=True) for i in n]
    gcc = [jnp.sum(eye * gcr[i], axis=1, keepdims=True) for i in n]
    decay = [jnp.exp(jnp.where(masks[i][0], gcc[i] - gcr[i], NEG)) for i in n]
    kb = [k[i] * beta[i] for i in n]
    lmat = [jnp.where(masks[i][1], _bdot(kb[i], k[i], 1, 1) * decay[i], 0.0) for i in n]
    if tinv_saved is None:
        tinv = _tri_inv_all(lmat, [i >= 4 for i in n])
    else:
        tinv = [_inv_saved(lmat[i], tinv_saved[i]) for i in n]
    eg = [jnp.exp(gcc[i]) for i in n]
    u = [_bdot(tinv[i], v[i] * beta[i], 1, 0) for i in n]
    w = [_bdot(tinv[i], kb[i] * eg[i], 1, 0) for i in n]
    qk = [_bdot(q[i], k[i], 1, 1) * decay[i] for i in n]
    v_new = [u[i] - _bdot(w[i], s[i], 1, 0) for i in n]
    o = [_bdot(q[i] * eg[i], s[i], 1, 0) + _bdot(qk[i], v_new[i], 1, 0) for i in n]
    s_new = [s[i] * jnp.exp(gl[i]) + _bdot(k[i] * jnp.exp(gl[i] - gcc[i]), v_new[i], 0, 0) for i in n]
    return (o, s_new), tinv


def _chain_loads(tok_pairs, small_pairs):
    toks = [[pair[i // 4][:, (i % 4) * 128:(i % 4 + 1) * 128] for i in range(N_CHAIN)] for pair in tok_pairs]
    smalls = [[pair[i // 4][i % 4] for i in range(N_CHAIN)] for pair in small_pairs]
    return toks, smalls


def _delta_fwd_call(q, k, v, beta, gc, gl, ride=None):
    sp = _scan_specs(lambda n: n, CHUNK)

    def body(qf, qr, kf, kr, vf, vr, bf, br, gcrf, gcrr, glf, glr, of, orv, ssf, ssr, tsf, tsr, s_scr):
        @pl.when(pl.program_id(0) == 0)
        def _():
            s_scr[...] = jnp.zeros_like(s_scr)

        masks, eye = _chain_masks()
        (qs, ks, vs), _ = _chain_loads([(qf, qr), (kf, kr), (vf, vr)], [])
        bs = [(bf, br)[i // 4][i % 4, 0] for i in range(N_CHAIN)]
        gcrs = [(gcrf, gcrr)[i // 4][i % 4, 0] for i in range(N_CHAIN)]
        gls = [(glf, glr)[i // 4][i % 4, 0] for i in range(N_CHAIN)]
        ss = [s_scr[i] for i in range(N_CHAIN)]
        (o, s_new), tinv = _delta_chains(qs, ks, vs, bs, gcrs, gls, ss, masks, eye, None)
        for i in range(N_CHAIN):
            d, h = i // 4, i % 4
            (ssf, ssr)[d][h] = ss[i]
            (tsf, tsr)[d][h] = tinv[i]
            (of, orv)[d][:, h * 128:(h + 1) * 128] = o[i]
            s_scr[i] = s_new[i]

    return _scan_call(
        body, "delta_fwd",
        _both(sp, ["tok", "tok", "tok", "row", "row", "one"]),
        _both(sp, ["tok", "state", "tinv"]),
        [jax.ShapeDtypeStruct((ROWS, 512), F32)] * 2 + [jax.ShapeDtypeStruct((N_CHUNK, 4, 128, 128), F32)] * 2
        + [jax.ShapeDtypeStruct((N_CHUNK, 4, CHUNK, CHUNK), F32)] * 2,
        [q, q, k, k, v, v, *beta, *gc, *gl], ride, CHUNK)


def _delta_bwd_call(q, k, v, beta, gc, gl, ssave, tsave, do, ride=None):
    sp = _scan_specs(lambda n: N_CHUNK - 1 - n, CHUNK)

    def body(qf, qr, kf, kr, vf, vr, bf, br, gcrf, gcrr, glf, glr, ssf, ssr, tsf, tsr, dof, dor,
             dqf, dqr, dkf, dkr, dvf, dvr, dbf, dbr, dgcrf, dgcrr, dglf, dglr, ds_scr):
        @pl.when(pl.program_id(0) == 0)
        def _():
            ds_scr[...] = jnp.zeros_like(ds_scr)

        masks, eye = _chain_masks()
        (qs, ks, vs, dos), (ss, ts) = _chain_loads(
            [(qf, qr), (kf, kr), (vf, vr), (dof, dor)], [(ssf, ssr), (tsf, tsr)])
        bs = [(bf, br)[i // 4][i % 4, 0] for i in range(N_CHAIN)]
        gcrs = [(gcrf, gcrr)[i // 4][i % 4, 0] for i in range(N_CHAIN)]
        gls = [(glf, glr)[i // 4][i % 4, 0] for i in range(N_CHAIN)]
        fn = lambda *a: _delta_chains(*a, masks, eye, ts)
        _, vjp, _ = jax.vjp(fn, qs, ks, vs, bs, gcrs, gls, ss, has_aux=True)
        dq, dk, dv, db, dgcr, dgl, ds = vjp((dos, [ds_scr[i] for i in range(N_CHAIN)]))
        for i in range(N_CHAIN):
            d, h = i // 4, i % 4
            hs = slice(h * 128, (h + 1) * 128)
            (dqf, dqr)[d][:, hs] = dq[i]
            (dkf, dkr)[d][:, hs] = dk[i]
            (dvf, dvr)[d][:, hs] = dv[i]
            (dbf, dbr)[d][h, 0] = db[i]
            (dgcrf, dgcrr)[d][h, 0] = dgcr[i]
            (dglf, dglr)[d][h, 0] = dgl[i]
            ds_scr[i] = ds[i]

    tok = jax.ShapeDtypeStruct((ROWS, 512), F32)
    return _scan_call(
        body, "delta_bwd",
        _both(sp, ["tok", "tok", "tok", "row", "row", "one", "state", "tinv", "tok"]),
        _both(sp, ["tok", "tok", "tok", "row", "row", "one"]),
        [tok] * 6 + [jax.ShapeDtypeStruct((4, N_CHUNK, 1, CHUNK), F32)] * 4
        + [jax.ShapeDtypeStruct((4, N_CHUNK, 1, 1), F32)] * 2,
        [q, q, k, k, v, v, *beta, *gc, *gl, *ssave, *tsave, do, do], ride, CHUNK)


def _ret_chains(q, k, v, dm, qs, ks, cd, s):
    n = range(len(q))
    a = [_bdot(q[i], k[i], 1, 1) * dm[i] for i in n]
    o = [_bdot(a[i], v[i], 1, 0) + _bdot(q[i] * qs[i], s[i], 1, 0) for i in n]
    s_new = [s[i] * cd[i] + _bdot(k[i] * ks[i], v[i], 0, 0) for i in n]
    return o, s_new


RET_CONST_SHAPES = ((N_CHAIN, RET_CHUNK, RET_CHUNK), (N_CHAIN, RET_CHUNK, 1), (N_CHAIN, RET_CHUNK, 1), (N_CHAIN, 1, 1))


def _ret_const_specs():
    return [pl.BlockSpec(shape, lambda n: (0, 0, 0)) for shape in RET_CONST_SHAPES]


def _ret_fwd_call(q, k, v, v_col, dm, qs, ks, cd, ride=None):
    sp = _scan_specs(lambda n: n, RET_CHUNK, v_col)

    def body(qf, qr, kf, kr, vf, vr, dm_ref, qs_ref, ks_ref, cd_ref, of, orv, ssf, ssr, s_scr):
        @pl.when(pl.program_id(0) == 0)
        def _():
            s_scr[...] = jnp.zeros_like(s_scr)

        (qc, kc, vc), _ = _chain_loads([(qf, qr), (kf, kr), (vf, vr)], [])
        ss = [s_scr[i] for i in range(N_CHAIN)]
        consts = [[r[i] for i in range(N_CHAIN)] for r in (dm_ref, qs_ref, ks_ref, cd_ref)]
        o, s_new = _ret_chains(qc, kc, vc, *consts, ss)
        for i in range(N_CHAIN):
            d, h = i // 4, i % 4
            (ssf, ssr)[d][h] = ss[i]
            (of, orv)[d][:, h * 128:(h + 1) * 128] = o[i]
            s_scr[i] = s_new[i]

    return _scan_call(
        body, "ret_fwd",
        _both(sp, ["tok", "tok", "tokv"]) + _ret_const_specs(),
        _both(sp, ["tok", "state"]),
        [jax.ShapeDtypeStruct((ROWS, 512), F32)] * 2
        + [jax.ShapeDtypeStruct((ROWS // RET_CHUNK, 4, 128, 128), F32)] * 2,
        [q, q, k, k, v, v, dm, qs, ks, cd], ride, RET_CHUNK)


def _ret_bwd_call(q, k, v, v_col, dm, qs, ks, cd, ssave, do, ride=None):
    sp = _scan_specs(lambda n: ROWS // RET_CHUNK - 1 - n, RET_CHUNK, v_col)

    def body(qf, qr, kf, kr, vf, vr, dm_ref, qs_ref, ks_ref, cd_ref, ssf, ssr, dof, dor,
             dqf, dqr, dkf, dkr, dvf, dvr, ddm_ref, dqs_ref, dks_ref, dcd_ref, ds_scr):
        @pl.when(pl.program_id(0) == 0)
        def _():
            ds_scr[...] = jnp.zeros_like(ds_scr)
            ddm_ref[...] = jnp.zeros_like(ddm_ref)
            dqs_ref[...] = jnp.zeros_like(dqs_ref)
            dks_ref[...] = jnp.zeros_like(dks_ref)
            dcd_ref[...] = jnp.zeros_like(dcd_ref)

        (qc, kc, vc, dos), (ss,) = _chain_loads([(qf, qr), (kf, kr), (vf, vr), (dof, dor)], [(ssf, ssr)])
        consts = [[r[i] for i in range(N_CHAIN)] for r in (dm_ref, qs_ref, ks_ref, cd_ref)]
        _, vjp = jax.vjp(_ret_chains, qc, kc, vc, *consts, ss)
        dq, dk, dv, ddm, dqs, dks, dcd, ds = vjp((dos, [ds_scr[i] for i in range(N_CHAIN)]))
        for i in range(N_CHAIN):
            d, h = i // 4, i % 4
            hs = slice(h * 128, (h + 1) * 128)
            (dqf, dqr)[d][:, hs] = dq[i]
            (dkf, dkr)[d][:, hs] = dk[i]
            (dvf, dvr)[d][:, hs] = dv[i]
            ddm_ref[i] += ddm[i]
            dqs_ref[i] += dqs[i]
            dks_ref[i] += dks[i]
            dcd_ref[i] += dcd[i]
            ds_scr[i] = ds[i]

    tok = jax.ShapeDtypeStruct((ROWS, 512), F32)
    return _scan_call(
        body, "ret_bwd",
        _both(sp, ["tok", "tok", "tokv"]) + _ret_const_specs() + _both(sp, ["state", "tok"]),
        _both(sp, ["tok", "tok", "tok"]) + _ret_const_specs(),
        [tok] * 6 + [jax.ShapeDtypeStruct(shape, F32) for shape in RET_CONST_SHAPES],
        [q, q, k, k, v, v, dm, qs, ks, cd, *ssave, do, do], ride, RET_CHUNK)


N_QBLK = ROWS // B_BLOCK
CTX_QBLK = CTX_LEN // B_BLOCK


def _attn_heads(q, kc, vc, kw, vw, sink, valid):
    n = range(len(q))
    qs = [q[i] * (B_HD ** -0.5) for i in n]
    s_c = [_bdot(qs[i], kc[i], 1, 1) for i in n]
    s_w = [jnp.where(valid, _bdot(qs[i], kw[i], 1, 1), NEG) for i in n]
    m = [lax.stop_gradient(jnp.maximum(jnp.maximum(jnp.max(s_c[i], axis=-1, keepdims=True), sink[i]),
                                       jnp.max(s_w[i], axis=-1, keepdims=True))) for i in n]
    e_c = [jnp.exp(s_c[i] - m[i]) for i in n]
    e_w = [jnp.exp(s_w[i] - m[i]) for i in n]
    den = [jnp.sum(e_c[i], axis=-1, keepdims=True) + jnp.sum(e_w[i], axis=-1, keepdims=True)
           + jnp.exp(sink[i] - m[i]) for i in n]
    return [(_bdot(e_c[i], vc[i], 1, 0) + _bdot(e_w[i], vw[i], 1, 0)) / den[i] for i in n]


def _attn_loads(q_ref, kv_ref, sink_ref, start):
    q, kc, vc, kw, vw, sink = [], [], [], [], [], []
    for hk in range(B_KV_HEADS):
        ks = slice(hk * B_HD, (hk + 1) * B_HD)
        vs = slice(128 + hk * B_HD, 128 + (hk + 1) * B_HD)
        grp = (kv_ref[0:CTX_LEN, ks], kv_ref[0:CTX_LEN, vs],
               kv_ref[pl.ds(start, 3 * B_BLOCK), ks], kv_ref[pl.ds(start, 3 * B_BLOCK), vs])
        for g in range(4):
            h = hk * 4 + g
            q.append(q_ref[:, h * B_HD:(h + 1) * B_HD])
            for lst, val in zip((kc, vc, kw, vw), grp):
                lst.append(val)
            sink.append(jnp.full((1, 1), sink_ref[h], F32))
    return q, kc, vc, kw, vw, sink


def _window(blk):
    xblk = blk - CTX_QBLK
    first = jnp.clip((xblk - 1) * B_BLOCK, 0, SEQ - 3 * B_BLOCK)
    qpos = xblk * B_BLOCK + lax.broadcasted_iota(jnp.int32, (B_BLOCK, 3 * B_BLOCK), 0)
    kpos = first + lax.broadcasted_iota(jnp.int32, (B_BLOCK, 3 * B_BLOCK), 1)
    far = jnp.where(blk >= CTX_QBLK, 0, 2 * SEQ)
    valid = jnp.abs(kpos - qpos) + far <= WINDOW
    return pl.multiple_of(first + CTX_LEN, B_BLOCK), valid


def _attn_specs():
    qspec = pl.BlockSpec((B_BLOCK, 512), lambda i: (i, 0))
    kvspec = pl.BlockSpec((ROWS, 256), lambda i: (0, 0))
    return qspec, kvspec, pl.BlockSpec(memory_space=pltpu.SMEM)


def _attn_fwd_call(q, kv, sink, ride=None):
    def body(q_ref, kv_ref, sink_ref, o_ref):
        start, valid = _window(pl.program_id(0))
        out = _attn_heads(*_attn_loads(q_ref, kv_ref, sink_ref, start), valid)
        for h in range(B_Q_HEADS):
            o_ref[:, h * B_HD:(h + 1) * B_HD] = out[h]

    qspec, kvspec, sspec = _attn_specs()
    body, r_in, r_out, r_shape, r_scratch = _riding(body, 3, 1, 0, ride, (N_QBLK,))
    return pl.pallas_call(
        body,
        grid=(N_QBLK,),
        in_specs=[qspec, kvspec, sspec] + r_in,
        out_specs=[qspec] + r_out,
        out_shape=[jax.ShapeDtypeStruct((ROWS, 512), F32)] + r_shape,
        scratch_shapes=r_scratch,
        compiler_params=_cparams(("arbitrary",)),
        name="attn_fwd",
    )(q, kv, sink, *(ride[0] if ride else []))


def _attn_bwd_call(q, kv, sink, do, ride=None):
    def body(q_ref, kv_ref, sink_ref, do_ref, dq_ref, dkv_ref, dsink_ref):
        @pl.when(pl.program_id(0) == 0)
        def _():
            dkv_ref[...] = jnp.zeros_like(dkv_ref)
            dsink_ref[...] = jnp.zeros_like(dsink_ref)

        start, valid = _window(pl.program_id(0))
        _, vjp = jax.vjp(functools.partial(_attn_heads, valid=valid), *_attn_loads(q_ref, kv_ref, sink_ref, start))
        dq, dkc, dvc, dkw, dvw, dsink = vjp([do_ref[:, h * B_HD:(h + 1) * B_HD] for h in range(B_Q_HEADS)])
        for h in range(B_Q_HEADS):
            dq_ref[:, h * B_HD:(h + 1) * B_HD] = dq[h]
            dsink_ref[h:h + 1, :] += jnp.broadcast_to(dsink[h], (1, 128))
        for hk in range(B_KV_HEADS):
            ks = slice(hk * B_HD, (hk + 1) * B_HD)
            vs = slice(128 + hk * B_HD, 128 + (hk + 1) * B_HD)
            grp = lambda parts: parts[hk * 4] + parts[hk * 4 + 1] + parts[hk * 4 + 2] + parts[hk * 4 + 3]
            dkv_ref[0:CTX_LEN, ks] += grp(dkc)
            dkv_ref[0:CTX_LEN, vs] += grp(dvc)
            dkv_ref[pl.ds(start, 3 * B_BLOCK), ks] += grp(dkw)
            dkv_ref[pl.ds(start, 3 * B_BLOCK), vs] += grp(dvw)

    qspec, kvspec, sspec = _attn_specs()
    body, r_in, r_out, r_shape, r_scratch = _riding(body, 4, 3, 0, ride, (N_QBLK,))
    return pl.pallas_call(
        body,
        grid=(N_QBLK,),
        in_specs=[qspec, kvspec, sspec, qspec] + r_in,
        out_specs=[qspec, kvspec, pl.BlockSpec((8, 128), lambda i: (0, 0))] + r_out,
        out_shape=[jax.ShapeDtypeStruct((ROWS, 512), F32), jax.ShapeDtypeStruct((ROWS, 256), F32),
                   jax.ShapeDtypeStruct((8, 128), F32)] + r_shape,
        scratch_shapes=r_scratch,
        compiler_params=_cparams(("arbitrary",)),
        name="attn_bwd",
    )(q, kv, sink, do, *(ride[0] if ride else []))


def _my_id():
    return 4 * lax.axis_index("x") + 2 * lax.axis_index("y") + lax.axis_index("c")


def _peer(k):
    x, y, c = lax.axis_index("x"), lax.axis_index("y"), lax.axis_index("c")
    return (1 - x if k & 4 else x, 1 - y if k & 2 else y, 1 - c if k & 1 else c)


SAME_CORE_PEERS = (2, 4, 6)


def _scatter_copies(ins, outs, sems):
    send_sems, recv_sems, local_sems = sems
    me = _my_id()
    own, remote = [], []
    for a in range(len(ins)):
        own.append(pltpu.make_async_copy(ins[a].at[me], outs[a].at[me], local_sems.at[a]))
        for k in range(1, N_DEV):
            peer_slot = jnp.bitwise_xor(me, k)
            common = dict(src_ref=ins[a].at[peer_slot], send_sem=send_sems.at[a, k - 1],
                          recv_sem=recv_sems.at[a, k - 1], device_id=_peer(k), device_id_type=MESH)
            remote.append((pltpu.make_async_remote_copy(dst_ref=outs[a].at[me], **common),
                           pltpu.make_async_remote_copy(dst_ref=outs[a].at[peer_slot], **common)))
    return own, remote


def _gather_copy(outs, sems, a, k, src, slot, to):
    return pltpu.make_async_remote_copy(src_ref=src, dst_ref=outs[a].at[slot], send_sem=sems[0].at[a, k - 1],
                                        recv_sem=sems[1].at[a, k - 1], device_id=_peer(to), device_id_type=MESH)


def _gather_first_copies(ins, outs, sems):
    me = _my_id()
    own = [pltpu.make_async_copy(ins[a], outs[a].at[me], sems[2].at[a]) for a in range(len(ins))]
    direct = [_gather_copy(outs, sems, a, k, ins[a], me, k) for a in range(len(ins)) for k in (1,) + SAME_CORE_PEERS]
    return own, direct


CHIP = "chip"
N_CHIP = N_DEV // 2


def _chip_copies(ins, outs, sems):
    send_sems, recv_sems, local_sems = sems
    chip = 2 * lax.axis_index("x") + lax.axis_index("y")
    own = [pltpu.make_async_copy(ins[a].at[chip], outs[a].at[chip], local_sems.at[a]) for a in range(len(ins))]
    remote = []
    for a in range(len(ins)):
        for k in range(1, N_CHIP):
            other = jnp.bitwise_xor(chip, k)
            common = dict(src_ref=ins[a].at[other], send_sem=send_sems.at[a, k - 1], recv_sem=recv_sems.at[a, k - 1],
                          device_id=_peer(2 * k), device_id_type=MESH)
            remote.append((pltpu.make_async_remote_copy(dst_ref=outs[a].at[chip], **common),
                           pltpu.make_async_remote_copy(dst_ref=outs[a].at[other], **common)))
    return own, remote


def _exchange_start(ins, outs, sems, gather):
    if gather is CHIP:
        own, remote = _chip_copies(ins, outs, sems)
    else:
        own, remote = _gather_first_copies(ins, outs, sems) if gather else _scatter_copies(ins, outs, sems)
    for cp in own:
        cp.start()
    for cp in remote:
        (cp if gather is True else cp[0]).start()


def _exchange_wait(ins, outs, sems, gather):
    if gather is not True:
        own, remote = _chip_copies(ins, outs, sems) if gather is CHIP else _scatter_copies(ins, outs, sems)
        for _, arrival in remote:
            arrival.wait_recv()
        for send, _ in remote:
            send.wait_send()
        for cp in own:
            cp.wait()
        return
    me = _my_id()
    own, direct = _gather_first_copies(ins, outs, sems)
    passed = []
    for a in range(len(ins)):
        for k in SAME_CORE_PEERS:
            origin = jnp.bitwise_xor(me, k)
            _gather_copy(outs, sems, a, k, ins[a], origin, k).wait_recv()
            onward = _gather_copy(outs, sems, a, k + 1, outs[a].at[origin], origin, 1)
            onward.start()
            passed.append(onward)
    for a in range(len(ins)):
        for k in (1, 3, 5, 7):
            _gather_copy(outs, sems, a, k, ins[a], jnp.bitwise_xor(me, k), 1).wait_recv()
    for cp in direct + passed:
        cp.wait_send()
    for cp in own:
        cp.wait()


def _exchange_plumbing(arrays, gather):
    n = len(arrays)
    hbm = [pl.BlockSpec(memory_space=pltpu.HBM)] * n
    if gather is CHIP:
        out_shape = [jax.ShapeDtypeStruct(a.shape, a.dtype) for a in arrays]
        peers = N_CHIP - 1
    else:
        out_shape = [jax.ShapeDtypeStruct((N_DEV,) + (a.shape if gather else a.shape[1:]), a.dtype) for a in arrays]
        peers = N_DEV - 1
    sems = [pltpu.SemaphoreType.DMA((n, peers)), pltpu.SemaphoreType.DMA((n, peers)), pltpu.SemaphoreType.DMA((n,))]
    return hbm, out_shape, sems


def _exchange(arrays, gather, name):
    n = len(arrays)

    def body(*refs):
        ins, outs, sems = refs[:n], refs[n:2 * n], refs[2 * n:]
        _exchange_start(ins, outs, sems, gather)
        _exchange_wait(ins, outs, sems, gather)

    hbm, out_shape, sems = _exchange_plumbing(arrays, gather)
    return pl.pallas_call(
        body,
        in_specs=hbm,
        out_specs=hbm,
        out_shape=out_shape,
        scratch_shapes=sems,
        compiler_params=pltpu.CompilerParams(has_side_effects=True),
        name=name,
    )(*arrays)


def _pair_swap(blocks, name):
    n = len(blocks)

    def body(*refs):
        ins, outs, (send_sems, recv_sems) = refs[:n], refs[n:2 * n], refs[2 * n:]
        core = lax.axis_index("c")
        copies = [pltpu.make_async_remote_copy(src_ref=ins[a].at[2 * chip + (1 - core)], dst_ref=outs[a].at[chip],
                                               send_sem=send_sems.at[a, chip], recv_sem=recv_sems.at[a, chip],
                                               device_id=_peer(1), device_id_type=MESH)
                  for a in range(n) for chip in range(N_CHIP)]
        for cp in copies:
            cp.start()
        for cp in copies:
            cp.wait_recv()
        for cp in copies:
            cp.wait_send()

    hbm = [pl.BlockSpec(memory_space=pltpu.HBM)] * n
    return pl.pallas_call(
        body,
        in_specs=hbm,
        out_specs=hbm,
        out_shape=[jax.ShapeDtypeStruct((N_CHIP,) + b.shape[1:], b.dtype) for b in blocks],
        scratch_shapes=[pltpu.SemaphoreType.DMA((n, N_CHIP)), pltpu.SemaphoreType.DMA((n, N_CHIP))],
        compiler_params=pltpu.CompilerParams(has_side_effects=True),
        name=name,
    )(*blocks)


def _chip_scatter(pairs, name):
    return _exchange(pairs, CHIP, name)


def _pair_sums(blocks, name):
    swapped = _pair_swap(blocks, name)
    core = lax.axis_index("c")
    pairs = []
    for b, s in zip(blocks, swapped):
        mine = lax.dynamic_index_in_dim(b.reshape((N_CHIP, 2) + b.shape[1:]), core, axis=1, keepdims=False)
        pairs.append((mine.astype(F32) + s.astype(F32)).astype(b.dtype))
    return pairs


def _scatter_two_level(blocks, name):
    return _chip_scatter(_pair_sums(blocks, name + "_pair"), name + "_chip")


def _riding(body, n_in, n_out, n_scratch, ride, grid):
    if ride is None:
        return body, [], [], [], []
    arrays, gather = ride
    n = len(arrays)

    def at(step_of):
        hit = pl.program_id(0) == step_of(grid[0])
        for d in range(1, len(grid)):
            hit = jnp.logical_and(hit, pl.program_id(d) == step_of(grid[d]))
        return hit

    def wrapped(*refs):
        ins, rin = refs[:n_in], refs[n_in:n_in + n]
        outs = refs[n_in + n:n_in + n + n_out]
        rout = refs[n_in + n + n_out:n_in + 2 * n + n_out]
        scratch = refs[n_in + 2 * n + n_out:n_in + 2 * n + n_out + n_scratch]
        sems = refs[n_in + 2 * n + n_out + n_scratch:]

        @pl.when(at(lambda size: 0))
        def _():
            _exchange_start(rin, rout, sems, gather)

        body(*ins, *outs, *scratch)

        @pl.when(at(lambda size: size - 1))
        def _():
            _exchange_wait(rin, rout, sems, gather)

    hbm, out_shape, sems = _exchange_plumbing(arrays, gather)
    return wrapped, hbm, hbm, out_shape, sems


def _sum_contributions(c_ref):
    g = c_ref[0].astype(F32)
    for j in range(1, c_ref.shape[0]):
        g = g + c_ref[j].astype(F32)
    return g


def _adamw_update(g, w_ref, m_ref, v_ref, g_ref, d_ref, nm_ref, nv_ref):
    m_new = ADAM_B1 * m_ref[...] + (1.0 - ADAM_B1) * g
    v_new = ADAM_B2 * v_ref[...] + (1.0 - ADAM_B2) * (g * g)
    m_hat = m_new / (1.0 - ADAM_B1 ** ADAM_STEP)
    v_hat = v_new / (1.0 - ADAM_B2 ** ADAM_STEP)
    g_ref[...] = g
    d_ref[...] = -ADAM_LR * (m_hat / (jnp.sqrt(v_hat) + ADAM_EPS) + ADAM_WD * w_ref[...])
    nm_ref[...] = m_new
    nv_ref[...] = v_new


def _adamw_layers(w, m, v, contrib0, contrib1, name, ride=None):
    _, r, c = w.shape
    br = _pick(r, (256, 128, 64, 32, 16, 8))
    nb = r // br

    def body(w_ref, m_ref, v_ref, c0_ref, c1_ref, g_ref, d_ref, nm_ref, nv_ref):
        g = jnp.where(pl.program_id(0) == 0, _sum_contributions(c0_ref), _sum_contributions(c1_ref))
        _adamw_update(g, w_ref, m_ref, v_ref, g_ref, d_ref, nm_ref, nv_ref)

    spec = pl.BlockSpec((None, br, c), lambda l, i: (l, i, 0))
    body, r_in, r_out, r_shape, r_scratch = _riding(body, 5, 4, 0, ride, (DEPTH, nb))
    return pl.pallas_call(
        body,
        grid=(DEPTH, nb),
        in_specs=[spec, spec, spec,
                  pl.BlockSpec((contrib0.shape[0], br, c), lambda l, i: (0, jnp.where(l == 0, i, nb - 1), 0)),
                  pl.BlockSpec((contrib1.shape[0], br, c), lambda l, i: (0, jnp.where(l == 1, i, 0), 0))] + r_in,
        out_specs=[spec] * 4 + r_out,
        out_shape=[jax.ShapeDtypeStruct(w.shape, F32)] * 4 + r_shape,
        scratch_shapes=r_scratch,
        compiler_params=_cparams(("arbitrary", "arbitrary")),
        name=name,
    )(w, m, v, contrib0, contrib1, *(ride[0] if ride else []))


def _adamw(w, m, v, contrib, name):
    r, c = w.shape
    br = _pick(r, (256, 128, 64, 32, 16, 8))

    def body(w_ref, m_ref, v_ref, c_ref, g_ref, d_ref, nm_ref, nv_ref):
        _adamw_update(_sum_contributions(c_ref), w_ref, m_ref, v_ref, g_ref, d_ref, nm_ref, nv_ref)

    spec = pl.BlockSpec((br, c), lambda i: (i, 0))
    cspec = pl.BlockSpec((contrib.shape[0], br, c), lambda i: (0, i, 0))
    return pl.pallas_call(
        body,
        grid=(r // br,),
        in_specs=[spec, spec, spec, cspec],
        out_specs=[spec] * 4,
        out_shape=[jax.ShapeDtypeStruct((r, c), F32)] * 4,
        compiler_params=_cparams(("parallel",)),
        name=name,
    )(w, m, v, contrib)


def _silu(x):
    return x * jax.nn.sigmoid(x)


def _rope_angles(pos, n_freq):
    inv = ROPE_BASE ** (-jnp.arange(n_freq, dtype=F32) / n_freq)
    return pos[:, None] * inv[None, :]


def _with_ctx_rows(cos, sin):
    return (jnp.concatenate([jnp.ones((CTX_LEN, 128), F32), cos], axis=0),
            jnp.concatenate([jnp.zeros((CTX_LEN, 128), F32), sin], axis=0))


def _rope_tables():
    rows_n = SEQ // GRID_W
    rows = jnp.repeat(jnp.arange(rows_n, dtype=F32), GRID_W)
    cols = jnp.tile(jnp.arange(GRID_W, dtype=F32), rows_n)
    ang_r = _rope_angles(rows, B_HD // 4)
    ang_c = _rope_angles(cols, B_HD // 4)
    cos_b = jnp.tile(jnp.concatenate([jnp.cos(ang_r)] * 2 + [jnp.cos(ang_c)] * 2, axis=1), (1, 2))
    sin_b = jnp.tile(jnp.concatenate([-jnp.sin(ang_r), jnp.sin(ang_r), -jnp.sin(ang_c), jnp.sin(ang_c)], axis=1), (1, 2))
    ang = _rope_angles(jnp.arange(SEQ, dtype=F32), C_HD // 2)
    cos_c = jnp.concatenate([jnp.cos(ang)] * 2, axis=1)
    sin_c = jnp.concatenate([-jnp.sin(ang), jnp.sin(ang)], axis=1)
    return _with_ctx_rows(cos_b, sin_b), _with_ctx_rows(cos_c, sin_c)


def _halves(a):
    return a[:4], a[4:]


def _delta_gates(ab, a_log, dt_bias):
    beta = jax.nn.sigmoid(ab[:, :8])
    g = -jnp.exp(a_log)[None, :] * jax.nn.softplus(ab[:, 8:] + dt_bias[None, :])
    gch = g.reshape(N_CHUNK, CHUNK, 8)
    tri = jnp.tril(jnp.ones((CHUNK, CHUNK), F32))
    fwd = jnp.einsum("ij,cjh->cih", tri, gch[..., :4], precision=HIGHEST)
    bwd = jnp.einsum("ji,cjh->cih", tri, gch[..., 4:], precision=HIGHEST)
    gc = jnp.concatenate([fwd, bwd], axis=-1)
    gl = jnp.sum(gch, axis=1)
    rows = lambda a: _halves(a.transpose(2, 0, 1)[:, :, None, :])
    return rows(beta.reshape(N_CHUNK, CHUNK, 8)), rows(gc), _halves(gl.T[:, :, None, None])


def _ret_consts(c_decay):
    lg = jax.nn.log_sigmoid(c_decay)
    idx = jnp.arange(RET_CHUNK, dtype=F32)
    diff = idx[:, None] - idx[None, :]
    lgf, lgb = lg[:4, None, None], lg[4:, None, None]
    dm = jnp.concatenate([jnp.exp(jnp.where(diff >= 0, diff * lgf, -jnp.inf)),
                          jnp.exp(jnp.where(diff <= 0, -diff * lgb, -jnp.inf))], axis=0)
    qs = jnp.concatenate([jnp.exp((idx + 1.0)[None, :] * lg[:4, None]),
                          jnp.exp((RET_CHUNK - idx)[None, :] * lg[4:, None])], axis=0)[:, :, None]
    ks = jnp.concatenate([jnp.exp((RET_CHUNK - 1.0 - idx)[None, :] * lg[:4, None]),
                          jnp.exp(idx[None, :] * lg[4:, None])], axis=0)[:, :, None]
    return dm, qs, ks, jnp.exp(RET_CHUNK * lg)[:, None, None]


A_PIECES = ((0, True, A_DK ** -0.5, "a_q"), (1, True, 1.0, "a_k"), (2, False, 1.0, "a_v"))
B_ROPE_COLS = [C_BQ // 512, C_BKV // 256, 0, 0]
C_ROPE_COLS = [C_CQ // 512, C_CK // 512, 0, 0]


def _conv8(conv_w):
    return jnp.pad(conv_w, ((0, 8 - A_CONV), (0, 0)))


W_IN_TILES = {"nn": (2176, 512, 1024), "nt": (1088, 1024, 4352), "db": (1024, 512, ROWS)}


def _core_forward(h, w16, p, rides):
    res = _matmul(h, w16, "w_in", "nn", W_IN_TILES["nn"], ride=rides.get("w_in"))
    proj, rode = (res[0], {"w_in": res[1:]}) if "w_in" in rides else (res, {})
    wb = p["w_branch"] if "w_in" not in rides else _unshard_layer("w_branch", rode["w_in"][0])
    (cos_b, sin_b), (cos_c, sin_c) = _rope_tables()
    conv8 = _conv8(p["a_conv_w"])
    q, k, v = [_a_prep_fwd(proj, conv8, col, nrm, scl, nm) for col, nrm, scl, nm in A_PIECES]
    gates = _delta_gates(proj[:, C_AB:C_AB + 16], p["a_log"], p["a_dt_bias"])
    res = _delta_fwd_call(q, k, v, *gates, ride=rides.get("delta"))
    (of, orv, ssf, ssr, tsf, tsr), rode["delta"] = res[:6], res[6:]

    qb, kvb = _b_rope.fwd([proj, proj, cos_b, sin_b], [], B_ROPE_COLS)
    res = _attn_fwd_call(qb, kvb, p["b_sink"], ride=rides.get("attn"))
    ob, rode["attn"] = res[0], res[1:]

    qc, kc = _c_rope.fwd([proj, proj, cos_c, sin_c], [], C_ROPE_COLS)
    res = _ret_fwd_call(qc, kc, proj, C_CV // 512, *_ret_consts(p["c_decay"]), ride=rides.get("ret"))
    (cf, cr, csf, csr), rode["ret"] = res[:4], res[4:]

    (merged,) = _mix_merge.fwd([(of, orv), proj, ob, proj, (cf, cr), proj, proj, proj, proj],
                               [p["a_norm_w"][None, :], p["c_norm_w"][None, :], wb[0], wb[1], wb[2]], MIX_MERGE_COLS)
    saved = dict(proj=proj, q=q, k=k, v=v, of=of, orv=orv, ss=(ssf, ssr), ts=(tsf, tsr), qb=qb, kvb=kvb, ob=ob,
                 qc=qc, kc=kc, cf=cf, cr=cr, cs=(csf, csr), wb=wb)
    return merged, saved, rode


def _core_backward(h, w16, p, s, dmerged, rides, branch_rides_in_attn=False):
    proj, wb = s["proj"], s["wb"]
    (cos_b, sin_b), (cos_c, sin_c) = _rope_tables()
    conv8 = _conv8(p["a_conv_w"])
    rode = {}

    (do_a, daz, dob, dbz, do_c, dcz, dma, dmb, dmc), (danw, dcnw, *dwb) = _mix_merge.bwd(
        [(s["of"], s["orv"]), proj, s["ob"], proj, (s["cf"], s["cr"]), proj, proj, proj, proj],
        [p["a_norm_w"][None, :], p["c_norm_w"][None, :], wb[0], wb[1], wb[2]], [dmerged], MIX_MERGE_COLS,
        bf16_rows=(1, 3, 5, 6, 7, 8))
    dwb = jnp.stack(dwb)

    consts, consts_vjp = jax.vjp(_ret_consts, p["c_decay"])
    g = _ret_bwd_call(s["qc"], s["kc"], proj, C_CV // 512, *consts, s["cs"], do_c, ride=rides.get("ret"))
    rode["ret"] = g[10:]
    (dcq, dck), _ = _c_rope.bwd([proj, proj, cos_c, sin_c], [], [(g[0], g[1]), (g[2], g[3])], C_ROPE_COLS,
                                bf16_rows=(0, 1))
    dcv = (g[4] + g[5]).astype(BF16)
    (dc_decay,) = consts_vjp(tuple(g[6:10]))

    attn_ride = rides.get("attn")
    if branch_rides_in_attn:
        attn_ride = (list(attn_ride[0]) + [_reshard_layer("w_branch", dwb).astype(BF16)], attn_ride[1])
    res = _attn_bwd_call(s["qb"], s["kvb"], p["b_sink"], dob, ride=attn_ride)
    (dqb, dkvb, dsink), rode["attn"] = res[:3], res[3:]
    (dbq, dbkv), _ = _b_rope.bwd([proj, proj, cos_b, sin_b], [], [dqb, dkvb], B_ROPE_COLS, bf16_rows=(0, 1))

    ab = proj[:, C_AB:C_AB + 16]
    gates, gates_vjp = jax.vjp(_delta_gates, ab, p["a_log"], p["a_dt_bias"])
    g = _delta_bwd_call(s["q"], s["k"], s["v"], *gates, s["ss"], s["ts"], do_a, ride=rides.get("delta"))
    rode["delta"] = g[12:]
    dgates = ((g[6], g[7]), (g[8], g[9]), (g[10], g[11]))
    dab, da_log, ddt = gates_vjp(dgates)
    dpre, dconv = [], []
    for (col, nrm, scl, nm), df, dr in zip(A_PIECES, (g[0], g[2], g[4]), (g[1], g[3], g[5])):
        dx, dw = _a_prep_bwd(proj, conv8, col, nrm, scl, df, dr, nm)
        dpre.append(dx)
        dconv.append(dw[:A_CONV])

    dproj = jnp.concatenate(dpre + [daz, dbq, dbz, dcq, dck, dcv, dcz, dma, dmb, dmc, dbkv,
                                    jnp.pad(dab, ((0, 0), (0, IN_PAD - C_AB - 16))).astype(BF16)], axis=1)
    dw = _matmul(h.T.astype(BF16), dproj, "w_in_db", "nn", W_IN_TILES["db"])
    if branch_rides_in_attn:
        pair = _pair_sums([_grad_blocks("w_in", dw)], "scatter_w_in_pair")
        res = _matmul(dproj, w16, "w_in_da", "nt", W_IN_TILES["nt"], ride=(pair, CHIP))
        dh, rode["w_in_da"] = res[0], res[1:]
    else:
        dh = _matmul(dproj, w16, "w_in_da", "nt", W_IN_TILES["nt"])
    dp = dict(a_conv_w=jnp.concatenate(dconv, axis=1), a_log=da_log, a_dt_bias=ddt, a_norm_w=danw[0],
              b_sink=dsink[:, 0], c_decay=dc_decay, c_norm_w=dcnw[0], w_branch=dwb)
    return dh, dw, dp, rode


CORE_PARAMS = ("a_conv_w", "a_log", "a_dt_bias", "a_norm_w", "b_sink", "c_decay", "c_norm_w", "w_branch")


W_IN_SHARD = IN_WIDTH // N_DEV
W_IN_RUNS = ((0, 2048, 0), (2064, 512, C_BQ), (2832, 512, C_BZ), (3344, 5120, C_CQ), (2576, 256, C_BKV),
             (2048, 16, C_AB))


def _shard_overlap(start, width, j):
    lo, hi = max(start, j * W_IN_SHARD), min(start + width, (j + 1) * W_IN_SHARD)
    return (lo, hi) if lo < hi else None


def _w_in_from_shards(g):
    parts = []
    for start, width, _ in W_IN_RUNS:
        for j in range(N_DEV):
            span = _shard_overlap(start, width, j)
            if span:
                parts.append(g[j, :, span[0] - j * W_IN_SHARD:span[1] - j * W_IN_SHARD])
    parts.append(jnp.zeros((D_MODEL, IN_PAD - IN_WIDTH), g.dtype))
    return jnp.concatenate(parts, axis=1)


def _w_in_blocks(dw):
    blocks = []
    for j in range(N_DEV):
        parts = []
        for start, width, pad in sorted(W_IN_RUNS):
            span = _shard_overlap(start, width, j)
            if span:
                parts.append(dw[:, pad + span[0] - start:pad + span[1] - start])
        blocks.append(jnp.concatenate(parts, axis=1))
    return jnp.stack(blocks)


LAYER_SHARDED = ("w_ada", "w_in", "w_branch", "w_out")


def _unshard_layer(name, g):
    if name == "w_branch":
        return g.transpose(1, 2, 0, 3).reshape(3, BR_WIDTH, D_MODEL)
    if name == "w_out":
        return g.reshape(D_MODEL, D_MODEL)
    return g.transpose(1, 0, 2).reshape(D_MODEL, -1)


def _reshard_layer(name, w):
    if name == "w_branch":
        return w.reshape(3, BR_WIDTH, N_DEV, D_MODEL // N_DEV).transpose(2, 0, 1, 3)
    if name == "w_out":
        return w.reshape(N_DEV, D_MODEL // N_DEV, D_MODEL)
    return w.reshape(D_MODEL, N_DEV, -1).transpose(1, 0, 2)


def _layer_weights(gathered):
    out = {n: _unshard_layer(n, g) for n, g in gathered.items() if n != "w_in"}
    out["w_in16"] = _w_in_from_shards(gathered["w_in"])
    return out


def _grad_blocks(name, g):
    return (_w_in_blocks(g) if name == "w_in" else _reshard_layer(name, g)).astype(BF16)


def _forward_backward(small, layer0, shards0, shards1, x, c, ctx, loss_target):
    c_ctx = small["c_ctx"]
    sc16 = jnp.zeros((16, D_MODEL), F32).at[0].set(_silu(c)).at[1].set(_silu(c_ctx))
    xs = jnp.concatenate([ctx, x], axis=0)
    weights = [dict(layer0), None]
    layers = []
    for l in range(DEPTH):
        wl = weights[l]
        mod16 = _matmul(sc16, wl["w_ada"], "ada") + small["b_ada"][l][None, :]
        mod_cx = jnp.stack([mod16[1], mod16[0]])
        shift, scale, gate = jnp.split(mod_cx, 3, axis=1)
        nw = small["norm_w"][l][None, :]
        (h,) = _norm_mod.fwd([xs], [nw, shift, scale])
        p = {n: small[n][l] for n in CORE_PARAMS if n != "w_branch"}
        p["w_branch"] = wl.get("w_branch")
        rides = {}
        if l == 0:
            rides = {"w_in": ([shards0["w_branch"], shards0["w_out"]], True), "delta": ([shards1["w_in"]], True),
                     "attn": ([shards1["w_ada"]], True), "ret": ([shards1["w_branch"], shards1["w_out"]], True)}
        merged, saved, rode = _core_forward(h, wl["w_in16"], p, rides)
        if l == 0:
            wl["w_out"] = _unshard_layer("w_out", rode["w_in"][1])
            weights[1] = _layer_weights(dict(w_in=rode["delta"][0], w_ada=rode["attn"][0],
                                             w_branch=rode["ret"][0], w_out=rode["ret"][1]))
        (xs_next,) = _out_residual.fwd([xs, merged], [wl["w_out"], gate])
        layers.append(dict(xs=xs, h=h, p=p, saved=saved, merged=merged, gate=gate, nw=nw, shift=shift, scale=scale))
        xs = xs_next
    fw = small["final_norm_w"][None, :]
    xs = xs[CTX_LEN:]
    (per_row,) = _loss_rows.fwd([xs, loss_target], [fw])
    loss = jnp.sum(per_row[:, 0])

    d_per_row = jnp.zeros((SEQ, 128), F32).at[:, 0].set(1.0)
    (dxs,), (dfw,) = _loss_rows.bwd([xs, loss_target], [fw], [d_per_row])
    dxs = jnp.pad(dxs, ((CTX_LEN, 0), (0, 0)))
    small_names = tuple(n for n in CORE_PARAMS if n != "w_branch") + ("b_ada", "norm_w")
    dsmall = {n: [None] * DEPTH for n in small_names}
    dlayer = [None] * DEPTH
    contrib0 = contrib1 = None
    dsc16 = jnp.zeros((16, D_MODEL), F32)
    for l in reversed(range(DEPTH)):
        s, wl = layers[l], weights[l]
        (dres, dmerged), (dw_out, dgate) = _out_residual.bwd([s["xs"], s["merged"]], [wl["w_out"], s["gate"]], [dxs])
        rides = {}
        if l == 0:
            blocks1 = {n: _grad_blocks(n, g) for n, g in dlayer[1].items()}
            rides = {"ret": ([blocks1["w_branch"], blocks1["w_out"]], False),
                     "attn": ([_reshard_layer("w_out", dw_out).astype(BF16), blocks1["w_ada"]], False),
                     "delta": ([blocks1["w_in"]], False)}
        dh, dw_in, dp, rode = _core_backward(s["h"], wl["w_in16"], s["p"], s["saved"], dmerged, rides,
                                             branch_rides_in_attn=(l == 0))
        if l == 0:
            contrib1 = dict(w_in=rode["delta"][0], w_ada=rode["attn"][1], w_branch=rode["ret"][0],
                            w_out=rode["ret"][1])
            contrib0 = dict(w_out=rode["attn"][0], w_branch=rode["attn"][2], w_in=rode["w_in_da"][0])
        (dxn,), (dnw, dshift, dscale) = _norm_mod.bwd([s["xs"]], [s["nw"], s["shift"], s["scale"]], [dh])
        dxs = dres + dxn
        dmod_cx = jnp.concatenate([dshift, dscale, dgate], axis=1)
        dmod16 = jnp.zeros((16, 3 * D_MODEL), F32).at[0].set(dmod_cx[1]).at[1].set(dmod_cx[0])
        dsc16 = dsc16 + _matmul(dmod16, wl["w_ada"], "ada_da", "nt")
        dlayer[l] = dict(w_ada=_matmul(sc16, dmod16, "ada_db", "tn"), w_in=dw_in,
                         w_branch=dp["w_branch"], w_out=dw_out)
        for n in small_names:
            if n in dp:
                dsmall[n][l] = dp[n]
        dsmall["norm_w"][l] = dnw[0]
        dsmall["b_ada"][l] = dmod_cx[0] + dmod_cx[1]
    gsmall = {n: jnp.stack(v) for n, v in dsmall.items()}
    gsmall["final_norm_w"] = dfw[0]
    sig = jax.nn.sigmoid(c_ctx)
    gsmall["c_ctx"] = dsc16[1] * sig * (1.0 + c_ctx * (1.0 - sig))
    return loss, dxs[CTX_LEN:], gsmall, {"w_ada": dlayer[0]["w_ada"]}, contrib0, contrib1


SMALL = ("c_ctx", "b_ada", "norm_w", "a_log", "a_dt_bias", "a_norm_w", "b_sink", "c_decay", "c_norm_w",
         "final_norm_w")
WEIGHTS = ("c_ctx", "w_ada", "b_ada", "norm_w", "w_in", "a_conv_w", "a_log", "a_dt_bias", "a_norm_w", "b_sink",
           "c_decay", "c_norm_w", "w_branch", "w_out", "final_norm_w")
SMALL_PACK = 12288


def _unshard_conv(g):
    return g.transpose(1, 2, 0, 3).reshape(DEPTH, A_CONV, 3 * A_WIDTH)


def _reshard_conv(w):
    return w.reshape(DEPTH, A_CONV, N_DEV, 3 * A_WIDTH // N_DEV).transpose(2, 0, 1, 3)


def _pack_small(tree):
    flat = jnp.concatenate([tree[n].reshape(-1) for n in SMALL])
    return jnp.pad(flat, (0, SMALL_PACK - flat.shape[0])).reshape(SMALL_PACK // 128, 128)


def _unpack_small(packed, like):
    flat = packed.reshape(-1)
    out, off = {}, 0
    for n in SMALL:
        size = math.prod(like[n].shape)
        out[n] = flat[off:off + size].reshape(like[n].shape)
        off += size
    return out


def kernel(x, c, ctx, c_ctx, w_ada, b_ada, norm_w, w_in, a_conv_w, a_log, a_dt_bias, a_norm_w, b_sink, c_decay, c_norm_w, w_branch, w_out, final_norm_w, loss_target, m_c_ctx, m_w_ada, m_b_ada, m_norm_w, m_w_in, m_a_conv_w, m_a_log, m_a_dt_bias, m_a_norm_w, m_b_sink, m_c_decay, m_c_norm_w, m_w_branch, m_w_out, m_final_norm_w, v_c_ctx, v_w_ada, v_b_ada, v_norm_w, v_w_in, v_a_conv_w, v_a_log, v_a_dt_bias, v_a_norm_w, v_b_sink, v_c_decay, v_c_norm_w, v_w_branch, v_w_out, v_final_norm_w):
    w = dict(c_ctx=c_ctx, w_ada=w_ada, b_ada=b_ada, norm_w=norm_w, w_in=w_in, a_conv_w=a_conv_w, a_log=a_log,
             a_dt_bias=a_dt_bias, a_norm_w=a_norm_w, b_sink=b_sink, c_decay=c_decay, c_norm_w=c_norm_w,
             w_branch=w_branch, w_out=w_out, final_norm_w=final_norm_w)
    m = dict(c_ctx=m_c_ctx, w_ada=m_w_ada, b_ada=m_b_ada, norm_w=m_norm_w, w_in=m_w_in, a_conv_w=m_a_conv_w,
             a_log=m_a_log, a_dt_bias=m_a_dt_bias, a_norm_w=m_a_norm_w, b_sink=m_b_sink, c_decay=m_c_decay,
             c_norm_w=m_c_norm_w, w_branch=m_w_branch, w_out=m_w_out, final_norm_w=m_final_norm_w)
    v = dict(c_ctx=v_c_ctx, w_ada=v_w_ada, b_ada=v_b_ada, norm_w=v_norm_w, w_in=v_w_in, a_conv_w=v_a_conv_w,
             a_log=v_a_log, a_dt_bias=v_a_dt_bias, a_norm_w=v_a_norm_w, b_sink=v_b_sink, c_decay=v_c_decay,
             c_norm_w=v_c_norm_w, w_branch=v_w_branch, w_out=v_w_out, final_norm_w=v_final_norm_w)

    shards = {n: w[n].astype(BF16) for n in LAYER_SHARDED}
    first = _exchange([shards["w_ada"][0], shards["w_in"][0], w["a_conv_w"]], True, "gather_layer0")
    layer0 = _layer_weights(dict(w_ada=first[0], w_in=first[1]))
    small_w = {n: w[n] for n in SMALL}
    small_w["a_conv_w"] = _unshard_conv(first[2])
    loss, gx, gw, glayer0, contrib0, contrib1 = _forward_backward(
        small_w, layer0, {n: shards[n][0] for n in ("w_branch", "w_out")}, {n: shards[n][1] for n in LAYER_SHARDED},
        x[0], c[0], ctx[0], loss_target[0])
    loss = lax.psum(loss, ("x", "y", "c"))

    pairs = _pair_sums([_reshard_layer("w_ada", glayer0["w_ada"]).astype(BF16), _reshard_conv(gw["a_conv_w"])],
                       "scatter_layer0_pair")
    small = _exchange([_pack_small(gw)], True, "gather_small_grads")[0]

    grad, delta, new_m, new_v = {}, {}, {}, {}
    last = None
    for n in ("w_in", "w_branch", "w_out", "w_ada"):
        shp = w[n].shape
        per_layer = (math.prod(shp[1:-1]), shp[-1])
        outs = _adamw_layers(*[a.reshape((DEPTH,) + per_layer) for a in (w[n], m[n], v[n])],
                             *[cb.reshape(cb.shape[:1] + per_layer) for cb in (contrib0[n], contrib1[n])], "adamw_" + n,
                             ride=(pairs, CHIP) if n == "w_in" else None)
        if n == "w_in":
            outs, last = outs[:4], outs[4:]
            contrib0["w_ada"] = last[0]
        grad[n], delta[n], new_m[n], new_v[n] = [o.reshape(shp) for o in outs]
    shp = a_conv_w.shape
    two_d = (math.prod(shp[:-1]), shp[-1])
    outs = _adamw(*[a.reshape(two_d) for a in (a_conv_w, m_a_conv_w, v_a_conv_w)],
                  last[1].reshape(last[1].shape[:1] + two_d), "adamw_a_conv_w")
    grad["a_conv_w"], delta["a_conv_w"], new_m["a_conv_w"], new_v["a_conv_w"] = [o.reshape(shp) for o in outs]
    outs = _adamw(_pack_small(w), _pack_small(m), _pack_small(v), small, "adamw_small")
    for tree, packed in zip((grad, delta, new_m, new_v), outs):
        tree.update(_unpack_small(packed, w))

    return (loss, gx[None], *[grad[n] for n in WEIGHTS], *[delta[n] for n in WEIGHTS],
            *[new_m[n] for n in WEIGHTS], *[new_v[n] for n in WEIGHTS])
```
